```python
import math
import jax, jax.numpy as jnp
from jax import lax
import numpy as np

D_MODEL = 2048
BATCH = 8
SEQ = 4096
DEPTH = 4

CHUNK = 64

D_MIX = 2 * D_MODEL
POOL_W = D_MIX // 2
SSD_W = D_MIX - POOL_W

POOL_WINDOWS = (2, 4, 8, 16)
POOL_GROUPS = len(POOL_WINDOWS)
POOL_GROUP_W = POOL_W // POOL_GROUPS
MAX_WINDOW = max(POOL_WINDOWS)

SSD_HEAD_DIM = 64
SSD_HEADS = SSD_W // SSD_HEAD_DIM
SSD_STATE = 128
SSD_GROUPS = 4
SSD_HEADS_PER_GROUP = SSD_HEADS // SSD_GROUPS
CONV_WIDTH = 4
CONV_DIM = SSD_W + 2 * SSD_GROUPS * SSD_STATE

D_IN_PROJ = 2 * POOL_W + 2 * SSD_W + 2 * SSD_GROUPS * SSD_STATE + SSD_HEADS

NORM_EPS = 1e-6

kernel_name = "hybrid_pool_ssd_sandwich_trunk"


def rms_norm(x, w):
    xf = x.astype(jnp.float32)
    var = jnp.mean(xf * xf, axis=-1, keepdims=True)
    return (xf * lax.rsqrt(var + NORM_EPS) * w.astype(jnp.float32)).astype(x.dtype)


def pool_mixer(u, mix_w, scale):
    b, l, _ = u.shape
    uf = u.astype(jnp.float32)
    cs = jnp.cumsum(uf, axis=1)
    csp = jnp.pad(cs, ((0, 0), (MAX_WINDOW, 0), (0, 0)))
    t = jnp.arange(l)
    outs = []
    for g, w in enumerate(POOL_WINDOWS):
        sl = slice(g * POOL_GROUP_W, (g + 1) * POOL_GROUP_W)
        win_sum = csp[:, MAX_WINDOW:, sl] - csp[:, MAX_WINDOW - w:MAX_WINDOW - w + l, sl]
        cnt = jnp.minimum(t + 1, w).astype(jnp.float32)[None, :, None]
        outs.append(win_sum / cnt - uf[:, :, sl])
    pooled = jnp.stack(outs, axis=2).astype(u.dtype)
    mixed = jnp.einsum('blgc,gcd->blgd', pooled, mix_w)
    return mixed.reshape(b, l, POOL_W) * scale


def causal_dwconv(u, w, bias):
    y = lax.conv_general_dilated(
        u, w[:, None, :], window_strides=(1,), padding=[(CONV_WIDTH - 1, 0)],
        dimension_numbers=('NWC', 'WIO', 'NWC'), feature_group_count=u.shape[-1])
    return y + bias


def ssd_scan(xh, dt, a, bm, cm):
    b, l, h, p = xh.shape
    c = l // CHUNK
    G, R, N, Q = SSD_GROUPS, SSD_HEADS_PER_GROUP, SSD_STATE, CHUNK
    xdt = (xh * dt[..., None]).reshape(b, c, Q, G, R, p)
    adt = (dt * a).reshape(b, c, Q, G, R).transpose(0, 3, 4, 1, 2)
    acs = jnp.cumsum(adt, axis=-1)
    bm = bm.reshape(b, c, Q, G, N)
    cm = cm.reshape(b, c, Q, G, N)
    mask = jnp.tril(jnp.ones((Q, Q), dtype=bool))
    seg = acs[..., :, None] - acs[..., None, :]
    decay = jnp.exp(jnp.where(mask, seg, -jnp.inf))
    scores = jnp.einsum('bclgn,bcsgn->bgcls', cm, bm)
    y_diag = jnp.einsum('bgcls,bgrcls,bcsgrp->bclgrp', scores, decay, xdt)
    decay_states = jnp.exp(acs[..., -1:] - acs)
    states = jnp.einsum('bcsgn,bgrcs,bcsgrp->bcgrpn', bm, decay_states, xdt)
    chunk_decay = jnp.exp(acs[..., -1])

    def step(carry, inp):
        st, dec = inp
        return carry * dec[..., None, None] + st, carry

    init = jnp.zeros((b, G, R, p, N), dtype=jnp.float32)
    _, prev = lax.scan(step, init, (jnp.moveaxis(states, 1, 0), jnp.moveaxis(chunk_decay, 3, 0)))
    prev = jnp.moveaxis(prev, 0, 1)
    y_off = jnp.einsum('bclgn,bcgrpn,bgrcl->bclgrp', cm, prev, jnp.exp(acs))
    return (y_diag + y_off).reshape(b, l, h, p)


def ssd_branch(z, xbc_raw, dt_raw, conv_w, conv_b, dt_bias, a_log, d_skip, norm_w):
    b, l, _ = z.shape
    xbc = jax.nn.silu(causal_dwconv(xbc_raw, conv_w, conv_b))
    xs = xbc[..., :SSD_W]
    bm = xbc[..., SSD_W:SSD_W + SSD_GROUPS * SSD_STATE].reshape(b, l, SSD_GROUPS, SSD_STATE)
    cm = xbc[..., SSD_W + SSD_GROUPS * SSD_STATE:].reshape(b, l, SSD_GROUPS, SSD_STATE)
    dt = jax.nn.softplus(dt_raw.astype(jnp.float32) + dt_bias.astype(jnp.float32))
    a = -jnp.exp(a_log.astype(jnp.float32))
    xh = xs.reshape(b, l, SSD_HEADS, SSD_HEAD_DIM).astype(jnp.float32)
    y = ssd_scan(xh, dt, a, bm.astype(jnp.float32), cm.astype(jnp.float32))
    y = y + d_skip.astype(jnp.float32)[:, None] * xh
    y = y.reshape(b, l, SSD_W) * jax.nn.silu(z.astype(jnp.float32))
    yg = y.reshape(b, l, SSD_GROUPS, SSD_W // SSD_GROUPS)
    yg = yg * lax.rsqrt(jnp.mean(yg * yg, axis=-1, keepdims=True) + NORM_EPS)
    y = yg.reshape(b, l, SSD_W) * norm_w.astype(jnp.float32)
    return y.astype(z.dtype)


def _fwd_setup_inputs(seed: int = 0) -> dict:
    key = jax.random.key(seed)
    ks = jax.random.split(key, 16)
    f32 = jnp.float32
    x = jax.random.normal(ks[0], (BATCH, SEQ, D_MODEL), f32)
    pre_norm_w = 1.0 + 0.02 * jax.random.normal(ks[1], (DEPTH, D_MODEL), f32)
    w_in = jax.random.normal(ks[2], (DEPTH, D_MODEL, D_IN_PROJ), f32) * D_MODEL ** -0.5
    pool_mix_w = jax.random.normal(ks[3], (DEPTH, POOL_GROUPS, POOL_GROUP_W, POOL_GROUP_W), f32) * POOL_GROUP_W ** -0.5
    pool_scale = 1.0 + 0.02 * jax.random.normal(ks[4], (DEPTH, POOL_W), f32)
    conv_w = jax.random.normal(ks[5], (DEPTH, CONV_WIDTH, CONV_DIM), f32) * CONV_WIDTH ** -0.5
    conv_b = 0.02 * jax.random.normal(ks[6], (DEPTH, CONV_DIM), f32)
    u = jax.random.uniform(ks[7], (DEPTH, SSD_HEADS), f32)
    dt0 = jnp.exp(u * (math.log(0.1) - math.log(0.001)) + math.log(0.001))
    dt_bias = dt0 + jnp.log(-jnp.expm1(-dt0))
    a_log = jnp.log(jax.random.uniform(ks[8], (DEPTH, SSD_HEADS), f32, minval=1.0, maxval=16.0))
    d_skip = 1.0 + 0.02 * jax.random.normal(ks[9], (DEPTH, SSD_HEADS), f32)
    ssd_norm_w = 1.0 + 0.02 * jax.random.normal(ks[10], (DEPTH, SSD_W), f32)
    w_out = jax.random.normal(ks[11], (DEPTH, D_MIX, D_MODEL), f32) * D_MIX ** -0.5
    post_norm_w = 1.0 + 0.02 * jax.random.normal(ks[12], (DEPTH, D_MODEL), f32)
    return {"x": x, "pre_norm_w": pre_norm_w, "w_in": w_in, "pool_mix_w": pool_mix_w,
            "pool_scale": pool_scale, "conv_w": conv_w, "conv_b": conv_b,
            "dt_bias": dt_bias, "a_log": a_log, "d_skip": d_skip,
            "ssd_norm_w": ssd_norm_w, "w_out": w_out, "post_norm_w": post_norm_w}


def _fwd_reference(x, pre_norm_w, w_in, pool_mix_w, pool_scale, conv_w, conv_b,
              dt_bias, a_log, d_skip, ssd_norm_w, w_out, post_norm_w):
    o1 = POOL_W
    o2 = o1 + POOL_W
    o3 = o2 + SSD_W
    o4 = o3 + CONV_DIM
    for layer in range(DEPTH):
        h = rms_norm(x, pre_norm_w[layer])
        proj = jnp.einsum('bld,de->ble', h, w_in[layer])
        pool_u = proj[..., :o1]
        pool_gate = proj[..., o1:o2]
        ssd_z = proj[..., o2:o3]
        ssd_xbc = proj[..., o3:o4]
        ssd_dt = proj[..., o4:]
        y_pool = pool_mixer(pool_u, pool_mix_w[layer], pool_scale[layer]) * jax.nn.silu(pool_gate)
        y_ssd = ssd_branch(ssd_z, ssd_xbc, ssd_dt, conv_w[layer], conv_b[layer],
                           dt_bias[layer], a_log[layer], d_skip[layer], ssd_norm_w[layer])
        mixed = jnp.concatenate([y_pool.astype(x.dtype), y_ssd.astype(x.dtype)], axis=-1)
        out = jnp.einsum('ble,ed->bld', mixed, w_out[layer])
        x = x + rms_norm(out, post_norm_w[layer])
    return x


import jax as _jax
import jax.numpy as _jnp

TWIN_FORMAT = 'train_step'
FWD_PARAMS = ['x', 'pre_norm_w', 'w_in', 'pool_mix_w', 'pool_scale', 'conv_w', 'conv_b', 'dt_bias', 'a_log', 'd_skip', 'ssd_norm_w', 'w_out', 'post_norm_w']
TWIN_WEIGHTS = ['pre_norm_w', 'w_in', 'pool_mix_w', 'pool_scale', 'conv_w', 'conv_b', 'dt_bias', 'a_log', 'd_skip', 'ssd_norm_w', 'w_out', 'post_norm_w']
TWIN_DIFF_INPUT = 'x'
TWIN_INPUTS = ['x', 'pre_norm_w', 'w_in', 'pool_mix_w', 'pool_scale', 'conv_w', 'conv_b', 'dt_bias', 'a_log', 'd_skip', 'ssd_norm_w', 'w_out', 'post_norm_w', 'loss_target', 'm_pre_norm_w', 'm_w_in', 'm_pool_mix_w', 'm_pool_scale', 'm_conv_w', 'm_conv_b', 'm_dt_bias', 'm_a_log', 'm_d_skip', 'm_ssd_norm_w', 'm_w_out', 'm_post_norm_w', 'v_pre_norm_w', 'v_w_in', 'v_pool_mix_w', 'v_pool_scale', 'v_conv_w', 'v_conv_b', 'v_dt_bias', 'v_a_log', 'v_d_skip', 'v_ssd_norm_w', 'v_w_out', 'v_post_norm_w']
TWIN_OUTPUTS = ['loss', 'grad_x', 'grad_pre_norm_w', 'grad_w_in', 'grad_pool_mix_w', 'grad_pool_scale', 'grad_conv_w', 'grad_conv_b', 'grad_dt_bias', 'grad_a_log', 'grad_d_skip', 'grad_ssd_norm_w', 'grad_w_out', 'grad_post_norm_w', 'delta_pre_norm_w', 'delta_w_in', 'delta_pool_mix_w', 'delta_pool_scale', 'delta_conv_w', 'delta_conv_b', 'delta_dt_bias', 'delta_a_log', 'delta_d_skip', 'delta_ssd_norm_w', 'delta_w_out', 'delta_post_norm_w', 'new_m_pre_norm_w', 'new_m_w_in', 'new_m_pool_mix_w', 'new_m_pool_scale', 'new_m_conv_w', 'new_m_conv_b', 'new_m_dt_bias', 'new_m_a_log', 'new_m_d_skip', 'new_m_ssd_norm_w', 'new_m_w_out', 'new_m_post_norm_w', 'new_v_pre_norm_w', 'new_v_w_in', 'new_v_pool_mix_w', 'new_v_pool_scale', 'new_v_conv_w', 'new_v_conv_b', 'new_v_dt_bias', 'new_v_a_log', 'new_v_d_skip', 'new_v_ssd_norm_w', 'new_v_w_out', 'new_v_post_norm_w']
TWIN_LEAF_KINDS = {'loss': 'loss', 'grad_x': 'grad_x', 'grad_pre_norm_w': 'grad_w', 'grad_w_in': 'grad_w', 'grad_pool_mix_w': 'grad_w', 'grad_pool_scale': 'grad_w', 'grad_conv_w': 'grad_w', 'grad_conv_b': 'grad_w', 'grad_dt_bias': 'grad_w', 'grad_a_log': 'grad_w', 'grad_d_skip': 'grad_w', 'grad_ssd_norm_w': 'grad_w', 'grad_w_out': 'grad_w', 'grad_post_norm_w': 'grad_w', 'delta_pre_norm_w': 'delta_w', 'delta_w_in': 'delta_w', 'delta_pool_mix_w': 'delta_w', 'delta_pool_scale': 'delta_w', 'delta_conv_w': 'delta_w', 'delta_conv_b': 'delta_w', 'delta_dt_bias': 'delta_w', 'delta_a_log': 'delta_w', 'delta_d_skip': 'delta_w', 'delta_ssd_norm_w': 'delta_w', 'delta_w_out': 'delta_w', 'delta_post_norm_w': 'delta_w', 'new_m_pre_norm_w': 'new_m', 'new_m_w_in': 'new_m', 'new_m_pool_mix_w': 'new_m', 'new_m_pool_scale': 'new_m', 'new_m_conv_w': 'new_m', 'new_m_conv_b': 'new_m', 'new_m_dt_bias': 'new_m', 'new_m_a_log': 'new_m', 'new_m_d_skip': 'new_m', 'new_m_ssd_norm_w': 'new_m', 'new_m_w_out': 'new_m', 'new_m_post_norm_w': 'new_m', 'new_v_pre_norm_w': 'new_v', 'new_v_w_in': 'new_v', 'new_v_pool_mix_w': 'new_v', 'new_v_pool_scale': 'new_v', 'new_v_conv_w': 'new_v', 'new_v_conv_b': 'new_v', 'new_v_dt_bias': 'new_v', 'new_v_a_log': 'new_v', 'new_v_d_skip': 'new_v', 'new_v_ssd_norm_w': 'new_v', 'new_v_w_out': 'new_v', 'new_v_post_norm_w': 'new_v'}


def _forward(args):
    return _fwd_reference(*[args[k] for k in FWD_PARAMS])


def _output_shape():
    def fwd():
        inp = _fwd_setup_inputs(0)
        return _fwd_reference(*[inp[k] for k in FWD_PARAMS])
    out = _jax.eval_shape(fwd)
    return out.shape, out.dtype

N_MICROBATCH = 1
ADAM_LR = 0.001
ADAM_B1 = 0.9
ADAM_B2 = 0.999
ADAM_EPS = 1e-08
ADAM_WD = 0.01
ADAM_STEP = 10
PER_EXAMPLE_BATCH_AXIS = {'x': 0, 'loss_target': 0}
SHARED_INPUTS = []
_WEIGHT_DTYPES = {'pre_norm_w': _jnp.float32, 'w_in': _jnp.float32, 'pool_mix_w': _jnp.float32, 'pool_scale': _jnp.float32, 'conv_w': _jnp.float32, 'conv_b': _jnp.float32, 'dt_bias': _jnp.float32, 'a_log': _jnp.float32, 'd_skip': _jnp.float32, 'ssd_norm_w': _jnp.float32, 'w_out': _jnp.float32, 'post_norm_w': _jnp.float32}
MOMENT_SCALE = {'pre_norm_w': 5.943750e-01, 'w_in': 2.791689e-01, 'pool_mix_w': 1.680996e-01, 'pool_scale': 1.705116e-01, 'conv_w': 5.528040e-01, 'conv_b': 1.788364e+00, 'dt_bias': 6.014366e-01, 'a_log': 3.971272e+00, 'd_skip': 3.712026e+00, 'ssd_norm_w': 9.772134e-01, 'w_out': 9.504071e-01, 'post_norm_w': 1.601480e+01}


def _to_microbatches(a, axis):
    t = _jnp.moveaxis(a, axis, 0)
    t = t.reshape((N_MICROBATCH, t.shape[0] // N_MICROBATCH) + t.shape[1:])
    return _jnp.moveaxis(t, 1, axis + 1)


def setup_inputs(seed: int = 0) -> dict:
    inp = _fwd_setup_inputs(seed)
    key = _jax.random.fold_in(_jax.random.key(seed), 7919)
    shape, _ = _output_shape()
    out = dict(inp)
    out["loss_target"] = _jax.random.normal(_jax.random.fold_in(key, 0), shape, _jnp.float32)
    for i, name in enumerate(TWIN_WEIGHTS):
        w = inp[name].astype(_jnp.float32)
        if MOMENT_SCALE is None:
            s = _jnp.sqrt(_jnp.mean(_jnp.square(w)) + 1e-30)
        else:
            s = MOMENT_SCALE[name]
        km, kv = _jax.random.split(_jax.random.fold_in(key, i + 1))
        out[name] = w
        out["m_" + name] = s * _jax.random.normal(km, w.shape, _jnp.float32)
        out["v_" + name] = (s * s) * _jax.random.uniform(kv, w.shape, _jnp.float32, 0.5, 1.5)
    if N_MICROBATCH > 1:
        for name, axis in PER_EXAMPLE_BATCH_AXIS.items():
            out[name] = _to_microbatches(out[name], axis)
    return {'x': out['x'], 'pre_norm_w': out['pre_norm_w'], 'w_in': out['w_in'], 'pool_mix_w': out['pool_mix_w'], 'pool_scale': out['pool_scale'], 'conv_w': out['conv_w'], 'conv_b': out['conv_b'], 'dt_bias': out['dt_bias'], 'a_log': out['a_log'], 'd_skip': out['d_skip'], 'ssd_norm_w': out['ssd_norm_w'], 'w_out': out['w_out'], 'post_norm_w': out['post_norm_w'], 'loss_target': out['loss_target'], 'm_pre_norm_w': out['m_pre_norm_w'], 'm_w_in': out['m_w_in'], 'm_pool_mix_w': out['m_pool_mix_w'], 'm_pool_scale': out['m_pool_scale'], 'm_conv_w': out['m_conv_w'], 'm_conv_b': out['m_conv_b'], 'm_dt_bias': out['m_dt_bias'], 'm_a_log': out['m_a_log'], 'm_d_skip': out['m_d_skip'], 'm_ssd_norm_w': out['m_ssd_norm_w'], 'm_w_out': out['m_w_out'], 'm_post_norm_w': out['m_post_norm_w'], 'v_pre_norm_w': out['v_pre_norm_w'], 'v_w_in': out['v_w_in'], 'v_pool_mix_w': out['v_pool_mix_w'], 'v_pool_scale': out['v_pool_scale'], 'v_conv_w': out['v_conv_w'], 'v_conv_b': out['v_conv_b'], 'v_dt_bias': out['v_dt_bias'], 'v_a_log': out['v_a_log'], 'v_d_skip': out['v_d_skip'], 'v_ssd_norm_w': out['v_ssd_norm_w'], 'v_w_out': out['v_w_out'], 'v_post_norm_w': out['v_post_norm_w']}


def _loss(weights, diff, rest, loss_target):
    with _jax.named_scope("forward"):
        args = {**rest, TWIN_DIFF_INPUT: diff, **{k: w.astype(_WEIGHT_DTYPES[k]) for k, w in weights.items()}}
        y = _forward(args)
    with _jax.named_scope("loss_head"):
        err = _jnp.square(y.astype(_jnp.float32) - loss_target)
        return 0.5 * _jnp.sum(_jnp.mean(err, axis=-1)) if err.ndim else 0.5 * err


def _adamw(w, g, m, v):
    m = ADAM_B1 * m + (1.0 - ADAM_B1) * g
    v = ADAM_B2 * v + (1.0 - ADAM_B2) * _jnp.square(g)
    m_hat = m / (1.0 - ADAM_B1 ** ADAM_STEP)
    v_hat = v / (1.0 - ADAM_B2 ** ADAM_STEP)
    delta = -ADAM_LR * (m_hat / (_jnp.sqrt(v_hat) + ADAM_EPS) + ADAM_WD * w)
    return delta, m, v


def reference(x, pre_norm_w, w_in, pool_mix_w, pool_scale, conv_w, conv_b, dt_bias, a_log, d_skip, ssd_norm_w, w_out, post_norm_w, loss_target, m_pre_norm_w, m_w_in, m_pool_mix_w, m_pool_scale, m_conv_w, m_conv_b, m_dt_bias, m_a_log, m_d_skip, m_ssd_norm_w, m_w_out, m_post_norm_w, v_pre_norm_w, v_w_in, v_pool_mix_w, v_pool_scale, v_conv_w, v_conv_b, v_dt_bias, v_a_log, v_d_skip, v_ssd_norm_w, v_w_out, v_post_norm_w):
    given = dict(x=x, pre_norm_w=pre_norm_w, w_in=w_in, pool_mix_w=pool_mix_w, pool_scale=pool_scale, conv_w=conv_w, conv_b=conv_b, dt_bias=dt_bias, a_log=a_log, d_skip=d_skip, ssd_norm_w=ssd_norm_w, w_out=w_out, post_norm_w=post_norm_w, loss_target=loss_target, m_pre_norm_w=m_pre_norm_w, m_w_in=m_w_in, m_pool_mix_w=m_pool_mix_w, m_pool_scale=m_pool_scale, m_conv_w=m_conv_w, m_conv_b=m_conv_b, m_dt_bias=m_dt_bias, m_a_log=m_a_log, m_d_skip=m_d_skip, m_ssd_norm_w=m_ssd_norm_w, m_w_out=m_w_out, m_post_norm_w=m_post_norm_w, v_pre_norm_w=v_pre_norm_w, v_w_in=v_w_in, v_pool_mix_w=v_pool_mix_w, v_pool_scale=v_pool_scale, v_conv_w=v_conv_w, v_conv_b=v_conv_b, v_dt_bias=v_dt_bias, v_a_log=v_a_log, v_d_skip=v_d_skip, v_ssd_norm_w=v_ssd_norm_w, v_w_out=v_w_out, v_post_norm_w=v_post_norm_w)
    weights = {n: given[n] for n in TWIN_WEIGHTS}
    shared = {n: given[n] for n in SHARED_INPUTS}
    per_example = {n: given[n] for n in ['x']}
    grad_fn = _jax.value_and_grad(_loss, argnums=(0, 1))

    def one_microbatch(ex, loss_target):
        ex = dict(ex)
        diff = ex.pop(TWIN_DIFF_INPUT)
        return grad_fn(weights, diff, {**shared, **ex}, loss_target)

    if N_MICROBATCH == 1:
        loss, (grad_w, grad_x) = one_microbatch(per_example, given["loss_target"])
    else:
        def body(carry, xs):
            loss_sum, grad_sum = carry
            l_k, (gw_k, gx_k) = one_microbatch(xs[0], xs[1])
            with _jax.named_scope("update"):
                return (loss_sum + l_k, _jax.tree.map(_jnp.add, grad_sum, gw_k)), gx_k

        init = (_jnp.zeros((), _jnp.float32), _jax.tree.map(_jnp.zeros_like, weights))
        (loss, grad_w), grad_x = _jax.lax.scan(body, init, (per_example, given["loss_target"]))
    with _jax.named_scope("update"):
        delta_w, new_m, new_v = {}, {}, {}
        for n in TWIN_WEIGHTS:
            delta_w[n], new_m[n], new_v[n] = _adamw(weights[n], grad_w[n], given["m_" + n], given["v_" + n])
    return (loss, grad_x, *[grad_w[n] for n in TWIN_WEIGHTS], *[delta_w[n] for n in TWIN_WEIGHTS],
            *[new_m[n] for n in TWIN_WEIGHTS], *[new_v[n] for n in TWIN_WEIGHTS])
```

```python
import functools
import math

import jax
import jax.numpy as jnp
from jax import lax
from jax.experimental import pallas as pl
from jax.experimental.pallas import tpu as pltpu

F32 = jnp.float32
BF16 = jnp.bfloat16
HIGHEST = lax.Precision.HIGHEST

NORM_EPS = 1e-6
HEAD_DIM = 64
STATE = 128
GROUPS = 4
POOL_WINDOWS = (2, 4, 8, 16)
POOL_HALO = 16
CONV_K = 4
CONV_HALO = 8
SCAN_CHUNK = 256
LANES = 128
VMEM_LIMIT = 52 * 1024 * 1024

ADAM_LR = 0.001
ADAM_B1 = 0.9
ADAM_B2 = 0.999
ADAM_EPS = 1e-08
ADAM_WD = 0.01
ADAM_STEP = 10

MESH = pl.DeviceIdType.MESH

NN = (((1,), (0,)), ((), ()))
NT = (((1,), (1,)), ((), ()))
TN = (((0,), (0,)), ((), ()))


def _params(*sem):
    return pltpu.CompilerParams(dimension_semantics=sem, vmem_limit_bytes=VMEM_LIMIT)


def _pick(dim, pref):
    if dim <= pref:
        return dim
    t = (pref // LANES) * LANES
    while t > LANES and dim % t:
        t -= LANES
    assert dim % t == 0, (dim, pref)
    return t


def _dot(a, b, dn=NN):
    return lax.dot_general(a, b, dn, preferred_element_type=F32)


def _dot_exact(a, b):
    return jnp.dot(a, b, precision=HIGHEST, preferred_element_type=F32)


def _sigmoid(v):
    return 1.0 / (1.0 + jnp.exp(-v))


def _silu_and_grad(v):
    s = _sigmoid(v)
    return v * s, s * (1.0 + v * (1.0 - s))


def _row(i, shape):
    return lax.broadcasted_iota(jnp.int32, shape, 0) + i


def matmul(a, b, mode, out_dtype, name, tm=512, tn=1024, tk=2048):
    if mode == "nn":
        (M, K), (K2, N) = a.shape, b.shape
    elif mode == "nt":
        (M, K), (N, K2) = a.shape, b.shape
    else:
        (K, M), (K2, N) = a.shape, b.shape
    assert K == K2
    tm, tn, tk = _pick(M, tm), _pick(N, tn), _pick(K, tk)
    nk = K // tk
    dn = {"nn": NN, "nt": NT, "tn": TN}[mode]

    def body(a_ref, b_ref, o_ref, acc_ref):
        k = pl.program_id(2)
        part = _dot(a_ref[...].astype(BF16), b_ref[...].astype(BF16), dn)

        @pl.when(k == 0)
        def _():
            acc_ref[...] = part

        @pl.when(k > 0)
        def _():
            acc_ref[...] += part

        @pl.when(k == nk - 1)
        def _():
            o_ref[...] = acc_ref[...].astype(o_ref.dtype)

    a_spec = (pl.BlockSpec((tk, tm), lambda i, j, k: (k, i)) if mode == "tn"
              else pl.BlockSpec((tm, tk), lambda i, j, k: (i, k)))
    b_spec = (pl.BlockSpec((tn, tk), lambda i, j, k: (j, k)) if mode == "nt"
              else pl.BlockSpec((tk, tn), lambda i, j, k: (k, j)))
    return pl.pallas_call(
        body, grid=(M // tm, N // tn, nk), in_specs=[a_spec, b_spec],
        out_specs=pl.BlockSpec((tm, tn), lambda i, j, k: (i, j)),
        out_shape=jax.ShapeDtypeStruct((M, N), out_dtype),
        scratch_shapes=[pltpu.VMEM((tm, tn), F32)], name=name,
        compiler_params=_params("parallel", "parallel", "arbitrary"))(a, b)


def rms_fwd(x, w):
    L, D = x.shape
    T = _pick(L, 256)

    def body(x_ref, w_ref, h_ref):
        xv = x_ref[...]
        r = lax.rsqrt(jnp.mean(xv * xv, axis=-1, keepdims=True) + NORM_EPS)
        h_ref[...] = (xv * r * w_ref[...]).astype(h_ref.dtype)

    return pl.pallas_call(
        body, grid=(L // T,),
        in_specs=[pl.BlockSpec((T, D), lambda i: (i, 0)), pl.BlockSpec((1, D), lambda i: (0, 0))],
        out_specs=pl.BlockSpec((T, D), lambda i: (i, 0)),
        out_shape=jax.ShapeDtypeStruct((L, D), BF16), name="rms_fwd",
        compiler_params=_params("parallel"))(x, w)


def post_fwd(x, o, w):
    L, D = x.shape
    T = _pick(L, 256)

    def body(x_ref, o_ref, w_ref, y_ref):
        ov = o_ref[...]
        r = lax.rsqrt(jnp.mean(ov * ov, axis=-1, keepdims=True) + NORM_EPS)
        y_ref[...] = x_ref[...] + ov * r * w_ref[...]

    row = pl.BlockSpec((T, D), lambda i: (i, 0))
    return pl.pallas_call(
        body, grid=(L // T,), in_specs=[row, row, pl.BlockSpec((1, D), lambda i: (0, 0))],
        out_specs=row, out_shape=jax.ShapeDtypeStruct((L, D), F32), name="post_fwd",
        compiler_params=_params("parallel"))(x, o, w)


def _rms_bwd_math(xv, w, dy):
    r = lax.rsqrt(jnp.mean(xv * xv, axis=-1, keepdims=True) + NORM_EPS)
    xhat = xv * r
    g = dy * w
    dx = r * (g - xhat * jnp.mean(g * xhat, axis=-1, keepdims=True))
    return dx, jnp.sum(dy * xhat, axis=0, keepdims=True)


def post_bwd(o, w, dxn):
    L, D = o.shape
    T = _pick(L, 256)

    def body(o_ref, w_ref, d_ref, do_ref, dw_ref):
        dx, dw = _rms_bwd_math(o_ref[...], w_ref[...], d_ref[...])
        do_ref[...] = dx.astype(do_ref.dtype)

        @pl.when(pl.program_id(0) == 0)
        def _():
            dw_ref[...] = jnp.zeros_like(dw_ref)

        dw_ref[...] += dw

    row = pl.BlockSpec((T, D), lambda i: (i, 0))
    vec = pl.BlockSpec((1, D), lambda i: (0, 0))
    return pl.pallas_call(
        body, grid=(L // T,), in_specs=[row, vec, row], out_specs=[row, vec],
        out_shape=[jax.ShapeDtypeStruct((L, D), BF16), jax.ShapeDtypeStruct((1, D), F32)],
        name="post_bwd", compiler_params=_params("arbitrary"))(o, w, dxn)


def rms_bwd(x, w, dh_a, dh_b, dxn):
    L, D = x.shape
    T = _pick(L, 256)

    def body(x_ref, w_ref, a_ref, b_ref, d_ref, dx_ref, dw_ref):
        dx, dw = _rms_bwd_math(x_ref[...], w_ref[...], a_ref[...] + b_ref[...])
        dx_ref[...] = d_ref[...] + dx

        @pl.when(pl.program_id(0) == 0)
        def _():
            dw_ref[...] = jnp.zeros_like(dw_ref)

        dw_ref[...] += dw

    row = pl.BlockSpec((T, D), lambda i: (i, 0))
    vec = pl.BlockSpec((1, D), lambda i: (0, 0))
    return pl.pallas_call(
        body, grid=(L // T,), in_specs=[row, vec, row, row, row], out_specs=[row, vec],
        out_shape=[jax.ShapeDtypeStruct((L, D), F32), jax.ShapeDtypeStruct((1, D), F32)],
        name="rms_bwd", compiler_params=_params("arbitrary"))(x, w, dh_a, dh_b, dxn)


def loss_head(y, target):
    L, D = y.shape
    T = _pick(L, 256)

    def body(y_ref, t_ref, d_ref, s_ref):
        e = y_ref[...] - t_ref[...]
        d_ref[...] = e * (1.0 / D)

        @pl.when(pl.program_id(0) == 0)
        def _():
            s_ref[...] = jnp.zeros_like(s_ref)

        s_ref[...] += jnp.sum(e * e)

    row = pl.BlockSpec((T, D), lambda i: (i, 0))
    return pl.pallas_call(
        body, grid=(L // T,), in_specs=[row, row],
        out_specs=[row, pl.BlockSpec((8, LANES), lambda i: (0, 0))],
        out_shape=[jax.ShapeDtypeStruct((L, D), F32), jax.ShapeDtypeStruct((8, LANES), F32)],
        name="loss_head", compiler_params=_params("arbitrary"))(y, target)


def _window_sums(xe, w, back):
    n = xe.shape[0]
    s, k = xe, 1
    while k < w:
        s = s + pltpu.roll(s, k if back else n - k, 0)
        k *= 2
    return s


def pool_fwd(proj, mixw, scale, D):
    L = proj.shape[0]
    PGW = D // GROUPS
    T = _pick(L, 256)
    hb = T // POOL_HALO

    def body(u_ref, halo_ref, g_ref, mw_ref, sc_ref, y_ref, p_ref):
        i = pl.program_id(0)
        u = u_ref[...]
        halo = jnp.where(i > 0, halo_ref[...], 0.0)
        xe = jnp.concatenate([halo, u], axis=0)
        t1 = _row(i * T + 1, (T, 1))
        for g, w in enumerate(POOL_WINDOWS):
            sl = slice(g * PGW, (g + 1) * PGW)
            win = _window_sums(xe[:, sl], w, True)[POOL_HALO:, :]
            cnt = jnp.minimum(t1, w).astype(F32)
            pooled = (win / cnt - u[:, sl]).astype(BF16)
            p_ref[:, sl] = pooled
            mixed = _dot(pooled, mw_ref[g])
            gate = g_ref[:, sl]
            y_ref[:, sl] = (mixed * sc_ref[:, sl] * (gate * _sigmoid(gate))).astype(BF16)

    return pl.pallas_call(
        body, grid=(L // T,),
        in_specs=[pl.BlockSpec((T, D), lambda i: (i, 0)),
                  pl.BlockSpec((POOL_HALO, D), lambda i: (jnp.maximum(i * hb - 1, 0), 0)),
                  pl.BlockSpec((T, D), lambda i: (i, 1)),
                  pl.BlockSpec((GROUPS, PGW, PGW), lambda i: (0, 0, 0)),
                  pl.BlockSpec((1, D), lambda i: (0, 0))],
        out_specs=[pl.BlockSpec((T, D), lambda i: (i, 0)), pl.BlockSpec((T, D), lambda i: (i, 0))],
        out_shape=[jax.ShapeDtypeStruct((L, D), BF16), jax.ShapeDtypeStruct((L, D), BF16)],
        name="pool_fwd", compiler_params=_params("parallel"))(proj, proj, proj, mixw, scale)


def pool_bwd(proj, dmix, pooled, mixw, scale, D):
    L = proj.shape[0]
    PGW = D // GROUPS
    T = _pick(L, 256)
    hb = T // POOL_HALO
    nT = L // T

    def body(g_ref, gh_ref, dy_ref, dyh_ref, p_ref, mw_ref, sc_ref, du_ref, dg_ref, dm_ref, ds_ref):
        i = pl.program_id(0)
        t1 = _row(i * T + 1, (T, 1))
        th1 = _row((i + 1) * T + 1, (POOL_HALO, 1))
        live = i < nT - 1

        @pl.when(i == 0)
        def _():
            ds_ref[...] = jnp.zeros_like(ds_ref)

        for g, w in enumerate(POOL_WINDOWS):
            sl = slice(g * PGW, (g + 1) * PGW)
            sc = sc_ref[:, sl]
            gate, dy = g_ref[:, sl], dy_ref[:, sl]
            sg, dsg = _silu_and_grad(gate)
            mixed = _dot(p_ref[:, sl], mw_ref[g])
            dmixed = (dy * sc * sg).astype(BF16)
            dm_ref[:, sl] = dmixed
            dg_ref[:, sl] = (dy * mixed * sc * dsg).astype(BF16)
            ds_ref[:, sl] += jnp.sum(dy * mixed * sg, axis=0, keepdims=True)
            dpool = _dot(dmixed, mw_ref[g], NT)
            gate_h = gh_ref[:, sl]
            dmixed_h = (dyh_ref[:, sl] * sc * (gate_h * _sigmoid(gate_h))).astype(BF16)
            dpool_h = jnp.where(live, _dot(dmixed_h, mw_ref[g], NT), 0.0)
            q = dpool / jnp.minimum(t1, w).astype(F32)
            q_h = dpool_h / jnp.minimum(th1, w).astype(F32)
            qe = jnp.concatenate([q, q_h], axis=0)
            du_ref[:, sl] = (_window_sums(qe, w, False)[:T, :] - dpool).astype(BF16)

    nxt = lambda i: (jnp.minimum((i + 1) * hb, L // POOL_HALO - 1), 0)
    row = lambda c: pl.BlockSpec((T, D), lambda i: (i, c))
    return pl.pallas_call(
        body, grid=(nT,),
        in_specs=[row(1),
                  pl.BlockSpec((POOL_HALO, D), lambda i: (nxt(i)[0], 1)),
                  row(0),
                  pl.BlockSpec((POOL_HALO, D), nxt),
                  row(0),
                  pl.BlockSpec((GROUPS, PGW, PGW), lambda i: (0, 0, 0)),
                  pl.BlockSpec((1, D), lambda i: (0, 0))],
        out_specs=[row(0), row(0), row(0), pl.BlockSpec((1, D), lambda i: (0, 0))],
        out_shape=[jax.ShapeDtypeStruct((L, D), BF16)] * 3 + [jax.ShapeDtypeStruct((1, D), F32)],
        name="pool_bwd", compiler_params=_params("arbitrary"))(proj, proj, dmix, dmix, pooled, mixw, scale)


def pool_dw(pooled, dmixed, D):
    L = pooled.shape[0]
    PGW = D // GROUPS
    tk = _pick(L, 1024)
    nk = L // tk

    def body(p_ref, d_ref, o_ref):
        @pl.when(pl.program_id(1) == 0)
        def _():
            o_ref[...] = jnp.zeros_like(o_ref)

        o_ref[0] += _dot(p_ref[...], d_ref[...], TN)

    blk = pl.BlockSpec((tk, PGW), lambda g, k: (k, g))
    return pl.pallas_call(
        body, grid=(GROUPS, nk), in_specs=[blk, blk],
        out_specs=pl.BlockSpec((1, PGW, PGW), lambda g, k: (g, 0, 0)),
        out_shape=jax.ShapeDtypeStruct((GROUPS, PGW, PGW), F32), name="pool_dw",
        compiler_params=_params("parallel", "arbitrary"))(pooled, dmixed)


def conv_fwd(proj, cw, cb, D):
    L = proj.shape[0]
    C = cw.shape[1]
    assert (3 * D) % C == 0
    cblk = (3 * D) // C
    T = _pick(L, 256)
    hb = T // CONV_HALO

    def body(u_ref, halo_ref, w_ref, b_ref, o_ref):
        i = pl.program_id(0)
        u = u_ref[...]
        xe = jnp.concatenate([jnp.where(i > 0, halo_ref[...], 0.0), u], axis=0)
        acc = b_ref[...] + w_ref[CONV_K - 1:CONV_K, :] * u
        for k in range(CONV_K - 1):
            acc = acc + w_ref[k:k + 1, :] * pltpu.roll(xe, CONV_K - 1 - k, 0)[CONV_HALO:, :]
        o_ref[...] = acc

    return pl.pallas_call(
        body, grid=(L // T,),
        in_specs=[pl.BlockSpec((T, C), lambda i: (i, cblk)),
                  pl.BlockSpec((CONV_HALO, C), lambda i: (jnp.maximum(i * hb - 1, 0), cblk)),
                  pl.BlockSpec((CONV_K, C), lambda i: (0, 0)),
                  pl.BlockSpec((1, C), lambda i: (0, 0))],
        out_specs=pl.BlockSpec((T, C), lambda i: (i, 0)),
        out_shape=jax.ShapeDtypeStruct((L, C), F32), name="conv_fwd",
        compiler_params=_params("parallel"))(proj, proj, cw, cb)


def conv_bwd(dpre, proj, cw, D):
    L, C = dpre.shape
    cblk = (3 * D) // C
    T = _pick(L, 256)
    hb = T // CONV_HALO
    nT = L // T

    def body(d_ref, dn_ref, u_ref, up_ref, w_ref, dr_ref, dw_ref, db_ref):
        i = pl.program_id(0)
        d = d_ref[...]
        de = jnp.concatenate([d, jnp.where(i < nT - 1, dn_ref[...], 0.0)], axis=0)
        ue = jnp.concatenate([jnp.where(i > 0, up_ref[...], 0.0), u_ref[...]], axis=0)

        @pl.when(i == 0)
        def _():
            dw_ref[...] = jnp.zeros_like(dw_ref)
            db_ref[...] = jnp.zeros_like(db_ref)

        acc = w_ref[CONV_K - 1:CONV_K, :] * d
        dw_ref[CONV_K - 1:CONV_K, :] += jnp.sum(d * u_ref[...], axis=0, keepdims=True)
        for k in range(CONV_K - 1):
            sh = CONV_K - 1 - k
            acc = acc + w_ref[k:k + 1, :] * pltpu.roll(de, T + CONV_HALO - sh, 0)[:T, :]
            dw_ref[k:k + 1, :] += jnp.sum(d * pltpu.roll(ue, sh, 0)[CONV_HALO:, :], axis=0, keepdims=True)
        dr_ref[...] = acc.astype(dr_ref.dtype)
        db_ref[...] += jnp.sum(d, axis=0, keepdims=True)

    return pl.pallas_call(
        body, grid=(nT,),
        in_specs=[pl.BlockSpec((T, C), lambda i: (i, 0)),
                  pl.BlockSpec((CONV_HALO, C), lambda i: (jnp.minimum((i + 1) * hb, L // CONV_HALO - 1), 0)),
                  pl.BlockSpec((T, C), lambda i: (i, cblk)),
                  pl.BlockSpec((CONV_HALO, C), lambda i: (jnp.maximum(i * hb - 1, 0), cblk)),
                  pl.BlockSpec((CONV_K, C), lambda i: (0, 0))],
        out_specs=[pl.BlockSpec((T, C), lambda i: (i, 0)),
                   pl.BlockSpec((8, C), lambda i: (0, 0)),
                   pl.BlockSpec((1, C), lambda i: (0, 0))],
        out_shape=[jax.ShapeDtypeStruct((L, C), BF16), jax.ShapeDtypeStruct((8, C), F32),
                   jax.ShapeDtypeStruct((1, C), F32)],
        name="conv_bwd", compiler_params=_params("arbitrary"))(dpre, dpre, proj, proj, cw)


def _softplus(v):
    y = jnp.exp(-jnp.abs(v))
    u = 1.0 + y
    log1p = jnp.where(u == 1.0, y, jnp.log(u) * y / jnp.where(u == 1.0, 1.0, u - 1.0))
    return jnp.maximum(v, 0.0) + log1p


def dt_prep(dtraw, bias, alog, expand, D):
    L = dtraw.shape[0]
    GC = D // GROUPS
    HPG = GC // HEAD_DIM
    Q = _pick(L, SCAN_CHUNK)
    nc = L // Q

    def body(r_ref, b_ref, a_ref, e_ref, dt_ref, acs_ref, acst_ref, dtx_ref, eax_ref, dsx_ref, cdx_ref):
        valid = lax.broadcasted_iota(jnp.int32, (1, LANES), 1) < HPG
        dt = jnp.where(valid, _softplus(r_ref[...] + b_ref[...]), 0.0)
        adt = dt * -jnp.exp(a_ref[...])
        tril = (_row(0, (Q, Q)) >= lax.broadcasted_iota(jnp.int32, (Q, Q), 1)).astype(F32)
        acs = _dot_exact(tril, adt)
        last = acs[Q - 1:Q, :]
        dt_ref[...] = dt
        acs_ref[...] = acs
        acst_ref[...] = acs.T
        e = e_ref[...]
        dtx_ref[...] = _dot_exact(dt, e)
        eax_ref[...] = jnp.exp(_dot_exact(acs, e))
        dsx_ref[...] = jnp.exp(_dot_exact(last - acs, e))
        cdx_ref[0] = jnp.exp(_dot_exact(jnp.broadcast_to(last, (8, LANES)), e))

    head = pl.BlockSpec((Q, LANES), lambda g, c: (c, g))
    hvec = pl.BlockSpec((1, LANES), lambda g, c: (0, g))
    chan = pl.BlockSpec((Q, GC), lambda g, c: (c, g))
    return pl.pallas_call(
        body, grid=(GROUPS, nc),
        in_specs=[head, hvec, hvec, pl.BlockSpec((LANES, GC), lambda g, c: (0, 0))],
        out_specs=[head, head, pl.BlockSpec((LANES, Q), lambda g, c: (g, c)), chan, chan, chan,
                   pl.BlockSpec((1, 8, GC), lambda g, c: (c, 0, g))],
        out_shape=[jax.ShapeDtypeStruct((L, GROUPS * LANES), F32),
                   jax.ShapeDtypeStruct((L, GROUPS * LANES), F32),
                   jax.ShapeDtypeStruct((GROUPS * LANES, L), F32),
                   jax.ShapeDtypeStruct((L, D), F32), jax.ShapeDtypeStruct((L, D), F32),
                   jax.ShapeDtypeStruct((L, D), F32), jax.ShapeDtypeStruct((nc, 8, D), F32)],
        name="dt_prep", compiler_params=_params("parallel", "parallel"))(dtraw, bias, alog, expand)


def _scan_specs(L, D, Q, rev):
    GC = D // GROUPS
    nc = L // Q
    ci = (lambda c: nc - 1 - c) if rev else (lambda c: c)
    return dict(
        xs=pl.BlockSpec((Q, GC), lambda g, c: (ci(c), g)),
        b=pl.BlockSpec((Q, STATE), lambda g, c: (ci(c), D // STATE + g)),
        c=pl.BlockSpec((Q, STATE), lambda g, c: (ci(c), D // STATE + GROUPS + g)),
        chan=pl.BlockSpec((Q, GC), lambda g, c: (ci(c), g)),
        cdx=pl.BlockSpec((1, 8, GC), lambda g, c: (ci(c), 0, g)),
        head=pl.BlockSpec((Q, LANES), lambda g, c: (ci(c), g)),
        headt=pl.BlockSpec((LANES, Q), lambda g, c: (g, ci(c))),
        state=pl.BlockSpec((1, 1, STATE, GC), lambda g, c: (ci(c), g, 0, 0)),
        hvec=pl.BlockSpec((1, LANES), lambda g, c: (0, g)),
        cvec=pl.BlockSpec((1, GC), lambda g, c: (0, g)))


def scan_fwd(pre, dtx, eax, dsx, cdx, acs, acst, D):
    L = pre.shape[0]
    GC = D // GROUPS
    Q = _pick(L, SCAN_CHUNK)
    nc = L // Q
    sp = _scan_specs(L, D, Q, False)

    def body(xs_ref, b_ref, c_ref, dtx_ref, eax_ref, dsx_ref, cdx_ref, acs_ref, acst_ref, y_ref, st_ref, s_scr):
        @pl.when(pl.program_id(1) == 0)
        def _():
            s_scr[...] = jnp.zeros_like(s_scr)

        tri = _row(0, (Q, Q)) >= lax.broadcasted_iota(jnp.int32, (Q, Q), 1)
        half = lax.broadcasted_iota(jnp.int32, (1, LANES), 1) // HEAD_DIM
        xs, _ = _silu_and_grad(xs_ref[...])
        bg = _silu_and_grad(b_ref[...])[0].astype(BF16)
        cg = _silu_and_grad(c_ref[...])[0].astype(BF16)
        xdt = xs * dtx_ref[...]
        sprev = s_scr[...]
        st_ref[0, 0] = sprev
        sc = _dot(cg, bg, NT)
        yoff = _dot(cg, sprev.astype(BF16)) * eax_ref[...]
        for j in range(GC // LANES):
            ps = slice(j * LANES, (j + 1) * LANES)
            xp = xdt[:, ps]
            acc = yoff[:, ps]
            for hh in range(2):
                h = 2 * j + hh
                lm = jnp.exp(jnp.where(tri, acs_ref[:, h:h + 1] - acst_ref[h:h + 1, :], -1e30))
                xm = jnp.where(half == hh, xp, 0.0).astype(BF16)
                acc = acc + _dot((sc * lm).astype(BF16), xm)
            y_ref[:, ps] = acc
        xw = (xdt * dsx_ref[...]).astype(BF16)
        s_scr[...] = cdx_ref[0, 0:1, :] * sprev + _dot(bg, xw, TN)

    return pl.pallas_call(
        body, grid=(GROUPS, nc),
        in_specs=[sp["xs"], sp["b"], sp["c"], sp["chan"], sp["chan"], sp["chan"], sp["cdx"], sp["head"], sp["headt"]],
        out_specs=[sp["chan"], sp["state"]],
        out_shape=[jax.ShapeDtypeStruct((L, D), F32), jax.ShapeDtypeStruct((nc, GROUPS, STATE, GC), F32)],
        scratch_shapes=[pltpu.VMEM((STATE, GC), F32)], name="scan_fwd",
        compiler_params=_params("parallel", "arbitrary"))(pre, pre, pre, dtx, eax, dsx, cdx, acs, acst)


def scan_bwd(pre, dtx, eax, dsx, cdx, acs, acst, dt, dtraw, bias, alog, states, dy, dexp, collapse, D):
    L, C = pre.shape
    GC = D // GROUPS
    Q = _pick(L, SCAN_CHUNK)
    nc = L // Q
    sp = _scan_specs(L, D, Q, True)
    rc = lambda c: nc - 1 - c

    def body(xs_ref, b_ref, c_ref, dtx_ref, eax_ref, dsx_ref, cdx_ref, acs_ref, acst_ref, dt_ref, raw_ref,
             bias_ref, alog_ref, st_ref, dy_ref, dexp_ref, col_ref,
             dxs_ref, db_ref, dc_ref, ddt_ref, dal_ref, dbi_ref, ds_scr, dx_scr):
        first = pl.program_id(1) == 0

        @pl.when(first)
        def _():
            ds_scr[...] = jnp.zeros_like(ds_scr)
            dal_ref[...] = jnp.zeros_like(dal_ref)
            dbi_ref[...] = jnp.zeros_like(dbi_ref)

        li = _row(0, (Q, Q))
        si = lax.broadcasted_iota(jnp.int32, (Q, Q), 1)
        lane = lax.broadcasted_iota(jnp.int32, (1, LANES), 1)
        half = lane // HEAD_DIM
        xs_pre, b_pre, c_pre = xs_ref[...], b_ref[...], c_ref[...]
        xs, xs_g = _silu_and_grad(xs_pre)
        bf, b_g = _silu_and_grad(b_pre)
        cf, c_g = _silu_and_grad(c_pre)
        bg, cg = bf.astype(BF16), cf.astype(BF16)
        dtx, eax, dsx = dtx_ref[...], eax_ref[...], dsx_ref[...]
        cd = cdx_ref[0, 0:1, :]
        xdt = xs * dtx
        G = dy_ref[...]
        prev = st_ref[0, 0]
        dsn = ds_scr[...]
        prev_b, dsn_b = prev.astype(BF16), dsn.astype(BF16)
        cp = _dot(cg, prev_b)
        ge_b = (G * eax).astype(BF16)
        d_c = _dot(ge_b, prev_b, NT)
        dprev = _dot(cg, ge_b, TN) + cd * dsn
        chan_a = G * cp * eax
        xw_b = (xdt * dsx).astype(BF16)
        dcd = jnp.sum(prev * dsn, axis=0, keepdims=True)
        d_b = _dot(xw_b, dsn_b, NT)
        dxw = _dot(bg, dsn_b)
        dd = dxw * xdt * dsx
        chan_a = chan_a - dd
        last_c = jnp.sum(dd, axis=0, keepdims=True) + dcd * cd
        sc = _dot(cg, bg, NT)
        sct = _dot(bg, cg, NT)
        dsc = jnp.zeros((Q, Q), F32)
        dsct = jnp.zeros((Q, Q), F32)
        dacs = jnp.zeros((Q, LANES), F32)
        for j in range(GC // LANES):
            ps = slice(j * LANES, (j + 1) * LANES)
            xp, gp = xdt[:, ps], G[:, ps]
            dxp = dxw[:, ps] * dsx[:, ps]
            for hh in range(2):
                h = 2 * j + hh
                col, row = acs_ref[:, h:h + 1], acst_ref[h:h + 1, :]
                lm = jnp.exp(jnp.where(li >= si, col - row, -1e30))
                lmt = jnp.exp(jnp.where(si >= li, row - col, -1e30))
                m, mt = sc * lm, sct * lmt
                xm = jnp.where(half == hh, xp, 0.0).astype(BF16)
                gm = jnp.where(half == hh, gp, 0.0).astype(BF16)
                dm = _dot(gm, xm, NT)
                dmt = _dot(xm, gm, NT)
                dxp = dxp + _dot(mt.astype(BF16), gm)
                dsc = dsc + dm * lm
                dsct = dsct + dmt * lmt
                rs = jnp.sum(dm * m, axis=1, keepdims=True) - jnp.sum(dmt * mt, axis=1, keepdims=True)
                dacs = dacs + jnp.where(lane == h, rs, 0.0)
            dx_scr[:, ps] = dxp
        d_c = d_c + _dot(dsc.astype(BF16), bg)
        d_b = d_b + _dot(dsct.astype(BF16), cg)
        ds_scr[...] = dprev
        dxdt = dx_scr[...]
        dxs_ref[...] = (dxdt * dtx + dexp_ref[...] * G) * xs_g
        db_ref[...] = d_b * b_g
        dc_ref[...] = d_c * c_g
        colm = col_ref[...]
        dacs = dacs + _dot_exact(chan_a, colm)
        dlast = _dot_exact(jnp.broadcast_to(last_c, (8, GC)), colm)[0:1, :]
        dacs = dacs + jnp.where(_row(0, (Q, 1)) == Q - 1, dlast, 0.0)
        dadt = _dot_exact((si >= li).astype(F32), dacs)
        a = -jnp.exp(alog_ref[...])
        dt = dt_ref[...]
        ddt = dadt * a + _dot_exact(dxdt * xs, colm)
        dal_ref[0:1, :] += jnp.sum(dadt * dt * a, axis=0, keepdims=True)
        draw = ddt * _sigmoid(raw_ref[...] + bias_ref[...])
        dbi_ref[0:1, :] += jnp.sum(draw, axis=0, keepdims=True)
        ddt_ref[...] = draw.astype(ddt_ref.dtype)

    acc = pl.BlockSpec((8, LANES), lambda g, c: (0, g))
    return pl.pallas_call(
        body, grid=(GROUPS, nc),
        in_specs=[sp["xs"], sp["b"], sp["c"], sp["chan"], sp["chan"], sp["chan"], sp["cdx"], sp["head"], sp["headt"],
                  sp["head"], sp["head"], sp["hvec"], sp["hvec"], sp["state"], sp["chan"], sp["cvec"],
                  pl.BlockSpec((GC, LANES), lambda g, c: (0, 0))],
        out_specs=[sp["chan"],
                   pl.BlockSpec((Q, STATE), lambda g, c: (rc(c), g)),
                   pl.BlockSpec((Q, STATE), lambda g, c: (rc(c), g)),
                   sp["head"], acc, acc],
        out_shape=[jax.ShapeDtypeStruct((L, D), F32),
                   jax.ShapeDtypeStruct((L, GROUPS * STATE), F32),
                   jax.ShapeDtypeStruct((L, GROUPS * STATE), F32),
                   jax.ShapeDtypeStruct((L, GROUPS * LANES), BF16),
                   jax.ShapeDtypeStruct((8, GROUPS * LANES), F32),
                   jax.ShapeDtypeStruct((8, GROUPS * LANES), F32)],
        scratch_shapes=[pltpu.VMEM((STATE, GC), F32), pltpu.VMEM((Q, GC), F32)], name="scan_bwd",
        compiler_params=_params("parallel", "arbitrary"))(
            pre, pre, pre, dtx, eax, dsx, cdx, acs, acst, dt, dtraw, bias, alog, states, dy, dexp, collapse)


def gate_fwd(y, pre, proj, dexp, nw, D):
    L = y.shape[0]
    GC = D // GROUPS
    T = _pick(L, 256)

    def body(y_ref, xs_ref, z_ref, de_ref, nw_ref, o_ref):
        xs, _ = _silu_and_grad(xs_ref[...])
        z = z_ref[...]
        y3 = (y_ref[...] + de_ref[...] * xs) * (z * _sigmoid(z))
        for g in range(GROUPS):
            sl = slice(g * GC, (g + 1) * GC)
            yg = y3[:, sl]
            r = lax.rsqrt(jnp.mean(yg * yg, axis=-1, keepdims=True) + NORM_EPS)
            o_ref[:, sl] = (yg * r * nw_ref[:, sl]).astype(o_ref.dtype)

    row = lambda c: pl.BlockSpec((T, D), lambda i: (i, c))
    vec = pl.BlockSpec((1, D), lambda i: (0, 0))
    return pl.pallas_call(
        body, grid=(L // T,), in_specs=[row(0), row(0), row(2), vec, vec], out_specs=row(0),
        out_shape=jax.ShapeDtypeStruct((L, D), BF16), name="gate_fwd",
        compiler_params=_params("parallel"))(y, pre, proj, dexp, nw)


def gate_bwd(y, pre, proj, dexp, nw, dmix, D):
    L = y.shape[0]
    GC = D // GROUPS
    T = _pick(L, 256)

    def body(y_ref, xs_ref, z_ref, de_ref, nw_ref, dm_ref, dy_ref, dz_ref, dnw_ref, dde_ref):
        @pl.when(pl.program_id(0) == 0)
        def _():
            dnw_ref[...] = jnp.zeros_like(dnw_ref)
            dde_ref[...] = jnp.zeros_like(dde_ref)

        xs, _ = _silu_and_grad(xs_ref[...])
        sz, dsz = _silu_and_grad(z_ref[...])
        y2 = y_ref[...] + de_ref[...] * xs
        y3 = y2 * sz
        for g in range(GROUPS):
            sl = slice(g * GC, (g + 1) * GC)
            yg, dm = y3[:, sl], dm_ref[:, sl]
            r = lax.rsqrt(jnp.mean(yg * yg, axis=-1, keepdims=True) + NORM_EPS)
            n = yg * r
            gg = dm * nw_ref[:, sl]
            dy3 = r * (gg - n * jnp.mean(gg * n, axis=-1, keepdims=True))
            dnw_ref[:, sl] += jnp.sum(dm * n, axis=0, keepdims=True)
            dy2 = dy3 * sz[:, sl]
            dy_ref[:, sl] = dy2
            dz_ref[:, sl] = (dy3 * y2[:, sl] * dsz[:, sl]).astype(dz_ref.dtype)
            dde_ref[:, sl] += jnp.sum(dy2 * xs[:, sl], axis=0, keepdims=True)

    row = lambda c: pl.BlockSpec((T, D), lambda i: (i, c))
    vec = pl.BlockSpec((1, D), lambda i: (0, 0))
    return pl.pallas_call(
        body, grid=(L // T,), in_specs=[row(0), row(0), row(2), vec, vec, row(1)],
        out_specs=[row(0), row(0), vec, vec],
        out_shape=[jax.ShapeDtypeStruct((L, D), F32), jax.ShapeDtypeStruct((L, D), BF16),
                   jax.ShapeDtypeStruct((1, D), F32), jax.ShapeDtypeStruct((1, D), F32)],
        name="gate_bwd", compiler_params=_params("arbitrary"))(y, pre, proj, dexp, nw, dmix)


def adamw(g, w, m, v, name):
    R, C = g.shape
    T = R if R <= 128 else 128
    assert R % T == 0
    c1 = 1.0 - ADAM_B1 ** ADAM_STEP
    c2 = 1.0 - ADAM_B2 ** ADAM_STEP

    def body(g_ref, w_ref, m_ref, v_ref, d_ref, nm_ref, nv_ref):
        gv = g_ref[...]
        nm = ADAM_B1 * m_ref[...] + (1.0 - ADAM_B1) * gv
        nv = ADAM_B2 * v_ref[...] + (1.0 - ADAM_B2) * (gv * gv)
        d_ref[...] = -ADAM_LR * ((nm / c1) / (jnp.sqrt(nv / c2) + ADAM_EPS) + ADAM_WD * w_ref[...])
        nm_ref[...] = nm
        nv_ref[...] = nv

    blk = pl.BlockSpec((T, C), lambda i: (i, 0))
    return pl.pallas_call(
        body, grid=(R // T,), in_specs=[blk] * 4, out_specs=[blk] * 3,
        out_shape=[jax.ShapeDtypeStruct((R, C), F32)] * 3, name=name,
        compiler_params=_params("parallel"))(g, w, m, v)


def _to_groups(v, hpg):
    lead = v.shape[:-1]
    t = v.reshape(lead + (GROUPS, hpg))
    t = jnp.pad(t, [(0, 0)] * (len(lead) + 1) + [(0, LANES - hpg)])
    return t.reshape(lead + (GROUPS * LANES,))


def _from_groups(a, hpg):
    lead = a.shape[:-1]
    return a.reshape(lead + (GROUPS, LANES))[..., :hpg].reshape(lead + (GROUPS * hpg,))


def _expand_matrix(D):
    gc = D // GROUPS
    return (jnp.arange(LANES)[:, None] == (jnp.arange(gc)[None, :] // HEAD_DIM)).astype(F32)


def layer_fwd(x, p, D):
    h = rms_fwd(x, p["pre_w"])
    proj = matmul(h, p["wmain"], "nn", F32, "proj")
    dtraw = matmul(h, p["wdt"], "nn", F32, "dtproj")
    ypool, pooled = pool_fwd(proj, p["mixw"], p["scale"], D)
    pre = conv_fwd(proj, p["cw"], p["cb"], D)
    dtp = dt_prep(dtraw, p["bias"], p["alog"], _expand_matrix(D), D)
    dt, acs, acst, dtx, eax, dsx, cdx = dtp
    y, states = scan_fwd(pre, dtx, eax, dsx, cdx, acs, acst, D)
    yssd = gate_fwd(y, pre, proj, p["dexp"], p["nw"], D)
    mixed = jnp.concatenate([ypool, yssd], axis=1)
    out = matmul(mixed, p["wout"], "nn", F32, "outproj")
    xn = post_fwd(x, out, p["post_w"])
    return xn, dict(x=x, h=h, proj=proj, dtraw=dtraw, pooled=pooled, pre=pre, dtp=dtp, y=y, states=states,
                    mixed=mixed, out=out)


def layer_bwd(dxn, p, s, D):
    hpg = D // GROUPS // HEAD_DIM
    dt, acs, acst, dtx, eax, dsx, cdx = s["dtp"]
    dout, d_post = post_bwd(s["out"], p["post_w"], dxn)
    dmix = matmul(dout, p["wout"], "nt", F32, "dmixed")
    d_wout = matmul(s["mixed"], dout, "tn", F32, "dwout")
    dy2, dz, d_nw, d_dexp = gate_bwd(s["y"], s["pre"], s["proj"], p["dexp"], p["nw"], dmix, D)
    dxs, db, dc, ddtraw, d_alog, d_bias = scan_bwd(
        s["pre"], dtx, eax, dsx, cdx, acs, acst, dt, s["dtraw"], p["bias"], p["alog"], s["states"], dy2,
        p["dexp"], _expand_matrix(D).T, D)
    dpre = jnp.concatenate([dxs, db, dc], axis=1)
    draw, d_cw, d_cb = conv_bwd(dpre, s["proj"], p["cw"], D)
    du, dgate, dmixed, d_scale = pool_bwd(s["proj"], dmix, s["pooled"], p["mixw"], p["scale"], D)
    d_mixw = pool_dw(s["pooled"], dmixed, D)
    dproj = jnp.concatenate([du, dgate, dz, draw], axis=1)
    dh_a = matmul(dproj, p["wmain"], "nt", F32, "dh_main")
    dh_b = matmul(ddtraw, p["wdt"], "nt", F32, "dh_dt")
    d_wmain = matmul(s["h"], dproj, "tn", F32, "dwmain")
    d_wdt = matmul(s["h"], ddtraw, "tn", F32, "dwdt")
    dx, d_pre = rms_bwd(s["x"], p["pre_w"], dh_a, dh_b, dxn)
    grads = dict(
        pre_norm_w=d_pre[0], w_in=jnp.concatenate([d_wmain, _from_groups(d_wdt, hpg)], axis=1), pool_mix_w=d_mixw,
        pool_scale=d_scale[0], conv_w=d_cw[:CONV_K], conv_b=d_cb[0], dt_bias=_from_groups(d_bias[0], hpg),
        a_log=_from_groups(d_alog[0], hpg), d_skip=d_dexp[0].reshape(-1, HEAD_DIM).sum(axis=-1),
        ssd_norm_w=d_nw[0], w_out=d_wout, post_norm_w=d_post[0])
    return dx, grads


def layer_params(pre_w, w_in_full, mixw, scale, cw, cb, bias, alog, dskip, nw, w_out_full, post_w, D):
    hpg = D // GROUPS // HEAD_DIM
    main = w_in_full.shape[1] - GROUPS * hpg
    return dict(
        pre_w=pre_w[None], wmain=w_in_full[:, :main], wdt=_to_groups(w_in_full[:, main:], hpg), mixw=mixw,
        scale=scale[None], cw=cw, cb=cb[None], bias=_to_groups(bias, hpg)[None], alog=_to_groups(alog, hpg)[None],
        dexp=jnp.repeat(dskip, HEAD_DIM)[None], nw=nw[None], wout=w_out_full, post_w=post_w[None])


def local_step(x, target, params, D):
    saved = []
    for p in params:
        x, s = layer_fwd(x, p, D)
        saved.append(s)
    dx, sumsq = loss_head(x, target)
    grads = [None] * len(params)
    for l in reversed(range(len(params))):
        dx, grads[l] = layer_bwd(dx, params[l], saved[l], D)
    return sumsq, dx, grads


_ANY = pl.BlockSpec(memory_space=pl.ANY)


def _place():
    x, y, c = lax.axis_index("x"), lax.axis_index("y"), lax.axis_index("c")
    return x, y, c, [(1 - x, y), (x, 1 - y), (1 - x, 1 - y)]


def _remote(src, dst, send_sem, recv_sem, device):
    return pltpu.make_async_remote_copy(src_ref=src, dst_ref=dst, send_sem=send_sem, recv_sem=recv_sem,
                                        device_id=device, device_id_type=MESH)


def gather_chips(arrs):
    n = len(arrs)

    def body(*refs):
        ins, outs = refs[:n], refs[n:2 * n]
        send_sems, recv_sems, local_sems = refs[2 * n:]
        x, y, c, chips = _place()
        sibling = (x, y, 1 - c)
        mine = 2 * x + y
        local, sent = [], []
        for a in range(n):
            ha = ins[a].shape[0] // 2
            own = pl.ds(c * ha, ha)
            local.append(pltpu.make_async_copy(ins[a], outs[a].at[mine], local_sems.at[a]))
            local[-1].start()
            for j, chip in enumerate(chips):
                cp = _remote(ins[a].at[own], outs[a].at[mine, own], send_sems.at[6 * a + j], recv_sems.at[6 * a + j],
                             (*chip, c))
                cp.start()
                sent.append(cp)
        for a in range(n):
            ha = ins[a].shape[0] // 2
            own = pl.ds(c * ha, ha)
            for j, chip in enumerate(chips):
                landed = outs[a].at[2 * chip[0] + chip[1], own]
                _remote(landed, landed, send_sems.at[6 * a + j], recv_sems.at[6 * a + j], sibling).wait_recv()
                fwd = _remote(landed, landed, send_sems.at[6 * a + 3 + j], recv_sems.at[6 * a + 3 + j], sibling)
                fwd.start()
                sent.append(fwd)
        for a in range(n):
            ha = ins[a].shape[0] // 2
            other = pl.ds((1 - c) * ha, ha)
            for j, chip in enumerate(chips):
                passed = outs[a].at[2 * chip[0] + chip[1], other]
                _remote(passed, passed, send_sems.at[6 * a + 3 + j], recv_sems.at[6 * a + 3 + j], sibling).wait_recv()
        for cp in sent:
            cp.wait_send()
        for cp in local:
            cp.wait()

    return pl.pallas_call(
        body, in_specs=[_ANY] * n, out_specs=[_ANY] * n,
        out_shape=[jax.ShapeDtypeStruct((4,) + a.shape, a.dtype) for a in arrs],
        scratch_shapes=[pltpu.SemaphoreType.DMA((6 * n,)), pltpu.SemaphoreType.DMA((6 * n,)),
                        pltpu.SemaphoreType.DMA((n,))],
        name="gather_chips")(*arrs)


def pair_exchange(arrs):
    n = len(arrs)

    def body(*refs):
        ins, outs = refs[:n], refs[n:2 * n]
        send_sems, recv_sems = refs[2 * n:]
        x, y, c, _ = _place()
        copies = []
        for a in range(n):
            hr = ins[a].shape[1] // 2
            cp = _remote(ins[a].at[:, pl.ds((1 - c) * hr, hr)], outs[a], send_sems.at[a], recv_sems.at[a],
                         (x, y, 1 - c))
            cp.start()
            copies.append(cp)
        for cp in copies:
            cp.wait()

    return pl.pallas_call(
        body, in_specs=[_ANY] * n, out_specs=[_ANY] * n,
        out_shape=[jax.ShapeDtypeStruct((4, a.shape[1] // 2, a.shape[2]), a.dtype) for a in arrs],
        scratch_shapes=[pltpu.SemaphoreType.DMA((n,)), pltpu.SemaphoreType.DMA((n,))],
        name="pair_exchange")(*arrs)


def pair_add(g, r, core):
    _, R, C = g.shape
    hr = R // 2
    T = _pick(hr, 256) if hr % 8 == 0 else hr
    nb = hr // T

    def body(c_ref, g_ref, r_ref, o_ref):
        o_ref[...] = (g_ref[...] + r_ref[...]).astype(o_ref.dtype)

    return pl.pallas_call(
        body,
        grid_spec=pltpu.PrefetchScalarGridSpec(
            num_scalar_prefetch=1, grid=(4, nb),
            in_specs=[pl.BlockSpec((1, T, C), lambda k, i, c_ref: (k, c_ref[0] * nb + i, 0)),
                      pl.BlockSpec((1, T, C), lambda k, i, c_ref: (k, i, 0))],
            out_specs=pl.BlockSpec((1, T, C), lambda k, i, c_ref: (k, i, 0))),
        out_shape=jax.ShapeDtypeStruct((4, hr, C), F32), name="pair_add",
        compiler_params=_params("parallel", "parallel"))(core, g, r)


def chip_exchange(arrs):
    n = len(arrs)

    def body(*refs):
        ins, outs = refs[:n], refs[n:2 * n]
        send_sems, recv_sems = refs[2 * n:]
        x, y, c, chips = _place()
        copies = []
        for a in range(n):
            for j, chip in enumerate(chips):
                cp = _remote(ins[a].at[2 * chip[0] + chip[1]], outs[a].at[j], send_sems.at[3 * a + j],
                             recv_sems.at[3 * a + j], (*chip, c))
                cp.start()
                copies.append(cp)
        for cp in copies:
            cp.wait()

    return pl.pallas_call(
        body, in_specs=[_ANY] * n, out_specs=[_ANY] * n,
        out_shape=[jax.ShapeDtypeStruct((3,) + a.shape[1:], a.dtype) for a in arrs],
        scratch_shapes=[pltpu.SemaphoreType.DMA((3 * n,)), pltpu.SemaphoreType.DMA((3 * n,))],
        name="chip_exchange")(*arrs)


def chip_add(p, r, chip):
    _, R, C = p.shape
    T = _pick(R, 256) if R % 8 == 0 else R
    nb = R // T

    def body(k_ref, p_ref, r0_ref, r1_ref, r2_ref, o_ref):
        o_ref[...] = ((p_ref[0].astype(F32) + r0_ref[0].astype(F32)) + r1_ref[0].astype(F32)) + r2_ref[0].astype(F32)

    slot = lambda j: pl.BlockSpec((1, T, C), lambda i, k_ref: (j, i, 0))
    return pl.pallas_call(
        body,
        grid_spec=pltpu.PrefetchScalarGridSpec(
            num_scalar_prefetch=1, grid=(nb,),
            in_specs=[pl.BlockSpec((1, T, C), lambda i, k_ref: (k_ref[0], i, 0)), slot(0), slot(1), slot(2)],
            out_specs=pl.BlockSpec((T, C), lambda i, k_ref: (i, 0))),
        out_shape=jax.ShapeDtypeStruct((R, C), F32), name="chip_add",
        compiler_params=_params("parallel"))(chip, p, r, r, r)


def pair_gather(arrs):
    n = len(arrs)

    def body(*refs):
        ins, outs = refs[:n], refs[n:2 * n]
        send_sems, recv_sems, local_sems = refs[2 * n:]
        x, y, c, _ = _place()
        copies = []
        for a in range(n):
            hr = ins[a].shape[0]
            mine = outs[a].at[pl.ds(c * hr, hr)]
            local = pltpu.make_async_copy(ins[a], mine, local_sems.at[a])
            local.start()
            cp = _remote(ins[a], mine, send_sems.at[a], recv_sems.at[a], (x, y, 1 - c))
            cp.start()
            copies += [local, cp]
        for cp in copies:
            cp.wait()

    return pl.pallas_call(
        body, in_specs=[_ANY] * n, out_specs=[_ANY] * n,
        out_shape=[jax.ShapeDtypeStruct((2 * a.shape[0], a.shape[1]), a.dtype) for a in arrs],
        scratch_shapes=[pltpu.SemaphoreType.DMA((n,)), pltpu.SemaphoreType.DMA((n,)), pltpu.SemaphoreType.DMA((n,))],
        name="pair_gather")(*arrs)


def allreduce_small(v):
    R = v.shape[0]

    def body(v_ref, o_ref, buf, send_sems, recv_sems, local_sem):
        x, y, c, chips = _place()
        me, sibling = (x, y, c), (x, y, 1 - c)

        def rows(px, py, pc):
            return buf.at[pl.ds((4 * px + 2 * py + pc) * R, R), :]

        def copy(k, block, to, src=None):
            return _remote(rows(*block) if src is None else src, rows(*block), send_sems.at[k], recv_sems.at[k], to)

        mine = pltpu.make_async_copy(v_ref, rows(*me), local_sem)
        mine.start()
        first = [copy(0, me, sibling, src=v_ref)]
        first += [copy(1 + j, me, (*chip, c), src=v_ref) for j, chip in enumerate(chips)]
        for cp in first:
            cp.start()
        passed = [copy(4 + j, (*chip, c), sibling) for j, chip in enumerate(chips)]
        for j, chip in enumerate(chips):
            copy(1 + j, (*chip, c), me).wait_recv()
            passed[j].start()
        copy(0, sibling, me).wait_recv()
        for j, chip in enumerate(chips):
            copy(4 + j, (*chip, 1 - c), me).wait_recv()
        for cp in first + passed:
            cp.wait_send()
        mine.wait()
        acc = buf[0:R, :]
        for d in range(1, 8):
            acc = acc + buf[d * R:(d + 1) * R, :]
        o_ref[...] = acc

    return pl.pallas_call(
        body, in_specs=[pl.BlockSpec(memory_space=pltpu.VMEM)], out_specs=pl.BlockSpec(memory_space=pltpu.VMEM),
        out_shape=jax.ShapeDtypeStruct((R, LANES), F32),
        scratch_shapes=[pltpu.VMEM((8 * R, LANES), F32), pltpu.SemaphoreType.DMA((7,)),
                        pltpu.SemaphoreType.DMA((7,)), pltpu.SemaphoreType.DMA],
        name="allreduce_small", compiler_params=pltpu.CompilerParams(vmem_limit_bytes=VMEM_LIMIT))(v)


def reduce_scatter(grads, core, chip):
    recv = pair_exchange(grads)
    pair = [pair_add(g, r, core) for g, r in zip(grads, recv)]
    recv = chip_exchange(pair)
    mine = [chip_add(p, r, chip) for p, r in zip(pair, recv)]
    return pair_gather(mine)


SMALL = ("pre_norm_w", "pool_scale", "conv_w", "conv_b", "dt_bias", "a_log", "d_skip", "ssd_norm_w", "post_norm_w")


def _pack(parts):
    flat = jnp.concatenate([p.reshape(-1) for p in parts])
    n = flat.shape[0]
    rows = -(-n // (LANES * LANES)) * LANES
    return jnp.pad(flat, (0, rows * LANES - n)).reshape(rows, LANES)


def _unpack(packed, shapes):
    flat, out, at = packed.reshape(-1), [], 0
    for s in shapes:
        n = math.prod(s)
        out.append(flat[at:at + n].reshape(s))
        at += n
    return out


def kernel(x, pre_norm_w, w_in, pool_mix_w, pool_scale, conv_w, conv_b, dt_bias, a_log, d_skip, ssd_norm_w, w_out, post_norm_w, loss_target, m_pre_norm_w, m_w_in, m_pool_mix_w, m_pool_scale, m_conv_w, m_conv_b, m_dt_bias, m_a_log, m_d_skip, m_ssd_norm_w, m_w_out, m_post_norm_w, v_pre_norm_w, v_w_in, v_pool_mix_w, v_pool_scale, v_conv_w, v_conv_b, v_dt_bias, v_a_log, v_d_skip, v_ssd_norm_w, v_w_out, v_post_norm_w):
    NL, D, SH = w_in.shape
    L = x.shape[1]
    PGW = D // GROUPS
    CS = conv_w.shape[2]
    C = 4 * CS
    core = lax.axis_index("c").astype(jnp.int32).reshape(1)
    chip = (2 * lax.axis_index("x") + lax.axis_index("y")).astype(jnp.int32)

    g_in, g_out, g_mix, g_cw = gather_chips([
        w_in.astype(BF16), w_out.astype(BF16), pool_mix_w.astype(BF16).reshape(NL, GROUPS * (PGW // 4), PGW),
        conv_w.reshape(NL, CONV_K * CS // LANES, LANES)])
    params = []
    for l in range(NL):
        w_in_l = jnp.concatenate([g_in[k, l] for k in range(4)], axis=1)
        mix_l = g_mix[:, l].reshape(4, GROUPS, PGW // 4, PGW).transpose(1, 0, 2, 3).reshape(GROUPS, PGW, PGW)
        cw_l = g_cw[:, l].reshape(4, CONV_K, CS).transpose(1, 0, 2).reshape(CONV_K, C)
        params.append(layer_params(pre_norm_w[l], w_in_l, mix_l, pool_scale[l], cw_l, conv_b[l], dt_bias[l], a_log[l],
                                   d_skip[l], ssd_norm_w[l], g_out[:, l].reshape(2 * D, D), post_norm_w[l], D))

    sumsq, grad_x, grads = local_step(x[0], loss_target[0], params, D)

    big = [
        jnp.stack([g["w_in"].reshape(D, 4, SH).transpose(1, 0, 2) for g in grads], axis=1).reshape(4, NL * D, SH),
        jnp.stack([g["w_out"].reshape(4, 2 * D // 4, D) for g in grads], axis=1).reshape(4, NL * 2 * D // 4, D),
        jnp.stack([g["pool_mix_w"].reshape(GROUPS, 4, PGW // 4, PGW).transpose(1, 0, 2, 3) for g in grads],
                  axis=1).reshape(4, NL * GROUPS * (PGW // 4), PGW)]
    r_in, r_out, r_mix = reduce_scatter(big, core, chip.reshape(1))

    small_shapes = [(NL,) + grads[0][n].shape for n in SMALL]
    packed = _pack([0.5 / D * sumsq[0, :1]] + [jnp.stack([g[n] for g in grads]) for n in SMALL])
    total = allreduce_small(packed)
    loss, *small = _unpack(total, [(1,)] + small_shapes)
    small = dict(zip(SMALL, small))
    small["conv_w"] = lax.dynamic_slice_in_dim(small["conv_w"], chip * CS, CS, axis=2)

    given = dict(pre_norm_w=(pre_norm_w, m_pre_norm_w, v_pre_norm_w), pool_scale=(pool_scale, m_pool_scale, v_pool_scale),
                 conv_w=(conv_w, m_conv_w, v_conv_w), conv_b=(conv_b, m_conv_b, v_conv_b),
                 dt_bias=(dt_bias, m_dt_bias, v_dt_bias), a_log=(a_log, m_a_log, v_a_log),
                 d_skip=(d_skip, m_d_skip, v_d_skip), ssd_norm_w=(ssd_norm_w, m_ssd_norm_w, v_ssd_norm_w),
                 post_norm_w=(post_norm_w, m_post_norm_w, v_post_norm_w))
    shapes = [given[n][0].shape for n in SMALL]
    upd = adamw(_pack([small[n] for n in SMALL]), *[_pack([given[n][i] for n in SMALL]) for i in range(3)],
                "adamw_small")
    upd = [dict(zip(SMALL, _unpack(u, shapes))) for u in upd]

    out = {n: (small[n], upd[0][n], upd[1][n], upd[2][n]) for n in SMALL}
    for n, r, w, m, v in (("w_in", r_in, w_in, m_w_in, v_w_in), ("w_out", r_out, w_out, m_w_out, v_w_out),
                          ("pool_mix_w", r_mix, pool_mix_w, m_pool_mix_w, v_pool_mix_w)):
        flat = lambda t: t.reshape(r.shape)
        out[n] = tuple(t.reshape(w.shape) for t in (r,) + tuple(adamw(r, flat(w), flat(m), flat(v), "adamw_" + n)))

    order = ("pre_norm_w", "w_in", "pool_mix_w", "pool_scale", "conv_w", "conv_b", "dt_bias", "a_log", "d_skip",
             "ssd_norm_w", "w_out", "post_norm_w")
    return (loss.reshape(()), grad_x[None], *[out[n][0] for n in order], *[out[n][1] for n in order],
            *[out[n][2] for n in order], *[out[n][3] for n in order])
```

```python
import functools
import math

import jax
import jax.numpy as jnp
from jax import lax
from jax.experimental import pallas as pl
from jax.experimental.pallas import tpu as pltpu

F32 = jnp.float32
BF16 = jnp.bfloat16
HIGHEST = lax.Precision.HIGHEST

NORM_EPS = 1e-6
HEAD_DIM = 64
STATE = 128
GROUPS = 4
POOL_WINDOWS = (2, 4, 8, 16)
POOL_HALO = 16
CONV_K = 4
CONV_HALO = 8
SCAN_CHUNK = 256
LANES = 128
VMEM_LIMIT = 52 * 1024 * 1024

ADAM_LR = 0.001
ADAM_B1 = 0.9
ADAM_B2 = 0.999
ADAM_EPS = 1e-08
ADAM_WD = 0.01
ADAM_STEP = 10

MESH = pl.DeviceIdType.MESH

NN = (((1,), (0,)), ((), ()))
NT = (((1,), (1,)), ((), ()))
TN = (((0,), (0,)), ((), ()))


def _params(*sem):
    return pltpu.CompilerParams(dimension_semantics=sem, vmem_limit_bytes=VMEM_LIMIT)


def _pick(dim, pref):
    if dim <= pref:
        return dim
    t = (pref // LANES) * LANES
    while t > LANES and dim % t:
        t -= LANES
    assert dim % t == 0, (dim, pref)
    return t


def _dot(a, b, dn=NN):
    return lax.dot_general(a, b, dn, preferred_element_type=F32)


def _dot_exact(a, b):
    return jnp.dot(a, b, precision=HIGHEST, preferred_element_type=F32)


def _sigmoid(v):
    return 1.0 / (1.0 + jnp.exp(-v))


def _silu_and_grad(v):
    s = _sigmoid(v)
    return v * s, s * (1.0 + v * (1.0 - s))


def _row(i, shape):
    return lax.broadcasted_iota(jnp.int32, shape, 0) + i


def matmul(a, b, mode, out_dtype, name, tm=512, tn=1024, tk=2048):
    if mode == "nn":
        (M, K), (K2, N) = a.shape, b.shape
    elif mode == "nt":
        (M, K), (N, K2) = a.shape, b.shape
    else:
        (K, M), (K2, N) = a.shape, b.shape
    assert K == K2
    tm, tn, tk = _pick(M, tm), _pick(N, tn), _pick(K, tk)
    nk = K // tk
    dn = {"nn": NN, "nt": NT, "tn": TN}[mode]

    def body(a_ref, b_ref, o_ref, acc_ref):
        k = pl.program_id(2)
        part = _dot(a_ref[...].astype(BF16), b_ref[...].astype(BF16), dn)

        @pl.when(k == 0)
        def _():
            acc_ref[...] = part

        @pl.when(k > 0)
        def _():
            acc_ref[...] += part

        @pl.when(k == nk - 1)
        def _():
            o_ref[...] = acc_ref[...].astype(o_ref.dtype)

    a_spec = (pl.BlockSpec((tk, tm), lambda i, j, k: (k, i)) if mode == "tn"
              else pl.BlockSpec((tm, tk), lambda i, j, k: (i, k)))
    b_spec = (pl.BlockSpec((tn, tk), lambda i, j, k: (j, k)) if mode == "nt"
              else pl.BlockSpec((tk, tn), lambda i, j, k: (k, j)))
    return pl.pallas_call(
        body, grid=(M // tm, N // tn, nk), in_specs=[a_spec, b_spec],
        out_specs=pl.BlockSpec((tm, tn), lambda i, j, k: (i, j)),
        out_shape=jax.ShapeDtypeStruct((M, N), out_dtype),
        scratch_shapes=[pltpu.VMEM((tm, tn), F32)], name=name,
        compiler_params=_params("parallel", "parallel", "arbitrary"))(a, b)


def rms_fwd(x, w):
    L, D = x.shape
    T = _pick(L, 256)

    def body(x_ref, w_ref, h_ref):
        xv = x_ref[...]
        r = lax.rsqrt(jnp.mean(xv * xv, axis=-1, keepdims=True) + NORM_EPS)
        h_ref[...] = (xv * r * w_ref[...]).astype(h_ref.dtype)

    return pl.pallas_call(
        body, grid=(L // T,),
        in_specs=[pl.BlockSpec((T, D), lambda i: (i, 0)), pl.BlockSpec((1, D), lambda i: (0, 0))],
        out_specs=pl.BlockSpec((T, D), lambda i: (i, 0)),
        out_shape=jax.ShapeDtypeStruct((L, D), BF16), name="rms_fwd",
        compiler_params=_params("parallel"))(x, w)


def post_fwd(x, o, w):
    L, D = x.shape
    T = _pick(L, 256)

    def body(x_ref, o_ref, w_ref, y_ref):
        ov = o_ref[...]
        r = lax.rsqrt(jnp.mean(ov * ov, axis=-1, keepdims=True) + NORM_EPS)
        y_ref[...] = x_ref[...] + ov * r * w_ref[...]

    row = pl.BlockSpec((T, D), lambda i: (i, 0))
    return pl.pallas_call(
        body, grid=(L // T,), in_specs=[row, row, pl.BlockSpec((1, D), lambda i: (0, 0))],
        out_specs=row, out_shape=jax.ShapeDtypeStruct((L, D), F32), name="post_fwd",
        compiler_params=_params("parallel"))(x, o, w)


def _rms_bwd_math(xv, w, dy):
    r = lax.rsqrt(jnp.mean(xv * xv, axis=-1, keepdims=True) + NORM_EPS)
    xhat = xv * r
    g = dy * w
    dx = r * (g - xhat * jnp.mean(g * xhat, axis=-1, keepdims=True))
    return dx, jnp.sum(dy * xhat, axis=0, keepdims=True)


def post_bwd(o, w, dxn):
    L, D = o.shape
    T = _pick(L, 256)

    def body(o_ref, w_ref, d_ref, do_ref, dw_ref):
        dx, dw = _rms_bwd_math(o_ref[...], w_ref[...], d_ref[...])
        do_ref[...] = dx.astype(do_ref.dtype)

        @pl.when(pl.program_id(0) == 0)
        def _():
            dw_ref[...] = jnp.zeros_like(dw_ref)

        dw_ref[...] += dw

    row = pl.BlockSpec((T, D), lambda i: (i, 0))
    vec = pl.BlockSpec((1, D), lambda i: (0, 0))
    return pl.pallas_call(
        body, grid=(L // T,), in_specs=[row, vec, row], out_specs=[row, vec],
        out_shape=[jax.ShapeDtypeStruct((L, D), BF16), jax.ShapeDtypeStruct((1, D), F32)],
        name="post_bwd", compiler_params=_params("arbitrary"))(o, w, dxn)


def rms_bwd(x, w, dh_a, dh_b, dxn):
    L, D = x.shape
    T = _pick(L, 256)

    def body(x_ref, w_ref, a_ref, b_ref, d_ref, dx_ref, dw_ref):
        dx, dw = _rms_bwd_math(x_ref[...], w_ref[...], a_ref[...] + b_ref[...])
        dx_ref[...] = d_ref[...] + dx

        @pl.when(pl.program_id(0) == 0)
        def _():
            dw_ref[...] = jnp.zeros_like(dw_ref)

        dw_ref[...] += dw

    row = pl.BlockSpec((T, D), lambda i: (i, 0))
    vec = pl.BlockSpec((1, D), lambda i: (0, 0))
    return pl.pallas_call(
        body, grid=(L // T,), in_specs=[row, vec, row, row, row], out_specs=[row, vec],
        out_shape=[jax.ShapeDtypeStruct((L, D), F32), jax.ShapeDtypeStruct((1, D), F32)],
        name="rms_bwd", compiler_params=_params("arbitrary"))(x, w, dh_a, dh_b, dxn)


def loss_head(y, target):
    L, D = y.shape
    T = _pick(L, 256)

    def body(y_ref, t_ref, d_ref, s_ref):
        e = y_ref[...] - t_ref[...]
        d_ref[...] = e * (1.0 / D)

        @pl.when(pl.program_id(0) == 0)
        def _():
            s_ref[...] = jnp.zeros_like(s_ref)

        s_ref[...] += jnp.sum(e * e)

    row = pl.BlockSpec((T, D), lambda i: (i, 0))
    return pl.pallas_call(
        body, grid=(L // T,), in_specs=[row, row],
        out_specs=[row, pl.BlockSpec((8, LANES), lambda i: (0, 0))],
        out_shape=[jax.ShapeDtypeStruct((L, D), F32), jax.ShapeDtypeStruct((8, LANES), F32)],
        name="loss_head", compiler_params=_params("arbitrary"))(y, target)


def _window_sums(xe, w, back):
    n = xe.shape[0]
    s, k = xe, 1
    while k < w:
        s = s + pltpu.roll(s, k if back else n - k, 0)
        k *= 2
    return s


def pool_fwd(proj, mixw, scale, D):
    L = proj.shape[0]
    PGW = D // GROUPS
    T = _pick(L, 256)
    hb = T // POOL_HALO

    def body(u_ref, halo_ref, g_ref, mw_ref, sc_ref, y_ref, p_ref):
        i = pl.program_id(0)
        u = u_ref[...]
        halo = jnp.where(i > 0, halo_ref[...], 0.0)
        xe = jnp.concatenate([halo, u], axis=0)
        t1 = _row(i * T + 1, (T, 1))
        for g, w in enumerate(POOL_WINDOWS):
            sl = slice(g * PGW, (g + 1) * PGW)
            win = _window_sums(xe[:, sl], w, True)[POOL_HALO:, :]
            cnt = jnp.minimum(t1, w).astype(F32)
            pooled = (win / cnt - u[:, sl]).astype(BF16)
            p_ref[:, sl] = pooled
            mixed = _dot(pooled, mw_ref[g])
            gate = g_ref[:, sl]
            y_ref[:, sl] = (mixed * sc_ref[:, sl] * (gate * _sigmoid(gate))).astype(BF16)

    return pl.pallas_call(
        body, grid=(L // T,),
        in_specs=[pl.BlockSpec((T, D), lambda i: (i, 0)),
                  pl.BlockSpec((POOL_HALO, D), lambda i: (jnp.maximum(i * hb - 1, 0), 0)),
                  pl.BlockSpec((T, D), lambda i: (i, 1)),
                  pl.BlockSpec((GROUPS, PGW, PGW), lambda i: (0, 0, 0)),
                  pl.BlockSpec((1, D), lambda i: (0, 0))],
        out_specs=[pl.BlockSpec((T, D), lambda i: (i, 0)), pl.BlockSpec((T, D), lambda i: (i, 0))],
        out_shape=[jax.ShapeDtypeStruct((L, D), BF16), jax.ShapeDtypeStruct((L, D), BF16)],
        name="pool_fwd", compiler_params=_params("parallel"))(proj, proj, proj, mixw, scale)


def pool_bwd(proj, dmix, pooled, mixw, scale, D):
    L = proj.shape[0]
    PGW = D // GROUPS
    T = _pick(L, 256)
    hb = T // POOL_HALO
    nT = L // T

    def body(g_ref, gh_ref, dy_ref, dyh_ref, p_ref, mw_ref, sc_ref, du_ref, dg_ref, dm_ref, ds_ref):
        i = pl.program_id(0)
        t1 = _row(i * T + 1, (T, 1))
        th1 = _row((i + 1) * T + 1, (POOL_HALO, 1))
        live = i < nT - 1

        @pl.when(i == 0)
        def _():
            ds_ref[...] = jnp.zeros_like(ds_ref)

        for g, w in enumerate(POOL_WINDOWS):
            sl = slice(g * PGW, (g + 1) * PGW)
            sc = sc_ref[:, sl]
            gate, dy = g_ref[:, sl], dy_ref[:, sl]
            sg, dsg = _silu_and_grad(gate)
            mixed = _dot(p_ref[:, sl], mw_ref[g])
            dmixed = (dy * sc * sg).astype(BF16)
            dm_ref[:, sl] = dmixed
            dg_ref[:, sl] = (dy * mixed * sc * dsg).astype(BF16)
            ds_ref[:, sl] += jnp.sum(dy * mixed * sg, axis=0, keepdims=True)
            dpool = _dot(dmixed, mw_ref[g], NT)
            gate_h = gh_ref[:, sl]
            dmixed_h = (dyh_ref[:, sl] * sc * (gate_h * _sigmoid(gate_h))).astype(BF16)
            dpool_h = jnp.where(live, _dot(dmixed_h, mw_ref[g], NT), 0.0)
            q = dpool / jnp.minimum(t1, w).astype(F32)
            q_h = dpool_h / jnp.minimum(th1, w).astype(F32)
            qe = jnp.concatenate([q, q_h], axis=0)
            du_ref[:, sl] = (_window_sums(qe, w, False)[:T, :] - dpool).astype(BF16)

    nxt = lambda i: (jnp.minimum((i + 1) * hb, L // POOL_HALO - 1), 0)
    row = lambda c: pl.BlockSpec((T, D), lambda i: (i, c))
    return pl.pallas_call(
        body, grid=(nT,),
        in_specs=[row(1),
                  pl.BlockSpec((POOL_HALO, D), lambda i: (nxt(i)[0], 1)),
                  row(0),
                  pl.BlockSpec((POOL_HALO, D), nxt),
                  row(0),
                  pl.BlockSpec((GROUPS, PGW, PGW), lambda i: (0, 0, 0)),
                  pl.BlockSpec((1, D), lambda i: (0, 0))],
        out_specs=[row(0), row(0), row(0), pl.BlockSpec((1, D), lambda i: (0, 0))],
        out_shape=[jax.ShapeDtypeStruct((L, D), BF16)] * 3 + [jax.ShapeDtypeStruct((1, D), F32)],
        name="pool_bwd", compiler_params=_params("arbitrary"))(proj, proj, dmix, dmix, pooled, mixw, scale)


def pool_dw(pooled, dmixed, D):
    L = pooled.shape[0]
    PGW = D // GROUPS
    tk = _pick(L, 1024)
    nk = L // tk

    def body(p_ref, d_ref, o_ref):
        @pl.when(pl.program_id(1) == 0)
        def _():
            o_ref[...] = jnp.zeros_like(o_ref)

        o_ref[0] += _dot(p_ref[...], d_ref[...], TN)

    blk = pl.BlockSpec((tk, PGW), lambda g, k: (k, g))
    return pl.pallas_call(
        body, grid=(GROUPS, nk), in_specs=[blk, blk],
        out_specs=pl.BlockSpec((1, PGW, PGW), lambda g, k: (g, 0, 0)),
        out_shape=jax.ShapeDtypeStruct((GROUPS, PGW, PGW), F32), name="pool_dw",
        compiler_params=_params("parallel", "arbitrary"))(pooled, dmixed)


def conv_fwd(proj, cw, cb, D):
    L = proj.shape[0]
    C = cw.shape[1]
    assert (3 * D) % C == 0
    cblk = (3 * D) // C
    T = _pick(L, 256)
    hb = T // CONV_HALO

    def body(u_ref, halo_ref, w_ref, b_ref, o_ref):
        i = pl.program_id(0)
        u = u_ref[...]
        xe = jnp.concatenate([jnp.where(i > 0, halo_ref[...], 0.0), u], axis=0)
        acc = b_ref[...] + w_ref[CONV_K - 1:CONV_K, :] * u
        for k in range(CONV_K - 1):
            acc = acc + w_ref[k:k + 1, :] * pltpu.roll(xe, CONV_K - 1 - k, 0)[CONV_HALO:, :]
        o_ref[...] = acc

    return pl.pallas_call(
        body, grid=(L // T,),
        in_specs=[pl.BlockSpec((T, C), lambda i: (i, cblk)),
                  pl.BlockSpec((CONV_HALO, C), lambda i: (jnp.maximum(i * hb - 1, 0), cblk)),
                  pl.BlockSpec((CONV_K, C), lambda i: (0, 0)),
                  pl.BlockSpec((1, C), lambda i: (0, 0))],
        out_specs=pl.BlockSpec((T, C), lambda i: (i, 0)),
        out_shape=jax.ShapeDtypeStruct((L, C), F32), name="conv_fwd",
        compiler_params=_params("parallel"))(proj, proj, cw, cb)


def conv_bwd(dpre, proj, cw, D):
    L, C = dpre.shape
    cblk = (3 * D) // C
    T = _pick(L, 256)
    hb = T // CONV_HALO
    nT = L // T

    def body(d_ref, dn_ref, u_ref, up_ref, w_ref, dr_ref, dw_ref, db_ref):
        i = pl.program_id(0)
        d = d_ref[...]
        de = jnp.concatenate([d, jnp.where(i < nT - 1, dn_ref[...], 0.0)], axis=0)
        ue = jnp.concatenate([jnp.where(i > 0, up_ref[...], 0.0), u_ref[...]], axis=0)

        @pl.when(i == 0)
        def _():
            dw_ref[...] = jnp.zeros_like(dw_ref)
            db_ref[...] = jnp.zeros_like(db_ref)

        acc = w_ref[CONV_K - 1:CONV_K, :] * d
        dw_ref[CONV_K - 1:CONV_K, :] += jnp.sum(d * u_ref[...], axis=0, keepdims=True)
        for k in range(CONV_K - 1):
            sh = CONV_K - 1 - k
            acc = acc + w_ref[k:k + 1, :] * pltpu.roll(de, T + CONV_HALO - sh, 0)[:T, :]
            dw_ref[k:k + 1, :] += jnp.sum(d * pltpu.roll(ue, sh, 0)[CONV_HALO:, :], axis=0, keepdims=True)
        dr_ref[...] = acc.astype(dr_ref.dtype)
        db_ref[...] += jnp.sum(d, axis=0, keepdims=True)

    return pl.pallas_call(
        body, grid=(nT,),
        in_specs=[pl.BlockSpec((T, C), lambda i: (i, 0)),
                  pl.BlockSpec((CONV_HALO, C), lambda i: (jnp.minimum((i + 1) * hb, L // CONV_HALO - 1), 0)),
                  pl.BlockSpec((T, C), lambda i: (i, cblk)),
                  pl.BlockSpec((CONV_HALO, C), lambda i: (jnp.maximum(i * hb - 1, 0), cblk)),
                  pl.BlockSpec((CONV_K, C), lambda i: (0, 0))],
        out_specs=[pl.BlockSpec((T, C), lambda i: (i, 0)),
                   pl.BlockSpec((8, C), lambda i: (0, 0)),
                   pl.BlockSpec((1, C), lambda i: (0, 0))],
        out_shape=[jax.ShapeDtypeStruct((L, C), BF16), jax.ShapeDtypeStruct((8, C), F32),
                   jax.ShapeDtypeStruct((1, C), F32)],
        name="conv_bwd", compiler_params=_params("arbitrary"))(dpre, dpre, proj, proj, cw)


def _softplus(v):
    y = jnp.exp(-jnp.abs(v))
    u = 1.0 + y
    log1p = jnp.where(u == 1.0, y, jnp.log(u) * y / jnp.where(u == 1.0, 1.0, u - 1.0))
    return jnp.maximum(v, 0.0) + log1p


def dt_prep(dtraw, bias, alog, expand, D):
    L = dtraw.shape[0]
    GC = D // GROUPS
    HPG = GC // HEAD_DIM
    Q = _pick(L, SCAN_CHUNK)
    nc = L // Q

    def body(r_ref, b_ref, a_ref, e_ref, dt_ref, acs_ref, acst_ref, dtx_ref, eax_ref, dsx_ref, cdx_ref):
        valid = lax.broadcasted_iota(jnp.int32, (1, LANES), 1) < HPG
        dt = jnp.where(valid, _softplus(r_ref[...] + b_ref[...]), 0.0)
        adt = dt * -jnp.exp(a_ref[...])
        tril = (_row(0, (Q, Q)) >= lax.broadcasted_iota(jnp.int32, (Q, Q), 1)).astype(F32)
        acs = _dot_exact(tril, adt)
        last = acs[Q - 1:Q, :]
        dt_ref[...] = dt
        acs_ref[...] = acs
        acst_ref[...] = acs.T
        e = e_ref[...]
        dtx_ref[...] = _dot_exact(dt, e)
        eax_ref[...] = jnp.exp(_dot_exact(acs, e))
        dsx_ref[...] = jnp.exp(_dot_exact(last - acs, e))
        cdx_ref[0] = jnp.exp(_dot_exact(jnp.broadcast_to(last, (8, LANES)), e))

    head = pl.BlockSpec((Q, LANES), lambda g, c: (c, g))
    hvec = pl.BlockSpec((1, LANES), lambda g, c: (0, g))
    chan = pl.BlockSpec((Q, GC), lambda g, c: (c, g))
    return pl.pallas_call(
        body, grid=(GROUPS, nc),
        in_specs=[head, hvec, hvec, pl.BlockSpec((LANES, GC), lambda g, c: (0, 0))],
        out_specs=[head, head, pl.BlockSpec((LANES, Q), lambda g, c: (g, c)), chan, chan, chan,
                   pl.BlockSpec((1, 8, GC), lambda g, c: (c, 0, g))],
        out_shape=[jax.ShapeDtypeStruct((L, GROUPS * LANES), F32),
                   jax.ShapeDtypeStruct((L, GROUPS * LANES), F32),
                   jax.ShapeDtypeStruct((GROUPS * LANES, L), F32),
                   jax.ShapeDtypeStruct((L, D), F32), jax.ShapeDtypeStruct((L, D), F32),
                   jax.ShapeDtypeStruct((L, D), F32), jax.ShapeDtypeStruct((nc, 8, D), F32)],
        name="dt_prep", compiler_params=_params("parallel", "parallel"))(dtraw, bias, alog, expand)


def _scan_specs(L, D, Q, rev):
    GC = D // GROUPS
    nc = L // Q
    ci = (lambda c: nc - 1 - c) if rev else (lambda c: c)
    return dict(
        xs=pl.BlockSpec((Q, GC), lambda g, c: (ci(c), g)),
        b=pl.BlockSpec((Q, STATE), lambda g, c: (ci(c), D // STATE + g)),
        c=pl.BlockSpec((Q, STATE), lambda g, c: (ci(c), D // STATE + GROUPS + g)),
        chan=pl.BlockSpec((Q, GC), lambda g, c: (ci(c), g)),
        cdx=pl.BlockSpec((1, 8, GC), lambda g, c: (ci(c), 0, g)),
        head=pl.BlockSpec((Q, LANES), lambda g, c: (ci(c), g)),
        headt=pl.BlockSpec((LANES, Q), lambda g, c: (g, ci(c))),
        state=pl.BlockSpec((1, 1, STATE, GC), lambda g, c: (ci(c), g, 0, 0)),
        hvec=pl.BlockSpec((1, LANES), lambda g, c: (0, g)),
        cvec=pl.BlockSpec((1, GC), lambda g, c: (0, g)))


def scan_fwd(pre, dtx, eax, dsx, cdx, acs, acst, D):
    L = pre.shape[0]
    GC = D // GROUPS
    Q = _pick(L, SCAN_CHUNK)
    nc = L // Q
    sp = _scan_specs(L, D, Q, False)

    def body(xs_ref, b_ref, c_ref, dtx_ref, eax_ref, dsx_ref, cdx_ref, acs_ref, acst_ref, y_ref, st_ref, s_scr):
        @pl.when(pl.program_id(1) == 0)
        def _():
            s_scr[...] = jnp.zeros_like(s_scr)

        tri = _row(0, (Q, Q)) >= lax.broadcasted_iota(jnp.int32, (Q, Q), 1)
        half = lax.broadcasted_iota(jnp.int32, (1, LANES), 1) // HEAD_DIM
        xs, _ = _silu_and_grad(xs_ref[...])
        bg = _silu_and_grad(b_ref[...])[0].astype(BF16)
        cg = _silu_and_grad(c_ref[...])[0].astype(BF16)
        xdt = xs * dtx_ref[...]
        sprev = s_scr[...]
        st_ref[0, 0] = sprev
        sc = _dot(cg, bg, NT)
        yoff = _dot(cg, sprev.astype(BF16)) * eax_ref[...]
        for j in range(GC // LANES):
            ps = slice(j * LANES, (j + 1) * LANES)
            xp = xdt[:, ps]
            acc = yoff[:, ps]
            for hh in range(2):
                h = 2 * j + hh
                lm = jnp.exp(jnp.where(tri, acs_ref[:, h:h + 1] - acst_ref[h:h + 1, :], -1e30))
                xm = jnp.where(half == hh, xp, 0.0).astype(BF16)
                acc = acc + _dot((sc * lm).astype(BF16), xm)
            y_ref[:, ps] = acc
        xw = (xdt * dsx_ref[...]).astype(BF16)
        s_scr[...] = cdx_ref[0, 0:1, :] * sprev + _dot(bg, xw, TN)

    return pl.pallas_call(
        body, grid=(GROUPS, nc),
        in_specs=[sp["xs"], sp["b"], sp["c"], sp["chan"], sp["chan"], sp["chan"], sp["cdx"], sp["head"], sp["headt"]],
        out_specs=[sp["chan"], sp["state"]],
        out_shape=[jax.ShapeDtypeStruct((L, D), F32), jax.ShapeDtypeStruct((nc, GROUPS, STATE, GC), F32)],
        scratch_shapes=[pltpu.VMEM((STATE, GC), F32)], name="scan_fwd",
        compiler_params=_params("parallel", "arbitrary"))(pre, pre, pre, dtx, eax, dsx, cdx, acs, acst)


def scan_bwd(pre, dtx, eax, dsx, cdx, acs, acst, dt, dtraw, bias, alog, states, dy, dexp, collapse, D):
    L, C = pre.shape
    GC = D // GROUPS
    Q = _pick(L, SCAN_CHUNK)
    nc = L // Q
    sp = _scan_specs(L, D, Q, True)
    rc = lambda c: nc - 1 - c

    def body(xs_ref, b_ref, c_ref, dtx_ref, eax_ref, dsx_ref, cdx_ref, acs_ref, acst_ref, dt_ref, raw_ref,
             bias_ref, alog_ref, st_ref, dy_ref, dexp_ref, col_ref,
             dxs_ref, db_ref, dc_ref, ddt_ref, dal_ref, dbi_ref, ds_scr, dx_scr):
        first = pl.program_id(1) == 0

        @pl.when(first)
        def _():
            ds_scr[...] = jnp.zeros_like(ds_scr)
            dal_ref[...] = jnp.zeros_like(dal_ref)
            dbi_ref[...] = jnp.zeros_like(dbi_ref)

        li = _row(0, (Q, Q))
        si = lax.broadcasted_iota(jnp.int32, (Q, Q), 1)
        lane = lax.broadcasted_iota(jnp.int32, (1, LANES), 1)
        half = lane // HEAD_DIM
        xs_pre, b_pre, c_pre = xs_ref[...], b_ref[...], c_ref[...]
        xs, xs_g = _silu_and_grad(xs_pre)
        bf, b_g = _silu_and_grad(b_pre)
        cf, c_g = _silu_and_grad(c_pre)
        bg, cg = bf.astype(BF16), cf.astype(BF16)
        dtx, eax, dsx = dtx_ref[...], eax_ref[...], dsx_ref[...]
        cd = cdx_ref[0, 0:1, :]
        xdt = xs * dtx
        G = dy_ref[...]
        prev = st_ref[0, 0]
        dsn = ds_scr[...]
        prev_b, dsn_b = prev.astype(BF16), dsn.astype(BF16)
        cp = _dot(cg, prev_b)
        ge_b = (G * eax).astype(BF16)
        d_c = _dot(ge_b, prev_b, NT)
        dprev = _dot(cg, ge_b, TN) + cd * dsn
        chan_a = G * cp * eax
        xw_b = (xdt * dsx).astype(BF16)
        dcd = jnp.sum(prev * dsn, axis=0, keepdims=True)
        d_b = _dot(xw_b, dsn_b, NT)
        dxw = _dot(bg, dsn_b)
        dd = dxw * xdt * dsx
        chan_a = chan_a - dd
        last_c = jnp.sum(dd, axis=0, keepdims=True) + dcd * cd
        sc = _dot(cg, bg, NT)
        sct = _dot(bg, cg, NT)
        dsc = jnp.zeros((Q, Q), F32)
        dsct = jnp.zeros((Q, Q), F32)
        dacs = jnp.zeros((Q, LANES), F32)
        for j in range(GC // LANES):
            ps = slice(j * LANES, (j + 1) * LANES)
            xp, gp = xdt[:, ps], G[:, ps]
            dxp = dxw[:, ps] * dsx[:, ps]
            for hh in range(2):
                h = 2 * j + hh
                col, row = acs_ref[:, h:h + 1], acst_ref[h:h + 1, :]
                lm = jnp.exp(jnp.where(li >= si, col - row, -1e30))
                lmt = jnp.exp(jnp.where(si >= li, row - col, -1e30))
                m, mt = sc * lm, sct * lmt
                xm = jnp.where(half == hh, xp, 0.0).astype(BF16)
                gm = jnp.where(half == hh, gp, 0.0).astype(BF16)
                dm = _dot(gm, xm, NT)
                dmt = _dot(xm, gm, NT)
                dxp = dxp + _dot(mt.astype(BF16), gm)
                dsc = dsc + dm * lm
                dsct = dsct + dmt * lmt
                rs = jnp.sum(dm * m, axis=1, keepdims=True) - jnp.sum(dmt * mt, axis=1, keepdims=True)
                dacs = dacs + jnp.where(lane == h, rs, 0.0)
            dx_scr[:, ps] = dxp
        d_c = d_c + _dot(dsc.astype(BF16), bg)
        d_b = d_b + _dot(dsct.astype(BF16), cg)
        ds_scr[...] = dprev
        dxdt = dx_scr[...]
        dxs_ref[...] = (dxdt * dtx + dexp_ref[...] * G) * xs_g
        db_ref[...] = d_b * b_g
        dc_ref[...] = d_c * c_g
        colm = col_ref[...]
        dacs = dacs + _dot_exact(chan_a, colm)
        dlast = _dot_exact(jnp.broadcast_to(last_c, (8, GC)), colm)[0:1, :]
        dacs = dacs + jnp.where(_row(0, (Q, 1)) == Q - 1, dlast, 0.0)
        dadt = _dot_exact((si >= li).astype(F32), dacs)
        a = -jnp.exp(alog_ref[...])
        dt = dt_ref[...]
        ddt = dadt * a + _dot_exact(dxdt * xs, colm)
        dal_ref[0:1, :] += jnp.sum(dadt * dt * a, axis=0, keepdims=True)
        draw = ddt * _sigmoid(raw_ref[...] + bias_ref[...])
        dbi_ref[0:1, :] += jnp.sum(draw, axis=0, keepdims=True)
        ddt_ref[...] = draw.astype(ddt_ref.dtype)

    acc = pl.BlockSpec((8, LANES), lambda g, c: (0, g))
    return pl.pallas_call(
        body, grid=(GROUPS, nc),
        in_specs=[sp["xs"], sp["b"], sp["c"], sp["chan"], sp["chan"], sp["chan"], sp["cdx"], sp["head"], sp["headt"],
                  sp["head"], sp["head"], sp["hvec"], sp["hvec"], sp["state"], sp["chan"], sp["cvec"],
                  pl.BlockSpec((GC, LANES), lambda g, c: (0, 0))],
        out_specs=[sp["chan"],
                   pl.BlockSpec((Q, STATE), lambda g, c: (rc(c), g)),
                   pl.BlockSpec((Q, STATE), lambda g, c: (rc(c), g)),
                   sp["head"], acc, acc],
        out_shape=[jax.ShapeDtypeStruct((L, D), F32),
                   jax.ShapeDtypeStruct((L, GROUPS * STATE), F32),
                   jax.ShapeDtypeStruct((L, GROUPS * STATE), F32),
                   jax.ShapeDtypeStruct((L, GROUPS * LANES), BF16),
                   jax.ShapeDtypeStruct((8, GROUPS * LANES), F32),
                   jax.ShapeDtypeStruct((8, GROUPS * LANES), F32)],
        scratch_shapes=[pltpu.VMEM((STATE, GC), F32), pltpu.VMEM((Q, GC), F32)], name="scan_bwd",
        compiler_params=_params("parallel", "arbitrary"))(
            pre, pre, pre, dtx, eax, dsx, cdx, acs, acst, dt, dtraw, bias, alog, states, dy, dexp, collapse)


def gate_fwd(y, pre, proj, dexp, nw, D):
    L = y.shape[0]
    GC = D // GROUPS
    T = _pick(L, 256)

    def body(y_ref, xs_ref, z_ref, de_ref, nw_ref, o_ref):
        xs, _ = _silu_and_grad(xs_ref[...])
        z = z_ref[...]
        y3 = (y_ref[...] + de_ref[...] * xs) * (z * _sigmoid(z))
        for g in range(GROUPS):
            sl = slice(g * GC, (g + 1) * GC)
            yg = y3[:, sl]
            r = lax.rsqrt(jnp.mean(yg * yg, axis=-1, keepdims=True) + NORM_EPS)
            o_ref[:, sl] = (yg * r * nw_ref[:, sl]).astype(o_ref.dtype)

    row = lambda c: pl.BlockSpec((T, D), lambda i: (i, c))
    vec = pl.BlockSpec((1, D), lambda i: (0, 0))
    return pl.pallas_call(
        body, grid=(L // T,), in_specs=[row(0), row(0), row(2), vec, vec], out_specs=row(0),
        out_shape=jax.ShapeDtypeStruct((L, D), BF16), name="gate_fwd",
        compiler_params=_params("parallel"))(y, pre, proj, dexp, nw)


def gate_bwd(y, pre, proj, dexp, nw, dmix, D):
    L = y.shape[0]
    GC = D // GROUPS
    T = _pick(L, 256)

    def body(y_ref, xs_ref, z_ref, de_ref, nw_ref, dm_ref, dy_ref, dz_ref, dnw_ref, dde_ref):
        @pl.when(pl.program_id(0) == 0)
        def _():
            dnw_ref[...] = jnp.zeros_like(dnw_ref)
            dde_ref[...] = jnp.zeros_like(dde_ref)

        xs, _ = _silu_and_grad(xs_ref[...])
        sz, dsz = _silu_and_grad(z_ref[...])
        y2 = y_ref[...] + de_ref[...] * xs
        y3 = y2 * sz
        for g in range(GROUPS):
            sl = slice(g * GC, (g + 1) * GC)
            yg, dm = y3[:, sl], dm_ref[:, sl]
            r = lax.rsqrt(jnp.mean(yg * yg, axis=-1, keepdims=True) + NORM_EPS)
            n = yg * r
            gg = dm * nw_ref[:, sl]
            dy3 = r * (gg - n * jnp.mean(gg * n, axis=-1, keepdims=True))
            dnw_ref[:, sl] += jnp.sum(dm * n, axis=0, keepdims=True)
            dy2 = dy3 * sz[:, sl]
            dy_ref[:, sl] = dy2
            dz_ref[:, sl] = (dy3 * y2[:, sl] * dsz[:, sl]).astype(dz_ref.dtype)
            dde_ref[:, sl] += jnp.sum(dy2 * xs[:, sl], axis=0, keepdims=True)

    row = lambda c: pl.BlockSpec((T, D), lambda i: (i, c))
    vec = pl.BlockSpec((1, D), lambda i: (0, 0))
    return pl.pallas_call(
        body, grid=(L // T,), in_specs=[row(0), row(0), row(2), vec, vec, row(1)],
        out_specs=[row(0), row(0), vec, vec],
        out_shape=[jax.ShapeDtypeStruct((L, D), F32), jax.ShapeDtypeStruct((L, D), BF16),
                   jax.ShapeDtypeStruct((1, D), F32), jax.ShapeDtypeStruct((1, D), F32)],
        name="gate_bwd", compiler_params=_params("arbitrary"))(y, pre, proj, dexp, nw, dmix)


def adamw(g, w, m, v, name):
    R, C = g.shape
    T = R if R <= 128 else 128
    assert R % T == 0
    c1 = 1.0 - ADAM_B1 ** ADAM_STEP
    c2 = 1.0 - ADAM_B2 ** ADAM_STEP

    def body(g_ref, w_ref, m_ref, v_ref, d_ref, nm_ref, nv_ref):
        gv = g_ref[...]
        nm = ADAM_B1 * m_ref[...] + (1.0 - ADAM_B1) * gv
        nv = ADAM_B2 * v_ref[...] + (1.0 - ADAM_B2) * (gv * gv)
        d_ref[...] = -ADAM_LR * ((nm / c1) / (jnp.sqrt(nv / c2) + ADAM_EPS) + ADAM_WD * w_ref[...])
        nm_ref[...] = nm
        nv_ref[...] = nv

    blk = pl.BlockSpec((T, C), lambda i: (i, 0))
    return pl.pallas_call(
        body, grid=(R // T,), in_specs=[blk] * 4, out_specs=[blk] * 3,
        out_shape=[jax.ShapeDtypeStruct((R, C), F32)] * 3, name=name,
        compiler_params=_params("parallel"))(g, w, m, v)


def _to_groups(v, hpg):
    lead = v.shape[:-1]
    t = v.reshape(lead + (GROUPS, hpg))
    t = jnp.pad(t, [(0, 0)] * (len(lead) + 1) + [(0, LANES - hpg)])
    return t.reshape(lead + (GROUPS * LANES,))


def _from_groups(a, hpg):
    lead = a.shape[:-1]
    return a.reshape(lead + (GROUPS, LANES))[..., :hpg].reshape(lead + (GROUPS * hpg,))


def _expand_matrix(D):
    gc = D // GROUPS
    return (jnp.arange(LANES)[:, None] == (jnp.arange(gc)[None, :] // HEAD_DIM)).astype(F32)


def layer_fwd(x, p, D):
    h = rms_fwd(x, p["pre_w"])
    proj = matmul(h, p["wmain"], "nn", F32, "proj")
    dtraw = matmul(h, p["wdt"], "nn", F32, "dtproj")
    ypool, pooled = pool_fwd(proj, p["mixw"], p["scale"], D)
    pre = conv_fwd(proj, p["cw"], p["cb"], D)
    dtp = dt_prep(dtraw, p["bias"], p["alog"], _expand_matrix(D), D)
    dt, acs, acst, dtx, eax, dsx, cdx = dtp
    y, states = scan_fwd(pre, dtx, eax, dsx, cdx, acs, acst, D)
    yssd = gate_fwd(y, pre, proj, p["dexp"], p["nw"], D)
    mixed = jnp.concatenate([ypool, yssd], axis=1)
    out = matmul(mixed, p["wout"], "nn", F32, "outproj")
    xn = post_fwd(x, out, p["post_w"])
    return xn, dict(x=x, h=h, proj=proj, dtraw=dtraw, pooled=pooled, pre=pre, dtp=dtp, y=y, states=states,
                    mixed=mixed, out=out)


def layer_bwd(dxn, p, s, D):
    hpg = D // GROUPS // HEAD_DIM
    dt, acs, acst, dtx, eax, dsx, cdx = s["dtp"]
    dout, d_post = post_bwd(s["out"], p["post_w"], dxn)
    dmix = matmul(dout, p["wout"], "nt", F32, "dmixed")
    d_wout = matmul(s["mixed"], dout, "tn", F32, "dwout")
    dy2, dz, d_nw, d_dexp = gate_bwd(s["y"], s["pre"], s["proj"], p["dexp"], p["nw"], dmix, D)
    dxs, db, dc, ddtraw, d_alog, d_bias = scan_bwd(
        s["pre"], dtx, eax, dsx, cdx, acs, acst, dt, s["dtraw"], p["bias"], p["alog"], s["states"], dy2,
        p["dexp"], _expand_matrix(D).T, D)
    dpre = jnp.concatenate([dxs, db, dc], axis=1)
    draw, d_cw, d_cb = conv_bwd(dpre, s["proj"], p["cw"], D)
    du, dgate, dmixed, d_scale = pool_bwd(s["proj"], dmix, s["pooled"], p["mixw"], p["scale"], D)
    d_mixw = pool_dw(s["pooled"], dmixed, D)
    dproj = jnp.concatenate([du, dgate, dz, draw], axis=1)
    dh_a = matmul(dproj, p["wmain"], "nt", F32, "dh_main")
    dh_b = matmul(ddtraw, p["wdt"], "nt", F32, "dh_dt")
    d_wmain = matmul(s["h"], dproj, "tn", F32, "dwmain")
    d_wdt = matmul(s["h"], ddtraw, "tn", F32, "dwdt")
    dx, d_pre = rms_bwd(s["x"], p["pre_w"], dh_a, dh_b, dxn)
    grads = dict(
        pre_norm_w=d_pre[0], w_in=jnp.concatenate([d_wmain, _from_groups(d_wdt, hpg)], axis=1), pool_mix_w=d_mixw,
        pool_scale=d_scale[0], conv_w=d_cw[:CONV_K], conv_b=d_cb[0], dt_bias=_from_groups(d_bias[0], hpg),
        a_log=_from_groups(d_alog[0], hpg), d_skip=d_dexp[0].reshape(-1, HEAD_DIM).sum(axis=-1),
        ssd_norm_w=d_nw[0], w_out=d_wout, post_norm_w=d_post[0])
    return dx, grads


def layer_params(pre_w, w_in_full, mixw, scale, cw, cb, bias, alog, dskip, nw, w_out_full, post_w, D):
    hpg = D // GROUPS // HEAD_DIM
    main = w_in_full.shape[1] - GROUPS * hpg
    return dict(
        pre_w=pre_w[None], wmain=w_in_full[:, :main], wdt=_to_groups(w_in_full[:, main:], hpg), mixw=mixw,
        scale=scale[None], cw=cw, cb=cb[None], bias=_to_groups(bias, hpg)[None], alog=_to_groups(alog, hpg)[None],
        dexp=jnp.repeat(dskip, HEAD_DIM)[None], nw=nw[None], wout=w_out_full, post_w=post_w[None])


def local_step(x, target, params, D):
    saved = []
    for p in params:
        x, s = layer_fwd(x, p, D)
        saved.append(s)
    dx, sumsq = loss_head(x, target)
    grads = [None] * len(params)
    for l in reversed(range(len(params))):
        dx, grads[l] = layer_bwd(dx, params[l], saved[l], D)
    return sumsq, dx, grads


_ANY = pl.BlockSpec(memory_space=pl.ANY)


def _place():
    x, y, c = lax.axis_index("x"), lax.axis_index("y"), lax.axis_index("c")
    return x, y, c, [(1 - x, y), (x, 1 - y), (1 - x, 1 - y)]


def _remote(src, dst, send_sem, recv_sem, device):
    return pltpu.make_async_remote_copy(src_ref=src, dst_ref=dst, send_sem=send_sem, recv_sem=recv_sem,
                                        device_id=device, device_id_type=MESH)


def gather_chips(arrs):
    n = len(arrs)

    def body(*refs):
        ins, outs = refs[:n], refs[n:2 * n]
        send_sems, recv_sems, twin_send, twin_recv = refs[2 * n:]
        x, y, c, chips = _place()
        sibling = (x, y, 1 - c)
        mine = 2 * x + y
        twin, sent = [], []
        for a in range(n):
            ha = ins[a].shape[0] // 2
            own = pl.ds(c * ha, ha)
            twin.append(_remote(ins[a], outs[a].at[mine], twin_send.at[a], twin_recv.at[a], sibling))
            twin[-1].start()
            for j, chip in enumerate(chips):
                cp = _remote(ins[a].at[own], outs[a].at[mine, own], send_sems.at[6 * a + j], recv_sems.at[6 * a + j],
                             (*chip, c))
                cp.start()
                sent.append(cp)
        for a in range(n):
            ha = ins[a].shape[0] // 2
            own = pl.ds(c * ha, ha)
            for j, chip in enumerate(chips):
                landed = outs[a].at[2 * chip[0] + chip[1], own]
                _remote(landed, landed, send_sems.at[6 * a + j], recv_sems.at[6 * a + j], sibling).wait_recv()
                fwd = _remote(landed, landed, send_sems.at[6 * a + 3 + j], recv_sems.at[6 * a + 3 + j], sibling)
                fwd.start()
                sent.append(fwd)
        for a in range(n):
            ha = ins[a].shape[0] // 2
            other = pl.ds((1 - c) * ha, ha)
            for j, chip in enumerate(chips):
                passed = outs[a].at[2 * chip[0] + chip[1], other]
                _remote(passed, passed, send_sems.at[6 * a + 3 + j], recv_sems.at[6 * a + 3 + j], sibling).wait_recv()
        for cp in sent:
            cp.wait_send()
        for cp in twin:
            cp.wait()

    return pl.pallas_call(
        body, in_specs=[_ANY] * n, out_specs=[_ANY] * n,
        out_shape=[jax.ShapeDtypeStruct((4,) + a.shape, a.dtype) for a in arrs],
        scratch_shapes=[pltpu.SemaphoreType.DMA((6 * n,)), pltpu.SemaphoreType.DMA((6 * n,)),
                        pltpu.SemaphoreType.DMA((n,)), pltpu.SemaphoreType.DMA((n,))],
        name="gather_chips")(*arrs)


def pair_exchange(arrs):
    n = len(arrs)

    def body(*refs):
        ins, outs = refs[:n], refs[n:2 * n]
        send_sems, recv_sems = refs[2 * n:]
        x, y, c, _ = _place()
        copies = []
        for a in range(n):
            hr = ins[a].shape[1] // 2
            cp = _remote(ins[a].at[:, pl.ds((1 - c) * hr, hr)], outs[a], send_sems.at[a], recv_sems.at[a],
                         (x, y, 1 - c))
            cp.start()
            copies.append(cp)
        for cp in copies:
            cp.wait()

    return pl.pallas_call(
        body, in_specs=[_ANY] * n, out_specs=[_ANY] * n,
        out_shape=[jax.ShapeDtypeStruct((4, a.shape[1] // 2, a.shape[2]), a.dtype) for a in arrs],
        scratch_shapes=[pltpu.SemaphoreType.DMA((n,)), pltpu.SemaphoreType.DMA((n,))],
        name="pair_exchange")(*arrs)


def pair_add(g, r, core, out_dtype):
    _, R, C = g.shape
    hr = R // 2
    T = _pick(hr, 256) if hr % 8 == 0 else hr
    nb = hr // T

    def body(c_ref, g_ref, r_ref, o_ref):
        o_ref[...] = (g_ref[...] + r_ref[...]).astype(o_ref.dtype)

    return pl.pallas_call(
        body,
        grid_spec=pltpu.PrefetchScalarGridSpec(
            num_scalar_prefetch=1, grid=(4, nb),
            in_specs=[pl.BlockSpec((1, T, C), lambda k, i, c_ref: (k, c_ref[0] * nb + i, 0)),
                      pl.BlockSpec((1, T, C), lambda k, i, c_ref: (k, i, 0))],
            out_specs=pl.BlockSpec((1, T, C), lambda k, i, c_ref: (k, i, 0))),
        out_shape=jax.ShapeDtypeStruct((4, hr, C), out_dtype), name="pair_add",
        compiler_params=_params("parallel", "parallel"))(core, g, r)


def chip_exchange(arrs):
    n = len(arrs)

    def body(*refs):
        ins, outs = refs[:n], refs[n:2 * n]
        send_sems, recv_sems = refs[2 * n:]
        x, y, c, chips = _place()
        copies = []
        for a in range(n):
            for j, chip in enumerate(chips):
                cp = _remote(ins[a].at[2 * chip[0] + chip[1]], outs[a].at[j], send_sems.at[3 * a + j],
                             recv_sems.at[3 * a + j], (*chip, c))
                cp.start()
                copies.append(cp)
        for cp in copies:
            cp.wait()

    return pl.pallas_call(
        body, in_specs=[_ANY] * n, out_specs=[_ANY] * n,
        out_shape=[jax.ShapeDtypeStruct((3,) + a.shape[1:], a.dtype) for a in arrs],
        scratch_shapes=[pltpu.SemaphoreType.DMA((3 * n,)), pltpu.SemaphoreType.DMA((3 * n,))],
        name="chip_exchange")(*arrs)


def chip_add(p, r, chip, core):
    _, R, C = p.shape
    T = _pick(R, 256) if R % 8 == 0 else R
    nb = R // T

    def body(k_ref, c_ref, p_ref, r0_ref, r1_ref, r2_ref, o_ref):
        o_ref[...] = ((p_ref[0].astype(F32) + r0_ref[0].astype(F32)) + r1_ref[0].astype(F32)) + r2_ref[0].astype(F32)

    slot = lambda j: pl.BlockSpec((1, T, C), lambda i, k_ref, c_ref: (j, i, 0))
    return pl.pallas_call(
        body,
        grid_spec=pltpu.PrefetchScalarGridSpec(
            num_scalar_prefetch=2, grid=(nb,),
            in_specs=[pl.BlockSpec((1, T, C), lambda i, k_ref, c_ref: (k_ref[0], i, 0)), slot(0), slot(1), slot(2)],
            out_specs=pl.BlockSpec((T, C), lambda i, k_ref, c_ref: (c_ref[0] * nb + i, 0))),
        out_shape=jax.ShapeDtypeStruct((2 * R, C), F32), name="chip_add",
        compiler_params=_params("parallel"))(chip, core, p, r, r, r)


def pair_gather(arrs):
    n = len(arrs)

    def body(*refs):
        outs = refs[n:2 * n]
        send_sems, recv_sems = refs[2 * n:]
        x, y, c, _ = _place()
        copies = []
        for a in range(n):
            hr = outs[a].shape[0] // 2
            mine = outs[a].at[pl.ds(c * hr, hr)]
            cp = _remote(mine, mine, send_sems.at[a], recv_sems.at[a], (x, y, 1 - c))
            cp.start()
            copies.append(cp)
        for cp in copies:
            cp.wait()

    return pl.pallas_call(
        body, in_specs=[_ANY] * n, out_specs=[_ANY] * n,
        out_shape=[jax.ShapeDtypeStruct(a.shape, a.dtype) for a in arrs],
        input_output_aliases={a: a for a in range(n)},
        scratch_shapes=[pltpu.SemaphoreType.DMA((n,)), pltpu.SemaphoreType.DMA((n,))],
        name="pair_gather")(*arrs)


def allreduce_small(v):
    R = v.shape[0]

    def body(v_ref, o_ref, buf, send_sems, recv_sems, local_sem):
        x, y, c, chips = _place()
        me, sibling = (x, y, c), (x, y, 1 - c)

        def rows(px, py, pc):
            return buf.at[pl.ds((4 * px + 2 * py + pc) * R, R), :]

        def copy(k, block, to, src=None):
            return _remote(rows(*block) if src is None else src, rows(*block), send_sems.at[k], recv_sems.at[k], to)

        mine = pltpu.make_async_copy(v_ref, rows(*me), local_sem)
        mine.start()
        first = [copy(0, me, sibling, src=v_ref)]
        first += [copy(1 + j, me, (*chip, c), src=v_ref) for j, chip in enumerate(chips)]
        for cp in first:
            cp.start()
        passed = [copy(4 + j, (*chip, c), sibling) for j, chip in enumerate(chips)]
        for j, chip in enumerate(chips):
            copy(1 + j, (*chip, c), me).wait_recv()
            passed[j].start()
        copy(0, sibling, me).wait_recv()
        for j, chip in enumerate(chips):
            copy(4 + j, (*chip, 1 - c), me).wait_recv()
        for cp in first + passed:
            cp.wait_send()
        mine.wait()
        acc = buf[0:R, :]
        for d in range(1, 8):
            acc = acc + buf[d * R:(d + 1) * R, :]
        o_ref[...] = acc

    return pl.pallas_call(
        body, in_specs=[pl.BlockSpec(memory_space=pltpu.VMEM)], out_specs=pl.BlockSpec(memory_space=pltpu.VMEM),
        out_shape=jax.ShapeDtypeStruct((R, LANES), F32),
        scratch_shapes=[pltpu.VMEM((8 * R, LANES), F32), pltpu.SemaphoreType.DMA((7,)),
                        pltpu.SemaphoreType.DMA((7,)), pltpu.SemaphoreType.DMA],
        name="allreduce_small", compiler_params=pltpu.CompilerParams(vmem_limit_bytes=VMEM_LIMIT))(v)


def reduce_scatter(grads, chip, core):
    recv = pair_exchange(grads)
    pair = [pair_add(g, r, core, BF16) for g, r in zip(grads, recv)]
    recv = chip_exchange(pair)
    return pair_gather([chip_add(p, r, chip, core) for p, r in zip(pair, recv)])


SMALL = ("pre_norm_w", "pool_scale", "conv_w", "conv_b", "dt_bias", "a_log", "d_skip", "ssd_norm_w", "post_norm_w")


def _pack(parts):
    flat = jnp.concatenate([p.reshape(-1) for p in parts])
    n = flat.shape[0]
    rows = -(-n // (LANES * LANES)) * LANES
    return jnp.pad(flat, (0, rows * LANES - n)).reshape(rows, LANES)


def _unpack(packed, shapes):
    flat, out, at = packed.reshape(-1), [], 0
    for s in shapes:
        n = math.prod(s)
        out.append(flat[at:at + n].reshape(s))
        at += n
    return out


def kernel(x, pre_norm_w, w_in, pool_mix_w, pool_scale, conv_w, conv_b, dt_bias, a_log, d_skip, ssd_norm_w, w_out, post_norm_w, loss_target, m_pre_norm_w, m_w_in, m_pool_mix_w, m_pool_scale, m_conv_w, m_conv_b, m_dt_bias, m_a_log, m_d_skip, m_ssd_norm_w, m_w_out, m_post_norm_w, v_pre_norm_w, v_w_in, v_pool_mix_w, v_pool_scale, v_conv_w, v_conv_b, v_dt_bias, v_a_log, v_d_skip, v_ssd_norm_w, v_w_out, v_post_norm_w):
    NL, D, SH = w_in.shape
    L = x.shape[1]
    PGW = D // GROUPS
    CS = conv_w.shape[2]
    C = 4 * CS
    chip = (2 * lax.axis_index("x") + lax.axis_index("y")).astype(jnp.int32)
    core = lax.axis_index("c").astype(jnp.int32).reshape(1)

    g_in, g_out, g_mix, g_cw = gather_chips([
        w_in.astype(BF16), w_out.astype(BF16), pool_mix_w.astype(BF16).reshape(NL, GROUPS * (PGW // 4), PGW),
        conv_w.reshape(NL, CONV_K * CS // LANES, LANES)])
    params = []
    for l in range(NL):
        w_in_l = jnp.concatenate([g_in[k, l] for k in range(4)], axis=1)
        mix_l = g_mix[:, l].reshape(4, GROUPS, PGW // 4, PGW).transpose(1, 0, 2, 3).reshape(GROUPS, PGW, PGW)
        cw_l = g_cw[:, l].reshape(4, CONV_K, CS).transpose(1, 0, 2).reshape(CONV_K, C)
        params.append(layer_params(pre_norm_w[l], w_in_l, mix_l, pool_scale[l], cw_l, conv_b[l], dt_bias[l], a_log[l],
                                   d_skip[l], ssd_norm_w[l], g_out[:, l].reshape(2 * D, D), post_norm_w[l], D))

    sumsq, grad_x, grads = local_step(x[0], loss_target[0], params, D)

    big = [
        jnp.stack([g["w_in"].reshape(D, 4, SH).transpose(1, 0, 2) for g in grads], axis=1).reshape(4, NL * D, SH),
        jnp.stack([g["w_out"].reshape(4, 2 * D // 4, D) for g in grads], axis=1).reshape(4, NL * 2 * D // 4, D),
        jnp.stack([g["pool_mix_w"].reshape(GROUPS, 4, PGW // 4, PGW).transpose(1, 0, 2, 3) for g in grads],
                  axis=1).reshape(4, NL * GROUPS * (PGW // 4), PGW)]
    r_in, r_out, r_mix = reduce_scatter(big, chip.reshape(1), core)

    small_shapes = [(NL,) + grads[0][n].shape for n in SMALL]
    packed = _pack([0.5 / D * sumsq[0, :1]] + [jnp.stack([g[n] for g in grads]) for n in SMALL])
    total = allreduce_small(packed)
    loss, *small = _unpack(total, [(1,)] + small_shapes)
    small = dict(zip(SMALL, small))
    small["conv_w"] = lax.dynamic_slice_in_dim(small["conv_w"], chip * CS, CS, axis=2)

    given = dict(pre_norm_w=(pre_norm_w, m_pre_norm_w, v_pre_norm_w), pool_scale=(pool_scale, m_pool_scale, v_pool_scale),
                 conv_w=(conv_w, m_conv_w, v_conv_w), conv_b=(conv_b, m_conv_b, v_conv_b),
                 dt_bias=(dt_bias, m_dt_bias, v_dt_bias), a_log=(a_log, m_a_log, v_a_log),
                 d_skip=(d_skip, m_d_skip, v_d_skip), ssd_norm_w=(ssd_norm_w, m_ssd_norm_w, v_ssd_norm_w),
                 post_norm_w=(post_norm_w, m_post_norm_w, v_post_norm_w))
    shapes = [given[n][0].shape for n in SMALL]
    upd = adamw(_pack([small[n] for n in SMALL]), *[_pack([given[n][i] for n in SMALL]) for i in range(3)],
                "adamw_small")
    upd = [dict(zip(SMALL, _unpack(u, shapes))) for u in upd]

    out = {n: (small[n], upd[0][n], upd[1][n], upd[2][n]) for n in SMALL}
    for n, r, w, m, v in (("w_in", r_in, w_in, m_w_in, v_w_in), ("w_out", r_out, w_out, m_w_out, v_w_out),
                          ("pool_mix_w", r_mix, pool_mix_w, m_pool_mix_w, v_pool_mix_w)):
        flat = lambda t: t.reshape(r.shape)
        out[n] = tuple(t.reshape(w.shape) for t in (r,) + tuple(adamw(r, flat(w), flat(m), flat(v), "adamw_" + n)))

    order = ("pre_norm_w", "w_in", "pool_mix_w", "pool_scale", "conv_w", "conv_b", "dt_bias", "a_log", "d_skip",
             "ssd_norm_w", "w_out", "post_norm_w")
    return (loss.reshape(()), grad_x[None], *[out[n][0] for n in order], *[out[n][1] for n in order],
            *[out[n][2] for n in order], *[out[n][3] for n in order])
```

```python
import functools
import math

import jax
import jax.numpy as jnp
from jax import lax
from jax.experimental import pallas as pl
from jax.experimental.pallas import tpu as pltpu

F32 = jnp.float32
BF16 = jnp.bfloat16
HIGHEST = lax.Precision.HIGHEST

NORM_EPS = 1e-6
HEAD_DIM = 64
STATE = 128
GROUPS = 4
POOL_WINDOWS = (2, 4, 8, 16)
POOL_HALO = 16
CONV_K = 4
CONV_HALO = 8
SCAN_CHUNK = 256
LANES = 128
VMEM_LIMIT = 52 * 1024 * 1024

ADAM_LR = 0.001
ADAM_B1 = 0.9
ADAM_B2 = 0.999
ADAM_EPS = 1e-08
ADAM_WD = 0.01
ADAM_STEP = 10

MESH = pl.DeviceIdType.MESH

NN = (((1,), (0,)), ((), ()))
NT = (((1,), (1,)), ((), ()))
TN = (((0,), (0,)), ((), ()))

_ANY = pl.BlockSpec(memory_space=pl.ANY)


def _params(*sem):
    return pltpu.CompilerParams(dimension_semantics=sem, vmem_limit_bytes=VMEM_LIMIT)


def _pick(dim, pref):
    if dim <= pref:
        return dim
    t = (pref // LANES) * LANES
    while t > LANES and dim % t:
        t -= LANES
    assert dim % t == 0, (dim, pref)
    return t


def _dot(a, b, dn=NN):
    return lax.dot_general(a, b, dn, preferred_element_type=F32)


def _dot_exact(a, b):
    return jnp.dot(a, b, precision=HIGHEST, preferred_element_type=F32)


def _sigmoid(v):
    return 1.0 / (1.0 + jnp.exp(-v))


def _silu_and_grad(v):
    s = _sigmoid(v)
    return v * s, s * (1.0 + v * (1.0 - s))


def _row(i, shape):
    return lax.broadcasted_iota(jnp.int32, shape, 0) + i


def _sds(shape, dtype):
    return jax.ShapeDtypeStruct(tuple(shape), dtype)


class Job:
    def __init__(self, ins, outs, aliased, nsem, start, finish):
        self.ins, self.outs, self.aliased, self.nsem, self.start, self.finish = ins, outs, aliased, nsem, start, finish


def _place():
    x, y, c = lax.axis_index("x"), lax.axis_index("y"), lax.axis_index("c")
    return x, y, c, [(1 - x, y), (x, 1 - y), (1 - x, 1 - y)]


def _remote(src, dst, send_sem, recv_sem, device):
    return pltpu.make_async_remote_copy(src_ref=src, dst_ref=dst, send_sem=send_sem, recv_sem=recv_sem,
                                        device_id=device, device_id_type=MESH)


def gather_send_job(arrs):
    n = len(arrs)

    def copies(ins, outs, send, recv):
        x, y, c, chips = _place()
        mine = 2 * x + y
        out = []
        for a in range(n):
            out.append(_remote(ins[a], outs[a].at[mine], send.at[4 * a + 3], recv.at[4 * a + 3], (x, y, 1 - c)))
            for j, chip in enumerate(chips):
                out.append(_remote(ins[a].at[c], outs[a].at[mine, c], send.at[4 * a + j], recv.at[4 * a + j],
                                   (*chip, c)))
        return out

    def start(ins, outs, send, recv):
        for cp in copies(ins, outs, send, recv):
            cp.start()

    def finish(ins, outs, send, recv):
        x, y, c, chips = _place()
        for a in range(n):
            for j, chip in enumerate(chips):
                landed = outs[a].at[2 * chip[0] + chip[1], c]
                _remote(landed, landed, send.at[4 * a + j], recv.at[4 * a + j], (x, y, 1 - c)).wait_recv()
            twin = outs[a].at[2 * x + y]
            _remote(twin, twin, send.at[4 * a + 3], recv.at[4 * a + 3], (x, y, 1 - c)).wait_recv()
        for cp in copies(ins, outs, send, recv):
            cp.wait_send()

    return Job(list(arrs), [_sds((4,) + a.shape, a.dtype) for a in arrs], False, 4 * n, start, finish)


def gather_pass_job(bufs):
    n = len(bufs)

    def copies(outs, send, recv):
        x, y, c, chips = _place()
        out = []
        for a in range(n):
            for j, chip in enumerate(chips):
                landed = outs[a].at[2 * chip[0] + chip[1], c]
                out.append(_remote(landed, landed, send.at[3 * a + j], recv.at[3 * a + j], (x, y, 1 - c)))
        return out

    def start(ins, outs, send, recv):
        for cp in copies(outs, send, recv):
            cp.start()

    def finish(ins, outs, send, recv):
        x, y, c, chips = _place()
        for a in range(n):
            for j, chip in enumerate(chips):
                passed = outs[a].at[2 * chip[0] + chip[1], 1 - c]
                _remote(passed, passed, send.at[3 * a + j], recv.at[3 * a + j], (x, y, 1 - c)).wait_recv()
        for cp in copies(outs, send, recv):
            cp.wait_send()

    return Job(list(bufs), [_sds(b.shape, b.dtype) for b in bufs], True, 3 * n, start, finish)


def pair_exchange_job(arrs):
    n = len(arrs)

    def copies(ins, outs, send, recv):
        x, y, c, _ = _place()
        return [_remote(ins[a].at[:, 1 - c], outs[a], send.at[a], recv.at[a], (x, y, 1 - c)) for a in range(n)]

    def start(ins, outs, send, recv):
        for cp in copies(ins, outs, send, recv):
            cp.start()

    def finish(ins, outs, send, recv):
        for cp in copies(ins, outs, send, recv):
            cp.wait()

    return Job(list(arrs), [_sds((4,) + a.shape[2:], a.dtype) for a in arrs], False, n, start, finish)


def chip_exchange_job(arrs):
    n = len(arrs)

    def copies(ins, outs, send, recv):
        x, y, c, chips = _place()
        return [_remote(ins[a].at[2 * chip[0] + chip[1]], outs[a].at[j], send.at[3 * a + j], recv.at[3 * a + j],
                        (*chip, c)) for a in range(n) for j, chip in enumerate(chips)]

    def start(ins, outs, send, recv):
        for cp in copies(ins, outs, send, recv):
            cp.start()

    def finish(ins, outs, send, recv):
        for cp in copies(ins, outs, send, recv):
            cp.wait()

    return Job(list(arrs), [_sds((3,) + a.shape[1:], a.dtype) for a in arrs], False, 3 * n, start, finish)


def pair_gather_job(bufs):
    n = len(bufs)

    def copies(outs, send, recv):
        x, y, c, _ = _place()
        return [_remote(outs[a].at[c], outs[a].at[c], send.at[a], recv.at[a], (x, y, 1 - c)) for a in range(n)]

    def start(ins, outs, send, recv):
        for cp in copies(outs, send, recv):
            cp.start()

    def finish(ins, outs, send, recv):
        for cp in copies(outs, send, recv):
            cp.wait()

    return Job(list(bufs), [_sds(b.shape, b.dtype) for b in bufs], True, n, start, finish)


def _call(body, *, grid, in_specs, out_specs, out_shape, name, sem, args, scratch_shapes=(), jobs=(), aliases=None):
    in_specs, out_specs, out_shape, scratch_shapes = list(in_specs), list(out_specs), list(out_shape), list(scratch_shapes)
    aliases = dict(aliases or {})
    n_in, n_out, n_scr = len(in_specs), len(out_specs), len(scratch_shapes)
    if jobs:
        sem = ("arbitrary",) * len(grid)
    at_in, at_out = n_in, n_out
    for j in jobs:
        if j.aliased:
            aliases.update({at_in + i: at_out + i for i in range(len(j.ins))})
        at_in, at_out = at_in + len(j.ins), at_out + len(j.outs)

    def wrapped(*refs):
        ins, p = refs[:n_in], n_in
        jins = []
        for j in jobs:
            jins.append(refs[p:p + len(j.ins)])
            p += len(j.ins)
        outs, p = refs[p:p + n_out], p + n_out
        jouts = []
        for j in jobs:
            jouts.append(refs[p:p + len(j.outs)])
            p += len(j.outs)
        scr, sems = refs[p:p + n_scr], refs[p + n_scr:]

        def start():
            for k, j in enumerate(jobs):
                j.start(jins[k], jouts[k], sems[2 * k], sems[2 * k + 1])

        def finish():
            for k, j in enumerate(jobs):
                j.finish(jins[k], jouts[k], sems[2 * k], sems[2 * k + 1])

        if jobs and grid:
            ids = [pl.program_id(d) for d in range(len(grid))]
            pl.when(functools.reduce(jnp.logical_and, [i == 0 for i in ids]))(start)
            body(*ins, *outs, *scr)
            pl.when(functools.reduce(jnp.logical_and, [i == g - 1 for i, g in zip(ids, grid)]))(finish)
        else:
            start()
            body(*ins, *outs, *scr)
            finish()

    kwargs = dict(grid=grid) if grid else {}
    res = pl.pallas_call(
        wrapped, in_specs=in_specs + [_ANY] * (at_in - n_in), out_specs=out_specs + [_ANY] * (at_out - n_out),
        out_shape=out_shape + [o for j in jobs for o in j.outs],
        scratch_shapes=scratch_shapes + [pltpu.SemaphoreType.DMA((j.nsem,)) for j in jobs for _ in range(2)],
        input_output_aliases=aliases, name=name,
        compiler_params=pltpu.CompilerParams(dimension_semantics=sem, vmem_limit_bytes=VMEM_LIMIT) if grid
        else pltpu.CompilerParams(vmem_limit_bytes=VMEM_LIMIT), **kwargs)(*args, *[a for j in jobs for a in j.ins])
    res = list(res)
    outs, rest, per_job = res[:n_out], res[n_out:], []
    for j in jobs:
        per_job.append(rest[:len(j.outs)])
        rest = rest[len(j.outs):]
    return outs, per_job


def run_jobs(jobs, name):
    return _call(lambda: None, grid=(), in_specs=[], out_specs=[], out_shape=[], name=name, sem=(), args=(), jobs=jobs)[1]


def pair_add(g, r, core, out_dtype):
    _, _, R, C = g.shape
    T = _pick(R, 256)
    assert R % T == 0

    def body(c_ref, g_ref, r_ref, o_ref):
        o_ref[0] = (g_ref[0, 0] + r_ref[0]).astype(o_ref.dtype)

    return pl.pallas_call(
        body,
        grid_spec=pltpu.PrefetchScalarGridSpec(
            num_scalar_prefetch=1, grid=(4, R // T),
            in_specs=[pl.BlockSpec((1, 1, T, C), lambda k, i, c_ref: (k, c_ref[0], i, 0)),
                      pl.BlockSpec((1, T, C), lambda k, i, c_ref: (k, i, 0))],
            out_specs=pl.BlockSpec((1, T, C), lambda k, i, c_ref: (k, i, 0))),
        out_shape=_sds((4, R, C), out_dtype), name="pair_add",
        compiler_params=_params("parallel", "parallel"))(core, g, r)


def chip_add(p, r, chip, core):
    _, R, C = p.shape
    T = _pick(R, 256)
    assert R % T == 0

    def body(k_ref, c_ref, p_ref, r0_ref, r1_ref, r2_ref, o_ref):
        o_ref[0] = ((p_ref[0].astype(F32) + r0_ref[0].astype(F32)) + r1_ref[0].astype(F32)) + r2_ref[0].astype(F32)

    slot = lambda j: pl.BlockSpec((1, T, C), lambda i, k_ref, c_ref: (j, i, 0))
    return pl.pallas_call(
        body,
        grid_spec=pltpu.PrefetchScalarGridSpec(
            num_scalar_prefetch=2, grid=(R // T,),
            in_specs=[pl.BlockSpec((1, T, C), lambda i, k_ref, c_ref: (k_ref[0], i, 0)), slot(0), slot(1), slot(2)],
            out_specs=pl.BlockSpec((1, T, C), lambda i, k_ref, c_ref: (c_ref[0], i, 0))),
        out_shape=_sds((2, R, C), F32), name="chip_add",
        compiler_params=_params("parallel"))(chip, core, p, r, r, r)


def allreduce_small(v):
    R = v.shape[0]

    def body(v_ref, o_ref, buf, send_sems, recv_sems, local_sem):
        x, y, c, chips = _place()
        me, sibling = (x, y, c), (x, y, 1 - c)

        def rows(px, py, pc):
            return buf.at[pl.ds((4 * px + 2 * py + pc) * R, R), :]

        def copy(k, block, to, src=None):
            return _remote(rows(*block) if src is None else src, rows(*block), send_sems.at[k], recv_sems.at[k], to)

        mine = pltpu.make_async_copy(v_ref, rows(*me), local_sem)
        mine.start()
        first = [copy(0, me, sibling, src=v_ref)]
        first += [copy(1 + j, me, (*chip, c), src=v_ref) for j, chip in enumerate(chips)]
        for cp in first:
            cp.start()
        passed = [copy(4 + j, (*chip, c), sibling) for j, chip in enumerate(chips)]
        for j, chip in enumerate(chips):
            copy(1 + j, (*chip, c), me).wait_recv()
            passed[j].start()
        copy(0, sibling, me).wait_recv()
        for j, chip in enumerate(chips):
            copy(4 + j, (*chip, 1 - c), me).wait_recv()
        for cp in first + passed:
            cp.wait_send()
        mine.wait()
        acc = buf[0:R, :]
        for d in range(1, 8):
            acc = acc + buf[d * R:(d + 1) * R, :]
        o_ref[...] = acc

    return pl.pallas_call(
        body, in_specs=[pl.BlockSpec(memory_space=pltpu.VMEM)], out_specs=pl.BlockSpec(memory_space=pltpu.VMEM),
        out_shape=_sds((R, LANES), F32),
        scratch_shapes=[pltpu.VMEM((8 * R, LANES), F32), pltpu.SemaphoreType.DMA((7,)),
                        pltpu.SemaphoreType.DMA((7,)), pltpu.SemaphoreType.DMA],
        name="allreduce_small", compiler_params=pltpu.CompilerParams(vmem_limit_bytes=VMEM_LIMIT))(v)


def matmul(a, b, mode, out_dtype, name, tm=512, tn=1024, tk=2048, jobs=()):
    if mode == "nn":
        (M, K), (K2, N) = a.shape, b.shape
    elif mode == "nt":
        (M, K), (N, K2) = a.shape, b.shape
    else:
        (K, M), (K2, N) = a.shape, b.shape
    assert K == K2
    tm, tn, tk = _pick(M, tm), _pick(N, tn), _pick(K, tk)
    nk = K // tk
    dn = {"nn": NN, "nt": NT, "tn": TN}[mode]

    def body(a_ref, b_ref, o_ref, acc_ref):
        k = pl.program_id(2)
        part = _dot(a_ref[...].astype(BF16), b_ref[...].astype(BF16), dn)

        @pl.when(k == 0)
        def _():
            acc_ref[...] = part

        @pl.when(k > 0)
        def _():
            acc_ref[...] += part

        @pl.when(k == nk - 1)
        def _():
            o_ref[...] = acc_ref[...].astype(o_ref.dtype)

    a_spec = (pl.BlockSpec((tk, tm), lambda i, j, k: (k, i)) if mode == "tn"
              else pl.BlockSpec((tm, tk), lambda i, j, k: (i, k)))
    b_spec = (pl.BlockSpec((tn, tk), lambda i, j, k: (j, k)) if mode == "nt"
              else pl.BlockSpec((tk, tn), lambda i, j, k: (k, j)))
    outs, per_job = _call(
        body, grid=(M // tm, N // tn, nk), in_specs=[a_spec, b_spec],
        out_specs=[pl.BlockSpec((tm, tn), lambda i, j, k: (i, j))], out_shape=[_sds((M, N), out_dtype)],
        scratch_shapes=[pltpu.VMEM((tm, tn), F32)], name=name, sem=("parallel", "parallel", "arbitrary"),
        args=(a, b), jobs=jobs)
    return outs[0], per_job


def rms_fwd(x, w):
    L, D = x.shape
    T = _pick(L, 256)

    def body(x_ref, w_ref, h_ref):
        xv = x_ref[...]
        r = lax.rsqrt(jnp.mean(xv * xv, axis=-1, keepdims=True) + NORM_EPS)
        h_ref[...] = (xv * r * w_ref[...]).astype(h_ref.dtype)

    return pl.pallas_call(
        body, grid=(L // T,),
        in_specs=[pl.BlockSpec((T, D), lambda i: (i, 0)), pl.BlockSpec((1, D), lambda i: (0, 0))],
        out_specs=pl.BlockSpec((T, D), lambda i: (i, 0)),
        out_shape=_sds((L, D), BF16), name="rms_fwd", compiler_params=_params("parallel"))(x, w)


def post_fwd(x, o, w, jobs=()):
    L, D = x.shape
    T = _pick(L, 256)

    def body(x_ref, o_ref, w_ref, y_ref):
        ov = o_ref[...]
        r = lax.rsqrt(jnp.mean(ov * ov, axis=-1, keepdims=True) + NORM_EPS)
        y_ref[...] = x_ref[...] + ov * r * w_ref[...]

    row = pl.BlockSpec((T, D), lambda i: (i, 0))
    outs, per_job = _call(
        body, grid=(L // T,), in_specs=[row, row, pl.BlockSpec((1, D), lambda i: (0, 0))], out_specs=[row],
        out_shape=[_sds((L, D), F32)], name="post_fwd", sem=("parallel",), args=(x, o, w), jobs=jobs)
    return outs[0], per_job


def _rms_bwd_math(xv, w, dy):
    r = lax.rsqrt(jnp.mean(xv * xv, axis=-1, keepdims=True) + NORM_EPS)
    xhat = xv * r
    g = dy * w
    dx = r * (g - xhat * jnp.mean(g * xhat, axis=-1, keepdims=True))
    return dx, jnp.sum(dy * xhat, axis=0, keepdims=True)


def post_bwd(o, w, dxn):
    L, D = o.shape
    T = _pick(L, 256)

    def body(o_ref, w_ref, d_ref, do_ref, dw_ref):
        dx, dw = _rms_bwd_math(o_ref[...], w_ref[...], d_ref[...])
        do_ref[...] = dx.astype(do_ref.dtype)

        @pl.when(pl.program_id(0) == 0)
        def _():
            dw_ref[...] = jnp.zeros_like(dw_ref)

        dw_ref[...] += dw

    row = pl.BlockSpec((T, D), lambda i: (i, 0))
    vec = pl.BlockSpec((1, D), lambda i: (0, 0))
    return pl.pallas_call(
        body, grid=(L // T,), in_specs=[row, vec, row], out_specs=[row, vec],
        out_shape=[_sds((L, D), BF16), _sds((1, D), F32)],
        name="post_bwd", compiler_params=_params("arbitrary"))(o, w, dxn)


def rms_bwd(x, w, dh_a, dh_b, dxn):
    L, D = x.shape
    T = _pick(L, 256)

    def body(x_ref, w_ref, a_ref, b_ref, d_ref, dx_ref, dw_ref):
        dx, dw = _rms_bwd_math(x_ref[...], w_ref[...], a_ref[...] + b_ref[...])
        dx_ref[...] = d_ref[...] + dx

        @pl.when(pl.program_id(0) == 0)
        def _():
            dw_ref[...] = jnp.zeros_like(dw_ref)

        dw_ref[...] += dw

    row = pl.BlockSpec((T, D), lambda i: (i, 0))
    vec = pl.BlockSpec((1, D), lambda i: (0, 0))
    return pl.pallas_call(
        body, grid=(L // T,), in_specs=[row, vec, row, row, row], out_specs=[row, vec],
        out_shape=[_sds((L, D), F32), _sds((1, D), F32)],
        name="rms_bwd", compiler_params=_params("arbitrary"))(x, w, dh_a, dh_b, dxn)


def loss_head(y, target):
    L, D = y.shape
    T = _pick(L, 256)

    def body(y_ref, t_ref, d_ref, s_ref):
        e = y_ref[...] - t_ref[...]
        d_ref[...] = e * (1.0 / D)

        @pl.when(pl.program_id(0) == 0)
        def _():
            s_ref[...] = jnp.zeros_like(s_ref)

        s_ref[...] += jnp.sum(e * e)

    row = pl.BlockSpec((T, D), lambda i: (i, 0))
    return pl.pallas_call(
        body, grid=(L // T,), in_specs=[row, row],
        out_specs=[row, pl.BlockSpec((8, LANES), lambda i: (0, 0))],
        out_shape=[_sds((L, D), F32), _sds((8, LANES), F32)],
        name="loss_head", compiler_params=_params("arbitrary"))(y, target)


def _window_sums(xe, w, back):
    n = xe.shape[0]
    s, k = xe, 1
    while k < w:
        s = s + pltpu.roll(s, k if back else n - k, 0)
        k *= 2
    return s


def pool_fwd(proj, mixw, scale, D):
    L = proj.shape[0]
    PGW = D // GROUPS
    T = _pick(L, 256)
    hb = T // POOL_HALO

    def body(u_ref, halo_ref, g_ref, mw_ref, sc_ref, y_ref, p_ref):
        i = pl.program_id(0)
        u = u_ref[...]
        halo = jnp.where(i > 0, halo_ref[...], 0.0)
        xe = jnp.concatenate([halo, u], axis=0)
        t1 = _row(i * T + 1, (T, 1))
        for g, w in enumerate(POOL_WINDOWS):
            sl = slice(g * PGW, (g + 1) * PGW)
            win = _window_sums(xe[:, sl], w, True)[POOL_HALO:, :]
            cnt = jnp.minimum(t1, w).astype(F32)
            pooled = (win / cnt - u[:, sl]).astype(BF16)
            p_ref[:, sl] = pooled
            mixed = _dot(pooled, mw_ref[g])
            gate = g_ref[:, sl]
            y_ref[:, sl] = (mixed * sc_ref[:, sl] * (gate * _sigmoid(gate))).astype(BF16)

    return pl.pallas_call(
        body, grid=(L // T,),
        in_specs=[pl.BlockSpec((T, D), lambda i: (i, 0)),
                  pl.BlockSpec((POOL_HALO, D), lambda i: (jnp.maximum(i * hb - 1, 0), 0)),
                  pl.BlockSpec((T, D), lambda i: (i, 1)),
                  pl.BlockSpec((GROUPS, PGW, PGW), lambda i: (0, 0, 0)),
                  pl.BlockSpec((1, D), lambda i: (0, 0))],
        out_specs=[pl.BlockSpec((T, D), lambda i: (i, 0)), pl.BlockSpec((T, D), lambda i: (i, 0))],
        out_shape=[_sds((L, 2 * D), BF16), _sds((L, D), BF16)],
        name="pool_fwd", compiler_params=_params("parallel"))(proj, proj, proj, mixw, scale)


def pool_bwd(proj, dmix, pooled, mixw, scale, dproj, D):
    L = proj.shape[0]
    PGW = D // GROUPS
    T = _pick(L, 256)
    hb = T // POOL_HALO
    nT = L // T

    def body(g_ref, gh_ref, dy_ref, dyh_ref, p_ref, mw_ref, sc_ref, old_ref, dp_ref, dm_ref, ds_ref):
        i = pl.program_id(0)
        t1 = _row(i * T + 1, (T, 1))
        th1 = _row((i + 1) * T + 1, (POOL_HALO, 1))
        live = i < nT - 1

        @pl.when(i == 0)
        def _():
            ds_ref[...] = jnp.zeros_like(ds_ref)

        for g, w in enumerate(POOL_WINDOWS):
            sl = slice(g * PGW, (g + 1) * PGW)
            sc = sc_ref[:, sl]
            gate, dy = g_ref[:, sl], dy_ref[:, sl]
            sg, dsg = _silu_and_grad(gate)
            mixed = _dot(p_ref[:, sl], mw_ref[g])
            dmixed = (dy * sc * sg).astype(BF16)
            dm_ref[:, sl] = dmixed
            dp_ref[:, D + g * PGW:D + (g + 1) * PGW] = (dy * mixed * sc * dsg).astype(BF16)
            ds_ref[:, sl] += jnp.sum(dy * mixed * sg, axis=0, keepdims=True)
            dpool = _dot(dmixed, mw_ref[g], NT)
            gate_h = gh_ref[:, sl]
            dmixed_h = (dyh_ref[:, sl] * sc * (gate_h * _sigmoid(gate_h))).astype(BF16)
            dpool_h = jnp.where(live, _dot(dmixed_h, mw_ref[g], NT), 0.0)
            q = dpool / jnp.minimum(t1, w).astype(F32)
            q_h = dpool_h / jnp.minimum(th1, w).astype(F32)
            qe = jnp.concatenate([q, q_h], axis=0)
            dp_ref[:, sl] = (_window_sums(qe, w, False)[:T, :] - dpool).astype(BF16)

    nxt = lambda i: jnp.minimum((i + 1) * hb, L // POOL_HALO - 1)
    row = lambda c: pl.BlockSpec((T, D), lambda i: (i, c))
    return pl.pallas_call(
        body, grid=(nT,),
        in_specs=[row(1), pl.BlockSpec((POOL_HALO, D), lambda i: (nxt(i), 1)),
                  row(0), pl.BlockSpec((POOL_HALO, D), lambda i: (nxt(i), 0)),
                  row(0), pl.BlockSpec((GROUPS, PGW, PGW), lambda i: (0, 0, 0)),
                  pl.BlockSpec((1, D), lambda i: (0, 0)), _ANY],
        out_specs=[pl.BlockSpec((T, 2 * D), lambda i: (i, 0)), row(0), pl.BlockSpec((1, D), lambda i: (0, 0))],
        out_shape=[_sds(dproj.shape, dproj.dtype), _sds((L, D), BF16), _sds((1, D), F32)],
        input_output_aliases={7: 0},
        name="pool_bwd", compiler_params=_params("arbitrary"))(proj, proj, dmix, dmix, pooled, mixw, scale, dproj)


def pool_dw(pooled, dmixed, D):
    L = pooled.shape[0]
    PGW = D // GROUPS
    tk = _pick(L, 1024)
    nk = L // tk

    def body(p_ref, d_ref, o_ref):
        @pl.when(pl.program_id(1) == 0)
        def _():
            o_ref[...] = jnp.zeros_like(o_ref)

        o_ref[0] += _dot(p_ref[...], d_ref[...], TN)

    blk = pl.BlockSpec((tk, PGW), lambda g, k: (k, g))
    return pl.pallas_call(
        body, grid=(GROUPS, nk), in_specs=[blk, blk],
        out_specs=pl.BlockSpec((1, PGW, PGW), lambda g, k: (g, 0, 0)),
        out_shape=_sds((GROUPS, PGW, PGW), F32), name="pool_dw",
        compiler_params=_params("parallel", "arbitrary"))(pooled, dmixed)


def conv_fwd(proj, cw, cb, D):
    L = proj.shape[0]
    C = cw.shape[1]
    assert (3 * D) % C == 0
    cblk = (3 * D) // C
    T = _pick(L, 256)
    hb = T // CONV_HALO

    def body(u_ref, halo_ref, w_ref, b_ref, o_ref):
        i = pl.program_id(0)
        u = u_ref[...]
        xe = jnp.concatenate([jnp.where(i > 0, halo_ref[...], 0.0), u], axis=0)
        acc = b_ref[...] + w_ref[CONV_K - 1:CONV_K, :] * u
        for k in range(CONV_K - 1):
            acc = acc + w_ref[k:k + 1, :] * pltpu.roll(xe, CONV_K - 1 - k, 0)[CONV_HALO:, :]
        o_ref[...] = acc

    return pl.pallas_call(
        body, grid=(L // T,),
        in_specs=[pl.BlockSpec((T, C), lambda i: (i, cblk)),
                  pl.BlockSpec((CONV_HALO, C), lambda i: (jnp.maximum(i * hb - 1, 0), cblk)),
                  pl.BlockSpec((CONV_K, C), lambda i: (0, 0)),
                  pl.BlockSpec((1, C), lambda i: (0, 0))],
        out_specs=pl.BlockSpec((T, C), lambda i: (i, 0)),
        out_shape=_sds((L, C), F32), name="conv_fwd", compiler_params=_params("parallel"))(proj, proj, cw, cb)


def conv_bwd(dparts, proj, cw, dproj, D):
    L = proj.shape[0]
    C = cw.shape[1]
    cblk = (3 * D) // C
    T = _pick(L, 256)
    hb = T // CONV_HALO
    nT = L // T
    widths = [p.shape[1] for p in dparts]
    assert sum(widths) == C
    n = len(dparts)

    def body(*refs):
        d_refs, dn_refs = refs[:n], refs[n:2 * n]
        u_ref, up_ref, w_ref, old_ref, dr_ref, dw_ref, db_ref = refs[2 * n:]
        i = pl.program_id(0)

        @pl.when(i == 0)
        def _():
            dw_ref[...] = jnp.zeros_like(dw_ref)
            db_ref[...] = jnp.zeros_like(db_ref)

        at = 0
        for d_ref, dn_ref, wd in zip(d_refs, dn_refs, widths):
            sl = slice(at, at + wd)
            at += wd
            d = d_ref[...]
            u = u_ref[:, sl]
            de = jnp.concatenate([d, jnp.where(i < nT - 1, dn_ref[...], 0.0)], axis=0)
            ue = jnp.concatenate([jnp.where(i > 0, up_ref[:, sl], 0.0), u], axis=0)
            acc = w_ref[CONV_K - 1:CONV_K, sl] * d
            dw_ref[CONV_K - 1:CONV_K, sl] += jnp.sum(d * u, axis=0, keepdims=True)
            for k in range(CONV_K - 1):
                sh = CONV_K - 1 - k
                acc = acc + w_ref[k:k + 1, sl] * pltpu.roll(de, T + CONV_HALO - sh, 0)[:T, :]
                dw_ref[k:k + 1, sl] += jnp.sum(d * pltpu.roll(ue, sh, 0)[CONV_HALO:, :], axis=0, keepdims=True)
            dr_ref[:, sl] = acc.astype(dr_ref.dtype)
            db_ref[:, sl] += jnp.sum(d, axis=0, keepdims=True)

    nxt = lambda i: jnp.minimum((i + 1) * hb, L // CONV_HALO - 1)
    return pl.pallas_call(
        body, grid=(nT,),
        in_specs=[pl.BlockSpec((T, wd), lambda i: (i, 0)) for wd in widths]
        + [pl.BlockSpec((CONV_HALO, wd), lambda i: (nxt(i), 0)) for wd in widths]
        + [pl.BlockSpec((T, C), lambda i: (i, cblk)),
           pl.BlockSpec((CONV_HALO, C), lambda i: (jnp.maximum(i * hb - 1, 0), cblk)),
           pl.BlockSpec((CONV_K, C), lambda i: (0, 0)), _ANY],
        out_specs=[pl.BlockSpec((T, C), lambda i: (i, cblk)),
                   pl.BlockSpec((8, C), lambda i: (0, 0)),
                   pl.BlockSpec((1, C), lambda i: (0, 0))],
        out_shape=[_sds(dproj.shape, dproj.dtype), _sds((8, C), F32), _sds((1, C), F32)],
        input_output_aliases={2 * n + 3: 0},
        name="conv_bwd", compiler_params=_params("arbitrary"))(*dparts, *dparts, proj, proj, cw, dproj)


def _softplus(v):
    y = jnp.exp(-jnp.abs(v))
    u = 1.0 + y
    log1p = jnp.where(u == 1.0, y, jnp.log(u) * y / jnp.where(u == 1.0, 1.0, u - 1.0))
    return jnp.maximum(v, 0.0) + log1p


def dt_prep(dtraw, bias, alog, expand, D):
    L = dtraw.shape[0]
    GC = D // GROUPS
    HPG = GC // HEAD_DIM
    Q = _pick(L, SCAN_CHUNK)
    nc = L // Q

    def body(r_ref, b_ref, a_ref, e_ref, dt_ref, acs_ref, acst_ref, dtx_ref, eax_ref, dsx_ref, cdx_ref):
        valid = lax.broadcasted_iota(jnp.int32, (1, LANES), 1) < HPG
        dt = jnp.where(valid, _softplus(r_ref[...] + b_ref[...]), 0.0)
        adt = dt * -jnp.exp(a_ref[...])
        tril = (_row(0, (Q, Q)) >= lax.broadcasted_iota(jnp.int32, (Q, Q), 1)).astype(F32)
        acs = _dot_exact(tril, adt)
        last = acs[Q - 1:Q, :]
        dt_ref[...] = dt
        acs_ref[...] = acs
        acst_ref[...] = acs.T
        e = e_ref[...]
        dtx_ref[...] = _dot_exact(dt, e)
        eax_ref[...] = jnp.exp(_dot_exact(acs, e))
        dsx_ref[...] = jnp.exp(_dot_exact(last - acs, e))
        cdx_ref[0] = jnp.exp(_dot_exact(jnp.broadcast_to(last, (8, LANES)), e))

    head = pl.BlockSpec((Q, LANES), lambda g, c: (c, g))
    hvec = pl.BlockSpec((1, LANES), lambda g, c: (0, g))
    chan = pl.BlockSpec((Q, GC), lambda g, c: (c, g))
    return pl.pallas_call(
        body, grid=(GROUPS, nc),
        in_specs=[head, hvec, hvec, pl.BlockSpec((LANES, GC), lambda g, c: (0, 0))],
        out_specs=[head, head, pl.BlockSpec((LANES, Q), lambda g, c: (g, c)), chan, chan, chan,
                   pl.BlockSpec((1, 8, GC), lambda g, c: (c, 0, g))],
        out_shape=[_sds((L, GROUPS * LANES), F32), _sds((L, GROUPS * LANES), F32), _sds((GROUPS * LANES, L), F32),
                   _sds((L, D), F32), _sds((L, D), F32), _sds((L, D), F32), _sds((nc, 8, D), F32)],
        name="dt_prep", compiler_params=_params("parallel", "parallel"))(dtraw, bias, alog, expand)


def _scan_specs(L, D, Q, rev):
    GC = D // GROUPS
    nc = L // Q
    ci = (lambda c: nc - 1 - c) if rev else (lambda c: c)
    return dict(
        xs=pl.BlockSpec((Q, GC), lambda g, c: (ci(c), g)),
        b=pl.BlockSpec((Q, STATE), lambda g, c: (ci(c), D // STATE + g)),
        c=pl.BlockSpec((Q, STATE), lambda g, c: (ci(c), D // STATE + GROUPS + g)),
        chan=pl.BlockSpec((Q, GC), lambda g, c: (ci(c), g)),
        cdx=pl.BlockSpec((1, 8, GC), lambda g, c: (ci(c), 0, g)),
        head=pl.BlockSpec((Q, LANES), lambda g, c: (ci(c), g)),
        headt=pl.BlockSpec((LANES, Q), lambda g, c: (g, ci(c))),
        state=pl.BlockSpec((1, 1, STATE, GC), lambda g, c: (ci(c), g, 0, 0)),
        hvec=pl.BlockSpec((1, LANES), lambda g, c: (0, g)),
        cvec=pl.BlockSpec((1, GC), lambda g, c: (0, g)))


def scan_fwd(pre, dtx, eax, dsx, cdx, acs, acst, D):
    L = pre.shape[0]
    GC = D // GROUPS
    Q = _pick(L, SCAN_CHUNK)
    nc = L // Q
    sp = _scan_specs(L, D, Q, False)

    def body(xs_ref, b_ref, c_ref, dtx_ref, eax_ref, dsx_ref, cdx_ref, acs_ref, acst_ref, y_ref, st_ref, s_scr):
        @pl.when(pl.program_id(1) == 0)
        def _():
            s_scr[...] = jnp.zeros_like(s_scr)

        tri = _row(0, (Q, Q)) >= lax.broadcasted_iota(jnp.int32, (Q, Q), 1)
        half = lax.broadcasted_iota(jnp.int32, (1, LANES), 1) // HEAD_DIM
        xs, _ = _silu_and_grad(xs_ref[...])
        bg = _silu_and_grad(b_ref[...])[0].astype(BF16)
        cg = _silu_and_grad(c_ref[...])[0].astype(BF16)
        xdt = xs * dtx_ref[...]
        sprev = s_scr[...]
        st_ref[0, 0] = sprev
        sc = _dot(cg, bg, NT)
        yoff = _dot(cg, sprev.astype(BF16)) * eax_ref[...]
        for j in range(GC // LANES):
            ps = slice(j * LANES, (j + 1) * LANES)
            xp = xdt[:, ps]
            acc = yoff[:, ps]
            for hh in range(2):
                h = 2 * j + hh
                lm = jnp.exp(jnp.where(tri, acs_ref[:, h:h + 1] - acst_ref[h:h + 1, :], -1e30))
                xm = jnp.where(half == hh, xp, 0.0).astype(BF16)
                acc = acc + _dot((sc * lm).astype(BF16), xm)
            y_ref[:, ps] = acc
        xw = (xdt * dsx_ref[...]).astype(BF16)
        s_scr[...] = cdx_ref[0, 0:1, :] * sprev + _dot(bg, xw, TN)

    return pl.pallas_call(
        body, grid=(GROUPS, nc),
        in_specs=[sp["xs"], sp["b"], sp["c"], sp["chan"], sp["chan"], sp["chan"], sp["cdx"], sp["head"], sp["headt"]],
        out_specs=[sp["chan"], sp["state"]],
        out_shape=[_sds((L, D), F32), _sds((nc, GROUPS, STATE, GC), F32)],
        scratch_shapes=[pltpu.VMEM((STATE, GC), F32)], name="scan_fwd",
        compiler_params=_params("parallel", "arbitrary"))(pre, pre, pre, dtx, eax, dsx, cdx, acs, acst)


def scan_bwd(pre, dtx, eax, dsx, cdx, acs, acst, dt, dtraw, bias, alog, states, dy, dexp, collapse, D, jobs=()):
    L = pre.shape[0]
    GC = D // GROUPS
    Q = _pick(L, SCAN_CHUNK)
    nc = L // Q
    sp = _scan_specs(L, D, Q, True)
    rc = lambda c: nc - 1 - c

    def body(xs_ref, b_ref, c_ref, dtx_ref, eax_ref, dsx_ref, cdx_ref, acs_ref, acst_ref, dt_ref, raw_ref,
             bias_ref, alog_ref, st_ref, dy_ref, dexp_ref, col_ref,
             dxs_ref, db_ref, dc_ref, ddt_ref, dal_ref, dbi_ref, ds_scr, dx_scr):
        first = pl.program_id(1) == 0

        @pl.when(first)
        def _():
            ds_scr[...] = jnp.zeros_like(ds_scr)
            dal_ref[...] = jnp.zeros_like(dal_ref)
            dbi_ref[...] = jnp.zeros_like(dbi_ref)

        li = _row(0, (Q, Q))
        si = lax.broadcasted_iota(jnp.int32, (Q, Q), 1)
        lane = lax.broadcasted_iota(jnp.int32, (1, LANES), 1)
        half = lane // HEAD_DIM
        xs_pre, b_pre, c_pre = xs_ref[...], b_ref[...], c_ref[...]
        xs, xs_g = _silu_and_grad(xs_pre)
        bf, b_g = _silu_and_grad(b_pre)
        cf, c_g = _silu_and_grad(c_pre)
        bg, cg = bf.astype(BF16), cf.astype(BF16)
        dtx, eax, dsx = dtx_ref[...], eax_ref[...], dsx_ref[...]
        cd = cdx_ref[0, 0:1, :]
        xdt = xs * dtx
        G = dy_ref[...]
        prev = st_ref[0, 0]
        dsn = ds_scr[...]
        prev_b, dsn_b = prev.astype(BF16), dsn.astype(BF16)
        cp = _dot(cg, prev_b)
        ge_b = (G * eax).astype(BF16)
        d_c = _dot(ge_b, prev_b, NT)
        dprev = _dot(cg, ge_b, TN) + cd * dsn
        chan_a = G * cp * eax
        xw_b = (xdt * dsx).astype(BF16)
        dcd = jnp.sum(prev * dsn, axis=0, keepdims=True)
        d_b = _dot(xw_b, dsn_b, NT)
        dxw = _dot(bg, dsn_b)
        dd = dxw * xdt * dsx
        chan_a = chan_a - dd
        last_c = jnp.sum(dd, axis=0, keepdims=True) + dcd * cd
        sc = _dot(cg, bg, NT)
        sct = _dot(bg, cg, NT)
        dsc = jnp.zeros((Q, Q), F32)
        dsct = jnp.zeros((Q, Q), F32)
        dacs = jnp.zeros((Q, LANES), F32)
        for j in range(GC // LANES):
            ps = slice(j * LANES, (j + 1) * LANES)
            xp, gp = xdt[:, ps], G[:, ps]
            dxp = dxw[:, ps] * dsx[:, ps]
            for hh in range(2):
                h = 2 * j + hh
                col, row = acs_ref[:, h:h + 1], acst_ref[h:h + 1, :]
                lm = jnp.exp(jnp.where(li >= si, col - row, -1e30))
                lmt = jnp.exp(jnp.where(si >= li, row - col, -1e30))
                m, mt = sc * lm, sct * lmt
                xm = jnp.where(half == hh, xp, 0.0).astype(BF16)
                gm = jnp.where(half == hh, gp, 0.0).astype(BF16)
                dm = _dot(gm, xm, NT)
                dmt = _dot(xm, gm, NT)
                dxp = dxp + _dot(mt.astype(BF16), gm)
                dsc = dsc + dm * lm
                dsct = dsct + dmt * lmt
                rs = jnp.sum(dm * m, axis=1, keepdims=True) - jnp.sum(dmt * mt, axis=1, keepdims=True)
                dacs = dacs + jnp.where(lane == h, rs, 0.0)
            dx_scr[:, ps] = dxp
        d_c = d_c + _dot(dsc.astype(BF16), bg)
        d_b = d_b + _dot(dsct.astype(BF16), cg)
        ds_scr[...] = dprev
        dxdt = dx_scr[...]
        dxs_ref[...] = (dxdt * dtx + dexp_ref[...] * G) * xs_g
        db_ref[...] = d_b * b_g
        dc_ref[...] = d_c * c_g
        colm = col_ref[...]
        dacs = dacs + _dot_exact(chan_a, colm)
        dlast = _dot_exact(jnp.broadcast_to(last_c, (8, GC)), colm)[0:1, :]
        dacs = dacs + jnp.where(_row(0, (Q, 1)) == Q - 1, dlast, 0.0)
        dadt = _dot_exact((si >= li).astype(F32), dacs)
        a = -jnp.exp(alog_ref[...])
        dt = dt_ref[...]
        ddt = dadt * a + _dot_exact(dxdt * xs, colm)
        dal_ref[0:1, :] += jnp.sum(dadt * dt * a, axis=0, keepdims=True)
        draw = ddt * _sigmoid(raw_ref[...] + bias_ref[...])
        dbi_ref[0:1, :] += jnp.sum(draw, axis=0, keepdims=True)
        ddt_ref[...] = draw.astype(ddt_ref.dtype)

    acc = pl.BlockSpec((8, LANES), lambda g, c: (0, g))
    return _call(
        body, grid=(GROUPS, nc),
        in_specs=[sp["xs"], sp["b"], sp["c"], sp["chan"], sp["chan"], sp["chan"], sp["cdx"], sp["head"], sp["headt"],
                  sp["head"], sp["head"], sp["hvec"], sp["hvec"], sp["state"], sp["chan"], sp["cvec"],
                  pl.BlockSpec((GC, LANES), lambda g, c: (0, 0))],
        out_specs=[sp["chan"],
                   pl.BlockSpec((Q, STATE), lambda g, c: (rc(c), g)),
                   pl.BlockSpec((Q, STATE), lambda g, c: (rc(c), g)),
                   sp["head"], acc, acc],
        out_shape=[_sds((L, D), F32), _sds((L, GROUPS * STATE), F32), _sds((L, GROUPS * STATE), F32),
                   _sds((L, GROUPS * LANES), BF16), _sds((8, GROUPS * LANES), F32), _sds((8, GROUPS * LANES), F32)],
        scratch_shapes=[pltpu.VMEM((STATE, GC), F32), pltpu.VMEM((Q, GC), F32)], name="scan_bwd",
        sem=("parallel", "arbitrary"),
        args=(pre, pre, pre, dtx, eax, dsx, cdx, acs, acst, dt, dtraw, bias, alog, states, dy, dexp, collapse),
        jobs=jobs)


def gate_fwd(y, pre, proj, dexp, nw, mixed, D):
    L = y.shape[0]
    GC = D // GROUPS
    T = _pick(L, 256)

    def body(y_ref, xs_ref, z_ref, de_ref, nw_ref, old_ref, o_ref):
        xs, _ = _silu_and_grad(xs_ref[...])
        z = z_ref[...]
        y3 = (y_ref[...] + de_ref[...] * xs) * (z * _sigmoid(z))
        for g in range(GROUPS):
            sl = slice(g * GC, (g + 1) * GC)
            yg = y3[:, sl]
            r = lax.rsqrt(jnp.mean(yg * yg, axis=-1, keepdims=True) + NORM_EPS)
            o_ref[:, sl] = (yg * r * nw_ref[:, sl]).astype(o_ref.dtype)

    row = lambda c: pl.BlockSpec((T, D), lambda i: (i, c))
    vec = pl.BlockSpec((1, D), lambda i: (0, 0))
    return pl.pallas_call(
        body, grid=(L // T,), in_specs=[row(0), row(0), row(2), vec, vec, _ANY], out_specs=row(1),
        out_shape=_sds(mixed.shape, mixed.dtype), input_output_aliases={5: 0}, name="gate_fwd",
        compiler_params=_params("parallel"))(y, pre, proj, dexp, nw, mixed)


def gate_bwd(y, pre, proj, dexp, nw, dmix, D):
    L = y.shape[0]
    GC = D // GROUPS
    T = _pick(L, 256)

    def body(y_ref, xs_ref, z_ref, de_ref, nw_ref, dm_ref, dy_ref, dz_ref, dnw_ref, dde_ref):
        @pl.when(pl.program_id(0) == 0)
        def _():
            dnw_ref[...] = jnp.zeros_like(dnw_ref)
            dde_ref[...] = jnp.zeros_like(dde_ref)

        xs, _ = _silu_and_grad(xs_ref[...])
        sz, dsz = _silu_and_grad(z_ref[...])
        y2 = y_ref[...] + de_ref[...] * xs
        y3 = y2 * sz
        for g in range(GROUPS):
            sl = slice(g * GC, (g + 1) * GC)
            yg, dm = y3[:, sl], dm_ref[:, sl]
            r = lax.rsqrt(jnp.mean(yg * yg, axis=-1, keepdims=True) + NORM_EPS)
            n = yg * r
            gg = dm * nw_ref[:, sl]
            dy3 = r * (gg - n * jnp.mean(gg * n, axis=-1, keepdims=True))
            dnw_ref[:, sl] += jnp.sum(dm * n, axis=0, keepdims=True)
            dy2 = dy3 * sz[:, sl]
            dy_ref[:, sl] = dy2
            dz_ref[:, sl] = (dy3 * y2[:, sl] * dsz[:, sl]).astype(dz_ref.dtype)
            dde_ref[:, sl] += jnp.sum(dy2 * xs[:, sl], axis=0, keepdims=True)

    row = lambda c: pl.BlockSpec((T, D), lambda i: (i, c))
    vec = pl.BlockSpec((1, D), lambda i: (0, 0))
    return pl.pallas_call(
        body, grid=(L // T,), in_specs=[row(0), row(0), row(2), vec, vec, row(1)],
        out_specs=[row(0), row(2), vec, vec],
        out_shape=[_sds((L, D), F32), _sds((L, proj.shape[1]), BF16), _sds((1, D), F32), _sds((1, D), F32)],
        name="gate_bwd", compiler_params=_params("arbitrary"))(y, pre, proj, dexp, nw, dmix)


def _adam_math(gv, w, m, v):
    c1 = 1.0 - ADAM_B1 ** ADAM_STEP
    c2 = 1.0 - ADAM_B2 ** ADAM_STEP
    nm = ADAM_B1 * m + (1.0 - ADAM_B1) * gv
    nv = ADAM_B2 * v + (1.0 - ADAM_B2) * (gv * gv)
    return -ADAM_LR * ((nm / c1) / (jnp.sqrt(nv / c2) + ADAM_EPS) + ADAM_WD * w), nm, nv


def adamw(g, w, m, v, name):
    R, C = g.shape
    T = R if R <= 128 else 128
    assert R % T == 0

    def body(g_ref, w_ref, m_ref, v_ref, d_ref, nm_ref, nv_ref):
        d_ref[...], nm_ref[...], nv_ref[...] = _adam_math(g_ref[...], w_ref[...], m_ref[...], v_ref[...])

    blk = pl.BlockSpec((T, C), lambda i: (i, 0))
    return pl.pallas_call(
        body, grid=(R // T,), in_specs=[blk] * 4, out_specs=[blk] * 3,
        out_shape=[_sds((R, C), F32)] * 3, name=name, compiler_params=_params("parallel"))(g, w, m, v)


def adamw_layer(g, w, m, v, layer, prev, name):
    R, C = g.shape
    T = _pick(R, 128)
    nb = R // T
    assert R % T == 0

    def body(g_ref, w_ref, m_ref, v_ref, *rest):
        go_ref, d_ref, nm_ref, nv_ref = rest[-4:]
        gv = g_ref[...]
        go_ref[...] = gv
        d_ref[...], nm_ref[...], nv_ref[...] = _adam_math(gv, w_ref[...], m_ref[...], v_ref[...])

    mine = pl.BlockSpec((T, C), lambda i: (layer * nb + i, 0))
    prev = list(prev or [])
    return pl.pallas_call(
        body, grid=(nb,), in_specs=[pl.BlockSpec((T, C), lambda i: (i, 0)), mine, mine, mine] + [_ANY] * len(prev),
        out_specs=[mine] * 4, out_shape=[_sds(w.shape, F32)] * 4,
        input_output_aliases={4 + i: i for i in range(len(prev))}, name=name,
        compiler_params=_params("parallel"))(g, w, m, v, *prev)


def _to_groups(v, hpg):
    lead = v.shape[:-1]
    t = v.reshape(lead + (GROUPS, hpg))
    t = jnp.pad(t, [(0, 0)] * (len(lead) + 1) + [(0, LANES - hpg)])
    return t.reshape(lead + (GROUPS * LANES,))


def _from_groups(a, hpg):
    lead = a.shape[:-1]
    return a.reshape(lead + (GROUPS, LANES))[..., :hpg].reshape(lead + (GROUPS * hpg,))


def _expand_matrix(D):
    gc = D // GROUPS
    return (jnp.arange(LANES)[:, None] == (jnp.arange(gc)[None, :] // HEAD_DIM)).astype(F32)


def layer_params(pre_w, w_in_full, mixw, scale, cw, cb, bias, alog, dskip, nw, w_out_full, post_w, D):
    hpg = D // GROUPS // HEAD_DIM
    main = w_in_full.shape[1] - GROUPS * hpg
    return dict(
        pre_w=pre_w[None], wmain=w_in_full[:, :main], wdt=_to_groups(w_in_full[:, main:], hpg), mixw=mixw,
        scale=scale[None], cw=cw, cb=cb[None], bias=_to_groups(bias, hpg)[None], alog=_to_groups(alog, hpg)[None],
        dexp=jnp.repeat(dskip, HEAD_DIM)[None], nw=nw[None], wout=w_out_full, post_w=post_w[None])


def layer_fwd(x, p, D, next_shards=None):
    send_in = [gather_send_job(next_shards[:1])] if next_shards else []
    h = rms_fwd(x, p["pre_w"])
    proj, got = matmul(h, p["wmain"], "nn", F32, "proj", jobs=send_in)
    dtraw, _ = matmul(h, p["wdt"], "nn", F32, "dtproj")
    mixed, pooled = pool_fwd(proj, p["mixw"], p["scale"], D)
    pre = conv_fwd(proj, p["cw"], p["cb"], D)
    dtp = dt_prep(dtraw, p["bias"], p["alog"], _expand_matrix(D), D)
    dt, acs, acst, dtx, eax, dsx, cdx = dtp
    y, states = scan_fwd(pre, dtx, eax, dsx, cdx, acs, acst, D)
    mixed = gate_fwd(y, pre, proj, p["dexp"], p["nw"], mixed, D)
    jobs = [gather_send_job(next_shards[1:]), gather_pass_job(got[0])] if next_shards else []
    out, got = matmul(mixed, p["wout"], "nn", F32, "outproj", jobs=jobs)
    xn, got2 = post_fwd(x, out, p["post_w"], jobs=[gather_pass_job(got[0])] if next_shards else [])
    gathered = got[1] + got2[0] if next_shards else None
    return xn, dict(x=x, h=h, proj=proj, dtraw=dtraw, pooled=pooled, pre=pre, dtp=dtp, y=y, states=states,
                    mixed=mixed, out=out), gathered


def layer_bwd(dxn, p, s, D, pending=None, chip=None, core=None):
    hpg = D // GROUPS // HEAD_DIM
    dt, acs, acst, dtx, eax, dsx, cdx = s["dtp"]
    dout, d_post = post_bwd(s["out"], p["post_w"], dxn)
    dmix, got = matmul(dout, p["wout"], "nt", F32, "dmixed", jobs=[pair_exchange_job(pending)] if pending else [])
    pair = [pair_add(g, r, core, BF16) for g, r in zip(pending, got[0])] if pending else None
    d_wout, got_b = matmul(s["mixed"], dout, "tn", F32, "dwout",
                           jobs=[chip_exchange_job(pair[1:])] if pending else [])
    dy2, dproj, d_nw, d_dexp = gate_bwd(s["y"], s["pre"], s["proj"], p["dexp"], p["nw"], dmix, D)
    (dxs, db, dc, ddtraw, d_alog, d_bias), got_a = scan_bwd(
        s["pre"], dtx, eax, dsx, cdx, acs, acst, dt, s["dtraw"], p["bias"], p["alog"], s["states"], dy2,
        p["dexp"], _expand_matrix(D).T, D, jobs=[chip_exchange_job(pair[:1])] if pending else [])
    mine = [chip_add(q, r, chip, core) for q, r in zip(pair, got_a[0] + got_b[0])] if pending else None
    dproj, d_cw, d_cb = conv_bwd([dxs, db, dc], s["proj"], p["cw"], dproj, D)
    dproj, dmixed, d_scale = pool_bwd(s["proj"], dmix, s["pooled"], p["mixw"], p["scale"], dproj, D)
    d_mixw = pool_dw(s["pooled"], dmixed, D)
    dh_a, got = matmul(dproj, p["wmain"], "nt", F32, "dh_main", jobs=[pair_gather_job(mine)] if pending else [])
    dh_b, _ = matmul(ddtraw, p["wdt"], "nt", F32, "dh_dt")
    d_wmain, _ = matmul(s["h"], dproj, "tn", F32, "dwmain")
    d_wdt, _ = matmul(s["h"], ddtraw, "tn", F32, "dwdt")
    dx, d_pre = rms_bwd(s["x"], p["pre_w"], dh_a, dh_b, dxn)
    grads = dict(
        pre_norm_w=d_pre[0], w_in=jnp.concatenate([d_wmain, _from_groups(d_wdt, hpg)], axis=1), pool_mix_w=d_mixw,
        pool_scale=d_scale[0], conv_w=d_cw[:CONV_K], conv_b=d_cb[0], dt_bias=_from_groups(d_bias[0], hpg),
        a_log=_from_groups(d_alog[0], hpg), d_skip=d_dexp[0].reshape(-1, HEAD_DIM).sum(axis=-1),
        ssd_norm_w=d_nw[0], w_out=d_wout, post_norm_w=d_post[0])
    return dx, grads, (got[0] if pending else None)


def reduce_scatter(pending, chip, core):
    recv, = run_jobs([pair_exchange_job(pending)], "pair_exchange")
    pair = [pair_add(g, r, core, BF16) for g, r in zip(pending, recv)]
    recv, = run_jobs([chip_exchange_job(pair)], "chip_exchange")
    return run_jobs([pair_gather_job([chip_add(q, r, chip, core) for q, r in zip(pair, recv)])], "pair_gather")[0]


def local_step(x, target, params, D):
    saved = []
    for p in params:
        x, s, _ = layer_fwd(x, p, D)
        saved.append(s)
    dx, sumsq = loss_head(x, target)
    grads = [None] * len(params)
    for l in reversed(range(len(params))):
        dx, grads[l], _ = layer_bwd(dx, params[l], saved[l], D)
    return sumsq, dx, grads


SMALL = ("pre_norm_w", "pool_scale", "conv_w", "conv_b", "dt_bias", "a_log", "d_skip", "ssd_norm_w", "post_norm_w")
BIG = ("w_in", "w_out", "pool_mix_w")


def _pack(parts):
    flat = jnp.concatenate([p.reshape(-1) for p in parts])
    n = flat.shape[0]
    rows = -(-n // (LANES * LANES)) * LANES
    return jnp.pad(flat, (0, rows * LANES - n)).reshape(rows, LANES)


def _unpack(packed, shapes):
    flat, out, at = packed.reshape(-1), [], 0
    for s in shapes:
        n = math.prod(s)
        out.append(flat[at:at + n].reshape(s))
        at += n
    return out


def kernel(x, pre_norm_w, w_in, pool_mix_w, pool_scale, conv_w, conv_b, dt_bias, a_log, d_skip, ssd_norm_w, w_out, post_norm_w, loss_target, m_pre_norm_w, m_w_in, m_pool_mix_w, m_pool_scale, m_conv_w, m_conv_b, m_dt_bias, m_a_log, m_d_skip, m_ssd_norm_w, m_w_out, m_post_norm_w, v_pre_norm_w, v_w_in, v_pool_mix_w, v_pool_scale, v_conv_w, v_conv_b, v_dt_bias, v_a_log, v_d_skip, v_ssd_norm_w, v_w_out, v_post_norm_w):
    NL, D, SH = w_in.shape
    PGW = D // GROUPS
    CS = conv_w.shape[2]
    chip = (2 * lax.axis_index("x") + lax.axis_index("y")).astype(jnp.int32)
    chip1, core = chip.reshape(1), lax.axis_index("c").astype(jnp.int32).reshape(1)

    def shards(l):
        return [w_in[l].astype(BF16).reshape(2, D // 2, SH), w_out[l].astype(BF16).reshape(2, D // 4, D),
                pool_mix_w[l].astype(BF16).reshape(2, GROUPS * PGW // 8, PGW),
                conv_w[l].reshape(2, CONV_K * CS // (2 * LANES), LANES)]

    def params(l, g):
        g_in, g_out, g_mix, g_cw = g
        w_in_l = jnp.concatenate([g_in[k].reshape(D, SH) for k in range(4)], axis=1)
        mix_l = g_mix.reshape(4, GROUPS, PGW // 4, PGW).transpose(1, 0, 2, 3).reshape(GROUPS, PGW, PGW)
        cw_l = g_cw.reshape(4, CONV_K, CS).transpose(1, 0, 2).reshape(CONV_K, 4 * CS)
        return layer_params(pre_norm_w[l], w_in_l, mix_l, pool_scale[l], cw_l, conv_b[l], dt_bias[l], a_log[l],
                            d_skip[l], ssd_norm_w[l], g_out.reshape(2 * D, D), post_norm_w[l], D)

    def slotted(g):
        return [g["w_in"].reshape(D, 4, SH).transpose(1, 0, 2).reshape(4, 2, D // 2, SH),
                g["w_out"].reshape(4, 2, D // 4, D),
                g["pool_mix_w"].reshape(GROUPS, 4, PGW // 4, PGW).transpose(1, 0, 2, 3).reshape(
                    4, 2, GROUPS * PGW // 8, PGW)]

    gathered = run_jobs([gather_pass_job(run_jobs([gather_send_job(shards(0))], "gather_send")[0])], "gather_pass")[0]
    h, ps, saved = x[0], [], []
    for l in range(NL):
        ps.append(params(l, gathered))
        h, s, gathered = layer_fwd(h, ps[l], D, shards(l + 1) if l + 1 < NL else None)
        saved.append(s)
    dx, sumsq = loss_head(h, loss_target[0])

    given = dict(w_in=(w_in, m_w_in, v_w_in), w_out=(w_out, m_w_out, v_w_out),
                 pool_mix_w=(pool_mix_w, m_pool_mix_w, v_pool_mix_w))
    flat = {n: [t.reshape(-1, t.shape[-1]) for t in given[n]] for n in BIG}
    done = {n: None for n in BIG}

    def update(l, reduced):
        for n, r in zip(BIG, reduced):
            done[n] = adamw_layer(r.reshape(-1, r.shape[-1]), *flat[n], l, done[n], "adamw_" + n)

    grads, pending = [None] * NL, None
    for l in reversed(range(NL)):
        dx, grads[l], reduced = layer_bwd(dx, ps[l], saved[l], D, pending, chip1, core)
        if pending:
            update(l + 1, reduced)
        pending = slotted(grads[l])
    update(0, reduce_scatter(pending, chip1, core))

    small_shapes = [(NL,) + grads[0][n].shape for n in SMALL]
    packed = _pack([0.5 / D * sumsq[0, :1]] + [jnp.stack([g[n] for g in grads]) for n in SMALL])
    total = allreduce_small(packed)
    loss, *small = _unpack(total, [(1,)] + small_shapes)
    small = dict(zip(SMALL, small))
    small["conv_w"] = lax.dynamic_slice_in_dim(small["conv_w"], chip * CS, CS, axis=2)

    given_small = dict(
        pre_norm_w=(pre_norm_w, m_pre_norm_w, v_pre_norm_w), pool_scale=(pool_scale, m_pool_scale, v_pool_scale),
        conv_w=(conv_w, m_conv_w, v_conv_w), conv_b=(conv_b, m_conv_b, v_conv_b),
        dt_bias=(dt_bias, m_dt_bias, v_dt_bias), a_log=(a_log, m_a_log, v_a_log),
        d_skip=(d_skip, m_d_skip, v_d_skip), ssd_norm_w=(ssd_norm_w, m_ssd_norm_w, v_ssd_norm_w),
        post_norm_w=(post_norm_w, m_post_norm_w, v_post_norm_w))
    shapes = [given_small[n][0].shape for n in SMALL]
    upd = adamw(_pack([small[n] for n in SMALL]), *[_pack([given_small[n][i] for n in SMALL]) for i in range(3)],
                "adamw_small")
    upd = [dict(zip(SMALL, _unpack(u, shapes))) for u in upd]

    out = {n: (small[n], upd[0][n], upd[1][n], upd[2][n]) for n in SMALL}
    for n in BIG:
        out[n] = tuple(t.reshape(given[n][0].shape) for t in done[n])

    order = ("pre_norm_w", "w_in", "pool_mix_w", "pool_scale", "conv_w", "conv_b", "dt_bias", "a_log", "d_skip",
             "ssd_norm_w", "w_out", "post_norm_w")
    return (loss.reshape(()), dx[None], *[out[n][0] for n in order], *[out[n][1] for n in order],
            *[out[n][2] for n in order], *[out[n][3] for n in order])
```

```python
import functools
import math

import jax
import jax.numpy as jnp
from jax import lax
from jax.experimental import pallas as pl
from jax.experimental.pallas import tpu as pltpu

F32 = jnp.float32
BF16 = jnp.bfloat16

NORM_EPS = 1e-6
HEAD_DIM = 64
STATE = 128
GROUPS = 4
POOL_WINDOWS = (2, 4, 8, 16)
POOL_HALO = 16
CONV_K = 4
CONV_HALO = 8
SCAN_CHUNK = 256
LANES = 128
VMEM_LIMIT = 52 * 1024 * 1024

ADAM_LR = 0.001
ADAM_B1 = 0.9
ADAM_B2 = 0.999
ADAM_EPS = 1e-08
ADAM_WD = 0.01
ADAM_STEP = 10

MESH = pl.DeviceIdType.MESH

NN = (((1,), (0,)), ((), ()))
NT = (((1,), (1,)), ((), ()))
TN = (((0,), (0,)), ((), ()))

_ANY = pl.BlockSpec(memory_space=pl.ANY)


def _params(*sem):
    return pltpu.CompilerParams(dimension_semantics=sem, vmem_limit_bytes=VMEM_LIMIT)


def _pick(dim, pref):
    if dim <= pref:
        return dim
    t = (pref // LANES) * LANES
    while t > LANES and dim % t:
        t -= LANES
    assert dim % t == 0, (dim, pref)
    return t


def _dot(a, b, dn=NN):
    return lax.dot_general(a, b, dn, preferred_element_type=F32)


def _split3(a):
    hi = a.astype(BF16)
    r = a - hi.astype(F32)
    mid = r.astype(BF16)
    return hi, mid, (r - mid.astype(F32)).astype(BF16)


def _dot_sel(a, e):
    hi, mid, lo = _split3(a)
    return (_dot(lo, e) + _dot(mid, e)) + _dot(hi, e)


def _sel_dot(e, b):
    hi, mid, lo = _split3(b)
    return (_dot(e, lo) + _dot(e, mid)) + _dot(e, hi)


def _sigmoid(v):
    return 1.0 / (1.0 + jnp.exp(-v))


def _silu_and_grad(v):
    s = _sigmoid(v)
    return v * s, s * (1.0 + v * (1.0 - s))


def _row(i, shape):
    return lax.broadcasted_iota(jnp.int32, shape, 0) + i


def _sds(shape, dtype):
    return jax.ShapeDtypeStruct(tuple(shape), dtype)


class Job:
    def __init__(self, ins, outs, aliased, nsem, start, finish):
        self.ins, self.outs, self.aliased, self.nsem, self.start, self.finish = ins, outs, aliased, nsem, start, finish


def _place():
    x, y, c = lax.axis_index("x"), lax.axis_index("y"), lax.axis_index("c")
    return x, y, c, [(1 - x, y), (x, 1 - y), (1 - x, 1 - y)]


def _remote(src, dst, send_sem, recv_sem, device):
    return pltpu.make_async_remote_copy(src_ref=src, dst_ref=dst, send_sem=send_sem, recv_sem=recv_sem,
                                        device_id=device, device_id_type=MESH)


def gather_send_job(arrs):
    n = len(arrs)

    def copies(ins, outs, send, recv):
        x, y, c, chips = _place()
        mine = 2 * x + y
        out = []
        for a in range(n):
            out.append(_remote(ins[a], outs[a].at[mine], send.at[4 * a + 3], recv.at[4 * a + 3], (x, y, 1 - c)))
            for j, chip in enumerate(chips):
                out.append(_remote(ins[a].at[c], outs[a].at[mine, c], send.at[4 * a + j], recv.at[4 * a + j],
                                   (*chip, c)))
        return out

    def start(ins, outs, send, recv):
        for cp in copies(ins, outs, send, recv):
            cp.start()

    def finish(ins, outs, send, recv):
        x, y, c, chips = _place()
        for a in range(n):
            for j, chip in enumerate(chips):
                landed = outs[a].at[2 * chip[0] + chip[1], c]
                _remote(landed, landed, send.at[4 * a + j], recv.at[4 * a + j], (x, y, 1 - c)).wait_recv()
            twin = outs[a].at[2 * x + y]
            _remote(twin, twin, send.at[4 * a + 3], recv.at[4 * a + 3], (x, y, 1 - c)).wait_recv()
        for cp in copies(ins, outs, send, recv):
            cp.wait_send()

    return Job(list(arrs), [_sds((4,) + a.shape, a.dtype) for a in arrs], False, 4 * n, start, finish)


def gather_pass_job(bufs):
    n = len(bufs)

    def copies(outs, send, recv):
        x, y, c, chips = _place()
        out = []
        for a in range(n):
            for j, chip in enumerate(chips):
                landed = outs[a].at[2 * chip[0] + chip[1], c]
                out.append(_remote(landed, landed, send.at[3 * a + j], recv.at[3 * a + j], (x, y, 1 - c)))
        return out

    def start(ins, outs, send, recv):
        for cp in copies(outs, send, recv):
            cp.start()

    def finish(ins, outs, send, recv):
        x, y, c, chips = _place()
        for a in range(n):
            for j, chip in enumerate(chips):
                passed = outs[a].at[2 * chip[0] + chip[1], 1 - c]
                _remote(passed, passed, send.at[3 * a + j], recv.at[3 * a + j], (x, y, 1 - c)).wait_recv()
        for cp in copies(outs, send, recv):
            cp.wait_send()

    return Job(list(bufs), [_sds(b.shape, b.dtype) for b in bufs], True, 3 * n, start, finish)


def pair_exchange_job(arrs):
    n = len(arrs)

    def copies(ins, outs, send, recv):
        x, y, c, _ = _place()
        return [_remote(ins[a].at[:, 1 - c], outs[a], send.at[a], recv.at[a], (x, y, 1 - c)) for a in range(n)]

    def start(ins, outs, send, recv):
        for cp in copies(ins, outs, send, recv):
            cp.start()

    def finish(ins, outs, send, recv):
        for cp in copies(ins, outs, send, recv):
            cp.wait()

    return Job(list(arrs), [_sds(a.shape[:1] + a.shape[2:], a.dtype) for a in arrs], False, n, start, finish)


def chip_exchange_job(arrs):
    n = len(arrs)

    def copies(ins, outs, send, recv):
        x, y, c, chips = _place()
        return [_remote(ins[a].at[2 * chip[0] + chip[1]], outs[a].at[j], send.at[3 * a + j], recv.at[3 * a + j],
                        (*chip, c)) for a in range(n) for j, chip in enumerate(chips)]

    def start(ins, outs, send, recv):
        for cp in copies(ins, outs, send, recv):
            cp.start()

    def finish(ins, outs, send, recv):
        for cp in copies(ins, outs, send, recv):
            cp.wait()

    return Job(list(arrs), [_sds((3,) + a.shape[1:], a.dtype) for a in arrs], False, 3 * n, start, finish)


def pair_gather_job(bufs):
    n = len(bufs)

    def copies(outs, send, recv):
        x, y, c, _ = _place()
        return [_remote(outs[a].at[c], outs[a].at[c], send.at[a], recv.at[a], (x, y, 1 - c)) for a in range(n)]

    def start(ins, outs, send, recv):
        for cp in copies(outs, send, recv):
            cp.start()

    def finish(ins, outs, send, recv):
        for cp in copies(outs, send, recv):
            cp.wait()

    return Job(list(bufs), [_sds(b.shape, b.dtype) for b in bufs], True, n, start, finish)


def _call(body, *, grid, in_specs, out_specs, out_shape, name, sem, args, scratch_shapes=(), jobs=(), aliases=None):
    in_specs, out_specs, out_shape, scratch_shapes = list(in_specs), list(out_specs), list(out_shape), list(scratch_shapes)
    aliases = dict(aliases or {})
    n_in, n_out, n_scr = len(in_specs), len(out_specs), len(scratch_shapes)
    if jobs:
        sem = ("arbitrary",) * len(grid)
    at_in, at_out = n_in, n_out
    for j in jobs:
        if j.aliased:
            aliases.update({at_in + i: at_out + i for i in range(len(j.ins))})
        at_in, at_out = at_in + len(j.ins), at_out + len(j.outs)

    def wrapped(*refs):
        ins, p = refs[:n_in], n_in
        jins = []
        for j in jobs:
            jins.append(refs[p:p + len(j.ins)])
            p += len(j.ins)
        outs, p = refs[p:p + n_out], p + n_out
        jouts = []
        for j in jobs:
            jouts.append(refs[p:p + len(j.outs)])
            p += len(j.outs)
        scr, sems = refs[p:p + n_scr], refs[p + n_scr:]

        def start():
            for k, j in enumerate(jobs):
                j.start(jins[k], jouts[k], sems[2 * k], sems[2 * k + 1])

        def finish():
            for k, j in enumerate(jobs):
                j.finish(jins[k], jouts[k], sems[2 * k], sems[2 * k + 1])

        if jobs and grid:
            ids = [pl.program_id(d) for d in range(len(grid))]
            pl.when(functools.reduce(jnp.logical_and, [i == 0 for i in ids]))(start)
            body(*ins, *outs, *scr)
            pl.when(functools.reduce(jnp.logical_and, [i == g - 1 for i, g in zip(ids, grid)]))(finish)
        else:
            start()
            body(*ins, *outs, *scr)
            finish()

    kwargs = dict(grid=grid) if grid else {}
    res = pl.pallas_call(
        wrapped, in_specs=in_specs + [_ANY] * (at_in - n_in), out_specs=out_specs + [_ANY] * (at_out - n_out),
        out_shape=out_shape + [o for j in jobs for o in j.outs],
        scratch_shapes=scratch_shapes + [pltpu.SemaphoreType.DMA((j.nsem,)) for j in jobs for _ in range(2)],
        input_output_aliases=aliases, name=name,
        compiler_params=pltpu.CompilerParams(dimension_semantics=sem, vmem_limit_bytes=VMEM_LIMIT) if grid
        else pltpu.CompilerParams(vmem_limit_bytes=VMEM_LIMIT), **kwargs)(*args, *[a for j in jobs for a in j.ins])
    res = list(res)
    outs, rest, per_job = res[:n_out], res[n_out:], []
    for j in jobs:
        per_job.append(rest[:len(j.outs)])
        rest = rest[len(j.outs):]
    return outs, per_job


def run_jobs(jobs, name):
    return _call(lambda: None, grid=(), in_specs=[], out_specs=[], out_shape=[], name=name, sem=(), args=(), jobs=jobs)[1]


def pair_add(g, r, core, out_dtype):
    S, _, R, C = g.shape
    T = _pick(R, 256 if C <= 4096 else 128)
    assert R % T == 0

    def body(c_ref, g_ref, r_ref, o_ref):
        o_ref[0] = (g_ref[0, 0] + r_ref[0]).astype(o_ref.dtype)

    return pl.pallas_call(
        body,
        grid_spec=pltpu.PrefetchScalarGridSpec(
            num_scalar_prefetch=1, grid=(S, R // T),
            in_specs=[pl.BlockSpec((1, 1, T, C), lambda k, i, c_ref: (k, c_ref[0], i, 0)),
                      pl.BlockSpec((1, T, C), lambda k, i, c_ref: (k, i, 0))],
            out_specs=pl.BlockSpec((1, T, C), lambda k, i, c_ref: (k, i, 0))),
        out_shape=_sds((S, R, C), out_dtype), name="pair_add",
        compiler_params=_params("parallel", "parallel"))(core, g, r)


def chip_add(p, r, chip, core):
    _, R, C = p.shape
    T = _pick(R, 256)
    assert R % T == 0

    def body(k_ref, c_ref, p_ref, r0_ref, r1_ref, r2_ref, o_ref):
        o_ref[0] = ((p_ref[0].astype(F32) + r0_ref[0].astype(F32)) + r1_ref[0].astype(F32)) + r2_ref[0].astype(F32)

    slot = lambda j: pl.BlockSpec((1, T, C), lambda i, k_ref, c_ref: (j, i, 0))
    return pl.pallas_call(
        body,
        grid_spec=pltpu.PrefetchScalarGridSpec(
            num_scalar_prefetch=2, grid=(R // T,),
            in_specs=[pl.BlockSpec((1, T, C), lambda i, k_ref, c_ref: (k_ref[0], i, 0)), slot(0), slot(1), slot(2)],
            out_specs=pl.BlockSpec((1, T, C), lambda i, k_ref, c_ref: (c_ref[0], i, 0))),
        out_shape=_sds((2, R, C), F32), name="chip_add",
        compiler_params=_params("parallel"))(chip, core, p, r, r, r)


def allreduce_small(v):
    R = v.shape[0]

    def body(v_ref, o_ref, buf, send_sems, recv_sems, local_sem):
        x, y, c, chips = _place()
        me, sibling = (x, y, c), (x, y, 1 - c)

        def rows(px, py, pc):
            return buf.at[pl.ds((4 * px + 2 * py + pc) * R, R), :]

        def copy(k, block, to, src=None):
            return _remote(rows(*block) if src is None else src, rows(*block), send_sems.at[k], recv_sems.at[k], to)

        mine = pltpu.make_async_copy(v_ref, rows(*me), local_sem)
        mine.start()
        first = [copy(0, me, sibling, src=v_ref)]
        first += [copy(1 + j, me, (*chip, c), src=v_ref) for j, chip in enumerate(chips)]
        for cp in first:
            cp.start()
        passed = [copy(4 + j, (*chip, c), sibling) for j, chip in enumerate(chips)]
        for j, chip in enumerate(chips):
            copy(1 + j, (*chip, c), me).wait_recv()
            passed[j].start()
        copy(0, sibling, me).wait_recv()
        for j, chip in enumerate(chips):
            copy(4 + j, (*chip, 1 - c), me).wait_recv()
        for cp in first + passed:
            cp.wait_send()
        mine.wait()
        acc = buf[0:R, :]
        for d in range(1, 8):
            acc = acc + buf[d * R:(d + 1) * R, :]
        o_ref[...] = acc

    return pl.pallas_call(
        body, in_specs=[pl.BlockSpec(memory_space=pltpu.VMEM)], out_specs=pl.BlockSpec(memory_space=pltpu.VMEM),
        out_shape=_sds((R, LANES), F32),
        scratch_shapes=[pltpu.VMEM((8 * R, LANES), F32), pltpu.SemaphoreType.DMA((7,)),
                        pltpu.SemaphoreType.DMA((7,)), pltpu.SemaphoreType.DMA],
        name="allreduce_small", compiler_params=pltpu.CompilerParams(vmem_limit_bytes=VMEM_LIMIT))(v)


def matmul(a, b, mode, out_dtype, name, tm=512, tn=1024, tk=4608, jobs=()):
    if mode == "nn":
        (M, K), (K2, N) = a.shape, b.shape
    elif mode == "nt":
        (M, K), (N, K2) = a.shape, b.shape
    else:
        (K, M), (K2, N) = a.shape, b.shape
    assert K == K2
    tm, tn, tk = _pick(M, tm), _pick(N, tn), _pick(K, tk)
    nk = K // tk
    dn = {"nn": NN, "nt": NT, "tn": TN}[mode]

    def body(a_ref, b_ref, o_ref, *acc):
        part = _dot(a_ref[...].astype(BF16), b_ref[...].astype(BF16), dn)
        if nk == 1:
            o_ref[...] = part.astype(o_ref.dtype)
            return
        acc_ref, = acc
        k = pl.program_id(2)

        @pl.when(k == 0)
        def _():
            acc_ref[...] = part

        @pl.when(jnp.logical_and(k > 0, k < nk - 1))
        def _():
            acc_ref[...] += part

        @pl.when(k == nk - 1)
        def _():
            o_ref[...] = (acc_ref[...] + part).astype(o_ref.dtype)

    a_spec = (pl.BlockSpec((tk, tm), lambda i, j, k: (k, i)) if mode == "tn"
              else pl.BlockSpec((tm, tk), lambda i, j, k: (i, k)))
    b_spec = (pl.BlockSpec((tn, tk), lambda i, j, k: (j, k)) if mode == "nt"
              else pl.BlockSpec((tk, tn), lambda i, j, k: (k, j)))
    outs, per_job = _call(
        body, grid=(M // tm, N // tn, nk), in_specs=[a_spec, b_spec],
        out_specs=[pl.BlockSpec((tm, tn), lambda i, j, k: (i, j))], out_shape=[_sds((M, N), out_dtype)],
        scratch_shapes=[pltpu.VMEM((tm, tn), F32)] if nk > 1 else [], name=name,
        sem=("parallel", "parallel", "arbitrary"), args=(a, b), jobs=jobs)
    return outs[0], per_job


def rms_fwd(x, w):
    L, D = x.shape
    T = _pick(L, 256)

    def body(x_ref, w_ref, h_ref):
        xv = x_ref[...]
        r = lax.rsqrt(jnp.mean(xv * xv, axis=-1, keepdims=True) + NORM_EPS)
        h_ref[...] = (xv * r * w_ref[...]).astype(h_ref.dtype)

    return pl.pallas_call(
        body, grid=(L // T,),
        in_specs=[pl.BlockSpec((T, D), lambda i: (i, 0)), pl.BlockSpec((1, D), lambda i: (0, 0))],
        out_specs=pl.BlockSpec((T, D), lambda i: (i, 0)),
        out_shape=_sds((L, D), BF16), name="rms_fwd", compiler_params=_params("parallel"))(x, w)


def post_fwd(x, o, w, jobs=()):
    L, D = x.shape
    T = _pick(L, 256)

    def body(x_ref, o_ref, w_ref, y_ref):
        ov = o_ref[...]
        r = lax.rsqrt(jnp.mean(ov * ov, axis=-1, keepdims=True) + NORM_EPS)
        y_ref[...] = x_ref[...] + ov * r * w_ref[...]

    row = pl.BlockSpec((T, D), lambda i: (i, 0))
    outs, per_job = _call(
        body, grid=(L // T,), in_specs=[row, row, pl.BlockSpec((1, D), lambda i: (0, 0))], out_specs=[row],
        out_shape=[_sds((L, D), F32)], name="post_fwd", sem=("parallel",), args=(x, o, w), jobs=jobs)
    return outs[0], per_job


def _rms_bwd_math(xv, w, dy):
    r = lax.rsqrt(jnp.mean(xv * xv, axis=-1, keepdims=True) + NORM_EPS)
    xhat = xv * r
    g = dy * w
    dx = r * (g - xhat * jnp.mean(g * xhat, axis=-1, keepdims=True))
    return dx, jnp.sum(dy * xhat, axis=0, keepdims=True)


def post_bwd(o, w, dxn):
    L, D = o.shape
    T = _pick(L, 256)

    def body(o_ref, w_ref, d_ref, do_ref, dw_ref):
        dx, dw = _rms_bwd_math(o_ref[...], w_ref[...], d_ref[...])
        do_ref[...] = dx.astype(do_ref.dtype)

        @pl.when(pl.program_id(0) == 0)
        def _():
            dw_ref[...] = jnp.zeros_like(dw_ref)

        dw_ref[...] += dw

    row = pl.BlockSpec((T, D), lambda i: (i, 0))
    vec = pl.BlockSpec((1, D), lambda i: (0, 0))
    return pl.pallas_call(
        body, grid=(L // T,), in_specs=[row, vec, row], out_specs=[row, vec],
        out_shape=[_sds((L, D), BF16), _sds((1, D), F32)],
        name="post_bwd", compiler_params=_params("arbitrary"))(o, w, dxn)


def rms_bwd(x, w, dh_a, dh_b, dxn, jobs=()):
    L, D = x.shape
    T = _pick(L, 256)

    def body(x_ref, w_ref, a_ref, b_ref, d_ref, dx_ref, dw_ref):
        dx, dw = _rms_bwd_math(x_ref[...], w_ref[...], a_ref[...] + b_ref[...])
        dx_ref[...] = d_ref[...] + dx

        @pl.when(pl.program_id(0) == 0)
        def _():
            dw_ref[...] = jnp.zeros_like(dw_ref)

        dw_ref[...] += dw

    row = pl.BlockSpec((T, D), lambda i: (i, 0))
    vec = pl.BlockSpec((1, D), lambda i: (0, 0))
    return _call(
        body, grid=(L // T,), in_specs=[row, vec, row, row, row], out_specs=[row, vec],
        out_shape=[_sds((L, D), F32), _sds((1, D), F32)],
        name="rms_bwd", sem=("arbitrary",), args=(x, w, dh_a, dh_b, dxn), jobs=jobs)


def loss_head(y, target):
    L, D = y.shape
    T = _pick(L, 256)

    def body(y_ref, t_ref, d_ref, s_ref):
        e = y_ref[...] - t_ref[...]
        d_ref[...] = e * (1.0 / D)

        @pl.when(pl.program_id(0) == 0)
        def _():
            s_ref[...] = jnp.zeros_like(s_ref)

        s_ref[...] += jnp.sum(e * e)

    row = pl.BlockSpec((T, D), lambda i: (i, 0))
    return pl.pallas_call(
        body, grid=(L // T,), in_specs=[row, row],
        out_specs=[row, pl.BlockSpec((8, LANES), lambda i: (0, 0))],
        out_shape=[_sds((L, D), F32), _sds((8, LANES), F32)],
        name="loss_head", compiler_params=_params("arbitrary"))(y, target)


def _window_sums(xe, w, back):
    n = xe.shape[0]
    s, k = xe, 1
    while k < w:
        s = s + pltpu.roll(s, k if back else n - k, 0)
        k *= 2
    return s


def pool_fwd(proj, mixw, scale, D):
    L = proj.shape[0]
    PGW = D // GROUPS
    T = _pick(L, 256)
    hb = T // POOL_HALO

    def body(u_ref, halo_ref, g_ref, mw_ref, sc_ref, y_ref, p_ref):
        i = pl.program_id(0)
        u = u_ref[...]
        halo = jnp.where(i > 0, halo_ref[...], 0.0)
        xe = jnp.concatenate([halo, u], axis=0)
        t1 = _row(i * T + 1, (T, 1))
        for g, w in enumerate(POOL_WINDOWS):
            sl = slice(g * PGW, (g + 1) * PGW)
            win = _window_sums(xe[:, sl], w, True)[POOL_HALO:, :]
            cnt = jnp.minimum(t1, w).astype(F32)
            pooled = (win / cnt - u[:, sl]).astype(BF16)
            p_ref[:, sl] = pooled
            mixed = _dot(pooled, mw_ref[g])
            gate = g_ref[:, sl]
            y_ref[:, sl] = (mixed * sc_ref[:, sl] * (gate * _sigmoid(gate))).astype(BF16)

    return pl.pallas_call(
        body, grid=(L // T,),
        in_specs=[pl.BlockSpec((T, D), lambda i: (i, 0)),
                  pl.BlockSpec((POOL_HALO, D), lambda i: (jnp.maximum(i * hb - 1, 0), 0)),
                  pl.BlockSpec((T, D), lambda i: (i, 1)),
                  pl.BlockSpec((GROUPS, PGW, PGW), lambda i: (0, 0, 0)),
                  pl.BlockSpec((1, D), lambda i: (0, 0))],
        out_specs=[pl.BlockSpec((T, D), lambda i: (i, 0)), pl.BlockSpec((T, D), lambda i: (i, 0))],
        out_shape=[_sds((L, 2 * D), BF16), _sds((L, D), BF16)],
        name="pool_fwd", compiler_params=_params("parallel"))(proj, proj, proj, mixw, scale)


def pool_bwd(proj, dmix, pooled, mixw, scale, dproj, D):
    L = proj.shape[0]
    PGW = D // GROUPS
    T = _pick(L, 256)
    hb = T // POOL_HALO
    nT = L // T

    def body(g_ref, gh_ref, dy_ref, dyh_ref, p_ref, mw_ref, sc_ref, old_ref, dp_ref, dm_ref, ds_ref):
        i = pl.program_id(0)
        t1 = _row(i * T + 1, (T, 1))
        th1 = _row((i + 1) * T + 1, (POOL_HALO, 1))
        live = i < nT - 1

        @pl.when(i == 0)
        def _():
            ds_ref[...] = jnp.zeros_like(ds_ref)

        for g, w in enumerate(POOL_WINDOWS):
            sl = slice(g * PGW, (g + 1) * PGW)
            sc = sc_ref[:, sl]
            gate, dy = g_ref[:, sl], dy_ref[:, sl]
            sg, dsg = _silu_and_grad(gate)
            mixed = _dot(p_ref[:, sl], mw_ref[g])
            dmixed = (dy * sc * sg).astype(BF16)
            dm_ref[:, sl] = dmixed
            dp_ref[:, D + g * PGW:D + (g + 1) * PGW] = (dy * mixed * sc * dsg).astype(BF16)
            ds_ref[:, sl] += jnp.sum(dy * mixed * sg, axis=0, keepdims=True)
            dpool = _dot(dmixed, mw_ref[g], NT)
            gate_h = gh_ref[:, sl]
            dmixed_h = (dyh_ref[:, sl] * sc * (gate_h * _sigmoid(gate_h))).astype(BF16)
            dpool_h = jnp.where(live, _dot(dmixed_h, mw_ref[g], NT), 0.0)
            q = dpool / jnp.minimum(t1, w).astype(F32)
            q_h = dpool_h / jnp.minimum(th1, w).astype(F32)
            qe = jnp.concatenate([q, q_h], axis=0)
            dp_ref[:, sl] = (_window_sums(qe, w, False)[:T, :] - dpool).astype(BF16)

    nxt = lambda i: jnp.minimum((i + 1) * hb, L // POOL_HALO - 1)
    row = lambda c: pl.BlockSpec((T, D), lambda i: (i, c))
    return pl.pallas_call(
        body, grid=(nT,),
        in_specs=[row(1), pl.BlockSpec((POOL_HALO, D), lambda i: (nxt(i), 1)),
                  row(0), pl.BlockSpec((POOL_HALO, D), lambda i: (nxt(i), 0)),
                  row(0), pl.BlockSpec((GROUPS, PGW, PGW), lambda i: (0, 0, 0)),
                  pl.BlockSpec((1, D), lambda i: (0, 0)), _ANY],
        out_specs=[pl.BlockSpec((T, 2 * D), lambda i: (i, 0)), row(0), pl.BlockSpec((1, D), lambda i: (0, 0))],
        out_shape=[_sds(dproj.shape, dproj.dtype), _sds((L, D), BF16), _sds((1, D), F32)],
        input_output_aliases={7: 0},
        name="pool_bwd", compiler_params=_params("arbitrary"))(proj, proj, dmix, dmix, pooled, mixw, scale, dproj)


def pool_dw(pooled, dmixed, D):
    L = pooled.shape[0]
    PGW = D // GROUPS
    tk = _pick(L, 1024)
    nk = L // tk

    def body(p_ref, d_ref, o_ref):
        @pl.when(pl.program_id(1) == 0)
        def _():
            o_ref[...] = jnp.zeros_like(o_ref)

        o_ref[0] += _dot(p_ref[...], d_ref[...], TN)

    blk = pl.BlockSpec((tk, PGW), lambda g, k: (k, g))
    return pl.pallas_call(
        body, grid=(GROUPS, nk), in_specs=[blk, blk],
        out_specs=pl.BlockSpec((1, PGW, PGW), lambda g, k: (g, 0, 0)),
        out_shape=_sds((GROUPS, PGW, PGW), F32), name="pool_dw",
        compiler_params=_params("parallel", "arbitrary"))(pooled, dmixed)


def conv_fwd(proj, cw, cb, D):
    L = proj.shape[0]
    C = cw.shape[1]
    assert (3 * D) % C == 0
    cblk = (3 * D) // C
    T = _pick(L, 256)
    hb = T // CONV_HALO

    def body(u_ref, halo_ref, w_ref, b_ref, o_ref):
        i = pl.program_id(0)
        u = u_ref[...]
        xe = jnp.concatenate([jnp.where(i > 0, halo_ref[...], 0.0), u], axis=0)
        acc = b_ref[...] + w_ref[CONV_K - 1:CONV_K, :] * u
        for k in range(CONV_K - 1):
            acc = acc + w_ref[k:k + 1, :] * pltpu.roll(xe, CONV_K - 1 - k, 0)[CONV_HALO:, :]
        o_ref[...] = acc

    return pl.pallas_call(
        body, grid=(L // T,),
        in_specs=[pl.BlockSpec((T, C), lambda i: (i, cblk)),
                  pl.BlockSpec((CONV_HALO, C), lambda i: (jnp.maximum(i * hb - 1, 0), cblk)),
                  pl.BlockSpec((CONV_K, C), lambda i: (0, 0)),
                  pl.BlockSpec((1, C), lambda i: (0, 0))],
        out_specs=pl.BlockSpec((T, C), lambda i: (i, 0)),
        out_shape=_sds((L, C), F32), name="conv_fwd", compiler_params=_params("parallel"))(proj, proj, cw, cb)


def conv_bwd(dparts, proj, cw, dproj, D, jobs=()):
    L = proj.shape[0]
    C = cw.shape[1]
    cblk = (3 * D) // C
    T = _pick(L, 256)
    hb = T // CONV_HALO
    nT = L // T
    widths = [p.shape[1] for p in dparts]
    assert sum(widths) == C
    n = len(dparts)

    def body(*refs):
        d_refs, dn_refs = refs[:n], refs[n:2 * n]
        u_ref, up_ref, w_ref, old_ref, dr_ref, dw_ref, db_ref = refs[2 * n:]
        i = pl.program_id(0)

        @pl.when(i == 0)
        def _():
            dw_ref[...] = jnp.zeros_like(dw_ref)
            db_ref[...] = jnp.zeros_like(db_ref)

        at = 0
        for d_ref, dn_ref, wd in zip(d_refs, dn_refs, widths):
            sl = slice(at, at + wd)
            at += wd
            d = d_ref[...]
            u = u_ref[:, sl]
            de = jnp.concatenate([d, jnp.where(i < nT - 1, dn_ref[...], 0.0)], axis=0)
            ue = jnp.concatenate([jnp.where(i > 0, up_ref[:, sl], 0.0), u], axis=0)
            acc = w_ref[CONV_K - 1:CONV_K, sl] * d
            dw_ref[CONV_K - 1:CONV_K, sl] += jnp.sum(d * u, axis=0, keepdims=True)
            for k in range(CONV_K - 1):
                sh = CONV_K - 1 - k
                acc = acc + w_ref[k:k + 1, sl] * pltpu.roll(de, T + CONV_HALO - sh, 0)[:T, :]
                dw_ref[k:k + 1, sl] += jnp.sum(d * pltpu.roll(ue, sh, 0)[CONV_HALO:, :], axis=0, keepdims=True)
            dr_ref[:, sl] = acc.astype(dr_ref.dtype)
            db_ref[:, sl] += jnp.sum(d, axis=0, keepdims=True)

    nxt = lambda i: jnp.minimum((i + 1) * hb, L // CONV_HALO - 1)
    return _call(
        body, grid=(nT,),
        in_specs=[pl.BlockSpec((T, wd), lambda i: (i, 0)) for wd in widths]
        + [pl.BlockSpec((CONV_HALO, wd), lambda i: (nxt(i), 0)) for wd in widths]
        + [pl.BlockSpec((T, C), lambda i: (i, cblk)),
           pl.BlockSpec((CONV_HALO, C), lambda i: (jnp.maximum(i * hb - 1, 0), cblk)),
           pl.BlockSpec((CONV_K, C), lambda i: (0, 0)), _ANY],
        out_specs=[pl.BlockSpec((T, C), lambda i: (i, cblk)),
                   pl.BlockSpec((8, C), lambda i: (0, 0)),
                   pl.BlockSpec((1, C), lambda i: (0, 0))],
        out_shape=[_sds(dproj.shape, dproj.dtype), _sds((8, C), F32), _sds((1, C), F32)],
        aliases={2 * n + 3: 0}, name="conv_bwd", sem=("arbitrary",),
        args=(*dparts, *dparts, proj, proj, cw, dproj), jobs=jobs)


def _softplus(v):
    y = jnp.exp(-jnp.abs(v))
    u = 1.0 + y
    log1p = jnp.where(u == 1.0, y, jnp.log(u) * y / jnp.where(u == 1.0, 1.0, u - 1.0))
    return jnp.maximum(v, 0.0) + log1p


def dt_prep(dtraw, bias, alog, expand, D):
    L = dtraw.shape[0]
    GC = D // GROUPS
    HPG = GC // HEAD_DIM
    Q = _pick(L, SCAN_CHUNK)
    nc = L // Q

    def body(r_ref, b_ref, a_ref, e_ref, dt_ref, acs_ref, acst_ref, dtx_ref, eax_ref, dsx_ref, cdx_ref):
        valid = lax.broadcasted_iota(jnp.int32, (1, LANES), 1) < HPG
        dt = jnp.where(valid, _softplus(r_ref[...] + b_ref[...]), 0.0)
        adt = dt * -jnp.exp(a_ref[...])
        tril = (_row(0, (Q, Q)) >= lax.broadcasted_iota(jnp.int32, (Q, Q), 1)).astype(BF16)
        acs = _sel_dot(tril, adt)
        last = acs[Q - 1:Q, :]
        dt_ref[...] = dt
        acs_ref[...] = acs
        acst_ref[...] = acs.T
        e = e_ref[...]
        dtx_ref[...] = _dot_sel(dt, e)
        eax_ref[...] = jnp.exp(_dot_sel(acs, e))
        dsx_ref[...] = jnp.exp(_dot_sel(last - acs, e))
        cdx_ref[0] = jnp.exp(_dot_sel(jnp.broadcast_to(last, (8, LANES)), e))

    head = pl.BlockSpec((Q, LANES), lambda g, c: (c, g))
    hvec = pl.BlockSpec((1, LANES), lambda g, c: (0, g))
    chan = pl.BlockSpec((Q, GC), lambda g, c: (c, g))
    return pl.pallas_call(
        body, grid=(GROUPS, nc),
        in_specs=[head, hvec, hvec, pl.BlockSpec((LANES, GC), lambda g, c: (0, 0))],
        out_specs=[head, head, pl.BlockSpec((LANES, Q), lambda g, c: (g, c)), chan, chan, chan,
                   pl.BlockSpec((1, 8, GC), lambda g, c: (c, 0, g))],
        out_shape=[_sds((L, GROUPS * LANES), F32), _sds((L, GROUPS * LANES), F32), _sds((GROUPS * LANES, L), F32),
                   _sds((L, D), F32), _sds((L, D), F32), _sds((L, D), F32), _sds((nc, 8, D), F32)],
        name="dt_prep", compiler_params=_params("parallel", "parallel"))(dtraw, bias, alog, expand)


def _scan_specs(L, D, Q, rev):
    GC = D // GROUPS
    nc = L // Q
    ci = (lambda c: nc - 1 - c) if rev else (lambda c: c)
    return dict(
        xs=pl.BlockSpec((Q, GC), lambda g, c: (ci(c), g)),
        b=pl.BlockSpec((Q, STATE), lambda g, c: (ci(c), D // STATE + g)),
        c=pl.BlockSpec((Q, STATE), lambda g, c: (ci(c), D // STATE + GROUPS + g)),
        chan=pl.BlockSpec((Q, GC), lambda g, c: (ci(c), g)),
        cdx=pl.BlockSpec((1, 8, GC), lambda g, c: (ci(c), 0, g)),
        head=pl.BlockSpec((Q, LANES), lambda g, c: (ci(c), g)),
        headt=pl.BlockSpec((LANES, Q), lambda g, c: (g, ci(c))),
        state=pl.BlockSpec((1, 1, STATE, GC), lambda g, c: (ci(c), g, 0, 0)),
        hvec=pl.BlockSpec((1, LANES), lambda g, c: (0, g)),
        cvec=pl.BlockSpec((1, GC), lambda g, c: (0, g)))


def scan_fwd(pre, dtx, eax, dsx, cdx, acs, acst, D):
    L = pre.shape[0]
    GC = D // GROUPS
    Q = _pick(L, SCAN_CHUNK)
    nc = L // Q
    sp = _scan_specs(L, D, Q, False)

    def body(xs_ref, b_ref, c_ref, dtx_ref, eax_ref, dsx_ref, cdx_ref, acs_ref, acst_ref, y_ref, st_ref, s_scr):
        @pl.when(pl.program_id(1) == 0)
        def _():
            s_scr[...] = jnp.zeros_like(s_scr)

        tri = _row(0, (Q, Q)) >= lax.broadcasted_iota(jnp.int32, (Q, Q), 1)
        half = lax.broadcasted_iota(jnp.int32, (1, LANES), 1) // HEAD_DIM
        xs, _ = _silu_and_grad(xs_ref[...])
        bg = _silu_and_grad(b_ref[...])[0].astype(BF16)
        cg = _silu_and_grad(c_ref[...])[0].astype(BF16)
        xdt = xs * dtx_ref[...]
        sprev = s_scr[...]
        st_ref[0, 0] = sprev
        sc = _dot(cg, bg, NT)
        yoff = _dot(cg, sprev.astype(BF16)) * eax_ref[...]
        for j in range(GC // LANES):
            ps = slice(j * LANES, (j + 1) * LANES)
            xp = xdt[:, ps]
            acc = yoff[:, ps]
            for hh in range(2):
                h = 2 * j + hh
                lm = jnp.exp(jnp.where(tri, acs_ref[:, h:h + 1] - acst_ref[h:h + 1, :], -1e30))
                xm = jnp.where(half == hh, xp, 0.0).astype(BF16)
                acc = acc + _dot((sc * lm).astype(BF16), xm)
            y_ref[:, ps] = acc
        xw = (xdt * dsx_ref[...]).astype(BF16)
        s_scr[...] = cdx_ref[0, 0:1, :] * sprev + _dot(bg, xw, TN)

    return pl.pallas_call(
        body, grid=(GROUPS, nc),
        in_specs=[sp["xs"], sp["b"], sp["c"], sp["chan"], sp["chan"], sp["chan"], sp["cdx"], sp["head"], sp["headt"]],
        out_specs=[sp["chan"], sp["state"]],
        out_shape=[_sds((L, D), F32), _sds((nc, GROUPS, STATE, GC), F32)],
        scratch_shapes=[pltpu.VMEM((STATE, GC), F32)], name="scan_fwd",
        compiler_params=_params("parallel", "arbitrary"))(pre, pre, pre, dtx, eax, dsx, cdx, acs, acst)


def scan_bwd(pre, dtx, eax, dsx, cdx, acs, acst, dt, dtraw, bias, alog, states, dy, dexp, collapse, D, jobs=()):
    L = pre.shape[0]
    GC = D // GROUPS
    Q = _pick(L, SCAN_CHUNK)
    nc = L // Q
    sp = _scan_specs(L, D, Q, True)
    rc = lambda c: nc - 1 - c

    def body(xs_ref, b_ref, c_ref, dtx_ref, eax_ref, dsx_ref, cdx_ref, acs_ref, acst_ref, dt_ref, raw_ref,
             bias_ref, alog_ref, st_ref, dy_ref, dexp_ref, col_ref,
             dxs_ref, db_ref, dc_ref, ddt_ref, dal_ref, dbi_ref, ds_scr, dx_scr):
        first = pl.program_id(1) == 0

        @pl.when(first)
        def _():
            ds_scr[...] = jnp.zeros_like(ds_scr)
            dal_ref[...] = jnp.zeros_like(dal_ref)
            dbi_ref[...] = jnp.zeros_like(dbi_ref)

        li = _row(0, (Q, Q))
        si = lax.broadcasted_iota(jnp.int32, (Q, Q), 1)
        lane = lax.broadcasted_iota(jnp.int32, (1, LANES), 1)
        half = lane // HEAD_DIM
        xs_pre, b_pre, c_pre = xs_ref[...], b_ref[...], c_ref[...]
        xs, xs_g = _silu_and_grad(xs_pre)
        bf, b_g = _silu_and_grad(b_pre)
        cf, c_g = _silu_and_grad(c_pre)
        bg, cg = bf.astype(BF16), cf.astype(BF16)
        dtx, eax, dsx = dtx_ref[...], eax_ref[...], dsx_ref[...]
        cd = cdx_ref[0, 0:1, :]
        xdt = xs * dtx
        G = dy_ref[...]
        prev = st_ref[0, 0]
        dsn = ds_scr[...]
        prev_b, dsn_b = prev.astype(BF16), dsn.astype(BF16)
        cp = _dot(cg, prev_b)
        ge_b = (G * eax).astype(BF16)
        d_c = _dot(ge_b, prev_b, NT)
        dprev = _dot(cg, ge_b, TN) + cd * dsn
        chan_a = G * cp * eax
        xw_b = (xdt * dsx).astype(BF16)
        dcd = jnp.sum(prev * dsn, axis=0, keepdims=True)
        d_b = _dot(xw_b, dsn_b, NT)
        dxw = _dot(bg, dsn_b)
        dd = dxw * xdt * dsx
        chan_a = chan_a - dd
        last_c = jnp.sum(dd, axis=0, keepdims=True) + dcd * cd
        sc = _dot(cg, bg, NT)
        sct = _dot(bg, cg, NT)
        dsc = jnp.zeros((Q, Q), F32)
        dsct = jnp.zeros((Q, Q), F32)
        dacs = jnp.zeros((Q, LANES), F32)
        for j in range(GC // LANES):
            ps = slice(j * LANES, (j + 1) * LANES)
            xp, gp = xdt[:, ps], G[:, ps]
            dxp = dxw[:, ps] * dsx[:, ps]
            for hh in range(2):
                h = 2 * j + hh
                col, row = acs_ref[:, h:h + 1], acst_ref[h:h + 1, :]
                lm = jnp.exp(jnp.where(li >= si, col - row, -1e30))
                lmt = jnp.exp(jnp.where(si >= li, row - col, -1e30))
                m, mt = sc * lm, sct * lmt
                xm = jnp.where(half == hh, xp, 0.0).astype(BF16)
                gm = jnp.where(half == hh, gp, 0.0).astype(BF16)
                dm = _dot(gm, xm, NT)
                dmt = _dot(xm, gm, NT)
                dxp = dxp + _dot(mt.astype(BF16), gm)
                dsc = dsc + dm * lm
                dsct = dsct + dmt * lmt
                rs = jnp.sum(dm * m, axis=1, keepdims=True) - jnp.sum(dmt * mt, axis=1, keepdims=True)
                dacs = dacs + jnp.where(lane == h, rs, 0.0)
            dx_scr[:, ps] = dxp
        d_c = d_c + _dot(dsc.astype(BF16), bg)
        d_b = d_b + _dot(dsct.astype(BF16), cg)
        ds_scr[...] = dprev
        dxdt = dx_scr[...]
        dxs_ref[...] = (dxdt * dtx + dexp_ref[...] * G) * xs_g
        db_ref[...] = d_b * b_g
        dc_ref[...] = d_c * c_g
        colm = col_ref[...]
        dacs = dacs + _dot_sel(chan_a, colm)
        dlast = _dot_sel(jnp.broadcast_to(last_c, (8, GC)), colm)[0:1, :]
        dacs = dacs + jnp.where(_row(0, (Q, 1)) == Q - 1, dlast, 0.0)
        dadt = _sel_dot((si >= li).astype(BF16), dacs)
        a = -jnp.exp(alog_ref[...])
        dt = dt_ref[...]
        ddt = dadt * a + _dot_sel(dxdt * xs, colm)
        dal_ref[0:1, :] += jnp.sum(dadt * dt * a, axis=0, keepdims=True)
        draw = ddt * _sigmoid(raw_ref[...] + bias_ref[...])
        dbi_ref[0:1, :] += jnp.sum(draw, axis=0, keepdims=True)
        ddt_ref[...] = draw.astype(ddt_ref.dtype)

    acc = pl.BlockSpec((8, LANES), lambda g, c: (0, g))
    return _call(
        body, grid=(GROUPS, nc),
        in_specs=[sp["xs"], sp["b"], sp["c"], sp["chan"], sp["chan"], sp["chan"], sp["cdx"], sp["head"], sp["headt"],
                  sp["head"], sp["head"], sp["hvec"], sp["hvec"], sp["state"], sp["chan"], sp["cvec"],
                  pl.BlockSpec((GC, LANES), lambda g, c: (0, 0))],
        out_specs=[sp["chan"],
                   pl.BlockSpec((Q, STATE), lambda g, c: (rc(c), g)),
                   pl.BlockSpec((Q, STATE), lambda g, c: (rc(c), g)),
                   sp["head"], acc, acc],
        out_shape=[_sds((L, D), F32), _sds((L, GROUPS * STATE), F32), _sds((L, GROUPS * STATE), F32),
                   _sds((L, GROUPS * LANES), BF16), _sds((8, GROUPS * LANES), F32), _sds((8, GROUPS * LANES), F32)],
        scratch_shapes=[pltpu.VMEM((STATE, GC), F32), pltpu.VMEM((Q, GC), F32)], name="scan_bwd",
        sem=("parallel", "arbitrary"),
        args=(pre, pre, pre, dtx, eax, dsx, cdx, acs, acst, dt, dtraw, bias, alog, states, dy, dexp, collapse),
        jobs=jobs)


def gate_fwd(y, pre, proj, dexp, nw, mixed, D):
    L = y.shape[0]
    GC = D // GROUPS
    T = _pick(L, 256)

    def body(y_ref, xs_ref, z_ref, de_ref, nw_ref, old_ref, o_ref):
        xs, _ = _silu_and_grad(xs_ref[...])
        z = z_ref[...]
        y3 = (y_ref[...] + de_ref[...] * xs) * (z * _sigmoid(z))
        for g in range(GROUPS):
            sl = slice(g * GC, (g + 1) * GC)
            yg = y3[:, sl]
            r = lax.rsqrt(jnp.mean(yg * yg, axis=-1, keepdims=True) + NORM_EPS)
            o_ref[:, sl] = (yg * r * nw_ref[:, sl]).astype(o_ref.dtype)

    row = lambda c: pl.BlockSpec((T, D), lambda i: (i, c))
    vec = pl.BlockSpec((1, D), lambda i: (0, 0))
    return pl.pallas_call(
        body, grid=(L // T,), in_specs=[row(0), row(0), row(2), vec, vec, _ANY], out_specs=row(1),
        out_shape=_sds(mixed.shape, mixed.dtype), input_output_aliases={5: 0}, name="gate_fwd",
        compiler_params=_params("parallel"))(y, pre, proj, dexp, nw, mixed)


def gate_bwd(y, pre, proj, dexp, nw, dmix, D, jobs=()):
    L = y.shape[0]
    GC = D // GROUPS
    T = _pick(L, 256)

    def body(y_ref, xs_ref, z_ref, de_ref, nw_ref, dm_ref, dy_ref, dz_ref, dnw_ref, dde_ref):
        @pl.when(pl.program_id(0) == 0)
        def _():
            dnw_ref[...] = jnp.zeros_like(dnw_ref)
            dde_ref[...] = jnp.zeros_like(dde_ref)

        xs, _ = _silu_and_grad(xs_ref[...])
        sz, dsz = _silu_and_grad(z_ref[...])
        y2 = y_ref[...] + de_ref[...] * xs
        y3 = y2 * sz
        for g in range(GROUPS):
            sl = slice(g * GC, (g + 1) * GC)
            yg, dm = y3[:, sl], dm_ref[:, sl]
            r = lax.rsqrt(jnp.mean(yg * yg, axis=-1, keepdims=True) + NORM_EPS)
            n = yg * r
            gg = dm * nw_ref[:, sl]
            dy3 = r * (gg - n * jnp.mean(gg * n, axis=-1, keepdims=True))
            dnw_ref[:, sl] += jnp.sum(dm * n, axis=0, keepdims=True)
            dy2 = dy3 * sz[:, sl]
            dy_ref[:, sl] = dy2
            dz_ref[:, sl] = (dy3 * y2[:, sl] * dsz[:, sl]).astype(dz_ref.dtype)
            dde_ref[:, sl] += jnp.sum(dy2 * xs[:, sl], axis=0, keepdims=True)

    row = lambda c: pl.BlockSpec((T, D), lambda i: (i, c))
    vec = pl.BlockSpec((1, D), lambda i: (0, 0))
    return _call(
        body, grid=(L // T,), in_specs=[row(0), row(0), row(2), vec, vec, row(1)],
        out_specs=[row(0), row(2), vec, vec],
        out_shape=[_sds((L, D), F32), _sds((L, proj.shape[1]), BF16), _sds((1, D), F32), _sds((1, D), F32)],
        name="gate_bwd", sem=("arbitrary",), args=(y, pre, proj, dexp, nw, dmix), jobs=jobs)


def _adam_math(gv, w, m, v):
    c1 = 1.0 - ADAM_B1 ** ADAM_STEP
    c2 = 1.0 - ADAM_B2 ** ADAM_STEP
    nm = ADAM_B1 * m + (1.0 - ADAM_B1) * gv
    nv = ADAM_B2 * v + (1.0 - ADAM_B2) * (gv * gv)
    return -ADAM_LR * ((nm / c1) / (jnp.sqrt(nv / c2) + ADAM_EPS) + ADAM_WD * w), nm, nv


def adamw(g, w, m, v, name):
    R, C = g.shape
    T = R if R <= 128 else 128
    assert R % T == 0

    def body(g_ref, w_ref, m_ref, v_ref, d_ref, nm_ref, nv_ref):
        d_ref[...], nm_ref[...], nv_ref[...] = _adam_math(g_ref[...], w_ref[...], m_ref[...], v_ref[...])

    blk = pl.BlockSpec((T, C), lambda i: (i, 0))
    return pl.pallas_call(
        body, grid=(R // T,), in_specs=[blk] * 4, out_specs=[blk] * 3,
        out_shape=[_sds((R, C), F32)] * 3, name=name, compiler_params=_params("parallel"))(g, w, m, v)


def adamw_layer(g, w, m, v, layer, prev, name):
    R, C = g.shape
    T = _pick(R, 128)
    nb = R // T
    assert R % T == 0

    def body(g_ref, w_ref, m_ref, v_ref, *rest):
        go_ref, d_ref, nm_ref, nv_ref = rest[-4:]
        gv = g_ref[...]
        go_ref[...] = gv
        d_ref[...], nm_ref[...], nv_ref[...] = _adam_math(gv, w_ref[...], m_ref[...], v_ref[...])

    mine = pl.BlockSpec((T, C), lambda i: (layer * nb + i, 0))
    prev = list(prev or [])
    return pl.pallas_call(
        body, grid=(nb,), in_specs=[pl.BlockSpec((T, C), lambda i: (i, 0)), mine, mine, mine] + [_ANY] * len(prev),
        out_specs=[mine] * 4, out_shape=[_sds(w.shape, F32)] * 4,
        input_output_aliases={4 + i: i for i in range(len(prev))}, name=name,
        compiler_params=_params("parallel"))(g, w, m, v, *prev)


def _to_groups(v, hpg):
    lead = v.shape[:-1]
    t = v.reshape(lead + (GROUPS, hpg))
    t = jnp.pad(t, [(0, 0)] * (len(lead) + 1) + [(0, LANES - hpg)])
    return t.reshape(lead + (GROUPS * LANES,))


def _from_groups(a, hpg):
    lead = a.shape[:-1]
    return a.reshape(lead + (GROUPS, LANES))[..., :hpg].reshape(lead + (GROUPS * hpg,))


def _expand_matrix(D):
    gc = D // GROUPS
    return (jnp.arange(LANES)[:, None] == (jnp.arange(gc)[None, :] // HEAD_DIM)).astype(BF16)


def layer_params(pre_w, w_in_full, mixw, scale, cw, cb, bias, alog, dskip, nw, w_out_full, post_w, D):
    hpg = D // GROUPS // HEAD_DIM
    main = w_in_full.shape[1] - GROUPS * hpg
    return dict(
        pre_w=pre_w[None], wmain=w_in_full[:, :main], wdt=_to_groups(w_in_full[:, main:], hpg), mixw=mixw,
        scale=scale[None], cw=cw, cb=cb[None], bias=_to_groups(bias, hpg)[None], alog=_to_groups(alog, hpg)[None],
        dexp=jnp.repeat(dskip, HEAD_DIM)[None], nw=nw[None], wout=w_out_full, post_w=post_w[None])


def layer_fwd(x, p, D, next_shards=None):
    send_in = [gather_send_job(next_shards[:1])] if next_shards else []
    h = rms_fwd(x, p["pre_w"])
    proj, got = matmul(h, p["wmain"], "nn", F32, "proj", jobs=send_in)
    dtraw, _ = matmul(h, p["wdt"], "nn", F32, "dtproj")
    mixed, pooled = pool_fwd(proj, p["mixw"], p["scale"], D)
    pre = conv_fwd(proj, p["cw"], p["cb"], D)
    dtp = dt_prep(dtraw, p["bias"], p["alog"], _expand_matrix(D), D)
    dt, acs, acst, dtx, eax, dsx, cdx = dtp
    y, states = scan_fwd(pre, dtx, eax, dsx, cdx, acs, acst, D)
    mixed = gate_fwd(y, pre, proj, p["dexp"], p["nw"], mixed, D)
    jobs = [gather_send_job(next_shards[1:]), gather_pass_job(got[0])] if next_shards else []
    out, got = matmul(mixed, p["wout"], "nn", F32, "outproj", jobs=jobs)
    xn, got2 = post_fwd(x, out, p["post_w"], jobs=[gather_pass_job(got[0])] if next_shards else [])
    gathered = got[1] + got2[0] if next_shards else None
    return xn, dict(x=x, h=h, proj=proj, dtraw=dtraw, pooled=pooled, pre=pre, dtp=dtp, y=y, states=states,
                    mixed=mixed, out=out), gathered


def layer_bwd(dxn, p, s, D, where=None):
    reduce = where is not None
    chip, core = where if reduce else (None, None)
    hpg = D // GROUPS // HEAD_DIM
    PGW = D // GROUPS
    main = p["wmain"].shape[1]
    SH = (main + GROUPS * hpg) // 4
    dt, acs, acst, dtx, eax, dsx, cdx = s["dtp"]
    dout, d_post = post_bwd(s["out"], p["post_w"], dxn)
    dmix, _ = matmul(dout, p["wout"], "nt", F32, "dmixed")
    d_wout, _ = matmul(s["mixed"], dout, "tn", F32, "dwout")
    g_out = d_wout.reshape(4, 2, D // 4, D)
    (dy2, dproj, d_nw, d_dexp), got = gate_bwd(s["y"], s["pre"], s["proj"], p["dexp"], p["nw"], dmix, D,
                                               jobs=[pair_exchange_job([g_out])] if reduce else [])
    pair_out = pair_add(g_out, got[0][0], core, BF16) if reduce else None
    (dxs, db, dc, ddtraw, d_alog, d_bias), got = scan_bwd(
        s["pre"], dtx, eax, dsx, cdx, acs, acst, dt, s["dtraw"], p["bias"], p["alog"], s["states"], dy2,
        p["dexp"], _expand_matrix(D).T, D, jobs=[chip_exchange_job([pair_out])] if reduce else [])
    mine_out = chip_add(pair_out, got[0][0], chip, core) if reduce else None
    (dproj, d_cw, d_cb), got = conv_bwd([dxs, db, dc], s["proj"], p["cw"], dproj, D,
                                        jobs=[pair_gather_job([mine_out])] if reduce else [])
    r_out = got[0][0] if reduce else None
    dproj, dmixed, d_scale = pool_bwd(s["proj"], dmix, s["pooled"], p["mixw"], p["scale"], dproj, D)
    d_mixw = pool_dw(s["pooled"], dmixed, D)
    d_wmain, _ = matmul(s["h"], dproj, "tn", F32, "dwmain")
    d_wdt, _ = matmul(s["h"], ddtraw, "tn", F32, "dwdt")
    late = [d_wmain.reshape(1, 2, D // 2, main), d_wdt.reshape(1, 2, D // 2, GROUPS * LANES),
            d_mixw.reshape(GROUPS, 4, PGW // 4, PGW).transpose(1, 0, 2, 3).reshape(4, 2, GROUPS * PGW // 8, PGW)]
    dh_b, got = matmul(ddtraw, p["wdt"], "nt", F32, "dh_dt", jobs=[pair_exchange_job(late)] if reduce else [])
    if reduce:
        p_main, p_dt, p_mix = [pair_add(g, r, core, BF16) for g, r in zip(late, got[0])]
        p_in = jnp.concatenate([p_main[0], _from_groups(p_dt[0], hpg)], axis=1)
        pairs = [p_in.reshape(D // 2, 4, SH).transpose(1, 0, 2), p_mix]
    dh_a, got = matmul(dproj, p["wmain"], "nt", F32, "dh_main", jobs=[chip_exchange_job(pairs)] if reduce else [])
    mines = [chip_add(q, r, chip, core) for q, r in zip(pairs, got[0])] if reduce else None
    (dx, d_pre), got = rms_bwd(s["x"], p["pre_w"], dh_a, dh_b, dxn, jobs=[pair_gather_job(mines)] if reduce else [])
    reduced = dict(w_in=got[0][0], w_out=r_out, pool_mix_w=got[0][1]) if reduce else None
    grads = dict(
        pre_norm_w=d_pre[0], pool_scale=d_scale[0], conv_w=d_cw[:CONV_K], conv_b=d_cb[0],
        dt_bias=_from_groups(d_bias[0], hpg), a_log=_from_groups(d_alog[0], hpg),
        d_skip=d_dexp[0].reshape(-1, HEAD_DIM).sum(axis=-1), ssd_norm_w=d_nw[0], post_norm_w=d_post[0])
    if not reduce:
        grads.update(w_in=jnp.concatenate([d_wmain, _from_groups(d_wdt, hpg)], axis=1), pool_mix_w=d_mixw,
                     w_out=d_wout)
    return dx, grads, reduced


def local_step(x, target, params, D):
    saved = []
    for p in params:
        x, s, _ = layer_fwd(x, p, D)
        saved.append(s)
    dx, sumsq = loss_head(x, target)
    grads = [None] * len(params)
    for l in reversed(range(len(params))):
        dx, grads[l], _ = layer_bwd(dx, params[l], saved[l], D)
    return sumsq, dx, grads


SMALL = ("pre_norm_w", "pool_scale", "conv_w", "conv_b", "dt_bias", "a_log", "d_skip", "ssd_norm_w", "post_norm_w")
BIG = ("w_in", "w_out", "pool_mix_w")


def _pack(parts):
    flat = jnp.concatenate([p.reshape(-1) for p in parts])
    n = flat.shape[0]
    rows = -(-n // (LANES * LANES)) * LANES
    return jnp.pad(flat, (0, rows * LANES - n)).reshape(rows, LANES)


def _unpack(packed, shapes):
    flat, out, at = packed.reshape(-1), [], 0
    for s in shapes:
        n = math.prod(s)
        out.append(flat[at:at + n].reshape(s))
        at += n
    return out


def kernel(x, pre_norm_w, w_in, pool_mix_w, pool_scale, conv_w, conv_b, dt_bias, a_log, d_skip, ssd_norm_w, w_out, post_norm_w, loss_target, m_pre_norm_w, m_w_in, m_pool_mix_w, m_pool_scale, m_conv_w, m_conv_b, m_dt_bias, m_a_log, m_d_skip, m_ssd_norm_w, m_w_out, m_post_norm_w, v_pre_norm_w, v_w_in, v_pool_mix_w, v_pool_scale, v_conv_w, v_conv_b, v_dt_bias, v_a_log, v_d_skip, v_ssd_norm_w, v_w_out, v_post_norm_w):
    NL, D, SH = w_in.shape
    PGW = D // GROUPS
    CS = conv_w.shape[2]
    chip = (2 * lax.axis_index("x") + lax.axis_index("y")).astype(jnp.int32)
    chip1, core = chip.reshape(1), lax.axis_index("c").astype(jnp.int32).reshape(1)

    def shards(l):
        return [w_in[l].astype(BF16).reshape(2, D // 2, SH), w_out[l].astype(BF16).reshape(2, D // 4, D),
                pool_mix_w[l].astype(BF16).reshape(2, GROUPS * PGW // 8, PGW),
                conv_w[l].reshape(2, CONV_K * CS // (2 * LANES), LANES)]

    def params(l, g):
        g_in, g_out, g_mix, g_cw = g
        w_in_l = jnp.concatenate([g_in[k].reshape(D, SH) for k in range(4)], axis=1)
        mix_l = g_mix.reshape(4, GROUPS, PGW // 4, PGW).transpose(1, 0, 2, 3).reshape(GROUPS, PGW, PGW)
        cw_l = g_cw.reshape(4, CONV_K, CS).transpose(1, 0, 2).reshape(CONV_K, 4 * CS)
        return layer_params(pre_norm_w[l], w_in_l, mix_l, pool_scale[l], cw_l, conv_b[l], dt_bias[l], a_log[l],
                            d_skip[l], ssd_norm_w[l], g_out.reshape(2 * D, D), post_norm_w[l], D)

    gathered = run_jobs([gather_pass_job(run_jobs([gather_send_job(shards(0))], "gather_send")[0])], "gather_pass")[0]
    h, ps, saved = x[0], [], []
    for l in range(NL):
        ps.append(params(l, gathered))
        h, s, gathered = layer_fwd(h, ps[l], D, shards(l + 1) if l + 1 < NL else None)
        saved.append(s)
    dx, sumsq = loss_head(h, loss_target[0])

    given = dict(w_in=(w_in, m_w_in, v_w_in), w_out=(w_out, m_w_out, v_w_out),
                 pool_mix_w=(pool_mix_w, m_pool_mix_w, v_pool_mix_w))
    flat = {n: [t.reshape(-1, t.shape[-1]) for t in given[n]] for n in BIG}
    done = {n: None for n in BIG}
    grads = [None] * NL
    for l in reversed(range(NL)):
        dx, grads[l], reduced = layer_bwd(dx, ps[l], saved[l], D, (chip1, core))
        for n in BIG:
            r = reduced[n]
            done[n] = adamw_layer(r.reshape(-1, r.shape[-1]), *flat[n], l, done[n], "adamw_" + n)

    small_shapes = [(NL,) + grads[0][n].shape for n in SMALL]
    packed = _pack([0.5 / D * sumsq[0, :1]] + [jnp.stack([g[n] for g in grads]) for n in SMALL])
    total = allreduce_small(packed)
    loss, *small = _unpack(total, [(1,)] + small_shapes)
    small = dict(zip(SMALL, small))
    small["conv_w"] = lax.dynamic_slice_in_dim(small["conv_w"], chip * CS, CS, axis=2)

    given_small = dict(
        pre_norm_w=(pre_norm_w, m_pre_norm_w, v_pre_norm_w), pool_scale=(pool_scale, m_pool_scale, v_pool_scale),
        conv_w=(conv_w, m_conv_w, v_conv_w), conv_b=(conv_b, m_conv_b, v_conv_b),
        dt_bias=(dt_bias, m_dt_bias, v_dt_bias), a_log=(a_log, m_a_log, v_a_log),
        d_skip=(d_skip, m_d_skip, v_d_skip), ssd_norm_w=(ssd_norm_w, m_ssd_norm_w, v_ssd_norm_w),
        post_norm_w=(post_norm_w, m_post_norm_w, v_post_norm_w))
    shapes = [given_small[n][0].shape for n in SMALL]
    upd = adamw(_pack([small[n] for n in SMALL]), *[_pack([given_small[n][i] for n in SMALL]) for i in range(3)],
                "adamw_small")
    upd = [dict(zip(SMALL, _unpack(u, shapes))) for u in upd]

    out = {n: (small[n], upd[0][n], upd[1][n], upd[2][n]) for n in SMALL}
    for n in BIG:
        out[n] = tuple(t.reshape(given[n][0].shape) for t in done[n])

    order = ("pre_norm_w", "w_in", "pool_mix_w", "pool_scale", "conv_w", "conv_b", "dt_bias", "a_log", "d_skip",
             "ssd_norm_w", "w_out", "post_norm_w")
    return (loss.reshape(()), dx[None], *[out[n][0] for n in order], *[out[n][1] for n in order],
            *[out[n][2] for n in order], *[out[n][3] for n in order])
```

```python
import functools
import math

import jax
import jax.numpy as jnp
from jax import lax
from jax.experimental import pallas as pl
from jax.experimental.pallas import tpu as pltpu

F32 = jnp.float32
BF16 = jnp.bfloat16

NORM_EPS = 1e-6
HEAD_DIM = 64
STATE = 128
GROUPS = 4
POOL_WINDOWS = (2, 4, 8, 16)
POOL_HALO = 16
CONV_K = 4
CONV_HALO = 8
SCAN_CHUNK = 256
LANES = 128
VMEM_LIMIT = 52 * 1024 * 1024

ADAM_LR = 0.001
ADAM_B1 = 0.9
ADAM_B2 = 0.999
ADAM_EPS = 1e-08
ADAM_WD = 0.01
ADAM_STEP = 10

MESH = pl.DeviceIdType.MESH

NN = (((1,), (0,)), ((), ()))
NT = (((1,), (1,)), ((), ()))
TN = (((0,), (0,)), ((), ()))

_ANY = pl.BlockSpec(memory_space=pl.ANY)


def _params(*sem):
    return pltpu.CompilerParams(dimension_semantics=sem, vmem_limit_bytes=VMEM_LIMIT)


def _pick(dim, pref):
    if dim <= pref:
        return dim
    t = (pref // LANES) * LANES
    while t > LANES and dim % t:
        t -= LANES
    assert dim % t == 0, (dim, pref)
    return t


def _dot(a, b, dn=NN):
    return lax.dot_general(a, b, dn, preferred_element_type=F32)


def _split3(a):
    hi = a.astype(BF16)
    r = a - hi.astype(F32)
    mid = r.astype(BF16)
    return hi, mid, (r - mid.astype(F32)).astype(BF16)


def _dot_sel(a, e):
    hi, mid, lo = _split3(a)
    return (_dot(lo, e) + _dot(mid, e)) + _dot(hi, e)


def _sel_dot(e, b):
    hi, mid, lo = _split3(b)
    return (_dot(e, lo) + _dot(e, mid)) + _dot(e, hi)


def _sigmoid(v):
    return 1.0 / (1.0 + jnp.exp(-v))


def _silu_and_grad(v):
    s = _sigmoid(v)
    return v * s, s * (1.0 + v * (1.0 - s))


def _row(i, shape):
    return lax.broadcasted_iota(jnp.int32, shape, 0) + i


def _sds(shape, dtype):
    return jax.ShapeDtypeStruct(tuple(shape), dtype)


class Job:
    def __init__(self, ins, outs, aliased, nsem, start, finish):
        self.ins, self.outs, self.aliased, self.nsem, self.start, self.finish = ins, outs, aliased, nsem, start, finish


def _place():
    x, y, c = lax.axis_index("x"), lax.axis_index("y"), lax.axis_index("c")
    return x, y, c, [(1 - x, y), (x, 1 - y), (1 - x, 1 - y)]


def _remote(src, dst, send_sem, recv_sem, device):
    return pltpu.make_async_remote_copy(src_ref=src, dst_ref=dst, send_sem=send_sem, recv_sem=recv_sem,
                                        device_id=device, device_id_type=MESH)


def gather_send_job(arrs):
    n = len(arrs)

    def copies(ins, outs, send, recv):
        x, y, c, chips = _place()
        mine = 2 * x + y
        out = []
        for a in range(n):
            out.append(_remote(ins[a], outs[a].at[mine], send.at[4 * a + 3], recv.at[4 * a + 3], (x, y, 1 - c)))
            for j, chip in enumerate(chips):
                out.append(_remote(ins[a].at[c], outs[a].at[mine, c], send.at[4 * a + j], recv.at[4 * a + j],
                                   (*chip, c)))
        return out

    def start(ins, outs, send, recv):
        for cp in copies(ins, outs, send, recv):
            cp.start()

    def finish(ins, outs, send, recv):
        x, y, c, chips = _place()
        for a in range(n):
            for j, chip in enumerate(chips):
                landed = outs[a].at[2 * chip[0] + chip[1], c]
                _remote(landed, landed, send.at[4 * a + j], recv.at[4 * a + j], (x, y, 1 - c)).wait_recv()
            twin = outs[a].at[2 * x + y]
            _remote(twin, twin, send.at[4 * a + 3], recv.at[4 * a + 3], (x, y, 1 - c)).wait_recv()
        for cp in copies(ins, outs, send, recv):
            cp.wait_send()

    return Job(list(arrs), [_sds((4,) + a.shape, a.dtype) for a in arrs], False, 4 * n, start, finish)


def gather_pass_job(bufs):
    n = len(bufs)

    def copies(outs, send, recv):
        x, y, c, chips = _place()
        out = []
        for a in range(n):
            for j, chip in enumerate(chips):
                landed = outs[a].at[2 * chip[0] + chip[1], c]
                out.append(_remote(landed, landed, send.at[3 * a + j], recv.at[3 * a + j], (x, y, 1 - c)))
        return out

    def start(ins, outs, send, recv):
        for cp in copies(outs, send, recv):
            cp.start()

    def finish(ins, outs, send, recv):
        x, y, c, chips = _place()
        for a in range(n):
            for j, chip in enumerate(chips):
                passed = outs[a].at[2 * chip[0] + chip[1], 1 - c]
                _remote(passed, passed, send.at[3 * a + j], recv.at[3 * a + j], (x, y, 1 - c)).wait_recv()
        for cp in copies(outs, send, recv):
            cp.wait_send()

    return Job(list(bufs), [_sds(b.shape, b.dtype) for b in bufs], True, 3 * n, start, finish)


def pair_exchange_job(arrs):
    n = len(arrs)

    def copies(ins, outs, send, recv):
        x, y, c, _ = _place()
        return [_remote(ins[a].at[:, 1 - c], outs[a], send.at[a], recv.at[a], (x, y, 1 - c)) for a in range(n)]

    def start(ins, outs, send, recv):
        for cp in copies(ins, outs, send, recv):
            cp.start()

    def finish(ins, outs, send, recv):
        for cp in copies(ins, outs, send, recv):
            cp.wait()

    return Job(list(arrs), [_sds(a.shape[:1] + a.shape[2:], a.dtype) for a in arrs], False, n, start, finish)


def chip_exchange_job(arrs):
    n = len(arrs)

    def copies(ins, outs, send, recv):
        x, y, c, chips = _place()
        return [_remote(ins[a].at[2 * chip[0] + chip[1]], outs[a].at[j], send.at[3 * a + j], recv.at[3 * a + j],
                        (*chip, c)) for a in range(n) for j, chip in enumerate(chips)]

    def start(ins, outs, send, recv):
        for cp in copies(ins, outs, send, recv):
            cp.start()

    def finish(ins, outs, send, recv):
        for cp in copies(ins, outs, send, recv):
            cp.wait()

    return Job(list(arrs), [_sds((3,) + a.shape[1:], a.dtype) for a in arrs], False, 3 * n, start, finish)


def pair_gather_job(bufs):
    n = len(bufs)

    def copies(outs, send, recv):
        x, y, c, _ = _place()
        return [_remote(outs[a].at[c], outs[a].at[c], send.at[a], recv.at[a], (x, y, 1 - c)) for a in range(n)]

    def start(ins, outs, send, recv):
        for cp in copies(outs, send, recv):
            cp.start()

    def finish(ins, outs, send, recv):
        for cp in copies(outs, send, recv):
            cp.wait()

    return Job(list(bufs), [_sds(b.shape, b.dtype) for b in bufs], True, n, start, finish)


def _call(body, *, grid, in_specs, out_specs, out_shape, name, sem, args, scratch_shapes=(), jobs=(), aliases=None):
    in_specs, out_specs, out_shape, scratch_shapes = list(in_specs), list(out_specs), list(out_shape), list(scratch_shapes)
    aliases = dict(aliases or {})
    n_in, n_out, n_scr = len(in_specs), len(out_specs), len(scratch_shapes)
    if jobs:
        sem = ("arbitrary",) * len(grid)
    at_in, at_out = n_in, n_out
    for j in jobs:
        if j.aliased:
            aliases.update({at_in + i: at_out + i for i in range(len(j.ins))})
        at_in, at_out = at_in + len(j.ins), at_out + len(j.outs)

    def wrapped(*refs):
        ins, p = refs[:n_in], n_in
        jins = []
        for j in jobs:
            jins.append(refs[p:p + len(j.ins)])
            p += len(j.ins)
        outs, p = refs[p:p + n_out], p + n_out
        jouts = []
        for j in jobs:
            jouts.append(refs[p:p + len(j.outs)])
            p += len(j.outs)
        scr, sems = refs[p:p + n_scr], refs[p + n_scr:]

        def start():
            for k, j in enumerate(jobs):
                j.start(jins[k], jouts[k], sems[2 * k], sems[2 * k + 1])

        def finish():
            for k, j in enumerate(jobs):
                j.finish(jins[k], jouts[k], sems[2 * k], sems[2 * k + 1])

        if jobs and grid:
            ids = [pl.program_id(d) for d in range(len(grid))]
            pl.when(functools.reduce(jnp.logical_and, [i == 0 for i in ids]))(start)
            body(*ins, *outs, *scr)
            pl.when(functools.reduce(jnp.logical_and, [i == g - 1 for i, g in zip(ids, grid)]))(finish)
        else:
            start()
            body(*ins, *outs, *scr)
            finish()

    kwargs = dict(grid=grid) if grid else {}
    res = pl.pallas_call(
        wrapped, in_specs=in_specs + [_ANY] * (at_in - n_in), out_specs=out_specs + [_ANY] * (at_out - n_out),
        out_shape=out_shape + [o for j in jobs for o in j.outs],
        scratch_shapes=scratch_shapes + [pltpu.SemaphoreType.DMA((j.nsem,)) for j in jobs for _ in range(2)],
        input_output_aliases=aliases, name=name,
        compiler_params=pltpu.CompilerParams(dimension_semantics=sem, vmem_limit_bytes=VMEM_LIMIT) if grid
        else pltpu.CompilerParams(vmem_limit_bytes=VMEM_LIMIT), **kwargs)(*args, *[a for j in jobs for a in j.ins])
    res = list(res)
    outs, rest, per_job = res[:n_out], res[n_out:], []
    for j in jobs:
        per_job.append(rest[:len(j.outs)])
        rest = rest[len(j.outs):]
    return outs, per_job


def run_jobs(jobs, name):
    return _call(lambda: None, grid=(), in_specs=[], out_specs=[], out_shape=[], name=name, sem=(), args=(), jobs=jobs)[1]


def pair_add(g, r, core, out_dtype):
    S, _, R, C = g.shape
    T = _pick(R, 256 if C <= 4096 else 128)
    assert R % T == 0

    def body(c_ref, g_ref, r_ref, o_ref):
        o_ref[0] = (g_ref[0, 0] + r_ref[0]).astype(o_ref.dtype)

    return pl.pallas_call(
        body,
        grid_spec=pltpu.PrefetchScalarGridSpec(
            num_scalar_prefetch=1, grid=(S, R // T),
            in_specs=[pl.BlockSpec((1, 1, T, C), lambda k, i, c_ref: (k, c_ref[0], i, 0)),
                      pl.BlockSpec((1, T, C), lambda k, i, c_ref: (k, i, 0))],
            out_specs=pl.BlockSpec((1, T, C), lambda k, i, c_ref: (k, i, 0))),
        out_shape=_sds((S, R, C), out_dtype), name="pair_add",
        compiler_params=_params("parallel", "parallel"))(core, g, r)


def chip_add(p, r, chip, core):
    _, R, C = p.shape
    T = _pick(R, 256)
    assert R % T == 0

    def body(k_ref, c_ref, p_ref, r0_ref, r1_ref, r2_ref, o_ref):
        o_ref[0] = ((p_ref[0].astype(F32) + r0_ref[0].astype(F32)) + r1_ref[0].astype(F32)) + r2_ref[0].astype(F32)

    slot = lambda j: pl.BlockSpec((1, T, C), lambda i, k_ref, c_ref: (j, i, 0))
    return pl.pallas_call(
        body,
        grid_spec=pltpu.PrefetchScalarGridSpec(
            num_scalar_prefetch=2, grid=(R // T,),
            in_specs=[pl.BlockSpec((1, T, C), lambda i, k_ref, c_ref: (k_ref[0], i, 0)), slot(0), slot(1), slot(2)],
            out_specs=pl.BlockSpec((1, T, C), lambda i, k_ref, c_ref: (c_ref[0], i, 0))),
        out_shape=_sds((2, R, C), F32), name="chip_add",
        compiler_params=_params("parallel"))(chip, core, p, r, r, r)


def allreduce_small(v):
    R = v.shape[0]

    def body(v_ref, o_ref, buf, send_sems, recv_sems, local_sem):
        x, y, c, chips = _place()
        me, sibling = (x, y, c), (x, y, 1 - c)

        def rows(px, py, pc):
            return buf.at[pl.ds((4 * px + 2 * py + pc) * R, R), :]

        def copy(k, block, to, src=None):
            return _remote(rows(*block) if src is None else src, rows(*block), send_sems.at[k], recv_sems.at[k], to)

        mine = pltpu.make_async_copy(v_ref, rows(*me), local_sem)
        mine.start()
        first = [copy(0, me, sibling, src=v_ref)]
        first += [copy(1 + j, me, (*chip, c), src=v_ref) for j, chip in enumerate(chips)]
        for cp in first:
            cp.start()
        passed = [copy(4 + j, (*chip, c), sibling) for j, chip in enumerate(chips)]
        for j, chip in enumerate(chips):
            copy(1 + j, (*chip, c), me).wait_recv()
            passed[j].start()
        copy(0, sibling, me).wait_recv()
        for j, chip in enumerate(chips):
            copy(4 + j, (*chip, 1 - c), me).wait_recv()
        for cp in first + passed:
            cp.wait_send()
        mine.wait()
        acc = buf[0:R, :]
        for d in range(1, 8):
            acc = acc + buf[d * R:(d + 1) * R, :]
        o_ref[...] = acc

    return pl.pallas_call(
        body, in_specs=[pl.BlockSpec(memory_space=pltpu.VMEM)], out_specs=pl.BlockSpec(memory_space=pltpu.VMEM),
        out_shape=_sds((R, LANES), F32),
        scratch_shapes=[pltpu.VMEM((8 * R, LANES), F32), pltpu.SemaphoreType.DMA((7,)),
                        pltpu.SemaphoreType.DMA((7,)), pltpu.SemaphoreType.DMA],
        name="allreduce_small", compiler_params=pltpu.CompilerParams(vmem_limit_bytes=VMEM_LIMIT))(v)


def matmul(a, b, mode, out_dtype, name, tm=512, tn=1024, tk=4608, jobs=()):
    if mode == "nn":
        (M, K), (K2, N) = a.shape, b.shape
    elif mode == "nt":
        (M, K), (N, K2) = a.shape, b.shape
    else:
        (K, M), (K2, N) = a.shape, b.shape
    assert K == K2
    tm, tn, tk = _pick(M, tm), _pick(N, tn), _pick(K, tk)
    nk = K // tk
    dn = {"nn": NN, "nt": NT, "tn": TN}[mode]

    def body(a_ref, b_ref, o_ref, *acc):
        part = _dot(a_ref[...].astype(BF16), b_ref[...].astype(BF16), dn)
        if nk == 1:
            o_ref[...] = part.astype(o_ref.dtype)
            return
        acc_ref, = acc
        k = pl.program_id(2)

        @pl.when(k == 0)
        def _():
            acc_ref[...] = part

        @pl.when(jnp.logical_and(k > 0, k < nk - 1))
        def _():
            acc_ref[...] += part

        @pl.when(k == nk - 1)
        def _():
            o_ref[...] = (acc_ref[...] + part).astype(o_ref.dtype)

    a_spec = (pl.BlockSpec((tk, tm), lambda i, j, k: (k, i)) if mode == "tn"
              else pl.BlockSpec((tm, tk), lambda i, j, k: (i, k)))
    b_spec = (pl.BlockSpec((tn, tk), lambda i, j, k: (j, k)) if mode == "nt"
              else pl.BlockSpec((tk, tn), lambda i, j, k: (k, j)))
    outs, per_job = _call(
        body, grid=(M // tm, N // tn, nk), in_specs=[a_spec, b_spec],
        out_specs=[pl.BlockSpec((tm, tn), lambda i, j, k: (i, j))], out_shape=[_sds((M, N), out_dtype)],
        scratch_shapes=[pltpu.VMEM((tm, tn), F32)] if nk > 1 else [], name=name,
        sem=("parallel", "parallel", "arbitrary"), args=(a, b), jobs=jobs)
    return outs[0], per_job


def rms_fwd(x, w):
    L, D = x.shape
    T = _pick(L, 256)

    def body(x_ref, w_ref, h_ref):
        xv = x_ref[...]
        r = lax.rsqrt(jnp.mean(xv * xv, axis=-1, keepdims=True) + NORM_EPS)
        h_ref[...] = (xv * r * w_ref[...]).astype(h_ref.dtype)

    return pl.pallas_call(
        body, grid=(L // T,),
        in_specs=[pl.BlockSpec((T, D), lambda i: (i, 0)), pl.BlockSpec((1, D), lambda i: (0, 0))],
        out_specs=pl.BlockSpec((T, D), lambda i: (i, 0)),
        out_shape=_sds((L, D), BF16), name="rms_fwd", compiler_params=_params("parallel"))(x, w)


def post_fwd(x, o, w, jobs=()):
    L, D = x.shape
    T = _pick(L, 256)

    def body(x_ref, o_ref, w_ref, y_ref):
        ov = o_ref[...]
        r = lax.rsqrt(jnp.mean(ov * ov, axis=-1, keepdims=True) + NORM_EPS)
        y_ref[...] = x_ref[...] + ov * r * w_ref[...]

    row = pl.BlockSpec((T, D), lambda i: (i, 0))
    outs, per_job = _call(
        body, grid=(L // T,), in_specs=[row, row, pl.BlockSpec((1, D), lambda i: (0, 0))], out_specs=[row],
        out_shape=[_sds((L, D), F32)], name="post_fwd", sem=("parallel",), args=(x, o, w), jobs=jobs)
    return outs[0], per_job


def _rms_bwd_math(xv, w, dy):
    r = lax.rsqrt(jnp.mean(xv * xv, axis=-1, keepdims=True) + NORM_EPS)
    xhat = xv * r
    g = dy * w
    dx = r * (g - xhat * jnp.mean(g * xhat, axis=-1, keepdims=True))
    return dx, jnp.sum(dy * xhat, axis=0, keepdims=True)


def post_bwd(o, w, dxn):
    L, D = o.shape
    T = _pick(L, 256)

    def body(o_ref, w_ref, d_ref, do_ref, dw_ref):
        dx, dw = _rms_bwd_math(o_ref[...], w_ref[...], d_ref[...])
        do_ref[...] = dx.astype(do_ref.dtype)

        @pl.when(pl.program_id(0) == 0)
        def _():
            dw_ref[...] = jnp.zeros_like(dw_ref)

        dw_ref[...] += dw

    row = pl.BlockSpec((T, D), lambda i: (i, 0))
    vec = pl.BlockSpec((1, D), lambda i: (0, 0))
    return pl.pallas_call(
        body, grid=(L // T,), in_specs=[row, vec, row], out_specs=[row, vec],
        out_shape=[_sds((L, D), BF16), _sds((1, D), F32)],
        name="post_bwd", compiler_params=_params("arbitrary"))(o, w, dxn)


def rms_bwd(x, w, dh_a, dh_b, dxn, jobs=()):
    L, D = x.shape
    T = _pick(L, 256)

    def body(x_ref, w_ref, a_ref, b_ref, d_ref, dx_ref, dw_ref):
        dx, dw = _rms_bwd_math(x_ref[...], w_ref[...], a_ref[...] + b_ref[...])
        dx_ref[...] = d_ref[...] + dx

        @pl.when(pl.program_id(0) == 0)
        def _():
            dw_ref[...] = jnp.zeros_like(dw_ref)

        dw_ref[...] += dw

    row = pl.BlockSpec((T, D), lambda i: (i, 0))
    vec = pl.BlockSpec((1, D), lambda i: (0, 0))
    return _call(
        body, grid=(L // T,), in_specs=[row, vec, row, row, row], out_specs=[row, vec],
        out_shape=[_sds((L, D), F32), _sds((1, D), F32)],
        name="rms_bwd", sem=("arbitrary",), args=(x, w, dh_a, dh_b, dxn), jobs=jobs)


def loss_head(y, target):
    L, D = y.shape
    T = _pick(L, 256)

    def body(y_ref, t_ref, d_ref, s_ref):
        e = y_ref[...] - t_ref[...]
        d_ref[...] = e * (1.0 / D)

        @pl.when(pl.program_id(0) == 0)
        def _():
            s_ref[...] = jnp.zeros_like(s_ref)

        s_ref[...] += jnp.sum(e * e)

    row = pl.BlockSpec((T, D), lambda i: (i, 0))
    return pl.pallas_call(
        body, grid=(L // T,), in_specs=[row, row],
        out_specs=[row, pl.BlockSpec((8, LANES), lambda i: (0, 0))],
        out_shape=[_sds((L, D), F32), _sds((8, LANES), F32)],
        name="loss_head", compiler_params=_params("arbitrary"))(y, target)


def _window_sums(xe, w, back):
    n = xe.shape[0]
    s, k = xe, 1
    while k < w:
        s = s + pltpu.roll(s, k if back else n - k, 0)
        k *= 2
    return s


def pool_fwd(proj, mixw, scale, D):
    L = proj.shape[0]
    PGW = D // GROUPS
    T = _pick(L, 256)
    hb = T // POOL_HALO

    def body(u_ref, halo_ref, g_ref, mw_ref, sc_ref, y_ref, p_ref):
        i = pl.program_id(0)
        u = u_ref[...]
        halo = jnp.where(i > 0, halo_ref[...], 0.0)
        xe = jnp.concatenate([halo, u], axis=0)
        t1 = _row(i * T + 1, (T, 1))
        for g, w in enumerate(POOL_WINDOWS):
            sl = slice(g * PGW, (g + 1) * PGW)
            win = _window_sums(xe[:, sl], w, True)[POOL_HALO:, :]
            cnt = jnp.minimum(t1, w).astype(F32)
            pooled = (win / cnt - u[:, sl]).astype(BF16)
            p_ref[:, sl] = pooled
            mixed = _dot(pooled, mw_ref[g])
            gate = g_ref[:, sl]
            y_ref[:, sl] = (mixed * sc_ref[:, sl] * (gate * _sigmoid(gate))).astype(BF16)

    return pl.pallas_call(
        body, grid=(L // T,),
        in_specs=[pl.BlockSpec((T, D), lambda i: (i, 0)),
                  pl.BlockSpec((POOL_HALO, D), lambda i: (jnp.maximum(i * hb - 1, 0), 0)),
                  pl.BlockSpec((T, D), lambda i: (i, 1)),
                  pl.BlockSpec((GROUPS, PGW, PGW), lambda i: (0, 0, 0)),
                  pl.BlockSpec((1, D), lambda i: (0, 0))],
        out_specs=[pl.BlockSpec((T, D), lambda i: (i, 0)), pl.BlockSpec((T, D), lambda i: (i, 0))],
        out_shape=[_sds((L, 2 * D), BF16), _sds((L, D), BF16)],
        name="pool_fwd", compiler_params=_params("parallel"))(proj, proj, proj, mixw, scale)


def pool_bwd(proj, dmix, pooled, mixw, scale, dproj, D):
    L = proj.shape[0]
    PGW = D // GROUPS
    T = _pick(L, 256)
    hb = T // POOL_HALO
    nT = L // T

    def body(g_ref, gh_ref, dy_ref, dyh_ref, p_ref, mw_ref, sc_ref, old_ref, dp_ref, dm_ref, ds_ref):
        i = pl.program_id(0)
        t1 = _row(i * T + 1, (T, 1))
        th1 = _row((i + 1) * T + 1, (POOL_HALO, 1))
        live = i < nT - 1

        @pl.when(i == 0)
        def _():
            ds_ref[...] = jnp.zeros_like(ds_ref)

        for g, w in enumerate(POOL_WINDOWS):
            sl = slice(g * PGW, (g + 1) * PGW)
            sc = sc_ref[:, sl]
            gate, dy = g_ref[:, sl], dy_ref[:, sl]
            sg, dsg = _silu_and_grad(gate)
            mixed = _dot(p_ref[:, sl], mw_ref[g])
            dmixed = (dy * sc * sg).astype(BF16)
            dm_ref[:, sl] = dmixed
            dp_ref[:, D + g * PGW:D + (g + 1) * PGW] = (dy * mixed * sc * dsg).astype(BF16)
            ds_ref[:, sl] += jnp.sum(dy * mixed * sg, axis=0, keepdims=True)
            dpool = _dot(dmixed, mw_ref[g], NT)
            gate_h = gh_ref[:, sl]
            dmixed_h = (dyh_ref[:, sl] * sc * (gate_h * _sigmoid(gate_h))).astype(BF16)
            dpool_h = jnp.where(live, _dot(dmixed_h, mw_ref[g], NT), 0.0)
            q = dpool / jnp.minimum(t1, w).astype(F32)
            q_h = dpool_h / jnp.minimum(th1, w).astype(F32)
            qe = jnp.concatenate([q, q_h], axis=0)
            dp_ref[:, sl] = (_window_sums(qe, w, False)[:T, :] - dpool).astype(BF16)

    nxt = lambda i: jnp.minimum((i + 1) * hb, L // POOL_HALO - 1)
    row = lambda c: pl.BlockSpec((T, D), lambda i: (i, c))
    return pl.pallas_call(
        body, grid=(nT,),
        in_specs=[row(1), pl.BlockSpec((POOL_HALO, D), lambda i: (nxt(i), 1)),
                  row(0), pl.BlockSpec((POOL_HALO, D), lambda i: (nxt(i), 0)),
                  row(0), pl.BlockSpec((GROUPS, PGW, PGW), lambda i: (0, 0, 0)),
                  pl.BlockSpec((1, D), lambda i: (0, 0)), _ANY],
        out_specs=[pl.BlockSpec((T, 2 * D), lambda i: (i, 0)), row(0), pl.BlockSpec((1, D), lambda i: (0, 0))],
        out_shape=[_sds(dproj.shape, dproj.dtype), _sds((L, D), BF16), _sds((1, D), F32)],
        input_output_aliases={7: 0},
        name="pool_bwd", compiler_params=_params("arbitrary"))(proj, proj, dmix, dmix, pooled, mixw, scale, dproj)


def pool_dw(pooled, dmixed, D):
    L = pooled.shape[0]
    PGW = D // GROUPS
    tk = _pick(L, 1024)
    nk = L // tk

    def body(p_ref, d_ref, o_ref):
        @pl.when(pl.program_id(1) == 0)
        def _():
            o_ref[...] = jnp.zeros_like(o_ref)

        o_ref[0] += _dot(p_ref[...], d_ref[...], TN)

    blk = pl.BlockSpec((tk, PGW), lambda g, k: (k, g))
    return pl.pallas_call(
        body, grid=(GROUPS, nk), in_specs=[blk, blk],
        out_specs=pl.BlockSpec((1, PGW, PGW), lambda g, k: (g, 0, 0)),
        out_shape=_sds((GROUPS, PGW, PGW), F32), name="pool_dw",
        compiler_params=_params("parallel", "arbitrary"))(pooled, dmixed)


def conv_fwd(proj, cw, cb, D):
    L = proj.shape[0]
    C = cw.shape[1]
    assert (3 * D) % C == 0
    cblk = (3 * D) // C
    T = _pick(L, 256)
    hb = T // CONV_HALO

    def body(u_ref, halo_ref, w_ref, b_ref, o_ref):
        i = pl.program_id(0)
        u = u_ref[...]
        xe = jnp.concatenate([jnp.where(i > 0, halo_ref[...], 0.0), u], axis=0)
        acc = b_ref[...] + w_ref[CONV_K - 1:CONV_K, :] * u
        for k in range(CONV_K - 1):
            acc = acc + w_ref[k:k + 1, :] * pltpu.roll(xe, CONV_K - 1 - k, 0)[CONV_HALO:, :]
        o_ref[...] = acc

    return pl.pallas_call(
        body, grid=(L // T,),
        in_specs=[pl.BlockSpec((T, C), lambda i: (i, cblk)),
                  pl.BlockSpec((CONV_HALO, C), lambda i: (jnp.maximum(i * hb - 1, 0), cblk)),
                  pl.BlockSpec((CONV_K, C), lambda i: (0, 0)),
                  pl.BlockSpec((1, C), lambda i: (0, 0))],
        out_specs=pl.BlockSpec((T, C), lambda i: (i, 0)),
        out_shape=_sds((L, C), F32), name="conv_fwd", compiler_params=_params("parallel"))(proj, proj, cw, cb)


def conv_bwd(dparts, proj, cw, dproj, D, jobs=()):
    L = proj.shape[0]
    C = cw.shape[1]
    cblk = (3 * D) // C
    T = _pick(L, 256)
    hb = T // CONV_HALO
    nT = L // T
    widths = [p.shape[1] for p in dparts]
    assert sum(widths) == C
    n = len(dparts)

    def body(*refs):
        d_refs, dn_refs = refs[:n], refs[n:2 * n]
        u_ref, up_ref, w_ref, old_ref, dr_ref, dw_ref, db_ref = refs[2 * n:]
        i = pl.program_id(0)

        @pl.when(i == 0)
        def _():
            dw_ref[...] = jnp.zeros_like(dw_ref)
            db_ref[...] = jnp.zeros_like(db_ref)

        at = 0
        for d_ref, dn_ref, wd in zip(d_refs, dn_refs, widths):
            sl = slice(at, at + wd)
            at += wd
            d = d_ref[...]
            u = u_ref[:, sl]
            de = jnp.concatenate([d, jnp.where(i < nT - 1, dn_ref[...], 0.0)], axis=0)
            ue = jnp.concatenate([jnp.where(i > 0, up_ref[:, sl], 0.0), u], axis=0)
            acc = w_ref[CONV_K - 1:CONV_K, sl] * d
            dw_ref[CONV_K - 1:CONV_K, sl] += jnp.sum(d * u, axis=0, keepdims=True)
            for k in range(CONV_K - 1):
                sh = CONV_K - 1 - k
                acc = acc + w_ref[k:k + 1, sl] * pltpu.roll(de, T + CONV_HALO - sh, 0)[:T, :]
                dw_ref[k:k + 1, sl] += jnp.sum(d * pltpu.roll(ue, sh, 0)[CONV_HALO:, :], axis=0, keepdims=True)
            dr_ref[:, sl] = acc.astype(dr_ref.dtype)
            db_ref[:, sl] += jnp.sum(d, axis=0, keepdims=True)

    nxt = lambda i: jnp.minimum((i + 1) * hb, L // CONV_HALO - 1)
    return _call(
        body, grid=(nT,),
        in_specs=[pl.BlockSpec((T, wd), lambda i: (i, 0)) for wd in widths]
        + [pl.BlockSpec((CONV_HALO, wd), lambda i: (nxt(i), 0)) for wd in widths]
        + [pl.BlockSpec((T, C), lambda i: (i, cblk)),
           pl.BlockSpec((CONV_HALO, C), lambda i: (jnp.maximum(i * hb - 1, 0), cblk)),
           pl.BlockSpec((CONV_K, C), lambda i: (0, 0)), _ANY],
        out_specs=[pl.BlockSpec((T, C), lambda i: (i, cblk)),
                   pl.BlockSpec((8, C), lambda i: (0, 0)),
                   pl.BlockSpec((1, C), lambda i: (0, 0))],
        out_shape=[_sds(dproj.shape, dproj.dtype), _sds((8, C), F32), _sds((1, C), F32)],
        aliases={2 * n + 3: 0}, name="conv_bwd", sem=("arbitrary",),
        args=(*dparts, *dparts, proj, proj, cw, dproj), jobs=jobs)


def _softplus(v):
    y = jnp.exp(-jnp.abs(v))
    u = 1.0 + y
    log1p = jnp.where(u == 1.0, y, jnp.log(u) * y / jnp.where(u == 1.0, 1.0, u - 1.0))
    return jnp.maximum(v, 0.0) + log1p


def dt_prep(dtraw, bias, alog, expand, D):
    L = dtraw.shape[0]
    GC = D // GROUPS
    HPG = GC // HEAD_DIM
    Q = _pick(L, SCAN_CHUNK)
    nc = L // Q

    def body(r_ref, b_ref, a_ref, e_ref, dt_ref, acs_ref, acst_ref, dtx_ref, eax_ref, dsx_ref, cdx_ref):
        valid = lax.broadcasted_iota(jnp.int32, (1, LANES), 1) < HPG
        dt = jnp.where(valid, _softplus(r_ref[...] + b_ref[...]), 0.0)
        adt = dt * -jnp.exp(a_ref[...])
        tril = (_row(0, (Q, Q)) >= lax.broadcasted_iota(jnp.int32, (Q, Q), 1)).astype(BF16)
        acs = _sel_dot(tril, adt)
        last = acs[Q - 1:Q, :]
        dt_ref[...] = dt
        acs_ref[...] = acs
        acst_ref[...] = acs.T
        e = e_ref[...]
        dtx_ref[...] = _dot_sel(dt, e)
        eax_ref[...] = jnp.exp(_dot_sel(acs, e))
        dsx_ref[...] = jnp.exp(_dot_sel(last - acs, e))
        cdx_ref[0] = jnp.exp(_dot_sel(jnp.broadcast_to(last, (8, LANES)), e))

    head = pl.BlockSpec((Q, LANES), lambda g, c: (c, g))
    hvec = pl.BlockSpec((1, LANES), lambda g, c: (0, g))
    chan = pl.BlockSpec((Q, GC), lambda g, c: (c, g))
    return pl.pallas_call(
        body, grid=(GROUPS, nc),
        in_specs=[head, hvec, hvec, pl.BlockSpec((LANES, GC), lambda g, c: (0, 0))],
        out_specs=[head, head, pl.BlockSpec((LANES, Q), lambda g, c: (g, c)), chan, chan, chan,
                   pl.BlockSpec((1, 8, GC), lambda g, c: (c, 0, g))],
        out_shape=[_sds((L, GROUPS * LANES), F32), _sds((L, GROUPS * LANES), F32), _sds((GROUPS * LANES, L), F32),
                   _sds((L, D), F32), _sds((L, D), F32), _sds((L, D), F32), _sds((nc, 8, D), F32)],
        name="dt_prep", compiler_params=_params("parallel", "parallel"))(dtraw, bias, alog, expand)


def _scan_specs(L, D, Q, rev):
    GC = D // GROUPS
    nc = L // Q
    ci = (lambda c: nc - 1 - c) if rev else (lambda c: c)
    return dict(
        xs=pl.BlockSpec((Q, GC), lambda g, c: (ci(c), g)),
        b=pl.BlockSpec((Q, STATE), lambda g, c: (ci(c), D // STATE + g)),
        c=pl.BlockSpec((Q, STATE), lambda g, c: (ci(c), D // STATE + GROUPS + g)),
        chan=pl.BlockSpec((Q, GC), lambda g, c: (ci(c), g)),
        cdx=pl.BlockSpec((1, 8, GC), lambda g, c: (ci(c), 0, g)),
        head=pl.BlockSpec((Q, LANES), lambda g, c: (ci(c), g)),
        headt=pl.BlockSpec((LANES, Q), lambda g, c: (g, ci(c))),
        state=pl.BlockSpec((1, 1, STATE, GC), lambda g, c: (ci(c), g, 0, 0)),
        hvec=pl.BlockSpec((1, LANES), lambda g, c: (0, g)),
        cvec=pl.BlockSpec((1, GC), lambda g, c: (0, g)))


def scan_fwd(pre, dtx, eax, dsx, cdx, acs, acst, D):
    L = pre.shape[0]
    GC = D // GROUPS
    Q = _pick(L, SCAN_CHUNK)
    nc = L // Q
    sp = _scan_specs(L, D, Q, False)

    def body(xs_ref, b_ref, c_ref, dtx_ref, eax_ref, dsx_ref, cdx_ref, acs_ref, acst_ref, y_ref, st_ref, s_scr):
        @pl.when(pl.program_id(1) == 0)
        def _():
            s_scr[...] = jnp.zeros_like(s_scr)

        tri = _row(0, (Q, Q)) >= lax.broadcasted_iota(jnp.int32, (Q, Q), 1)
        half = lax.broadcasted_iota(jnp.int32, (1, LANES), 1) // HEAD_DIM
        xs, _ = _silu_and_grad(xs_ref[...])
        bg = _silu_and_grad(b_ref[...])[0].astype(BF16)
        cg = _silu_and_grad(c_ref[...])[0].astype(BF16)
        xdt = xs * dtx_ref[...]
        sprev = s_scr[...]
        st_ref[0, 0] = sprev
        sc = _dot(cg, bg, NT)
        yoff = _dot(cg, sprev.astype(BF16)) * eax_ref[...]
        for j in range(GC // LANES):
            ps = slice(j * LANES, (j + 1) * LANES)
            xp = xdt[:, ps]
            acc = yoff[:, ps]
            for hh in range(2):
                h = 2 * j + hh
                lm = jnp.exp(jnp.where(tri, acs_ref[:, h:h + 1] - acst_ref[h:h + 1, :], -1e30))
                xm = jnp.where(half == hh, xp, 0.0).astype(BF16)
                acc = acc + _dot((sc * lm).astype(BF16), xm)
            y_ref[:, ps] = acc
        xw = (xdt * dsx_ref[...]).astype(BF16)
        s_scr[...] = cdx_ref[0, 0:1, :] * sprev + _dot(bg, xw, TN)

    return pl.pallas_call(
        body, grid=(GROUPS, nc),
        in_specs=[sp["xs"], sp["b"], sp["c"], sp["chan"], sp["chan"], sp["chan"], sp["cdx"], sp["head"], sp["headt"]],
        out_specs=[sp["chan"], sp["state"]],
        out_shape=[_sds((L, D), F32), _sds((nc, GROUPS, STATE, GC), F32)],
        scratch_shapes=[pltpu.VMEM((STATE, GC), F32)], name="scan_fwd",
        compiler_params=_params("parallel", "arbitrary"))(pre, pre, pre, dtx, eax, dsx, cdx, acs, acst)


def scan_bwd(pre, dtx, eax, dsx, cdx, acs, acst, dt, dtraw, bias, alog, states, dy, dexp, collapse, D, jobs=()):
    L = pre.shape[0]
    GC = D // GROUPS
    Q = _pick(L, SCAN_CHUNK)
    nc = L // Q
    sp = _scan_specs(L, D, Q, True)
    rc = lambda c: nc - 1 - c

    def body(xs_ref, b_ref, c_ref, dtx_ref, eax_ref, dsx_ref, cdx_ref, acs_ref, acst_ref, dt_ref, raw_ref,
             bias_ref, alog_ref, st_ref, dy_ref, dexp_ref, col_ref,
             dxs_ref, db_ref, dc_ref, ddt_ref, dal_ref, dbi_ref, ds_scr, dx_scr):
        first = pl.program_id(1) == 0

        @pl.when(first)
        def _():
            ds_scr[...] = jnp.zeros_like(ds_scr)
            dal_ref[...] = jnp.zeros_like(dal_ref)
            dbi_ref[...] = jnp.zeros_like(dbi_ref)

        li = _row(0, (Q, Q))
        si = lax.broadcasted_iota(jnp.int32, (Q, Q), 1)
        lane = lax.broadcasted_iota(jnp.int32, (1, LANES), 1)
        half = lane // HEAD_DIM
        xs_pre, b_pre, c_pre = xs_ref[...], b_ref[...], c_ref[...]
        xs, xs_g = _silu_and_grad(xs_pre)
        bf, b_g = _silu_and_grad(b_pre)
        cf, c_g = _silu_and_grad(c_pre)
        bg, cg = bf.astype(BF16), cf.astype(BF16)
        dtx, eax, dsx = dtx_ref[...], eax_ref[...], dsx_ref[...]
        cd = cdx_ref[0, 0:1, :]
        xdt = xs * dtx
        G = dy_ref[...]
        prev = st_ref[0, 0]
        dsn = ds_scr[...]
        prev_b, dsn_b = prev.astype(BF16), dsn.astype(BF16)
        cp = _dot(cg, prev_b)
        ge_b = (G * eax).astype(BF16)
        d_c = _dot(ge_b, prev_b, NT)
        dprev = _dot(cg, ge_b, TN) + cd * dsn
        chan_a = G * cp * eax
        xw_b = (xdt * dsx).astype(BF16)
        dcd = jnp.sum(prev * dsn, axis=0, keepdims=True)
        d_b = _dot(xw_b, dsn_b, NT)
        dxw = _dot(bg, dsn_b)
        dd = dxw * xdt * dsx
        chan_a = chan_a - dd
        last_c = jnp.sum(dd, axis=0, keepdims=True) + dcd * cd
        sc = _dot(cg, bg, NT)
        sct = _dot(bg, cg, NT)
        dsc = jnp.zeros((Q, Q), F32)
        dsct = jnp.zeros((Q, Q), F32)
        dacs = jnp.zeros((Q, LANES), F32)
        for j in range(GC // LANES):
            ps = slice(j * LANES, (j + 1) * LANES)
            xp, gp = xdt[:, ps], G[:, ps]
            dxp = dxw[:, ps] * dsx[:, ps]
            for hh in range(2):
                h = 2 * j + hh
                col, row = acs_ref[:, h:h + 1], acst_ref[h:h + 1, :]
                lm = jnp.exp(jnp.where(li >= si, col - row, -1e30))
                lmt = jnp.exp(jnp.where(si >= li, row - col, -1e30))
                m, mt = sc * lm, sct * lmt
                xm = jnp.where(half == hh, xp, 0.0).astype(BF16)
                gm = jnp.where(half == hh, gp, 0.0).astype(BF16)
                dm = _dot(gm, xm, NT)
                dmt = _dot(xm, gm, NT)
                dxp = dxp + _dot(mt.astype(BF16), gm)
                dsc = dsc + dm * lm
                dsct = dsct + dmt * lmt
                rs = jnp.sum(dm * m, axis=1, keepdims=True) - jnp.sum(dmt * mt, axis=1, keepdims=True)
                dacs = dacs + jnp.where(lane == h, rs, 0.0)
            dx_scr[:, ps] = dxp
        d_c = d_c + _dot(dsc.astype(BF16), bg)
        d_b = d_b + _dot(dsct.astype(BF16), cg)
        ds_scr[...] = dprev
        dxdt = dx_scr[...]
        dxs_ref[...] = (dxdt * dtx + dexp_ref[...] * G) * xs_g
        db_ref[...] = d_b * b_g
        dc_ref[...] = d_c * c_g
        colm = col_ref[...]
        dacs = dacs + _dot_sel(chan_a, colm)
        dlast = _dot_sel(jnp.broadcast_to(last_c, (8, GC)), colm)[0:1, :]
        dacs = dacs + jnp.where(_row(0, (Q, 1)) == Q - 1, dlast, 0.0)
        dadt = _sel_dot((si >= li).astype(BF16), dacs)
        a = -jnp.exp(alog_ref[...])
        dt = dt_ref[...]
        ddt = dadt * a + _dot_sel(dxdt * xs, colm)
        dal_ref[0:1, :] += jnp.sum(dadt * dt * a, axis=0, keepdims=True)
        draw = ddt * _sigmoid(raw_ref[...] + bias_ref[...])
        dbi_ref[0:1, :] += jnp.sum(draw, axis=0, keepdims=True)
        ddt_ref[...] = draw.astype(ddt_ref.dtype)

    acc = pl.BlockSpec((8, LANES), lambda g, c: (0, g))
    return _call(
        body, grid=(GROUPS, nc),
        in_specs=[sp["xs"], sp["b"], sp["c"], sp["chan"], sp["chan"], sp["chan"], sp["cdx"], sp["head"], sp["headt"],
                  sp["head"], sp["head"], sp["hvec"], sp["hvec"], sp["state"], sp["chan"], sp["cvec"],
                  pl.BlockSpec((GC, LANES), lambda g, c: (0, 0))],
        out_specs=[sp["chan"],
                   pl.BlockSpec((Q, STATE), lambda g, c: (rc(c), g)),
                   pl.BlockSpec((Q, STATE), lambda g, c: (rc(c), g)),
                   sp["head"], acc, acc],
        out_shape=[_sds((L, D), F32), _sds((L, GROUPS * STATE), F32), _sds((L, GROUPS * STATE), F32),
                   _sds((L, GROUPS * LANES), BF16), _sds((8, GROUPS * LANES), F32), _sds((8, GROUPS * LANES), F32)],
        scratch_shapes=[pltpu.VMEM((STATE, GC), F32), pltpu.VMEM((Q, GC), F32)], name="scan_bwd",
        sem=("parallel", "arbitrary"),
        args=(pre, pre, pre, dtx, eax, dsx, cdx, acs, acst, dt, dtraw, bias, alog, states, dy, dexp, collapse),
        jobs=jobs)


def gate_fwd(y, pre, proj, dexp, nw, mixed, D):
    L = y.shape[0]
    GC = D // GROUPS
    T = _pick(L, 256)

    def body(y_ref, xs_ref, z_ref, de_ref, nw_ref, old_ref, o_ref):
        xs, _ = _silu_and_grad(xs_ref[...])
        z = z_ref[...]
        y3 = (y_ref[...] + de_ref[...] * xs) * (z * _sigmoid(z))
        for g in range(GROUPS):
            sl = slice(g * GC, (g + 1) * GC)
            yg = y3[:, sl]
            r = lax.rsqrt(jnp.mean(yg * yg, axis=-1, keepdims=True) + NORM_EPS)
            o_ref[:, sl] = (yg * r * nw_ref[:, sl]).astype(o_ref.dtype)

    row = lambda c: pl.BlockSpec((T, D), lambda i: (i, c))
    vec = pl.BlockSpec((1, D), lambda i: (0, 0))
    return pl.pallas_call(
        body, grid=(L // T,), in_specs=[row(0), row(0), row(2), vec, vec, _ANY], out_specs=row(1),
        out_shape=_sds(mixed.shape, mixed.dtype), input_output_aliases={5: 0}, name="gate_fwd",
        compiler_params=_params("parallel"))(y, pre, proj, dexp, nw, mixed)


def gate_bwd(y, pre, proj, dexp, nw, dmix, D, jobs=()):
    L = y.shape[0]
    GC = D // GROUPS
    T = _pick(L, 256)

    def body(y_ref, xs_ref, z_ref, de_ref, nw_ref, dm_ref, dy_ref, dz_ref, dnw_ref, dde_ref):
        @pl.when(pl.program_id(0) == 0)
        def _():
            dnw_ref[...] = jnp.zeros_like(dnw_ref)
            dde_ref[...] = jnp.zeros_like(dde_ref)

        xs, _ = _silu_and_grad(xs_ref[...])
        sz, dsz = _silu_and_grad(z_ref[...])
        y2 = y_ref[...] + de_ref[...] * xs
        y3 = y2 * sz
        for g in range(GROUPS):
            sl = slice(g * GC, (g + 1) * GC)
            yg, dm = y3[:, sl], dm_ref[:, sl]
            r = lax.rsqrt(jnp.mean(yg * yg, axis=-1, keepdims=True) + NORM_EPS)
            n = yg * r
            gg = dm * nw_ref[:, sl]
            dy3 = r * (gg - n * jnp.mean(gg * n, axis=-1, keepdims=True))
            dnw_ref[:, sl] += jnp.sum(dm * n, axis=0, keepdims=True)
            dy2 = dy3 * sz[:, sl]
            dy_ref[:, sl] = dy2
            dz_ref[:, sl] = (dy3 * y2[:, sl] * dsz[:, sl]).astype(dz_ref.dtype)
            dde_ref[:, sl] += jnp.sum(dy2 * xs[:, sl], axis=0, keepdims=True)

    row = lambda c: pl.BlockSpec((T, D), lambda i: (i, c))
    vec = pl.BlockSpec((1, D), lambda i: (0, 0))
    return _call(
        body, grid=(L // T,), in_specs=[row(0), row(0), row(2), vec, vec, row(1)],
        out_specs=[row(0), row(2), vec, vec],
        out_shape=[_sds((L, D), F32), _sds((L, proj.shape[1]), BF16), _sds((1, D), F32), _sds((1, D), F32)],
        name="gate_bwd", sem=("arbitrary",), args=(y, pre, proj, dexp, nw, dmix), jobs=jobs)


def _adam_math(gv, w, m, v):
    c1 = 1.0 - ADAM_B1 ** ADAM_STEP
    c2 = 1.0 - ADAM_B2 ** ADAM_STEP
    nm = ADAM_B1 * m + (1.0 - ADAM_B1) * gv
    nv = ADAM_B2 * v + (1.0 - ADAM_B2) * (gv * gv)
    return -ADAM_LR * ((nm / c1) / (jnp.sqrt(nv / c2) + ADAM_EPS) + ADAM_WD * w), nm, nv


def adamw(g, w, m, v, name):
    R, C = g.shape
    T = R if R <= 128 else 128
    assert R % T == 0

    def body(g_ref, w_ref, m_ref, v_ref, d_ref, nm_ref, nv_ref):
        d_ref[...], nm_ref[...], nv_ref[...] = _adam_math(g_ref[...], w_ref[...], m_ref[...], v_ref[...])

    blk = pl.BlockSpec((T, C), lambda i: (i, 0))
    return pl.pallas_call(
        body, grid=(R // T,), in_specs=[blk] * 4, out_specs=[blk] * 3,
        out_shape=[_sds((R, C), F32)] * 3, name=name, compiler_params=_params("parallel"))(g, w, m, v)


def adamw_layer(g, w, m, v, layer, prev, name):
    R, C = g.shape
    T = _pick(R, 128)
    assert R % T == 0

    def body(g_ref, w_ref, m_ref, v_ref, *rest):
        go_ref, d_ref, nm_ref, nv_ref = rest[-4:]
        gv = g_ref[...]
        go_ref[0] = gv
        d_ref[0], nm_ref[0], nv_ref[0] = _adam_math(gv, w_ref[0], m_ref[0], v_ref[0])

    mine = pl.BlockSpec((1, T, C), lambda i: (layer, i, 0))
    prev = list(prev or [])
    return pl.pallas_call(
        body, grid=(R // T,), in_specs=[pl.BlockSpec((T, C), lambda i: (i, 0)), mine, mine, mine] + [_ANY] * len(prev),
        out_specs=[mine] * 4, out_shape=[_sds(w.shape, F32)] * 4,
        input_output_aliases={4 + i: i for i in range(len(prev))}, name=name,
        compiler_params=_params("parallel"))(g, w, m, v, *prev)


def cast_bf16(a):
    R, C = a.shape
    T = _pick(R, 256)
    assert R % T == 0

    def body(a_ref, o_ref):
        o_ref[...] = a_ref[...].astype(BF16)

    blk = pl.BlockSpec((T, C), lambda i: (i, 0))
    return pl.pallas_call(body, grid=(R // T,), in_specs=[blk], out_specs=blk, out_shape=_sds((R, C), BF16),
                          name="cast_bf16", compiler_params=_params("parallel"))(a)


def _to_groups(v, hpg):
    lead = v.shape[:-1]
    t = v.reshape(lead + (GROUPS, hpg))
    t = jnp.pad(t, [(0, 0)] * (len(lead) + 1) + [(0, LANES - hpg)])
    return t.reshape(lead + (GROUPS * LANES,))


def _from_groups(a, hpg):
    lead = a.shape[:-1]
    return a.reshape(lead + (GROUPS, LANES))[..., :hpg].reshape(lead + (GROUPS * hpg,))


def _expand_matrix(D):
    gc = D // GROUPS
    return (jnp.arange(LANES)[:, None] == (jnp.arange(gc)[None, :] // HEAD_DIM)).astype(BF16)


def layer_params(pre_w, w_in_full, mixw, scale, cw, cb, bias, alog, dskip, nw, w_out_full, post_w, D):
    hpg = D // GROUPS // HEAD_DIM
    if isinstance(w_in_full, tuple):
        wmain, wdt = w_in_full
    else:
        main = w_in_full.shape[1] - GROUPS * hpg
        wmain, wdt = w_in_full[:, :main], w_in_full[:, main:]
    return dict(
        pre_w=pre_w[None], wmain=wmain, wdt=_to_groups(wdt, hpg), mixw=mixw,
        scale=scale[None], cw=cw, cb=cb[None], bias=_to_groups(bias, hpg)[None], alog=_to_groups(alog, hpg)[None],
        dexp=jnp.repeat(dskip, HEAD_DIM)[None], nw=nw[None], wout=w_out_full, post_w=post_w[None])


def layer_fwd(x, p, D, next_shards=None):
    send_in = [gather_send_job(next_shards[:1])] if next_shards else []
    h = rms_fwd(x, p["pre_w"])
    proj, got = matmul(h, p["wmain"], "nn", F32, "proj", jobs=send_in)
    dtraw, _ = matmul(h, p["wdt"], "nn", F32, "dtproj")
    mixed, pooled = pool_fwd(proj, p["mixw"], p["scale"], D)
    pre = conv_fwd(proj, p["cw"], p["cb"], D)
    dtp = dt_prep(dtraw, p["bias"], p["alog"], _expand_matrix(D), D)
    dt, acs, acst, dtx, eax, dsx, cdx = dtp
    y, states = scan_fwd(pre, dtx, eax, dsx, cdx, acs, acst, D)
    mixed = gate_fwd(y, pre, proj, p["dexp"], p["nw"], mixed, D)
    jobs = [gather_send_job(next_shards[1:]), gather_pass_job(got[0])] if next_shards else []
    out, got = matmul(mixed, p["wout"], "nn", F32, "outproj", jobs=jobs)
    xn, got2 = post_fwd(x, out, p["post_w"], jobs=[gather_pass_job(got[0])] if next_shards else [])
    gathered = got[1] + got2[0] if next_shards else None
    return xn, dict(x=x, h=h, proj=proj, dtraw=dtraw, pooled=pooled, pre=pre, dtp=dtp, y=y, states=states,
                    mixed=mixed, out=out), gathered


def layer_bwd(dxn, p, s, D, where=None):
    reduce = where is not None
    chip, core = where if reduce else (None, None)
    hpg = D // GROUPS // HEAD_DIM
    PGW = D // GROUPS
    main = p["wmain"].shape[1]
    SH = (main + GROUPS * hpg) // 4
    dt, acs, acst, dtx, eax, dsx, cdx = s["dtp"]
    dout, d_post = post_bwd(s["out"], p["post_w"], dxn)
    dmix, _ = matmul(dout, p["wout"], "nt", F32, "dmixed")
    d_wout, _ = matmul(s["mixed"], dout, "tn", F32, "dwout")
    g_out = d_wout.reshape(4, 2, D // 4, D)
    (dy2, dproj, d_nw, d_dexp), got = gate_bwd(s["y"], s["pre"], s["proj"], p["dexp"], p["nw"], dmix, D,
                                               jobs=[pair_exchange_job([g_out])] if reduce else [])
    pair_out = pair_add(g_out, got[0][0], core, BF16) if reduce else None
    (dxs, db, dc, ddtraw, d_alog, d_bias), got = scan_bwd(
        s["pre"], dtx, eax, dsx, cdx, acs, acst, dt, s["dtraw"], p["bias"], p["alog"], s["states"], dy2,
        p["dexp"], _expand_matrix(D).T, D, jobs=[chip_exchange_job([pair_out])] if reduce else [])
    mine_out = chip_add(pair_out, got[0][0], chip, core) if reduce else None
    (dproj, d_cw, d_cb), got = conv_bwd([dxs, db, dc], s["proj"], p["cw"], dproj, D,
                                        jobs=[pair_gather_job([mine_out])] if reduce else [])
    r_out = got[0][0] if reduce else None
    dproj, dmixed, d_scale = pool_bwd(s["proj"], dmix, s["pooled"], p["mixw"], p["scale"], dproj, D)
    d_mixw = pool_dw(s["pooled"], dmixed, D)
    d_wmain, _ = matmul(s["h"], dproj, "tn", F32, "dwmain")
    d_wdt, _ = matmul(s["h"], ddtraw, "tn", F32, "dwdt")
    late = [d_wmain.reshape(1, 2, D // 2, main), d_wdt.reshape(1, 2, D // 2, GROUPS * LANES),
            d_mixw.reshape(GROUPS, 4, PGW // 4, PGW).transpose(1, 0, 2, 3).reshape(4, 2, GROUPS * PGW // 8, PGW)]
    dh_b, got = matmul(ddtraw, p["wdt"], "nt", F32, "dh_dt", jobs=[pair_exchange_job(late)] if reduce else [])
    if reduce:
        p_main, p_dt, p_mix = [pair_add(g, r, core, BF16) for g, r in zip(late, got[0])]
        p_in = jnp.concatenate([p_main[0], _from_groups(p_dt[0], hpg)], axis=1)
        pairs = [p_in.reshape(D // 2, 4, SH).transpose(1, 0, 2), p_mix]
    dh_a, got = matmul(dproj, p["wmain"], "nt", F32, "dh_main", jobs=[chip_exchange_job(pairs)] if reduce else [])
    mines = [chip_add(q, r, chip, core) for q, r in zip(pairs, got[0])] if reduce else None
    (dx, d_pre), got = rms_bwd(s["x"], p["pre_w"], dh_a, dh_b, dxn, jobs=[pair_gather_job(mines)] if reduce else [])
    reduced = dict(w_in=got[0][0], w_out=r_out, pool_mix_w=got[0][1]) if reduce else None
    grads = dict(
        pre_norm_w=d_pre[0], pool_scale=d_scale[0], conv_w=d_cw[:CONV_K], conv_b=d_cb[0],
        dt_bias=_from_groups(d_bias[0], hpg), a_log=_from_groups(d_alog[0], hpg),
        d_skip=d_dexp[0].reshape(-1, HEAD_DIM).sum(axis=-1), ssd_norm_w=d_nw[0], post_norm_w=d_post[0])
    if not reduce:
        grads.update(w_in=jnp.concatenate([d_wmain, _from_groups(d_wdt, hpg)], axis=1), pool_mix_w=d_mixw,
                     w_out=d_wout)
    return dx, grads, reduced


def local_step(x, target, params, D):
    saved = []
    for p in params:
        x, s, _ = layer_fwd(x, p, D)
        saved.append(s)
    dx, sumsq = loss_head(x, target)
    grads = [None] * len(params)
    for l in reversed(range(len(params))):
        dx, grads[l], _ = layer_bwd(dx, params[l], saved[l], D)
    return sumsq, dx, grads


SMALL = ("pre_norm_w", "pool_scale", "conv_w", "conv_b", "dt_bias", "a_log", "d_skip", "ssd_norm_w", "post_norm_w")
BIG = ("w_in", "w_out", "pool_mix_w")


def _pack(parts):
    flat = jnp.concatenate([p.reshape(-1) for p in parts])
    n = flat.shape[0]
    rows = -(-n // (LANES * LANES)) * LANES
    return jnp.pad(flat, (0, rows * LANES - n)).reshape(rows, LANES)


def _unpack(packed, shapes):
    flat, out, at = packed.reshape(-1), [], 0
    for s in shapes:
        n = math.prod(s)
        out.append(flat[at:at + n].reshape(s))
        at += n
    return out


def kernel(x, pre_norm_w, w_in, pool_mix_w, pool_scale, conv_w, conv_b, dt_bias, a_log, d_skip, ssd_norm_w, w_out, post_norm_w, loss_target, m_pre_norm_w, m_w_in, m_pool_mix_w, m_pool_scale, m_conv_w, m_conv_b, m_dt_bias, m_a_log, m_d_skip, m_ssd_norm_w, m_w_out, m_post_norm_w, v_pre_norm_w, v_w_in, v_pool_mix_w, v_pool_scale, v_conv_w, v_conv_b, v_dt_bias, v_a_log, v_d_skip, v_ssd_norm_w, v_w_out, v_post_norm_w):
    NL, D, SH = w_in.shape
    PGW = D // GROUPS
    CS = conv_w.shape[2]
    chip = (2 * lax.axis_index("x") + lax.axis_index("y")).astype(jnp.int32)
    chip1, core = chip.reshape(1), lax.axis_index("c").astype(jnp.int32).reshape(1)

    def shards(l):
        return [cast_bf16(w_in[l]).reshape(2, D // 2, SH), cast_bf16(w_out[l]).reshape(2, D // 4, D),
                cast_bf16(pool_mix_w[l].reshape(GROUPS * PGW // 4, PGW)).reshape(2, GROUPS * PGW // 8, PGW),
                conv_w[l].reshape(2, CONV_K * CS // (2 * LANES), LANES)]

    def params(l, g):
        g_in, g_out, g_mix, g_cw = g
        g_in = g_in.reshape(4, D, SH)
        heads = D // HEAD_DIM
        w_in_l = (jnp.concatenate([g_in[0], g_in[1], g_in[2], g_in[3, :, :SH - heads]], axis=1), g_in[3, :, SH - heads:])
        mix_l = g_mix.reshape(4, GROUPS, PGW // 4, PGW).transpose(1, 0, 2, 3).reshape(GROUPS, PGW, PGW)
        cw_l = g_cw.reshape(4, CONV_K, CS).transpose(1, 0, 2).reshape(CONV_K, 4 * CS)
        return layer_params(pre_norm_w[l], w_in_l, mix_l, pool_scale[l], cw_l, conv_b[l], dt_bias[l], a_log[l],
                            d_skip[l], ssd_norm_w[l], g_out.reshape(2 * D, D), post_norm_w[l], D)

    gathered = run_jobs([gather_pass_job(run_jobs([gather_send_job(shards(0))], "gather_send")[0])], "gather_pass")[0]
    h, ps, saved = x[0], [], []
    for l in range(NL):
        ps.append(params(l, gathered))
        h, s, gathered = layer_fwd(h, ps[l], D, shards(l + 1) if l + 1 < NL else None)
        saved.append(s)
    dx, sumsq = loss_head(h, loss_target[0])

    given = dict(w_in=(w_in, m_w_in, v_w_in), w_out=(w_out, m_w_out, v_w_out),
                 pool_mix_w=(pool_mix_w, m_pool_mix_w, v_pool_mix_w))
    flat = {n: [t.reshape(NL, -1, t.shape[-1]) for t in given[n]] for n in BIG}
    done = {n: None for n in BIG}
    grads = [None] * NL
    for l in reversed(range(NL)):
        dx, grads[l], reduced = layer_bwd(dx, ps[l], saved[l], D, (chip1, core))
        for n in BIG:
            r = reduced[n]
            done[n] = adamw_layer(r.reshape(-1, r.shape[-1]), *flat[n], l, done[n], "adamw_" + n)

    small_shapes = [(NL,) + grads[0][n].shape for n in SMALL]
    packed = _pack([0.5 / D * sumsq[0, :1]] + [jnp.stack([g[n] for g in grads]) for n in SMALL])
    total = allreduce_small(packed)
    loss, *small = _unpack(total, [(1,)] + small_shapes)
    small = dict(zip(SMALL, small))
    small["conv_w"] = lax.dynamic_slice_in_dim(small["conv_w"], chip * CS, CS, axis=2)

    given_small = dict(
        pre_norm_w=(pre_norm_w, m_pre_norm_w, v_pre_norm_w), pool_scale=(pool_scale, m_pool_scale, v_pool_scale),
        conv_w=(conv_w, m_conv_w, v_conv_w), conv_b=(conv_b, m_conv_b, v_conv_b),
        dt_bias=(dt_bias, m_dt_bias, v_dt_bias), a_log=(a_log, m_a_log, v_a_log),
        d_skip=(d_skip, m_d_skip, v_d_skip), ssd_norm_w=(ssd_norm_w, m_ssd_norm_w, v_ssd_norm_w),
        post_norm_w=(post_norm_w, m_post_norm_w, v_post_norm_w))
    shapes = [given_small[n][0].shape for n in SMALL]
    upd = adamw(_pack([small[n] for n in SMALL]), *[_pack([given_small[n][i] for n in SMALL]) for i in range(3)],
                "adamw_small")
    upd = [dict(zip(SMALL, _unpack(u, shapes))) for u in upd]

    out = {n: (small[n], upd[0][n], upd[1][n], upd[2][n]) for n in SMALL}
    for n in BIG:
        out[n] = tuple(t.reshape(given[n][0].shape) for t in done[n])

    order = ("pre_norm_w", "w_in", "pool_mix_w", "pool_scale", "conv_w", "conv_b", "dt_bias", "a_log", "d_skip",
             "ssd_norm_w", "w_out", "post_norm_w")
    return (loss.reshape(()), dx[None], *[out[n][0] for n in order], *[out[n][1] for n in order],
            *[out[n][2] for n in order], *[out[n][3] for n in order])
```

```python
import functools
import math

import jax
import jax.numpy as jnp
from jax import lax
from jax.experimental import pallas as pl
from jax.experimental.pallas import tpu as pltpu

F32 = jnp.float32
BF16 = jnp.bfloat16

NORM_EPS = 1e-6
HEAD_DIM = 64
STATE = 128
GROUPS = 4
POOL_WINDOWS = (2, 4, 8, 16)
POOL_HALO = 16
CONV_K = 4
CONV_HALO = 8
SCAN_CHUNK = 256
LANES = 128
VMEM_LIMIT = 52 * 1024 * 1024

ADAM_LR = 0.001
ADAM_B1 = 0.9
ADAM_B2 = 0.999
ADAM_EPS = 1e-08
ADAM_WD = 0.01
ADAM_STEP = 10

MESH = pl.DeviceIdType.MESH

NN = (((1,), (0,)), ((), ()))
NT = (((1,), (1,)), ((), ()))
TN = (((0,), (0,)), ((), ()))

_ANY = pl.BlockSpec(memory_space=pl.ANY)


def _params(*sem):
    return pltpu.CompilerParams(dimension_semantics=sem, vmem_limit_bytes=VMEM_LIMIT)


def _pick(dim, pref):
    if dim <= pref:
        return dim
    t = (pref // LANES) * LANES
    while t > LANES and dim % t:
        t -= LANES
    assert dim % t == 0, (dim, pref)
    return t


def _rows_tile(rows, pref):
    t = (min(pref, rows) // 8) * 8
    while t >= 8 and rows % t:
        t -= 8
    return t if t >= 8 else rows


def _dot(a, b, dn=NN):
    return lax.dot_general(a, b, dn, preferred_element_type=F32)


def _split3(a):
    hi = a.astype(BF16)
    r = a - hi.astype(F32)
    mid = r.astype(BF16)
    return hi, mid, (r - mid.astype(F32)).astype(BF16)


def _dot_sel(a, e):
    hi, mid, lo = _split3(a)
    return (_dot(lo, e) + _dot(mid, e)) + _dot(hi, e)


def _sel_dot(e, b):
    hi, mid, lo = _split3(b)
    return (_dot(e, lo) + _dot(e, mid)) + _dot(e, hi)


def _sigmoid(v):
    return 1.0 / (1.0 + jnp.exp(-v))


def _silu_and_grad(v):
    s = _sigmoid(v)
    return v * s, s * (1.0 + v * (1.0 - s))


def _row(i, shape):
    return lax.broadcasted_iota(jnp.int32, shape, 0) + i


def _sds(shape, dtype):
    return jax.ShapeDtypeStruct(tuple(shape), dtype)


class Job:
    def __init__(self, ins, outs, aliased, nsem, start, finish):
        self.ins, self.outs, self.aliased, self.nsem, self.start, self.finish = ins, outs, aliased, nsem, start, finish


def _place():
    x, y, c = lax.axis_index("x"), lax.axis_index("y"), lax.axis_index("c")
    return x, y, c, [(1 - x, y), (x, 1 - y), (1 - x, 1 - y)]


def _remote(src, dst, send_sem, recv_sem, device):
    return pltpu.make_async_remote_copy(src_ref=src, dst_ref=dst, send_sem=send_sem, recv_sem=recv_sem,
                                        device_id=device, device_id_type=MESH)


def _half(ref, c, cols, lead=0):
    idx = [slice(None)] * lead
    if cols:
        w = ref.shape[-1] // 2
        idx += [slice(None)] * (len(ref.shape) - lead - 1) + [pl.ds(pl.multiple_of(c * w, LANES), w)]
    else:
        idx += [c]
    return ref.at[tuple(idx)]


def _flags(cols, n):
    return list(cols) if cols else [False] * n


def gather_send_job(arrs, cols=None):
    n = len(arrs)
    cols = _flags(cols, n)

    def copies(ins, outs, send, recv):
        x, y, c, chips = _place()
        mine = 2 * x + y
        out = []
        for a in range(n):
            out.append(_remote(ins[a], outs[a].at[mine], send.at[4 * a + 3], recv.at[4 * a + 3], (x, y, 1 - c)))
            for j, chip in enumerate(chips):
                out.append(_remote(_half(ins[a], c, cols[a]), _half(outs[a].at[mine], c, cols[a]),
                                   send.at[4 * a + j], recv.at[4 * a + j], (*chip, c)))
        return out

    def start(ins, outs, send, recv):
        for cp in copies(ins, outs, send, recv):
            cp.start()

    def finish(ins, outs, send, recv):
        x, y, c, chips = _place()
        for a in range(n):
            for j, chip in enumerate(chips):
                landed = _half(outs[a].at[2 * chip[0] + chip[1]], c, cols[a])
                _remote(landed, landed, send.at[4 * a + j], recv.at[4 * a + j], (x, y, 1 - c)).wait_recv()
            twin = outs[a].at[2 * x + y]
            _remote(twin, twin, send.at[4 * a + 3], recv.at[4 * a + 3], (x, y, 1 - c)).wait_recv()
        for cp in copies(ins, outs, send, recv):
            cp.wait_send()

    return Job(list(arrs), [_sds((4,) + a.shape, a.dtype) for a in arrs], False, 4 * n, start, finish)


def gather_pass_job(bufs, cols=None):
    n = len(bufs)
    cols = _flags(cols, n)

    def copies(outs, send, recv):
        x, y, c, chips = _place()
        out = []
        for a in range(n):
            for j, chip in enumerate(chips):
                landed = _half(outs[a].at[2 * chip[0] + chip[1]], c, cols[a])
                out.append(_remote(landed, landed, send.at[3 * a + j], recv.at[3 * a + j], (x, y, 1 - c)))
        return out

    def start(ins, outs, send, recv):
        for cp in copies(outs, send, recv):
            cp.start()

    def finish(ins, outs, send, recv):
        x, y, c, chips = _place()
        for a in range(n):
            for j, chip in enumerate(chips):
                passed = _half(outs[a].at[2 * chip[0] + chip[1]], 1 - c, cols[a])
                _remote(passed, passed, send.at[3 * a + j], recv.at[3 * a + j], (x, y, 1 - c)).wait_recv()
        for cp in copies(outs, send, recv):
            cp.wait_send()

    return Job(list(bufs), [_sds(b.shape, b.dtype) for b in bufs], True, 3 * n, start, finish)


def pair_exchange_job(arrs, cols=None):
    n = len(arrs)
    cols = _flags(cols, n)

    def copies(ins, outs, send, recv):
        x, y, c, _ = _place()
        return [_remote(_half(ins[a], 1 - c, cols[a], 1), outs[a], send.at[a], recv.at[a], (x, y, 1 - c))
                for a in range(n)]

    def start(ins, outs, send, recv):
        for cp in copies(ins, outs, send, recv):
            cp.start()

    def finish(ins, outs, send, recv):
        for cp in copies(ins, outs, send, recv):
            cp.wait()

    shape = lambda a, k: a.shape[:-1] + (a.shape[-1] // 2,) if k else a.shape[:1] + a.shape[2:]
    return Job(list(arrs), [_sds(shape(a, k), a.dtype) for a, k in zip(arrs, cols)], False, n, start, finish)


def chip_exchange_job(arrs):
    n = len(arrs)

    def copies(ins, outs, send, recv):
        x, y, c, chips = _place()
        return [_remote(ins[a].at[2 * chip[0] + chip[1]], outs[a].at[j], send.at[3 * a + j], recv.at[3 * a + j],
                        (*chip, c)) for a in range(n) for j, chip in enumerate(chips)]

    def start(ins, outs, send, recv):
        for cp in copies(ins, outs, send, recv):
            cp.start()

    def finish(ins, outs, send, recv):
        for cp in copies(ins, outs, send, recv):
            cp.wait()

    return Job(list(arrs), [_sds((3,) + a.shape[1:], a.dtype) for a in arrs], False, 3 * n, start, finish)


def pair_gather_job(bufs, cols=None):
    n = len(bufs)
    cols = _flags(cols, n)

    def copies(outs, send, recv):
        x, y, c, _ = _place()
        return [_remote(_half(outs[a], c, cols[a]), _half(outs[a], c, cols[a]), send.at[a], recv.at[a],
                        (x, y, 1 - c)) for a in range(n)]

    def start(ins, outs, send, recv):
        for cp in copies(outs, send, recv):
            cp.start()

    def finish(ins, outs, send, recv):
        for cp in copies(outs, send, recv):
            cp.wait()

    return Job(list(bufs), [_sds(b.shape, b.dtype) for b in bufs], True, n, start, finish)


def _call(body, *, grid, in_specs, out_specs, out_shape, name, sem, args, scratch_shapes=(), jobs=(), aliases=None):
    in_specs, out_specs, out_shape, scratch_shapes = list(in_specs), list(out_specs), list(out_shape), list(scratch_shapes)
    aliases = dict(aliases or {})
    n_in, n_out, n_scr = len(in_specs), len(out_specs), len(scratch_shapes)
    if jobs:
        sem = ("arbitrary",) * len(grid)
    at_in, at_out = n_in, n_out
    for j in jobs:
        if j.aliased:
            aliases.update({at_in + i: at_out + i for i in range(len(j.ins))})
        at_in, at_out = at_in + len(j.ins), at_out + len(j.outs)

    def wrapped(*refs):
        ins, p = refs[:n_in], n_in
        jins = []
        for j in jobs:
            jins.append(refs[p:p + len(j.ins)])
            p += len(j.ins)
        outs, p = refs[p:p + n_out], p + n_out
        jouts = []
        for j in jobs:
            jouts.append(refs[p:p + len(j.outs)])
            p += len(j.outs)
        scr, sems = refs[p:p + n_scr], refs[p + n_scr:]

        def start():
            for k, j in enumerate(jobs):
                j.start(jins[k], jouts[k], sems[2 * k], sems[2 * k + 1])

        def finish():
            for k, j in enumerate(jobs):
                j.finish(jins[k], jouts[k], sems[2 * k], sems[2 * k + 1])

        if jobs and grid:
            ids = [pl.program_id(d) for d in range(len(grid))]
            pl.when(functools.reduce(jnp.logical_and, [i == 0 for i in ids]))(start)
            body(*ins, *outs, *scr)
            pl.when(functools.reduce(jnp.logical_and, [i == g - 1 for i, g in zip(ids, grid)]))(finish)
        else:
            start()
            body(*ins, *outs, *scr)
            finish()

    kwargs = dict(grid=grid) if grid else {}
    res = pl.pallas_call(
        wrapped, in_specs=in_specs + [_ANY] * (at_in - n_in), out_specs=out_specs + [_ANY] * (at_out - n_out),
        out_shape=out_shape + [o for j in jobs for o in j.outs],
        scratch_shapes=scratch_shapes + [pltpu.SemaphoreType.DMA((j.nsem,)) for j in jobs for _ in range(2)],
        input_output_aliases=aliases, name=name,
        compiler_params=pltpu.CompilerParams(dimension_semantics=sem, vmem_limit_bytes=VMEM_LIMIT) if grid
        else pltpu.CompilerParams(vmem_limit_bytes=VMEM_LIMIT), **kwargs)(*args, *[a for j in jobs for a in j.ins])
    res = list(res)
    outs, rest, per_job = res[:n_out], res[n_out:], []
    for j in jobs:
        per_job.append(rest[:len(j.outs)])
        rest = rest[len(j.outs):]
    return outs, per_job


def run_jobs(jobs, name):
    return _call(lambda: None, grid=(), in_specs=[], out_specs=[], out_shape=[], name=name, sem=(), args=(), jobs=jobs)[1]


def pair_add(g, r, core, out_dtype, cols=False):
    S, R, C = r.shape
    T = _rows_tile(R, 256 if C <= 4096 else 128)

    def body(c_ref, g_ref, r_ref, o_ref):
        o_ref[0] = ((g_ref[0] if cols else g_ref[0, 0]) + r_ref[0]).astype(o_ref.dtype)

    g_spec = (pl.BlockSpec((1, T, C), lambda k, i, c_ref: (k, i, c_ref[0])) if cols
              else pl.BlockSpec((1, 1, T, C), lambda k, i, c_ref: (k, c_ref[0], i, 0)))
    return pl.pallas_call(
        body,
        grid_spec=pltpu.PrefetchScalarGridSpec(
            num_scalar_prefetch=1, grid=(S, R // T),
            in_specs=[g_spec, pl.BlockSpec((1, T, C), lambda k, i, c_ref: (k, i, 0))],
            out_specs=pl.BlockSpec((1, T, C), lambda k, i, c_ref: (k, i, 0))),
        out_shape=_sds((S, R, C), out_dtype), name="pair_add",
        compiler_params=_params("parallel", "parallel"))(core, g, r)


def chip_add(p, r, chip, core, cols=False):
    _, R, C = p.shape
    T = _rows_tile(R, 256)

    def body(k_ref, c_ref, p_ref, r0_ref, r1_ref, r2_ref, o_ref):
        s = ((p_ref[0].astype(F32) + r0_ref[0].astype(F32)) + r1_ref[0].astype(F32)) + r2_ref[0].astype(F32)
        if cols:
            o_ref[...] = s
        else:
            o_ref[0] = s

    slot = lambda j: pl.BlockSpec((1, T, C), lambda i, k_ref, c_ref: (j, i, 0))
    out_spec = (pl.BlockSpec((T, C), lambda i, k_ref, c_ref: (i, c_ref[0])) if cols
                else pl.BlockSpec((1, T, C), lambda i, k_ref, c_ref: (c_ref[0], i, 0)))
    return pl.pallas_call(
        body,
        grid_spec=pltpu.PrefetchScalarGridSpec(
            num_scalar_prefetch=2, grid=(R // T,),
            in_specs=[pl.BlockSpec((1, T, C), lambda i, k_ref, c_ref: (k_ref[0], i, 0)), slot(0), slot(1), slot(2)],
            out_specs=out_spec),
        out_shape=_sds((R, 2 * C) if cols else (2, R, C), F32), name="chip_add",
        compiler_params=_params("parallel"))(chip, core, p, r, r, r)


def allreduce_small(v):
    R = v.shape[0]

    def body(v_ref, o_ref, buf, send_sems, recv_sems, local_sem):
        x, y, c, chips = _place()
        me, sibling = (x, y, c), (x, y, 1 - c)

        def rows(px, py, pc):
            return buf.at[pl.ds((4 * px + 2 * py + pc) * R, R), :]

        def copy(k, block, to, src=None):
            return _remote(rows(*block) if src is None else src, rows(*block), send_sems.at[k], recv_sems.at[k], to)

        mine = pltpu.make_async_copy(v_ref, rows(*me), local_sem)
        mine.start()
        first = [copy(0, me, sibling, src=v_ref)]
        first += [copy(1 + j, me, (*chip, c), src=v_ref) for j, chip in enumerate(chips)]
        for cp in first:
            cp.start()
        passed = [copy(4 + j, (*chip, c), sibling) for j, chip in enumerate(chips)]
        for j, chip in enumerate(chips):
            copy(1 + j, (*chip, c), me).wait_recv()
            passed[j].start()
        copy(0, sibling, me).wait_recv()
        for j, chip in enumerate(chips):
            copy(4 + j, (*chip, 1 - c), me).wait_recv()
        for cp in first + passed:
            cp.wait_send()
        mine.wait()
        acc = buf[0:R, :]
        for d in range(1, 8):
            acc = acc + buf[d * R:(d + 1) * R, :]
        o_ref[...] = acc

    return pl.pallas_call(
        body, in_specs=[pl.BlockSpec(memory_space=pltpu.VMEM)], out_specs=pl.BlockSpec(memory_space=pltpu.VMEM),
        out_shape=_sds((R, LANES), F32),
        scratch_shapes=[pltpu.VMEM((8 * R, LANES), F32), pltpu.SemaphoreType.DMA((7,)),
                        pltpu.SemaphoreType.DMA((7,)), pltpu.SemaphoreType.DMA],
        name="allreduce_small", compiler_params=pltpu.CompilerParams(vmem_limit_bytes=VMEM_LIMIT))(v)


def matmul(a, b, mode, out_dtype, name, tm=512, tn=1024, tk=4608, jobs=(), n_out=None):
    if mode == "nn":
        (M, K), (K2, N) = a.shape, b.shape
    elif mode == "nt":
        (M, K), (N, K2) = a.shape, b.shape
        N = n_out or N
    else:
        (K, M), (K2, N) = a.shape, b.shape
    assert K == K2 or (mode == "nn" and K2 > K)
    tm, tn, tk = _pick(M, tm), _pick(N, tn), _pick(K, tk)
    nk = K // tk
    dn = {"nn": NN, "nt": NT, "tn": TN}[mode]

    def body(a_ref, b_ref, o_ref, *acc):
        part = _dot(a_ref[...].astype(BF16), b_ref[...].astype(BF16), dn)
        if nk == 1:
            o_ref[...] = part.astype(o_ref.dtype)
            return
        acc_ref, = acc
        k = pl.program_id(2)

        @pl.when(k == 0)
        def _():
            acc_ref[...] = part

        @pl.when(jnp.logical_and(k > 0, k < nk - 1))
        def _():
            acc_ref[...] += part

        @pl.when(k == nk - 1)
        def _():
            o_ref[...] = (acc_ref[...] + part).astype(o_ref.dtype)

    a_spec = (pl.BlockSpec((tk, tm), lambda i, j, k: (k, i)) if mode == "tn"
              else pl.BlockSpec((tm, tk), lambda i, j, k: (i, k)))
    b_spec = (pl.BlockSpec((tn, tk), lambda i, j, k: (j, k)) if mode == "nt"
              else pl.BlockSpec((tk, tn), lambda i, j, k: (k, j)))
    outs, per_job = _call(
        body, grid=(M // tm, N // tn, nk), in_specs=[a_spec, b_spec],
        out_specs=[pl.BlockSpec((tm, tn), lambda i, j, k: (i, j))], out_shape=[_sds((M, N), out_dtype)],
        scratch_shapes=[pltpu.VMEM((tm, tn), F32)] if nk > 1 else [], name=name,
        sem=("parallel", "parallel", "arbitrary"), args=(a, b), jobs=jobs)
    return outs[0], per_job


def rms_fwd(x, w):
    L, D = x.shape
    T = _pick(L, 256)

    def body(x_ref, w_ref, h_ref):
        xv = x_ref[...]
        r = lax.rsqrt(jnp.mean(xv * xv, axis=-1, keepdims=True) + NORM_EPS)
        h_ref[...] = (xv * r * w_ref[...]).astype(h_ref.dtype)

    return pl.pallas_call(
        body, grid=(L // T,),
        in_specs=[pl.BlockSpec((T, D), lambda i: (i, 0)), pl.BlockSpec((1, D), lambda i: (0, 0))],
        out_specs=pl.BlockSpec((T, D), lambda i: (i, 0)),
        out_shape=_sds((L, D), BF16), name="rms_fwd", compiler_params=_params("parallel"))(x, w)


def post_fwd(x, o, w, jobs=()):
    L, D = x.shape
    T = _pick(L, 256)

    def body(x_ref, o_ref, w_ref, y_ref):
        ov = o_ref[...]
        r = lax.rsqrt(jnp.mean(ov * ov, axis=-1, keepdims=True) + NORM_EPS)
        y_ref[...] = x_ref[...] + ov * r * w_ref[...]

    row = pl.BlockSpec((T, D), lambda i: (i, 0))
    outs, per_job = _call(
        body, grid=(L // T,), in_specs=[row, row, pl.BlockSpec((1, D), lambda i: (0, 0))], out_specs=[row],
        out_shape=[_sds((L, D), F32)], name="post_fwd", sem=("parallel",), args=(x, o, w), jobs=jobs)
    return outs[0], per_job


def _rms_bwd_math(xv, w, dy):
    r = lax.rsqrt(jnp.mean(xv * xv, axis=-1, keepdims=True) + NORM_EPS)
    xhat = xv * r
    g = dy * w
    dx = r * (g - xhat * jnp.mean(g * xhat, axis=-1, keepdims=True))
    return dx, jnp.sum(dy * xhat, axis=0, keepdims=True)


def post_bwd(o, w, dxn):
    L, D = o.shape
    T = _pick(L, 256)

    def body(o_ref, w_ref, d_ref, do_ref, dw_ref):
        dx, dw = _rms_bwd_math(o_ref[...], w_ref[...], d_ref[...])
        do_ref[...] = dx.astype(do_ref.dtype)

        @pl.when(pl.program_id(0) == 0)
        def _():
            dw_ref[...] = jnp.zeros_like(dw_ref)

        dw_ref[...] += dw

    row = pl.BlockSpec((T, D), lambda i: (i, 0))
    vec = pl.BlockSpec((1, D), lambda i: (0, 0))
    return pl.pallas_call(
        body, grid=(L // T,), in_specs=[row, vec, row], out_specs=[row, vec],
        out_shape=[_sds((L, D), BF16), _sds((1, D), F32)],
        name="post_bwd", compiler_params=_params("arbitrary"))(o, w, dxn)


def rms_bwd(x, w, dh_a, dh_b, dxn, jobs=()):
    L, D = x.shape
    T = _pick(L, 256)

    def body(x_ref, w_ref, a_ref, b_ref, d_ref, dx_ref, dw_ref):
        dx, dw = _rms_bwd_math(x_ref[...], w_ref[...], a_ref[...] + b_ref[...])
        dx_ref[...] = d_ref[...] + dx

        @pl.when(pl.program_id(0) == 0)
        def _():
            dw_ref[...] = jnp.zeros_like(dw_ref)

        dw_ref[...] += dw

    row = pl.BlockSpec((T, D), lambda i: (i, 0))
    vec = pl.BlockSpec((1, D), lambda i: (0, 0))
    return _call(
        body, grid=(L // T,), in_specs=[row, vec, row, row, row], out_specs=[row, vec],
        out_shape=[_sds((L, D), F32), _sds((1, D), F32)],
        name="rms_bwd", sem=("arbitrary",), args=(x, w, dh_a, dh_b, dxn), jobs=jobs)


def loss_head(y, target):
    L, D = y.shape
    T = _pick(L, 256)

    def body(y_ref, t_ref, d_ref, s_ref):
        e = y_ref[...] - t_ref[...]
        d_ref[...] = e * (1.0 / D)

        @pl.when(pl.program_id(0) == 0)
        def _():
            s_ref[...] = jnp.zeros_like(s_ref)

        s_ref[...] += jnp.sum(e * e)

    row = pl.BlockSpec((T, D), lambda i: (i, 0))
    return pl.pallas_call(
        body, grid=(L // T,), in_specs=[row, row],
        out_specs=[row, pl.BlockSpec((8, LANES), lambda i: (0, 0))],
        out_shape=[_sds((L, D), F32), _sds((8, LANES), F32)],
        name="loss_head", compiler_params=_params("arbitrary"))(y, target)


def _window_sums(xe, w, back):
    n = xe.shape[0]
    s, k = xe, 1
    while k < w:
        s = s + pltpu.roll(s, k if back else n - k, 0)
        k *= 2
    return s


def pool_fwd(proj, mixw, scale, D):
    L = proj.shape[0]
    PGW = D // GROUPS
    T = _pick(L, 256)
    hb = T // POOL_HALO

    def body(u_ref, halo_ref, g_ref, mw_ref, sc_ref, y_ref, p_ref):
        i = pl.program_id(0)
        u = u_ref[...]
        halo = jnp.where(i > 0, halo_ref[...], 0.0)
        xe = jnp.concatenate([halo, u], axis=0)
        t1 = _row(i * T + 1, (T, 1))
        for g, w in enumerate(POOL_WINDOWS):
            sl = slice(g * PGW, (g + 1) * PGW)
            win = _window_sums(xe[:, sl], w, True)[POOL_HALO:, :]
            cnt = jnp.minimum(t1, w).astype(F32)
            pooled = (win / cnt - u[:, sl]).astype(BF16)
            p_ref[:, sl] = pooled
            mixed = _dot(pooled, mw_ref[g])
            gate = g_ref[:, sl]
            y_ref[:, sl] = (mixed * sc_ref[:, sl] * (gate * _sigmoid(gate))).astype(BF16)

    return pl.pallas_call(
        body, grid=(L // T,),
        in_specs=[pl.BlockSpec((T, D), lambda i: (i, 0)),
                  pl.BlockSpec((POOL_HALO, D), lambda i: (jnp.maximum(i * hb - 1, 0), 0)),
                  pl.BlockSpec((T, D), lambda i: (i, 1)),
                  pl.BlockSpec((GROUPS, PGW, PGW), lambda i: (0, 0, 0)),
                  pl.BlockSpec((1, D), lambda i: (0, 0))],
        out_specs=[pl.BlockSpec((T, D), lambda i: (i, 0)), pl.BlockSpec((T, D), lambda i: (i, 0))],
        out_shape=[_sds((L, 2 * D), BF16), _sds((L, D), BF16)],
        name="pool_fwd", compiler_params=_params("parallel"))(proj, proj, proj, mixw, scale)


def pool_bwd(proj, dmix, pooled, mixw, scale, dproj, D):
    L = proj.shape[0]
    PGW = D // GROUPS
    T = _pick(L, 256)
    hb = T // POOL_HALO
    nT = L // T

    def body(g_ref, gh_ref, dy_ref, dyh_ref, p_ref, mw_ref, sc_ref, old_ref, dp_ref, dm_ref, ds_ref):
        i = pl.program_id(0)
        t1 = _row(i * T + 1, (T, 1))
        th1 = _row((i + 1) * T + 1, (POOL_HALO, 1))
        live = i < nT - 1

        @pl.when(i == 0)
        def _():
            ds_ref[...] = jnp.zeros_like(ds_ref)

        for g, w in enumerate(POOL_WINDOWS):
            sl = slice(g * PGW, (g + 1) * PGW)
            sc = sc_ref[:, sl]
            gate, dy = g_ref[:, sl], dy_ref[:, sl]
            sg, dsg = _silu_and_grad(gate)
            mixed = _dot(p_ref[:, sl], mw_ref[g])
            dmixed = (dy * sc * sg).astype(BF16)
            dm_ref[:, sl] = dmixed
            dp_ref[:, D + g * PGW:D + (g + 1) * PGW] = (dy * mixed * sc * dsg).astype(BF16)
            ds_ref[:, sl] += jnp.sum(dy * mixed * sg, axis=0, keepdims=True)
            dpool = _dot(dmixed, mw_ref[g], NT)
            gate_h = gh_ref[:, sl]
            dmixed_h = (dyh_ref[:, sl] * sc * (gate_h * _sigmoid(gate_h))).astype(BF16)
            dpool_h = jnp.where(live, _dot(dmixed_h, mw_ref[g], NT), 0.0)
            q = dpool / jnp.minimum(t1, w).astype(F32)
            q_h = dpool_h / jnp.minimum(th1, w).astype(F32)
            qe = jnp.concatenate([q, q_h], axis=0)
            dp_ref[:, sl] = (_window_sums(qe, w, False)[:T, :] - dpool).astype(BF16)

    nxt = lambda i: jnp.minimum((i + 1) * hb, L // POOL_HALO - 1)
    row = lambda c: pl.BlockSpec((T, D), lambda i: (i, c))
    return pl.pallas_call(
        body, grid=(nT,),
        in_specs=[row(1), pl.BlockSpec((POOL_HALO, D), lambda i: (nxt(i), 1)),
                  row(0), pl.BlockSpec((POOL_HALO, D), lambda i: (nxt(i), 0)),
                  row(0), pl.BlockSpec((GROUPS, PGW, PGW), lambda i: (0, 0, 0)),
                  pl.BlockSpec((1, D), lambda i: (0, 0)), _ANY],
        out_specs=[pl.BlockSpec((T, 2 * D), lambda i: (i, 0)), row(0), pl.BlockSpec((1, D), lambda i: (0, 0))],
        out_shape=[_sds(dproj.shape, dproj.dtype), _sds((L, D), BF16), _sds((1, D), F32)],
        input_output_aliases={7: 0},
        name="pool_bwd", compiler_params=_params("arbitrary"))(proj, proj, dmix, dmix, pooled, mixw, scale, dproj)


def pool_dw(pooled, dmixed, D):
    L = pooled.shape[0]
    PGW = D // GROUPS
    tk = _pick(L, 1024)
    nk = L // tk

    def body(p_ref, d_ref, o_ref):
        @pl.when(pl.program_id(1) == 0)
        def _():
            o_ref[...] = jnp.zeros_like(o_ref)

        o_ref[0] += _dot(p_ref[...], d_ref[...], TN)

    blk = pl.BlockSpec((tk, PGW), lambda g, k: (k, g))
    return pl.pallas_call(
        body, grid=(GROUPS, nk), in_specs=[blk, blk],
        out_specs=pl.BlockSpec((1, PGW, PGW), lambda g, k: (g, 0, 0)),
        out_shape=_sds((GROUPS, PGW, PGW), F32), name="pool_dw",
        compiler_params=_params("parallel", "arbitrary"))(pooled, dmixed)


def conv_fwd(proj, cw, cb, D):
    L = proj.shape[0]
    C = cw.shape[1]
    assert (3 * D) % C == 0
    cblk = (3 * D) // C
    T = _pick(L, 256)
    hb = T // CONV_HALO

    def body(u_ref, halo_ref, w_ref, b_ref, o_ref):
        i = pl.program_id(0)
        u = u_ref[...]
        xe = jnp.concatenate([jnp.where(i > 0, halo_ref[...], 0.0), u], axis=0)
        acc = b_ref[...] + w_ref[CONV_K - 1:CONV_K, :] * u
        for k in range(CONV_K - 1):
            acc = acc + w_ref[k:k + 1, :] * pltpu.roll(xe, CONV_K - 1 - k, 0)[CONV_HALO:, :]
        o_ref[...] = acc

    return pl.pallas_call(
        body, grid=(L // T,),
        in_specs=[pl.BlockSpec((T, C), lambda i: (i, cblk)),
                  pl.BlockSpec((CONV_HALO, C), lambda i: (jnp.maximum(i * hb - 1, 0), cblk)),
                  pl.BlockSpec((CONV_K, C), lambda i: (0, 0)),
                  pl.BlockSpec((1, C), lambda i: (0, 0))],
        out_specs=pl.BlockSpec((T, C), lambda i: (i, 0)),
        out_shape=_sds((L, C), F32), name="conv_fwd", compiler_params=_params("parallel"))(proj, proj, cw, cb)


def conv_bwd(dparts, proj, cw, dproj, D, jobs=()):
    L = proj.shape[0]
    C = cw.shape[1]
    cblk = (3 * D) // C
    T = _pick(L, 256)
    hb = T // CONV_HALO
    nT = L // T
    widths = [p.shape[1] for p in dparts]
    assert sum(widths) == C
    n = len(dparts)

    def body(*refs):
        d_refs, dn_refs = refs[:n], refs[n:2 * n]
        u_ref, up_ref, w_ref, old_ref, dr_ref, dw_ref, db_ref = refs[2 * n:]
        i = pl.program_id(0)

        @pl.when(i == 0)
        def _():
            dw_ref[...] = jnp.zeros_like(dw_ref)
            db_ref[...] = jnp.zeros_like(db_ref)

        at = 0
        for d_ref, dn_ref, wd in zip(d_refs, dn_refs, widths):
            sl = slice(at, at + wd)
            at += wd
            d = d_ref[...]
            u = u_ref[:, sl]
            de = jnp.concatenate([d, jnp.where(i < nT - 1, dn_ref[...], 0.0)], axis=0)
            ue = jnp.concatenate([jnp.where(i > 0, up_ref[:, sl], 0.0), u], axis=0)
            acc = w_ref[CONV_K - 1:CONV_K, sl] * d
            dw_ref[CONV_K - 1:CONV_K, sl] += jnp.sum(d * u, axis=0, keepdims=True)
            for k in range(CONV_K - 1):
                sh = CONV_K - 1 - k
                acc = acc + w_ref[k:k + 1, sl] * pltpu.roll(de, T + CONV_HALO - sh, 0)[:T, :]
                dw_ref[k:k + 1, sl] += jnp.sum(d * pltpu.roll(ue, sh, 0)[CONV_HALO:, :], axis=0, keepdims=True)
            dr_ref[:, sl] = acc.astype(dr_ref.dtype)
            db_ref[:, sl] += jnp.sum(d, axis=0, keepdims=True)

    nxt = lambda i: jnp.minimum((i + 1) * hb, L // CONV_HALO - 1)
    return _call(
        body, grid=(nT,),
        in_specs=[pl.BlockSpec((T, wd), lambda i: (i, 0)) for wd in widths]
        + [pl.BlockSpec((CONV_HALO, wd), lambda i: (nxt(i), 0)) for wd in widths]
        + [pl.BlockSpec((T, C), lambda i: (i, cblk)),
           pl.BlockSpec((CONV_HALO, C), lambda i: (jnp.maximum(i * hb - 1, 0), cblk)),
           pl.BlockSpec((CONV_K, C), lambda i: (0, 0)), _ANY],
        out_specs=[pl.BlockSpec((T, C), lambda i: (i, cblk)),
                   pl.BlockSpec((8, C), lambda i: (0, 0)),
                   pl.BlockSpec((1, C), lambda i: (0, 0))],
        out_shape=[_sds(dproj.shape, dproj.dtype), _sds((8, C), F32), _sds((1, C), F32)],
        aliases={2 * n + 3: 0}, name="conv_bwd", sem=("arbitrary",),
        args=(*dparts, *dparts, proj, proj, cw, dproj), jobs=jobs)


def _softplus(v):
    y = jnp.exp(-jnp.abs(v))
    u = 1.0 + y
    log1p = jnp.where(u == 1.0, y, jnp.log(u) * y / jnp.where(u == 1.0, 1.0, u - 1.0))
    return jnp.maximum(v, 0.0) + log1p


def dt_prep(dtraw, bias, alog, expand, D):
    L = dtraw.shape[0]
    GC = D // GROUPS
    HPG = GC // HEAD_DIM
    Q = _pick(L, SCAN_CHUNK)
    nc = L // Q

    def body(r_ref, b_ref, a_ref, e_ref, dt_ref, acs_ref, acst_ref, dtx_ref, eax_ref, dsx_ref, cdx_ref):
        valid = lax.broadcasted_iota(jnp.int32, (1, LANES), 1) < HPG
        dt = jnp.where(valid, _softplus(r_ref[...] + b_ref[...]), 0.0)
        adt = dt * -jnp.exp(a_ref[...])
        tril = (_row(0, (Q, Q)) >= lax.broadcasted_iota(jnp.int32, (Q, Q), 1)).astype(BF16)
        acs = _sel_dot(tril, adt)
        last = acs[Q - 1:Q, :]
        dt_ref[...] = dt
        acs_ref[...] = acs
        acst_ref[...] = acs.T
        e = e_ref[...]
        dtx_ref[...] = _dot_sel(dt, e)
        eax_ref[...] = jnp.exp(_dot_sel(acs, e))
        dsx_ref[...] = jnp.exp(_dot_sel(last - acs, e))
        cdx_ref[0] = jnp.exp(_dot_sel(jnp.broadcast_to(last, (8, LANES)), e))

    head = pl.BlockSpec((Q, LANES), lambda g, c: (c, g))
    hvec = pl.BlockSpec((1, LANES), lambda g, c: (0, g))
    chan = pl.BlockSpec((Q, GC), lambda g, c: (c, g))
    return pl.pallas_call(
        body, grid=(GROUPS, nc),
        in_specs=[head, hvec, hvec, pl.BlockSpec((LANES, GC), lambda g, c: (0, 0))],
        out_specs=[head, head, pl.BlockSpec((LANES, Q), lambda g, c: (g, c)), chan, chan, chan,
                   pl.BlockSpec((1, 8, GC), lambda g, c: (c, 0, g))],
        out_shape=[_sds((L, GROUPS * LANES), F32), _sds((L, GROUPS * LANES), F32), _sds((GROUPS * LANES, L), F32),
                   _sds((L, D), F32), _sds((L, D), F32), _sds((L, D), F32), _sds((nc, 8, D), F32)],
        name="dt_prep", compiler_params=_params("parallel", "parallel"))(dtraw, bias, alog, expand)


def _scan_specs(L, D, Q, rev):
    GC = D // GROUPS
    nc = L // Q
    ci = (lambda c: nc - 1 - c) if rev else (lambda c: c)
    return dict(
        xs=pl.BlockSpec((Q, GC), lambda g, c: (ci(c), g)),
        b=pl.BlockSpec((Q, STATE), lambda g, c: (ci(c), D // STATE + g)),
        c=pl.BlockSpec((Q, STATE), lambda g, c: (ci(c), D // STATE + GROUPS + g)),
        chan=pl.BlockSpec((Q, GC), lambda g, c: (ci(c), g)),
        cdx=pl.BlockSpec((1, 8, GC), lambda g, c: (ci(c), 0, g)),
        head=pl.BlockSpec((Q, LANES), lambda g, c: (ci(c), g)),
        headt=pl.BlockSpec((LANES, Q), lambda g, c: (g, ci(c))),
        state=pl.BlockSpec((1, 1, STATE, GC), lambda g, c: (ci(c), g, 0, 0)),
        hvec=pl.BlockSpec((1, LANES), lambda g, c: (0, g)),
        cvec=pl.BlockSpec((1, GC), lambda g, c: (0, g)))


def scan_fwd(pre, dtx, eax, dsx, cdx, acs, acst, D):
    L = pre.shape[0]
    GC = D // GROUPS
    Q = _pick(L, SCAN_CHUNK)
    nc = L // Q
    sp = _scan_specs(L, D, Q, False)

    def body(xs_ref, b_ref, c_ref, dtx_ref, eax_ref, dsx_ref, cdx_ref, acs_ref, acst_ref, y_ref, st_ref, s_scr):
        @pl.when(pl.program_id(1) == 0)
        def _():
            s_scr[...] = jnp.zeros_like(s_scr)

        tri = _row(0, (Q, Q)) >= lax.broadcasted_iota(jnp.int32, (Q, Q), 1)
        half = lax.broadcasted_iota(jnp.int32, (1, LANES), 1) // HEAD_DIM
        xs, _ = _silu_and_grad(xs_ref[...])
        bg = _silu_and_grad(b_ref[...])[0].astype(BF16)
        cg = _silu_and_grad(c_ref[...])[0].astype(BF16)
        xdt = xs * dtx_ref[...]
        sprev = s_scr[...]
        st_ref[0, 0] = sprev
        sc = _dot(cg, bg, NT)
        yoff = _dot(cg, sprev.astype(BF16)) * eax_ref[...]
        for j in range(GC // LANES):
            ps = slice(j * LANES, (j + 1) * LANES)
            xp = xdt[:, ps]
            acc = yoff[:, ps]
            for hh in range(2):
                h = 2 * j + hh
                lm = jnp.exp(jnp.where(tri, acs_ref[:, h:h + 1] - acst_ref[h:h + 1, :], -1e30))
                xm = jnp.where(half == hh, xp, 0.0).astype(BF16)
                acc = acc + _dot((sc * lm).astype(BF16), xm)
            y_ref[:, ps] = acc
        xw = (xdt * dsx_ref[...]).astype(BF16)
        s_scr[...] = cdx_ref[0, 0:1, :] * sprev + _dot(bg, xw, TN)

    return pl.pallas_call(
        body, grid=(GROUPS, nc),
        in_specs=[sp["xs"], sp["b"], sp["c"], sp["chan"], sp["chan"], sp["chan"], sp["cdx"], sp["head"], sp["headt"]],
        out_specs=[sp["chan"], sp["state"]],
        out_shape=[_sds((L, D), F32), _sds((nc, GROUPS, STATE, GC), F32)],
        scratch_shapes=[pltpu.VMEM((STATE, GC), F32)], name="scan_fwd",
        compiler_params=_params("parallel", "arbitrary"))(pre, pre, pre, dtx, eax, dsx, cdx, acs, acst)


def scan_bwd(pre, dtx, eax, dsx, cdx, acs, acst, dt, dtraw, bias, alog, states, dy, dexp, collapse, D, jobs=()):
    L = pre.shape[0]
    GC = D // GROUPS
    Q = _pick(L, SCAN_CHUNK)
    nc = L // Q
    sp = _scan_specs(L, D, Q, True)
    rc = lambda c: nc - 1 - c

    def body(xs_ref, b_ref, c_ref, dtx_ref, eax_ref, dsx_ref, cdx_ref, acs_ref, acst_ref, dt_ref, raw_ref,
             bias_ref, alog_ref, st_ref, dy_ref, dexp_ref, col_ref,
             dxs_ref, db_ref, dc_ref, ddt_ref, dal_ref, dbi_ref, ds_scr, dx_scr):
        first = pl.program_id(1) == 0

        @pl.when(first)
        def _():
            ds_scr[...] = jnp.zeros_like(ds_scr)
            dal_ref[...] = jnp.zeros_like(dal_ref)
            dbi_ref[...] = jnp.zeros_like(dbi_ref)

        li = _row(0, (Q, Q))
        si = lax.broadcasted_iota(jnp.int32, (Q, Q), 1)
        lane = lax.broadcasted_iota(jnp.int32, (1, LANES), 1)
        half = lane // HEAD_DIM
        xs_pre, b_pre, c_pre = xs_ref[...], b_ref[...], c_ref[...]
        xs, xs_g = _silu_and_grad(xs_pre)
        bf, b_g = _silu_and_grad(b_pre)
        cf, c_g = _silu_and_grad(c_pre)
        bg, cg = bf.astype(BF16), cf.astype(BF16)
        dtx, eax, dsx = dtx_ref[...], eax_ref[...], dsx_ref[...]
        cd = cdx_ref[0, 0:1, :]
        xdt = xs * dtx
        G = dy_ref[...]
        prev = st_ref[0, 0]
        dsn = ds_scr[...]
        prev_b, dsn_b = prev.astype(BF16), dsn.astype(BF16)
        cp = _dot(cg, prev_b)
        ge_b = (G * eax).astype(BF16)
        d_c = _dot(ge_b, prev_b, NT)
        dprev = _dot(cg, ge_b, TN) + cd * dsn
        chan_a = G * cp * eax
        xw_b = (xdt * dsx).astype(BF16)
        dcd = jnp.sum(prev * dsn, axis=0, keepdims=True)
        d_b = _dot(xw_b, dsn_b, NT)
        dxw = _dot(bg, dsn_b)
        dd = dxw * xdt * dsx
        chan_a = chan_a - dd
        last_c = jnp.sum(dd, axis=0, keepdims=True) + dcd * cd
        sc = _dot(cg, bg, NT)
        sct = _dot(bg, cg, NT)
        dsc = jnp.zeros((Q, Q), F32)
        dsct = jnp.zeros((Q, Q), F32)
        dacs = jnp.zeros((Q, LANES), F32)
        for j in range(GC // LANES):
            ps = slice(j * LANES, (j + 1) * LANES)
            xp, gp = xdt[:, ps], G[:, ps]
            dxp = dxw[:, ps] * dsx[:, ps]
            for hh in range(2):
                h = 2 * j + hh
                col, row = acs_ref[:, h:h + 1], acst_ref[h:h + 1, :]
                lm = jnp.exp(jnp.where(li >= si, col - row, -1e30))
                lmt = jnp.exp(jnp.where(si >= li, row - col, -1e30))
                m, mt = sc * lm, sct * lmt
                xm = jnp.where(half == hh, xp, 0.0).astype(BF16)
                gm = jnp.where(half == hh, gp, 0.0).astype(BF16)
                dm = _dot(gm, xm, NT)
                dmt = _dot(xm, gm, NT)
                dxp = dxp + _dot(mt.astype(BF16), gm)
                dsc = dsc + dm * lm
                dsct = dsct + dmt * lmt
                rs = jnp.sum(dm * m, axis=1, keepdims=True) - jnp.sum(dmt * mt, axis=1, keepdims=True)
                dacs = dacs + jnp.where(lane == h, rs, 0.0)
            dx_scr[:, ps] = dxp
        d_c = d_c + _dot(dsc.astype(BF16), bg)
        d_b = d_b + _dot(dsct.astype(BF16), cg)
        ds_scr[...] = dprev
        dxdt = dx_scr[...]
        dxs_ref[...] = (dxdt * dtx + dexp_ref[...] * G) * xs_g
        db_ref[...] = d_b * b_g
        dc_ref[...] = d_c * c_g
        colm = col_ref[...]
        dacs = dacs + _dot_sel(chan_a, colm)
        dlast = _dot_sel(jnp.broadcast_to(last_c, (8, GC)), colm)[0:1, :]
        dacs = dacs + jnp.where(_row(0, (Q, 1)) == Q - 1, dlast, 0.0)
        dadt = _sel_dot((si >= li).astype(BF16), dacs)
        a = -jnp.exp(alog_ref[...])
        dt = dt_ref[...]
        ddt = dadt * a + _dot_sel(dxdt * xs, colm)
        dal_ref[0:1, :] += jnp.sum(dadt * dt * a, axis=0, keepdims=True)
        draw = ddt * _sigmoid(raw_ref[...] + bias_ref[...])
        dbi_ref[0:1, :] += jnp.sum(draw, axis=0, keepdims=True)
        ddt_ref[...] = draw.astype(ddt_ref.dtype)

    acc = pl.BlockSpec((8, LANES), lambda g, c: (0, g))
    return _call(
        body, grid=(GROUPS, nc),
        in_specs=[sp["xs"], sp["b"], sp["c"], sp["chan"], sp["chan"], sp["chan"], sp["cdx"], sp["head"], sp["headt"],
                  sp["head"], sp["head"], sp["hvec"], sp["hvec"], sp["state"], sp["chan"], sp["cvec"],
                  pl.BlockSpec((GC, LANES), lambda g, c: (0, 0))],
        out_specs=[sp["chan"],
                   pl.BlockSpec((Q, STATE), lambda g, c: (rc(c), g)),
                   pl.BlockSpec((Q, STATE), lambda g, c: (rc(c), g)),
                   sp["head"], acc, acc],
        out_shape=[_sds((L, D), F32), _sds((L, GROUPS * STATE), F32), _sds((L, GROUPS * STATE), F32),
                   _sds((L, GROUPS * LANES), BF16), _sds((8, GROUPS * LANES), F32), _sds((8, GROUPS * LANES), F32)],
        scratch_shapes=[pltpu.VMEM((STATE, GC), F32), pltpu.VMEM((Q, GC), F32)], name="scan_bwd",
        sem=("parallel", "arbitrary"),
        args=(pre, pre, pre, dtx, eax, dsx, cdx, acs, acst, dt, dtraw, bias, alog, states, dy, dexp, collapse),
        jobs=jobs)


def gate_fwd(y, pre, proj, dexp, nw, mixed, D):
    L = y.shape[0]
    GC = D // GROUPS
    T = _pick(L, 256)

    def body(y_ref, xs_ref, z_ref, de_ref, nw_ref, old_ref, o_ref):
        xs, _ = _silu_and_grad(xs_ref[...])
        z = z_ref[...]
        y3 = (y_ref[...] + de_ref[...] * xs) * (z * _sigmoid(z))
        for g in range(GROUPS):
            sl = slice(g * GC, (g + 1) * GC)
            yg = y3[:, sl]
            r = lax.rsqrt(jnp.mean(yg * yg, axis=-1, keepdims=True) + NORM_EPS)
            o_ref[:, sl] = (yg * r * nw_ref[:, sl]).astype(o_ref.dtype)

    row = lambda c: pl.BlockSpec((T, D), lambda i: (i, c))
    vec = pl.BlockSpec((1, D), lambda i: (0, 0))
    return pl.pallas_call(
        body, grid=(L // T,), in_specs=[row(0), row(0), row(2), vec, vec, _ANY], out_specs=row(1),
        out_shape=_sds(mixed.shape, mixed.dtype), input_output_aliases={5: 0}, name="gate_fwd",
        compiler_params=_params("parallel"))(y, pre, proj, dexp, nw, mixed)


def gate_bwd(y, pre, proj, dexp, nw, dmix, D, jobs=()):
    L = y.shape[0]
    GC = D // GROUPS
    T = _pick(L, 256)

    def body(y_ref, xs_ref, z_ref, de_ref, nw_ref, dm_ref, dy_ref, dz_ref, dnw_ref, dde_ref):
        @pl.when(pl.program_id(0) == 0)
        def _():
            dnw_ref[...] = jnp.zeros_like(dnw_ref)
            dde_ref[...] = jnp.zeros_like(dde_ref)

        xs, _ = _silu_and_grad(xs_ref[...])
        sz, dsz = _silu_and_grad(z_ref[...])
        y2 = y_ref[...] + de_ref[...] * xs
        y3 = y2 * sz
        for g in range(GROUPS):
            sl = slice(g * GC, (g + 1) * GC)
            yg, dm = y3[:, sl], dm_ref[:, sl]
            r = lax.rsqrt(jnp.mean(yg * yg, axis=-1, keepdims=True) + NORM_EPS)
            n = yg * r
            gg = dm * nw_ref[:, sl]
            dy3 = r * (gg - n * jnp.mean(gg * n, axis=-1, keepdims=True))
            dnw_ref[:, sl] += jnp.sum(dm * n, axis=0, keepdims=True)
            dy2 = dy3 * sz[:, sl]
            dy_ref[:, sl] = dy2
            dz_ref[:, sl] = (dy3 * y2[:, sl] * dsz[:, sl]).astype(dz_ref.dtype)
            dde_ref[:, sl] += jnp.sum(dy2 * xs[:, sl], axis=0, keepdims=True)

    row = lambda c: pl.BlockSpec((T, D), lambda i: (i, c))
    vec = pl.BlockSpec((1, D), lambda i: (0, 0))
    return _call(
        body, grid=(L // T,), in_specs=[row(0), row(0), row(2), vec, vec, row(1)],
        out_specs=[row(0), row(2), vec, vec],
        out_shape=[_sds((L, D), F32), _sds((L, proj.shape[1]), BF16), _sds((1, D), F32), _sds((1, D), F32)],
        name="gate_bwd", sem=("arbitrary",), args=(y, pre, proj, dexp, nw, dmix), jobs=jobs)


def _adam_math(gv, w, m, v):
    c1 = 1.0 - ADAM_B1 ** ADAM_STEP
    c2 = 1.0 - ADAM_B2 ** ADAM_STEP
    nm = ADAM_B1 * m + (1.0 - ADAM_B1) * gv
    nv = ADAM_B2 * v + (1.0 - ADAM_B2) * (gv * gv)
    return -ADAM_LR * ((nm / c1) / (jnp.sqrt(nv / c2) + ADAM_EPS) + ADAM_WD * w), nm, nv


def adamw(g, w, m, v, name):
    R, C = g.shape
    T = R if R <= 128 else 128
    assert R % T == 0

    def body(g_ref, w_ref, m_ref, v_ref, d_ref, nm_ref, nv_ref):
        d_ref[...], nm_ref[...], nv_ref[...] = _adam_math(g_ref[...], w_ref[...], m_ref[...], v_ref[...])

    blk = pl.BlockSpec((T, C), lambda i: (i, 0))
    return pl.pallas_call(
        body, grid=(R // T,), in_specs=[blk] * 4, out_specs=[blk] * 3,
        out_shape=[_sds((R, C), F32)] * 3, name=name, compiler_params=_params("parallel"))(g, w, m, v)


def adamw_layer(g, w, m, v, layer, prev, name):
    R, C = g.shape
    T = _rows_tile(R, 128)
    assert R % T == 0

    def body(g_ref, w_ref, m_ref, v_ref, *rest):
        go_ref, d_ref, nm_ref, nv_ref = rest[-4:]
        gv = g_ref[...]
        go_ref[0] = gv
        d_ref[0], nm_ref[0], nv_ref[0] = _adam_math(gv, w_ref[0], m_ref[0], v_ref[0])

    mine = pl.BlockSpec((1, T, C), lambda i: (layer, i, 0))
    prev = list(prev or [])
    return pl.pallas_call(
        body, grid=(R // T,), in_specs=[pl.BlockSpec((T, C), lambda i: (i, 0)), mine, mine, mine] + [_ANY] * len(prev),
        out_specs=[mine] * 4, out_shape=[_sds(w.shape, F32)] * 4,
        input_output_aliases={4 + i: i for i in range(len(prev))}, name=name,
        compiler_params=_params("parallel"))(g, w, m, v, *prev)


def cast_bf16(a):
    R, C = a.shape
    T = _rows_tile(R, 256)

    def body(a_ref, o_ref):
        o_ref[...] = a_ref[...].astype(BF16)

    blk = pl.BlockSpec((T, C), lambda i: (i, 0))
    return pl.pallas_call(body, grid=(R // T,), in_specs=[blk], out_specs=blk, out_shape=_sds((R, C), BF16),
                          name="cast_bf16", compiler_params=_params("parallel"))(a)


def _to_groups(v, hpg):
    lead = v.shape[:-1]
    t = v.reshape(lead + (GROUPS, hpg))
    t = jnp.pad(t, [(0, 0)] * (len(lead) + 1) + [(0, LANES - hpg)])
    return t.reshape(lead + (GROUPS * LANES,))


def _from_groups(a, hpg):
    lead = a.shape[:-1]
    return a.reshape(lead + (GROUPS, LANES))[..., :hpg].reshape(lead + (GROUPS * hpg,))


def _expand_matrix(D):
    gc = D // GROUPS
    return (jnp.arange(LANES)[:, None] == (jnp.arange(gc)[None, :] // HEAD_DIM)).astype(BF16)


def layer_params(pre_w, w_in_t, mixw, scale, cw, cb, bias, alog, dskip, nw, w_out_full, post_w, D):
    hpg = D // GROUPS // HEAD_DIM
    main = w_in_t.shape[0] - GROUPS * hpg
    wdt_t = _to_groups(w_in_t[main:].T, hpg).T
    return dict(
        pre_w=pre_w[None], win_t=w_in_t, main=main, wdt_t=wdt_t, mixw=mixw,
        scale=scale[None], cw=cw, cb=cb[None], bias=_to_groups(bias, hpg)[None], alog=_to_groups(alog, hpg)[None],
        dexp=jnp.repeat(dskip, HEAD_DIM)[None], nw=nw[None], wout=w_out_full, post_w=post_w[None])


def layer_fwd(x, p, D, next_shards=None):
    send_in = [gather_send_job(next_shards[:1], [True])] if next_shards else []
    h = rms_fwd(x, p["pre_w"])
    proj, got = matmul(h, p["win_t"], "nt", F32, "proj", jobs=send_in, n_out=p["main"])
    dtraw, _ = matmul(h, p["wdt_t"], "nt", F32, "dtproj")
    mixed, pooled = pool_fwd(proj, p["mixw"], p["scale"], D)
    pre = conv_fwd(proj, p["cw"], p["cb"], D)
    dtp = dt_prep(dtraw, p["bias"], p["alog"], _expand_matrix(D), D)
    dt, acs, acst, dtx, eax, dsx, cdx = dtp
    y, states = scan_fwd(pre, dtx, eax, dsx, cdx, acs, acst, D)
    mixed = gate_fwd(y, pre, proj, p["dexp"], p["nw"], mixed, D)
    jobs = [gather_send_job(next_shards[1:]), gather_pass_job(got[0], [True])] if next_shards else []
    out, got = matmul(mixed, p["wout"], "nn", F32, "outproj", jobs=jobs)
    xn, got2 = post_fwd(x, out, p["post_w"], jobs=[gather_pass_job(got[0])] if next_shards else [])
    gathered = got[1] + got2[0] if next_shards else None
    return xn, dict(x=x, h=h, proj=proj, dtraw=dtraw, pooled=pooled, pre=pre, dtp=dtp, y=y, states=states,
                    mixed=mixed, out=out), gathered


def layer_bwd(dxn, p, s, D, where=None):
    reduce = where is not None
    chip, core = where if reduce else (None, None)
    hpg = D // GROUPS // HEAD_DIM
    PGW = D // GROUPS
    main = p["main"]
    SH = (main + GROUPS * hpg) // 4
    dt, acs, acst, dtx, eax, dsx, cdx = s["dtp"]
    dout, d_post = post_bwd(s["out"], p["post_w"], dxn)
    dmix, _ = matmul(dout, p["wout"], "nt", F32, "dmixed")
    d_wout, _ = matmul(s["mixed"], dout, "tn", F32, "dwout")
    g_out = d_wout.reshape(4, 2, D // 4, D)
    (dy2, dproj, d_nw, d_dexp), got = gate_bwd(s["y"], s["pre"], s["proj"], p["dexp"], p["nw"], dmix, D,
                                               jobs=[pair_exchange_job([g_out])] if reduce else [])
    pair_out = pair_add(g_out, got[0][0], core, BF16) if reduce else None
    (dxs, db, dc, ddtraw, d_alog, d_bias), got = scan_bwd(
        s["pre"], dtx, eax, dsx, cdx, acs, acst, dt, s["dtraw"], p["bias"], p["alog"], s["states"], dy2,
        p["dexp"], _expand_matrix(D).T, D, jobs=[chip_exchange_job([pair_out])] if reduce else [])
    mine_out = chip_add(pair_out, got[0][0], chip, core) if reduce else None
    (dproj, d_cw, d_cb), got = conv_bwd([dxs, db, dc], s["proj"], p["cw"], dproj, D,
                                        jobs=[pair_gather_job([mine_out])] if reduce else [])
    r_out = got[0][0] if reduce else None
    dproj, dmixed, d_scale = pool_bwd(s["proj"], dmix, s["pooled"], p["mixw"], p["scale"], dproj, D)
    d_mixw = pool_dw(s["pooled"], dmixed, D)
    d_wmain_t, _ = matmul(dproj, s["h"], "tn", F32, "dwmain")
    d_wdt_t, _ = matmul(ddtraw, s["h"], "tn", F32, "dwdt")
    late = [d_wmain_t[None], d_wdt_t[None],
            d_mixw.reshape(GROUPS, 4, PGW // 4, PGW).transpose(1, 0, 2, 3).reshape(4, 2, GROUPS * PGW // 8, PGW)]
    cols = [True, True, False]
    dh_b, got = matmul(ddtraw, p["wdt_t"], "nn", F32, "dh_dt",
                       jobs=[pair_exchange_job(late, cols)] if reduce else [])
    if reduce:
        p_main, p_dt, p_mix = [pair_add(g, r, core, BF16, k) for g, r, k in zip(late, got[0], cols)]
        p_in = jnp.concatenate([p_main[0], _from_groups(p_dt[0].T, hpg).T], axis=0)
        pairs = [p_in.reshape(4, SH, D // 2), p_mix]
    dh_a, got = matmul(dproj, p["win_t"], "nn", F32, "dh_main", jobs=[chip_exchange_job(pairs)] if reduce else [])
    mines = [chip_add(q, r, chip, core, k) for q, r, k in zip(pairs, got[0], [True, False])] if reduce else None
    (dx, d_pre), got = rms_bwd(s["x"], p["pre_w"], dh_a, dh_b, dxn,
                               jobs=[pair_gather_job(mines, [True, False])] if reduce else [])
    reduced = dict(w_in=got[0][0], w_out=r_out, pool_mix_w=got[0][1]) if reduce else None
    grads = dict(
        pre_norm_w=d_pre[0], pool_scale=d_scale[0], conv_w=d_cw[:CONV_K], conv_b=d_cb[0],
        dt_bias=_from_groups(d_bias[0], hpg), a_log=_from_groups(d_alog[0], hpg),
        d_skip=d_dexp[0].reshape(-1, HEAD_DIM).sum(axis=-1), ssd_norm_w=d_nw[0], post_norm_w=d_post[0])
    if not reduce:
        grads.update(w_in=jnp.concatenate([d_wmain_t.T, _from_groups(d_wdt_t.T, hpg)], axis=1), pool_mix_w=d_mixw,
                     w_out=d_wout)
    return dx, grads, reduced


def local_step(x, target, params, D):
    saved = []
    for p in params:
        x, s, _ = layer_fwd(x, p, D)
        saved.append(s)
    dx, sumsq = loss_head(x, target)
    grads = [None] * len(params)
    for l in reversed(range(len(params))):
        dx, grads[l], _ = layer_bwd(dx, params[l], saved[l], D)
    return sumsq, dx, grads


SMALL = ("pre_norm_w", "pool_scale", "conv_w", "conv_b", "dt_bias", "a_log", "d_skip", "ssd_norm_w", "post_norm_w")
BIG = ("w_in", "w_out", "pool_mix_w")


def _pack(parts):
    flat = jnp.concatenate([p.reshape(-1) for p in parts])
    n = flat.shape[0]
    rows = -(-n // (LANES * LANES)) * LANES
    return jnp.pad(flat, (0, rows * LANES - n)).reshape(rows, LANES)


def _unpack(packed, shapes):
    flat, out, at = packed.reshape(-1), [], 0
    for s in shapes:
        n = math.prod(s)
        out.append(flat[at:at + n].reshape(s))
        at += n
    return out


def kernel(x, pre_norm_w, w_in, pool_mix_w, pool_scale, conv_w, conv_b, dt_bias, a_log, d_skip, ssd_norm_w, w_out, post_norm_w, loss_target, m_pre_norm_w, m_w_in, m_pool_mix_w, m_pool_scale, m_conv_w, m_conv_b, m_dt_bias, m_a_log, m_d_skip, m_ssd_norm_w, m_w_out, m_post_norm_w, v_pre_norm_w, v_w_in, v_pool_mix_w, v_pool_scale, v_conv_w, v_conv_b, v_dt_bias, v_a_log, v_d_skip, v_ssd_norm_w, v_w_out, v_post_norm_w):
    NL, D, SH = w_in.shape
    PGW = D // GROUPS
    CS = conv_w.shape[2]
    chip = (2 * lax.axis_index("x") + lax.axis_index("y")).astype(jnp.int32)
    chip1, core = chip.reshape(1), lax.axis_index("c").astype(jnp.int32).reshape(1)

    tr = lambda t: jnp.transpose(t, (0, 2, 1))
    w_in_t, m_w_in_t, v_w_in_t = tr(w_in), tr(m_w_in), tr(v_w_in)
    halved_by_cols = [True, False, False, False]

    def shards(l):
        return [cast_bf16(w_in_t[l]), cast_bf16(w_out[l]).reshape(2, D // 4, D),
                cast_bf16(pool_mix_w[l].reshape(GROUPS * PGW // 4, PGW)).reshape(2, GROUPS * PGW // 8, PGW),
                conv_w[l].reshape(2, CONV_K * CS // (2 * LANES), LANES)]

    def params(l, g):
        g_in, g_out, g_mix, g_cw = g
        mix_l = g_mix.reshape(4, GROUPS, PGW // 4, PGW).transpose(1, 0, 2, 3).reshape(GROUPS, PGW, PGW)
        cw_l = g_cw.reshape(4, CONV_K, CS).transpose(1, 0, 2).reshape(CONV_K, 4 * CS)
        return layer_params(pre_norm_w[l], g_in.reshape(4 * SH, D), mix_l, pool_scale[l], cw_l, conv_b[l], dt_bias[l],
                            a_log[l], d_skip[l], ssd_norm_w[l], g_out.reshape(2 * D, D), post_norm_w[l], D)

    gathered = run_jobs([gather_send_job(shards(0), halved_by_cols)], "gather_send")[0]
    gathered = run_jobs([gather_pass_job(gathered, halved_by_cols)], "gather_pass")[0]
    h, ps, saved = x[0], [], []
    for l in range(NL):
        ps.append(params(l, gathered))
        h, s, gathered = layer_fwd(h, ps[l], D, shards(l + 1) if l + 1 < NL else None)
        saved.append(s)
    dx, sumsq = loss_head(h, loss_target[0])

    given = dict(w_in=(w_in_t, m_w_in_t, v_w_in_t), w_out=(w_out, m_w_out, v_w_out),
                 pool_mix_w=(pool_mix_w, m_pool_mix_w, v_pool_mix_w))
    flat = {n: [t.reshape(NL, -1, t.shape[-1]) for t in given[n]] for n in BIG}
    done = {n: None for n in BIG}
    grads = [None] * NL
    for l in reversed(range(NL)):
        dx, grads[l], reduced = layer_bwd(dx, ps[l], saved[l], D, (chip1, core))
        for n in BIG:
            r = reduced[n]
            done[n] = adamw_layer(r.reshape(-1, r.shape[-1]), *flat[n], l, done[n], "adamw_" + n)

    small_shapes = [(NL,) + grads[0][n].shape for n in SMALL]
    packed = _pack([0.5 / D * sumsq[0, :1]] + [jnp.stack([g[n] for g in grads]) for n in SMALL])
    total = allreduce_small(packed)
    loss, *small = _unpack(total, [(1,)] + small_shapes)
    small = dict(zip(SMALL, small))
    small["conv_w"] = lax.dynamic_slice_in_dim(small["conv_w"], chip * CS, CS, axis=2)

    given_small = dict(
        pre_norm_w=(pre_norm_w, m_pre_norm_w, v_pre_norm_w), pool_scale=(pool_scale, m_pool_scale, v_pool_scale),
        conv_w=(conv_w, m_conv_w, v_conv_w), conv_b=(conv_b, m_conv_b, v_conv_b),
        dt_bias=(dt_bias, m_dt_bias, v_dt_bias), a_log=(a_log, m_a_log, v_a_log),
        d_skip=(d_skip, m_d_skip, v_d_skip), ssd_norm_w=(ssd_norm_w, m_ssd_norm_w, v_ssd_norm_w),
        post_norm_w=(post_norm_w, m_post_norm_w, v_post_norm_w))
    shapes = [given_small[n][0].shape for n in SMALL]
    upd = adamw(_pack([small[n] for n in SMALL]), *[_pack([given_small[n][i] for n in SMALL]) for i in range(3)],
                "adamw_small")
    upd = [dict(zip(SMALL, _unpack(u, shapes))) for u in upd]

    out = {n: (small[n], upd[0][n], upd[1][n], upd[2][n]) for n in SMALL}
    for n in BIG:
        out[n] = tuple(t.reshape(given[n][0].shape) for t in done[n])
    out["w_in"] = tuple(tr(t) for t in out["w_in"])

    order = ("pre_norm_w", "w_in", "pool_mix_w", "pool_scale", "conv_w", "conv_b", "dt_bias", "a_log", "d_skip",
             "ssd_norm_w", "w_out", "post_norm_w")
    return (loss.reshape(()), dx[None], *[out[n][0] for n in order], *[out[n][1] for n in order],
            *[out[n][2] for n in order], *[out[n][3] for n in order])
```

```python
import functools
import math

import jax
import jax.numpy as jnp
from jax import lax
from jax.experimental import pallas as pl
from jax.experimental.pallas import tpu as pltpu

F32 = jnp.float32
BF16 = jnp.bfloat16

NORM_EPS = 1e-6
HEAD_DIM = 64
STATE = 128
GROUPS = 4
POOL_WINDOWS = (2, 4, 8, 16)
POOL_HALO = 16
CONV_K = 4
CONV_HALO = 8
SCAN_CHUNK = 256
LANES = 128
VMEM_LIMIT = 52 * 1024 * 1024

ADAM_LR = 0.001
ADAM_B1 = 0.9
ADAM_B2 = 0.999
ADAM_EPS = 1e-08
ADAM_WD = 0.01
ADAM_STEP = 10

MESH = pl.DeviceIdType.MESH

NN = (((1,), (0,)), ((), ()))
NT = (((1,), (1,)), ((), ()))
TN = (((0,), (0,)), ((), ()))

_ANY = pl.BlockSpec(memory_space=pl.ANY)


def _params(*sem):
    return pltpu.CompilerParams(dimension_semantics=sem, vmem_limit_bytes=VMEM_LIMIT)


def _pick(dim, pref):
    if dim <= pref:
        return dim
    t = (pref // LANES) * LANES
    while t > LANES and dim % t:
        t -= LANES
    assert dim % t == 0, (dim, pref)
    return t


def _rows_tile(rows, pref):
    t = (min(pref, rows) // 8) * 8
    while t >= 8 and rows % t:
        t -= 8
    return t if t >= 8 else rows


def _dot(a, b, dn=NN):
    return lax.dot_general(a, b, dn, preferred_element_type=F32)


def _split3(a):
    hi = a.astype(BF16)
    r = a - hi.astype(F32)
    mid = r.astype(BF16)
    return hi, mid, (r - mid.astype(F32)).astype(BF16)


def _dot_sel(a, e):
    hi, mid, lo = _split3(a)
    return (_dot(lo, e) + _dot(mid, e)) + _dot(hi, e)


def _sel_dot(e, b):
    hi, mid, lo = _split3(b)
    return (_dot(e, lo) + _dot(e, mid)) + _dot(e, hi)


def _sigmoid(v):
    return 1.0 / (1.0 + jnp.exp(-v))


def _silu_and_grad(v):
    s = _sigmoid(v)
    return v * s, s * (1.0 + v * (1.0 - s))


def _row(i, shape):
    return lax.broadcasted_iota(jnp.int32, shape, 0) + i


def _sds(shape, dtype):
    return jax.ShapeDtypeStruct(tuple(shape), dtype)


class Job:
    def __init__(self, ins, outs, aliased, nsem, start, finish):
        self.ins, self.outs, self.aliased, self.nsem, self.start, self.finish = ins, outs, aliased, nsem, start, finish


def _place():
    x, y, c = lax.axis_index("x"), lax.axis_index("y"), lax.axis_index("c")
    return x, y, c, [(1 - x, y), (x, 1 - y), (1 - x, 1 - y)]


def _remote(src, dst, send_sem, recv_sem, device):
    return pltpu.make_async_remote_copy(src_ref=src, dst_ref=dst, send_sem=send_sem, recv_sem=recv_sem,
                                        device_id=device, device_id_type=MESH)


def _half(ref, c, cols, lead=0):
    idx = [slice(None)] * lead
    if cols:
        w = ref.shape[-1] // 2
        idx += [slice(None)] * (len(ref.shape) - lead - 1) + [pl.ds(pl.multiple_of(c * w, LANES), w)]
    else:
        idx += [c]
    return ref.at[tuple(idx)]


def _flags(cols, n):
    return list(cols) if cols else [False] * n


def gather_send_job(arrs, cols=None):
    n = len(arrs)
    cols = _flags(cols, n)

    def copies(ins, outs, send, recv):
        x, y, c, chips = _place()
        mine = 2 * x + y
        out = []
        for a in range(n):
            out.append(_remote(ins[a], outs[a].at[mine], send.at[4 * a + 3], recv.at[4 * a + 3], (x, y, 1 - c)))
            for j, chip in enumerate(chips):
                out.append(_remote(_half(ins[a], c, cols[a]), _half(outs[a].at[mine], c, cols[a]),
                                   send.at[4 * a + j], recv.at[4 * a + j], (*chip, c)))
        return out

    def start(ins, outs, send, recv):
        for cp in copies(ins, outs, send, recv):
            cp.start()

    def finish(ins, outs, send, recv):
        x, y, c, chips = _place()
        for a in range(n):
            for j, chip in enumerate(chips):
                landed = _half(outs[a].at[2 * chip[0] + chip[1]], c, cols[a])
                _remote(landed, landed, send.at[4 * a + j], recv.at[4 * a + j], (x, y, 1 - c)).wait_recv()
            twin = outs[a].at[2 * x + y]
            _remote(twin, twin, send.at[4 * a + 3], recv.at[4 * a + 3], (x, y, 1 - c)).wait_recv()
        for cp in copies(ins, outs, send, recv):
            cp.wait_send()

    return Job(list(arrs), [_sds((4,) + a.shape, a.dtype) for a in arrs], False, 4 * n, start, finish)


def gather_pass_job(bufs, cols=None):
    n = len(bufs)
    cols = _flags(cols, n)

    def copies(outs, send, recv):
        x, y, c, chips = _place()
        out = []
        for a in range(n):
            for j, chip in enumerate(chips):
                landed = _half(outs[a].at[2 * chip[0] + chip[1]], c, cols[a])
                out.append(_remote(landed, landed, send.at[3 * a + j], recv.at[3 * a + j], (x, y, 1 - c)))
        return out

    def start(ins, outs, send, recv):
        for cp in copies(outs, send, recv):
            cp.start()

    def finish(ins, outs, send, recv):
        x, y, c, chips = _place()
        for a in range(n):
            for j, chip in enumerate(chips):
                passed = _half(outs[a].at[2 * chip[0] + chip[1]], 1 - c, cols[a])
                _remote(passed, passed, send.at[3 * a + j], recv.at[3 * a + j], (x, y, 1 - c)).wait_recv()
        for cp in copies(outs, send, recv):
            cp.wait_send()

    return Job(list(bufs), [_sds(b.shape, b.dtype) for b in bufs], True, 3 * n, start, finish)


def pair_exchange_job(arrs, cols=None):
    n = len(arrs)
    cols = _flags(cols, n)

    def copies(ins, outs, send, recv):
        x, y, c, _ = _place()
        return [_remote(_half(ins[a], 1 - c, cols[a], 1), outs[a], send.at[a], recv.at[a], (x, y, 1 - c))
                for a in range(n)]

    def start(ins, outs, send, recv):
        for cp in copies(ins, outs, send, recv):
            cp.start()

    def finish(ins, outs, send, recv):
        for cp in copies(ins, outs, send, recv):
            cp.wait()

    shape = lambda a, k: a.shape[:-1] + (a.shape[-1] // 2,) if k else a.shape[:1] + a.shape[2:]
    return Job(list(arrs), [_sds(shape(a, k), a.dtype) for a, k in zip(arrs, cols)], False, n, start, finish)


def chip_exchange_job(arrs):
    n = len(arrs)

    def copies(ins, outs, send, recv):
        x, y, c, chips = _place()
        return [_remote(ins[a].at[2 * chip[0] + chip[1]], outs[a].at[j], send.at[3 * a + j], recv.at[3 * a + j],
                        (*chip, c)) for a in range(n) for j, chip in enumerate(chips)]

    def start(ins, outs, send, recv):
        for cp in copies(ins, outs, send, recv):
            cp.start()

    def finish(ins, outs, send, recv):
        for cp in copies(ins, outs, send, recv):
            cp.wait()

    return Job(list(arrs), [_sds((3,) + a.shape[1:], a.dtype) for a in arrs], False, 3 * n, start, finish)


def pair_gather_job(bufs, cols=None):
    n = len(bufs)
    cols = _flags(cols, n)

    def copies(outs, send, recv):
        x, y, c, _ = _place()
        return [_remote(_half(outs[a], c, cols[a]), _half(outs[a], c, cols[a]), send.at[a], recv.at[a],
                        (x, y, 1 - c)) for a in range(n)]

    def start(ins, outs, send, recv):
        for cp in copies(outs, send, recv):
            cp.start()

    def finish(ins, outs, send, recv):
        for cp in copies(outs, send, recv):
            cp.wait()

    return Job(list(bufs), [_sds(b.shape, b.dtype) for b in bufs], True, n, start, finish)


def _call(body, *, grid, in_specs, out_specs, out_shape, name, sem, args, scratch_shapes=(), jobs=(), aliases=None):
    in_specs, out_specs, out_shape, scratch_shapes = list(in_specs), list(out_specs), list(out_shape), list(scratch_shapes)
    aliases = dict(aliases or {})
    n_in, n_out, n_scr = len(in_specs), len(out_specs), len(scratch_shapes)
    if jobs:
        sem = ("arbitrary",) * len(grid)
    at_in, at_out = n_in, n_out
    for j in jobs:
        if j.aliased:
            aliases.update({at_in + i: at_out + i for i in range(len(j.ins))})
        at_in, at_out = at_in + len(j.ins), at_out + len(j.outs)

    def wrapped(*refs):
        ins, p = refs[:n_in], n_in
        jins = []
        for j in jobs:
            jins.append(refs[p:p + len(j.ins)])
            p += len(j.ins)
        outs, p = refs[p:p + n_out], p + n_out
        jouts = []
        for j in jobs:
            jouts.append(refs[p:p + len(j.outs)])
            p += len(j.outs)
        scr, sems = refs[p:p + n_scr], refs[p + n_scr:]

        def start():
            for k, j in enumerate(jobs):
                j.start(jins[k], jouts[k], sems[2 * k], sems[2 * k + 1])

        def finish():
            for k, j in enumerate(jobs):
                j.finish(jins[k], jouts[k], sems[2 * k], sems[2 * k + 1])

        if jobs and grid:
            ids = [pl.program_id(d) for d in range(len(grid))]
            pl.when(functools.reduce(jnp.logical_and, [i == 0 for i in ids]))(start)
            body(*ins, *outs, *scr)
            pl.when(functools.reduce(jnp.logical_and, [i == g - 1 for i, g in zip(ids, grid)]))(finish)
        else:
            start()
            body(*ins, *outs, *scr)
            finish()

    kwargs = dict(grid=grid) if grid else {}
    res = pl.pallas_call(
        wrapped, in_specs=in_specs + [_ANY] * (at_in - n_in), out_specs=out_specs + [_ANY] * (at_out - n_out),
        out_shape=out_shape + [o for j in jobs for o in j.outs],
        scratch_shapes=scratch_shapes + [pltpu.SemaphoreType.DMA((j.nsem,)) for j in jobs for _ in range(2)],
        input_output_aliases=aliases, name=name,
        compiler_params=pltpu.CompilerParams(dimension_semantics=sem, vmem_limit_bytes=VMEM_LIMIT) if grid
        else pltpu.CompilerParams(vmem_limit_bytes=VMEM_LIMIT), **kwargs)(*args, *[a for j in jobs for a in j.ins])
    res = list(res)
    outs, rest, per_job = res[:n_out], res[n_out:], []
    for j in jobs:
        per_job.append(rest[:len(j.outs)])
        rest = rest[len(j.outs):]
    return outs, per_job


def run_jobs(jobs, name):
    return _call(lambda: None, grid=(), in_specs=[], out_specs=[], out_shape=[], name=name, sem=(), args=(), jobs=jobs)[1]


def pair_add(g, r, core, out_dtype, cols=False):
    S, R, C = r.shape
    T = _rows_tile(R, 256 if C <= 4096 else 128)

    def body(c_ref, g_ref, r_ref, o_ref):
        o_ref[0] = ((g_ref[0] if cols else g_ref[0, 0]).astype(F32) + r_ref[0].astype(F32)).astype(o_ref.dtype)

    g_spec = (pl.BlockSpec((1, T, C), lambda k, i, c_ref: (k, i, c_ref[0])) if cols
              else pl.BlockSpec((1, 1, T, C), lambda k, i, c_ref: (k, c_ref[0], i, 0)))
    return pl.pallas_call(
        body,
        grid_spec=pltpu.PrefetchScalarGridSpec(
            num_scalar_prefetch=1, grid=(S, R // T),
            in_specs=[g_spec, pl.BlockSpec((1, T, C), lambda k, i, c_ref: (k, i, 0))],
            out_specs=pl.BlockSpec((1, T, C), lambda k, i, c_ref: (k, i, 0))),
        out_shape=_sds((S, R, C), out_dtype), name="pair_add",
        compiler_params=_params("parallel", "parallel"))(core, g, r)


def chip_add(p, r, chip, core, cols=False):
    _, R, C = p.shape
    T = _rows_tile(R, 256)

    def body(k_ref, c_ref, p_ref, r0_ref, r1_ref, r2_ref, o_ref):
        s = ((p_ref[0].astype(F32) + r0_ref[0].astype(F32)) + r1_ref[0].astype(F32)) + r2_ref[0].astype(F32)
        if cols:
            o_ref[...] = s
        else:
            o_ref[0] = s

    slot = lambda j: pl.BlockSpec((1, T, C), lambda i, k_ref, c_ref: (j, i, 0))
    out_spec = (pl.BlockSpec((T, C), lambda i, k_ref, c_ref: (i, c_ref[0])) if cols
                else pl.BlockSpec((1, T, C), lambda i, k_ref, c_ref: (c_ref[0], i, 0)))
    return pl.pallas_call(
        body,
        grid_spec=pltpu.PrefetchScalarGridSpec(
            num_scalar_prefetch=2, grid=(R // T,),
            in_specs=[pl.BlockSpec((1, T, C), lambda i, k_ref, c_ref: (k_ref[0], i, 0)), slot(0), slot(1), slot(2)],
            out_specs=out_spec),
        out_shape=_sds((R, 2 * C) if cols else (2, R, C), F32), name="chip_add",
        compiler_params=_params("parallel"))(chip, core, p, r, r, r)


def allreduce_small(v):
    R = v.shape[0]

    def body(v_ref, o_ref, buf, send_sems, recv_sems, local_sem):
        x, y, c, chips = _place()
        me, sibling = (x, y, c), (x, y, 1 - c)

        def rows(px, py, pc):
            return buf.at[pl.ds((4 * px + 2 * py + pc) * R, R), :]

        def copy(k, block, to, src=None):
            return _remote(rows(*block) if src is None else src, rows(*block), send_sems.at[k], recv_sems.at[k], to)

        mine = pltpu.make_async_copy(v_ref, rows(*me), local_sem)
        mine.start()
        first = [copy(0, me, sibling, src=v_ref)]
        first += [copy(1 + j, me, (*chip, c), src=v_ref) for j, chip in enumerate(chips)]
        for cp in first:
            cp.start()
        passed = [copy(4 + j, (*chip, c), sibling) for j, chip in enumerate(chips)]
        for j, chip in enumerate(chips):
            copy(1 + j, (*chip, c), me).wait_recv()
            passed[j].start()
        copy(0, sibling, me).wait_recv()
        for j, chip in enumerate(chips):
            copy(4 + j, (*chip, 1 - c), me).wait_recv()
        for cp in first + passed:
            cp.wait_send()
        mine.wait()
        acc = buf[0:R, :]
        for d in range(1, 8):
            acc = acc + buf[d * R:(d + 1) * R, :]
        o_ref[...] = acc

    return pl.pallas_call(
        body, in_specs=[pl.BlockSpec(memory_space=pltpu.VMEM)], out_specs=pl.BlockSpec(memory_space=pltpu.VMEM),
        out_shape=_sds((R, LANES), F32),
        scratch_shapes=[pltpu.VMEM((8 * R, LANES), F32), pltpu.SemaphoreType.DMA((7,)),
                        pltpu.SemaphoreType.DMA((7,)), pltpu.SemaphoreType.DMA],
        name="allreduce_small", compiler_params=pltpu.CompilerParams(vmem_limit_bytes=VMEM_LIMIT))(v)


def matmul(a, b, mode, out_dtype, name, tm=512, tn=1024, tk=4608, jobs=(), n_out=None):
    if mode == "nn":
        (M, K), (K2, N) = a.shape, b.shape
    elif mode == "nt":
        (M, K), (N, K2) = a.shape, b.shape
        N = n_out or N
    else:
        (K, M), (K2, N) = a.shape, b.shape
    assert K == K2 or (mode == "nn" and K2 > K)
    tm, tn, tk = _pick(M, tm), _pick(N, tn), _pick(K, tk)
    nk = K // tk
    dn = {"nn": NN, "nt": NT, "tn": TN}[mode]

    def body(a_ref, b_ref, o_ref, *acc):
        part = _dot(a_ref[...].astype(BF16), b_ref[...].astype(BF16), dn)
        if nk == 1:
            o_ref[...] = part.astype(o_ref.dtype)
            return
        acc_ref, = acc
        k = pl.program_id(2)

        @pl.when(k == 0)
        def _():
            acc_ref[...] = part

        @pl.when(jnp.logical_and(k > 0, k < nk - 1))
        def _():
            acc_ref[...] += part

        @pl.when(k == nk - 1)
        def _():
            o_ref[...] = (acc_ref[...] + part).astype(o_ref.dtype)

    a_spec = (pl.BlockSpec((tk, tm), lambda i, j, k: (k, i)) if mode == "tn"
              else pl.BlockSpec((tm, tk), lambda i, j, k: (i, k)))
    b_spec = (pl.BlockSpec((tn, tk), lambda i, j, k: (j, k)) if mode == "nt"
              else pl.BlockSpec((tk, tn), lambda i, j, k: (k, j)))
    outs, per_job = _call(
        body, grid=(M // tm, N // tn, nk), in_specs=[a_spec, b_spec],
        out_specs=[pl.BlockSpec((tm, tn), lambda i, j, k: (i, j))], out_shape=[_sds((M, N), out_dtype)],
        scratch_shapes=[pltpu.VMEM((tm, tn), F32)] if nk > 1 else [], name=name,
        sem=("parallel", "parallel", "arbitrary"), args=(a, b), jobs=jobs)
    return outs[0], per_job


def rms_fwd(x, w):
    L, D = x.shape
    T = _pick(L, 256)

    def body(x_ref, w_ref, h_ref):
        xv = x_ref[...]
        r = lax.rsqrt(jnp.mean(xv * xv, axis=-1, keepdims=True) + NORM_EPS)
        h_ref[...] = (xv * r * w_ref[...]).astype(h_ref.dtype)

    return pl.pallas_call(
        body, grid=(L // T,),
        in_specs=[pl.BlockSpec((T, D), lambda i: (i, 0)), pl.BlockSpec((1, D), lambda i: (0, 0))],
        out_specs=pl.BlockSpec((T, D), lambda i: (i, 0)),
        out_shape=_sds((L, D), BF16), name="rms_fwd", compiler_params=_params("parallel"))(x, w)


def post_fwd(x, o, w, jobs=()):
    L, D = x.shape
    T = _pick(L, 256)

    def body(x_ref, o_ref, w_ref, y_ref):
        ov = o_ref[...]
        r = lax.rsqrt(jnp.mean(ov * ov, axis=-1, keepdims=True) + NORM_EPS)
        y_ref[...] = x_ref[...] + ov * r * w_ref[...]

    row = pl.BlockSpec((T, D), lambda i: (i, 0))
    outs, per_job = _call(
        body, grid=(L // T,), in_specs=[row, row, pl.BlockSpec((1, D), lambda i: (0, 0))], out_specs=[row],
        out_shape=[_sds((L, D), F32)], name="post_fwd", sem=("parallel",), args=(x, o, w), jobs=jobs)
    return outs[0], per_job


def _rms_bwd_math(xv, w, dy):
    r = lax.rsqrt(jnp.mean(xv * xv, axis=-1, keepdims=True) + NORM_EPS)
    xhat = xv * r
    g = dy * w
    dx = r * (g - xhat * jnp.mean(g * xhat, axis=-1, keepdims=True))
    return dx, jnp.sum(dy * xhat, axis=0, keepdims=True)


def post_bwd(o, w, dxn):
    L, D = o.shape
    T = _pick(L, 256)

    def body(o_ref, w_ref, d_ref, do_ref, dw_ref):
        dx, dw = _rms_bwd_math(o_ref[...], w_ref[...], d_ref[...])
        do_ref[...] = dx.astype(do_ref.dtype)

        @pl.when(pl.program_id(0) == 0)
        def _():
            dw_ref[...] = jnp.zeros_like(dw_ref)

        dw_ref[...] += dw

    row = pl.BlockSpec((T, D), lambda i: (i, 0))
    vec = pl.BlockSpec((1, D), lambda i: (0, 0))
    return pl.pallas_call(
        body, grid=(L // T,), in_specs=[row, vec, row], out_specs=[row, vec],
        out_shape=[_sds((L, D), BF16), _sds((1, D), F32)],
        name="post_bwd", compiler_params=_params("arbitrary"))(o, w, dxn)


def rms_bwd(x, w, dh_a, dh_b, dxn, jobs=()):
    L, D = x.shape
    T = _pick(L, 256)

    def body(x_ref, w_ref, a_ref, b_ref, d_ref, dx_ref, dw_ref):
        dx, dw = _rms_bwd_math(x_ref[...], w_ref[...], a_ref[...] + b_ref[...])
        dx_ref[...] = d_ref[...] + dx

        @pl.when(pl.program_id(0) == 0)
        def _():
            dw_ref[...] = jnp.zeros_like(dw_ref)

        dw_ref[...] += dw

    row = pl.BlockSpec((T, D), lambda i: (i, 0))
    vec = pl.BlockSpec((1, D), lambda i: (0, 0))
    return _call(
        body, grid=(L // T,), in_specs=[row, vec, row, row, row], out_specs=[row, vec],
        out_shape=[_sds((L, D), F32), _sds((1, D), F32)],
        name="rms_bwd", sem=("arbitrary",), args=(x, w, dh_a, dh_b, dxn), jobs=jobs)


def loss_head(y, target):
    L, D = y.shape
    T = _pick(L, 256)

    def body(y_ref, t_ref, d_ref, s_ref):
        e = y_ref[...] - t_ref[...]
        d_ref[...] = e * (1.0 / D)

        @pl.when(pl.program_id(0) == 0)
        def _():
            s_ref[...] = jnp.zeros_like(s_ref)

        s_ref[...] += jnp.sum(e * e)

    row = pl.BlockSpec((T, D), lambda i: (i, 0))
    return pl.pallas_call(
        body, grid=(L // T,), in_specs=[row, row],
        out_specs=[row, pl.BlockSpec((8, LANES), lambda i: (0, 0))],
        out_shape=[_sds((L, D), F32), _sds((8, LANES), F32)],
        name="loss_head", compiler_params=_params("arbitrary"))(y, target)


def _window_sums(xe, w, back):
    n = xe.shape[0]
    s, k = xe, 1
    while k < w:
        s = s + pltpu.roll(s, k if back else n - k, 0)
        k *= 2
    return s


def pool_fwd(proj, mixw, scale, D):
    L = proj.shape[0]
    PGW = D // GROUPS
    T = _pick(L, 256)
    hb = T // POOL_HALO

    def body(u_ref, halo_ref, g_ref, mw_ref, sc_ref, y_ref, p_ref):
        i = pl.program_id(0)
        u = u_ref[...]
        halo = jnp.where(i > 0, halo_ref[...], 0.0)
        xe = jnp.concatenate([halo, u], axis=0)
        t1 = _row(i * T + 1, (T, 1))
        for g, w in enumerate(POOL_WINDOWS):
            sl = slice(g * PGW, (g + 1) * PGW)
            win = _window_sums(xe[:, sl], w, True)[POOL_HALO:, :]
            cnt = jnp.minimum(t1, w).astype(F32)
            pooled = (win / cnt - u[:, sl]).astype(BF16)
            p_ref[:, sl] = pooled
            mixed = _dot(pooled, mw_ref[g])
            gate = g_ref[:, sl]
            y_ref[:, sl] = (mixed * sc_ref[:, sl] * (gate * _sigmoid(gate))).astype(BF16)

    return pl.pallas_call(
        body, grid=(L // T,),
        in_specs=[pl.BlockSpec((T, D), lambda i: (i, 0)),
                  pl.BlockSpec((POOL_HALO, D), lambda i: (jnp.maximum(i * hb - 1, 0), 0)),
                  pl.BlockSpec((T, D), lambda i: (i, 1)),
                  pl.BlockSpec((GROUPS, PGW, PGW), lambda i: (0, 0, 0)),
                  pl.BlockSpec((1, D), lambda i: (0, 0))],
        out_specs=[pl.BlockSpec((T, D), lambda i: (i, 0)), pl.BlockSpec((T, D), lambda i: (i, 0))],
        out_shape=[_sds((L, 2 * D), BF16), _sds((L, D), BF16)],
        name="pool_fwd", compiler_params=_params("parallel"))(proj, proj, proj, mixw, scale)


def pool_bwd(proj, dmix, pooled, mixw, scale, dproj, D):
    L = proj.shape[0]
    PGW = D // GROUPS
    T = _pick(L, 256)
    hb = T // POOL_HALO
    nT = L // T

    def body(g_ref, gh_ref, dy_ref, dyh_ref, p_ref, mw_ref, sc_ref, old_ref, dp_ref, dm_ref, ds_ref):
        i = pl.program_id(0)
        t1 = _row(i * T + 1, (T, 1))
        th1 = _row((i + 1) * T + 1, (POOL_HALO, 1))
        live = i < nT - 1

        @pl.when(i == 0)
        def _():
            ds_ref[...] = jnp.zeros_like(ds_ref)

        for g, w in enumerate(POOL_WINDOWS):
            sl = slice(g * PGW, (g + 1) * PGW)
            sc = sc_ref[:, sl]
            gate, dy = g_ref[:, sl], dy_ref[:, sl]
            sg, dsg = _silu_and_grad(gate)
            mixed = _dot(p_ref[:, sl], mw_ref[g])
            dmixed = (dy * sc * sg).astype(BF16)
            dm_ref[:, sl] = dmixed
            dp_ref[:, D + g * PGW:D + (g + 1) * PGW] = (dy * mixed * sc * dsg).astype(BF16)
            ds_ref[:, sl] += jnp.sum(dy * mixed * sg, axis=0, keepdims=True)
            dpool = _dot(dmixed, mw_ref[g], NT)
            gate_h = gh_ref[:, sl]
            dmixed_h = (dyh_ref[:, sl] * sc * (gate_h * _sigmoid(gate_h))).astype(BF16)
            dpool_h = jnp.where(live, _dot(dmixed_h, mw_ref[g], NT), 0.0)
            q = dpool / jnp.minimum(t1, w).astype(F32)
            q_h = dpool_h / jnp.minimum(th1, w).astype(F32)
            qe = jnp.concatenate([q, q_h], axis=0)
            dp_ref[:, sl] = (_window_sums(qe, w, False)[:T, :] - dpool).astype(BF16)

    nxt = lambda i: jnp.minimum((i + 1) * hb, L // POOL_HALO - 1)
    row = lambda c: pl.BlockSpec((T, D), lambda i: (i, c))
    return pl.pallas_call(
        body, grid=(nT,),
        in_specs=[row(1), pl.BlockSpec((POOL_HALO, D), lambda i: (nxt(i), 1)),
                  row(0), pl.BlockSpec((POOL_HALO, D), lambda i: (nxt(i), 0)),
                  row(0), pl.BlockSpec((GROUPS, PGW, PGW), lambda i: (0, 0, 0)),
                  pl.BlockSpec((1, D), lambda i: (0, 0)), _ANY],
        out_specs=[pl.BlockSpec((T, 2 * D), lambda i: (i, 0)), row(0), pl.BlockSpec((1, D), lambda i: (0, 0))],
        out_shape=[_sds(dproj.shape, dproj.dtype), _sds((L, D), BF16), _sds((1, D), F32)],
        input_output_aliases={7: 0},
        name="pool_bwd", compiler_params=_params("arbitrary"))(proj, proj, dmix, dmix, pooled, mixw, scale, dproj)


def pool_dw(pooled, dmixed, D):
    L = pooled.shape[0]
    PGW = D // GROUPS
    tk = _pick(L, 1024)
    nk = L // tk

    def body(p_ref, d_ref, o_ref):
        @pl.when(pl.program_id(1) == 0)
        def _():
            o_ref[...] = jnp.zeros_like(o_ref)

        o_ref[0] += _dot(p_ref[...], d_ref[...], TN)

    blk = pl.BlockSpec((tk, PGW), lambda g, k: (k, g))
    return pl.pallas_call(
        body, grid=(GROUPS, nk), in_specs=[blk, blk],
        out_specs=pl.BlockSpec((1, PGW, PGW), lambda g, k: (g, 0, 0)),
        out_shape=_sds((GROUPS, PGW, PGW), F32), name="pool_dw",
        compiler_params=_params("parallel", "arbitrary"))(pooled, dmixed)


def conv_fwd(proj, cw, cb, D):
    L = proj.shape[0]
    C = cw.shape[1]
    assert (3 * D) % C == 0
    cblk = (3 * D) // C
    T = _pick(L, 256)
    hb = T // CONV_HALO

    def body(u_ref, halo_ref, w_ref, b_ref, o_ref):
        i = pl.program_id(0)
        u = u_ref[...]
        xe = jnp.concatenate([jnp.where(i > 0, halo_ref[...], 0.0), u], axis=0)
        acc = b_ref[...] + w_ref[CONV_K - 1:CONV_K, :] * u
        for k in range(CONV_K - 1):
            acc = acc + w_ref[k:k + 1, :] * pltpu.roll(xe, CONV_K - 1 - k, 0)[CONV_HALO:, :]
        o_ref[...] = acc

    return pl.pallas_call(
        body, grid=(L // T,),
        in_specs=[pl.BlockSpec((T, C), lambda i: (i, cblk)),
                  pl.BlockSpec((CONV_HALO, C), lambda i: (jnp.maximum(i * hb - 1, 0), cblk)),
                  pl.BlockSpec((CONV_K, C), lambda i: (0, 0)),
                  pl.BlockSpec((1, C), lambda i: (0, 0))],
        out_specs=pl.BlockSpec((T, C), lambda i: (i, 0)),
        out_shape=_sds((L, C), F32), name="conv_fwd", compiler_params=_params("parallel"))(proj, proj, cw, cb)


def conv_bwd(dparts, proj, cw, dproj, D, jobs=()):
    L = proj.shape[0]
    C = cw.shape[1]
    cblk = (3 * D) // C
    T = _pick(L, 256)
    hb = T // CONV_HALO
    nT = L // T
    widths = [p.shape[1] for p in dparts]
    assert sum(widths) == C
    n = len(dparts)

    def body(*refs):
        d_refs, dn_refs = refs[:n], refs[n:2 * n]
        u_ref, up_ref, w_ref, old_ref, dr_ref, dw_ref, db_ref = refs[2 * n:]
        i = pl.program_id(0)

        @pl.when(i == 0)
        def _():
            dw_ref[...] = jnp.zeros_like(dw_ref)
            db_ref[...] = jnp.zeros_like(db_ref)

        at = 0
        for d_ref, dn_ref, wd in zip(d_refs, dn_refs, widths):
            sl = slice(at, at + wd)
            at += wd
            d = d_ref[...]
            u = u_ref[:, sl]
            de = jnp.concatenate([d, jnp.where(i < nT - 1, dn_ref[...], 0.0)], axis=0)
            ue = jnp.concatenate([jnp.where(i > 0, up_ref[:, sl], 0.0), u], axis=0)
            acc = w_ref[CONV_K - 1:CONV_K, sl] * d
            dw_ref[CONV_K - 1:CONV_K, sl] += jnp.sum(d * u, axis=0, keepdims=True)
            for k in range(CONV_K - 1):
                sh = CONV_K - 1 - k
                acc = acc + w_ref[k:k + 1, sl] * pltpu.roll(de, T + CONV_HALO - sh, 0)[:T, :]
                dw_ref[k:k + 1, sl] += jnp.sum(d * pltpu.roll(ue, sh, 0)[CONV_HALO:, :], axis=0, keepdims=True)
            dr_ref[:, sl] = acc.astype(dr_ref.dtype)
            db_ref[:, sl] += jnp.sum(d, axis=0, keepdims=True)

    nxt = lambda i: jnp.minimum((i + 1) * hb, L // CONV_HALO - 1)
    return _call(
        body, grid=(nT,),
        in_specs=[pl.BlockSpec((T, wd), lambda i: (i, 0)) for wd in widths]
        + [pl.BlockSpec((CONV_HALO, wd), lambda i: (nxt(i), 0)) for wd in widths]
        + [pl.BlockSpec((T, C), lambda i: (i, cblk)),
           pl.BlockSpec((CONV_HALO, C), lambda i: (jnp.maximum(i * hb - 1, 0), cblk)),
           pl.BlockSpec((CONV_K, C), lambda i: (0, 0)), _ANY],
        out_specs=[pl.BlockSpec((T, C), lambda i: (i, cblk)),
                   pl.BlockSpec((8, C), lambda i: (0, 0)),
                   pl.BlockSpec((1, C), lambda i: (0, 0))],
        out_shape=[_sds(dproj.shape, dproj.dtype), _sds((8, C), F32), _sds((1, C), F32)],
        aliases={2 * n + 3: 0}, name="conv_bwd", sem=("arbitrary",),
        args=(*dparts, *dparts, proj, proj, cw, dproj), jobs=jobs)


def _softplus(v):
    y = jnp.exp(-jnp.abs(v))
    u = 1.0 + y
    log1p = jnp.where(u == 1.0, y, jnp.log(u) * y / jnp.where(u == 1.0, 1.0, u - 1.0))
    return jnp.maximum(v, 0.0) + log1p


def dt_prep(dtraw, bias, alog, expand, D):
    L = dtraw.shape[0]
    GC = D // GROUPS
    HPG = GC // HEAD_DIM
    Q = _pick(L, SCAN_CHUNK)
    nc = L // Q

    def body(r_ref, b_ref, a_ref, e_ref, dt_ref, acs_ref, acst_ref, dtx_ref, eax_ref, dsx_ref, cdx_ref):
        valid = lax.broadcasted_iota(jnp.int32, (1, LANES), 1) < HPG
        dt = jnp.where(valid, _softplus(r_ref[...] + b_ref[...]), 0.0)
        adt = dt * -jnp.exp(a_ref[...])
        tril = (_row(0, (Q, Q)) >= lax.broadcasted_iota(jnp.int32, (Q, Q), 1)).astype(BF16)
        acs = _sel_dot(tril, adt)
        last = acs[Q - 1:Q, :]
        dt_ref[...] = dt
        acs_ref[...] = acs
        acst_ref[...] = acs.T
        e = e_ref[...]
        dtx_ref[...] = _dot_sel(dt, e)
        eax_ref[...] = jnp.exp(_dot_sel(acs, e))
        dsx_ref[...] = jnp.exp(_dot_sel(last - acs, e))
        cdx_ref[0] = jnp.exp(_dot_sel(jnp.broadcast_to(last, (8, LANES)), e))

    head = pl.BlockSpec((Q, LANES), lambda g, c: (c, g))
    hvec = pl.BlockSpec((1, LANES), lambda g, c: (0, g))
    chan = pl.BlockSpec((Q, GC), lambda g, c: (c, g))
    return pl.pallas_call(
        body, grid=(GROUPS, nc),
        in_specs=[head, hvec, hvec, pl.BlockSpec((LANES, GC), lambda g, c: (0, 0))],
        out_specs=[head, head, pl.BlockSpec((LANES, Q), lambda g, c: (g, c)), chan, chan, chan,
                   pl.BlockSpec((1, 8, GC), lambda g, c: (c, 0, g))],
        out_shape=[_sds((L, GROUPS * LANES), F32), _sds((L, GROUPS * LANES), F32), _sds((GROUPS * LANES, L), F32),
                   _sds((L, D), F32), _sds((L, D), F32), _sds((L, D), F32), _sds((nc, 8, D), F32)],
        name="dt_prep", compiler_params=_params("parallel", "parallel"))(dtraw, bias, alog, expand)


def _scan_specs(L, D, Q, rev):
    GC = D // GROUPS
    nc = L // Q
    ci = (lambda c: nc - 1 - c) if rev else (lambda c: c)
    return dict(
        xs=pl.BlockSpec((Q, GC), lambda g, c: (ci(c), g)),
        b=pl.BlockSpec((Q, STATE), lambda g, c: (ci(c), D // STATE + g)),
        c=pl.BlockSpec((Q, STATE), lambda g, c: (ci(c), D // STATE + GROUPS + g)),
        chan=pl.BlockSpec((Q, GC), lambda g, c: (ci(c), g)),
        cdx=pl.BlockSpec((1, 8, GC), lambda g, c: (ci(c), 0, g)),
        head=pl.BlockSpec((Q, LANES), lambda g, c: (ci(c), g)),
        headt=pl.BlockSpec((LANES, Q), lambda g, c: (g, ci(c))),
        state=pl.BlockSpec((1, 1, STATE, GC), lambda g, c: (ci(c), g, 0, 0)),
        hvec=pl.BlockSpec((1, LANES), lambda g, c: (0, g)),
        cvec=pl.BlockSpec((1, GC), lambda g, c: (0, g)))


def scan_fwd(pre, dtx, eax, dsx, cdx, acs, acst, D):
    L = pre.shape[0]
    GC = D // GROUPS
    Q = _pick(L, SCAN_CHUNK)
    nc = L // Q
    sp = _scan_specs(L, D, Q, False)

    def body(xs_ref, b_ref, c_ref, dtx_ref, eax_ref, dsx_ref, cdx_ref, acs_ref, acst_ref, y_ref, st_ref, s_scr):
        @pl.when(pl.program_id(1) == 0)
        def _():
            s_scr[...] = jnp.zeros_like(s_scr)

        tri = _row(0, (Q, Q)) >= lax.broadcasted_iota(jnp.int32, (Q, Q), 1)
        half = lax.broadcasted_iota(jnp.int32, (1, LANES), 1) // HEAD_DIM
        xs, _ = _silu_and_grad(xs_ref[...])
        bg = _silu_and_grad(b_ref[...])[0].astype(BF16)
        cg = _silu_and_grad(c_ref[...])[0].astype(BF16)
        xdt = xs * dtx_ref[...]
        sprev = s_scr[...]
        st_ref[0, 0] = sprev
        sc = _dot(cg, bg, NT)
        yoff = _dot(cg, sprev.astype(BF16)) * eax_ref[...]
        for j in range(GC // LANES):
            ps = slice(j * LANES, (j + 1) * LANES)
            xp = xdt[:, ps]
            acc = yoff[:, ps]
            for hh in range(2):
                h = 2 * j + hh
                lm = jnp.exp(jnp.where(tri, acs_ref[:, h:h + 1] - acst_ref[h:h + 1, :], -1e30))
                xm = jnp.where(half == hh, xp, 0.0).astype(BF16)
                acc = acc + _dot((sc * lm).astype(BF16), xm)
            y_ref[:, ps] = acc
        xw = (xdt * dsx_ref[...]).astype(BF16)
        s_scr[...] = cdx_ref[0, 0:1, :] * sprev + _dot(bg, xw, TN)

    return pl.pallas_call(
        body, grid=(GROUPS, nc),
        in_specs=[sp["xs"], sp["b"], sp["c"], sp["chan"], sp["chan"], sp["chan"], sp["cdx"], sp["head"], sp["headt"]],
        out_specs=[sp["chan"], sp["state"]],
        out_shape=[_sds((L, D), F32), _sds((nc, GROUPS, STATE, GC), F32)],
        scratch_shapes=[pltpu.VMEM((STATE, GC), F32)], name="scan_fwd",
        compiler_params=_params("parallel", "arbitrary"))(pre, pre, pre, dtx, eax, dsx, cdx, acs, acst)


def scan_bwd(pre, dtx, eax, dsx, cdx, acs, acst, dt, dtraw, bias, alog, states, dy, dexp, collapse, D, jobs=()):
    L = pre.shape[0]
    GC = D // GROUPS
    Q = _pick(L, SCAN_CHUNK)
    nc = L // Q
    sp = _scan_specs(L, D, Q, True)
    rc = lambda c: nc - 1 - c

    def body(xs_ref, b_ref, c_ref, dtx_ref, eax_ref, dsx_ref, cdx_ref, acs_ref, acst_ref, dt_ref, raw_ref,
             bias_ref, alog_ref, st_ref, dy_ref, dexp_ref, col_ref,
             dxs_ref, db_ref, dc_ref, ddt_ref, dal_ref, dbi_ref, ds_scr, dx_scr):
        first = pl.program_id(1) == 0

        @pl.when(first)
        def _():
            ds_scr[...] = jnp.zeros_like(ds_scr)
            dal_ref[...] = jnp.zeros_like(dal_ref)
            dbi_ref[...] = jnp.zeros_like(dbi_ref)

        li = _row(0, (Q, Q))
        si = lax.broadcasted_iota(jnp.int32, (Q, Q), 1)
        lane = lax.broadcasted_iota(jnp.int32, (1, LANES), 1)
        half = lane // HEAD_DIM
        xs_pre, b_pre, c_pre = xs_ref[...], b_ref[...], c_ref[...]
        xs, xs_g = _silu_and_grad(xs_pre)
        bf, b_g = _silu_and_grad(b_pre)
        cf, c_g = _silu_and_grad(c_pre)
        bg, cg = bf.astype(BF16), cf.astype(BF16)
        dtx, eax, dsx = dtx_ref[...], eax_ref[...], dsx_ref[...]
        cd = cdx_ref[0, 0:1, :]
        xdt = xs * dtx
        G = dy_ref[...]
        prev = st_ref[0, 0]
        dsn = ds_scr[...]
        prev_b, dsn_b = prev.astype(BF16), dsn.astype(BF16)
        cp = _dot(cg, prev_b)
        ge_b = (G * eax).astype(BF16)
        d_c = _dot(ge_b, prev_b, NT)
        dprev = _dot(cg, ge_b, TN) + cd * dsn
        chan_a = G * cp * eax
        xw_b = (xdt * dsx).astype(BF16)
        dcd = jnp.sum(prev * dsn, axis=0, keepdims=True)
        d_b = _dot(xw_b, dsn_b, NT)
        dxw = _dot(bg, dsn_b)
        dd = dxw * xdt * dsx
        chan_a = chan_a - dd
        last_c = jnp.sum(dd, axis=0, keepdims=True) + dcd * cd
        sc = _dot(cg, bg, NT)
        head_row = _row(0, (LANES, 1))
        dsc = jnp.zeros((Q, Q), F32)
        dacs = jnp.zeros((Q, LANES), F32)
        colsums = jnp.zeros((LANES, Q), F32)
        for j in range(GC // LANES):
            ps = slice(j * LANES, (j + 1) * LANES)
            xp, gp = xdt[:, ps], G[:, ps]
            dxp = dxw[:, ps] * dsx[:, ps]
            for hh in range(2):
                h = 2 * j + hh
                lm = jnp.exp(jnp.where(li >= si, acs_ref[:, h:h + 1] - acst_ref[h:h + 1, :], -1e30))
                m = sc * lm
                xm = jnp.where(half == hh, xp, 0.0).astype(BF16)
                gm = jnp.where(half == hh, gp, 0.0).astype(BF16)
                dm = _dot(gm, xm, NT)
                dxp = dxp + _dot(m.astype(BF16), gm, TN)
                dsc = dsc + dm * lm
                w = dm * m
                dacs = dacs + jnp.where(lane == h, jnp.sum(w, axis=1, keepdims=True), 0.0)
                colsums = jnp.where(head_row == h, jnp.sum(w, axis=0, keepdims=True), colsums)
            dx_scr[:, ps] = dxp
        dacs = dacs - colsums.T
        dsc_b = dsc.astype(BF16)
        d_c = d_c + _dot(dsc_b, bg)
        d_b = d_b + _dot(dsc_b, cg, TN)
        ds_scr[...] = dprev
        dxdt = dx_scr[...]
        dxs_ref[...] = (dxdt * dtx + dexp_ref[...] * G) * xs_g
        db_ref[...] = d_b * b_g
        dc_ref[...] = d_c * c_g
        colm = col_ref[...]
        dacs = dacs + _dot_sel(chan_a, colm)
        dlast = _dot_sel(jnp.broadcast_to(last_c, (8, GC)), colm)[0:1, :]
        dacs = dacs + jnp.where(_row(0, (Q, 1)) == Q - 1, dlast, 0.0)
        dadt = _sel_dot((si >= li).astype(BF16), dacs)
        a = -jnp.exp(alog_ref[...])
        dt = dt_ref[...]
        ddt = dadt * a + _dot_sel(dxdt * xs, colm)
        dal_ref[0:1, :] += jnp.sum(dadt * dt * a, axis=0, keepdims=True)
        draw = ddt * _sigmoid(raw_ref[...] + bias_ref[...])
        dbi_ref[0:1, :] += jnp.sum(draw, axis=0, keepdims=True)
        ddt_ref[...] = draw.astype(ddt_ref.dtype)

    acc = pl.BlockSpec((8, LANES), lambda g, c: (0, g))
    return _call(
        body, grid=(GROUPS, nc),
        in_specs=[sp["xs"], sp["b"], sp["c"], sp["chan"], sp["chan"], sp["chan"], sp["cdx"], sp["head"], sp["headt"],
                  sp["head"], sp["head"], sp["hvec"], sp["hvec"], sp["state"], sp["chan"], sp["cvec"],
                  pl.BlockSpec((GC, LANES), lambda g, c: (0, 0))],
        out_specs=[sp["chan"],
                   pl.BlockSpec((Q, STATE), lambda g, c: (rc(c), g)),
                   pl.BlockSpec((Q, STATE), lambda g, c: (rc(c), g)),
                   sp["head"], acc, acc],
        out_shape=[_sds((L, D), F32), _sds((L, GROUPS * STATE), F32), _sds((L, GROUPS * STATE), F32),
                   _sds((L, GROUPS * LANES), BF16), _sds((8, GROUPS * LANES), F32), _sds((8, GROUPS * LANES), F32)],
        scratch_shapes=[pltpu.VMEM((STATE, GC), F32), pltpu.VMEM((Q, GC), F32)], name="scan_bwd",
        sem=("parallel", "arbitrary"),
        args=(pre, pre, pre, dtx, eax, dsx, cdx, acs, acst, dt, dtraw, bias, alog, states, dy, dexp, collapse),
        jobs=jobs)


def gate_fwd(y, pre, proj, dexp, nw, mixed, D):
    L = y.shape[0]
    GC = D // GROUPS
    T = _pick(L, 256)

    def body(y_ref, xs_ref, z_ref, de_ref, nw_ref, old_ref, o_ref):
        xs, _ = _silu_and_grad(xs_ref[...])
        z = z_ref[...]
        y3 = (y_ref[...] + de_ref[...] * xs) * (z * _sigmoid(z))
        for g in range(GROUPS):
            sl = slice(g * GC, (g + 1) * GC)
            yg = y3[:, sl]
            r = lax.rsqrt(jnp.mean(yg * yg, axis=-1, keepdims=True) + NORM_EPS)
            o_ref[:, sl] = (yg * r * nw_ref[:, sl]).astype(o_ref.dtype)

    row = lambda c: pl.BlockSpec((T, D), lambda i: (i, c))
    vec = pl.BlockSpec((1, D), lambda i: (0, 0))
    return pl.pallas_call(
        body, grid=(L // T,), in_specs=[row(0), row(0), row(2), vec, vec, _ANY], out_specs=row(1),
        out_shape=_sds(mixed.shape, mixed.dtype), input_output_aliases={5: 0}, name="gate_fwd",
        compiler_params=_params("parallel"))(y, pre, proj, dexp, nw, mixed)


def gate_bwd(y, pre, proj, dexp, nw, dmix, D, jobs=()):
    L = y.shape[0]
    GC = D // GROUPS
    T = _pick(L, 256)

    def body(y_ref, xs_ref, z_ref, de_ref, nw_ref, dm_ref, dy_ref, dz_ref, dnw_ref, dde_ref):
        @pl.when(pl.program_id(0) == 0)
        def _():
            dnw_ref[...] = jnp.zeros_like(dnw_ref)
            dde_ref[...] = jnp.zeros_like(dde_ref)

        xs, _ = _silu_and_grad(xs_ref[...])
        sz, dsz = _silu_and_grad(z_ref[...])
        y2 = y_ref[...] + de_ref[...] * xs
        y3 = y2 * sz
        for g in range(GROUPS):
            sl = slice(g * GC, (g + 1) * GC)
            yg, dm = y3[:, sl], dm_ref[:, sl]
            r = lax.rsqrt(jnp.mean(yg * yg, axis=-1, keepdims=True) + NORM_EPS)
            n = yg * r
            gg = dm * nw_ref[:, sl]
            dy3 = r * (gg - n * jnp.mean(gg * n, axis=-1, keepdims=True))
            dnw_ref[:, sl] += jnp.sum(dm * n, axis=0, keepdims=True)
            dy2 = dy3 * sz[:, sl]
            dy_ref[:, sl] = dy2
            dz_ref[:, sl] = (dy3 * y2[:, sl] * dsz[:, sl]).astype(dz_ref.dtype)
            dde_ref[:, sl] += jnp.sum(dy2 * xs[:, sl], axis=0, keepdims=True)

    row = lambda c: pl.BlockSpec((T, D), lambda i: (i, c))
    vec = pl.BlockSpec((1, D), lambda i: (0, 0))
    return _call(
        body, grid=(L // T,), in_specs=[row(0), row(0), row(2), vec, vec, row(1)],
        out_specs=[row(0), row(2), vec, vec],
        out_shape=[_sds((L, D), F32), _sds((L, proj.shape[1]), BF16), _sds((1, D), F32), _sds((1, D), F32)],
        name="gate_bwd", sem=("arbitrary",), args=(y, pre, proj, dexp, nw, dmix), jobs=jobs)


def _adam_math(gv, w, m, v):
    c1 = 1.0 - ADAM_B1 ** ADAM_STEP
    c2 = 1.0 - ADAM_B2 ** ADAM_STEP
    nm = ADAM_B1 * m + (1.0 - ADAM_B1) * gv
    nv = ADAM_B2 * v + (1.0 - ADAM_B2) * (gv * gv)
    return -ADAM_LR * ((nm / c1) / (jnp.sqrt(nv / c2) + ADAM_EPS) + ADAM_WD * w), nm, nv


def adamw(g, w, m, v, name):
    R, C = g.shape
    T = R if R <= 128 else 128
    assert R % T == 0

    def body(g_ref, w_ref, m_ref, v_ref, d_ref, nm_ref, nv_ref):
        d_ref[...], nm_ref[...], nv_ref[...] = _adam_math(g_ref[...], w_ref[...], m_ref[...], v_ref[...])

    blk = pl.BlockSpec((T, C), lambda i: (i, 0))
    return pl.pallas_call(
        body, grid=(R // T,), in_specs=[blk] * 4, out_specs=[blk] * 3,
        out_shape=[_sds((R, C), F32)] * 3, name=name, compiler_params=_params("parallel"))(g, w, m, v)


def adamw_layer(g, w, m, v, layer, prev, name):
    R, C = g.shape
    T = _rows_tile(R, 256)
    assert R % T == 0

    def body(g_ref, w_ref, m_ref, v_ref, *rest):
        go_ref, d_ref, nm_ref, nv_ref = rest[-4:]
        gv = g_ref[...]
        go_ref[0] = gv
        d_ref[0], nm_ref[0], nv_ref[0] = _adam_math(gv, w_ref[0], m_ref[0], v_ref[0])

    mine = pl.BlockSpec((1, T, C), lambda i: (layer, i, 0))
    prev = list(prev or [])
    return pl.pallas_call(
        body, grid=(R // T,), in_specs=[pl.BlockSpec((T, C), lambda i: (i, 0)), mine, mine, mine] + [_ANY] * len(prev),
        out_specs=[mine] * 4, out_shape=[_sds(w.shape, F32)] * 4,
        input_output_aliases={4 + i: i for i in range(len(prev))}, name=name,
        compiler_params=_params("parallel"))(g, w, m, v, *prev)


def cast_bf16(a):
    R, C = a.shape
    T = _rows_tile(R, 256)

    def body(a_ref, o_ref):
        o_ref[...] = a_ref[...].astype(BF16)

    blk = pl.BlockSpec((T, C), lambda i: (i, 0))
    return pl.pallas_call(body, grid=(R // T,), in_specs=[blk], out_specs=blk, out_shape=_sds((R, C), BF16),
                          name="cast_bf16", compiler_params=_params("parallel"))(a)


def _to_groups(v, hpg):
    lead = v.shape[:-1]
    t = v.reshape(lead + (GROUPS, hpg))
    t = jnp.pad(t, [(0, 0)] * (len(lead) + 1) + [(0, LANES - hpg)])
    return t.reshape(lead + (GROUPS * LANES,))


def _from_groups(a, hpg):
    lead = a.shape[:-1]
    return a.reshape(lead + (GROUPS, LANES))[..., :hpg].reshape(lead + (GROUPS * hpg,))


def _expand_matrix(D):
    gc = D // GROUPS
    return (jnp.arange(LANES)[:, None] == (jnp.arange(gc)[None, :] // HEAD_DIM)).astype(BF16)


def layer_params(pre_w, w_in_t, mixw, scale, cw, cb, bias, alog, dskip, nw, w_out_full, post_w, D):
    hpg = D // GROUPS // HEAD_DIM
    main = w_in_t.shape[0] - GROUPS * hpg
    wdt_t = _to_groups(w_in_t[main:].T, hpg).T
    return dict(
        pre_w=pre_w[None], win_t=w_in_t, main=main, wdt_t=wdt_t, mixw=mixw,
        scale=scale[None], cw=cw, cb=cb[None], bias=_to_groups(bias, hpg)[None], alog=_to_groups(alog, hpg)[None],
        dexp=jnp.repeat(dskip, HEAD_DIM)[None], nw=nw[None], wout=w_out_full, post_w=post_w[None])


def layer_fwd(x, p, D, next_shards=None):
    send_in = [gather_send_job(next_shards[:1], [True])] if next_shards else []
    h = rms_fwd(x, p["pre_w"])
    proj, got = matmul(h, p["win_t"], "nt", F32, "proj", jobs=send_in, n_out=p["main"])
    dtraw, _ = matmul(h, p["wdt_t"], "nt", F32, "dtproj")
    mixed, pooled = pool_fwd(proj, p["mixw"], p["scale"], D)
    pre = conv_fwd(proj, p["cw"], p["cb"], D)
    dtp = dt_prep(dtraw, p["bias"], p["alog"], _expand_matrix(D), D)
    dt, acs, acst, dtx, eax, dsx, cdx = dtp
    y, states = scan_fwd(pre, dtx, eax, dsx, cdx, acs, acst, D)
    mixed = gate_fwd(y, pre, proj, p["dexp"], p["nw"], mixed, D)
    jobs = [gather_send_job(next_shards[1:]), gather_pass_job(got[0], [True])] if next_shards else []
    out, got = matmul(mixed, p["wout"], "nn", F32, "outproj", jobs=jobs)
    xn, got2 = post_fwd(x, out, p["post_w"], jobs=[gather_pass_job(got[0])] if next_shards else [])
    gathered = got[1] + got2[0] if next_shards else None
    return xn, dict(x=x, h=h, proj=proj, dtraw=dtraw, pooled=pooled, pre=pre, dtp=dtp, y=y, states=states,
                    mixed=mixed, out=out), gathered


def layer_bwd(dxn, p, s, D, where=None):
    reduce = where is not None
    chip, core = where if reduce else (None, None)
    hpg = D // GROUPS // HEAD_DIM
    PGW = D // GROUPS
    main = p["main"]
    SH = (main + GROUPS * hpg) // 4
    dt, acs, acst, dtx, eax, dsx, cdx = s["dtp"]
    dout, d_post = post_bwd(s["out"], p["post_w"], dxn)
    dmix, _ = matmul(dout, p["wout"], "nt", F32, "dmixed")
    part = BF16 if reduce else F32
    d_wout, _ = matmul(s["mixed"], dout, "tn", part, "dwout")
    g_out = d_wout.reshape(4, 2, D // 4, D)
    (dy2, dproj, d_nw, d_dexp), got = gate_bwd(s["y"], s["pre"], s["proj"], p["dexp"], p["nw"], dmix, D,
                                               jobs=[pair_exchange_job([g_out])] if reduce else [])
    pair_out = pair_add(g_out, got[0][0], core, BF16) if reduce else None
    (dxs, db, dc, ddtraw, d_alog, d_bias), got = scan_bwd(
        s["pre"], dtx, eax, dsx, cdx, acs, acst, dt, s["dtraw"], p["bias"], p["alog"], s["states"], dy2,
        p["dexp"], _expand_matrix(D).T, D, jobs=[chip_exchange_job([pair_out])] if reduce else [])
    mine_out = chip_add(pair_out, got[0][0], chip, core) if reduce else None
    (dproj, d_cw, d_cb), got = conv_bwd([dxs, db, dc], s["proj"], p["cw"], dproj, D,
                                        jobs=[pair_gather_job([mine_out])] if reduce else [])
    r_out = got[0][0] if reduce else None
    dproj, dmixed, d_scale = pool_bwd(s["proj"], dmix, s["pooled"], p["mixw"], p["scale"], dproj, D)
    d_mixw = pool_dw(s["pooled"], dmixed, D)
    d_wmain_t, _ = matmul(dproj, s["h"], "tn", part, "dwmain")
    d_wdt_t, _ = matmul(ddtraw, s["h"], "tn", part, "dwdt")
    late = [d_wmain_t[None], d_wdt_t[None],
            d_mixw.reshape(GROUPS, 4, PGW // 4, PGW).transpose(1, 0, 2, 3).reshape(4, 2, GROUPS * PGW // 8, PGW)]
    cols = [True, True, False]
    dh_b, got = matmul(ddtraw, p["wdt_t"], "nn", F32, "dh_dt",
                       jobs=[pair_exchange_job(late, cols)] if reduce else [])
    if reduce:
        p_main, p_dt, p_mix = [pair_add(g, r, core, BF16, k) for g, r, k in zip(late, got[0], cols)]
        p_in = jnp.concatenate([p_main[0], _from_groups(p_dt[0].T, hpg).T], axis=0)
        pairs = [p_in.reshape(4, SH, D // 2), p_mix]
    dh_a, got = matmul(dproj, p["win_t"], "nn", F32, "dh_main", jobs=[chip_exchange_job(pairs)] if reduce else [])
    mines = [chip_add(q, r, chip, core, k) for q, r, k in zip(pairs, got[0], [True, False])] if reduce else None
    (dx, d_pre), got = rms_bwd(s["x"], p["pre_w"], dh_a, dh_b, dxn,
                               jobs=[pair_gather_job(mines, [True, False])] if reduce else [])
    reduced = dict(w_in=got[0][0], w_out=r_out, pool_mix_w=got[0][1]) if reduce else None
    grads = dict(
        pre_norm_w=d_pre[0], pool_scale=d_scale[0], conv_w=d_cw[:CONV_K], conv_b=d_cb[0],
        dt_bias=_from_groups(d_bias[0], hpg), a_log=_from_groups(d_alog[0], hpg),
        d_skip=d_dexp[0].reshape(-1, HEAD_DIM).sum(axis=-1), ssd_norm_w=d_nw[0], post_norm_w=d_post[0])
    if not reduce:
        grads.update(w_in=jnp.concatenate([d_wmain_t.T, _from_groups(d_wdt_t.T, hpg)], axis=1), pool_mix_w=d_mixw,
                     w_out=d_wout)
    return dx, grads, reduced


def local_step(x, target, params, D):
    saved = []
    for p in params:
        x, s, _ = layer_fwd(x, p, D)
        saved.append(s)
    dx, sumsq = loss_head(x, target)
    grads = [None] * len(params)
    for l in reversed(range(len(params))):
        dx, grads[l], _ = layer_bwd(dx, params[l], saved[l], D)
    return sumsq, dx, grads


SMALL = ("pre_norm_w", "pool_scale", "conv_w", "conv_b", "dt_bias", "a_log", "d_skip", "ssd_norm_w", "post_norm_w")
BIG = ("w_in", "w_out", "pool_mix_w")


def _pack(parts):
    flat = jnp.concatenate([p.reshape(-1) for p in parts])
    n = flat.shape[0]
    rows = -(-n // (LANES * LANES)) * LANES
    return jnp.pad(flat, (0, rows * LANES - n)).reshape(rows, LANES)


def _unpack(packed, shapes):
    flat, out, at = packed.reshape(-1), [], 0
    for s in shapes:
        n = math.prod(s)
        out.append(flat[at:at + n].reshape(s))
        at += n
    return out


def kernel(x, pre_norm_w, w_in, pool_mix_w, pool_scale, conv_w, conv_b, dt_bias, a_log, d_skip, ssd_norm_w, w_out, post_norm_w, loss_target, m_pre_norm_w, m_w_in, m_pool_mix_w, m_pool_scale, m_conv_w, m_conv_b, m_dt_bias, m_a_log, m_d_skip, m_ssd_norm_w, m_w_out, m_post_norm_w, v_pre_norm_w, v_w_in, v_pool_mix_w, v_pool_scale, v_conv_w, v_conv_b, v_dt_bias, v_a_log, v_d_skip, v_ssd_norm_w, v_w_out, v_post_norm_w):
    NL, D, SH = w_in.shape
    PGW = D // GROUPS
    CS = conv_w.shape[2]
    chip = (2 * lax.axis_index("x") + lax.axis_index("y")).astype(jnp.int32)
    chip1, core = chip.reshape(1), lax.axis_index("c").astype(jnp.int32).reshape(1)

    tr = lambda t: jnp.transpose(t, (0, 2, 1))
    w_in_t, m_w_in_t, v_w_in_t = tr(w_in), tr(m_w_in), tr(v_w_in)
    halved_by_cols = [True, False, False, False]

    def shards(l):
        return [cast_bf16(w_in_t[l]), cast_bf16(w_out[l]).reshape(2, D // 4, D),
                cast_bf16(pool_mix_w[l].reshape(GROUPS * PGW // 4, PGW)).reshape(2, GROUPS * PGW // 8, PGW),
                conv_w[l].reshape(2, CONV_K * CS // (2 * LANES), LANES)]

    def params(l, g):
        g_in, g_out, g_mix, g_cw = g
        mix_l = g_mix.reshape(4, GROUPS, PGW // 4, PGW).transpose(1, 0, 2, 3).reshape(GROUPS, PGW, PGW)
        cw_l = g_cw.reshape(4, CONV_K, CS).transpose(1, 0, 2).reshape(CONV_K, 4 * CS)
        return layer_params(pre_norm_w[l], g_in.reshape(4 * SH, D), mix_l, pool_scale[l], cw_l, conv_b[l], dt_bias[l],
                            a_log[l], d_skip[l], ssd_norm_w[l], g_out.reshape(2 * D, D), post_norm_w[l], D)

    gathered = run_jobs([gather_send_job(shards(0), halved_by_cols)], "gather_send")[0]
    gathered = run_jobs([gather_pass_job(gathered, halved_by_cols)], "gather_pass")[0]
    h, ps, saved = x[0], [], []
    for l in range(NL):
        ps.append(params(l, gathered))
        h, s, gathered = layer_fwd(h, ps[l], D, shards(l + 1) if l + 1 < NL else None)
        saved.append(s)
    dx, sumsq = loss_head(h, loss_target[0])

    given = dict(w_in=(w_in_t, m_w_in_t, v_w_in_t), w_out=(w_out, m_w_out, v_w_out),
                 pool_mix_w=(pool_mix_w, m_pool_mix_w, v_pool_mix_w))
    flat = {n: [t.reshape(NL, -1, t.shape[-1]) for t in given[n]] for n in BIG}
    done = {n: None for n in BIG}
    grads = [None] * NL
    for l in reversed(range(NL)):
        dx, grads[l], reduced = layer_bwd(dx, ps[l], saved[l], D, (chip1, core))
        for n in BIG:
            r = reduced[n]
            done[n] = adamw_layer(r.reshape(-1, r.shape[-1]), *flat[n], l, done[n], "adamw_" + n)

    small_shapes = [(NL,) + grads[0][n].shape for n in SMALL]
    packed = _pack([0.5 / D * sumsq[0, :1]] + [jnp.stack([g[n] for g in grads]) for n in SMALL])
    total = allreduce_small(packed)
    loss, *small = _unpack(total, [(1,)] + small_shapes)
    small = dict(zip(SMALL, small))
    small["conv_w"] = lax.dynamic_slice_in_dim(small["conv_w"], chip * CS, CS, axis=2)

    given_small = dict(
        pre_norm_w=(pre_norm_w, m_pre_norm_w, v_pre_norm_w), pool_scale=(pool_scale, m_pool_scale, v_pool_scale),
        conv_w=(conv_w, m_conv_w, v_conv_w), conv_b=(conv_b, m_conv_b, v_conv_b),
        dt_bias=(dt_bias, m_dt_bias, v_dt_bias), a_log=(a_log, m_a_log, v_a_log),
        d_skip=(d_skip, m_d_skip, v_d_skip), ssd_norm_w=(ssd_norm_w, m_ssd_norm_w, v_ssd_norm_w),
        post_norm_w=(post_norm_w, m_post_norm_w, v_post_norm_w))
    shapes = [given_small[n][0].shape for n in SMALL]
    upd = adamw(_pack([small[n] for n in SMALL]), *[_pack([given_small[n][i] for n in SMALL]) for i in range(3)],
                "adamw_small")
    upd = [dict(zip(SMALL, _unpack(u, shapes))) for u in upd]

    out = {n: (small[n], upd[0][n], upd[1][n], upd[2][n]) for n in SMALL}
    for n in BIG:
        out[n] = tuple(t.reshape(given[n][0].shape) for t in done[n])
    out["w_in"] = tuple(tr(t) for t in out["w_in"])

    order = ("pre_norm_w", "w_in", "pool_mix_w", "pool_scale", "conv_w", "conv_b", "dt_bias", "a_log", "d_skip",
             "ssd_norm_w", "w_out", "post_norm_w")
    return (loss.reshape(()), dx[None], *[out[n][0] for n in order], *[out[n][1] for n in order],
            *[out[n][2] for n in order], *[out[n][3] for n in order])
```

```python
import functools
import math

import jax
import jax.numpy as jnp
from jax import lax
from jax.experimental import pallas as pl
from jax.experimental.pallas import tpu as pltpu

F32 = jnp.float32
BF16 = jnp.bfloat16

NORM_EPS = 1e-6
HEAD_DIM = 64
STATE = 128
GROUPS = 4
POOL_WINDOWS = (2, 4, 8, 16)
POOL_HALO = 16
CONV_K = 4
CONV_HALO = 8
SCAN_CHUNK = 256
LANES = 128
VMEM_LIMIT = 52 * 1024 * 1024

ADAM_LR = 0.001
ADAM_B1 = 0.9
ADAM_B2 = 0.999
ADAM_EPS = 1e-08
ADAM_WD = 0.01
ADAM_STEP = 10

MESH = pl.DeviceIdType.MESH

NN = (((1,), (0,)), ((), ()))
NT = (((1,), (1,)), ((), ()))
TN = (((0,), (0,)), ((), ()))

_ANY = pl.BlockSpec(memory_space=pl.ANY)


def _params(*sem):
    return pltpu.CompilerParams(dimension_semantics=sem, vmem_limit_bytes=VMEM_LIMIT)


def _pick(dim, pref):
    if dim <= pref:
        return dim
    t = (pref // LANES) * LANES
    while t > LANES and dim % t:
        t -= LANES
    assert dim % t == 0, (dim, pref)
    return t


def _rows_tile(rows, pref):
    t = (min(pref, rows) // 8) * 8
    while t >= 8 and rows % t:
        t -= 8
    return t if t >= 8 else rows


def _dot(a, b, dn=NN):
    return lax.dot_general(a, b, dn, preferred_element_type=F32)


def _split3(a):
    hi = a.astype(BF16)
    r = a - hi.astype(F32)
    mid = r.astype(BF16)
    return hi, mid, (r - mid.astype(F32)).astype(BF16)


def _dot_sel(a, e):
    hi, mid, lo = _split3(a)
    return (_dot(lo, e) + _dot(mid, e)) + _dot(hi, e)


def _sel_dot(e, b):
    hi, mid, lo = _split3(b)
    return (_dot(e, lo) + _dot(e, mid)) + _dot(e, hi)


def _sigmoid(v):
    return 1.0 / (1.0 + jnp.exp(-v))


def _silu_and_grad(v):
    s = _sigmoid(v)
    return v * s, s * (1.0 + v * (1.0 - s))


def _row(i, shape):
    return lax.broadcasted_iota(jnp.int32, shape, 0) + i


def _sds(shape, dtype):
    return jax.ShapeDtypeStruct(tuple(shape), dtype)


class Job:
    def __init__(self, ins, outs, aliased, nsem, start, finish):
        self.ins, self.outs, self.aliased, self.nsem, self.start, self.finish = ins, outs, aliased, nsem, start, finish


def _place():
    x, y, c = lax.axis_index("x"), lax.axis_index("y"), lax.axis_index("c")
    return x, y, c, [(1 - x, y), (x, 1 - y), (1 - x, 1 - y)]


def _remote(src, dst, send_sem, recv_sem, device):
    return pltpu.make_async_remote_copy(src_ref=src, dst_ref=dst, send_sem=send_sem, recv_sem=recv_sem,
                                        device_id=device, device_id_type=MESH)


def _half(ref, c, cols, lead=0):
    idx = [slice(None)] * lead
    if cols:
        w = ref.shape[-1] // 2
        idx += [slice(None)] * (len(ref.shape) - lead - 1) + [pl.ds(pl.multiple_of(c * w, LANES), w)]
    else:
        idx += [c]
    return ref.at[tuple(idx)]


def _flags(cols, n):
    return list(cols) if cols else [False] * n


def gather_send_job(arrs, cols=None):
    n = len(arrs)
    cols = _flags(cols, n)

    def copies(ins, outs, send, recv):
        x, y, c, chips = _place()
        mine = 2 * x + y
        out = []
        for a in range(n):
            out.append(_remote(ins[a], outs[a].at[mine], send.at[4 * a + 3], recv.at[4 * a + 3], (x, y, 1 - c)))
            for j, chip in enumerate(chips):
                out.append(_remote(_half(ins[a], c, cols[a]), _half(outs[a].at[mine], c, cols[a]),
                                   send.at[4 * a + j], recv.at[4 * a + j], (*chip, c)))
        return out

    def start(ins, outs, send, recv):
        for cp in copies(ins, outs, send, recv):
            cp.start()

    def finish(ins, outs, send, recv):
        x, y, c, chips = _place()
        for a in range(n):
            for j, chip in enumerate(chips):
                landed = _half(outs[a].at[2 * chip[0] + chip[1]], c, cols[a])
                _remote(landed, landed, send.at[4 * a + j], recv.at[4 * a + j], (x, y, 1 - c)).wait_recv()
            twin = outs[a].at[2 * x + y]
            _remote(twin, twin, send.at[4 * a + 3], recv.at[4 * a + 3], (x, y, 1 - c)).wait_recv()
        for cp in copies(ins, outs, send, recv):
            cp.wait_send()

    return Job(list(arrs), [_sds((4,) + a.shape, a.dtype) for a in arrs], False, 4 * n, start, finish)


def gather_pass_job(bufs, cols=None):
    n = len(bufs)
    cols = _flags(cols, n)

    def copies(outs, send, recv):
        x, y, c, chips = _place()
        out = []
        for a in range(n):
            for j, chip in enumerate(chips):
                landed = _half(outs[a].at[2 * chip[0] + chip[1]], c, cols[a])
                out.append(_remote(landed, landed, send.at[3 * a + j], recv.at[3 * a + j], (x, y, 1 - c)))
        return out

    def start(ins, outs, send, recv):
        for cp in copies(outs, send, recv):
            cp.start()

    def finish(ins, outs, send, recv):
        x, y, c, chips = _place()
        for a in range(n):
            for j, chip in enumerate(chips):
                passed = _half(outs[a].at[2 * chip[0] + chip[1]], 1 - c, cols[a])
                _remote(passed, passed, send.at[3 * a + j], recv.at[3 * a + j], (x, y, 1 - c)).wait_recv()
        for cp in copies(outs, send, recv):
            cp.wait_send()

    return Job(list(bufs), [_sds(b.shape, b.dtype) for b in bufs], True, 3 * n, start, finish)


def pair_exchange_job(arrs, cols=None):
    n = len(arrs)
    cols = _flags(cols, n)

    def copies(ins, outs, send, recv):
        x, y, c, _ = _place()
        return [_remote(_half(ins[a], 1 - c, cols[a], 1), outs[a], send.at[a], recv.at[a], (x, y, 1 - c))
                for a in range(n)]

    def start(ins, outs, send, recv):
        for cp in copies(ins, outs, send, recv):
            cp.start()

    def finish(ins, outs, send, recv):
        for cp in copies(ins, outs, send, recv):
            cp.wait()

    shape = lambda a, k: a.shape[:-1] + (a.shape[-1] // 2,) if k else a.shape[:1] + a.shape[2:]
    return Job(list(arrs), [_sds(shape(a, k), a.dtype) for a, k in zip(arrs, cols)], False, n, start, finish)


def chip_exchange_job(arrs):
    n = len(arrs)

    def copies(ins, outs, send, recv):
        x, y, c, chips = _place()
        return [_remote(ins[a].at[2 * chip[0] + chip[1]], outs[a].at[j], send.at[3 * a + j], recv.at[3 * a + j],
                        (*chip, c)) for a in range(n) for j, chip in enumerate(chips)]

    def start(ins, outs, send, recv):
        for cp in copies(ins, outs, send, recv):
            cp.start()

    def finish(ins, outs, send, recv):
        for cp in copies(ins, outs, send, recv):
            cp.wait()

    return Job(list(arrs), [_sds((3,) + a.shape[1:], a.dtype) for a in arrs], False, 3 * n, start, finish)


def pair_gather_job(bufs, cols=None):
    n = len(bufs)
    cols = _flags(cols, n)

    def copies(outs, send, recv):
        x, y, c, _ = _place()
        return [_remote(_half(outs[a], c, cols[a]), _half(outs[a], c, cols[a]), send.at[a], recv.at[a],
                        (x, y, 1 - c)) for a in range(n)]

    def start(ins, outs, send, recv):
        for cp in copies(outs, send, recv):
            cp.start()

    def finish(ins, outs, send, recv):
        for cp in copies(outs, send, recv):
            cp.wait()

    return Job(list(bufs), [_sds(b.shape, b.dtype) for b in bufs], True, n, start, finish)


def _call(body, *, grid, in_specs, out_specs, out_shape, name, sem, args, scratch_shapes=(), jobs=(), aliases=None):
    in_specs, out_specs, out_shape, scratch_shapes = list(in_specs), list(out_specs), list(out_shape), list(scratch_shapes)
    aliases = dict(aliases or {})
    n_in, n_out, n_scr = len(in_specs), len(out_specs), len(scratch_shapes)
    if jobs:
        sem = ("arbitrary",) * len(grid)
    at_in, at_out = n_in, n_out
    for j in jobs:
        if j.aliased:
            aliases.update({at_in + i: at_out + i for i in range(len(j.ins))})
        at_in, at_out = at_in + len(j.ins), at_out + len(j.outs)

    def wrapped(*refs):
        ins, p = refs[:n_in], n_in
        jins = []
        for j in jobs:
            jins.append(refs[p:p + len(j.ins)])
            p += len(j.ins)
        outs, p = refs[p:p + n_out], p + n_out
        jouts = []
        for j in jobs:
            jouts.append(refs[p:p + len(j.outs)])
            p += len(j.outs)
        scr, sems = refs[p:p + n_scr], refs[p + n_scr:]

        def start():
            for k, j in enumerate(jobs):
                j.start(jins[k], jouts[k], sems[2 * k], sems[2 * k + 1])

        def finish():
            for k, j in enumerate(jobs):
                j.finish(jins[k], jouts[k], sems[2 * k], sems[2 * k + 1])

        if jobs and grid:
            ids = [pl.program_id(d) for d in range(len(grid))]
            pl.when(functools.reduce(jnp.logical_and, [i == 0 for i in ids]))(start)
            body(*ins, *outs, *scr)
            pl.when(functools.reduce(jnp.logical_and, [i == g - 1 for i, g in zip(ids, grid)]))(finish)
        else:
            start()
            body(*ins, *outs, *scr)
            finish()

    kwargs = dict(grid=grid) if grid else {}
    res = pl.pallas_call(
        wrapped, in_specs=in_specs + [_ANY] * (at_in - n_in), out_specs=out_specs + [_ANY] * (at_out - n_out),
        out_shape=out_shape + [o for j in jobs for o in j.outs],
        scratch_shapes=scratch_shapes + [pltpu.SemaphoreType.DMA((j.nsem,)) for j in jobs for _ in range(2)],
        input_output_aliases=aliases, name=name,
        compiler_params=pltpu.CompilerParams(dimension_semantics=sem, vmem_limit_bytes=VMEM_LIMIT) if grid
        else pltpu.CompilerParams(vmem_limit_bytes=VMEM_LIMIT), **kwargs)(*args, *[a for j in jobs for a in j.ins])
    res = list(res)
    outs, rest, per_job = res[:n_out], res[n_out:], []
    for j in jobs:
        per_job.append(rest[:len(j.outs)])
        rest = rest[len(j.outs):]
    return outs, per_job


def run_jobs(jobs, name):
    return _call(lambda: None, grid=(), in_specs=[], out_specs=[], out_shape=[], name=name, sem=(), args=(), jobs=jobs)[1]


def pair_add(g, r, core, out_dtype, cols=False):
    S, R, C = r.shape
    T = _rows_tile(R, 256 if C <= 4096 else 128)

    def body(c_ref, g_ref, r_ref, o_ref):
        o_ref[0] = ((g_ref[0] if cols else g_ref[0, 0]).astype(F32) + r_ref[0].astype(F32)).astype(o_ref.dtype)

    g_spec = (pl.BlockSpec((1, T, C), lambda k, i, c_ref: (k, i, c_ref[0])) if cols
              else pl.BlockSpec((1, 1, T, C), lambda k, i, c_ref: (k, c_ref[0], i, 0)))
    return pl.pallas_call(
        body,
        grid_spec=pltpu.PrefetchScalarGridSpec(
            num_scalar_prefetch=1, grid=(S, R // T),
            in_specs=[g_spec, pl.BlockSpec((1, T, C), lambda k, i, c_ref: (k, i, 0))],
            out_specs=pl.BlockSpec((1, T, C), lambda k, i, c_ref: (k, i, 0))),
        out_shape=_sds((S, R, C), out_dtype), name="pair_add",
        compiler_params=_params("parallel", "parallel"))(core, g, r)


def chip_add(p, r, chip, core, cols=False):
    _, R, C = p.shape
    T = _rows_tile(R, 256)

    def body(k_ref, c_ref, p_ref, r0_ref, r1_ref, r2_ref, o_ref):
        s = ((p_ref[0].astype(F32) + r0_ref[0].astype(F32)) + r1_ref[0].astype(F32)) + r2_ref[0].astype(F32)
        if cols:
            o_ref[...] = s
        else:
            o_ref[0] = s

    slot = lambda j: pl.BlockSpec((1, T, C), lambda i, k_ref, c_ref: (j, i, 0))
    out_spec = (pl.BlockSpec((T, C), lambda i, k_ref, c_ref: (i, c_ref[0])) if cols
                else pl.BlockSpec((1, T, C), lambda i, k_ref, c_ref: (c_ref[0], i, 0)))
    return pl.pallas_call(
        body,
        grid_spec=pltpu.PrefetchScalarGridSpec(
            num_scalar_prefetch=2, grid=(R // T,),
            in_specs=[pl.BlockSpec((1, T, C), lambda i, k_ref, c_ref: (k_ref[0], i, 0)), slot(0), slot(1), slot(2)],
            out_specs=out_spec),
        out_shape=_sds((R, 2 * C) if cols else (2, R, C), F32), name="chip_add",
        compiler_params=_params("parallel"))(chip, core, p, r, r, r)


def allreduce_small(v):
    R = v.shape[0]

    def body(v_ref, o_ref, buf, send_sems, recv_sems, local_sem):
        x, y, c, chips = _place()
        me, sibling = (x, y, c), (x, y, 1 - c)

        def rows(px, py, pc):
            return buf.at[pl.ds((4 * px + 2 * py + pc) * R, R), :]

        def copy(k, block, to, src=None):
            return _remote(rows(*block) if src is None else src, rows(*block), send_sems.at[k], recv_sems.at[k], to)

        mine = pltpu.make_async_copy(v_ref, rows(*me), local_sem)
        mine.start()
        first = [copy(0, me, sibling, src=v_ref)]
        first += [copy(1 + j, me, (*chip, c), src=v_ref) for j, chip in enumerate(chips)]
        for cp in first:
            cp.start()
        passed = [copy(4 + j, (*chip, c), sibling) for j, chip in enumerate(chips)]
        for j, chip in enumerate(chips):
            copy(1 + j, (*chip, c), me).wait_recv()
            passed[j].start()
        copy(0, sibling, me).wait_recv()
        for j, chip in enumerate(chips):
            copy(4 + j, (*chip, 1 - c), me).wait_recv()
        for cp in first + passed:
            cp.wait_send()
        mine.wait()
        acc = buf[0:R, :]
        for d in range(1, 8):
            acc = acc + buf[d * R:(d + 1) * R, :]
        o_ref[...] = acc

    return pl.pallas_call(
        body, in_specs=[pl.BlockSpec(memory_space=pltpu.VMEM)], out_specs=pl.BlockSpec(memory_space=pltpu.VMEM),
        out_shape=_sds((R, LANES), F32),
        scratch_shapes=[pltpu.VMEM((8 * R, LANES), F32), pltpu.SemaphoreType.DMA((7,)),
                        pltpu.SemaphoreType.DMA((7,)), pltpu.SemaphoreType.DMA],
        name="allreduce_small", compiler_params=pltpu.CompilerParams(vmem_limit_bytes=VMEM_LIMIT))(v)


def matmul(a, b, mode, out_dtype, name, tm=512, tn=1024, tk=4608, jobs=(), n_out=None):
    if mode == "nn":
        (M, K), (K2, N) = a.shape, b.shape
    elif mode == "nt":
        (M, K), (N, K2) = a.shape, b.shape
        N = n_out or N
    else:
        (K, M), (K2, N) = a.shape, b.shape
    assert K == K2 or (mode == "nn" and K2 > K)
    tm, tn, tk = _pick(M, tm), _pick(N, tn), _pick(K, tk)
    nk = K // tk
    dn = {"nn": NN, "nt": NT, "tn": TN}[mode]

    def body(a_ref, b_ref, o_ref, *acc):
        part = _dot(a_ref[...].astype(BF16), b_ref[...].astype(BF16), dn)
        if nk == 1:
            o_ref[...] = part.astype(o_ref.dtype)
            return
        acc_ref, = acc
        k = pl.program_id(2)

        @pl.when(k == 0)
        def _():
            acc_ref[...] = part

        @pl.when(jnp.logical_and(k > 0, k < nk - 1))
        def _():
            acc_ref[...] += part

        @pl.when(k == nk - 1)
        def _():
            o_ref[...] = (acc_ref[...] + part).astype(o_ref.dtype)

    a_spec = (pl.BlockSpec((tk, tm), lambda i, j, k: (k, i)) if mode == "tn"
              else pl.BlockSpec((tm, tk), lambda i, j, k: (i, k)))
    b_spec = (pl.BlockSpec((tn, tk), lambda i, j, k: (j, k)) if mode == "nt"
              else pl.BlockSpec((tk, tn), lambda i, j, k: (k, j)))
    outs, per_job = _call(
        body, grid=(M // tm, N // tn, nk), in_specs=[a_spec, b_spec],
        out_specs=[pl.BlockSpec((tm, tn), lambda i, j, k: (i, j))], out_shape=[_sds((M, N), out_dtype)],
        scratch_shapes=[pltpu.VMEM((tm, tn), F32)] if nk > 1 else [], name=name,
        sem=("parallel", "parallel", "arbitrary"), args=(a, b), jobs=jobs)
    return outs[0], per_job


def rms_fwd(x, w):
    L, D = x.shape
    T = _pick(L, 256)

    def body(x_ref, w_ref, h_ref):
        xv = x_ref[...]
        r = lax.rsqrt(jnp.mean(xv * xv, axis=-1, keepdims=True) + NORM_EPS)
        h_ref[...] = (xv * r * w_ref[...]).astype(h_ref.dtype)

    return pl.pallas_call(
        body, grid=(L // T,),
        in_specs=[pl.BlockSpec((T, D), lambda i: (i, 0)), pl.BlockSpec((1, D), lambda i: (0, 0))],
        out_specs=pl.BlockSpec((T, D), lambda i: (i, 0)),
        out_shape=_sds((L, D), BF16), name="rms_fwd", compiler_params=_params("parallel"))(x, w)


def post_fwd(x, o, w, jobs=()):
    L, D = x.shape
    T = _pick(L, 256)

    def body(x_ref, o_ref, w_ref, y_ref):
        ov = o_ref[...]
        r = lax.rsqrt(jnp.mean(ov * ov, axis=-1, keepdims=True) + NORM_EPS)
        y_ref[...] = x_ref[...] + ov * r * w_ref[...]

    row = pl.BlockSpec((T, D), lambda i: (i, 0))
    outs, per_job = _call(
        body, grid=(L // T,), in_specs=[row, row, pl.BlockSpec((1, D), lambda i: (0, 0))], out_specs=[row],
        out_shape=[_sds((L, D), F32)], name="post_fwd", sem=("parallel",), args=(x, o, w), jobs=jobs)
    return outs[0], per_job


def _rms_bwd_math(xv, w, dy):
    r = lax.rsqrt(jnp.mean(xv * xv, axis=-1, keepdims=True) + NORM_EPS)
    xhat = xv * r
    g = dy * w
    dx = r * (g - xhat * jnp.mean(g * xhat, axis=-1, keepdims=True))
    return dx, jnp.sum(dy * xhat, axis=0, keepdims=True)


def post_bwd(o, w, dxn):
    L, D = o.shape
    T = _pick(L, 256)

    def body(o_ref, w_ref, d_ref, do_ref, dw_ref):
        dx, dw = _rms_bwd_math(o_ref[...], w_ref[...], d_ref[...])
        do_ref[...] = dx.astype(do_ref.dtype)

        @pl.when(pl.program_id(0) == 0)
        def _():
            dw_ref[...] = jnp.zeros_like(dw_ref)

        dw_ref[...] += dw

    row = pl.BlockSpec((T, D), lambda i: (i, 0))
    vec = pl.BlockSpec((1, D), lambda i: (0, 0))
    return pl.pallas_call(
        body, grid=(L // T,), in_specs=[row, vec, row], out_specs=[row, vec],
        out_shape=[_sds((L, D), BF16), _sds((1, D), F32)],
        name="post_bwd", compiler_params=_params("arbitrary"))(o, w, dxn)


def rms_bwd(x, w, dh_a, dh_b, dxn, jobs=()):
    L, D = x.shape
    T = _pick(L, 256)

    def body(x_ref, w_ref, a_ref, b_ref, d_ref, dx_ref, dw_ref):
        dx, dw = _rms_bwd_math(x_ref[...], w_ref[...], a_ref[...] + b_ref[...])
        dx_ref[...] = d_ref[...] + dx

        @pl.when(pl.program_id(0) == 0)
        def _():
            dw_ref[...] = jnp.zeros_like(dw_ref)

        dw_ref[...] += dw

    row = pl.BlockSpec((T, D), lambda i: (i, 0))
    vec = pl.BlockSpec((1, D), lambda i: (0, 0))
    return _call(
        body, grid=(L // T,), in_specs=[row, vec, row, row, row], out_specs=[row, vec],
        out_shape=[_sds((L, D), F32), _sds((1, D), F32)],
        name="rms_bwd", sem=("arbitrary",), args=(x, w, dh_a, dh_b, dxn), jobs=jobs)


def loss_head(y, target):
    L, D = y.shape
    T = _pick(L, 256)

    def body(y_ref, t_ref, d_ref, s_ref):
        e = y_ref[...] - t_ref[...]
        d_ref[...] = e * (1.0 / D)

        @pl.when(pl.program_id(0) == 0)
        def _():
            s_ref[...] = jnp.zeros_like(s_ref)

        s_ref[...] += jnp.sum(e * e)

    row = pl.BlockSpec((T, D), lambda i: (i, 0))
    return pl.pallas_call(
        body, grid=(L // T,), in_specs=[row, row],
        out_specs=[row, pl.BlockSpec((8, LANES), lambda i: (0, 0))],
        out_shape=[_sds((L, D), F32), _sds((8, LANES), F32)],
        name="loss_head", compiler_params=_params("arbitrary"))(y, target)


def _window_sums(xe, w, back):
    n = xe.shape[0]
    s, k = xe, 1
    while k < w:
        s = s + pltpu.roll(s, k if back else n - k, 0)
        k *= 2
    return s


def pool_fwd(proj, mixw, scale, D):
    L = proj.shape[0]
    PGW = D // GROUPS
    T = _pick(L, 256)
    hb = T // POOL_HALO

    def body(u_ref, halo_ref, g_ref, mw_ref, sc_ref, y_ref, p_ref):
        i = pl.program_id(0)
        u = u_ref[...]
        halo = jnp.where(i > 0, halo_ref[...], 0.0)
        xe = jnp.concatenate([halo, u], axis=0)
        t1 = _row(i * T + 1, (T, 1))
        for g, w in enumerate(POOL_WINDOWS):
            sl = slice(g * PGW, (g + 1) * PGW)
            win = _window_sums(xe[:, sl], w, True)[POOL_HALO:, :]
            cnt = jnp.minimum(t1, w).astype(F32)
            pooled = (win / cnt - u[:, sl]).astype(BF16)
            p_ref[:, sl] = pooled
            mixed = _dot(pooled, mw_ref[g])
            gate = g_ref[:, sl]
            y_ref[:, sl] = (mixed * sc_ref[:, sl] * (gate * _sigmoid(gate))).astype(BF16)

    return pl.pallas_call(
        body, grid=(L // T,),
        in_specs=[pl.BlockSpec((T, D), lambda i: (i, 0)),
                  pl.BlockSpec((POOL_HALO, D), lambda i: (jnp.maximum(i * hb - 1, 0), 0)),
                  pl.BlockSpec((T, D), lambda i: (i, 1)),
                  pl.BlockSpec((GROUPS, PGW, PGW), lambda i: (0, 0, 0)),
                  pl.BlockSpec((1, D), lambda i: (0, 0))],
        out_specs=[pl.BlockSpec((T, D), lambda i: (i, 0)), pl.BlockSpec((T, D), lambda i: (i, 0))],
        out_shape=[_sds((L, 2 * D), BF16), _sds((L, D), BF16)],
        name="pool_fwd", compiler_params=_params("parallel"))(proj, proj, proj, mixw, scale)


def pool_bwd(proj, dmix, pooled, mixw, scale, dproj, D):
    L = proj.shape[0]
    PGW = D // GROUPS
    T = _pick(L, 256)
    hb = T // POOL_HALO
    nT = L // T

    def body(g_ref, gh_ref, dy_ref, dyh_ref, p_ref, mw_ref, sc_ref, old_ref, dp_ref, dm_ref, ds_ref):
        i = pl.program_id(0)
        t1 = _row(i * T + 1, (T, 1))
        th1 = _row((i + 1) * T + 1, (POOL_HALO, 1))
        live = i < nT - 1

        @pl.when(i == 0)
        def _():
            ds_ref[...] = jnp.zeros_like(ds_ref)

        for g, w in enumerate(POOL_WINDOWS):
            sl = slice(g * PGW, (g + 1) * PGW)
            sc = sc_ref[:, sl]
            gate, dy = g_ref[:, sl], dy_ref[:, sl]
            sg, dsg = _silu_and_grad(gate)
            mixed = _dot(p_ref[:, sl], mw_ref[g])
            dmixed = (dy * sc * sg).astype(BF16)
            dm_ref[:, sl] = dmixed
            dp_ref[:, D + g * PGW:D + (g + 1) * PGW] = (dy * mixed * sc * dsg).astype(BF16)
            ds_ref[:, sl] += jnp.sum(dy * mixed * sg, axis=0, keepdims=True)
            dpool = _dot(dmixed, mw_ref[g], NT)
            gate_h = gh_ref[:, sl]
            dmixed_h = (dyh_ref[:, sl] * sc * (gate_h * _sigmoid(gate_h))).astype(BF16)
            dpool_h = jnp.where(live, _dot(dmixed_h, mw_ref[g], NT), 0.0)
            q = dpool / jnp.minimum(t1, w).astype(F32)
            q_h = dpool_h / jnp.minimum(th1, w).astype(F32)
            qe = jnp.concatenate([q, q_h], axis=0)
            dp_ref[:, sl] = (_window_sums(qe, w, False)[:T, :] - dpool).astype(BF16)

    nxt = lambda i: jnp.minimum((i + 1) * hb, L // POOL_HALO - 1)
    row = lambda c: pl.BlockSpec((T, D), lambda i: (i, c))
    return pl.pallas_call(
        body, grid=(nT,),
        in_specs=[row(1), pl.BlockSpec((POOL_HALO, D), lambda i: (nxt(i), 1)),
                  row(0), pl.BlockSpec((POOL_HALO, D), lambda i: (nxt(i), 0)),
                  row(0), pl.BlockSpec((GROUPS, PGW, PGW), lambda i: (0, 0, 0)),
                  pl.BlockSpec((1, D), lambda i: (0, 0)), _ANY],
        out_specs=[pl.BlockSpec((T, 2 * D), lambda i: (i, 0)), row(0), pl.BlockSpec((1, D), lambda i: (0, 0))],
        out_shape=[_sds(dproj.shape, dproj.dtype), _sds((L, D), BF16), _sds((1, D), F32)],
        input_output_aliases={7: 0},
        name="pool_bwd", compiler_params=_params("arbitrary"))(proj, proj, dmix, dmix, pooled, mixw, scale, dproj)


def pool_dw(pooled, dmixed, D):
    L = pooled.shape[0]
    PGW = D // GROUPS
    tk = _pick(L, 1024)
    nk = L // tk

    def body(p_ref, d_ref, o_ref):
        @pl.when(pl.program_id(1) == 0)
        def _():
            o_ref[...] = jnp.zeros_like(o_ref)

        o_ref[0] += _dot(p_ref[...], d_ref[...], TN)

    blk = pl.BlockSpec((tk, PGW), lambda g, k: (k, g))
    return pl.pallas_call(
        body, grid=(GROUPS, nk), in_specs=[blk, blk],
        out_specs=pl.BlockSpec((1, PGW, PGW), lambda g, k: (g, 0, 0)),
        out_shape=_sds((GROUPS, PGW, PGW), F32), name="pool_dw",
        compiler_params=_params("parallel", "arbitrary"))(pooled, dmixed)


def conv_fwd(proj, cw, cb, D):
    L = proj.shape[0]
    C = cw.shape[1]
    assert (3 * D) % C == 0
    cblk = (3 * D) // C
    T = _pick(L, 256)
    hb = T // CONV_HALO

    def body(u_ref, halo_ref, w_ref, b_ref, o_ref):
        i = pl.program_id(0)
        u = u_ref[...]
        xe = jnp.concatenate([jnp.where(i > 0, halo_ref[...], 0.0), u], axis=0)
        acc = b_ref[...] + w_ref[CONV_K - 1:CONV_K, :] * u
        for k in range(CONV_K - 1):
            acc = acc + w_ref[k:k + 1, :] * pltpu.roll(xe, CONV_K - 1 - k, 0)[CONV_HALO:, :]
        o_ref[...] = acc

    return pl.pallas_call(
        body, grid=(L // T,),
        in_specs=[pl.BlockSpec((T, C), lambda i: (i, cblk)),
                  pl.BlockSpec((CONV_HALO, C), lambda i: (jnp.maximum(i * hb - 1, 0), cblk)),
                  pl.BlockSpec((CONV_K, C), lambda i: (0, 0)),
                  pl.BlockSpec((1, C), lambda i: (0, 0))],
        out_specs=pl.BlockSpec((T, C), lambda i: (i, 0)),
        out_shape=_sds((L, C), F32), name="conv_fwd", compiler_params=_params("parallel"))(proj, proj, cw, cb)


def conv_bwd(dparts, proj, cw, dproj, D, jobs=()):
    L = proj.shape[0]
    C = cw.shape[1]
    cblk = (3 * D) // C
    T = _pick(L, 256)
    hb = T // CONV_HALO
    nT = L // T
    widths = [p.shape[1] for p in dparts]
    assert sum(widths) == C
    n = len(dparts)

    def body(*refs):
        d_refs, dn_refs = refs[:n], refs[n:2 * n]
        u_ref, up_ref, w_ref, old_ref, dr_ref, dw_ref, db_ref = refs[2 * n:]
        i = pl.program_id(0)

        @pl.when(i == 0)
        def _():
            dw_ref[...] = jnp.zeros_like(dw_ref)
            db_ref[...] = jnp.zeros_like(db_ref)

        at = 0
        for d_ref, dn_ref, wd in zip(d_refs, dn_refs, widths):
            sl = slice(at, at + wd)
            at += wd
            d = d_ref[...]
            u = u_ref[:, sl]
            de = jnp.concatenate([d, jnp.where(i < nT - 1, dn_ref[...], 0.0)], axis=0)
            ue = jnp.concatenate([jnp.where(i > 0, up_ref[:, sl], 0.0), u], axis=0)
            acc = w_ref[CONV_K - 1:CONV_K, sl] * d
            dw_ref[CONV_K - 1:CONV_K, sl] += jnp.sum(d * u, axis=0, keepdims=True)
            for k in range(CONV_K - 1):
                sh = CONV_K - 1 - k
                acc = acc + w_ref[k:k + 1, sl] * pltpu.roll(de, T + CONV_HALO - sh, 0)[:T, :]
                dw_ref[k:k + 1, sl] += jnp.sum(d * pltpu.roll(ue, sh, 0)[CONV_HALO:, :], axis=0, keepdims=True)
            dr_ref[:, sl] = acc.astype(dr_ref.dtype)
            db_ref[:, sl] += jnp.sum(d, axis=0, keepdims=True)

    nxt = lambda i: jnp.minimum((i + 1) * hb, L // CONV_HALO - 1)
    return _call(
        body, grid=(nT,),
        in_specs=[pl.BlockSpec((T, wd), lambda i: (i, 0)) for wd in widths]
        + [pl.BlockSpec((CONV_HALO, wd), lambda i: (nxt(i), 0)) for wd in widths]
        + [pl.BlockSpec((T, C), lambda i: (i, cblk)),
           pl.BlockSpec((CONV_HALO, C), lambda i: (jnp.maximum(i * hb - 1, 0), cblk)),
           pl.BlockSpec((CONV_K, C), lambda i: (0, 0)), _ANY],
        out_specs=[pl.BlockSpec((T, C), lambda i: (i, cblk)),
                   pl.BlockSpec((8, C), lambda i: (0, 0)),
                   pl.BlockSpec((1, C), lambda i: (0, 0))],
        out_shape=[_sds(dproj.shape, dproj.dtype), _sds((8, C), F32), _sds((1, C), F32)],
        aliases={2 * n + 3: 0}, name="conv_bwd", sem=("arbitrary",),
        args=(*dparts, *dparts, proj, proj, cw, dproj), jobs=jobs)


def _softplus(v):
    y = jnp.exp(-jnp.abs(v))
    u = 1.0 + y
    log1p = jnp.where(u == 1.0, y, jnp.log(u) * y / jnp.where(u == 1.0, 1.0, u - 1.0))
    return jnp.maximum(v, 0.0) + log1p


def dt_prep(dtraw, bias, alog, expand, D):
    L = dtraw.shape[0]
    GC = D // GROUPS
    HPG = GC // HEAD_DIM
    Q = _pick(L, SCAN_CHUNK)
    nc = L // Q

    def body(r_ref, b_ref, a_ref, e_ref, dt_ref, acs_ref, acst_ref, dtx_ref, eax_ref, dsx_ref, cdx_ref):
        valid = lax.broadcasted_iota(jnp.int32, (1, LANES), 1) < HPG
        dt = jnp.where(valid, _softplus(r_ref[...] + b_ref[...]), 0.0)
        adt = dt * -jnp.exp(a_ref[...])
        tril = (_row(0, (Q, Q)) >= lax.broadcasted_iota(jnp.int32, (Q, Q), 1)).astype(BF16)
        acs = _sel_dot(tril, adt)
        last = acs[Q - 1:Q, :]
        dt_ref[...] = dt
        acs_ref[...] = acs
        acst_ref[...] = acs.T
        e = e_ref[...]
        dtx_ref[...] = _dot_sel(dt, e)
        eax_ref[...] = jnp.exp(_dot_sel(acs, e))
        dsx_ref[...] = jnp.exp(_dot_sel(last - acs, e))
        cdx_ref[0] = jnp.exp(_dot_sel(jnp.broadcast_to(last, (8, LANES)), e))

    head = pl.BlockSpec((Q, LANES), lambda g, c: (c, g))
    hvec = pl.BlockSpec((1, LANES), lambda g, c: (0, g))
    chan = pl.BlockSpec((Q, GC), lambda g, c: (c, g))
    return pl.pallas_call(
        body, grid=(GROUPS, nc),
        in_specs=[head, hvec, hvec, pl.BlockSpec((LANES, GC), lambda g, c: (0, 0))],
        out_specs=[head, head, pl.BlockSpec((LANES, Q), lambda g, c: (g, c)), chan, chan, chan,
                   pl.BlockSpec((1, 8, GC), lambda g, c: (c, 0, g))],
        out_shape=[_sds((L, GROUPS * LANES), F32), _sds((L, GROUPS * LANES), F32), _sds((GROUPS * LANES, L), F32),
                   _sds((L, D), F32), _sds((L, D), F32), _sds((L, D), F32), _sds((nc, 8, D), F32)],
        name="dt_prep", compiler_params=_params("parallel", "parallel"))(dtraw, bias, alog, expand)


def _scan_specs(L, D, Q, rev):
    GC = D // GROUPS
    nc = L // Q
    ci = (lambda c: nc - 1 - c) if rev else (lambda c: c)
    return dict(
        xs=pl.BlockSpec((Q, GC), lambda g, c: (ci(c), g)),
        b=pl.BlockSpec((Q, STATE), lambda g, c: (ci(c), D // STATE + g)),
        c=pl.BlockSpec((Q, STATE), lambda g, c: (ci(c), D // STATE + GROUPS + g)),
        chan=pl.BlockSpec((Q, GC), lambda g, c: (ci(c), g)),
        cdx=pl.BlockSpec((1, 8, GC), lambda g, c: (ci(c), 0, g)),
        head=pl.BlockSpec((Q, LANES), lambda g, c: (ci(c), g)),
        headt=pl.BlockSpec((LANES, Q), lambda g, c: (g, ci(c))),
        state=pl.BlockSpec((1, 1, STATE, GC), lambda g, c: (ci(c), g, 0, 0)),
        hvec=pl.BlockSpec((1, LANES), lambda g, c: (0, g)),
        cvec=pl.BlockSpec((1, GC), lambda g, c: (0, g)))


def scan_fwd(pre, dtx, eax, dsx, cdx, acs, acst, proj, dexp, nw, mixed, D):
    L = pre.shape[0]
    GC = D // GROUPS
    Q = _pick(L, SCAN_CHUNK)
    nc = L // Q
    sp = _scan_specs(L, D, Q, False)

    def body(xs_ref, b_ref, c_ref, dtx_ref, eax_ref, dsx_ref, cdx_ref, acs_ref, acst_ref, z_ref, de_ref, nw_ref,
             old_ref, y_ref, st_ref, o_ref, s_scr):
        @pl.when(pl.program_id(1) == 0)
        def _():
            s_scr[...] = jnp.zeros_like(s_scr)

        tri = _row(0, (Q, Q)) >= lax.broadcasted_iota(jnp.int32, (Q, Q), 1)
        half = lax.broadcasted_iota(jnp.int32, (1, LANES), 1) // HEAD_DIM
        xs, _ = _silu_and_grad(xs_ref[...])
        bg = _silu_and_grad(b_ref[...])[0].astype(BF16)
        cg = _silu_and_grad(c_ref[...])[0].astype(BF16)
        xdt = xs * dtx_ref[...]
        sprev = s_scr[...]
        st_ref[0, 0] = sprev
        sc = _dot(cg, bg, NT)
        yoff = _dot(cg, sprev.astype(BF16)) * eax_ref[...]
        for j in range(GC // LANES):
            ps = slice(j * LANES, (j + 1) * LANES)
            xp = xdt[:, ps]
            acc = yoff[:, ps]
            for hh in range(2):
                h = 2 * j + hh
                lm = jnp.exp(jnp.where(tri, acs_ref[:, h:h + 1] - acst_ref[h:h + 1, :], -1e30))
                xm = jnp.where(half == hh, xp, 0.0).astype(BF16)
                acc = acc + _dot((sc * lm).astype(BF16), xm)
            y_ref[:, ps] = acc
        xw = (xdt * dsx_ref[...]).astype(BF16)
        s_scr[...] = cdx_ref[0, 0:1, :] * sprev + _dot(bg, xw, TN)
        z = z_ref[...]
        y3 = (y_ref[...] + de_ref[...] * xs) * (z * _sigmoid(z))
        r = lax.rsqrt(jnp.mean(y3 * y3, axis=-1, keepdims=True) + NORM_EPS)
        o_ref[...] = (y3 * r * nw_ref[...]).astype(o_ref.dtype)

    return pl.pallas_call(
        body, grid=(GROUPS, nc),
        in_specs=[sp["xs"], sp["b"], sp["c"], sp["chan"], sp["chan"], sp["chan"], sp["cdx"], sp["head"], sp["headt"],
                  pl.BlockSpec((Q, GC), lambda g, c: (c, 2 * GROUPS + g)), sp["cvec"], sp["cvec"], _ANY],
        out_specs=[sp["chan"], sp["state"], pl.BlockSpec((Q, GC), lambda g, c: (c, GROUPS + g))],
        out_shape=[_sds((L, D), F32), _sds((nc, GROUPS, STATE, GC), F32), _sds(mixed.shape, mixed.dtype)],
        input_output_aliases={12: 2},
        scratch_shapes=[pltpu.VMEM((STATE, GC), F32)], name="scan_fwd",
        compiler_params=_params("parallel", "arbitrary"))(
            pre, pre, pre, dtx, eax, dsx, cdx, acs, acst, proj, dexp, nw, mixed)


def scan_bwd(pre, dtx, eax, dsx, cdx, acs, acst, dt, dtraw, bias, alog, states, dy, dexp, collapse, D, jobs=()):
    L = pre.shape[0]
    GC = D // GROUPS
    Q = _pick(L, SCAN_CHUNK)
    nc = L // Q
    sp = _scan_specs(L, D, Q, True)
    rc = lambda c: nc - 1 - c

    def body(xs_ref, b_ref, c_ref, dtx_ref, eax_ref, dsx_ref, cdx_ref, acs_ref, acst_ref, dt_ref, raw_ref,
             bias_ref, alog_ref, st_ref, dy_ref, dexp_ref, col_ref,
             dxs_ref, db_ref, dc_ref, ddt_ref, dal_ref, dbi_ref, ds_scr, dx_scr):
        first = pl.program_id(1) == 0

        @pl.when(first)
        def _():
            ds_scr[...] = jnp.zeros_like(ds_scr)
            dal_ref[...] = jnp.zeros_like(dal_ref)
            dbi_ref[...] = jnp.zeros_like(dbi_ref)

        li = _row(0, (Q, Q))
        si = lax.broadcasted_iota(jnp.int32, (Q, Q), 1)
        lane = lax.broadcasted_iota(jnp.int32, (1, LANES), 1)
        half = lane // HEAD_DIM
        xs_pre, b_pre, c_pre = xs_ref[...], b_ref[...], c_ref[...]
        xs, xs_g = _silu_and_grad(xs_pre)
        bf, b_g = _silu_and_grad(b_pre)
        cf, c_g = _silu_and_grad(c_pre)
        bg, cg = bf.astype(BF16), cf.astype(BF16)
        dtx, eax, dsx = dtx_ref[...], eax_ref[...], dsx_ref[...]
        cd = cdx_ref[0, 0:1, :]
        xdt = xs * dtx
        G = dy_ref[...]
        prev = st_ref[0, 0]
        dsn = ds_scr[...]
        prev_b, dsn_b = prev.astype(BF16), dsn.astype(BF16)
        cp = _dot(cg, prev_b)
        ge_b = (G * eax).astype(BF16)
        d_c = _dot(ge_b, prev_b, NT)
        dprev = _dot(cg, ge_b, TN) + cd * dsn
        chan_a = G * cp * eax
        xw_b = (xdt * dsx).astype(BF16)
        dcd = jnp.sum(prev * dsn, axis=0, keepdims=True)
        d_b = _dot(xw_b, dsn_b, NT)
        dxw = _dot(bg, dsn_b)
        dd = dxw * xdt * dsx
        chan_a = chan_a - dd
        last_c = jnp.sum(dd, axis=0, keepdims=True) + dcd * cd
        sc = _dot(cg, bg, NT)
        head_row = _row(0, (LANES, 1))
        dsc = jnp.zeros((Q, Q), F32)
        dacs = jnp.zeros((Q, LANES), F32)
        colsums = jnp.zeros((LANES, Q), F32)
        for j in range(GC // LANES):
            ps = slice(j * LANES, (j + 1) * LANES)
            xp, gp = xdt[:, ps], G[:, ps]
            dxp = dxw[:, ps] * dsx[:, ps]
            for hh in range(2):
                h = 2 * j + hh
                lm = jnp.exp(jnp.where(li >= si, acs_ref[:, h:h + 1] - acst_ref[h:h + 1, :], -1e30))
                m = sc * lm
                xm = jnp.where(half == hh, xp, 0.0).astype(BF16)
                gm = jnp.where(half == hh, gp, 0.0).astype(BF16)
                dm = _dot(gm, xm, NT)
                dxp = dxp + _dot(m.astype(BF16), gm, TN)
                dsc = dsc + dm * lm
                w = dm * m
                dacs = dacs + jnp.where(lane == h, jnp.sum(w, axis=1, keepdims=True), 0.0)
                colsums = jnp.where(head_row == h, jnp.sum(w, axis=0, keepdims=True), colsums)
            dx_scr[:, ps] = dxp
        dacs = dacs - colsums.T
        dsc_b = dsc.astype(BF16)
        d_c = d_c + _dot(dsc_b, bg)
        d_b = d_b + _dot(dsc_b, cg, TN)
        ds_scr[...] = dprev
        dxdt = dx_scr[...]
        dxs_ref[...] = (dxdt * dtx + dexp_ref[...] * G) * xs_g
        db_ref[...] = d_b * b_g
        dc_ref[...] = d_c * c_g
        colm = col_ref[...]
        dacs = dacs + _dot_sel(chan_a, colm)
        dlast = _dot_sel(jnp.broadcast_to(last_c, (8, GC)), colm)[0:1, :]
        dacs = dacs + jnp.where(_row(0, (Q, 1)) == Q - 1, dlast, 0.0)
        dadt = _sel_dot((si >= li).astype(BF16), dacs)
        a = -jnp.exp(alog_ref[...])
        dt = dt_ref[...]
        ddt = dadt * a + _dot_sel(dxdt * xs, colm)
        dal_ref[0:1, :] += jnp.sum(dadt * dt * a, axis=0, keepdims=True)
        draw = ddt * _sigmoid(raw_ref[...] + bias_ref[...])
        dbi_ref[0:1, :] += jnp.sum(draw, axis=0, keepdims=True)
        ddt_ref[...] = draw.astype(ddt_ref.dtype)

    acc = pl.BlockSpec((8, LANES), lambda g, c: (0, g))
    return _call(
        body, grid=(GROUPS, nc),
        in_specs=[sp["xs"], sp["b"], sp["c"], sp["chan"], sp["chan"], sp["chan"], sp["cdx"], sp["head"], sp["headt"],
                  sp["head"], sp["head"], sp["hvec"], sp["hvec"], sp["state"], sp["chan"], sp["cvec"],
                  pl.BlockSpec((GC, LANES), lambda g, c: (0, 0))],
        out_specs=[sp["chan"],
                   pl.BlockSpec((Q, STATE), lambda g, c: (rc(c), g)),
                   pl.BlockSpec((Q, STATE), lambda g, c: (rc(c), g)),
                   sp["head"], acc, acc],
        out_shape=[_sds((L, D), F32), _sds((L, GROUPS * STATE), F32), _sds((L, GROUPS * STATE), F32),
                   _sds((L, GROUPS * LANES), BF16), _sds((8, GROUPS * LANES), F32), _sds((8, GROUPS * LANES), F32)],
        scratch_shapes=[pltpu.VMEM((STATE, GC), F32), pltpu.VMEM((Q, GC), F32)], name="scan_bwd",
        sem=("parallel", "arbitrary"),
        args=(pre, pre, pre, dtx, eax, dsx, cdx, acs, acst, dt, dtraw, bias, alog, states, dy, dexp, collapse),
        jobs=jobs)


def gate_bwd(y, pre, proj, dexp, nw, dmix, D, jobs=()):
    L = y.shape[0]
    GC = D // GROUPS
    T = _pick(L, 256)

    def body(y_ref, xs_ref, z_ref, de_ref, nw_ref, dm_ref, dy_ref, dz_ref, dnw_ref, dde_ref):
        @pl.when(pl.program_id(0) == 0)
        def _():
            dnw_ref[...] = jnp.zeros_like(dnw_ref)
            dde_ref[...] = jnp.zeros_like(dde_ref)

        xs, _ = _silu_and_grad(xs_ref[...])
        sz, dsz = _silu_and_grad(z_ref[...])
        y2 = y_ref[...] + de_ref[...] * xs
        y3 = y2 * sz
        for g in range(GROUPS):
            sl = slice(g * GC, (g + 1) * GC)
            yg, dm = y3[:, sl], dm_ref[:, sl]
            r = lax.rsqrt(jnp.mean(yg * yg, axis=-1, keepdims=True) + NORM_EPS)
            n = yg * r
            gg = dm * nw_ref[:, sl]
            dy3 = r * (gg - n * jnp.mean(gg * n, axis=-1, keepdims=True))
            dnw_ref[:, sl] += jnp.sum(dm * n, axis=0, keepdims=True)
            dy2 = dy3 * sz[:, sl]
            dy_ref[:, sl] = dy2
            dz_ref[:, sl] = (dy3 * y2[:, sl] * dsz[:, sl]).astype(dz_ref.dtype)
            dde_ref[:, sl] += jnp.sum(dy2 * xs[:, sl], axis=0, keepdims=True)

    row = lambda c: pl.BlockSpec((T, D), lambda i: (i, c))
    vec = pl.BlockSpec((1, D), lambda i: (0, 0))
    return _call(
        body, grid=(L // T,), in_specs=[row(0), row(0), row(2), vec, vec, row(1)],
        out_specs=[row(0), row(2), vec, vec],
        out_shape=[_sds((L, D), F32), _sds((L, proj.shape[1]), BF16), _sds((1, D), F32), _sds((1, D), F32)],
        name="gate_bwd", sem=("arbitrary",), args=(y, pre, proj, dexp, nw, dmix), jobs=jobs)


def _adam_math(gv, w, m, v):
    c1 = 1.0 - ADAM_B1 ** ADAM_STEP
    c2 = 1.0 - ADAM_B2 ** ADAM_STEP
    nm = ADAM_B1 * m + (1.0 - ADAM_B1) * gv
    nv = ADAM_B2 * v + (1.0 - ADAM_B2) * (gv * gv)
    return -ADAM_LR * ((nm / c1) / (jnp.sqrt(nv / c2) + ADAM_EPS) + ADAM_WD * w), nm, nv


def adamw(g, w, m, v, name):
    R, C = g.shape
    T = R if R <= 128 else 128
    assert R % T == 0

    def body(g_ref, w_ref, m_ref, v_ref, d_ref, nm_ref, nv_ref):
        d_ref[...], nm_ref[...], nv_ref[...] = _adam_math(g_ref[...], w_ref[...], m_ref[...], v_ref[...])

    blk = pl.BlockSpec((T, C), lambda i: (i, 0))
    return pl.pallas_call(
        body, grid=(R // T,), in_specs=[blk] * 4, out_specs=[blk] * 3,
        out_shape=[_sds((R, C), F32)] * 3, name=name, compiler_params=_params("parallel"))(g, w, m, v)


def adamw_layer(g, w, m, v, layer, prev, name):
    R, C = g.shape
    T = _rows_tile(R, 256)
    assert R % T == 0

    def body(g_ref, w_ref, m_ref, v_ref, *rest):
        go_ref, d_ref, nm_ref, nv_ref = rest[-4:]
        gv = g_ref[...]
        go_ref[0] = gv
        d_ref[0], nm_ref[0], nv_ref[0] = _adam_math(gv, w_ref[0], m_ref[0], v_ref[0])

    mine = pl.BlockSpec((1, T, C), lambda i: (layer, i, 0))
    prev = list(prev or [])
    return pl.pallas_call(
        body, grid=(R // T,), in_specs=[pl.BlockSpec((T, C), lambda i: (i, 0)), mine, mine, mine] + [_ANY] * len(prev),
        out_specs=[mine] * 4, out_shape=[_sds(w.shape, F32)] * 4,
        input_output_aliases={4 + i: i for i in range(len(prev))}, name=name,
        compiler_params=_params("parallel"))(g, w, m, v, *prev)


def cast_bf16(a):
    R, C = a.shape
    T = _rows_tile(R, 256)

    def body(a_ref, o_ref):
        o_ref[...] = a_ref[...].astype(BF16)

    blk = pl.BlockSpec((T, C), lambda i: (i, 0))
    return pl.pallas_call(body, grid=(R // T,), in_specs=[blk], out_specs=blk, out_shape=_sds((R, C), BF16),
                          name="cast_bf16", compiler_params=_params("parallel"))(a)


def _to_groups(v, hpg):
    lead = v.shape[:-1]
    t = v.reshape(lead + (GROUPS, hpg))
    t = jnp.pad(t, [(0, 0)] * (len(lead) + 1) + [(0, LANES - hpg)])
    return t.reshape(lead + (GROUPS * LANES,))


def _from_groups(a, hpg):
    lead = a.shape[:-1]
    return a.reshape(lead + (GROUPS, LANES))[..., :hpg].reshape(lead + (GROUPS * hpg,))


def _expand_matrix(D):
    gc = D // GROUPS
    return (jnp.arange(LANES)[:, None] == (jnp.arange(gc)[None, :] // HEAD_DIM)).astype(BF16)


def layer_params(pre_w, w_in_t, mixw, scale, cw, cb, bias, alog, dskip, nw, w_out_full, post_w, D):
    hpg = D // GROUPS // HEAD_DIM
    main = w_in_t.shape[0] - GROUPS * hpg
    wdt_t = _to_groups(w_in_t[main:].T, hpg).T
    return dict(
        pre_w=pre_w[None], win_t=w_in_t, main=main, wdt_t=wdt_t, mixw=mixw,
        scale=scale[None], cw=cw, cb=cb[None], bias=_to_groups(bias, hpg)[None], alog=_to_groups(alog, hpg)[None],
        dexp=jnp.repeat(dskip, HEAD_DIM)[None], nw=nw[None], wout=w_out_full, post_w=post_w[None])


def layer_fwd(x, p, D, next_shards=None):
    send_in = [gather_send_job(next_shards[:1], [True])] if next_shards else []
    h = rms_fwd(x, p["pre_w"])
    proj, got = matmul(h, p["win_t"], "nt", F32, "proj", tm=1024, jobs=send_in, n_out=p["main"])
    dtraw, _ = matmul(h, p["wdt_t"], "nt", F32, "dtproj")
    mixed, pooled = pool_fwd(proj, p["mixw"], p["scale"], D)
    pre = conv_fwd(proj, p["cw"], p["cb"], D)
    dtp = dt_prep(dtraw, p["bias"], p["alog"], _expand_matrix(D), D)
    dt, acs, acst, dtx, eax, dsx, cdx = dtp
    y, states, mixed = scan_fwd(pre, dtx, eax, dsx, cdx, acs, acst, proj, p["dexp"], p["nw"], mixed, D)
    jobs = [gather_send_job(next_shards[1:]), gather_pass_job(got[0], [True])] if next_shards else []
    out, got = matmul(mixed, p["wout"], "nn", F32, "outproj", tm=1024, jobs=jobs)
    xn, got2 = post_fwd(x, out, p["post_w"], jobs=[gather_pass_job(got[0])] if next_shards else [])
    gathered = got[1] + got2[0] if next_shards else None
    return xn, dict(x=x, h=h, proj=proj, dtraw=dtraw, pooled=pooled, pre=pre, dtp=dtp, y=y, states=states,
                    mixed=mixed, out=out), gathered


def layer_bwd(dxn, p, s, D, where=None):
    reduce = where is not None
    chip, core = where if reduce else (None, None)
    hpg = D // GROUPS // HEAD_DIM
    PGW = D // GROUPS
    main = p["main"]
    SH = (main + GROUPS * hpg) // 4
    dt, acs, acst, dtx, eax, dsx, cdx = s["dtp"]
    dout, d_post = post_bwd(s["out"], p["post_w"], dxn)
    dmix, _ = matmul(dout, p["wout"], "nt", F32, "dmixed", tm=1024)
    part = BF16 if reduce else F32
    d_wout, _ = matmul(s["mixed"], dout, "tn", part, "dwout", tm=1024)
    g_out = d_wout.reshape(4, 2, D // 4, D)
    (dy2, dproj, d_nw, d_dexp), got = gate_bwd(s["y"], s["pre"], s["proj"], p["dexp"], p["nw"], dmix, D,
                                               jobs=[pair_exchange_job([g_out])] if reduce else [])
    pair_out = pair_add(g_out, got[0][0], core, BF16) if reduce else None
    (dxs, db, dc, ddtraw, d_alog, d_bias), got = scan_bwd(
        s["pre"], dtx, eax, dsx, cdx, acs, acst, dt, s["dtraw"], p["bias"], p["alog"], s["states"], dy2,
        p["dexp"], _expand_matrix(D).T, D, jobs=[chip_exchange_job([pair_out])] if reduce else [])
    mine_out = chip_add(pair_out, got[0][0], chip, core) if reduce else None
    (dproj, d_cw, d_cb), got = conv_bwd([dxs, db, dc], s["proj"], p["cw"], dproj, D,
                                        jobs=[pair_gather_job([mine_out])] if reduce else [])
    r_out = got[0][0] if reduce else None
    dproj, dmixed, d_scale = pool_bwd(s["proj"], dmix, s["pooled"], p["mixw"], p["scale"], dproj, D)
    d_mixw = pool_dw(s["pooled"], dmixed, D)
    d_wmain_t, _ = matmul(dproj, s["h"], "tn", part, "dwmain", tm=1024)
    d_wdt_t, _ = matmul(ddtraw, s["h"], "tn", part, "dwdt")
    late = [d_wmain_t[None], d_wdt_t[None],
            d_mixw.reshape(GROUPS, 4, PGW // 4, PGW).transpose(1, 0, 2, 3).reshape(4, 2, GROUPS * PGW // 8, PGW)]
    cols = [True, True, False]
    dh_b, got = matmul(ddtraw, p["wdt_t"], "nn", F32, "dh_dt",
                       jobs=[pair_exchange_job(late, cols)] if reduce else [])
    if reduce:
        p_main, p_dt, p_mix = [pair_add(g, r, core, BF16, k) for g, r, k in zip(late, got[0], cols)]
        p_in = jnp.concatenate([p_main[0], _from_groups(p_dt[0].T, hpg).T], axis=0)
        pairs = [p_in.reshape(4, SH, D // 2), p_mix]
    dh_a, got = matmul(dproj, p["win_t"], "nn", F32, "dh_main", jobs=[chip_exchange_job(pairs)] if reduce else [])
    mines = [chip_add(q, r, chip, core, k) for q, r, k in zip(pairs, got[0], [True, False])] if reduce else None
    (dx, d_pre), got = rms_bwd(s["x"], p["pre_w"], dh_a, dh_b, dxn,
                               jobs=[pair_gather_job(mines, [True, False])] if reduce else [])
    reduced = dict(w_in=got[0][0], w_out=r_out, pool_mix_w=got[0][1]) if reduce else None
    grads = dict(
        pre_norm_w=d_pre[0], pool_scale=d_scale[0], conv_w=d_cw[:CONV_K], conv_b=d_cb[0],
        dt_bias=_from_groups(d_bias[0], hpg), a_log=_from_groups(d_alog[0], hpg),
        d_skip=d_dexp[0].reshape(-1, HEAD_DIM).sum(axis=-1), ssd_norm_w=d_nw[0], post_norm_w=d_post[0])
    if not reduce:
        grads.update(w_in=jnp.concatenate([d_wmain_t.T, _from_groups(d_wdt_t.T, hpg)], axis=1), pool_mix_w=d_mixw,
                     w_out=d_wout)
    return dx, grads, reduced


def local_step(x, target, params, D):
    saved = []
    for p in params:
        x, s, _ = layer_fwd(x, p, D)
        saved.append(s)
    dx, sumsq = loss_head(x, target)
    grads = [None] * len(params)
    for l in reversed(range(len(params))):
        dx, grads[l], _ = layer_bwd(dx, params[l], saved[l], D)
    return sumsq, dx, grads


SMALL = ("pre_norm_w", "pool_scale", "conv_w", "conv_b", "dt_bias", "a_log", "d_skip", "ssd_norm_w", "post_norm_w")
BIG = ("w_in", "w_out", "pool_mix_w")


def _pack(parts):
    flat = jnp.concatenate([p.reshape(-1) for p in parts])
    n = flat.shape[0]
    rows = -(-n // (LANES * LANES)) * LANES
    return jnp.pad(flat, (0, rows * LANES - n)).reshape(rows, LANES)


def _unpack(packed, shapes):
    flat, out, at = packed.reshape(-1), [], 0
    for s in shapes:
        n = math.prod(s)
        out.append(flat[at:at + n].reshape(s))
        at += n
    return out


def kernel(x, pre_norm_w, w_in, pool_mix_w, pool_scale, conv_w, conv_b, dt_bias, a_log, d_skip, ssd_norm_w, w_out, post_norm_w, loss_target, m_pre_norm_w, m_w_in, m_pool_mix_w, m_pool_scale, m_conv_w, m_conv_b, m_dt_bias, m_a_log, m_d_skip, m_ssd_norm_w, m_w_out, m_post_norm_w, v_pre_norm_w, v_w_in, v_pool_mix_w, v_pool_scale, v_conv_w, v_conv_b, v_dt_bias, v_a_log, v_d_skip, v_ssd_norm_w, v_w_out, v_post_norm_w):
    NL, D, SH = w_in.shape
    PGW = D // GROUPS
    CS = conv_w.shape[2]
    chip = (2 * lax.axis_index("x") + lax.axis_index("y")).astype(jnp.int32)
    chip1, core = chip.reshape(1), lax.axis_index("c").astype(jnp.int32).reshape(1)

    tr = lambda t: jnp.transpose(t, (0, 2, 1))
    w_in_t, m_w_in_t, v_w_in_t = tr(w_in), tr(m_w_in), tr(v_w_in)
    halved_by_cols = [True, False, False, False]

    def shards(l):
        return [cast_bf16(w_in_t[l]), cast_bf16(w_out[l]).reshape(2, D // 4, D),
                cast_bf16(pool_mix_w[l].reshape(GROUPS * PGW // 4, PGW)).reshape(2, GROUPS * PGW // 8, PGW),
                conv_w[l].reshape(2, CONV_K * CS // (2 * LANES), LANES)]

    def params(l, g):
        g_in, g_out, g_mix, g_cw = g
        mix_l = g_mix.reshape(4, GROUPS, PGW // 4, PGW).transpose(1, 0, 2, 3).reshape(GROUPS, PGW, PGW)
        cw_l = g_cw.reshape(4, CONV_K, CS).transpose(1, 0, 2).reshape(CONV_K, 4 * CS)
        return layer_params(pre_norm_w[l], g_in.reshape(4 * SH, D), mix_l, pool_scale[l], cw_l, conv_b[l], dt_bias[l],
                            a_log[l], d_skip[l], ssd_norm_w[l], g_out.reshape(2 * D, D), post_norm_w[l], D)

    gathered = run_jobs([gather_send_job(shards(0), halved_by_cols)], "gather_send")[0]
    gathered = run_jobs([gather_pass_job(gathered, halved_by_cols)], "gather_pass")[0]
    h, ps, saved = x[0], [], []
    for l in range(NL):
        ps.append(params(l, gathered))
        h, s, gathered = layer_fwd(h, ps[l], D, shards(l + 1) if l + 1 < NL else None)
        saved.append(s)
    dx, sumsq = loss_head(h, loss_target[0])

    given = dict(w_in=(w_in_t, m_w_in_t, v_w_in_t), w_out=(w_out, m_w_out, v_w_out),
                 pool_mix_w=(pool_mix_w, m_pool_mix_w, v_pool_mix_w))
    flat = {n: [t.reshape(NL, -1, t.shape[-1]) for t in given[n]] for n in BIG}
    done = {n: None for n in BIG}
    grads = [None] * NL
    for l in reversed(range(NL)):
        dx, grads[l], reduced = layer_bwd(dx, ps[l], saved[l], D, (chip1, core))
        for n in BIG:
            r = reduced[n]
            done[n] = adamw_layer(r.reshape(-1, r.shape[-1]), *flat[n], l, done[n], "adamw_" + n)

    small_shapes = [(NL,) + grads[0][n].shape for n in SMALL]
    packed = _pack([0.5 / D * sumsq[0, :1]] + [jnp.stack([g[n] for g in grads]) for n in SMALL])
    total = allreduce_small(packed)
    loss, *small = _unpack(total, [(1,)] + small_shapes)
    small = dict(zip(SMALL, small))
    small["conv_w"] = lax.dynamic_slice_in_dim(small["conv_w"], chip * CS, CS, axis=2)

    given_small = dict(
        pre_norm_w=(pre_norm_w, m_pre_norm_w, v_pre_norm_w), pool_scale=(pool_scale, m_pool_scale, v_pool_scale),
        conv_w=(conv_w, m_conv_w, v_conv_w), conv_b=(conv_b, m_conv_b, v_conv_b),
        dt_bias=(dt_bias, m_dt_bias, v_dt_bias), a_log=(a_log, m_a_log, v_a_log),
        d_skip=(d_skip, m_d_skip, v_d_skip), ssd_norm_w=(ssd_norm_w, m_ssd_norm_w, v_ssd_norm_w),
        post_norm_w=(post_norm_w, m_post_norm_w, v_post_norm_w))
    shapes = [given_small[n][0].shape for n in SMALL]
    upd = adamw(_pack([small[n] for n in SMALL]), *[_pack([given_small[n][i] for n in SMALL]) for i in range(3)],
                "adamw_small")
    upd = [dict(zip(SMALL, _unpack(u, shapes))) for u in upd]

    out = {n: (small[n], upd[0][n], upd[1][n], upd[2][n]) for n in SMALL}
    for n in BIG:
        out[n] = tuple(t.reshape(given[n][0].shape) for t in done[n])
    out["w_in"] = tuple(tr(t) for t in out["w_in"])

    order = ("pre_norm_w", "w_in", "pool_mix_w", "pool_scale", "conv_w", "conv_b", "dt_bias", "a_log", "d_skip",
             "ssd_norm_w", "w_out", "post_norm_w")
    return (loss.reshape(()), dx[None], *[out[n][0] for n in order], *[out[n][1] for n in order],
            *[out[n][2] for n in order], *[out[n][3] for n in order])
```

```python
import functools
import math

import jax
import jax.numpy as jnp
from jax import lax
from jax.experimental import pallas as pl
from jax.experimental.pallas import tpu as pltpu

F32 = jnp.float32
BF16 = jnp.bfloat16

NORM_EPS = 1e-6
HEAD_DIM = 64
STATE = 128
GROUPS = 4
POOL_WINDOWS = (2, 4, 8, 16)
POOL_HALO = 16
CONV_K = 4
CONV_HALO = 8
SCAN_CHUNK = 256
LANES = 128
VMEM_LIMIT = 52 * 1024 * 1024

ADAM_LR = 0.001
ADAM_B1 = 0.9
ADAM_B2 = 0.999
ADAM_EPS = 1e-08
ADAM_WD = 0.01
ADAM_STEP = 10

MESH = pl.DeviceIdType.MESH

NN = (((1,), (0,)), ((), ()))
NT = (((1,), (1,)), ((), ()))
TN = (((0,), (0,)), ((), ()))

_ANY = pl.BlockSpec(memory_space=pl.ANY)


def _params(*sem):
    return pltpu.CompilerParams(dimension_semantics=sem, vmem_limit_bytes=VMEM_LIMIT)


def _pick(dim, pref):
    if dim <= pref:
        return dim
    t = (pref // LANES) * LANES
    while t > LANES and dim % t:
        t -= LANES
    assert dim % t == 0, (dim, pref)
    return t


def _rows_tile(rows, pref):
    t = (min(pref, rows) // 8) * 8
    while t >= 8 and rows % t:
        t -= 8
    return t if t >= 8 else rows


def _dot(a, b, dn=NN):
    return lax.dot_general(a, b, dn, preferred_element_type=F32)


def _split3(a):
    hi = a.astype(BF16)
    r = a - hi.astype(F32)
    mid = r.astype(BF16)
    return hi, mid, (r - mid.astype(F32)).astype(BF16)


def _dot_sel(a, e):
    hi, mid, lo = _split3(a)
    return (_dot(lo, e) + _dot(mid, e)) + _dot(hi, e)


def _sel_dot(e, b):
    hi, mid, lo = _split3(b)
    return (_dot(e, lo) + _dot(e, mid)) + _dot(e, hi)


def _sigmoid(v):
    return 1.0 / (1.0 + jnp.exp(-v))


def _silu_and_grad(v):
    s = _sigmoid(v)
    return v * s, s * (1.0 + v * (1.0 - s))


def _row(i, shape):
    return lax.broadcasted_iota(jnp.int32, shape, 0) + i


def _sds(shape, dtype):
    return jax.ShapeDtypeStruct(tuple(shape), dtype)


class Job:
    def __init__(self, ins, outs, aliased, nsem, start, finish):
        self.ins, self.outs, self.aliased, self.nsem, self.start, self.finish = ins, outs, aliased, nsem, start, finish


def _place():
    x, y, c = lax.axis_index("x"), lax.axis_index("y"), lax.axis_index("c")
    return x, y, c, [(1 - x, y), (x, 1 - y), (1 - x, 1 - y)]


def _remote(src, dst, send_sem, recv_sem, device):
    return pltpu.make_async_remote_copy(src_ref=src, dst_ref=dst, send_sem=send_sem, recv_sem=recv_sem,
                                        device_id=device, device_id_type=MESH)


def _half(ref, c, cols, lead=0):
    idx = [slice(None)] * lead
    if cols:
        w = ref.shape[-1] // 2
        idx += [slice(None)] * (len(ref.shape) - lead - 1) + [pl.ds(pl.multiple_of(c * w, LANES), w)]
    else:
        idx += [c]
    return ref.at[tuple(idx)]


def _flags(cols, n):
    return list(cols) if cols else [False] * n


def gather_send_job(arrs, cols=None):
    n = len(arrs)
    cols = _flags(cols, n)

    def copies(ins, outs, send, recv):
        x, y, c, chips = _place()
        mine = 2 * x + y
        out = []
        for a in range(n):
            out.append(_remote(ins[a], outs[a].at[mine], send.at[4 * a + 3], recv.at[4 * a + 3], (x, y, 1 - c)))
            for j, chip in enumerate(chips):
                out.append(_remote(_half(ins[a], c, cols[a]), _half(outs[a].at[mine], c, cols[a]),
                                   send.at[4 * a + j], recv.at[4 * a + j], (*chip, c)))
        return out

    def start(ins, outs, send, recv):
        for cp in copies(ins, outs, send, recv):
            cp.start()

    def finish(ins, outs, send, recv):
        x, y, c, chips = _place()
        for a in range(n):
            for j, chip in enumerate(chips):
                landed = _half(outs[a].at[2 * chip[0] + chip[1]], c, cols[a])
                _remote(landed, landed, send.at[4 * a + j], recv.at[4 * a + j], (x, y, 1 - c)).wait_recv()
            twin = outs[a].at[2 * x + y]
            _remote(twin, twin, send.at[4 * a + 3], recv.at[4 * a + 3], (x, y, 1 - c)).wait_recv()
        for cp in copies(ins, outs, send, recv):
            cp.wait_send()

    return Job(list(arrs), [_sds((4,) + a.shape, a.dtype) for a in arrs], False, 4 * n, start, finish)


def gather_pass_job(bufs, cols=None):
    n = len(bufs)
    cols = _flags(cols, n)

    def copies(outs, send, recv):
        x, y, c, chips = _place()
        out = []
        for a in range(n):
            for j, chip in enumerate(chips):
                landed = _half(outs[a].at[2 * chip[0] + chip[1]], c, cols[a])
                out.append(_remote(landed, landed, send.at[3 * a + j], recv.at[3 * a + j], (x, y, 1 - c)))
        return out

    def start(ins, outs, send, recv):
        for cp in copies(outs, send, recv):
            cp.start()

    def finish(ins, outs, send, recv):
        x, y, c, chips = _place()
        for a in range(n):
            for j, chip in enumerate(chips):
                passed = _half(outs[a].at[2 * chip[0] + chip[1]], 1 - c, cols[a])
                _remote(passed, passed, send.at[3 * a + j], recv.at[3 * a + j], (x, y, 1 - c)).wait_recv()
        for cp in copies(outs, send, recv):
            cp.wait_send()

    return Job(list(bufs), [_sds(b.shape, b.dtype) for b in bufs], True, 3 * n, start, finish)


def pair_exchange_job(arrs, cols=None):
    n = len(arrs)
    cols = _flags(cols, n)

    def copies(ins, outs, send, recv):
        x, y, c, _ = _place()
        return [_remote(_half(ins[a], 1 - c, cols[a], 1), outs[a], send.at[a], recv.at[a], (x, y, 1 - c))
                for a in range(n)]

    def start(ins, outs, send, recv):
        for cp in copies(ins, outs, send, recv):
            cp.start()

    def finish(ins, outs, send, recv):
        for cp in copies(ins, outs, send, recv):
            cp.wait()

    shape = lambda a, k: a.shape[:-1] + (a.shape[-1] // 2,) if k else a.shape[:1] + a.shape[2:]
    return Job(list(arrs), [_sds(shape(a, k), a.dtype) for a, k in zip(arrs, cols)], False, n, start, finish)


def chip_exchange_job(arrs):
    n = len(arrs)

    def copies(ins, outs, send, recv):
        x, y, c, chips = _place()
        return [_remote(ins[a].at[2 * chip[0] + chip[1]], outs[a].at[j], send.at[3 * a + j], recv.at[3 * a + j],
                        (*chip, c)) for a in range(n) for j, chip in enumerate(chips)]

    def start(ins, outs, send, recv):
        for cp in copies(ins, outs, send, recv):
            cp.start()

    def finish(ins, outs, send, recv):
        for cp in copies(ins, outs, send, recv):
            cp.wait()

    return Job(list(arrs), [_sds((3,) + a.shape[1:], a.dtype) for a in arrs], False, 3 * n, start, finish)


def pair_gather_job(bufs, cols=None):
    n = len(bufs)
    cols = _flags(cols, n)

    def copies(outs, send, recv):
        x, y, c, _ = _place()
        return [_remote(_half(outs[a], c, cols[a]), _half(outs[a], c, cols[a]), send.at[a], recv.at[a],
                        (x, y, 1 - c)) for a in range(n)]

    def start(ins, outs, send, recv):
        for cp in copies(outs, send, recv):
            cp.start()

    def finish(ins, outs, send, recv):
        for cp in copies(outs, send, recv):
            cp.wait()

    return Job(list(bufs), [_sds(b.shape, b.dtype) for b in bufs], True, n, start, finish)


def _call(body, *, grid, in_specs, out_specs, out_shape, name, sem, args, scratch_shapes=(), jobs=(), aliases=None):
    in_specs, out_specs, out_shape, scratch_shapes = list(in_specs), list(out_specs), list(out_shape), list(scratch_shapes)
    aliases = dict(aliases or {})
    n_in, n_out, n_scr = len(in_specs), len(out_specs), len(scratch_shapes)
    if jobs:
        sem = ("arbitrary",) * len(grid)
    at_in, at_out = n_in, n_out
    for j in jobs:
        if j.aliased:
            aliases.update({at_in + i: at_out + i for i in range(len(j.ins))})
        at_in, at_out = at_in + len(j.ins), at_out + len(j.outs)

    def wrapped(*refs):
        ins, p = refs[:n_in], n_in
        jins = []
        for j in jobs:
            jins.append(refs[p:p + len(j.ins)])
            p += len(j.ins)
        outs, p = refs[p:p + n_out], p + n_out
        jouts = []
        for j in jobs:
            jouts.append(refs[p:p + len(j.outs)])
            p += len(j.outs)
        scr, sems = refs[p:p + n_scr], refs[p + n_scr:]

        def start():
            for k, j in enumerate(jobs):
                j.start(jins[k], jouts[k], sems[2 * k], sems[2 * k + 1])

        def finish():
            for k, j in enumerate(jobs):
                j.finish(jins[k], jouts[k], sems[2 * k], sems[2 * k + 1])

        if jobs and grid:
            ids = [pl.program_id(d) for d in range(len(grid))]
            pl.when(functools.reduce(jnp.logical_and, [i == 0 for i in ids]))(start)
            body(*ins, *outs, *scr)
            pl.when(functools.reduce(jnp.logical_and, [i == g - 1 for i, g in zip(ids, grid)]))(finish)
        else:
            start()
            body(*ins, *outs, *scr)
            finish()

    kwargs = dict(grid=grid) if grid else {}
    res = pl.pallas_call(
        wrapped, in_specs=in_specs + [_ANY] * (at_in - n_in), out_specs=out_specs + [_ANY] * (at_out - n_out),
        out_shape=out_shape + [o for j in jobs for o in j.outs],
        scratch_shapes=scratch_shapes + [pltpu.SemaphoreType.DMA((j.nsem,)) for j in jobs for _ in range(2)],
        input_output_aliases=aliases, name=name,
        compiler_params=pltpu.CompilerParams(dimension_semantics=sem, vmem_limit_bytes=VMEM_LIMIT) if grid
        else pltpu.CompilerParams(vmem_limit_bytes=VMEM_LIMIT), **kwargs)(*args, *[a for j in jobs for a in j.ins])
    res = list(res)
    outs, rest, per_job = res[:n_out], res[n_out:], []
    for j in jobs:
        per_job.append(rest[:len(j.outs)])
        rest = rest[len(j.outs):]
    return outs, per_job


def run_jobs(jobs, name):
    return _call(lambda: None, grid=(), in_specs=[], out_specs=[], out_shape=[], name=name, sem=(), args=(), jobs=jobs)[1]


def pair_add(g, r, core, out_dtype, cols=False):
    S, R, C = r.shape
    T = _rows_tile(R, 256 if C <= 4096 else 128)

    def body(c_ref, g_ref, r_ref, o_ref):
        o_ref[0] = ((g_ref[0] if cols else g_ref[0, 0]).astype(F32) + r_ref[0].astype(F32)).astype(o_ref.dtype)

    g_spec = (pl.BlockSpec((1, T, C), lambda k, i, c_ref: (k, i, c_ref[0])) if cols
              else pl.BlockSpec((1, 1, T, C), lambda k, i, c_ref: (k, c_ref[0], i, 0)))
    return pl.pallas_call(
        body,
        grid_spec=pltpu.PrefetchScalarGridSpec(
            num_scalar_prefetch=1, grid=(S, R // T),
            in_specs=[g_spec, pl.BlockSpec((1, T, C), lambda k, i, c_ref: (k, i, 0))],
            out_specs=pl.BlockSpec((1, T, C), lambda k, i, c_ref: (k, i, 0))),
        out_shape=_sds((S, R, C), out_dtype), name="pair_add",
        compiler_params=_params("parallel", "parallel"))(core, g, r)


def chip_add(p, r, chip, core, cols=False):
    _, R, C = p.shape
    T = _rows_tile(R, 256)

    def body(k_ref, c_ref, p_ref, r0_ref, r1_ref, r2_ref, o_ref):
        s = ((p_ref[0].astype(F32) + r0_ref[0].astype(F32)) + r1_ref[0].astype(F32)) + r2_ref[0].astype(F32)
        if cols:
            o_ref[...] = s
        else:
            o_ref[0] = s

    slot = lambda j: pl.BlockSpec((1, T, C), lambda i, k_ref, c_ref: (j, i, 0))
    out_spec = (pl.BlockSpec((T, C), lambda i, k_ref, c_ref: (i, c_ref[0])) if cols
                else pl.BlockSpec((1, T, C), lambda i, k_ref, c_ref: (c_ref[0], i, 0)))
    return pl.pallas_call(
        body,
        grid_spec=pltpu.PrefetchScalarGridSpec(
            num_scalar_prefetch=2, grid=(R // T,),
            in_specs=[pl.BlockSpec((1, T, C), lambda i, k_ref, c_ref: (k_ref[0], i, 0)), slot(0), slot(1), slot(2)],
            out_specs=out_spec),
        out_shape=_sds((R, 2 * C) if cols else (2, R, C), F32), name="chip_add",
        compiler_params=_params("parallel"))(chip, core, p, r, r, r)


def allreduce_small(v):
    R = v.shape[0]

    def body(v_ref, o_ref, buf, send_sems, recv_sems, local_sem):
        x, y, c, chips = _place()
        me, sibling = (x, y, c), (x, y, 1 - c)

        def rows(px, py, pc):
            return buf.at[pl.ds((4 * px + 2 * py + pc) * R, R), :]

        def copy(k, block, to, src=None):
            return _remote(rows(*block) if src is None else src, rows(*block), send_sems.at[k], recv_sems.at[k], to)

        mine = pltpu.make_async_copy(v_ref, rows(*me), local_sem)
        mine.start()
        first = [copy(0, me, sibling, src=v_ref)]
        first += [copy(1 + j, me, (*chip, c), src=v_ref) for j, chip in enumerate(chips)]
        for cp in first:
            cp.start()
        passed = [copy(4 + j, (*chip, c), sibling) for j, chip in enumerate(chips)]
        for j, chip in enumerate(chips):
            copy(1 + j, (*chip, c), me).wait_recv()
            passed[j].start()
        copy(0, sibling, me).wait_recv()
        for j, chip in enumerate(chips):
            copy(4 + j, (*chip, 1 - c), me).wait_recv()
        for cp in first + passed:
            cp.wait_send()
        mine.wait()
        acc = buf[0:R, :]
        for d in range(1, 8):
            acc = acc + buf[d * R:(d + 1) * R, :]
        o_ref[...] = acc

    return pl.pallas_call(
        body, in_specs=[pl.BlockSpec(memory_space=pltpu.VMEM)], out_specs=pl.BlockSpec(memory_space=pltpu.VMEM),
        out_shape=_sds((R, LANES), F32),
        scratch_shapes=[pltpu.VMEM((8 * R, LANES), F32), pltpu.SemaphoreType.DMA((7,)),
                        pltpu.SemaphoreType.DMA((7,)), pltpu.SemaphoreType.DMA],
        name="allreduce_small", compiler_params=pltpu.CompilerParams(vmem_limit_bytes=VMEM_LIMIT))(v)


def matmul(a, b, mode, out_dtype, name, tm=512, tn=1024, tk=4608, jobs=(), n_out=None):
    if mode == "nn":
        (M, K), (K2, N) = a.shape, b.shape
    elif mode == "nt":
        (M, K), (N, K2) = a.shape, b.shape
        N = n_out or N
    else:
        (K, M), (K2, N) = a.shape, b.shape
    assert K == K2 or (mode == "nn" and K2 > K)
    tm, tn, tk = _pick(M, tm), _pick(N, tn), _pick(K, tk)
    nk = K // tk
    dn = {"nn": NN, "nt": NT, "tn": TN}[mode]

    def body(a_ref, b_ref, o_ref, *acc):
        part = _dot(a_ref[...].astype(BF16), b_ref[...].astype(BF16), dn)
        if nk == 1:
            o_ref[...] = part.astype(o_ref.dtype)
            return
        acc_ref, = acc
        k = pl.program_id(2)

        @pl.when(k == 0)
        def _():
            acc_ref[...] = part

        @pl.when(jnp.logical_and(k > 0, k < nk - 1))
        def _():
            acc_ref[...] += part

        @pl.when(k == nk - 1)
        def _():
            o_ref[...] = (acc_ref[...] + part).astype(o_ref.dtype)

    a_spec = (pl.BlockSpec((tk, tm), lambda i, j, k: (k, i)) if mode == "tn"
              else pl.BlockSpec((tm, tk), lambda i, j, k: (i, k)))
    b_spec = (pl.BlockSpec((tn, tk), lambda i, j, k: (j, k)) if mode == "nt"
              else pl.BlockSpec((tk, tn), lambda i, j, k: (k, j)))
    outs, per_job = _call(
        body, grid=(M // tm, N // tn, nk), in_specs=[a_spec, b_spec],
        out_specs=[pl.BlockSpec((tm, tn), lambda i, j, k: (i, j))], out_shape=[_sds((M, N), out_dtype)],
        scratch_shapes=[pltpu.VMEM((tm, tn), F32)] if nk > 1 else [], name=name,
        sem=("parallel", "parallel", "arbitrary"), args=(a, b), jobs=jobs)
    return outs[0], per_job


def rms_fwd(x, w):
    L, D = x.shape
    T = _pick(L, 256)

    def body(x_ref, w_ref, h_ref):
        xv = x_ref[...]
        r = lax.rsqrt(jnp.mean(xv * xv, axis=-1, keepdims=True) + NORM_EPS)
        h_ref[...] = (xv * r * w_ref[...]).astype(h_ref.dtype)

    return pl.pallas_call(
        body, grid=(L // T,),
        in_specs=[pl.BlockSpec((T, D), lambda i: (i, 0)), pl.BlockSpec((1, D), lambda i: (0, 0))],
        out_specs=pl.BlockSpec((T, D), lambda i: (i, 0)),
        out_shape=_sds((L, D), BF16), name="rms_fwd", compiler_params=_params("parallel"))(x, w)


def post_fwd(x, o, w, jobs=()):
    L, D = x.shape
    T = _pick(L, 256)

    def body(x_ref, o_ref, w_ref, y_ref):
        ov = o_ref[...]
        r = lax.rsqrt(jnp.mean(ov * ov, axis=-1, keepdims=True) + NORM_EPS)
        y_ref[...] = x_ref[...] + ov * r * w_ref[...]

    row = pl.BlockSpec((T, D), lambda i: (i, 0))
    outs, per_job = _call(
        body, grid=(L // T,), in_specs=[row, row, pl.BlockSpec((1, D), lambda i: (0, 0))], out_specs=[row],
        out_shape=[_sds((L, D), F32)], name="post_fwd", sem=("parallel",), args=(x, o, w), jobs=jobs)
    return outs[0], per_job


def _rms_bwd_math(xv, w, dy):
    r = lax.rsqrt(jnp.mean(xv * xv, axis=-1, keepdims=True) + NORM_EPS)
    xhat = xv * r
    g = dy * w
    dx = r * (g - xhat * jnp.mean(g * xhat, axis=-1, keepdims=True))
    return dx, jnp.sum(dy * xhat, axis=0, keepdims=True)


def post_bwd(o, w, dxn):
    L, D = o.shape
    T = _pick(L, 256)

    def body(o_ref, w_ref, d_ref, do_ref, dw_ref):
        dx, dw = _rms_bwd_math(o_ref[...], w_ref[...], d_ref[...])
        do_ref[...] = dx.astype(do_ref.dtype)

        @pl.when(pl.program_id(0) == 0)
        def _():
            dw_ref[...] = jnp.zeros_like(dw_ref)

        dw_ref[...] += dw

    row = pl.BlockSpec((T, D), lambda i: (i, 0))
    vec = pl.BlockSpec((1, D), lambda i: (0, 0))
    return pl.pallas_call(
        body, grid=(L // T,), in_specs=[row, vec, row], out_specs=[row, vec],
        out_shape=[_sds((L, D), BF16), _sds((1, D), F32)],
        name="post_bwd", compiler_params=_params("arbitrary"))(o, w, dxn)


def rms_bwd(x, w, dh_a, dh_b, dxn, jobs=()):
    L, D = x.shape
    T = _pick(L, 256)

    def body(x_ref, w_ref, a_ref, b_ref, d_ref, dx_ref, dw_ref):
        dx, dw = _rms_bwd_math(x_ref[...], w_ref[...], a_ref[...] + b_ref[...])
        dx_ref[...] = d_ref[...] + dx

        @pl.when(pl.program_id(0) == 0)
        def _():
            dw_ref[...] = jnp.zeros_like(dw_ref)

        dw_ref[...] += dw

    row = pl.BlockSpec((T, D), lambda i: (i, 0))
    vec = pl.BlockSpec((1, D), lambda i: (0, 0))
    return _call(
        body, grid=(L // T,), in_specs=[row, vec, row, row, row], out_specs=[row, vec],
        out_shape=[_sds((L, D), F32), _sds((1, D), F32)],
        name="rms_bwd", sem=("arbitrary",), args=(x, w, dh_a, dh_b, dxn), jobs=jobs)


def loss_head(y, target):
    L, D = y.shape
    T = _pick(L, 256)

    def body(y_ref, t_ref, d_ref, s_ref):
        e = y_ref[...] - t_ref[...]
        d_ref[...] = e * (1.0 / D)

        @pl.when(pl.program_id(0) == 0)
        def _():
            s_ref[...] = jnp.zeros_like(s_ref)

        s_ref[...] += jnp.sum(e * e)

    row = pl.BlockSpec((T, D), lambda i: (i, 0))
    return pl.pallas_call(
        body, grid=(L // T,), in_specs=[row, row],
        out_specs=[row, pl.BlockSpec((8, LANES), lambda i: (0, 0))],
        out_shape=[_sds((L, D), F32), _sds((8, LANES), F32)],
        name="loss_head", compiler_params=_params("arbitrary"))(y, target)


def _window_sums(xe, w, back):
    n = xe.shape[0]
    s, k = xe, 1
    while k < w:
        s = s + pltpu.roll(s, k if back else n - k, 0)
        k *= 2
    return s


def pool_fwd(proj, mixw, scale, D):
    L = proj.shape[0]
    PGW = D // GROUPS
    T = _pick(L, 256)
    hb = T // POOL_HALO

    def body(u_ref, halo_ref, g_ref, mw_ref, sc_ref, y_ref, p_ref):
        i = pl.program_id(0)
        u = u_ref[...]
        halo = jnp.where(i > 0, halo_ref[...], 0.0)
        xe = jnp.concatenate([halo, u], axis=0)
        t1 = _row(i * T + 1, (T, 1))
        for g, w in enumerate(POOL_WINDOWS):
            sl = slice(g * PGW, (g + 1) * PGW)
            win = _window_sums(xe[:, sl], w, True)[POOL_HALO:, :]
            cnt = jnp.minimum(t1, w).astype(F32)
            pooled = (win / cnt - u[:, sl]).astype(BF16)
            p_ref[:, sl] = pooled
            mixed = _dot(pooled, mw_ref[g])
            gate = g_ref[:, sl]
            y_ref[:, sl] = (mixed * sc_ref[:, sl] * (gate * _sigmoid(gate))).astype(BF16)

    return pl.pallas_call(
        body, grid=(L // T,),
        in_specs=[pl.BlockSpec((T, D), lambda i: (i, 0)),
                  pl.BlockSpec((POOL_HALO, D), lambda i: (jnp.maximum(i * hb - 1, 0), 0)),
                  pl.BlockSpec((T, D), lambda i: (i, 1)),
                  pl.BlockSpec((GROUPS, PGW, PGW), lambda i: (0, 0, 0)),
                  pl.BlockSpec((1, D), lambda i: (0, 0))],
        out_specs=[pl.BlockSpec((T, D), lambda i: (i, 0)), pl.BlockSpec((T, D), lambda i: (i, 0))],
        out_shape=[_sds((L, 2 * D), BF16), _sds((L, D), BF16)],
        name="pool_fwd", compiler_params=_params("parallel"))(proj, proj, proj, mixw, scale)


def pool_bwd(proj, dmix, pooled, mixw, scale, dproj, D):
    L = proj.shape[0]
    PGW = D // GROUPS
    T = _pick(L, 256)
    hb = T // POOL_HALO
    nT = L // T

    def body(g_ref, gh_ref, dy_ref, dyh_ref, p_ref, mw_ref, sc_ref, old_ref, dp_ref, dm_ref, ds_ref):
        i = pl.program_id(0)
        t1 = _row(i * T + 1, (T, 1))
        th1 = _row((i + 1) * T + 1, (POOL_HALO, 1))
        live = i < nT - 1

        @pl.when(i == 0)
        def _():
            ds_ref[...] = jnp.zeros_like(ds_ref)

        for g, w in enumerate(POOL_WINDOWS):
            sl = slice(g * PGW, (g + 1) * PGW)
            sc = sc_ref[:, sl]
            gate, dy = g_ref[:, sl], dy_ref[:, sl]
            sg, dsg = _silu_and_grad(gate)
            mixed = _dot(p_ref[:, sl], mw_ref[g])
            dmixed = (dy * sc * sg).astype(BF16)
            dm_ref[:, sl] = dmixed
            dp_ref[:, D + g * PGW:D + (g + 1) * PGW] = (dy * mixed * sc * dsg).astype(BF16)
            ds_ref[:, sl] += jnp.sum(dy * mixed * sg, axis=0, keepdims=True)
            dpool = _dot(dmixed, mw_ref[g], NT)
            gate_h = gh_ref[:, sl]
            dmixed_h = (dyh_ref[:, sl] * sc * (gate_h * _sigmoid(gate_h))).astype(BF16)
            dpool_h = jnp.where(live, _dot(dmixed_h, mw_ref[g], NT), 0.0)
            q = dpool / jnp.minimum(t1, w).astype(F32)
            q_h = dpool_h / jnp.minimum(th1, w).astype(F32)
            qe = jnp.concatenate([q, q_h], axis=0)
            dp_ref[:, sl] = (_window_sums(qe, w, False)[:T, :] - dpool).astype(BF16)

    nxt = lambda i: jnp.minimum((i + 1) * hb, L // POOL_HALO - 1)
    row = lambda c: pl.BlockSpec((T, D), lambda i: (i, c))
    return pl.pallas_call(
        body, grid=(nT,),
        in_specs=[row(1), pl.BlockSpec((POOL_HALO, D), lambda i: (nxt(i), 1)),
                  row(0), pl.BlockSpec((POOL_HALO, D), lambda i: (nxt(i), 0)),
                  row(0), pl.BlockSpec((GROUPS, PGW, PGW), lambda i: (0, 0, 0)),
                  pl.BlockSpec((1, D), lambda i: (0, 0)), _ANY],
        out_specs=[pl.BlockSpec((T, 2 * D), lambda i: (i, 0)), row(0), pl.BlockSpec((1, D), lambda i: (0, 0))],
        out_shape=[_sds(dproj.shape, dproj.dtype), _sds((L, D), BF16), _sds((1, D), F32)],
        input_output_aliases={7: 0},
        name="pool_bwd", compiler_params=_params("arbitrary"))(proj, proj, dmix, dmix, pooled, mixw, scale, dproj)


def pool_dw(pooled, dmixed, D):
    L = pooled.shape[0]
    PGW = D // GROUPS
    tk = _pick(L, 1024)
    nk = L // tk

    def body(p_ref, d_ref, o_ref):
        @pl.when(pl.program_id(1) == 0)
        def _():
            o_ref[...] = jnp.zeros_like(o_ref)

        o_ref[0] += _dot(p_ref[...], d_ref[...], TN)

    blk = pl.BlockSpec((tk, PGW), lambda g, k: (k, g))
    return pl.pallas_call(
        body, grid=(GROUPS, nk), in_specs=[blk, blk],
        out_specs=pl.BlockSpec((1, PGW, PGW), lambda g, k: (g, 0, 0)),
        out_shape=_sds((GROUPS, PGW, PGW), F32), name="pool_dw",
        compiler_params=_params("parallel", "arbitrary"))(pooled, dmixed)


def conv_fwd(proj, cw, cb, D):
    L = proj.shape[0]
    C = cw.shape[1]
    assert (3 * D) % C == 0
    cblk = (3 * D) // C
    T = _pick(L, 256)
    hb = T // CONV_HALO

    def body(u_ref, halo_ref, w_ref, b_ref, o_ref):
        i = pl.program_id(0)
        u = u_ref[...]
        xe = jnp.concatenate([jnp.where(i > 0, halo_ref[...], 0.0), u], axis=0)
        acc = b_ref[...] + w_ref[CONV_K - 1:CONV_K, :] * u
        for k in range(CONV_K - 1):
            acc = acc + w_ref[k:k + 1, :] * pltpu.roll(xe, CONV_K - 1 - k, 0)[CONV_HALO:, :]
        o_ref[...] = acc

    return pl.pallas_call(
        body, grid=(L // T,),
        in_specs=[pl.BlockSpec((T, C), lambda i: (i, cblk)),
                  pl.BlockSpec((CONV_HALO, C), lambda i: (jnp.maximum(i * hb - 1, 0), cblk)),
                  pl.BlockSpec((CONV_K, C), lambda i: (0, 0)),
                  pl.BlockSpec((1, C), lambda i: (0, 0))],
        out_specs=pl.BlockSpec((T, C), lambda i: (i, 0)),
        out_shape=_sds((L, C), F32), name="conv_fwd", compiler_params=_params("parallel"))(proj, proj, cw, cb)


def conv_bwd(dparts, proj, cw, dproj, D, jobs=()):
    L = proj.shape[0]
    C = cw.shape[1]
    cblk = (3 * D) // C
    T = _pick(L, 256)
    hb = T // CONV_HALO
    nT = L // T
    widths = [p.shape[1] for p in dparts]
    assert sum(widths) == C
    n = len(dparts)

    def body(*refs):
        d_refs, dn_refs = refs[:n], refs[n:2 * n]
        u_ref, up_ref, w_ref, old_ref, dr_ref, dw_ref, db_ref = refs[2 * n:]
        i = pl.program_id(0)

        @pl.when(i == 0)
        def _():
            dw_ref[...] = jnp.zeros_like(dw_ref)
            db_ref[...] = jnp.zeros_like(db_ref)

        at = 0
        for d_ref, dn_ref, wd in zip(d_refs, dn_refs, widths):
            sl = slice(at, at + wd)
            at += wd
            d = d_ref[...]
            u = u_ref[:, sl]
            de = jnp.concatenate([d, jnp.where(i < nT - 1, dn_ref[...], 0.0)], axis=0)
            ue = jnp.concatenate([jnp.where(i > 0, up_ref[:, sl], 0.0), u], axis=0)
            acc = w_ref[CONV_K - 1:CONV_K, sl] * d
            dw_ref[CONV_K - 1:CONV_K, sl] += jnp.sum(d * u, axis=0, keepdims=True)
            for k in range(CONV_K - 1):
                sh = CONV_K - 1 - k
                acc = acc + w_ref[k:k + 1, sl] * pltpu.roll(de, T + CONV_HALO - sh, 0)[:T, :]
                dw_ref[k:k + 1, sl] += jnp.sum(d * pltpu.roll(ue, sh, 0)[CONV_HALO:, :], axis=0, keepdims=True)
            dr_ref[:, sl] = acc.astype(dr_ref.dtype)
            db_ref[:, sl] += jnp.sum(d, axis=0, keepdims=True)

    nxt = lambda i: jnp.minimum((i + 1) * hb, L // CONV_HALO - 1)
    return _call(
        body, grid=(nT,),
        in_specs=[pl.BlockSpec((T, wd), lambda i: (i, 0)) for wd in widths]
        + [pl.BlockSpec((CONV_HALO, wd), lambda i: (nxt(i), 0)) for wd in widths]
        + [pl.BlockSpec((T, C), lambda i: (i, cblk)),
           pl.BlockSpec((CONV_HALO, C), lambda i: (jnp.maximum(i * hb - 1, 0), cblk)),
           pl.BlockSpec((CONV_K, C), lambda i: (0, 0)), _ANY],
        out_specs=[pl.BlockSpec((T, C), lambda i: (i, cblk)),
                   pl.BlockSpec((8, C), lambda i: (0, 0)),
                   pl.BlockSpec((1, C), lambda i: (0, 0))],
        out_shape=[_sds(dproj.shape, dproj.dtype), _sds((8, C), F32), _sds((1, C), F32)],
        aliases={2 * n + 3: 0}, name="conv_bwd", sem=("arbitrary",),
        args=(*dparts, *dparts, proj, proj, cw, dproj), jobs=jobs)


def _softplus(v):
    y = jnp.exp(-jnp.abs(v))
    u = 1.0 + y
    log1p = jnp.where(u == 1.0, y, jnp.log(u) * y / jnp.where(u == 1.0, 1.0, u - 1.0))
    return jnp.maximum(v, 0.0) + log1p


def dt_prep(dtraw, bias, alog, expand, D):
    L = dtraw.shape[0]
    GC = D // GROUPS
    HPG = GC // HEAD_DIM
    Q = _pick(L, SCAN_CHUNK)
    nc = L // Q

    def body(r_ref, b_ref, a_ref, e_ref, dt_ref, acs_ref, acst_ref, dtx_ref, eax_ref, dsx_ref, cdx_ref):
        valid = lax.broadcasted_iota(jnp.int32, (1, LANES), 1) < HPG
        dt = jnp.where(valid, _softplus(r_ref[...] + b_ref[...]), 0.0)
        adt = dt * -jnp.exp(a_ref[...])
        tril = (_row(0, (Q, Q)) >= lax.broadcasted_iota(jnp.int32, (Q, Q), 1)).astype(BF16)
        acs = _sel_dot(tril, adt)
        last = acs[Q - 1:Q, :]
        dt_ref[...] = dt
        acs_ref[...] = acs
        acst_ref[...] = acs.T
        e = e_ref[...]
        dtx_ref[...] = _dot_sel(dt, e)
        eax_ref[...] = jnp.exp(_dot_sel(acs, e))
        dsx_ref[...] = jnp.exp(_dot_sel(last - acs, e))
        cdx_ref[0] = jnp.exp(_dot_sel(jnp.broadcast_to(last, (8, LANES)), e))

    head = pl.BlockSpec((Q, LANES), lambda g, c: (c, g))
    hvec = pl.BlockSpec((1, LANES), lambda g, c: (0, g))
    chan = pl.BlockSpec((Q, GC), lambda g, c: (c, g))
    return pl.pallas_call(
        body, grid=(GROUPS, nc),
        in_specs=[head, hvec, hvec, pl.BlockSpec((LANES, GC), lambda g, c: (0, 0))],
        out_specs=[head, head, pl.BlockSpec((LANES, Q), lambda g, c: (g, c)), chan, chan, chan,
                   pl.BlockSpec((1, 8, GC), lambda g, c: (c, 0, g))],
        out_shape=[_sds((L, GROUPS * LANES), F32), _sds((L, GROUPS * LANES), F32), _sds((GROUPS * LANES, L), F32),
                   _sds((L, D), F32), _sds((L, D), F32), _sds((L, D), F32), _sds((nc, 8, D), F32)],
        name="dt_prep", compiler_params=_params("parallel", "parallel"))(dtraw, bias, alog, expand)


def _scan_specs(L, D, Q, rev):
    GC = D // GROUPS
    nc = L // Q
    ci = (lambda c: nc - 1 - c) if rev else (lambda c: c)
    return dict(
        xs=pl.BlockSpec((Q, GC), lambda g, c: (ci(c), g)),
        b=pl.BlockSpec((Q, STATE), lambda g, c: (ci(c), D // STATE + g)),
        c=pl.BlockSpec((Q, STATE), lambda g, c: (ci(c), D // STATE + GROUPS + g)),
        chan=pl.BlockSpec((Q, GC), lambda g, c: (ci(c), g)),
        cdx=pl.BlockSpec((1, 8, GC), lambda g, c: (ci(c), 0, g)),
        head=pl.BlockSpec((Q, LANES), lambda g, c: (ci(c), g)),
        headt=pl.BlockSpec((LANES, Q), lambda g, c: (g, ci(c))),
        state=pl.BlockSpec((1, 1, STATE, GC), lambda g, c: (ci(c), g, 0, 0)),
        hvec=pl.BlockSpec((1, LANES), lambda g, c: (0, g)),
        cvec=pl.BlockSpec((1, GC), lambda g, c: (0, g)))


def scan_fwd(pre, dtx, eax, dsx, cdx, acs, acst, proj, dexp, nw, mixed, D):
    L = pre.shape[0]
    GC = D // GROUPS
    Q = _pick(L, SCAN_CHUNK)
    nc = L // Q
    sp = _scan_specs(L, D, Q, False)

    def body(xs_ref, b_ref, c_ref, dtx_ref, eax_ref, dsx_ref, cdx_ref, acs_ref, acst_ref, z_ref, de_ref, nw_ref,
             old_ref, y_ref, st_ref, o_ref, s_scr):
        @pl.when(pl.program_id(1) == 0)
        def _():
            s_scr[...] = jnp.zeros_like(s_scr)

        tri = _row(0, (Q, Q)) >= lax.broadcasted_iota(jnp.int32, (Q, Q), 1)
        half = lax.broadcasted_iota(jnp.int32, (1, LANES), 1) // HEAD_DIM
        xs, _ = _silu_and_grad(xs_ref[...])
        bg = _silu_and_grad(b_ref[...])[0].astype(BF16)
        cg = _silu_and_grad(c_ref[...])[0].astype(BF16)
        xdt = xs * dtx_ref[...]
        sprev = s_scr[...]
        st_ref[0, 0] = sprev
        sc = _dot(cg, bg, NT)
        yoff = _dot(cg, sprev.astype(BF16)) * eax_ref[...]
        for j in range(GC // LANES):
            ps = slice(j * LANES, (j + 1) * LANES)
            xp = xdt[:, ps]
            acc = yoff[:, ps]
            for hh in range(2):
                h = 2 * j + hh
                lm = jnp.exp(jnp.where(tri, acs_ref[:, h:h + 1] - acst_ref[h:h + 1, :], -1e30))
                xm = jnp.where(half == hh, xp, 0.0).astype(BF16)
                acc = acc + _dot((sc * lm).astype(BF16), xm)
            y_ref[:, ps] = acc
        xw = (xdt * dsx_ref[...]).astype(BF16)
        s_scr[...] = cdx_ref[0, 0:1, :] * sprev + _dot(bg, xw, TN)
        z = z_ref[...]
        y3 = (y_ref[...] + de_ref[...] * xs) * (z * _sigmoid(z))
        r = lax.rsqrt(jnp.mean(y3 * y3, axis=-1, keepdims=True) + NORM_EPS)
        o_ref[...] = (y3 * r * nw_ref[...]).astype(o_ref.dtype)

    return pl.pallas_call(
        body, grid=(GROUPS, nc),
        in_specs=[sp["xs"], sp["b"], sp["c"], sp["chan"], sp["chan"], sp["chan"], sp["cdx"], sp["head"], sp["headt"],
                  pl.BlockSpec((Q, GC), lambda g, c: (c, 2 * GROUPS + g)), sp["cvec"], sp["cvec"], _ANY],
        out_specs=[sp["chan"], sp["state"], pl.BlockSpec((Q, GC), lambda g, c: (c, GROUPS + g))],
        out_shape=[_sds((L, D), F32), _sds((nc, GROUPS, STATE, GC), F32), _sds(mixed.shape, mixed.dtype)],
        input_output_aliases={12: 2},
        scratch_shapes=[pltpu.VMEM((STATE, GC), F32)], name="scan_fwd",
        compiler_params=_params("parallel", "arbitrary"))(
            pre, pre, pre, dtx, eax, dsx, cdx, acs, acst, proj, dexp, nw, mixed)


def scan_bwd(pre, dtx, eax, dsx, cdx, acs, acst, dt, dtraw, bias, alog, states, y, proj, dmix, nw, dexp, collapse, D,
             jobs=()):
    L = pre.shape[0]
    GC = D // GROUPS
    Q = _pick(L, SCAN_CHUNK)
    nc = L // Q
    sp = _scan_specs(L, D, Q, True)
    rc = lambda c: nc - 1 - c

    def body(xs_ref, b_ref, c_ref, dtx_ref, eax_ref, dsx_ref, cdx_ref, acs_ref, acst_ref, dt_ref, raw_ref,
             bias_ref, alog_ref, st_ref, y_ref, z_ref, dm_ref, nw_ref, dexp_ref, col_ref,
             dxs_ref, db_ref, dc_ref, ddt_ref, dal_ref, dbi_ref, dz_ref, dnw_ref, dde_ref, ds_scr, dx_scr):
        first = pl.program_id(1) == 0

        @pl.when(first)
        def _():
            ds_scr[...] = jnp.zeros_like(ds_scr)
            dal_ref[...] = jnp.zeros_like(dal_ref)
            dbi_ref[...] = jnp.zeros_like(dbi_ref)
            dnw_ref[...] = jnp.zeros_like(dnw_ref)
            dde_ref[...] = jnp.zeros_like(dde_ref)

        li = _row(0, (Q, Q))
        si = lax.broadcasted_iota(jnp.int32, (Q, Q), 1)
        lane = lax.broadcasted_iota(jnp.int32, (1, LANES), 1)
        half = lane // HEAD_DIM
        xs_pre, b_pre, c_pre = xs_ref[...], b_ref[...], c_ref[...]
        xs, xs_g = _silu_and_grad(xs_pre)
        bf, b_g = _silu_and_grad(b_pre)
        cf, c_g = _silu_and_grad(c_pre)
        bg, cg = bf.astype(BF16), cf.astype(BF16)
        dtx, eax, dsx = dtx_ref[...], eax_ref[...], dsx_ref[...]
        cd = cdx_ref[0, 0:1, :]
        xdt = xs * dtx
        sz, dsz = _silu_and_grad(z_ref[...])
        y2 = y_ref[...] + dexp_ref[...] * xs
        y3 = y2 * sz
        r = lax.rsqrt(jnp.mean(y3 * y3, axis=-1, keepdims=True) + NORM_EPS)
        n = y3 * r
        dm = dm_ref[...]
        gg = dm * nw_ref[...]
        dy3 = r * (gg - n * jnp.mean(gg * n, axis=-1, keepdims=True))
        dnw_ref[0:1, :] += jnp.sum(dm * n, axis=0, keepdims=True)
        G = dy3 * sz
        dz_ref[...] = (dy3 * y2 * dsz).astype(dz_ref.dtype)
        dde_ref[0:1, :] += jnp.sum(G * xs, axis=0, keepdims=True)
        prev = st_ref[0, 0]
        dsn = ds_scr[...]
        prev_b, dsn_b = prev.astype(BF16), dsn.astype(BF16)
        cp = _dot(cg, prev_b)
        ge_b = (G * eax).astype(BF16)
        d_c = _dot(ge_b, prev_b, NT)
        dprev = _dot(cg, ge_b, TN) + cd * dsn
        chan_a = G * cp * eax
        xw_b = (xdt * dsx).astype(BF16)
        dcd = jnp.sum(prev * dsn, axis=0, keepdims=True)
        d_b = _dot(xw_b, dsn_b, NT)
        dxw = _dot(bg, dsn_b)
        dd = dxw * xdt * dsx
        chan_a = chan_a - dd
        last_c = jnp.sum(dd, axis=0, keepdims=True) + dcd * cd
        sc = _dot(cg, bg, NT)
        head_row = _row(0, (LANES, 1))
        dsc = jnp.zeros((Q, Q), F32)
        dacs = jnp.zeros((Q, LANES), F32)
        colsums = jnp.zeros((LANES, Q), F32)
        for j in range(GC // LANES):
            ps = slice(j * LANES, (j + 1) * LANES)
            xp, gp = xdt[:, ps], G[:, ps]
            dxp = dxw[:, ps] * dsx[:, ps]
            for hh in range(2):
                h = 2 * j + hh
                lm = jnp.exp(jnp.where(li >= si, acs_ref[:, h:h + 1] - acst_ref[h:h + 1, :], -1e30))
                m = sc * lm
                xm = jnp.where(half == hh, xp, 0.0).astype(BF16)
                gm = jnp.where(half == hh, gp, 0.0).astype(BF16)
                dm = _dot(gm, xm, NT)
                dxp = dxp + _dot(m.astype(BF16), gm, TN)
                dsc = dsc + dm * lm
                w = dm * m
                dacs = dacs + jnp.where(lane == h, jnp.sum(w, axis=1, keepdims=True), 0.0)
                colsums = jnp.where(head_row == h, jnp.sum(w, axis=0, keepdims=True), colsums)
            dx_scr[:, ps] = dxp
        dacs = dacs - colsums.T
        dsc_b = dsc.astype(BF16)
        d_c = d_c + _dot(dsc_b, bg)
        d_b = d_b + _dot(dsc_b, cg, TN)
        ds_scr[...] = dprev
        dxdt = dx_scr[...]
        dxs_ref[...] = (dxdt * dtx + dexp_ref[...] * G) * xs_g
        db_ref[...] = d_b * b_g
        dc_ref[...] = d_c * c_g
        colm = col_ref[...]
        dacs = dacs + _dot_sel(chan_a, colm)
        dlast = _dot_sel(jnp.broadcast_to(last_c, (8, GC)), colm)[0:1, :]
        dacs = dacs + jnp.where(_row(0, (Q, 1)) == Q - 1, dlast, 0.0)
        dadt = _sel_dot((si >= li).astype(BF16), dacs)
        a = -jnp.exp(alog_ref[...])
        dt = dt_ref[...]
        ddt = dadt * a + _dot_sel(dxdt * xs, colm)
        dal_ref[0:1, :] += jnp.sum(dadt * dt * a, axis=0, keepdims=True)
        draw = ddt * _sigmoid(raw_ref[...] + bias_ref[...])
        dbi_ref[0:1, :] += jnp.sum(draw, axis=0, keepdims=True)
        ddt_ref[...] = draw.astype(ddt_ref.dtype)

    acc = pl.BlockSpec((8, LANES), lambda g, c: (0, g))
    cacc = pl.BlockSpec((8, GC), lambda g, c: (0, g))
    return _call(
        body, grid=(GROUPS, nc),
        in_specs=[sp["xs"], sp["b"], sp["c"], sp["chan"], sp["chan"], sp["chan"], sp["cdx"], sp["head"], sp["headt"],
                  sp["head"], sp["head"], sp["hvec"], sp["hvec"], sp["state"], sp["chan"],
                  pl.BlockSpec((Q, GC), lambda g, c: (rc(c), 2 * GROUPS + g)),
                  pl.BlockSpec((Q, GC), lambda g, c: (rc(c), GROUPS + g)), sp["cvec"], sp["cvec"],
                  pl.BlockSpec((GC, LANES), lambda g, c: (0, 0))],
        out_specs=[sp["chan"],
                   pl.BlockSpec((Q, STATE), lambda g, c: (rc(c), g)),
                   pl.BlockSpec((Q, STATE), lambda g, c: (rc(c), g)),
                   sp["head"], acc, acc,
                   pl.BlockSpec((Q, GC), lambda g, c: (rc(c), 2 * GROUPS + g)), cacc, cacc],
        out_shape=[_sds((L, D), F32), _sds((L, GROUPS * STATE), F32), _sds((L, GROUPS * STATE), F32),
                   _sds((L, GROUPS * LANES), BF16), _sds((8, GROUPS * LANES), F32), _sds((8, GROUPS * LANES), F32),
                   _sds((L, proj.shape[1]), BF16), _sds((8, D), F32), _sds((8, D), F32)],
        scratch_shapes=[pltpu.VMEM((STATE, GC), F32), pltpu.VMEM((Q, GC), F32)], name="scan_bwd",
        sem=("parallel", "arbitrary"),
        args=(pre, pre, pre, dtx, eax, dsx, cdx, acs, acst, dt, dtraw, bias, alog, states, y, proj, dmix, nw, dexp,
              collapse),
        jobs=jobs)


def _adam_math(gv, w, m, v):
    c1 = 1.0 - ADAM_B1 ** ADAM_STEP
    c2 = 1.0 - ADAM_B2 ** ADAM_STEP
    nm = ADAM_B1 * m + (1.0 - ADAM_B1) * gv
    nv = ADAM_B2 * v + (1.0 - ADAM_B2) * (gv * gv)
    return -ADAM_LR * ((nm / c1) / (jnp.sqrt(nv / c2) + ADAM_EPS) + ADAM_WD * w), nm, nv


def adamw(g, w, m, v, name):
    R, C = g.shape
    T = R if R <= 128 else 128
    assert R % T == 0

    def body(g_ref, w_ref, m_ref, v_ref, d_ref, nm_ref, nv_ref):
        d_ref[...], nm_ref[...], nv_ref[...] = _adam_math(g_ref[...], w_ref[...], m_ref[...], v_ref[...])

    blk = pl.BlockSpec((T, C), lambda i: (i, 0))
    return pl.pallas_call(
        body, grid=(R // T,), in_specs=[blk] * 4, out_specs=[blk] * 3,
        out_shape=[_sds((R, C), F32)] * 3, name=name, compiler_params=_params("parallel"))(g, w, m, v)


def adamw_layer(g, w, m, v, layer, prev, name):
    R, C = g.shape
    T = _rows_tile(R, 256)
    assert R % T == 0

    def body(g_ref, w_ref, m_ref, v_ref, *rest):
        go_ref, d_ref, nm_ref, nv_ref = rest[-4:]
        gv = g_ref[...]
        go_ref[0] = gv
        d_ref[0], nm_ref[0], nv_ref[0] = _adam_math(gv, w_ref[0], m_ref[0], v_ref[0])

    mine = pl.BlockSpec((1, T, C), lambda i: (layer, i, 0))
    prev = list(prev or [])
    return pl.pallas_call(
        body, grid=(R // T,), in_specs=[pl.BlockSpec((T, C), lambda i: (i, 0)), mine, mine, mine] + [_ANY] * len(prev),
        out_specs=[mine] * 4, out_shape=[_sds(w.shape, F32)] * 4,
        input_output_aliases={4 + i: i for i in range(len(prev))}, name=name,
        compiler_params=_params("parallel"))(g, w, m, v, *prev)


def cast_bf16(a):
    R, C = a.shape
    T = _rows_tile(R, 256)

    def body(a_ref, o_ref):
        o_ref[...] = a_ref[...].astype(BF16)

    blk = pl.BlockSpec((T, C), lambda i: (i, 0))
    return pl.pallas_call(body, grid=(R // T,), in_specs=[blk], out_specs=blk, out_shape=_sds((R, C), BF16),
                          name="cast_bf16", compiler_params=_params("parallel"))(a)


def _to_groups(v, hpg):
    lead = v.shape[:-1]
    t = v.reshape(lead + (GROUPS, hpg))
    t = jnp.pad(t, [(0, 0)] * (len(lead) + 1) + [(0, LANES - hpg)])
    return t.reshape(lead + (GROUPS * LANES,))


def _from_groups(a, hpg):
    lead = a.shape[:-1]
    return a.reshape(lead + (GROUPS, LANES))[..., :hpg].reshape(lead + (GROUPS * hpg,))


def _expand_matrix(D):
    gc = D // GROUPS
    return (jnp.arange(LANES)[:, None] == (jnp.arange(gc)[None, :] // HEAD_DIM)).astype(BF16)


def layer_params(pre_w, w_in_t, mixw, scale, cw, cb, bias, alog, dskip, nw, w_out_full, post_w, D):
    hpg = D // GROUPS // HEAD_DIM
    main = w_in_t.shape[0] - GROUPS * hpg
    wdt_t = _to_groups(w_in_t[main:].T, hpg).T
    return dict(
        pre_w=pre_w[None], win_t=w_in_t, main=main, wdt_t=wdt_t, mixw=mixw,
        scale=scale[None], cw=cw, cb=cb[None], bias=_to_groups(bias, hpg)[None], alog=_to_groups(alog, hpg)[None],
        dexp=jnp.repeat(dskip, HEAD_DIM)[None], nw=nw[None], wout=w_out_full, post_w=post_w[None])


def layer_fwd(x, p, D, next_shards=None):
    send_in = [gather_send_job(next_shards[:1], [True])] if next_shards else []
    h = rms_fwd(x, p["pre_w"])
    proj, got = matmul(h, p["win_t"], "nt", F32, "proj", tm=1024, jobs=send_in, n_out=p["main"])
    dtraw, _ = matmul(h, p["wdt_t"], "nt", F32, "dtproj")
    mixed, pooled = pool_fwd(proj, p["mixw"], p["scale"], D)
    pre = conv_fwd(proj, p["cw"], p["cb"], D)
    dtp = dt_prep(dtraw, p["bias"], p["alog"], _expand_matrix(D), D)
    dt, acs, acst, dtx, eax, dsx, cdx = dtp
    y, states, mixed = scan_fwd(pre, dtx, eax, dsx, cdx, acs, acst, proj, p["dexp"], p["nw"], mixed, D)
    jobs = [gather_send_job(next_shards[1:]), gather_pass_job(got[0], [True])] if next_shards else []
    out, got = matmul(mixed, p["wout"], "nn", F32, "outproj", tm=1024, jobs=jobs)
    xn, got2 = post_fwd(x, out, p["post_w"], jobs=[gather_pass_job(got[0])] if next_shards else [])
    gathered = got[1] + got2[0] if next_shards else None
    return xn, dict(x=x, h=h, proj=proj, dtraw=dtraw, pooled=pooled, pre=pre, dtp=dtp, y=y, states=states,
                    mixed=mixed, out=out), gathered


def layer_bwd(dxn, p, s, D, where=None):
    reduce = where is not None
    chip, core = where if reduce else (None, None)
    hpg = D // GROUPS // HEAD_DIM
    PGW = D // GROUPS
    main = p["main"]
    SH = (main + GROUPS * hpg) // 4
    dt, acs, acst, dtx, eax, dsx, cdx = s["dtp"]
    dout, d_post = post_bwd(s["out"], p["post_w"], dxn)
    part = BF16 if reduce else F32
    d_wout, _ = matmul(s["mixed"], dout, "tn", part, "dwout", tm=1024)
    g_out = d_wout.reshape(4, 2, D // 4, D)
    dmix, got = matmul(dout, p["wout"], "nt", F32, "dmixed", tm=1024,
                       jobs=[pair_exchange_job([g_out])] if reduce else [])
    pair_out = pair_add(g_out, got[0][0], core, BF16) if reduce else None
    (dxs, db, dc, ddtraw, d_alog, d_bias, dproj, d_nw, d_dexp), got = scan_bwd(
        s["pre"], dtx, eax, dsx, cdx, acs, acst, dt, s["dtraw"], p["bias"], p["alog"], s["states"], s["y"],
        s["proj"], dmix, p["nw"], p["dexp"], _expand_matrix(D).T, D,
        jobs=[chip_exchange_job([pair_out])] if reduce else [])
    mine_out = chip_add(pair_out, got[0][0], chip, core) if reduce else None
    (dproj, d_cw, d_cb), got = conv_bwd([dxs, db, dc], s["proj"], p["cw"], dproj, D,
                                        jobs=[pair_gather_job([mine_out])] if reduce else [])
    r_out = got[0][0] if reduce else None
    dproj, dmixed, d_scale = pool_bwd(s["proj"], dmix, s["pooled"], p["mixw"], p["scale"], dproj, D)
    d_mixw = pool_dw(s["pooled"], dmixed, D)
    d_wmain_t, _ = matmul(dproj, s["h"], "tn", part, "dwmain", tm=1024)
    d_wdt_t, _ = matmul(ddtraw, s["h"], "tn", part, "dwdt")
    late = [d_wmain_t[None], d_wdt_t[None],
            d_mixw.reshape(GROUPS, 4, PGW // 4, PGW).transpose(1, 0, 2, 3).reshape(4, 2, GROUPS * PGW // 8, PGW)]
    cols = [True, True, False]
    dh_b, got = matmul(ddtraw, p["wdt_t"], "nn", F32, "dh_dt",
                       jobs=[pair_exchange_job(late, cols)] if reduce else [])
    if reduce:
        p_main, p_dt, p_mix = [pair_add(g, r, core, BF16, k) for g, r, k in zip(late, got[0], cols)]
        p_in = jnp.concatenate([p_main[0], _from_groups(p_dt[0].T, hpg).T], axis=0)
        pairs = [p_in.reshape(4, SH, D // 2), p_mix]
    dh_a, got = matmul(dproj, p["win_t"], "nn", F32, "dh_main", jobs=[chip_exchange_job(pairs)] if reduce else [])
    mines = [chip_add(q, r, chip, core, k) for q, r, k in zip(pairs, got[0], [True, False])] if reduce else None
    (dx, d_pre), got = rms_bwd(s["x"], p["pre_w"], dh_a, dh_b, dxn,
                               jobs=[pair_gather_job(mines, [True, False])] if reduce else [])
    reduced = dict(w_in=got[0][0], w_out=r_out, pool_mix_w=got[0][1]) if reduce else None
    grads = dict(
        pre_norm_w=d_pre[0], pool_scale=d_scale[0], conv_w=d_cw[:CONV_K], conv_b=d_cb[0],
        dt_bias=_from_groups(d_bias[0], hpg), a_log=_from_groups(d_alog[0], hpg),
        d_skip=d_dexp[0].reshape(-1, HEAD_DIM).sum(axis=-1), ssd_norm_w=d_nw[0], post_norm_w=d_post[0])
    if not reduce:
        grads.update(w_in=jnp.concatenate([d_wmain_t.T, _from_groups(d_wdt_t.T, hpg)], axis=1), pool_mix_w=d_mixw,
                     w_out=d_wout)
    return dx, grads, reduced


def local_step(x, target, params, D):
    saved = []
    for p in params:
        x, s, _ = layer_fwd(x, p, D)
        saved.append(s)
    dx, sumsq = loss_head(x, target)
    grads = [None] * len(params)
    for l in reversed(range(len(params))):
        dx, grads[l], _ = layer_bwd(dx, params[l], saved[l], D)
    return sumsq, dx, grads


SMALL = ("pre_norm_w", "pool_scale", "conv_w", "conv_b", "dt_bias", "a_log", "d_skip", "ssd_norm_w", "post_norm_w")
BIG = ("w_in", "w_out", "pool_mix_w")


def _pack(parts):
    flat = jnp.concatenate([p.reshape(-1) for p in parts])
    n = flat.shape[0]
    rows = -(-n // (LANES * LANES)) * LANES
    return jnp.pad(flat, (0, rows * LANES - n)).reshape(rows, LANES)


def _unpack(packed, shapes):
    flat, out, at = packed.reshape(-1), [], 0
    for s in shapes:
        n = math.prod(s)
        out.append(flat[at:at + n].reshape(s))
        at += n
    return out


def kernel(x, pre_norm_w, w_in, pool_mix_w, pool_scale, conv_w, conv_b, dt_bias, a_log, d_skip, ssd_norm_w, w_out, post_norm_w, loss_target, m_pre_norm_w, m_w_in, m_pool_mix_w, m_pool_scale, m_conv_w, m_conv_b, m_dt_bias, m_a_log, m_d_skip, m_ssd_norm_w, m_w_out, m_post_norm_w, v_pre_norm_w, v_w_in, v_pool_mix_w, v_pool_scale, v_conv_w, v_conv_b, v_dt_bias, v_a_log, v_d_skip, v_ssd_norm_w, v_w_out, v_post_norm_w):
    NL, D, SH = w_in.shape
    PGW = D // GROUPS
    CS = conv_w.shape[2]
    chip = (2 * lax.axis_index("x") + lax.axis_index("y")).astype(jnp.int32)
    chip1, core = chip.reshape(1), lax.axis_index("c").astype(jnp.int32).reshape(1)

    tr = lambda t: jnp.transpose(t, (0, 2, 1))
    w_in_t, m_w_in_t, v_w_in_t = tr(w_in), tr(m_w_in), tr(v_w_in)
    halved_by_cols = [True, False, False, False]

    def shards(l):
        return [cast_bf16(w_in_t[l]), cast_bf16(w_out[l]).reshape(2, D // 4, D),
                cast_bf16(pool_mix_w[l].reshape(GROUPS * PGW // 4, PGW)).reshape(2, GROUPS * PGW // 8, PGW),
                conv_w[l].reshape(2, CONV_K * CS // (2 * LANES), LANES)]

    def params(l, g):
        g_in, g_out, g_mix, g_cw = g
        mix_l = g_mix.reshape(4, GROUPS, PGW // 4, PGW).transpose(1, 0, 2, 3).reshape(GROUPS, PGW, PGW)
        cw_l = g_cw.reshape(4, CONV_K, CS).transpose(1, 0, 2).reshape(CONV_K, 4 * CS)
        return layer_params(pre_norm_w[l], g_in.reshape(4 * SH, D), mix_l, pool_scale[l], cw_l, conv_b[l], dt_bias[l],
                            a_log[l], d_skip[l], ssd_norm_w[l], g_out.reshape(2 * D, D), post_norm_w[l], D)

    gathered = run_jobs([gather_send_job(shards(0), halved_by_cols)], "gather_send")[0]
    gathered = run_jobs([gather_pass_job(gathered, halved_by_cols)], "gather_pass")[0]
    h, ps, saved = x[0], [], []
    for l in range(NL):
        ps.append(params(l, gathered))
        h, s, gathered = layer_fwd(h, ps[l], D, shards(l + 1) if l + 1 < NL else None)
        saved.append(s)
    dx, sumsq = loss_head(h, loss_target[0])

    given = dict(w_in=(w_in_t, m_w_in_t, v_w_in_t), w_out=(w_out, m_w_out, v_w_out),
                 pool_mix_w=(pool_mix_w, m_pool_mix_w, v_pool_mix_w))
    flat = {n: [t.reshape(NL, -1, t.shape[-1]) for t in given[n]] for n in BIG}
    done = {n: None for n in BIG}
    grads = [None] * NL
    for l in reversed(range(NL)):
        dx, grads[l], reduced = layer_bwd(dx, ps[l], saved[l], D, (chip1, core))
        for n in BIG:
            r = reduced[n]
            done[n] = adamw_layer(r.reshape(-1, r.shape[-1]), *flat[n], l, done[n], "adamw_" + n)

    small_shapes = [(NL,) + grads[0][n].shape for n in SMALL]
    packed = _pack([0.5 / D * sumsq[0, :1]] + [jnp.stack([g[n] for g in grads]) for n in SMALL])
    total = allreduce_small(packed)
    loss, *small = _unpack(total, [(1,)] + small_shapes)
    small = dict(zip(SMALL, small))
    small["conv_w"] = lax.dynamic_slice_in_dim(small["conv_w"], chip * CS, CS, axis=2)

    given_small = dict(
        pre_norm_w=(pre_norm_w, m_pre_norm_w, v_pre_norm_w), pool_scale=(pool_scale, m_pool_scale, v_pool_scale),
        conv_w=(conv_w, m_conv_w, v_conv_w), conv_b=(conv_b, m_conv_b, v_conv_b),
        dt_bias=(dt_bias, m_dt_bias, v_dt_bias), a_log=(a_log, m_a_log, v_a_log),
        d_skip=(d_skip, m_d_skip, v_d_skip), ssd_norm_w=(ssd_norm_w, m_ssd_norm_w, v_ssd_norm_w),
        post_norm_w=(post_norm_w, m_post_norm_w, v_post_norm_w))
    shapes = [given_small[n][0].shape for n in SMALL]
    upd = adamw(_pack([small[n] for n in SMALL]), *[_pack([given_small[n][i] for n in SMALL]) for i in range(3)],
                "adamw_small")
    upd = [dict(zip(SMALL, _unpack(u, shapes))) for u in upd]

    out = {n: (small[n], upd[0][n], upd[1][n], upd[2][n]) for n in SMALL}
    for n in BIG:
        out[n] = tuple(t.reshape(given[n][0].shape) for t in done[n])
    out["w_in"] = tuple(tr(t) for t in out["w_in"])

    order = ("pre_norm_w", "w_in", "pool_mix_w", "pool_scale", "conv_w", "conv_b", "dt_bias", "a_log", "d_skip",
             "ssd_norm_w", "w_out", "post_norm_w")
    return (loss.reshape(()), dx[None], *[out[n][0] for n in order], *[out[n][1] for n in order],
            *[out[n][2] for n in order], *[out[n][3] for n in order])
```

```python
import functools
import math

import jax
import jax.numpy as jnp
from jax import lax
from jax.experimental import pallas as pl
from jax.experimental.pallas import tpu as pltpu

F32 = jnp.float32
BF16 = jnp.bfloat16

NORM_EPS = 1e-6
HEAD_DIM = 64
STATE = 128
GROUPS = 4
POOL_WINDOWS = (2, 4, 8, 16)
POOL_HALO = 16
CONV_K = 4
CONV_HALO = 8
SCAN_CHUNK = 256
LANES = 128
VMEM_LIMIT = 52 * 1024 * 1024

ADAM_LR = 0.001
ADAM_B1 = 0.9
ADAM_B2 = 0.999
ADAM_EPS = 1e-08
ADAM_WD = 0.01
ADAM_STEP = 10

MESH = pl.DeviceIdType.MESH

NN = (((1,), (0,)), ((), ()))
NT = (((1,), (1,)), ((), ()))
TN = (((0,), (0,)), ((), ()))

_ANY = pl.BlockSpec(memory_space=pl.ANY)


def _params(*sem):
    return pltpu.CompilerParams(dimension_semantics=sem, vmem_limit_bytes=VMEM_LIMIT)


def _pick(dim, pref):
    if dim <= pref:
        return dim
    t = (pref // LANES) * LANES
    while t > LANES and dim % t:
        t -= LANES
    assert dim % t == 0, (dim, pref)
    return t


def _rows_tile(rows, pref):
    t = (min(pref, rows) // 8) * 8
    while t >= 8 and rows % t:
        t -= 8
    return t if t >= 8 else rows


def _dot(a, b, dn=NN):
    return lax.dot_general(a, b, dn, preferred_element_type=F32)


def _split3(a):
    hi = a.astype(BF16)
    r = a - hi.astype(F32)
    mid = r.astype(BF16)
    return hi, mid, (r - mid.astype(F32)).astype(BF16)


def _dot_sel(a, e):
    hi, mid, lo = _split3(a)
    return (_dot(lo, e) + _dot(mid, e)) + _dot(hi, e)


def _sel_dot(e, b):
    hi, mid, lo = _split3(b)
    return (_dot(e, lo) + _dot(e, mid)) + _dot(e, hi)


def _sigmoid(v):
    return 0.5 * jnp.tanh(0.5 * v) + 0.5


def _silu_and_grad(v):
    s = _sigmoid(v)
    return v * s, s * (1.0 + v * (1.0 - s))


def _row(i, shape):
    return lax.broadcasted_iota(jnp.int32, shape, 0) + i


def _sds(shape, dtype):
    return jax.ShapeDtypeStruct(tuple(shape), dtype)


class Job:
    def __init__(self, ins, outs, aliased, nsem, start, finish):
        self.ins, self.outs, self.aliased, self.nsem, self.start, self.finish = ins, outs, aliased, nsem, start, finish


def _place():
    x, y, c = lax.axis_index("x"), lax.axis_index("y"), lax.axis_index("c")
    return x, y, c, [(1 - x, y), (x, 1 - y), (1 - x, 1 - y)]


def _remote(src, dst, send_sem, recv_sem, device):
    return pltpu.make_async_remote_copy(src_ref=src, dst_ref=dst, send_sem=send_sem, recv_sem=recv_sem,
                                        device_id=device, device_id_type=MESH)


def _half(ref, c, cols, lead=0):
    idx = [slice(None)] * lead
    if cols:
        w = ref.shape[-1] // 2
        idx += [slice(None)] * (len(ref.shape) - lead - 1) + [pl.ds(pl.multiple_of(c * w, LANES), w)]
    else:
        idx += [c]
    return ref.at[tuple(idx)]


def _flags(cols, n):
    return list(cols) if cols else [False] * n


def gather_send_job(arrs, cols=None):
    n = len(arrs)
    cols = _flags(cols, n)

    def copies(ins, outs, send, recv):
        x, y, c, chips = _place()
        mine = 2 * x + y
        out = []
        for a in range(n):
            out.append(_remote(ins[a], outs[a].at[mine], send.at[4 * a + 3], recv.at[4 * a + 3], (x, y, 1 - c)))
            for j, chip in enumerate(chips):
                out.append(_remote(_half(ins[a], c, cols[a]), _half(outs[a].at[mine], c, cols[a]),
                                   send.at[4 * a + j], recv.at[4 * a + j], (*chip, c)))
        return out

    def start(ins, outs, send, recv):
        for cp in copies(ins, outs, send, recv):
            cp.start()

    def finish(ins, outs, send, recv):
        x, y, c, chips = _place()
        for a in range(n):
            for j, chip in enumerate(chips):
                landed = _half(outs[a].at[2 * chip[0] + chip[1]], c, cols[a])
                _remote(landed, landed, send.at[4 * a + j], recv.at[4 * a + j], (x, y, 1 - c)).wait_recv()
            twin = outs[a].at[2 * x + y]
            _remote(twin, twin, send.at[4 * a + 3], recv.at[4 * a + 3], (x, y, 1 - c)).wait_recv()
        for cp in copies(ins, outs, send, recv):
            cp.wait_send()

    return Job(list(arrs), [_sds((4,) + a.shape, a.dtype) for a in arrs], False, 4 * n, start, finish)


def gather_pass_job(bufs, cols=None):
    n = len(bufs)
    cols = _flags(cols, n)

    def copies(outs, send, recv):
        x, y, c, chips = _place()
        out = []
        for a in range(n):
            for j, chip in enumerate(chips):
                landed = _half(outs[a].at[2 * chip[0] + chip[1]], c, cols[a])
                out.append(_remote(landed, landed, send.at[3 * a + j], recv.at[3 * a + j], (x, y, 1 - c)))
        return out

    def start(ins, outs, send, recv):
        for cp in copies(outs, send, recv):
            cp.start()

    def finish(ins, outs, send, recv):
        x, y, c, chips = _place()
        for a in range(n):
            for j, chip in enumerate(chips):
                passed = _half(outs[a].at[2 * chip[0] + chip[1]], 1 - c, cols[a])
                _remote(passed, passed, send.at[3 * a + j], recv.at[3 * a + j], (x, y, 1 - c)).wait_recv()
        for cp in copies(outs, send, recv):
            cp.wait_send()

    return Job(list(bufs), [_sds(b.shape, b.dtype) for b in bufs], True, 3 * n, start, finish)


def pair_exchange_job(arrs, cols=None):
    n = len(arrs)
    cols = _flags(cols, n)

    def copies(ins, outs, send, recv):
        x, y, c, _ = _place()
        return [_remote(_half(ins[a], 1 - c, cols[a], 1), outs[a], send.at[a], recv.at[a], (x, y, 1 - c))
                for a in range(n)]

    def start(ins, outs, send, recv):
        for cp in copies(ins, outs, send, recv):
            cp.start()

    def finish(ins, outs, send, recv):
        for cp in copies(ins, outs, send, recv):
            cp.wait()

    shape = lambda a, k: a.shape[:-1] + (a.shape[-1] // 2,) if k else a.shape[:1] + a.shape[2:]
    return Job(list(arrs), [_sds(shape(a, k), a.dtype) for a, k in zip(arrs, cols)], False, n, start, finish)


def chip_exchange_job(arrs):
    n = len(arrs)

    def copies(ins, outs, send, recv):
        x, y, c, chips = _place()
        return [_remote(ins[a].at[2 * chip[0] + chip[1]], outs[a].at[j], send.at[3 * a + j], recv.at[3 * a + j],
                        (*chip, c)) for a in range(n) for j, chip in enumerate(chips)]

    def start(ins, outs, send, recv):
        for cp in copies(ins, outs, send, recv):
            cp.start()

    def finish(ins, outs, send, recv):
        for cp in copies(ins, outs, send, recv):
            cp.wait()

    return Job(list(arrs), [_sds((3,) + a.shape[1:], a.dtype) for a in arrs], False, 3 * n, start, finish)


def pair_gather_job(bufs, cols=None):
    n = len(bufs)
    cols = _flags(cols, n)

    def copies(outs, send, recv):
        x, y, c, _ = _place()
        return [_remote(_half(outs[a], c, cols[a]), _half(outs[a], c, cols[a]), send.at[a], recv.at[a],
                        (x, y, 1 - c)) for a in range(n)]

    def start(ins, outs, send, recv):
        for cp in copies(outs, send, recv):
            cp.start()

    def finish(ins, outs, send, recv):
        for cp in copies(outs, send, recv):
            cp.wait()

    return Job(list(bufs), [_sds(b.shape, b.dtype) for b in bufs], True, n, start, finish)


def _call(body, *, grid, in_specs, out_specs, out_shape, name, sem, args, scratch_shapes=(), jobs=(), aliases=None):
    in_specs, out_specs, out_shape, scratch_shapes = list(in_specs), list(out_specs), list(out_shape), list(scratch_shapes)
    aliases = dict(aliases or {})
    n_in, n_out, n_scr = len(in_specs), len(out_specs), len(scratch_shapes)
    if jobs:
        sem = ("arbitrary",) * len(grid)
    at_in, at_out = n_in, n_out
    for j in jobs:
        if j.aliased:
            aliases.update({at_in + i: at_out + i for i in range(len(j.ins))})
        at_in, at_out = at_in + len(j.ins), at_out + len(j.outs)

    def wrapped(*refs):
        ins, p = refs[:n_in], n_in
        jins = []
        for j in jobs:
            jins.append(refs[p:p + len(j.ins)])
            p += len(j.ins)
        outs, p = refs[p:p + n_out], p + n_out
        jouts = []
        for j in jobs:
            jouts.append(refs[p:p + len(j.outs)])
            p += len(j.outs)
        scr, sems = refs[p:p + n_scr], refs[p + n_scr:]

        def start():
            for k, j in enumerate(jobs):
                j.start(jins[k], jouts[k], sems[2 * k], sems[2 * k + 1])

        def finish():
            for k, j in enumerate(jobs):
                j.finish(jins[k], jouts[k], sems[2 * k], sems[2 * k + 1])

        if jobs and grid:
            ids = [pl.program_id(d) for d in range(len(grid))]
            pl.when(functools.reduce(jnp.logical_and, [i == 0 for i in ids]))(start)
            body(*ins, *outs, *scr)
            pl.when(functools.reduce(jnp.logical_and, [i == g - 1 for i, g in zip(ids, grid)]))(finish)
        else:
            start()
            body(*ins, *outs, *scr)
            finish()

    kwargs = dict(grid=grid) if grid else {}
    res = pl.pallas_call(
        wrapped, in_specs=in_specs + [_ANY] * (at_in - n_in), out_specs=out_specs + [_ANY] * (at_out - n_out),
        out_shape=out_shape + [o for j in jobs for o in j.outs],
        scratch_shapes=scratch_shapes + [pltpu.SemaphoreType.DMA((j.nsem,)) for j in jobs for _ in range(2)],
        input_output_aliases=aliases, name=name,
        compiler_params=pltpu.CompilerParams(dimension_semantics=sem, vmem_limit_bytes=VMEM_LIMIT) if grid
        else pltpu.CompilerParams(vmem_limit_bytes=VMEM_LIMIT), **kwargs)(*args, *[a for j in jobs for a in j.ins])
    res = list(res)
    outs, rest, per_job = res[:n_out], res[n_out:], []
    for j in jobs:
        per_job.append(rest[:len(j.outs)])
        rest = rest[len(j.outs):]
    return outs, per_job


def run_jobs(jobs, name):
    return _call(lambda: None, grid=(), in_specs=[], out_specs=[], out_shape=[], name=name, sem=(), args=(), jobs=jobs)[1]


def pair_add(g, r, core, out_dtype, cols=False):
    S, R, C = r.shape
    T = _rows_tile(R, 256 if C <= 4096 else 128)

    def body(c_ref, g_ref, r_ref, o_ref):
        o_ref[0] = ((g_ref[0] if cols else g_ref[0, 0]).astype(F32) + r_ref[0].astype(F32)).astype(o_ref.dtype)

    g_spec = (pl.BlockSpec((1, T, C), lambda k, i, c_ref: (k, i, c_ref[0])) if cols
              else pl.BlockSpec((1, 1, T, C), lambda k, i, c_ref: (k, c_ref[0], i, 0)))
    return pl.pallas_call(
        body,
        grid_spec=pltpu.PrefetchScalarGridSpec(
            num_scalar_prefetch=1, grid=(S, R // T),
            in_specs=[g_spec, pl.BlockSpec((1, T, C), lambda k, i, c_ref: (k, i, 0))],
            out_specs=pl.BlockSpec((1, T, C), lambda k, i, c_ref: (k, i, 0))),
        out_shape=_sds((S, R, C), out_dtype), name="pair_add",
        compiler_params=_params("parallel", "parallel"))(core, g, r)


def chip_add(p, r, chip, core, cols=False):
    _, R, C = p.shape
    T = _rows_tile(R, 256)

    def body(k_ref, c_ref, p_ref, r0_ref, r1_ref, r2_ref, o_ref):
        s = ((p_ref[0].astype(F32) + r0_ref[0].astype(F32)) + r1_ref[0].astype(F32)) + r2_ref[0].astype(F32)
        if cols:
            o_ref[...] = s
        else:
            o_ref[0] = s

    slot = lambda j: pl.BlockSpec((1, T, C), lambda i, k_ref, c_ref: (j, i, 0))
    out_spec = (pl.BlockSpec((T, C), lambda i, k_ref, c_ref: (i, c_ref[0])) if cols
                else pl.BlockSpec((1, T, C), lambda i, k_ref, c_ref: (c_ref[0], i, 0)))
    return pl.pallas_call(
        body,
        grid_spec=pltpu.PrefetchScalarGridSpec(
            num_scalar_prefetch=2, grid=(R // T,),
            in_specs=[pl.BlockSpec((1, T, C), lambda i, k_ref, c_ref: (k_ref[0], i, 0)), slot(0), slot(1), slot(2)],
            out_specs=out_spec),
        out_shape=_sds((R, 2 * C) if cols else (2, R, C), F32), name="chip_add",
        compiler_params=_params("parallel"))(chip, core, p, r, r, r)


def allreduce_small(v):
    R = v.shape[0]

    def body(v_ref, o_ref, buf, send_sems, recv_sems, local_sem):
        x, y, c, chips = _place()
        me, sibling = (x, y, c), (x, y, 1 - c)

        def rows(px, py, pc):
            return buf.at[pl.ds((4 * px + 2 * py + pc) * R, R), :]

        def copy(k, block, to, src=None):
            return _remote(rows(*block) if src is None else src, rows(*block), send_sems.at[k], recv_sems.at[k], to)

        mine = pltpu.make_async_copy(v_ref, rows(*me), local_sem)
        mine.start()
        first = [copy(0, me, sibling, src=v_ref)]
        first += [copy(1 + j, me, (*chip, c), src=v_ref) for j, chip in enumerate(chips)]
        for cp in first:
            cp.start()
        passed = [copy(4 + j, (*chip, c), sibling) for j, chip in enumerate(chips)]
        for j, chip in enumerate(chips):
            copy(1 + j, (*chip, c), me).wait_recv()
            passed[j].start()
        copy(0, sibling, me).wait_recv()
        for j, chip in enumerate(chips):
            copy(4 + j, (*chip, 1 - c), me).wait_recv()
        for cp in first + passed:
            cp.wait_send()
        mine.wait()
        acc = buf[0:R, :]
        for d in range(1, 8):
            acc = acc + buf[d * R:(d + 1) * R, :]
        o_ref[...] = acc

    return pl.pallas_call(
        body, in_specs=[pl.BlockSpec(memory_space=pltpu.VMEM)], out_specs=pl.BlockSpec(memory_space=pltpu.VMEM),
        out_shape=_sds((R, LANES), F32),
        scratch_shapes=[pltpu.VMEM((8 * R, LANES), F32), pltpu.SemaphoreType.DMA((7,)),
                        pltpu.SemaphoreType.DMA((7,)), pltpu.SemaphoreType.DMA],
        name="allreduce_small", compiler_params=pltpu.CompilerParams(vmem_limit_bytes=VMEM_LIMIT))(v)


def matmul(a, b, mode, out_dtype, name, tm=512, tn=1024, tk=4608, jobs=(), n_out=None):
    if mode == "nn":
        (M, K), (K2, N) = a.shape, b.shape
    elif mode == "nt":
        (M, K), (N, K2) = a.shape, b.shape
        N = n_out or N
    else:
        (K, M), (K2, N) = a.shape, b.shape
    assert K == K2 or (mode == "nn" and K2 > K)
    tm, tn, tk = _pick(M, tm), _pick(N, tn), _pick(K, tk)
    nk = K // tk
    dn = {"nn": NN, "nt": NT, "tn": TN}[mode]

    def body(a_ref, b_ref, o_ref, *acc):
        part = _dot(a_ref[...].astype(BF16), b_ref[...].astype(BF16), dn)
        if nk == 1:
            o_ref[...] = part.astype(o_ref.dtype)
            return
        acc_ref, = acc
        k = pl.program_id(2)

        @pl.when(k == 0)
        def _():
            acc_ref[...] = part

        @pl.when(jnp.logical_and(k > 0, k < nk - 1))
        def _():
            acc_ref[...] += part

        @pl.when(k == nk - 1)
        def _():
            o_ref[...] = (acc_ref[...] + part).astype(o_ref.dtype)

    a_spec = (pl.BlockSpec((tk, tm), lambda i, j, k: (k, i)) if mode == "tn"
              else pl.BlockSpec((tm, tk), lambda i, j, k: (i, k)))
    b_spec = (pl.BlockSpec((tn, tk), lambda i, j, k: (j, k)) if mode == "nt"
              else pl.BlockSpec((tk, tn), lambda i, j, k: (k, j)))
    outs, per_job = _call(
        body, grid=(M // tm, N // tn, nk), in_specs=[a_spec, b_spec],
        out_specs=[pl.BlockSpec((tm, tn), lambda i, j, k: (i, j))], out_shape=[_sds((M, N), out_dtype)],
        scratch_shapes=[pltpu.VMEM((tm, tn), F32)] if nk > 1 else [], name=name,
        sem=("parallel", "parallel", "arbitrary"), args=(a, b), jobs=jobs)
    return outs[0], per_job


def rms_fwd(x, w):
    L, D = x.shape
    T = _pick(L, 256)

    def body(x_ref, w_ref, h_ref):
        xv = x_ref[...]
        r = lax.rsqrt(jnp.mean(xv * xv, axis=-1, keepdims=True) + NORM_EPS)
        h_ref[...] = (xv * r * w_ref[...]).astype(h_ref.dtype)

    return pl.pallas_call(
        body, grid=(L // T,),
        in_specs=[pl.BlockSpec((T, D), lambda i: (i, 0)), pl.BlockSpec((1, D), lambda i: (0, 0))],
        out_specs=pl.BlockSpec((T, D), lambda i: (i, 0)),
        out_shape=_sds((L, D), BF16), name="rms_fwd", compiler_params=_params("parallel"))(x, w)


def post_fwd(x, o, w, next_w=None, jobs=()):
    L, D = x.shape
    T = _pick(L, 256)

    def body(x_ref, o_ref, w_ref, *rest):
        ov = o_ref[...]
        r = lax.rsqrt(jnp.mean(ov * ov, axis=-1, keepdims=True) + NORM_EPS)
        y = x_ref[...] + ov * r * w_ref[...]
        rest[-1 if next_w is None else -2][...] = y
        if next_w is not None:
            r2 = lax.rsqrt(jnp.mean(y * y, axis=-1, keepdims=True) + NORM_EPS)
            rest[-1][...] = (y * r2 * rest[0][...]).astype(BF16)

    row = pl.BlockSpec((T, D), lambda i: (i, 0))
    vec = pl.BlockSpec((1, D), lambda i: (0, 0))
    more = [] if next_w is None else [next_w]
    outs, per_job = _call(
        body, grid=(L // T,), in_specs=[row, row, vec] + [vec] * len(more), out_specs=[row] * (1 + len(more)),
        out_shape=[_sds((L, D), F32)] + [_sds((L, D), BF16)] * len(more), name="post_fwd", sem=("parallel",),
        args=(x, o, w, *more), jobs=jobs)
    return (outs[0], outs[1] if more else None), per_job


def _rms_bwd_math(xv, w, dy):
    r = lax.rsqrt(jnp.mean(xv * xv, axis=-1, keepdims=True) + NORM_EPS)
    xhat = xv * r
    g = dy * w
    dx = r * (g - xhat * jnp.mean(g * xhat, axis=-1, keepdims=True))
    return dx, jnp.sum(dy * xhat, axis=0, keepdims=True)


def post_bwd(o, w, dxn):
    L, D = o.shape
    T = _pick(L, 256)

    def body(o_ref, w_ref, d_ref, do_ref, dw_ref):
        dx, dw = _rms_bwd_math(o_ref[...], w_ref[...], d_ref[...])
        do_ref[...] = dx.astype(do_ref.dtype)

        @pl.when(pl.program_id(0) == 0)
        def _():
            dw_ref[...] = jnp.zeros_like(dw_ref)

        dw_ref[...] += dw

    row = pl.BlockSpec((T, D), lambda i: (i, 0))
    vec = pl.BlockSpec((1, D), lambda i: (0, 0))
    return pl.pallas_call(
        body, grid=(L // T,), in_specs=[row, vec, row], out_specs=[row, vec],
        out_shape=[_sds((L, D), BF16), _sds((1, D), F32)],
        name="post_bwd", compiler_params=_params("arbitrary"))(o, w, dxn)


def rms_bwd(x, w, dh_a, dh_b, dxn, below=None, jobs=()):
    L, D = x.shape
    T = _pick(L, 256)

    def body(x_ref, w_ref, a_ref, b_ref, d_ref, *rest):
        dx, dw = _rms_bwd_math(x_ref[...], w_ref[...], a_ref[...] + b_ref[...])
        dx = d_ref[...] + dx
        outs = rest[2:] if below else rest
        outs[0][...] = dx

        @pl.when(pl.program_id(0) == 0)
        def _():
            for acc in outs[1::2]:
                acc[...] = jnp.zeros_like(acc)

        outs[1][...] += dw
        if below:
            do, dwp = _rms_bwd_math(rest[0][...], rest[1][...], dx)
            outs[2][...] = do.astype(BF16)
            outs[3][...] += dwp

    row = pl.BlockSpec((T, D), lambda i: (i, 0))
    vec = pl.BlockSpec((1, D), lambda i: (0, 0))
    more = list(below) if below else []
    return _call(
        body, grid=(L // T,), in_specs=[row, vec, row, row, row] + ([row, vec] if below else []),
        out_specs=[row, vec] + ([row, vec] if below else []),
        out_shape=[_sds((L, D), F32), _sds((1, D), F32)] + ([_sds((L, D), BF16), _sds((1, D), F32)] if below else []),
        name="rms_bwd", sem=("arbitrary",), args=(x, w, dh_a, dh_b, dxn, *more), jobs=jobs)


def loss_head(y, target):
    L, D = y.shape
    T = _pick(L, 256)

    def body(y_ref, t_ref, d_ref, s_ref):
        e = y_ref[...] - t_ref[...]
        d_ref[...] = e * (1.0 / D)

        @pl.when(pl.program_id(0) == 0)
        def _():
            s_ref[...] = jnp.zeros_like(s_ref)

        s_ref[...] += jnp.sum(e * e)

    row = pl.BlockSpec((T, D), lambda i: (i, 0))
    return pl.pallas_call(
        body, grid=(L // T,), in_specs=[row, row],
        out_specs=[row, pl.BlockSpec((8, LANES), lambda i: (0, 0))],
        out_shape=[_sds((L, D), F32), _sds((8, LANES), F32)],
        name="loss_head", compiler_params=_params("arbitrary"))(y, target)


def _window_sums(xe, w, back):
    n = xe.shape[0]
    s, k = xe, 1
    while k < w:
        s = s + pltpu.roll(s, k if back else n - k, 0)
        k *= 2
    return s


def pool_fwd(proj, mixw, scale, D):
    L = proj.shape[0]
    PGW = D // GROUPS
    T = _pick(L, 256)
    hb = T // POOL_HALO

    def body(u_ref, halo_ref, g_ref, mw_ref, sc_ref, y_ref, p_ref):
        i = pl.program_id(0)
        u = u_ref[...]
        halo = jnp.where(i > 0, halo_ref[...], 0.0)
        xe = jnp.concatenate([halo, u], axis=0)
        t1 = _row(i * T + 1, (T, 1))
        for g, w in enumerate(POOL_WINDOWS):
            sl = slice(g * PGW, (g + 1) * PGW)
            win = _window_sums(xe[:, sl], w, True)[POOL_HALO:, :]
            cnt = jnp.minimum(t1, w).astype(F32)
            pooled = (win / cnt - u[:, sl]).astype(BF16)
            p_ref[:, sl] = pooled
            mixed = _dot(pooled, mw_ref[g])
            gate = g_ref[:, sl]
            y_ref[:, sl] = (mixed * sc_ref[:, sl] * (gate * _sigmoid(gate))).astype(BF16)

    return pl.pallas_call(
        body, grid=(L // T,),
        in_specs=[pl.BlockSpec((T, D), lambda i: (i, 0)),
                  pl.BlockSpec((POOL_HALO, D), lambda i: (jnp.maximum(i * hb - 1, 0), 0)),
                  pl.BlockSpec((T, D), lambda i: (i, 1)),
                  pl.BlockSpec((GROUPS, PGW, PGW), lambda i: (0, 0, 0)),
                  pl.BlockSpec((1, D), lambda i: (0, 0))],
        out_specs=[pl.BlockSpec((T, D), lambda i: (i, 0)), pl.BlockSpec((T, D), lambda i: (i, 0))],
        out_shape=[_sds((L, 2 * D), BF16), _sds((L, D), BF16)],
        name="pool_fwd", compiler_params=_params("parallel"))(proj, proj, proj, mixw, scale)


def pool_bwd(proj, dmix, pooled, mixw, scale, dproj, D):
    L = proj.shape[0]
    PGW = D // GROUPS
    T = _pick(L, 256)
    hb = T // POOL_HALO
    nT = L // T

    def body(g_ref, gh_ref, dy_ref, dyh_ref, p_ref, mw_ref, sc_ref, old_ref, dp_ref, dm_ref, ds_ref):
        i = pl.program_id(0)
        t1 = _row(i * T + 1, (T, 1))
        th1 = _row((i + 1) * T + 1, (POOL_HALO, 1))
        live = i < nT - 1

        @pl.when(i == 0)
        def _():
            ds_ref[...] = jnp.zeros_like(ds_ref)
            dm_ref[...] = jnp.zeros_like(dm_ref)

        for g, w in enumerate(POOL_WINDOWS):
            sl = slice(g * PGW, (g + 1) * PGW)
            sc = sc_ref[:, sl]
            gate, dy = g_ref[:, sl], dy_ref[:, sl]
            sg, dsg = _silu_and_grad(gate)
            pooled = p_ref[:, sl]
            mixed = _dot(pooled, mw_ref[g])
            dmixed = (dy * sc * sg).astype(BF16)
            dm_ref[g] += _dot(pooled, dmixed, TN)
            dp_ref[:, D + g * PGW:D + (g + 1) * PGW] = (dy * mixed * sc * dsg).astype(BF16)
            ds_ref[:, sl] += jnp.sum(dy * mixed * sg, axis=0, keepdims=True)
            dpool = _dot(dmixed, mw_ref[g], NT)
            gate_h = gh_ref[:, sl]
            dmixed_h = (dyh_ref[:, sl] * sc * (gate_h * _sigmoid(gate_h))).astype(BF16)
            dpool_h = jnp.where(live, _dot(dmixed_h, mw_ref[g], NT), 0.0)
            q = dpool / jnp.minimum(t1, w).astype(F32)
            q_h = dpool_h / jnp.minimum(th1, w).astype(F32)
            qe = jnp.concatenate([q, q_h], axis=0)
            dp_ref[:, sl] = (_window_sums(qe, w, False)[:T, :] - dpool).astype(BF16)

    nxt = lambda i: jnp.minimum((i + 1) * hb, L // POOL_HALO - 1)
    row = lambda c: pl.BlockSpec((T, D), lambda i: (i, c))
    return pl.pallas_call(
        body, grid=(nT,),
        in_specs=[row(1), pl.BlockSpec((POOL_HALO, D), lambda i: (nxt(i), 1)),
                  row(0), pl.BlockSpec((POOL_HALO, D), lambda i: (nxt(i), 0)),
                  row(0), pl.BlockSpec((GROUPS, PGW, PGW), lambda i: (0, 0, 0)),
                  pl.BlockSpec((1, D), lambda i: (0, 0)), _ANY],
        out_specs=[pl.BlockSpec((T, 2 * D), lambda i: (i, 0)), pl.BlockSpec((GROUPS, PGW, PGW), lambda i: (0, 0, 0)),
                   pl.BlockSpec((1, D), lambda i: (0, 0))],
        out_shape=[_sds(dproj.shape, dproj.dtype), _sds((GROUPS, PGW, PGW), F32), _sds((1, D), F32)],
        input_output_aliases={7: 0},
        name="pool_bwd", compiler_params=_params("arbitrary"))(proj, proj, dmix, dmix, pooled, mixw, scale, dproj)


def conv_fwd(proj, cw, cb, D):
    L = proj.shape[0]
    C = cw.shape[1]
    assert (3 * D) % C == 0
    cblk = (3 * D) // C
    T = _pick(L, 256)
    hb = T // CONV_HALO

    def body(u_ref, halo_ref, w_ref, b_ref, o_ref):
        i = pl.program_id(0)
        u = u_ref[...]
        xe = jnp.concatenate([jnp.where(i > 0, halo_ref[...], 0.0), u], axis=0)
        acc = b_ref[...] + w_ref[CONV_K - 1:CONV_K, :] * u
        for k in range(CONV_K - 1):
            acc = acc + w_ref[k:k + 1, :] * pltpu.roll(xe, CONV_K - 1 - k, 0)[CONV_HALO:, :]
        o_ref[...] = acc

    return pl.pallas_call(
        body, grid=(L // T,),
        in_specs=[pl.BlockSpec((T, C), lambda i: (i, cblk)),
                  pl.BlockSpec((CONV_HALO, C), lambda i: (jnp.maximum(i * hb - 1, 0), cblk)),
                  pl.BlockSpec((CONV_K, C), lambda i: (0, 0)),
                  pl.BlockSpec((1, C), lambda i: (0, 0))],
        out_specs=pl.BlockSpec((T, C), lambda i: (i, 0)),
        out_shape=_sds((L, C), F32), name="conv_fwd", compiler_params=_params("parallel"))(proj, proj, cw, cb)


def conv_bwd(dparts, proj, cw, dproj, D, jobs=()):
    L = proj.shape[0]
    C = cw.shape[1]
    cblk = (3 * D) // C
    T = _pick(L, 256)
    hb = T // CONV_HALO
    nT = L // T
    widths = [p.shape[1] for p in dparts]
    assert sum(widths) == C
    n = len(dparts)

    def body(*refs):
        d_refs, dn_refs = refs[:n], refs[n:2 * n]
        u_ref, up_ref, w_ref, old_ref, dr_ref, dw_ref, db_ref = refs[2 * n:]
        i = pl.program_id(0)

        @pl.when(i == 0)
        def _():
            dw_ref[...] = jnp.zeros_like(dw_ref)
            db_ref[...] = jnp.zeros_like(db_ref)

        at = 0
        for d_ref, dn_ref, wd in zip(d_refs, dn_refs, widths):
            sl = slice(at, at + wd)
            at += wd
            d = d_ref[...]
            u = u_ref[:, sl]
            de = jnp.concatenate([d, jnp.where(i < nT - 1, dn_ref[...], 0.0)], axis=0)
            ue = jnp.concatenate([jnp.where(i > 0, up_ref[:, sl], 0.0), u], axis=0)
            acc = w_ref[CONV_K - 1:CONV_K, sl] * d
            dw_ref[CONV_K - 1:CONV_K, sl] += jnp.sum(d * u, axis=0, keepdims=True)
            for k in range(CONV_K - 1):
                sh = CONV_K - 1 - k
                acc = acc + w_ref[k:k + 1, sl] * pltpu.roll(de, T + CONV_HALO - sh, 0)[:T, :]
                dw_ref[k:k + 1, sl] += jnp.sum(d * pltpu.roll(ue, sh, 0)[CONV_HALO:, :], axis=0, keepdims=True)
            dr_ref[:, sl] = acc.astype(dr_ref.dtype)
            db_ref[:, sl] += jnp.sum(d, axis=0, keepdims=True)

    nxt = lambda i: jnp.minimum((i + 1) * hb, L // CONV_HALO - 1)
    return _call(
        body, grid=(nT,),
        in_specs=[pl.BlockSpec((T, wd), lambda i: (i, 0)) for wd in widths]
        + [pl.BlockSpec((CONV_HALO, wd), lambda i: (nxt(i), 0)) for wd in widths]
        + [pl.BlockSpec((T, C), lambda i: (i, cblk)),
           pl.BlockSpec((CONV_HALO, C), lambda i: (jnp.maximum(i * hb - 1, 0), cblk)),
           pl.BlockSpec((CONV_K, C), lambda i: (0, 0)), _ANY],
        out_specs=[pl.BlockSpec((T, C), lambda i: (i, cblk)),
                   pl.BlockSpec((8, C), lambda i: (0, 0)),
                   pl.BlockSpec((1, C), lambda i: (0, 0))],
        out_shape=[_sds(dproj.shape, dproj.dtype), _sds((8, C), F32), _sds((1, C), F32)],
        aliases={2 * n + 3: 0}, name="conv_bwd", sem=("arbitrary",),
        args=(*dparts, *dparts, proj, proj, cw, dproj), jobs=jobs)


def _softplus(v):
    y = jnp.exp(-jnp.abs(v))
    u = 1.0 + y
    log1p = jnp.where(u == 1.0, y, jnp.log(u) * y / jnp.where(u == 1.0, 1.0, u - 1.0))
    return jnp.maximum(v, 0.0) + log1p


def dt_prep(dtraw, bias, alog, expand, D):
    L = dtraw.shape[0]
    GC = D // GROUPS
    HPG = GC // HEAD_DIM
    Q = _pick(L, SCAN_CHUNK)
    nc = L // Q

    def body(r_ref, b_ref, a_ref, e_ref, dt_ref, acs_ref, acst_ref, dtx_ref, eax_ref, dsx_ref, cdx_ref):
        valid = lax.broadcasted_iota(jnp.int32, (1, LANES), 1) < HPG
        dt = jnp.where(valid, _softplus(r_ref[...] + b_ref[...]), 0.0)
        adt = dt * -jnp.exp(a_ref[...])
        tril = (_row(0, (Q, Q)) >= lax.broadcasted_iota(jnp.int32, (Q, Q), 1)).astype(BF16)
        acs = _sel_dot(tril, adt)
        last = acs[Q - 1:Q, :]
        dt_ref[...] = dt
        acs_ref[...] = acs
        acst_ref[...] = acs.T
        e = e_ref[...]
        dtx_ref[...] = _dot_sel(dt, e)
        eax_ref[...] = jnp.exp(_dot_sel(acs, e))
        dsx_ref[...] = jnp.exp(_dot_sel(last - acs, e))
        cdx_ref[0] = jnp.exp(_dot_sel(jnp.broadcast_to(last, (8, LANES)), e))

    head = pl.BlockSpec((Q, LANES), lambda g, c: (c, g))
    hvec = pl.BlockSpec((1, LANES), lambda g, c: (0, g))
    chan = pl.BlockSpec((Q, GC), lambda g, c: (c, g))
    return pl.pallas_call(
        body, grid=(GROUPS, nc),
        in_specs=[head, hvec, hvec, pl.BlockSpec((LANES, GC), lambda g, c: (0, 0))],
        out_specs=[head, head, pl.BlockSpec((LANES, Q), lambda g, c: (g, c)), chan, chan, chan,
                   pl.BlockSpec((1, 8, GC), lambda g, c: (c, 0, g))],
        out_shape=[_sds((L, GROUPS * LANES), F32), _sds((L, GROUPS * LANES), F32), _sds((GROUPS * LANES, L), F32),
                   _sds((L, D), F32), _sds((L, D), F32), _sds((L, D), F32), _sds((nc, 8, D), F32)],
        name="dt_prep", compiler_params=_params("parallel", "parallel"))(dtraw, bias, alog, expand)


def _scan_specs(L, D, Q, rev):
    GC = D // GROUPS
    nc = L // Q
    ci = (lambda c: nc - 1 - c) if rev else (lambda c: c)
    return dict(
        xs=pl.BlockSpec((Q, GC), lambda g, c: (ci(c), g)),
        b=pl.BlockSpec((Q, STATE), lambda g, c: (ci(c), D // STATE + g)),
        c=pl.BlockSpec((Q, STATE), lambda g, c: (ci(c), D // STATE + GROUPS + g)),
        chan=pl.BlockSpec((Q, GC), lambda g, c: (ci(c), g)),
        cdx=pl.BlockSpec((1, 8, GC), lambda g, c: (ci(c), 0, g)),
        head=pl.BlockSpec((Q, LANES), lambda g, c: (ci(c), g)),
        headt=pl.BlockSpec((LANES, Q), lambda g, c: (g, ci(c))),
        state=pl.BlockSpec((1, 1, STATE, GC), lambda g, c: (ci(c), g, 0, 0)),
        hvec=pl.BlockSpec((1, LANES), lambda g, c: (0, g)),
        cvec=pl.BlockSpec((1, GC), lambda g, c: (0, g)))


def scan_fwd(pre, dtx, eax, dsx, cdx, acs, acst, proj, dexp, nw, mixed, D):
    L = pre.shape[0]
    GC = D // GROUPS
    Q = _pick(L, SCAN_CHUNK)
    nc = L // Q
    sp = _scan_specs(L, D, Q, False)

    def body(xs_ref, b_ref, c_ref, dtx_ref, eax_ref, dsx_ref, cdx_ref, acs_ref, acst_ref, z_ref, de_ref, nw_ref,
             old_ref, y_ref, st_ref, o_ref, s_scr):
        @pl.when(pl.program_id(1) == 0)
        def _():
            s_scr[...] = jnp.zeros_like(s_scr)

        tri = _row(0, (Q, Q)) >= lax.broadcasted_iota(jnp.int32, (Q, Q), 1)
        half = lax.broadcasted_iota(jnp.int32, (1, LANES), 1) // HEAD_DIM
        xs, _ = _silu_and_grad(xs_ref[...])
        bg = _silu_and_grad(b_ref[...])[0].astype(BF16)
        cg = _silu_and_grad(c_ref[...])[0].astype(BF16)
        xdt = xs * dtx_ref[...]
        sprev = s_scr[...]
        st_ref[0, 0] = sprev
        sc = _dot(cg, bg, NT)
        yoff = _dot(cg, sprev.astype(BF16)) * eax_ref[...]
        for j in range(GC // LANES):
            ps = slice(j * LANES, (j + 1) * LANES)
            xp = xdt[:, ps]
            acc = yoff[:, ps]
            for hh in range(2):
                h = 2 * j + hh
                lm = jnp.exp(jnp.where(tri, acs_ref[:, h:h + 1] - acst_ref[h:h + 1, :], -1e30))
                xm = jnp.where(half == hh, xp, 0.0).astype(BF16)
                acc = acc + _dot((sc * lm).astype(BF16), xm)
            y_ref[:, ps] = acc
        xw = (xdt * dsx_ref[...]).astype(BF16)
        s_scr[...] = cdx_ref[0, 0:1, :] * sprev + _dot(bg, xw, TN)
        z = z_ref[...]
        y3 = (y_ref[...] + de_ref[...] * xs) * (z * _sigmoid(z))
        r = lax.rsqrt(jnp.mean(y3 * y3, axis=-1, keepdims=True) + NORM_EPS)
        o_ref[...] = (y3 * r * nw_ref[...]).astype(o_ref.dtype)

    return pl.pallas_call(
        body, grid=(GROUPS, nc),
        in_specs=[sp["xs"], sp["b"], sp["c"], sp["chan"], sp["chan"], sp["chan"], sp["cdx"], sp["head"], sp["headt"],
                  pl.BlockSpec((Q, GC), lambda g, c: (c, 2 * GROUPS + g)), sp["cvec"], sp["cvec"], _ANY],
        out_specs=[sp["chan"], sp["state"], pl.BlockSpec((Q, GC), lambda g, c: (c, GROUPS + g))],
        out_shape=[_sds((L, D), F32), _sds((nc, GROUPS, STATE, GC), F32), _sds(mixed.shape, mixed.dtype)],
        input_output_aliases={12: 2},
        scratch_shapes=[pltpu.VMEM((STATE, GC), F32)], name="scan_fwd",
        compiler_params=_params("parallel", "arbitrary"))(
            pre, pre, pre, dtx, eax, dsx, cdx, acs, acst, proj, dexp, nw, mixed)


def scan_bwd(pre, dtx, eax, dsx, cdx, acs, acst, dt, dtraw, bias, alog, states, y, proj, dmix, nw, dexp, collapse, D,
             jobs=()):
    L = pre.shape[0]
    GC = D // GROUPS
    Q = _pick(L, SCAN_CHUNK)
    nc = L // Q
    sp = _scan_specs(L, D, Q, True)
    rc = lambda c: nc - 1 - c

    def body(xs_ref, b_ref, c_ref, dtx_ref, eax_ref, dsx_ref, cdx_ref, acs_ref, acst_ref, dt_ref, raw_ref,
             bias_ref, alog_ref, st_ref, y_ref, z_ref, dm_ref, nw_ref, dexp_ref, col_ref,
             dxs_ref, db_ref, dc_ref, ddt_ref, dal_ref, dbi_ref, dz_ref, dnw_ref, dde_ref, ds_scr, dx_scr):
        first = pl.program_id(1) == 0

        @pl.when(first)
        def _():
            ds_scr[...] = jnp.zeros_like(ds_scr)
            dal_ref[...] = jnp.zeros_like(dal_ref)
            dbi_ref[...] = jnp.zeros_like(dbi_ref)
            dnw_ref[...] = jnp.zeros_like(dnw_ref)
            dde_ref[...] = jnp.zeros_like(dde_ref)

        li = _row(0, (Q, Q))
        si = lax.broadcasted_iota(jnp.int32, (Q, Q), 1)
        lane = lax.broadcasted_iota(jnp.int32, (1, LANES), 1)
        half = lane // HEAD_DIM
        xs_pre, b_pre, c_pre = xs_ref[...], b_ref[...], c_ref[...]
        xs, xs_g = _silu_and_grad(xs_pre)
        bf, b_g = _silu_and_grad(b_pre)
        cf, c_g = _silu_and_grad(c_pre)
        bg, cg = bf.astype(BF16), cf.astype(BF16)
        dtx, eax, dsx = dtx_ref[...], eax_ref[...], dsx_ref[...]
        cd = cdx_ref[0, 0:1, :]
        xdt = xs * dtx
        sz, dsz = _silu_and_grad(z_ref[...])
        y2 = y_ref[...] + dexp_ref[...] * xs
        y3 = y2 * sz
        r = lax.rsqrt(jnp.mean(y3 * y3, axis=-1, keepdims=True) + NORM_EPS)
        n = y3 * r
        dm = dm_ref[...]
        gg = dm * nw_ref[...]
        dy3 = r * (gg - n * jnp.mean(gg * n, axis=-1, keepdims=True))
        dnw_ref[0:1, :] += jnp.sum(dm * n, axis=0, keepdims=True)
        G = dy3 * sz
        dz_ref[...] = (dy3 * y2 * dsz).astype(dz_ref.dtype)
        dde_ref[0:1, :] += jnp.sum(G * xs, axis=0, keepdims=True)
        prev = st_ref[0, 0]
        dsn = ds_scr[...]
        prev_b, dsn_b = prev.astype(BF16), dsn.astype(BF16)
        cp = _dot(cg, prev_b)
        ge_b = (G * eax).astype(BF16)
        d_c = _dot(ge_b, prev_b, NT)
        dprev = _dot(cg, ge_b, TN) + cd * dsn
        chan_a = G * cp * eax
        xw_b = (xdt * dsx).astype(BF16)
        dcd = jnp.sum(prev * dsn, axis=0, keepdims=True)
        d_b = _dot(xw_b, dsn_b, NT)
        dxw = _dot(bg, dsn_b)
        dd = dxw * xdt * dsx
        chan_a = chan_a - dd
        last_c = jnp.sum(dd, axis=0, keepdims=True) + dcd * cd
        sc = _dot(cg, bg, NT)
        head_row = _row(0, (LANES, 1))
        dsc = jnp.zeros((Q, Q), F32)
        dacs = jnp.zeros((Q, LANES), F32)
        colsums = jnp.zeros((LANES, Q), F32)
        for j in range(GC // LANES):
            ps = slice(j * LANES, (j + 1) * LANES)
            xp, gp = xdt[:, ps], G[:, ps]
            dxp = dxw[:, ps] * dsx[:, ps]
            for hh in range(2):
                h = 2 * j + hh
                lm = jnp.exp(jnp.where(li >= si, acs_ref[:, h:h + 1] - acst_ref[h:h + 1, :], -1e30))
                m = sc * lm
                xm = jnp.where(half == hh, xp, 0.0).astype(BF16)
                gm = jnp.where(half == hh, gp, 0.0).astype(BF16)
                dm = _dot(gm, xm, NT)
                dxp = dxp + _dot(m.astype(BF16), gm, TN)
                dsc = dsc + dm * lm
                w = dm * m
                dacs = dacs + jnp.where(lane == h, jnp.sum(w, axis=1, keepdims=True), 0.0)
                colsums = jnp.where(head_row == h, jnp.sum(w, axis=0, keepdims=True), colsums)
            dx_scr[:, ps] = dxp
        dacs = dacs - colsums.T
        dsc_b = dsc.astype(BF16)
        d_c = d_c + _dot(dsc_b, bg)
        d_b = d_b + _dot(dsc_b, cg, TN)
        ds_scr[...] = dprev
        dxdt = dx_scr[...]
        dxs_ref[...] = (dxdt * dtx + dexp_ref[...] * G) * xs_g
        db_ref[...] = d_b * b_g
        dc_ref[...] = d_c * c_g
        colm = col_ref[...]
        dacs = dacs + _dot_sel(chan_a, colm)
        dlast = _dot_sel(jnp.broadcast_to(last_c, (8, GC)), colm)[0:1, :]
        dacs = dacs + jnp.where(_row(0, (Q, 1)) == Q - 1, dlast, 0.0)
        dadt = _sel_dot((si >= li).astype(BF16), dacs)
        a = -jnp.exp(alog_ref[...])
        dt = dt_ref[...]
        ddt = dadt * a + _dot_sel(dxdt * xs, colm)
        dal_ref[0:1, :] += jnp.sum(dadt * dt * a, axis=0, keepdims=True)
        draw = ddt * _sigmoid(raw_ref[...] + bias_ref[...])
        dbi_ref[0:1, :] += jnp.sum(draw, axis=0, keepdims=True)
        ddt_ref[...] = draw.astype(ddt_ref.dtype)

    acc = pl.BlockSpec((8, LANES), lambda g, c: (0, g))
    cacc = pl.BlockSpec((8, GC), lambda g, c: (0, g))
    return _call(
        body, grid=(GROUPS, nc),
        in_specs=[sp["xs"], sp["b"], sp["c"], sp["chan"], sp["chan"], sp["chan"], sp["cdx"], sp["head"], sp["headt"],
                  sp["head"], sp["head"], sp["hvec"], sp["hvec"], sp["state"], sp["chan"],
                  pl.BlockSpec((Q, GC), lambda g, c: (rc(c), 2 * GROUPS + g)),
                  pl.BlockSpec((Q, GC), lambda g, c: (rc(c), GROUPS + g)), sp["cvec"], sp["cvec"],
                  pl.BlockSpec((GC, LANES), lambda g, c: (0, 0))],
        out_specs=[sp["chan"],
                   pl.BlockSpec((Q, STATE), lambda g, c: (rc(c), g)),
                   pl.BlockSpec((Q, STATE), lambda g, c: (rc(c), g)),
                   sp["head"], acc, acc,
                   pl.BlockSpec((Q, GC), lambda g, c: (rc(c), 2 * GROUPS + g)), cacc, cacc],
        out_shape=[_sds((L, D), F32), _sds((L, GROUPS * STATE), F32), _sds((L, GROUPS * STATE), F32),
                   _sds((L, GROUPS * LANES), BF16), _sds((8, GROUPS * LANES), F32), _sds((8, GROUPS * LANES), F32),
                   _sds((L, proj.shape[1]), BF16), _sds((8, D), F32), _sds((8, D), F32)],
        scratch_shapes=[pltpu.VMEM((STATE, GC), F32), pltpu.VMEM((Q, GC), F32)], name="scan_bwd",
        sem=("parallel", "arbitrary"),
        args=(pre, pre, pre, dtx, eax, dsx, cdx, acs, acst, dt, dtraw, bias, alog, states, y, proj, dmix, nw, dexp,
              collapse),
        jobs=jobs)


def _adam_math(gv, w, m, v):
    c1 = 1.0 - ADAM_B1 ** ADAM_STEP
    c2 = 1.0 - ADAM_B2 ** ADAM_STEP
    nm = ADAM_B1 * m + (1.0 - ADAM_B1) * gv
    nv = ADAM_B2 * v + (1.0 - ADAM_B2) * (gv * gv)
    return -ADAM_LR * ((nm / c1) / (jnp.sqrt(nv / c2) + ADAM_EPS) + ADAM_WD * w), nm, nv


def adamw(g, w, m, v, name):
    R, C = g.shape
    T = R if R <= 128 else 128
    assert R % T == 0

    def body(g_ref, w_ref, m_ref, v_ref, d_ref, nm_ref, nv_ref):
        d_ref[...], nm_ref[...], nv_ref[...] = _adam_math(g_ref[...], w_ref[...], m_ref[...], v_ref[...])

    blk = pl.BlockSpec((T, C), lambda i: (i, 0))
    return pl.pallas_call(
        body, grid=(R // T,), in_specs=[blk] * 4, out_specs=[blk] * 3,
        out_shape=[_sds((R, C), F32)] * 3, name=name, compiler_params=_params("parallel"))(g, w, m, v)


def adamw_layer(g, w, m, v, layer, prev, name):
    R, C = g.shape
    T = _rows_tile(R, 256)
    assert R % T == 0

    def body(g_ref, w_ref, m_ref, v_ref, *rest):
        go_ref, d_ref, nm_ref, nv_ref = rest[-4:]
        gv = g_ref[...]
        go_ref[0] = gv
        d_ref[0], nm_ref[0], nv_ref[0] = _adam_math(gv, w_ref[0], m_ref[0], v_ref[0])

    mine = pl.BlockSpec((1, T, C), lambda i: (layer, i, 0))
    prev = list(prev or [])
    return pl.pallas_call(
        body, grid=(R // T,), in_specs=[pl.BlockSpec((T, C), lambda i: (i, 0)), mine, mine, mine] + [_ANY] * len(prev),
        out_specs=[mine] * 4, out_shape=[_sds(w.shape, F32)] * 4,
        input_output_aliases={4 + i: i for i in range(len(prev))}, name=name,
        compiler_params=_params("parallel"))(g, w, m, v, *prev)


def cast_bf16(a):
    R, C = a.shape
    T = _rows_tile(R, 256)

    def body(a_ref, o_ref):
        o_ref[...] = a_ref[...].astype(BF16)

    blk = pl.BlockSpec((T, C), lambda i: (i, 0))
    return pl.pallas_call(body, grid=(R // T,), in_specs=[blk], out_specs=blk, out_shape=_sds((R, C), BF16),
                          name="cast_bf16", compiler_params=_params("parallel"))(a)


def _to_groups(v, hpg):
    lead = v.shape[:-1]
    t = v.reshape(lead + (GROUPS, hpg))
    t = jnp.pad(t, [(0, 0)] * (len(lead) + 1) + [(0, LANES - hpg)])
    return t.reshape(lead + (GROUPS * LANES,))


def _from_groups(a, hpg):
    lead = a.shape[:-1]
    return a.reshape(lead + (GROUPS, LANES))[..., :hpg].reshape(lead + (GROUPS * hpg,))


def _expand_matrix(D):
    gc = D // GROUPS
    return (jnp.arange(LANES)[:, None] == (jnp.arange(gc)[None, :] // HEAD_DIM)).astype(BF16)


def layer_params(pre_w, w_in_t, mixw, scale, cw, cb, bias, alog, dskip, nw, w_out_full, post_w, D):
    hpg = D // GROUPS // HEAD_DIM
    main = w_in_t.shape[0] - GROUPS * hpg
    wdt_t = _to_groups(w_in_t[main:].T, hpg).T
    return dict(
        pre_w=pre_w[None], win_t=w_in_t, main=main, wdt_t=wdt_t, mixw=mixw,
        scale=scale[None], cw=cw, cb=cb[None], bias=_to_groups(bias, hpg)[None], alog=_to_groups(alog, hpg)[None],
        dexp=jnp.repeat(dskip, HEAD_DIM)[None], nw=nw[None], wout=w_out_full, post_w=post_w[None])


def layer_fwd(x, p, D, next_shards=None, h=None, next_pre_w=None):
    send_in = [gather_send_job(next_shards[:1], [True])] if next_shards else []
    if h is None:
        h = rms_fwd(x, p["pre_w"])
    proj, got = matmul(h, p["win_t"], "nt", F32, "proj", tm=1024, jobs=send_in, n_out=p["main"])
    dtraw, _ = matmul(h, p["wdt_t"], "nt", F32, "dtproj")
    mixed, pooled = pool_fwd(proj, p["mixw"], p["scale"], D)
    pre = conv_fwd(proj, p["cw"], p["cb"], D)
    dtp = dt_prep(dtraw, p["bias"], p["alog"], _expand_matrix(D), D)
    dt, acs, acst, dtx, eax, dsx, cdx = dtp
    y, states, mixed = scan_fwd(pre, dtx, eax, dsx, cdx, acs, acst, proj, p["dexp"], p["nw"], mixed, D)
    jobs = [gather_send_job(next_shards[1:]), gather_pass_job(got[0], [True])] if next_shards else []
    out, got = matmul(mixed, p["wout"], "nn", F32, "outproj", tm=1024, jobs=jobs)
    (xn, h_next), got2 = post_fwd(x, out, p["post_w"], next_pre_w,
                                  jobs=[gather_pass_job(got[0])] if next_shards else [])
    gathered = got[1] + got2[0] if next_shards else None
    return xn, dict(x=x, h=h, proj=proj, dtraw=dtraw, pooled=pooled, pre=pre, dtp=dtp, y=y, states=states,
                    mixed=mixed, out=out), gathered, h_next


def layer_bwd(dxn, p, s, D, where=None, post=None, below=None):
    reduce = where is not None
    chip, core = where if reduce else (None, None)
    hpg = D // GROUPS // HEAD_DIM
    PGW = D // GROUPS
    main = p["main"]
    SH = (main + GROUPS * hpg) // 4
    dt, acs, acst, dtx, eax, dsx, cdx = s["dtp"]
    dout, d_post = post if post else post_bwd(s["out"], p["post_w"], dxn)
    part = BF16 if reduce else F32
    d_wout, _ = matmul(s["mixed"], dout, "tn", part, "dwout", tm=1024)
    g_out = d_wout.reshape(4, 2, D // 4, D)
    dmix, got = matmul(dout, p["wout"], "nt", F32, "dmixed", tm=1024,
                       jobs=[pair_exchange_job([g_out])] if reduce else [])
    pair_out = pair_add(g_out, got[0][0], core, BF16) if reduce else None
    (dxs, db, dc, ddtraw, d_alog, d_bias, dproj, d_nw, d_dexp), got = scan_bwd(
        s["pre"], dtx, eax, dsx, cdx, acs, acst, dt, s["dtraw"], p["bias"], p["alog"], s["states"], s["y"],
        s["proj"], dmix, p["nw"], p["dexp"], _expand_matrix(D).T, D,
        jobs=[chip_exchange_job([pair_out])] if reduce else [])
    mine_out = chip_add(pair_out, got[0][0], chip, core) if reduce else None
    (dproj, d_cw, d_cb), got = conv_bwd([dxs, db, dc], s["proj"], p["cw"], dproj, D,
                                        jobs=[pair_gather_job([mine_out])] if reduce else [])
    r_out = got[0][0] if reduce else None
    dproj, d_mixw, d_scale = pool_bwd(s["proj"], dmix, s["pooled"], p["mixw"], p["scale"], dproj, D)
    d_wmain_t, _ = matmul(dproj, s["h"], "tn", part, "dwmain", tm=1024)
    d_wdt_t, _ = matmul(ddtraw, s["h"], "tn", part, "dwdt")
    late = [d_wmain_t[None], d_wdt_t[None],
            d_mixw.reshape(GROUPS, 4, PGW // 4, PGW).transpose(1, 0, 2, 3).reshape(4, 2, GROUPS * PGW // 8, PGW)]
    cols = [True, True, False]
    dh_b, got = matmul(ddtraw, p["wdt_t"], "nn", F32, "dh_dt",
                       jobs=[pair_exchange_job(late, cols)] if reduce else [])
    if reduce:
        p_main, p_dt, p_mix = [pair_add(g, r, core, BF16, k) for g, r, k in zip(late, got[0], cols)]
        p_in = jnp.concatenate([p_main[0], _from_groups(p_dt[0].T, hpg).T], axis=0)
        pairs = [p_in.reshape(4, SH, D // 2), p_mix]
    dh_a, got = matmul(dproj, p["win_t"], "nn", F32, "dh_main", jobs=[chip_exchange_job(pairs)] if reduce else [])
    mines = [chip_add(q, r, chip, core, k) for q, r, k in zip(pairs, got[0], [True, False])] if reduce else None
    (dx, d_pre, *post_below), got = rms_bwd(s["x"], p["pre_w"], dh_a, dh_b, dxn, below,
                                            jobs=[pair_gather_job(mines, [True, False])] if reduce else [])
    reduced = dict(w_in=got[0][0], w_out=r_out, pool_mix_w=got[0][1]) if reduce else None
    grads = dict(
        pre_norm_w=d_pre[0], pool_scale=d_scale[0], conv_w=d_cw[:CONV_K], conv_b=d_cb[0],
        dt_bias=_from_groups(d_bias[0], hpg), a_log=_from_groups(d_alog[0], hpg),
        d_skip=d_dexp[0].reshape(-1, HEAD_DIM).sum(axis=-1), ssd_norm_w=d_nw[0], post_norm_w=d_post[0])
    if not reduce:
        grads.update(w_in=jnp.concatenate([d_wmain_t.T, _from_groups(d_wdt_t.T, hpg)], axis=1), pool_mix_w=d_mixw,
                     w_out=d_wout)
    return dx, grads, reduced, tuple(post_below) or None


def local_step(x, target, params, D):
    saved, h, n = [], None, len(params)
    for l, p in enumerate(params):
        x, s, _, h = layer_fwd(x, p, D, h=h, next_pre_w=params[l + 1]["pre_w"] if l + 1 < n else None)
        saved.append(s)
    dx, sumsq = loss_head(x, target)
    grads, post = [None] * n, None
    for l in reversed(range(n)):
        below = (saved[l - 1]["out"], params[l - 1]["post_w"]) if l else None
        dx, grads[l], _, post = layer_bwd(dx, params[l], saved[l], D, post=post, below=below)
    return sumsq, dx, grads


SMALL = ("pre_norm_w", "pool_scale", "conv_w", "conv_b", "dt_bias", "a_log", "d_skip", "ssd_norm_w", "post_norm_w")
BIG = ("w_in", "w_out", "pool_mix_w")


def _pack(parts):
    flat = jnp.concatenate([p.reshape(-1) for p in parts])
    n = flat.shape[0]
    rows = -(-n // (LANES * LANES)) * LANES
    return jnp.pad(flat, (0, rows * LANES - n)).reshape(rows, LANES)


def _unpack(packed, shapes):
    flat, out, at = packed.reshape(-1), [], 0
    for s in shapes:
        n = math.prod(s)
        out.append(flat[at:at + n].reshape(s))
        at += n
    return out


def kernel(x, pre_norm_w, w_in, pool_mix_w, pool_scale, conv_w, conv_b, dt_bias, a_log, d_skip, ssd_norm_w, w_out, post_norm_w, loss_target, m_pre_norm_w, m_w_in, m_pool_mix_w, m_pool_scale, m_conv_w, m_conv_b, m_dt_bias, m_a_log, m_d_skip, m_ssd_norm_w, m_w_out, m_post_norm_w, v_pre_norm_w, v_w_in, v_pool_mix_w, v_pool_scale, v_conv_w, v_conv_b, v_dt_bias, v_a_log, v_d_skip, v_ssd_norm_w, v_w_out, v_post_norm_w):
    NL, D, SH = w_in.shape
    PGW = D // GROUPS
    CS = conv_w.shape[2]
    chip = (2 * lax.axis_index("x") + lax.axis_index("y")).astype(jnp.int32)
    chip1, core = chip.reshape(1), lax.axis_index("c").astype(jnp.int32).reshape(1)

    tr = lambda t: jnp.transpose(t, (0, 2, 1))
    w_in_t, m_w_in_t, v_w_in_t = tr(w_in), tr(m_w_in), tr(v_w_in)
    halved_by_cols = [True, False, False, False]

    def shards(l):
        return [cast_bf16(w_in_t[l]), cast_bf16(w_out[l]).reshape(2, D // 4, D),
                cast_bf16(pool_mix_w[l].reshape(GROUPS * PGW // 4, PGW)).reshape(2, GROUPS * PGW // 8, PGW),
                conv_w[l].reshape(2, CONV_K * CS // (2 * LANES), LANES)]

    def params(l, g):
        g_in, g_out, g_mix, g_cw = g
        mix_l = g_mix.reshape(4, GROUPS, PGW // 4, PGW).transpose(1, 0, 2, 3).reshape(GROUPS, PGW, PGW)
        cw_l = g_cw.reshape(4, CONV_K, CS).transpose(1, 0, 2).reshape(CONV_K, 4 * CS)
        return layer_params(pre_norm_w[l], g_in.reshape(4 * SH, D), mix_l, pool_scale[l], cw_l, conv_b[l], dt_bias[l],
                            a_log[l], d_skip[l], ssd_norm_w[l], g_out.reshape(2 * D, D), post_norm_w[l], D)

    gathered = run_jobs([gather_send_job(shards(0), halved_by_cols)], "gather_send")[0]
    gathered = run_jobs([gather_pass_job(gathered, halved_by_cols)], "gather_pass")[0]
    xl, h, ps, saved = x[0], None, [], []
    for l in range(NL):
        ps.append(params(l, gathered))
        last = l + 1 == NL
        xl, s, gathered, h = layer_fwd(xl, ps[l], D, None if last else shards(l + 1), h,
                                       None if last else pre_norm_w[l + 1][None])
        saved.append(s)
    dx, sumsq = loss_head(xl, loss_target[0])

    given = dict(w_in=(w_in_t, m_w_in_t, v_w_in_t), w_out=(w_out, m_w_out, v_w_out),
                 pool_mix_w=(pool_mix_w, m_pool_mix_w, v_pool_mix_w))
    flat = {n: [t.reshape(NL, -1, t.shape[-1]) for t in given[n]] for n in BIG}
    done = {n: None for n in BIG}
    grads, post = [None] * NL, None
    for l in reversed(range(NL)):
        below = (saved[l - 1]["out"], ps[l - 1]["post_w"]) if l else None
        dx, grads[l], reduced, post = layer_bwd(dx, ps[l], saved[l], D, (chip1, core), post, below)
        for n in BIG:
            r = reduced[n]
            done[n] = adamw_layer(r.reshape(-1, r.shape[-1]), *flat[n], l, done[n], "adamw_" + n)

    small_shapes = [(NL,) + grads[0][n].shape for n in SMALL]
    packed = _pack([0.5 / D * sumsq[0, :1]] + [jnp.stack([g[n] for g in grads]) for n in SMALL])
    total = allreduce_small(packed)
    loss, *small = _unpack(total, [(1,)] + small_shapes)
    small = dict(zip(SMALL, small))
    small["conv_w"] = lax.dynamic_slice_in_dim(small["conv_w"], chip * CS, CS, axis=2)

    given_small = dict(
        pre_norm_w=(pre_norm_w, m_pre_norm_w, v_pre_norm_w), pool_scale=(pool_scale, m_pool_scale, v_pool_scale),
        conv_w=(conv_w, m_conv_w, v_conv_w), conv_b=(conv_b, m_conv_b, v_conv_b),
        dt_bias=(dt_bias, m_dt_bias, v_dt_bias), a_log=(a_log, m_a_log, v_a_log),
        d_skip=(d_skip, m_d_skip, v_d_skip), ssd_norm_w=(ssd_norm_w, m_ssd_norm_w, v_ssd_norm_w),
        post_norm_w=(post_norm_w, m_post_norm_w, v_post_norm_w))
    shapes = [given_small[n][0].shape for n in SMALL]
    upd = adamw(_pack([small[n] for n in SMALL]), *[_pack([given_small[n][i] for n in SMALL]) for i in range(3)],
                "adamw_small")
    upd = [dict(zip(SMALL, _unpack(u, shapes))) for u in upd]

    out = {n: (small[n], upd[0][n], upd[1][n], upd[2][n]) for n in SMALL}
    for n in BIG:
        out[n] = tuple(t.reshape(given[n][0].shape) for t in done[n])
    out["w_in"] = tuple(tr(t) for t in out["w_in"])

    order = ("pre_norm_w", "w_in", "pool_mix_w", "pool_scale", "conv_w", "conv_b", "dt_bias", "a_log", "d_skip",
             "ssd_norm_w", "w_out", "post_norm_w")
    return (loss.reshape(()), dx[None], *[out[n][0] for n in order], *[out[n][1] for n in order],
            *[out[n][2] for n in order], *[out[n][3] for n in order])
```

```python
import functools
import math

import jax
import jax.numpy as jnp
from jax import lax
from jax.experimental import pallas as pl
from jax.experimental.pallas import tpu as pltpu

F32 = jnp.float32
BF16 = jnp.bfloat16

NORM_EPS = 1e-6
HEAD_DIM = 64
STATE = 128
GROUPS = 4
POOL_WINDOWS = (2, 4, 8, 16)
POOL_HALO = 16
CONV_K = 4
CONV_HALO = 8
SCAN_CHUNK = 256
LANES = 128
VMEM_LIMIT = 52 * 1024 * 1024
ROW_TILE = 256

ADAM_LR = 0.001
ADAM_B1 = 0.9
ADAM_B2 = 0.999
ADAM_EPS = 1e-08
ADAM_WD = 0.01
ADAM_STEP = 10

MESH = pl.DeviceIdType.MESH

NN = (((1,), (0,)), ((), ()))
NT = (((1,), (1,)), ((), ()))
TN = (((0,), (0,)), ((), ()))

_ANY = pl.BlockSpec(memory_space=pl.ANY)


def _params(*sem):
    return pltpu.CompilerParams(dimension_semantics=sem, vmem_limit_bytes=VMEM_LIMIT)


def _pick(dim, pref):
    if dim <= pref:
        return dim
    t = (pref // LANES) * LANES
    while t > LANES and dim % t:
        t -= LANES
    assert dim % t == 0, (dim, pref)
    return t


def _rows_tile(rows, pref):
    t = (min(pref, rows) // 8) * 8
    while t >= 8 and rows % t:
        t -= 8
    return t if t >= 8 else rows


def _dot(a, b, dn=NN):
    return lax.dot_general(a, b, dn, preferred_element_type=F32)


def _split3(a):
    hi = a.astype(BF16)
    r = a - hi.astype(F32)
    mid = r.astype(BF16)
    return hi, mid, (r - mid.astype(F32)).astype(BF16)


def _dot_sel(a, e):
    hi, mid, lo = _split3(a)
    return (_dot(lo, e) + _dot(mid, e)) + _dot(hi, e)


def _sel_dot(e, b):
    hi, mid, lo = _split3(b)
    return (_dot(e, lo) + _dot(e, mid)) + _dot(e, hi)


def _sigmoid(v):
    return 0.5 * jnp.tanh(0.5 * v) + 0.5


def _silu_and_grad(v):
    s = _sigmoid(v)
    return v * s, s * (1.0 + v * (1.0 - s))


def _row(i, shape):
    return lax.broadcasted_iota(jnp.int32, shape, 0) + i


def _sds(shape, dtype):
    return jax.ShapeDtypeStruct(tuple(shape), dtype)


class Job:
    def __init__(self, ins, outs, aliased, nsem, start, finish):
        self.ins, self.outs, self.aliased, self.nsem, self.start, self.finish = ins, outs, aliased, nsem, start, finish


def _place():
    x, y, c = lax.axis_index("x"), lax.axis_index("y"), lax.axis_index("c")
    return x, y, c, [(1 - x, y), (x, 1 - y), (1 - x, 1 - y)]


def _remote(src, dst, send_sem, recv_sem, device):
    return pltpu.make_async_remote_copy(src_ref=src, dst_ref=dst, send_sem=send_sem, recv_sem=recv_sem,
                                        device_id=device, device_id_type=MESH)


def _half(ref, c, cols, lead=0):
    idx = [slice(None)] * lead
    if cols:
        w = ref.shape[-1] // 2
        idx += [slice(None)] * (len(ref.shape) - lead - 1) + [pl.ds(pl.multiple_of(c * w, LANES), w)]
    else:
        idx += [c]
    return ref.at[tuple(idx)]


def _flags(cols, n):
    return list(cols) if cols else [False] * n


def gather_send_job(arrs, cols=None):
    n = len(arrs)
    cols = _flags(cols, n)

    def copies(ins, outs, send, recv):
        x, y, c, chips = _place()
        mine = 2 * x + y
        out = []
        for a in range(n):
            out.append(_remote(ins[a], outs[a].at[mine], send.at[4 * a + 3], recv.at[4 * a + 3], (x, y, 1 - c)))
            for j, chip in enumerate(chips):
                out.append(_remote(_half(ins[a], c, cols[a]), _half(outs[a].at[mine], c, cols[a]),
                                   send.at[4 * a + j], recv.at[4 * a + j], (*chip, c)))
        return out

    def start(ins, outs, send, recv):
        for cp in copies(ins, outs, send, recv):
            cp.start()

    def finish(ins, outs, send, recv):
        x, y, c, chips = _place()
        for a in range(n):
            for j, chip in enumerate(chips):
                landed = _half(outs[a].at[2 * chip[0] + chip[1]], c, cols[a])
                _remote(landed, landed, send.at[4 * a + j], recv.at[4 * a + j], (x, y, 1 - c)).wait_recv()
            twin = outs[a].at[2 * x + y]
            _remote(twin, twin, send.at[4 * a + 3], recv.at[4 * a + 3], (x, y, 1 - c)).wait_recv()
        for cp in copies(ins, outs, send, recv):
            cp.wait_send()

    return Job(list(arrs), [_sds((4,) + a.shape, a.dtype) for a in arrs], False, 4 * n, start, finish)


def gather_rows_job(shard, start, size, into=None):
    rows = pl.ds(start, size)

    def copies(ins, outs, send, recv):
        x, y, c, chips = _place()
        mine = outs[0].at[2 * x + y, rows]
        out = [_remote(ins[0].at[rows], mine, send.at[3], recv.at[3], (x, y, 1 - c))]
        for j, chip in enumerate(chips):
            out.append(_remote(_half(ins[0].at[rows], c, True), _half(mine, c, True), send.at[j], recv.at[j],
                               (*chip, c)))
        return out

    def start_(ins, outs, send, recv):
        for cp in copies(ins, outs, send, recv):
            cp.start()

    def finish(ins, outs, send, recv):
        x, y, c, chips = _place()
        for j, chip in enumerate(chips):
            landed = _half(outs[0].at[2 * chip[0] + chip[1], rows], c, True)
            _remote(landed, landed, send.at[j], recv.at[j], (x, y, 1 - c)).wait_recv()
        twin = outs[0].at[2 * x + y, rows]
        _remote(twin, twin, send.at[3], recv.at[3], (x, y, 1 - c)).wait_recv()
        for cp in copies(ins, outs, send, recv):
            cp.wait_send()

    ins = [shard] if into is None else [shard, into]
    return Job(ins, [_sds((4,) + shard.shape, shard.dtype)], {} if into is None else {1: 0}, 4, start_, finish)


def gather_pass_job(bufs, cols=None):
    n = len(bufs)
    cols = _flags(cols, n)

    def copies(outs, send, recv):
        x, y, c, chips = _place()
        out = []
        for a in range(n):
            for j, chip in enumerate(chips):
                landed = _half(outs[a].at[2 * chip[0] + chip[1]], c, cols[a])
                out.append(_remote(landed, landed, send.at[3 * a + j], recv.at[3 * a + j], (x, y, 1 - c)))
        return out

    def start(ins, outs, send, recv):
        for cp in copies(outs, send, recv):
            cp.start()

    def finish(ins, outs, send, recv):
        x, y, c, chips = _place()
        for a in range(n):
            for j, chip in enumerate(chips):
                passed = _half(outs[a].at[2 * chip[0] + chip[1]], 1 - c, cols[a])
                _remote(passed, passed, send.at[3 * a + j], recv.at[3 * a + j], (x, y, 1 - c)).wait_recv()
        for cp in copies(outs, send, recv):
            cp.wait_send()

    return Job(list(bufs), [_sds(b.shape, b.dtype) for b in bufs], True, 3 * n, start, finish)


def pair_exchange_job(arrs, cols=None):
    n = len(arrs)
    cols = _flags(cols, n)

    def copies(ins, outs, send, recv):
        x, y, c, _ = _place()
        return [_remote(_half(ins[a], 1 - c, cols[a], 1), outs[a], send.at[a], recv.at[a], (x, y, 1 - c))
                for a in range(n)]

    def start(ins, outs, send, recv):
        for cp in copies(ins, outs, send, recv):
            cp.start()

    def finish(ins, outs, send, recv):
        for cp in copies(ins, outs, send, recv):
            cp.wait()

    shape = lambda a, k: a.shape[:-1] + (a.shape[-1] // 2,) if k else a.shape[:1] + a.shape[2:]
    return Job(list(arrs), [_sds(shape(a, k), a.dtype) for a, k in zip(arrs, cols)], False, n, start, finish)


def chip_exchange_job(arrs):
    n = len(arrs)

    def copies(ins, outs, send, recv):
        x, y, c, chips = _place()
        return [_remote(ins[a].at[2 * chip[0] + chip[1]], outs[a].at[j], send.at[3 * a + j], recv.at[3 * a + j],
                        (*chip, c)) for a in range(n) for j, chip in enumerate(chips)]

    def start(ins, outs, send, recv):
        for cp in copies(ins, outs, send, recv):
            cp.start()

    def finish(ins, outs, send, recv):
        for cp in copies(ins, outs, send, recv):
            cp.wait()

    return Job(list(arrs), [_sds((3,) + a.shape[1:], a.dtype) for a in arrs], False, 3 * n, start, finish)


def pair_gather_job(bufs, cols=None):
    n = len(bufs)
    cols = _flags(cols, n)

    def copies(outs, send, recv):
        x, y, c, _ = _place()
        return [_remote(_half(outs[a], c, cols[a]), _half(outs[a], c, cols[a]), send.at[a], recv.at[a],
                        (x, y, 1 - c)) for a in range(n)]

    def start(ins, outs, send, recv):
        for cp in copies(outs, send, recv):
            cp.start()

    def finish(ins, outs, send, recv):
        for cp in copies(outs, send, recv):
            cp.wait()

    return Job(list(bufs), [_sds(b.shape, b.dtype) for b in bufs], True, n, start, finish)


def _call(body, *, grid, in_specs, out_specs, out_shape, name, sem, args, scratch_shapes=(), jobs=(), aliases=None):
    in_specs, out_specs, out_shape, scratch_shapes = list(in_specs), list(out_specs), list(out_shape), list(scratch_shapes)
    aliases = dict(aliases or {})
    n_in, n_out, n_scr = len(in_specs), len(out_specs), len(scratch_shapes)
    if jobs:
        sem = ("arbitrary",) * len(grid)
    at_in, at_out = n_in, n_out
    for j in jobs:
        pairs = j.aliased if isinstance(j.aliased, dict) else {i: i for i in range(len(j.ins))} if j.aliased else {}
        aliases.update({at_in + i: at_out + o for i, o in pairs.items()})
        at_in, at_out = at_in + len(j.ins), at_out + len(j.outs)

    def wrapped(*refs):
        ins, p = refs[:n_in], n_in
        jins = []
        for j in jobs:
            jins.append(refs[p:p + len(j.ins)])
            p += len(j.ins)
        outs, p = refs[p:p + n_out], p + n_out
        jouts = []
        for j in jobs:
            jouts.append(refs[p:p + len(j.outs)])
            p += len(j.outs)
        scr, sems = refs[p:p + n_scr], refs[p + n_scr:]

        def start():
            for k, j in enumerate(jobs):
                j.start(jins[k], jouts[k], sems[2 * k], sems[2 * k + 1])

        def finish():
            for k, j in enumerate(jobs):
                j.finish(jins[k], jouts[k], sems[2 * k], sems[2 * k + 1])

        if jobs and grid:
            ids = [pl.program_id(d) for d in range(len(grid))]
            pl.when(functools.reduce(jnp.logical_and, [i == 0 for i in ids]))(start)
            body(*ins, *outs, *scr)
            pl.when(functools.reduce(jnp.logical_and, [i == g - 1 for i, g in zip(ids, grid)]))(finish)
        else:
            start()
            body(*ins, *outs, *scr)
            finish()

    kwargs = dict(grid=grid) if grid else {}
    res = pl.pallas_call(
        wrapped, in_specs=in_specs + [_ANY] * (at_in - n_in), out_specs=out_specs + [_ANY] * (at_out - n_out),
        out_shape=out_shape + [o for j in jobs for o in j.outs],
        scratch_shapes=scratch_shapes + [pltpu.SemaphoreType.DMA((j.nsem,)) for j in jobs for _ in range(2)],
        input_output_aliases=aliases, name=name,
        compiler_params=pltpu.CompilerParams(dimension_semantics=sem, vmem_limit_bytes=VMEM_LIMIT) if grid
        else pltpu.CompilerParams(vmem_limit_bytes=VMEM_LIMIT), **kwargs)(*args, *[a for j in jobs for a in j.ins])
    res = list(res)
    outs, rest, per_job = res[:n_out], res[n_out:], []
    for j in jobs:
        per_job.append(rest[:len(j.outs)])
        rest = rest[len(j.outs):]
    return outs, per_job


def run_jobs(jobs, name):
    return _call(lambda: None, grid=(), in_specs=[], out_specs=[], out_shape=[], name=name, sem=(), args=(), jobs=jobs)[1]


def pair_add(g, r, core, out_dtype, cols=False):
    S, R, C = r.shape
    T = _rows_tile(R, ROW_TILE if C <= 4096 else ROW_TILE // 2)

    def body(c_ref, g_ref, r_ref, o_ref):
        o_ref[0] = ((g_ref[0] if cols else g_ref[0, 0]).astype(F32) + r_ref[0].astype(F32)).astype(o_ref.dtype)

    g_spec = (pl.BlockSpec((1, T, C), lambda k, i, c_ref: (k, i, c_ref[0])) if cols
              else pl.BlockSpec((1, 1, T, C), lambda k, i, c_ref: (k, c_ref[0], i, 0)))
    return pl.pallas_call(
        body,
        grid_spec=pltpu.PrefetchScalarGridSpec(
            num_scalar_prefetch=1, grid=(S, R // T),
            in_specs=[g_spec, pl.BlockSpec((1, T, C), lambda k, i, c_ref: (k, i, 0))],
            out_specs=pl.BlockSpec((1, T, C), lambda k, i, c_ref: (k, i, 0))),
        out_shape=_sds((S, R, C), out_dtype), name="pair_add",
        compiler_params=_params("parallel", "parallel"))(core, g, r)


def chip_add(p, r, chip, core, cols=False):
    _, R, C = p.shape
    T = _rows_tile(R, ROW_TILE)

    def body(k_ref, c_ref, p_ref, r0_ref, r1_ref, r2_ref, o_ref):
        s = ((p_ref[0].astype(F32) + r0_ref[0].astype(F32)) + r1_ref[0].astype(F32)) + r2_ref[0].astype(F32)
        if cols:
            o_ref[...] = s
        else:
            o_ref[0] = s

    slot = lambda j: pl.BlockSpec((1, T, C), lambda i, k_ref, c_ref: (j, i, 0))
    out_spec = (pl.BlockSpec((T, C), lambda i, k_ref, c_ref: (i, c_ref[0])) if cols
                else pl.BlockSpec((1, T, C), lambda i, k_ref, c_ref: (c_ref[0], i, 0)))
    return pl.pallas_call(
        body,
        grid_spec=pltpu.PrefetchScalarGridSpec(
            num_scalar_prefetch=2, grid=(R // T,),
            in_specs=[pl.BlockSpec((1, T, C), lambda i, k_ref, c_ref: (k_ref[0], i, 0)), slot(0), slot(1), slot(2)],
            out_specs=out_spec),
        out_shape=_sds((R, 2 * C) if cols else (2, R, C), F32), name="chip_add",
        compiler_params=_params("parallel"))(chip, core, p, r, r, r)


def allreduce_small(v):
    R = v.shape[0]

    def body(v_ref, o_ref, buf, send_sems, recv_sems, local_sem):
        x, y, c, chips = _place()
        me, sibling = (x, y, c), (x, y, 1 - c)

        def rows(px, py, pc):
            return buf.at[pl.ds((4 * px + 2 * py + pc) * R, R), :]

        def copy(k, block, to, src=None):
            return _remote(rows(*block) if src is None else src, rows(*block), send_sems.at[k], recv_sems.at[k], to)

        mine = pltpu.make_async_copy(v_ref, rows(*me), local_sem)
        mine.start()
        first = [copy(0, me, sibling, src=v_ref)]
        first += [copy(1 + j, me, (*chip, c), src=v_ref) for j, chip in enumerate(chips)]
        for cp in first:
            cp.start()
        passed = [copy(4 + j, (*chip, c), sibling) for j, chip in enumerate(chips)]
        for j, chip in enumerate(chips):
            copy(1 + j, (*chip, c), me).wait_recv()
            passed[j].start()
        copy(0, sibling, me).wait_recv()
        for j, chip in enumerate(chips):
            copy(4 + j, (*chip, 1 - c), me).wait_recv()
        for cp in first + passed:
            cp.wait_send()
        mine.wait()
        acc = buf[0:R, :]
        for d in range(1, 8):
            acc = acc + buf[d * R:(d + 1) * R, :]
        o_ref[...] = acc

    return pl.pallas_call(
        body, in_specs=[pl.BlockSpec(memory_space=pltpu.VMEM)], out_specs=pl.BlockSpec(memory_space=pltpu.VMEM),
        out_shape=_sds((R, LANES), F32),
        scratch_shapes=[pltpu.VMEM((8 * R, LANES), F32), pltpu.SemaphoreType.DMA((7,)),
                        pltpu.SemaphoreType.DMA((7,)), pltpu.SemaphoreType.DMA],
        name="allreduce_small", compiler_params=pltpu.CompilerParams(vmem_limit_bytes=VMEM_LIMIT))(v)


def matmul(a, b, mode, out_dtype, name, tm=512, tn=1024, tk=4608, jobs=(), n_out=None):
    if mode == "nn":
        (M, K), (K2, N) = a.shape, b.shape
    elif mode == "nt":
        (M, K), (N, K2) = a.shape, b.shape
        N = n_out or N
    else:
        (K, M), (K2, N) = a.shape, b.shape
    assert K == K2 or (mode == "nn" and K2 > K)
    tm, tn, tk = _pick(M, tm), _pick(N, tn), _pick(K, tk)
    nk = K // tk
    dn = {"nn": NN, "nt": NT, "tn": TN}[mode]

    def body(a_ref, b_ref, o_ref, *acc):
        part = _dot(a_ref[...].astype(BF16), b_ref[...].astype(BF16), dn)
        if nk == 1:
            o_ref[...] = part.astype(o_ref.dtype)
            return
        acc_ref, = acc
        k = pl.program_id(2)

        @pl.when(k == 0)
        def _():
            acc_ref[...] = part

        @pl.when(jnp.logical_and(k > 0, k < nk - 1))
        def _():
            acc_ref[...] += part

        @pl.when(k == nk - 1)
        def _():
            o_ref[...] = (acc_ref[...] + part).astype(o_ref.dtype)

    a_spec = (pl.BlockSpec((tk, tm), lambda i, j, k: (k, i)) if mode == "tn"
              else pl.BlockSpec((tm, tk), lambda i, j, k: (i, k)))
    b_spec = (pl.BlockSpec((tn, tk), lambda i, j, k: (j, k)) if mode == "nt"
              else pl.BlockSpec((tk, tn), lambda i, j, k: (k, j)))
    outs, per_job = _call(
        body, grid=(M // tm, N // tn, nk), in_specs=[a_spec, b_spec],
        out_specs=[pl.BlockSpec((tm, tn), lambda i, j, k: (i, j))], out_shape=[_sds((M, N), out_dtype)],
        scratch_shapes=[pltpu.VMEM((tm, tn), F32)] if nk > 1 else [], name=name,
        sem=("parallel", "parallel", "arbitrary"), args=(a, b), jobs=jobs)
    return outs[0], per_job


def rms_fwd(x, w):
    L, D = x.shape
    T = _pick(L, ROW_TILE)

    def body(x_ref, w_ref, h_ref):
        xv = x_ref[...]
        r = lax.rsqrt(jnp.mean(xv * xv, axis=-1, keepdims=True) + NORM_EPS)
        h_ref[...] = (xv * r * w_ref[...]).astype(h_ref.dtype)

    return pl.pallas_call(
        body, grid=(L // T,),
        in_specs=[pl.BlockSpec((T, D), lambda i: (i, 0)), pl.BlockSpec((1, D), lambda i: (0, 0))],
        out_specs=pl.BlockSpec((T, D), lambda i: (i, 0)),
        out_shape=_sds((L, D), BF16), name="rms_fwd", compiler_params=_params("parallel"))(x, w)


def post_fwd(x, o, w, next_w=None, jobs=()):
    L, D = x.shape
    T = _pick(L, ROW_TILE)

    def body(x_ref, o_ref, w_ref, *rest):
        ov = o_ref[...]
        r = lax.rsqrt(jnp.mean(ov * ov, axis=-1, keepdims=True) + NORM_EPS)
        y = x_ref[...] + ov * r * w_ref[...]
        rest[-1 if next_w is None else -2][...] = y
        if next_w is not None:
            r2 = lax.rsqrt(jnp.mean(y * y, axis=-1, keepdims=True) + NORM_EPS)
            rest[-1][...] = (y * r2 * rest[0][...]).astype(BF16)

    row = pl.BlockSpec((T, D), lambda i: (i, 0))
    vec = pl.BlockSpec((1, D), lambda i: (0, 0))
    more = [] if next_w is None else [next_w]
    outs, per_job = _call(
        body, grid=(L // T,), in_specs=[row, row, vec] + [vec] * len(more), out_specs=[row] * (1 + len(more)),
        out_shape=[_sds((L, D), F32)] + [_sds((L, D), BF16)] * len(more), name="post_fwd", sem=("parallel",),
        args=(x, o, w, *more), jobs=jobs)
    return (outs[0], outs[1] if more else None), per_job


def _rms_bwd_math(xv, w, dy):
    r = lax.rsqrt(jnp.mean(xv * xv, axis=-1, keepdims=True) + NORM_EPS)
    xhat = xv * r
    g = dy * w
    dx = r * (g - xhat * jnp.mean(g * xhat, axis=-1, keepdims=True))
    return dx, jnp.sum(dy * xhat, axis=0, keepdims=True)


def post_bwd(o, w, dxn):
    L, D = o.shape
    T = _pick(L, ROW_TILE)

    def body(o_ref, w_ref, d_ref, do_ref, dw_ref):
        dx, dw = _rms_bwd_math(o_ref[...], w_ref[...], d_ref[...])
        do_ref[...] = dx.astype(do_ref.dtype)

        @pl.when(pl.program_id(0) == 0)
        def _():
            dw_ref[...] = jnp.zeros_like(dw_ref)

        dw_ref[...] += dw

    row = pl.BlockSpec((T, D), lambda i: (i, 0))
    vec = pl.BlockSpec((1, D), lambda i: (0, 0))
    return pl.pallas_call(
        body, grid=(L // T,), in_specs=[row, vec, row], out_specs=[row, vec],
        out_shape=[_sds((L, D), BF16), _sds((1, D), F32)],
        name="post_bwd", compiler_params=_params("arbitrary"))(o, w, dxn)


def rms_bwd(x, w, dh_a, dh_b, dxn, below=None, jobs=()):
    L, D = x.shape
    T = _pick(L, ROW_TILE)

    def body(x_ref, w_ref, a_ref, b_ref, d_ref, *rest):
        dx, dw = _rms_bwd_math(x_ref[...], w_ref[...], a_ref[...] + b_ref[...])
        dx = d_ref[...] + dx
        outs = rest[2:] if below else rest
        outs[0][...] = dx

        @pl.when(pl.program_id(0) == 0)
        def _():
            for acc in outs[1::2]:
                acc[...] = jnp.zeros_like(acc)

        outs[1][...] += dw
        if below:
            do, dwp = _rms_bwd_math(rest[0][...], rest[1][...], dx)
            outs[2][...] = do.astype(BF16)
            outs[3][...] += dwp

    row = pl.BlockSpec((T, D), lambda i: (i, 0))
    vec = pl.BlockSpec((1, D), lambda i: (0, 0))
    more = list(below) if below else []
    return _call(
        body, grid=(L // T,), in_specs=[row, vec, row, row, row] + ([row, vec] if below else []),
        out_specs=[row, vec] + ([row, vec] if below else []),
        out_shape=[_sds((L, D), F32), _sds((1, D), F32)] + ([_sds((L, D), BF16), _sds((1, D), F32)] if below else []),
        name="rms_bwd", sem=("arbitrary",), args=(x, w, dh_a, dh_b, dxn, *more), jobs=jobs)


def loss_head(y, target):
    L, D = y.shape
    T = _pick(L, ROW_TILE)

    def body(y_ref, t_ref, d_ref, s_ref):
        e = y_ref[...] - t_ref[...]
        d_ref[...] = e * (1.0 / D)

        @pl.when(pl.program_id(0) == 0)
        def _():
            s_ref[...] = jnp.zeros_like(s_ref)

        s_ref[...] += jnp.sum(e * e)

    row = pl.BlockSpec((T, D), lambda i: (i, 0))
    return pl.pallas_call(
        body, grid=(L // T,), in_specs=[row, row],
        out_specs=[row, pl.BlockSpec((8, LANES), lambda i: (0, 0))],
        out_shape=[_sds((L, D), F32), _sds((8, LANES), F32)],
        name="loss_head", compiler_params=_params("arbitrary"))(y, target)


def _window_sums(xe, w, back):
    n = xe.shape[0]
    s, k = xe, 1
    while k < w:
        s = s + pltpu.roll(s, k if back else n - k, 0)
        k *= 2
    return s


def pool_fwd(proj, mixw, scale, D):
    L = proj.shape[0]
    PGW = D // GROUPS
    T = _pick(L, ROW_TILE)
    hb = T // POOL_HALO

    def body(u_ref, halo_ref, g_ref, mw_ref, sc_ref, y_ref, p_ref):
        i = pl.program_id(0)
        u = u_ref[...]
        halo = jnp.where(i > 0, halo_ref[...], 0.0)
        xe = jnp.concatenate([halo, u], axis=0)
        t1 = _row(i * T + 1, (T, 1))
        for g, w in enumerate(POOL_WINDOWS):
            sl = slice(g * PGW, (g + 1) * PGW)
            win = _window_sums(xe[:, sl], w, True)[POOL_HALO:, :]
            cnt = jnp.minimum(t1, w).astype(F32)
            pooled = (win / cnt - u[:, sl]).astype(BF16)
            p_ref[:, sl] = pooled
            mixed = _dot(pooled, mw_ref[g])
            gate = g_ref[:, sl]
            y_ref[:, sl] = (mixed * sc_ref[:, sl] * (gate * _sigmoid(gate))).astype(BF16)

    return pl.pallas_call(
        body, grid=(L // T,),
        in_specs=[pl.BlockSpec((T, D), lambda i: (i, 0)),
                  pl.BlockSpec((POOL_HALO, D), lambda i: (jnp.maximum(i * hb - 1, 0), 0)),
                  pl.BlockSpec((T, D), lambda i: (i, 1)),
                  pl.BlockSpec((GROUPS, PGW, PGW), lambda i: (0, 0, 0)),
                  pl.BlockSpec((1, D), lambda i: (0, 0))],
        out_specs=[pl.BlockSpec((T, D), lambda i: (i, 0)), pl.BlockSpec((T, D), lambda i: (i, 0))],
        out_shape=[_sds((L, 2 * D), BF16), _sds((L, D), BF16)],
        name="pool_fwd", compiler_params=_params("parallel"))(proj, proj, proj, mixw, scale)


def pool_bwd(proj, dmix, pooled, mixw, scale, dproj, D):
    L = proj.shape[0]
    PGW = D // GROUPS
    T = _pick(L, ROW_TILE)
    hb = T // POOL_HALO
    nT = L // T

    def body(g_ref, gh_ref, dy_ref, dyh_ref, p_ref, mw_ref, sc_ref, old_ref, dp_ref, dm_ref, ds_ref):
        i = pl.program_id(0)
        t1 = _row(i * T + 1, (T, 1))
        th1 = _row((i + 1) * T + 1, (POOL_HALO, 1))
        live = i < nT - 1

        @pl.when(i == 0)
        def _():
            ds_ref[...] = jnp.zeros_like(ds_ref)
            dm_ref[...] = jnp.zeros_like(dm_ref)

        for g, w in enumerate(POOL_WINDOWS):
            sl = slice(g * PGW, (g + 1) * PGW)
            sc = sc_ref[:, sl]
            gate, dy = g_ref[:, sl], dy_ref[:, sl]
            sg, dsg = _silu_and_grad(gate)
            pooled = p_ref[:, sl]
            mixed = _dot(pooled, mw_ref[g])
            dmixed = (dy * sc * sg).astype(BF16)
            dm_ref[g] += _dot(pooled, dmixed, TN)
            dp_ref[:, D + g * PGW:D + (g + 1) * PGW] = (dy * mixed * sc * dsg).astype(BF16)
            ds_ref[:, sl] += jnp.sum(dy * mixed * sg, axis=0, keepdims=True)
            dpool = _dot(dmixed, mw_ref[g], NT)
            gate_h = gh_ref[:, sl]
            dmixed_h = (dyh_ref[:, sl] * sc * (gate_h * _sigmoid(gate_h))).astype(BF16)
            dpool_h = jnp.where(live, _dot(dmixed_h, mw_ref[g], NT), 0.0)
            q = dpool / jnp.minimum(t1, w).astype(F32)
            q_h = dpool_h / jnp.minimum(th1, w).astype(F32)
            qe = jnp.concatenate([q, q_h], axis=0)
            dp_ref[:, sl] = (_window_sums(qe, w, False)[:T, :] - dpool).astype(BF16)

    nxt = lambda i: jnp.minimum((i + 1) * hb, L // POOL_HALO - 1)
    row = lambda c: pl.BlockSpec((T, D), lambda i: (i, c))
    return pl.pallas_call(
        body, grid=(nT,),
        in_specs=[row(1), pl.BlockSpec((POOL_HALO, D), lambda i: (nxt(i), 1)),
                  row(0), pl.BlockSpec((POOL_HALO, D), lambda i: (nxt(i), 0)),
                  row(0), pl.BlockSpec((GROUPS, PGW, PGW), lambda i: (0, 0, 0)),
                  pl.BlockSpec((1, D), lambda i: (0, 0)), _ANY],
        out_specs=[pl.BlockSpec((T, 2 * D), lambda i: (i, 0)), pl.BlockSpec((GROUPS, PGW, PGW), lambda i: (0, 0, 0)),
                   pl.BlockSpec((1, D), lambda i: (0, 0))],
        out_shape=[_sds(dproj.shape, dproj.dtype), _sds((GROUPS, PGW, PGW), F32), _sds((1, D), F32)],
        input_output_aliases={7: 0},
        name="pool_bwd", compiler_params=_params("arbitrary"))(proj, proj, dmix, dmix, pooled, mixw, scale, dproj)


def conv_fwd(proj, cw, cb, D):
    L = proj.shape[0]
    C = cw.shape[1]
    assert (3 * D) % C == 0
    cblk = (3 * D) // C
    T = _pick(L, ROW_TILE)
    hb = T // CONV_HALO

    def body(u_ref, halo_ref, w_ref, b_ref, o_ref):
        i = pl.program_id(0)
        u = u_ref[...]
        xe = jnp.concatenate([jnp.where(i > 0, halo_ref[...], 0.0), u], axis=0)
        acc = b_ref[...] + w_ref[CONV_K - 1:CONV_K, :] * u
        for k in range(CONV_K - 1):
            acc = acc + w_ref[k:k + 1, :] * pltpu.roll(xe, CONV_K - 1 - k, 0)[CONV_HALO:, :]
        o_ref[...] = acc

    return pl.pallas_call(
        body, grid=(L // T,),
        in_specs=[pl.BlockSpec((T, C), lambda i: (i, cblk)),
                  pl.BlockSpec((CONV_HALO, C), lambda i: (jnp.maximum(i * hb - 1, 0), cblk)),
                  pl.BlockSpec((CONV_K, C), lambda i: (0, 0)),
                  pl.BlockSpec((1, C), lambda i: (0, 0))],
        out_specs=pl.BlockSpec((T, C), lambda i: (i, 0)),
        out_shape=_sds((L, C), F32), name="conv_fwd", compiler_params=_params("parallel"))(proj, proj, cw, cb)


def conv_bwd(dparts, proj, cw, dproj, D, jobs=()):
    L = proj.shape[0]
    C = cw.shape[1]
    cblk = (3 * D) // C
    T = _pick(L, ROW_TILE)
    hb = T // CONV_HALO
    nT = L // T
    widths = [p.shape[1] for p in dparts]
    assert sum(widths) == C
    n = len(dparts)

    def body(*refs):
        d_refs, dn_refs = refs[:n], refs[n:2 * n]
        u_ref, up_ref, w_ref, old_ref, dr_ref, dw_ref, db_ref = refs[2 * n:]
        i = pl.program_id(0)

        @pl.when(i == 0)
        def _():
            dw_ref[...] = jnp.zeros_like(dw_ref)
            db_ref[...] = jnp.zeros_like(db_ref)

        at = 0
        for d_ref, dn_ref, wd in zip(d_refs, dn_refs, widths):
            sl = slice(at, at + wd)
            at += wd
            d = d_ref[...]
            u = u_ref[:, sl]
            de = jnp.concatenate([d, jnp.where(i < nT - 1, dn_ref[...], 0.0)], axis=0)
            ue = jnp.concatenate([jnp.where(i > 0, up_ref[:, sl], 0.0), u], axis=0)
            acc = w_ref[CONV_K - 1:CONV_K, sl] * d
            dw_ref[CONV_K - 1:CONV_K, sl] += jnp.sum(d * u, axis=0, keepdims=True)
            for k in range(CONV_K - 1):
                sh = CONV_K - 1 - k
                acc = acc + w_ref[k:k + 1, sl] * pltpu.roll(de, T + CONV_HALO - sh, 0)[:T, :]
                dw_ref[k:k + 1, sl] += jnp.sum(d * pltpu.roll(ue, sh, 0)[CONV_HALO:, :], axis=0, keepdims=True)
            dr_ref[:, sl] = acc.astype(dr_ref.dtype)
            db_ref[:, sl] += jnp.sum(d, axis=0, keepdims=True)

    nxt = lambda i: jnp.minimum((i + 1) * hb, L // CONV_HALO - 1)
    return _call(
        body, grid=(nT,),
        in_specs=[pl.BlockSpec((T, wd), lambda i: (i, 0)) for wd in widths]
        + [pl.BlockSpec((CONV_HALO, wd), lambda i: (nxt(i), 0)) for wd in widths]
        + [pl.BlockSpec((T, C), lambda i: (i, cblk)),
           pl.BlockSpec((CONV_HALO, C), lambda i: (jnp.maximum(i * hb - 1, 0), cblk)),
           pl.BlockSpec((CONV_K, C), lambda i: (0, 0)), _ANY],
        out_specs=[pl.BlockSpec((T, C), lambda i: (i, cblk)),
                   pl.BlockSpec((8, C), lambda i: (0, 0)),
                   pl.BlockSpec((1, C), lambda i: (0, 0))],
        out_shape=[_sds(dproj.shape, dproj.dtype), _sds((8, C), F32), _sds((1, C), F32)],
        aliases={2 * n + 3: 0}, name="conv_bwd", sem=("arbitrary",),
        args=(*dparts, *dparts, proj, proj, cw, dproj), jobs=jobs)


def _softplus(v):
    y = jnp.exp(-jnp.abs(v))
    u = 1.0 + y
    log1p = jnp.where(u == 1.0, y, jnp.log(u) * y / jnp.where(u == 1.0, 1.0, u - 1.0))
    return jnp.maximum(v, 0.0) + log1p


def dt_prep(dtraw, bias, alog, expand, D):
    L = dtraw.shape[0]
    GC = D // GROUPS
    HPG = GC // HEAD_DIM
    Q = _pick(L, SCAN_CHUNK)
    nc = L // Q

    def body(r_ref, b_ref, a_ref, e_ref, dt_ref, acs_ref, acst_ref, dtx_ref, eax_ref, dsx_ref, cdx_ref):
        valid = lax.broadcasted_iota(jnp.int32, (1, LANES), 1) < HPG
        dt = jnp.where(valid, _softplus(r_ref[...] + b_ref[...]), 0.0)
        adt = dt * -jnp.exp(a_ref[...])
        tril = (_row(0, (Q, Q)) >= lax.broadcasted_iota(jnp.int32, (Q, Q), 1)).astype(BF16)
        acs = _sel_dot(tril, adt)
        last = acs[Q - 1:Q, :]
        dt_ref[...] = dt
        acs_ref[...] = acs
        acst_ref[...] = acs.T
        e = e_ref[...]
        dtx_ref[...] = _dot_sel(dt, e)
        eax_ref[...] = jnp.exp(_dot_sel(acs, e))
        dsx_ref[...] = jnp.exp(_dot_sel(last - acs, e))
        cdx_ref[0] = jnp.exp(_dot_sel(jnp.broadcast_to(last, (8, LANES)), e))

    head = pl.BlockSpec((Q, LANES), lambda g, c: (c, g))
    hvec = pl.BlockSpec((1, LANES), lambda g, c: (0, g))
    chan = pl.BlockSpec((Q, GC), lambda g, c: (c, g))
    return pl.pallas_call(
        body, grid=(GROUPS, nc),
        in_specs=[head, hvec, hvec, pl.BlockSpec((LANES, GC), lambda g, c: (0, 0))],
        out_specs=[head, head, pl.BlockSpec((LANES, Q), lambda g, c: (g, c)), chan, chan, chan,
                   pl.BlockSpec((1, 8, GC), lambda g, c: (c, 0, g))],
        out_shape=[_sds((L, GROUPS * LANES), F32), _sds((L, GROUPS * LANES), F32), _sds((GROUPS * LANES, L), F32),
                   _sds((L, D), F32), _sds((L, D), F32), _sds((L, D), F32), _sds((nc, 8, D), F32)],
        name="dt_prep", compiler_params=_params("parallel", "parallel"))(dtraw, bias, alog, expand)


def _scan_specs(L, D, Q, rev):
    GC = D // GROUPS
    nc = L // Q
    ci = (lambda c: nc - 1 - c) if rev else (lambda c: c)
    return dict(
        xs=pl.BlockSpec((Q, GC), lambda g, c: (ci(c), g)),
        b=pl.BlockSpec((Q, STATE), lambda g, c: (ci(c), D // STATE + g)),
        c=pl.BlockSpec((Q, STATE), lambda g, c: (ci(c), D // STATE + GROUPS + g)),
        chan=pl.BlockSpec((Q, GC), lambda g, c: (ci(c), g)),
        cdx=pl.BlockSpec((1, 8, GC), lambda g, c: (ci(c), 0, g)),
        head=pl.BlockSpec((Q, LANES), lambda g, c: (ci(c), g)),
        headt=pl.BlockSpec((LANES, Q), lambda g, c: (g, ci(c))),
        state=pl.BlockSpec((1, 1, STATE, GC), lambda g, c: (ci(c), g, 0, 0)),
        hvec=pl.BlockSpec((1, LANES), lambda g, c: (0, g)),
        cvec=pl.BlockSpec((1, GC), lambda g, c: (0, g)))


def scan_fwd(pre, dtx, eax, dsx, cdx, acs, acst, proj, dexp, nw, mixed, D, jobs=()):
    L = pre.shape[0]
    GC = D // GROUPS
    Q = _pick(L, SCAN_CHUNK)
    nc = L // Q
    sp = _scan_specs(L, D, Q, False)

    def body(xs_ref, b_ref, c_ref, dtx_ref, eax_ref, dsx_ref, cdx_ref, acs_ref, acst_ref, z_ref, de_ref, nw_ref,
             old_ref, y_ref, st_ref, o_ref, s_scr):
        @pl.when(pl.program_id(1) == 0)
        def _():
            s_scr[...] = jnp.zeros_like(s_scr)

        tri = _row(0, (Q, Q)) >= lax.broadcasted_iota(jnp.int32, (Q, Q), 1)
        half = lax.broadcasted_iota(jnp.int32, (1, LANES), 1) // HEAD_DIM
        xs, _ = _silu_and_grad(xs_ref[...])
        bg = _silu_and_grad(b_ref[...])[0].astype(BF16)
        cg = _silu_and_grad(c_ref[...])[0].astype(BF16)
        xdt = xs * dtx_ref[...]
        sprev = s_scr[...]
        st_ref[0, 0] = sprev
        sc = _dot(cg, bg, NT)
        yoff = _dot(cg, sprev.astype(BF16)) * eax_ref[...]
        for j in range(GC // LANES):
            ps = slice(j * LANES, (j + 1) * LANES)
            xp = xdt[:, ps]
            acc = yoff[:, ps]
            for hh in range(2):
                h = 2 * j + hh
                lm = jnp.exp(jnp.where(tri, acs_ref[:, h:h + 1] - acst_ref[h:h + 1, :], -1e30))
                xm = jnp.where(half == hh, xp, 0.0).astype(BF16)
                acc = acc + _dot((sc * lm).astype(BF16), xm)
            y_ref[:, ps] = acc
        xw = (xdt * dsx_ref[...]).astype(BF16)
        s_scr[...] = cdx_ref[0, 0:1, :] * sprev + _dot(bg, xw, TN)
        z = z_ref[...]
        y3 = (y_ref[...] + de_ref[...] * xs) * (z * _sigmoid(z))
        r = lax.rsqrt(jnp.mean(y3 * y3, axis=-1, keepdims=True) + NORM_EPS)
        o_ref[...] = (y3 * r * nw_ref[...]).astype(o_ref.dtype)

    return _call(
        body, grid=(GROUPS, nc),
        in_specs=[sp["xs"], sp["b"], sp["c"], sp["chan"], sp["chan"], sp["chan"], sp["cdx"], sp["head"], sp["headt"],
                  pl.BlockSpec((Q, GC), lambda g, c: (c, 2 * GROUPS + g)), sp["cvec"], sp["cvec"], _ANY],
        out_specs=[sp["chan"], sp["state"], pl.BlockSpec((Q, GC), lambda g, c: (c, GROUPS + g))],
        out_shape=[_sds((L, D), F32), _sds((nc, GROUPS, STATE, GC), F32), _sds(mixed.shape, mixed.dtype)],
        aliases={12: 2}, scratch_shapes=[pltpu.VMEM((STATE, GC), F32)], name="scan_fwd",
        sem=("parallel", "arbitrary"), args=(pre, pre, pre, dtx, eax, dsx, cdx, acs, acst, proj, dexp, nw, mixed),
        jobs=jobs)


def scan_bwd(pre, dtx, eax, dsx, cdx, acs, acst, dt, dtraw, bias, alog, states, y, proj, dmix, nw, dexp, collapse, D,
             jobs=()):
    L = pre.shape[0]
    GC = D // GROUPS
    Q = _pick(L, SCAN_CHUNK)
    nc = L // Q
    sp = _scan_specs(L, D, Q, True)
    rc = lambda c: nc - 1 - c

    def body(xs_ref, b_ref, c_ref, dtx_ref, eax_ref, dsx_ref, cdx_ref, acs_ref, acst_ref, dt_ref, raw_ref,
             bias_ref, alog_ref, st_ref, y_ref, z_ref, dm_ref, nw_ref, dexp_ref, col_ref,
             dxs_ref, db_ref, dc_ref, ddt_ref, dal_ref, dbi_ref, dz_ref, dnw_ref, dde_ref, ds_scr, dx_scr):
        first = pl.program_id(1) == 0

        @pl.when(first)
        def _():
            ds_scr[...] = jnp.zeros_like(ds_scr)
            dal_ref[...] = jnp.zeros_like(dal_ref)
            dbi_ref[...] = jnp.zeros_like(dbi_ref)
            dnw_ref[...] = jnp.zeros_like(dnw_ref)
            dde_ref[...] = jnp.zeros_like(dde_ref)

        li = _row(0, (Q, Q))
        si = lax.broadcasted_iota(jnp.int32, (Q, Q), 1)
        lane = lax.broadcasted_iota(jnp.int32, (1, LANES), 1)
        half = lane // HEAD_DIM
        xs_pre, b_pre, c_pre = xs_ref[...], b_ref[...], c_ref[...]
        xs, xs_g = _silu_and_grad(xs_pre)
        bf, b_g = _silu_and_grad(b_pre)
        cf, c_g = _silu_and_grad(c_pre)
        bg, cg = bf.astype(BF16), cf.astype(BF16)
        dtx, eax, dsx = dtx_ref[...], eax_ref[...], dsx_ref[...]
        cd = cdx_ref[0, 0:1, :]
        xdt = xs * dtx
        sz, dsz = _silu_and_grad(z_ref[...])
        y2 = y_ref[...] + dexp_ref[...] * xs
        y3 = y2 * sz
        r = lax.rsqrt(jnp.mean(y3 * y3, axis=-1, keepdims=True) + NORM_EPS)
        n = y3 * r
        dm = dm_ref[...]
        gg = dm * nw_ref[...]
        dy3 = r * (gg - n * jnp.mean(gg * n, axis=-1, keepdims=True))
        dnw_ref[0:1, :] += jnp.sum(dm * n, axis=0, keepdims=True)
        G = dy3 * sz
        dz_ref[...] = (dy3 * y2 * dsz).astype(dz_ref.dtype)
        dde_ref[0:1, :] += jnp.sum(G * xs, axis=0, keepdims=True)
        prev = st_ref[0, 0]
        dsn = ds_scr[...]
        prev_b, dsn_b = prev.astype(BF16), dsn.astype(BF16)
        cp = _dot(cg, prev_b)
        ge_b = (G * eax).astype(BF16)
        d_c = _dot(ge_b, prev_b, NT)
        dprev = _dot(cg, ge_b, TN) + cd * dsn
        chan_a = G * cp * eax
        xw_b = (xdt * dsx).astype(BF16)
        dcd = jnp.sum(prev * dsn, axis=0, keepdims=True)
        d_b = _dot(xw_b, dsn_b, NT)
        dxw = _dot(bg, dsn_b)
        dd = dxw * xdt * dsx
        chan_a = chan_a - dd
        last_c = jnp.sum(dd, axis=0, keepdims=True) + dcd * cd
        sc = _dot(cg, bg, NT)
        head_row = _row(0, (LANES, 1))
        dsc = jnp.zeros((Q, Q), F32)
        dacs = jnp.zeros((Q, LANES), F32)
        colsums = jnp.zeros((LANES, Q), F32)
        for j in range(GC // LANES):
            ps = slice(j * LANES, (j + 1) * LANES)
            xp, gp = xdt[:, ps], G[:, ps]
            dxp = dxw[:, ps] * dsx[:, ps]
            for hh in range(2):
                h = 2 * j + hh
                lm = jnp.exp(jnp.where(li >= si, acs_ref[:, h:h + 1] - acst_ref[h:h + 1, :], -1e30))
                m = sc * lm
                xm = jnp.where(half == hh, xp, 0.0).astype(BF16)
                gm = jnp.where(half == hh, gp, 0.0).astype(BF16)
                dm = _dot(gm, xm, NT)
                dxp = dxp + _dot(m.astype(BF16), gm, TN)
                dsc = dsc + dm * lm
                w = dm * m
                dacs = dacs + jnp.where(lane == h, jnp.sum(w, axis=1, keepdims=True), 0.0)
                colsums = jnp.where(head_row == h, jnp.sum(w, axis=0, keepdims=True), colsums)
            dx_scr[:, ps] = dxp
        dacs = dacs - colsums.T
        dsc_b = dsc.astype(BF16)
        d_c = d_c + _dot(dsc_b, bg)
        d_b = d_b + _dot(dsc_b, cg, TN)
        ds_scr[...] = dprev
        dxdt = dx_scr[...]
        dxs_ref[...] = (dxdt * dtx + dexp_ref[...] * G) * xs_g
        db_ref[...] = d_b * b_g
        dc_ref[...] = d_c * c_g
        colm = col_ref[...]
        dacs = dacs + _dot_sel(chan_a, colm)
        dlast = _dot_sel(jnp.broadcast_to(last_c, (8, GC)), colm)[0:1, :]
        dacs = dacs + jnp.where(_row(0, (Q, 1)) == Q - 1, dlast, 0.0)
        dadt = _sel_dot((si >= li).astype(BF16), dacs)
        a = -jnp.exp(alog_ref[...])
        dt = dt_ref[...]
        ddt = dadt * a + _dot_sel(dxdt * xs, colm)
        dal_ref[0:1, :] += jnp.sum(dadt * dt * a, axis=0, keepdims=True)
        draw = ddt * _sigmoid(raw_ref[...] + bias_ref[...])
        dbi_ref[0:1, :] += jnp.sum(draw, axis=0, keepdims=True)
        ddt_ref[...] = draw.astype(ddt_ref.dtype)

    acc = pl.BlockSpec((8, LANES), lambda g, c: (0, g))
    cacc = pl.BlockSpec((8, GC), lambda g, c: (0, g))
    return _call(
        body, grid=(GROUPS, nc),
        in_specs=[sp["xs"], sp["b"], sp["c"], sp["chan"], sp["chan"], sp["chan"], sp["cdx"], sp["head"], sp["headt"],
                  sp["head"], sp["head"], sp["hvec"], sp["hvec"], sp["state"], sp["chan"],
                  pl.BlockSpec((Q, GC), lambda g, c: (rc(c), 2 * GROUPS + g)),
                  pl.BlockSpec((Q, GC), lambda g, c: (rc(c), GROUPS + g)), sp["cvec"], sp["cvec"],
                  pl.BlockSpec((GC, LANES), lambda g, c: (0, 0))],
        out_specs=[sp["chan"],
                   pl.BlockSpec((Q, STATE), lambda g, c: (rc(c), g)),
                   pl.BlockSpec((Q, STATE), lambda g, c: (rc(c), g)),
                   sp["head"], acc, acc,
                   pl.BlockSpec((Q, GC), lambda g, c: (rc(c), 2 * GROUPS + g)), cacc, cacc],
        out_shape=[_sds((L, D), F32), _sds((L, GROUPS * STATE), F32), _sds((L, GROUPS * STATE), F32),
                   _sds((L, GROUPS * LANES), BF16), _sds((8, GROUPS * LANES), F32), _sds((8, GROUPS * LANES), F32),
                   _sds((L, proj.shape[1]), BF16), _sds((8, D), F32), _sds((8, D), F32)],
        scratch_shapes=[pltpu.VMEM((STATE, GC), F32), pltpu.VMEM((Q, GC), F32)], name="scan_bwd",
        sem=("parallel", "arbitrary"),
        args=(pre, pre, pre, dtx, eax, dsx, cdx, acs, acst, dt, dtraw, bias, alog, states, y, proj, dmix, nw, dexp,
              collapse),
        jobs=jobs)


def _adam_math(gv, w, m, v):
    c1 = 1.0 - ADAM_B1 ** ADAM_STEP
    c2 = 1.0 - ADAM_B2 ** ADAM_STEP
    nm = ADAM_B1 * m + (1.0 - ADAM_B1) * gv
    nv = ADAM_B2 * v + (1.0 - ADAM_B2) * (gv * gv)
    return -ADAM_LR * ((nm / c1) / (jnp.sqrt(nv / c2) + ADAM_EPS) + ADAM_WD * w), nm, nv


def adamw(g, w, m, v, name):
    R, C = g.shape
    T = R if R <= 128 else 128
    assert R % T == 0

    def body(g_ref, w_ref, m_ref, v_ref, d_ref, nm_ref, nv_ref):
        d_ref[...], nm_ref[...], nv_ref[...] = _adam_math(g_ref[...], w_ref[...], m_ref[...], v_ref[...])

    blk = pl.BlockSpec((T, C), lambda i: (i, 0))
    return pl.pallas_call(
        body, grid=(R // T,), in_specs=[blk] * 4, out_specs=[blk] * 3,
        out_shape=[_sds((R, C), F32)] * 3, name=name, compiler_params=_params("parallel"))(g, w, m, v)


def adamw_layer(g, w, m, v, layer, prev, name):
    R, C = g.shape
    T = _rows_tile(R, ROW_TILE)
    assert R % T == 0

    def body(g_ref, w_ref, m_ref, v_ref, *rest):
        go_ref, d_ref, nm_ref, nv_ref = rest[-4:]
        gv = g_ref[...]
        go_ref[0] = gv
        d_ref[0], nm_ref[0], nv_ref[0] = _adam_math(gv, w_ref[0], m_ref[0], v_ref[0])

    mine = pl.BlockSpec((1, T, C), lambda i: (layer, i, 0))
    prev = list(prev or [])
    return pl.pallas_call(
        body, grid=(R // T,), in_specs=[pl.BlockSpec((T, C), lambda i: (i, 0)), mine, mine, mine] + [_ANY] * len(prev),
        out_specs=[mine] * 4, out_shape=[_sds(w.shape, F32)] * 4,
        input_output_aliases={4 + i: i for i in range(len(prev))}, name=name,
        compiler_params=_params("parallel"))(g, w, m, v, *prev)


def cast_bf16(a):
    R, C = a.shape
    T = _rows_tile(R, ROW_TILE)

    def body(a_ref, o_ref):
        o_ref[...] = a_ref[...].astype(BF16)

    blk = pl.BlockSpec((T, C), lambda i: (i, 0))
    return pl.pallas_call(body, grid=(R // T,), in_specs=[blk], out_specs=blk, out_shape=_sds((R, C), BF16),
                          name="cast_bf16", compiler_params=_params("parallel"))(a)


def _to_groups(v, hpg):
    lead = v.shape[:-1]
    t = v.reshape(lead + (GROUPS, hpg))
    t = jnp.pad(t, [(0, 0)] * (len(lead) + 1) + [(0, LANES - hpg)])
    return t.reshape(lead + (GROUPS * LANES,))


def _from_groups(a, hpg):
    lead = a.shape[:-1]
    return a.reshape(lead + (GROUPS, LANES))[..., :hpg].reshape(lead + (GROUPS * hpg,))


def _expand_matrix(D):
    gc = D // GROUPS
    return (jnp.arange(LANES)[:, None] == (jnp.arange(gc)[None, :] // HEAD_DIM)).astype(BF16)


def layer_params(pre_w, w_in_t, mixw, scale, cw, cb, bias, alog, dskip, nw, w_out_full, post_w, D):
    hpg = D // GROUPS // HEAD_DIM
    main = w_in_t.shape[0] - GROUPS * hpg
    wdt_t = _to_groups(w_in_t[main:].T, hpg).T
    return dict(
        pre_w=pre_w[None], win_t=w_in_t, main=main, wdt_t=wdt_t, mixw=mixw,
        scale=scale[None], cw=cw, cb=cb[None], bias=_to_groups(bias, hpg)[None], alog=_to_groups(alog, hpg)[None],
        dexp=jnp.repeat(dskip, HEAD_DIM)[None], nw=nw[None], wout=w_out_full, post_w=post_w[None])


def layer_fwd(x, p, D, next_shards=None, h=None, next_pre_w=None):
    cut = (next_shards[0].shape[0] // 32) * 16 if next_shards else 0
    send_in = [gather_rows_job(next_shards[0], 0, cut)] if next_shards else []
    if h is None:
        h = rms_fwd(x, p["pre_w"])
    proj, got = matmul(h, p["win_t"], "nt", F32, "proj", tm=1024, jobs=send_in, n_out=p["main"])
    dtraw, _ = matmul(h, p["wdt_t"], "nt", F32, "dtproj")
    mixed, pooled = pool_fwd(proj, p["mixw"], p["scale"], D)
    pre = conv_fwd(proj, p["cw"], p["cb"], D)
    dtp = dt_prep(dtraw, p["bias"], p["alog"], _expand_matrix(D), D)
    dt, acs, acst, dtx, eax, dsx, cdx = dtp
    rest = [gather_rows_job(next_shards[0], cut, next_shards[0].shape[0] - cut, got[0][0])] if next_shards else []
    (y, states, mixed), got = scan_fwd(pre, dtx, eax, dsx, cdx, acs, acst, proj, p["dexp"], p["nw"], mixed, D,
                                       jobs=rest)
    jobs = [gather_send_job(next_shards[1:]), gather_pass_job(got[0], [True])] if next_shards else []
    out, got = matmul(mixed, p["wout"], "nn", F32, "outproj", tm=1024, jobs=jobs)
    (xn, h_next), got2 = post_fwd(x, out, p["post_w"], next_pre_w,
                                  jobs=[gather_pass_job(got[0])] if next_shards else [])
    gathered = got[1] + got2[0] if next_shards else None
    return xn, dict(x=x, h=h, proj=proj, dtraw=dtraw, pooled=pooled, pre=pre, dtp=dtp, y=y, states=states,
                    mixed=mixed, out=out), gathered, h_next


def layer_bwd(dxn, p, s, D, where=None, post=None, below=None):
    reduce = where is not None
    chip, core = where if reduce else (None, None)
    hpg = D // GROUPS // HEAD_DIM
    PGW = D // GROUPS
    main = p["main"]
    SH = (main + GROUPS * hpg) // 4
    dt, acs, acst, dtx, eax, dsx, cdx = s["dtp"]
    dout, d_post = post if post else post_bwd(s["out"], p["post_w"], dxn)
    part = BF16 if reduce else F32
    d_wout, _ = matmul(s["mixed"], dout, "tn", part, "dwout", tm=1024)
    g_out = d_wout.reshape(4, 2, D // 4, D)
    dmix, got = matmul(dout, p["wout"], "nt", F32, "dmixed", tm=1024,
                       jobs=[pair_exchange_job([g_out])] if reduce else [])
    pair_out = pair_add(g_out, got[0][0], core, BF16) if reduce else None
    (dxs, db, dc, ddtraw, d_alog, d_bias, dproj, d_nw, d_dexp), got = scan_bwd(
        s["pre"], dtx, eax, dsx, cdx, acs, acst, dt, s["dtraw"], p["bias"], p["alog"], s["states"], s["y"],
        s["proj"], dmix, p["nw"], p["dexp"], _expand_matrix(D).T, D,
        jobs=[chip_exchange_job([pair_out])] if reduce else [])
    mine_out = chip_add(pair_out, got[0][0], chip, core) if reduce else None
    (dproj, d_cw, d_cb), got = conv_bwd([dxs, db, dc], s["proj"], p["cw"], dproj, D,
                                        jobs=[pair_gather_job([mine_out])] if reduce else [])
    r_out = got[0][0] if reduce else None
    dproj, d_mixw, d_scale = pool_bwd(s["proj"], dmix, s["pooled"], p["mixw"], p["scale"], dproj, D)
    d_wmain_t, _ = matmul(dproj, s["h"], "tn", part, "dwmain", tm=1024)
    d_wdt_t, _ = matmul(ddtraw, s["h"], "tn", part, "dwdt")
    late = [d_wmain_t[None], d_wdt_t[None],
            d_mixw.reshape(GROUPS, 4, PGW // 4, PGW).transpose(1, 0, 2, 3).reshape(4, 2, GROUPS * PGW // 8, PGW)]
    cols = [True, True, False]
    dh_b, got = matmul(ddtraw, p["wdt_t"], "nn", F32, "dh_dt",
                       jobs=[pair_exchange_job(late, cols)] if reduce else [])
    if reduce:
        p_main, p_dt, p_mix = [pair_add(g, r, core, BF16, k) for g, r, k in zip(late, got[0], cols)]
        p_in = jnp.concatenate([p_main[0], _from_groups(p_dt[0].T, hpg).T], axis=0)
        pairs = [p_in.reshape(4, SH, D // 2), p_mix]
    dh_a, got = matmul(dproj, p["win_t"], "nn", F32, "dh_main", jobs=[chip_exchange_job(pairs)] if reduce else [])
    mines = [chip_add(q, r, chip, core, k) for q, r, k in zip(pairs, got[0], [True, False])] if reduce else None
    (dx, d_pre, *post_below), got = rms_bwd(s["x"], p["pre_w"], dh_a, dh_b, dxn, below,
                                            jobs=[pair_gather_job(mines, [True, False])] if reduce else [])
    reduced = dict(w_in=got[0][0], w_out=r_out, pool_mix_w=got[0][1]) if reduce else None
    grads = dict(
        pre_norm_w=d_pre[0], pool_scale=d_scale[0], conv_w=d_cw[:CONV_K], conv_b=d_cb[0],
        dt_bias=_from_groups(d_bias[0], hpg), a_log=_from_groups(d_alog[0], hpg),
        d_skip=d_dexp[0].reshape(-1, HEAD_DIM).sum(axis=-1), ssd_norm_w=d_nw[0], post_norm_w=d_post[0])
    if not reduce:
        grads.update(w_in=jnp.concatenate([d_wmain_t.T, _from_groups(d_wdt_t.T, hpg)], axis=1), pool_mix_w=d_mixw,
                     w_out=d_wout)
    return dx, grads, reduced, tuple(post_below) or None


def local_step(x, target, params, D):
    saved, h, n = [], None, len(params)
    for l, p in enumerate(params):
        x, s, _, h = layer_fwd(x, p, D, h=h, next_pre_w=params[l + 1]["pre_w"] if l + 1 < n else None)
        saved.append(s)
    dx, sumsq = loss_head(x, target)
    grads, post = [None] * n, None
    for l in reversed(range(n)):
        below = (saved[l - 1]["out"], params[l - 1]["post_w"]) if l else None
        dx, grads[l], _, post = layer_bwd(dx, params[l], saved[l], D, post=post, below=below)
    return sumsq, dx, grads


SMALL = ("pre_norm_w", "pool_scale", "conv_w", "conv_b", "dt_bias", "a_log", "d_skip", "ssd_norm_w", "post_norm_w")
BIG = ("w_in", "w_out", "pool_mix_w")


def _pack(parts):
    flat = jnp.concatenate([p.reshape(-1) for p in parts])
    n = flat.shape[0]
    rows = -(-n // (LANES * LANES)) * LANES
    return jnp.pad(flat, (0, rows * LANES - n)).reshape(rows, LANES)


def _unpack(packed, shapes):
    flat, out, at = packed.reshape(-1), [], 0
    for s in shapes:
        n = math.prod(s)
        out.append(flat[at:at + n].reshape(s))
        at += n
    return out


def kernel(x, pre_norm_w, w_in, pool_mix_w, pool_scale, conv_w, conv_b, dt_bias, a_log, d_skip, ssd_norm_w, w_out, post_norm_w, loss_target, m_pre_norm_w, m_w_in, m_pool_mix_w, m_pool_scale, m_conv_w, m_conv_b, m_dt_bias, m_a_log, m_d_skip, m_ssd_norm_w, m_w_out, m_post_norm_w, v_pre_norm_w, v_w_in, v_pool_mix_w, v_pool_scale, v_conv_w, v_conv_b, v_dt_bias, v_a_log, v_d_skip, v_ssd_norm_w, v_w_out, v_post_norm_w):
    NL, D, SH = w_in.shape
    PGW = D // GROUPS
    CS = conv_w.shape[2]
    chip = (2 * lax.axis_index("x") + lax.axis_index("y")).astype(jnp.int32)
    chip1, core = chip.reshape(1), lax.axis_index("c").astype(jnp.int32).reshape(1)

    tr = lambda t: jnp.transpose(t, (0, 2, 1))
    w_in_t, m_w_in_t, v_w_in_t = tr(w_in), tr(m_w_in), tr(v_w_in)
    halved_by_cols = [True, False, False, False]

    def shards(l):
        return [cast_bf16(w_in_t[l]), cast_bf16(w_out[l]).reshape(2, D // 4, D),
                cast_bf16(pool_mix_w[l].reshape(GROUPS * PGW // 4, PGW)).reshape(2, GROUPS * PGW // 8, PGW),
                conv_w[l].reshape(2, CONV_K * CS // (2 * LANES), LANES)]

    def params(l, g):
        g_in, g_out, g_mix, g_cw = g
        mix_l = g_mix.reshape(4, GROUPS, PGW // 4, PGW).transpose(1, 0, 2, 3).reshape(GROUPS, PGW, PGW)
        cw_l = g_cw.reshape(4, CONV_K, CS).transpose(1, 0, 2).reshape(CONV_K, 4 * CS)
        return layer_params(pre_norm_w[l], g_in.reshape(4 * SH, D), mix_l, pool_scale[l], cw_l, conv_b[l], dt_bias[l],
                            a_log[l], d_skip[l], ssd_norm_w[l], g_out.reshape(2 * D, D), post_norm_w[l], D)

    gathered = run_jobs([gather_send_job(shards(0), halved_by_cols)], "gather_send")[0]
    gathered = run_jobs([gather_pass_job(gathered, halved_by_cols)], "gather_pass")[0]
    xl, h, ps, saved = x[0], None, [], []
    for l in range(NL):
        ps.append(params(l, gathered))
        last = l + 1 == NL
        xl, s, gathered, h = layer_fwd(xl, ps[l], D, None if last else shards(l + 1), h,
                                       None if last else pre_norm_w[l + 1][None])
        saved.append(s)
    dx, sumsq = loss_head(xl, loss_target[0])

    given = dict(w_in=(w_in_t, m_w_in_t, v_w_in_t), w_out=(w_out, m_w_out, v_w_out),
                 pool_mix_w=(pool_mix_w, m_pool_mix_w, v_pool_mix_w))
    flat = {n: [t.reshape(NL, -1, t.shape[-1]) for t in given[n]] for n in BIG}
    done = {n: None for n in BIG}
    grads, post = [None] * NL, None
    for l in reversed(range(NL)):
        below = (saved[l - 1]["out"], ps[l - 1]["post_w"]) if l else None
        dx, grads[l], reduced, post = layer_bwd(dx, ps[l], saved[l], D, (chip1, core), post, below)
        for n in BIG:
            r = reduced[n]
            done[n] = adamw_layer(r.reshape(-1, r.shape[-1]), *flat[n], l, done[n], "adamw_" + n)

    small_shapes = [(NL,) + grads[0][n].shape for n in SMALL]
    packed = _pack([0.5 / D * sumsq[0, :1]] + [jnp.stack([g[n] for g in grads]) for n in SMALL])
    total = allreduce_small(packed)
    loss, *small = _unpack(total, [(1,)] + small_shapes)
    small = dict(zip(SMALL, small))
    small["conv_w"] = lax.dynamic_slice_in_dim(small["conv_w"], chip * CS, CS, axis=2)

    given_small = dict(
        pre_norm_w=(pre_norm_w, m_pre_norm_w, v_pre_norm_w), pool_scale=(pool_scale, m_pool_scale, v_pool_scale),
        conv_w=(conv_w, m_conv_w, v_conv_w), conv_b=(conv_b, m_conv_b, v_conv_b),
        dt_bias=(dt_bias, m_dt_bias, v_dt_bias), a_log=(a_log, m_a_log, v_a_log),
        d_skip=(d_skip, m_d_skip, v_d_skip), ssd_norm_w=(ssd_norm_w, m_ssd_norm_w, v_ssd_norm_w),
        post_norm_w=(post_norm_w, m_post_norm_w, v_post_norm_w))
    shapes = [given_small[n][0].shape for n in SMALL]
    upd = adamw(_pack([small[n] for n in SMALL]), *[_pack([given_small[n][i] for n in SMALL]) for i in range(3)],
                "adamw_small")
    upd = [dict(zip(SMALL, _unpack(u, shapes))) for u in upd]

    out = {n: (small[n], upd[0][n], upd[1][n], upd[2][n]) for n in SMALL}
    for n in BIG:
        out[n] = tuple(t.reshape(given[n][0].shape) for t in done[n])
    out["w_in"] = tuple(tr(t) for t in out["w_in"])

    order = ("pre_norm_w", "w_in", "pool_mix_w", "pool_scale", "conv_w", "conv_b", "dt_bias", "a_log", "d_skip",
             "ssd_norm_w", "w_out", "post_norm_w")
    return (loss.reshape(()), dx[None], *[out[n][0] for n in order], *[out[n][1] for n in order],
            *[out[n][2] for n in order], *[out[n][3] for n in order])
```

```python
import functools
import math

import jax
import jax.numpy as jnp
from jax import lax
from jax.experimental import pallas as pl
from jax.experimental.pallas import tpu as pltpu

F32 = jnp.float32
BF16 = jnp.bfloat16

NORM_EPS = 1e-6
HEAD_DIM = 64
STATE = 128
GROUPS = 4
POOL_WINDOWS = (2, 4, 8, 16)
POOL_HALO = 16
CONV_K = 4
CONV_HALO = 8
SCAN_CHUNK = 256
LANES = 128
VMEM_LIMIT = 52 * 1024 * 1024
ROW_TILE = 256

ADAM_LR = 0.001
ADAM_B1 = 0.9
ADAM_B2 = 0.999
ADAM_EPS = 1e-08
ADAM_WD = 0.01
ADAM_STEP = 10

MESH = pl.DeviceIdType.MESH

NN = (((1,), (0,)), ((), ()))
NT = (((1,), (1,)), ((), ()))
TN = (((0,), (0,)), ((), ()))

_ANY = pl.BlockSpec(memory_space=pl.ANY)


def _params(*sem):
    return pltpu.CompilerParams(dimension_semantics=sem, vmem_limit_bytes=VMEM_LIMIT)


def _pick(dim, pref):
    if dim <= pref:
        return dim
    t = (pref // LANES) * LANES
    while t > LANES and dim % t:
        t -= LANES
    assert dim % t == 0, (dim, pref)
    return t


def _rows_tile(rows, pref):
    t = (min(pref, rows) // 8) * 8
    while t >= 8 and rows % t:
        t -= 8
    return t if t >= 8 else rows


def _dot(a, b, dn=NN):
    return lax.dot_general(a, b, dn, preferred_element_type=F32)


def _split3(a):
    hi = a.astype(BF16)
    r = a - hi.astype(F32)
    mid = r.astype(BF16)
    return hi, mid, (r - mid.astype(F32)).astype(BF16)


def _dot_sel(a, e, parts=3):
    hi, mid, lo = _split3(a)
    return (_dot(lo, e) + _dot(mid, e)) + _dot(hi, e) if parts == 3 else _dot(mid, e) + _dot(hi, e)


def _sel_dot(e, b):
    hi, mid, lo = _split3(b)
    return (_dot(e, lo) + _dot(e, mid)) + _dot(e, hi)


def _sigmoid(v):
    return 0.5 * jnp.tanh(0.5 * v) + 0.5


def _silu_and_grad(v):
    s = _sigmoid(v)
    return v * s, s * (1.0 + v * (1.0 - s))


def _row(i, shape):
    return lax.broadcasted_iota(jnp.int32, shape, 0) + i


def _sds(shape, dtype):
    return jax.ShapeDtypeStruct(tuple(shape), dtype)


class Job:
    def __init__(self, ins, outs, aliased, nsem, start, finish):
        self.ins, self.outs, self.aliased, self.nsem, self.start, self.finish = ins, outs, aliased, nsem, start, finish


def _place():
    x, y, c = lax.axis_index("x"), lax.axis_index("y"), lax.axis_index("c")
    return x, y, c, [(1 - x, y), (x, 1 - y), (1 - x, 1 - y)]


def _remote(src, dst, send_sem, recv_sem, device):
    return pltpu.make_async_remote_copy(src_ref=src, dst_ref=dst, send_sem=send_sem, recv_sem=recv_sem,
                                        device_id=device, device_id_type=MESH)


def _half(ref, c, cols, lead=0):
    idx = [slice(None)] * lead
    if cols:
        w = ref.shape[-1] // 2
        idx += [slice(None)] * (len(ref.shape) - lead - 1) + [pl.ds(pl.multiple_of(c * w, LANES), w)]
    else:
        idx += [c]
    return ref.at[tuple(idx)]


def _flags(cols, n):
    return list(cols) if cols else [False] * n


def gather_send_job(arrs, cols=None):
    n = len(arrs)
    cols = _flags(cols, n)

    def copies(ins, outs, send, recv):
        x, y, c, chips = _place()
        mine = 2 * x + y
        out = []
        for a in range(n):
            out.append(_remote(ins[a], outs[a].at[mine], send.at[4 * a + 3], recv.at[4 * a + 3], (x, y, 1 - c)))
            for j, chip in enumerate(chips):
                out.append(_remote(_half(ins[a], c, cols[a]), _half(outs[a].at[mine], c, cols[a]),
                                   send.at[4 * a + j], recv.at[4 * a + j], (*chip, c)))
        return out

    def start(ins, outs, send, recv):
        for cp in copies(ins, outs, send, recv):
            cp.start()

    def finish(ins, outs, send, recv):
        x, y, c, chips = _place()
        for a in range(n):
            for j, chip in enumerate(chips):
                landed = _half(outs[a].at[2 * chip[0] + chip[1]], c, cols[a])
                _remote(landed, landed, send.at[4 * a + j], recv.at[4 * a + j], (x, y, 1 - c)).wait_recv()
            twin = outs[a].at[2 * x + y]
            _remote(twin, twin, send.at[4 * a + 3], recv.at[4 * a + 3], (x, y, 1 - c)).wait_recv()
        for cp in copies(ins, outs, send, recv):
            cp.wait_send()

    return Job(list(arrs), [_sds((4,) + a.shape, a.dtype) for a in arrs], False, 4 * n, start, finish)


def gather_rows_job(shard, start, size, into=None):
    rows = pl.ds(start, size)

    def copies(ins, outs, send, recv):
        x, y, c, chips = _place()
        mine = outs[0].at[2 * x + y, rows]
        out = [_remote(ins[0].at[rows], mine, send.at[3], recv.at[3], (x, y, 1 - c))]
        for j, chip in enumerate(chips):
            out.append(_remote(_half(ins[0].at[rows], c, True), _half(mine, c, True), send.at[j], recv.at[j],
                               (*chip, c)))
        return out

    def start_(ins, outs, send, recv):
        for cp in copies(ins, outs, send, recv):
            cp.start()

    def finish(ins, outs, send, recv):
        x, y, c, chips = _place()
        for j, chip in enumerate(chips):
            landed = _half(outs[0].at[2 * chip[0] + chip[1], rows], c, True)
            _remote(landed, landed, send.at[j], recv.at[j], (x, y, 1 - c)).wait_recv()
        twin = outs[0].at[2 * x + y, rows]
        _remote(twin, twin, send.at[3], recv.at[3], (x, y, 1 - c)).wait_recv()
        for cp in copies(ins, outs, send, recv):
            cp.wait_send()

    ins = [shard] if into is None else [shard, into]
    return Job(ins, [_sds((4,) + shard.shape, shard.dtype)], {} if into is None else {1: 0}, 4, start_, finish)


def gather_pass_job(bufs, cols=None):
    n = len(bufs)
    cols = _flags(cols, n)

    def copies(outs, send, recv):
        x, y, c, chips = _place()
        out = []
        for a in range(n):
            for j, chip in enumerate(chips):
                landed = _half(outs[a].at[2 * chip[0] + chip[1]], c, cols[a])
                out.append(_remote(landed, landed, send.at[3 * a + j], recv.at[3 * a + j], (x, y, 1 - c)))
        return out

    def start(ins, outs, send, recv):
        for cp in copies(outs, send, recv):
            cp.start()

    def finish(ins, outs, send, recv):
        x, y, c, chips = _place()
        for a in range(n):
            for j, chip in enumerate(chips):
                passed = _half(outs[a].at[2 * chip[0] + chip[1]], 1 - c, cols[a])
                _remote(passed, passed, send.at[3 * a + j], recv.at[3 * a + j], (x, y, 1 - c)).wait_recv()
        for cp in copies(outs, send, recv):
            cp.wait_send()

    return Job(list(bufs), [_sds(b.shape, b.dtype) for b in bufs], True, 3 * n, start, finish)


def pair_exchange_job(arrs, cols=None):
    n = len(arrs)
    cols = _flags(cols, n)

    def copies(ins, outs, send, recv):
        x, y, c, _ = _place()
        return [_remote(_half(ins[a], 1 - c, cols[a], 1), outs[a], send.at[a], recv.at[a], (x, y, 1 - c))
                for a in range(n)]

    def start(ins, outs, send, recv):
        for cp in copies(ins, outs, send, recv):
            cp.start()

    def finish(ins, outs, send, recv):
        for cp in copies(ins, outs, send, recv):
            cp.wait()

    shape = lambda a, k: a.shape[:-1] + (a.shape[-1] // 2,) if k else a.shape[:1] + a.shape[2:]
    return Job(list(arrs), [_sds(shape(a, k), a.dtype) for a, k in zip(arrs, cols)], False, n, start, finish)


def chip_exchange_job(arrs):
    n = len(arrs)

    def copies(ins, outs, send, recv):
        x, y, c, chips = _place()
        return [_remote(ins[a].at[2 * chip[0] + chip[1]], outs[a].at[j], send.at[3 * a + j], recv.at[3 * a + j],
                        (*chip, c)) for a in range(n) for j, chip in enumerate(chips)]

    def start(ins, outs, send, recv):
        for cp in copies(ins, outs, send, recv):
            cp.start()

    def finish(ins, outs, send, recv):
        for cp in copies(ins, outs, send, recv):
            cp.wait()

    return Job(list(arrs), [_sds((3,) + a.shape[1:], a.dtype) for a in arrs], False, 3 * n, start, finish)


def pair_gather_job(bufs, cols=None):
    n = len(bufs)
    cols = _flags(cols, n)

    def copies(outs, send, recv):
        x, y, c, _ = _place()
        return [_remote(_half(outs[a], c, cols[a]), _half(outs[a], c, cols[a]), send.at[a], recv.at[a],
                        (x, y, 1 - c)) for a in range(n)]

    def start(ins, outs, send, recv):
        for cp in copies(outs, send, recv):
            cp.start()

    def finish(ins, outs, send, recv):
        for cp in copies(outs, send, recv):
            cp.wait()

    return Job(list(bufs), [_sds(b.shape, b.dtype) for b in bufs], True, n, start, finish)


def _call(body, *, grid, in_specs, out_specs, out_shape, name, sem, args, scratch_shapes=(), jobs=(), aliases=None):
    in_specs, out_specs, out_shape, scratch_shapes = list(in_specs), list(out_specs), list(out_shape), list(scratch_shapes)
    aliases = dict(aliases or {})
    n_in, n_out, n_scr = len(in_specs), len(out_specs), len(scratch_shapes)
    if jobs:
        sem = ("arbitrary",) * len(grid)
    at_in, at_out = n_in, n_out
    for j in jobs:
        pairs = j.aliased if isinstance(j.aliased, dict) else {i: i for i in range(len(j.ins))} if j.aliased else {}
        aliases.update({at_in + i: at_out + o for i, o in pairs.items()})
        at_in, at_out = at_in + len(j.ins), at_out + len(j.outs)

    def wrapped(*refs):
        ins, p = refs[:n_in], n_in
        jins = []
        for j in jobs:
            jins.append(refs[p:p + len(j.ins)])
            p += len(j.ins)
        outs, p = refs[p:p + n_out], p + n_out
        jouts = []
        for j in jobs:
            jouts.append(refs[p:p + len(j.outs)])
            p += len(j.outs)
        scr, sems = refs[p:p + n_scr], refs[p + n_scr:]

        def start():
            for k, j in enumerate(jobs):
                j.start(jins[k], jouts[k], sems[2 * k], sems[2 * k + 1])

        def finish():
            for k, j in enumerate(jobs):
                j.finish(jins[k], jouts[k], sems[2 * k], sems[2 * k + 1])

        if jobs and grid:
            ids = [pl.program_id(d) for d in range(len(grid))]
            pl.when(functools.reduce(jnp.logical_and, [i == 0 for i in ids]))(start)
            body(*ins, *outs, *scr)
            pl.when(functools.reduce(jnp.logical_and, [i == g - 1 for i, g in zip(ids, grid)]))(finish)
        else:
            start()
            body(*ins, *outs, *scr)
            finish()

    kwargs = dict(grid=grid) if grid else {}
    res = pl.pallas_call(
        wrapped, in_specs=in_specs + [_ANY] * (at_in - n_in), out_specs=out_specs + [_ANY] * (at_out - n_out),
        out_shape=out_shape + [o for j in jobs for o in j.outs],
        scratch_shapes=scratch_shapes + [pltpu.SemaphoreType.DMA((j.nsem,)) for j in jobs for _ in range(2)],
        input_output_aliases=aliases, name=name,
        compiler_params=pltpu.CompilerParams(dimension_semantics=sem, vmem_limit_bytes=VMEM_LIMIT) if grid
        else pltpu.CompilerParams(vmem_limit_bytes=VMEM_LIMIT), **kwargs)(*args, *[a for j in jobs for a in j.ins])
    res = list(res)
    outs, rest, per_job = res[:n_out], res[n_out:], []
    for j in jobs:
        per_job.append(rest[:len(j.outs)])
        rest = rest[len(j.outs):]
    return outs, per_job


def run_jobs(jobs, name):
    return _call(lambda: None, grid=(), in_specs=[], out_specs=[], out_shape=[], name=name, sem=(), args=(), jobs=jobs)[1]


def pair_add(g, r, core, out_dtype, cols=False, out_rows=None):
    S, R, C = r.shape
    T = _rows_tile(R, 2 * ROW_TILE if C <= 1024 else ROW_TILE if C <= 4096 else ROW_TILE // 2)

    def body(c_ref, g_ref, r_ref, o_ref):
        o_ref[0] = ((g_ref[0] if cols else g_ref[0, 0]).astype(F32) + r_ref[0].astype(F32)).astype(o_ref.dtype)

    g_spec = (pl.BlockSpec((1, T, C), lambda k, i, c_ref: (k, i, c_ref[0])) if cols
              else pl.BlockSpec((1, 1, T, C), lambda k, i, c_ref: (k, c_ref[0], i, 0)))
    return pl.pallas_call(
        body,
        grid_spec=pltpu.PrefetchScalarGridSpec(
            num_scalar_prefetch=1, grid=(S, R // T),
            in_specs=[g_spec, pl.BlockSpec((1, T, C), lambda k, i, c_ref: (k, i, 0))],
            out_specs=pl.BlockSpec((1, T, C), lambda k, i, c_ref: (k, i, 0))),
        out_shape=_sds((S, out_rows or R, C), out_dtype), name="pair_add",
        compiler_params=_params("parallel", "parallel"))(core, g, r)


def chip_add(p, r, chip, core, cols=False):
    _, R, C = p.shape
    T = _rows_tile(R, ROW_TILE)

    def body(k_ref, c_ref, p_ref, r0_ref, r1_ref, r2_ref, o_ref):
        s = ((p_ref[0].astype(F32) + r0_ref[0].astype(F32)) + r1_ref[0].astype(F32)) + r2_ref[0].astype(F32)
        if cols:
            o_ref[...] = s
        else:
            o_ref[0] = s

    slot = lambda j: pl.BlockSpec((1, T, C), lambda i, k_ref, c_ref: (j, i, 0))
    out_spec = (pl.BlockSpec((T, C), lambda i, k_ref, c_ref: (i, c_ref[0])) if cols
                else pl.BlockSpec((1, T, C), lambda i, k_ref, c_ref: (c_ref[0], i, 0)))
    return pl.pallas_call(
        body,
        grid_spec=pltpu.PrefetchScalarGridSpec(
            num_scalar_prefetch=2, grid=(R // T,),
            in_specs=[pl.BlockSpec((1, T, C), lambda i, k_ref, c_ref: (k_ref[0], i, 0)), slot(0), slot(1), slot(2)],
            out_specs=out_spec),
        out_shape=_sds((R, 2 * C) if cols else (2, R, C), F32), name="chip_add",
        compiler_params=_params("parallel"))(chip, core, p, r, r, r)


def allreduce_small(v):
    R = v.shape[0]

    def body(v_ref, o_ref, buf, send_sems, recv_sems, local_sem):
        x, y, c, chips = _place()
        me, sibling = (x, y, c), (x, y, 1 - c)

        def rows(px, py, pc):
            return buf.at[pl.ds((4 * px + 2 * py + pc) * R, R), :]

        def copy(k, block, to, src=None):
            return _remote(rows(*block) if src is None else src, rows(*block), send_sems.at[k], recv_sems.at[k], to)

        mine = pltpu.make_async_copy(v_ref, rows(*me), local_sem)
        mine.start()
        first = [copy(0, me, sibling, src=v_ref)]
        first += [copy(1 + j, me, (*chip, c), src=v_ref) for j, chip in enumerate(chips)]
        for cp in first:
            cp.start()
        passed = [copy(4 + j, (*chip, c), sibling) for j, chip in enumerate(chips)]
        for j, chip in enumerate(chips):
            copy(1 + j, (*chip, c), me).wait_recv()
            passed[j].start()
        copy(0, sibling, me).wait_recv()
        for j, chip in enumerate(chips):
            copy(4 + j, (*chip, 1 - c), me).wait_recv()
        for cp in first + passed:
            cp.wait_send()
        mine.wait()
        acc = buf[0:R, :]
        for d in range(1, 8):
            acc = acc + buf[d * R:(d + 1) * R, :]
        o_ref[...] = acc

    return pl.pallas_call(
        body, in_specs=[pl.BlockSpec(memory_space=pltpu.VMEM)], out_specs=pl.BlockSpec(memory_space=pltpu.VMEM),
        out_shape=_sds((R, LANES), F32),
        scratch_shapes=[pltpu.VMEM((8 * R, LANES), F32), pltpu.SemaphoreType.DMA((7,)),
                        pltpu.SemaphoreType.DMA((7,)), pltpu.SemaphoreType.DMA],
        name="allreduce_small", compiler_params=pltpu.CompilerParams(vmem_limit_bytes=VMEM_LIMIT))(v)


def matmul(a, b, mode, out_dtype, name, tm=512, tn=1024, tk=4608, jobs=(), n_out=None):
    if mode == "nn":
        (M, K), (K2, N) = a.shape, b.shape
    elif mode == "nt":
        (M, K), (N, K2) = a.shape, b.shape
        N = n_out or N
    else:
        (K, M), (K2, N) = a.shape, b.shape
    assert K == K2 or (mode == "nn" and K2 > K)
    tm, tn, tk = _pick(M, tm), _pick(N, tn), _pick(K, tk)
    nk = K // tk
    dn = {"nn": NN, "nt": NT, "tn": TN}[mode]

    def body(a_ref, b_ref, o_ref, *acc):
        part = _dot(a_ref[...].astype(BF16), b_ref[...].astype(BF16), dn)
        if nk == 1:
            o_ref[...] = part.astype(o_ref.dtype)
            return
        acc_ref, = acc
        k = pl.program_id(2)

        @pl.when(k == 0)
        def _():
            acc_ref[...] = part

        @pl.when(jnp.logical_and(k > 0, k < nk - 1))
        def _():
            acc_ref[...] += part

        @pl.when(k == nk - 1)
        def _():
            o_ref[...] = (acc_ref[...] + part).astype(o_ref.dtype)

    a_spec = (pl.BlockSpec((tk, tm), lambda i, j, k: (k, i)) if mode == "tn"
              else pl.BlockSpec((tm, tk), lambda i, j, k: (i, k)))
    b_spec = (pl.BlockSpec((tn, tk), lambda i, j, k: (j, k)) if mode == "nt"
              else pl.BlockSpec((tk, tn), lambda i, j, k: (k, j)))
    outs, per_job = _call(
        body, grid=(M // tm, N // tn, nk), in_specs=[a_spec, b_spec],
        out_specs=[pl.BlockSpec((tm, tn), lambda i, j, k: (i, j))], out_shape=[_sds((M, N), out_dtype)],
        scratch_shapes=[pltpu.VMEM((tm, tn), F32)] if nk > 1 else [], name=name,
        sem=("parallel", "parallel", "arbitrary"), args=(a, b), jobs=jobs)
    return outs[0], per_job


def rms_fwd(x, w):
    L, D = x.shape
    T = _pick(L, ROW_TILE)

    def body(x_ref, w_ref, h_ref):
        xv = x_ref[...]
        r = lax.rsqrt(jnp.mean(xv * xv, axis=-1, keepdims=True) + NORM_EPS)
        h_ref[...] = (xv * r * w_ref[...]).astype(h_ref.dtype)

    return pl.pallas_call(
        body, grid=(L // T,),
        in_specs=[pl.BlockSpec((T, D), lambda i: (i, 0)), pl.BlockSpec((1, D), lambda i: (0, 0))],
        out_specs=pl.BlockSpec((T, D), lambda i: (i, 0)),
        out_shape=_sds((L, D), BF16), name="rms_fwd", compiler_params=_params("parallel"))(x, w)


def post_fwd(x, o, w, next_w=None, jobs=()):
    L, D = x.shape
    T = _pick(L, ROW_TILE)

    def body(x_ref, o_ref, w_ref, *rest):
        ov = o_ref[...]
        r = lax.rsqrt(jnp.mean(ov * ov, axis=-1, keepdims=True) + NORM_EPS)
        y = x_ref[...] + ov * r * w_ref[...]
        rest[-1 if next_w is None else -2][...] = y
        if next_w is not None:
            r2 = lax.rsqrt(jnp.mean(y * y, axis=-1, keepdims=True) + NORM_EPS)
            rest[-1][...] = (y * r2 * rest[0][...]).astype(BF16)

    row = pl.BlockSpec((T, D), lambda i: (i, 0))
    vec = pl.BlockSpec((1, D), lambda i: (0, 0))
    more = [] if next_w is None else [next_w]
    outs, per_job = _call(
        body, grid=(L // T,), in_specs=[row, row, vec] + [vec] * len(more), out_specs=[row] * (1 + len(more)),
        out_shape=[_sds((L, D), F32)] + [_sds((L, D), BF16)] * len(more), name="post_fwd", sem=("parallel",),
        args=(x, o, w, *more), jobs=jobs)
    return (outs[0], outs[1] if more else None), per_job


def _rms_bwd_math(xv, w, dy):
    r = lax.rsqrt(jnp.mean(xv * xv, axis=-1, keepdims=True) + NORM_EPS)
    xhat = xv * r
    g = dy * w
    dx = r * (g - xhat * jnp.mean(g * xhat, axis=-1, keepdims=True))
    return dx, jnp.sum(dy * xhat, axis=0, keepdims=True)


def post_bwd(o, w, dxn):
    L, D = o.shape
    T = _pick(L, ROW_TILE)

    def body(o_ref, w_ref, d_ref, do_ref, dw_ref):
        dx, dw = _rms_bwd_math(o_ref[...], w_ref[...], d_ref[...])
        do_ref[...] = dx.astype(do_ref.dtype)

        @pl.when(pl.program_id(0) == 0)
        def _():
            dw_ref[...] = jnp.zeros_like(dw_ref)

        dw_ref[...] += dw

    row = pl.BlockSpec((T, D), lambda i: (i, 0))
    vec = pl.BlockSpec((1, D), lambda i: (0, 0))
    return pl.pallas_call(
        body, grid=(L // T,), in_specs=[row, vec, row], out_specs=[row, vec],
        out_shape=[_sds((L, D), BF16), _sds((1, D), F32)],
        name="post_bwd", compiler_params=_params("arbitrary"))(o, w, dxn)


def rms_bwd(x, w, dh_a, dh_b, dxn, below=None, jobs=()):
    L, D = x.shape
    T = _pick(L, ROW_TILE)

    def body(x_ref, w_ref, a_ref, b_ref, d_ref, *rest):
        dx, dw = _rms_bwd_math(x_ref[...], w_ref[...], a_ref[...] + b_ref[...])
        dx = d_ref[...] + dx
        outs = rest[2:] if below else rest
        outs[0][...] = dx

        @pl.when(pl.program_id(0) == 0)
        def _():
            for acc in outs[1::2]:
                acc[...] = jnp.zeros_like(acc)

        outs[1][...] += dw
        if below:
            do, dwp = _rms_bwd_math(rest[0][...], rest[1][...], dx)
            outs[2][...] = do.astype(BF16)
            outs[3][...] += dwp

    row = pl.BlockSpec((T, D), lambda i: (i, 0))
    vec = pl.BlockSpec((1, D), lambda i: (0, 0))
    more = list(below) if below else []
    return _call(
        body, grid=(L // T,), in_specs=[row, vec, row, row, row] + ([row, vec] if below else []),
        out_specs=[row, vec] + ([row, vec] if below else []),
        out_shape=[_sds((L, D), F32), _sds((1, D), F32)] + ([_sds((L, D), BF16), _sds((1, D), F32)] if below else []),
        name="rms_bwd", sem=("arbitrary",), args=(x, w, dh_a, dh_b, dxn, *more), jobs=jobs)


def loss_head(y, target):
    L, D = y.shape
    T = _pick(L, ROW_TILE)

    def body(y_ref, t_ref, d_ref, s_ref):
        e = y_ref[...] - t_ref[...]
        d_ref[...] = e * (1.0 / D)

        @pl.when(pl.program_id(0) == 0)
        def _():
            s_ref[...] = jnp.zeros_like(s_ref)

        s_ref[...] += jnp.sum(e * e)

    row = pl.BlockSpec((T, D), lambda i: (i, 0))
    return pl.pallas_call(
        body, grid=(L // T,), in_specs=[row, row],
        out_specs=[row, pl.BlockSpec((8, LANES), lambda i: (0, 0))],
        out_shape=[_sds((L, D), F32), _sds((8, LANES), F32)],
        name="loss_head", compiler_params=_params("arbitrary"))(y, target)


def _window_sums(xe, w, back):
    n = xe.shape[0]
    s, k = xe, 1
    while k < w:
        s = s + pltpu.roll(s, k if back else n - k, 0)
        k *= 2
    return s


def pool_fwd(proj, mixw, scale, D):
    L = proj.shape[0]
    PGW = D // GROUPS
    T = _pick(L, ROW_TILE)
    hb = T // POOL_HALO

    def body(u_ref, halo_ref, g_ref, mw_ref, sc_ref, y_ref, p_ref):
        i = pl.program_id(0)
        u = u_ref[...]
        halo = jnp.where(i > 0, halo_ref[...], 0.0)
        xe = jnp.concatenate([halo, u], axis=0)
        t1 = _row(i * T + 1, (T, 1))
        for g, w in enumerate(POOL_WINDOWS):
            sl = slice(g * PGW, (g + 1) * PGW)
            win = _window_sums(xe[:, sl], w, True)[POOL_HALO:, :]
            cnt = jnp.minimum(t1, w).astype(F32)
            pooled = (win / cnt - u[:, sl]).astype(BF16)
            p_ref[:, sl] = pooled
            mixed = _dot(pooled, mw_ref[g])
            gate = g_ref[:, sl]
            y_ref[:, sl] = (mixed * sc_ref[:, sl] * (gate * _sigmoid(gate))).astype(BF16)

    return pl.pallas_call(
        body, grid=(L // T,),
        in_specs=[pl.BlockSpec((T, D), lambda i: (i, 0)),
                  pl.BlockSpec((POOL_HALO, D), lambda i: (jnp.maximum(i * hb - 1, 0), 0)),
                  pl.BlockSpec((T, D), lambda i: (i, 1)),
                  pl.BlockSpec((GROUPS, PGW, PGW), lambda i: (0, 0, 0)),
                  pl.BlockSpec((1, D), lambda i: (0, 0))],
        out_specs=[pl.BlockSpec((T, D), lambda i: (i, 0)), pl.BlockSpec((T, D), lambda i: (i, 0))],
        out_shape=[_sds((L, 2 * D), BF16), _sds((L, D), BF16)],
        name="pool_fwd", compiler_params=_params("parallel"))(proj, proj, proj, mixw, scale)


def pool_bwd(proj, dmix, pooled, mixw, scale, dproj, D):
    L = proj.shape[0]
    PGW = D // GROUPS
    T = _pick(L, ROW_TILE)
    hb = T // POOL_HALO
    nT = L // T

    def body(g_ref, gh_ref, dy_ref, dyh_ref, p_ref, mw_ref, sc_ref, old_ref, dp_ref, dm_ref, ds_ref):
        i = pl.program_id(0)
        t1 = _row(i * T + 1, (T, 1))
        th1 = _row((i + 1) * T + 1, (POOL_HALO, 1))
        live = i < nT - 1

        @pl.when(i == 0)
        def _():
            ds_ref[...] = jnp.zeros_like(ds_ref)
            dm_ref[...] = jnp.zeros_like(dm_ref)

        for g, w in enumerate(POOL_WINDOWS):
            sl = slice(g * PGW, (g + 1) * PGW)
            sc = sc_ref[:, sl]
            gate, dy = g_ref[:, sl], dy_ref[:, sl]
            sg, dsg = _silu_and_grad(gate)
            pooled = p_ref[:, sl]
            mixed = _dot(pooled, mw_ref[g])
            dmixed = (dy * sc * sg).astype(BF16)
            dm_ref[g] += _dot(pooled, dmixed, TN)
            dp_ref[:, D + g * PGW:D + (g + 1) * PGW] = (dy * mixed * sc * dsg).astype(BF16)
            ds_ref[:, sl] += jnp.sum(dy * mixed * sg, axis=0, keepdims=True)
            dpool = _dot(dmixed, mw_ref[g], NT)
            gate_h = gh_ref[:, sl]
            dmixed_h = (dyh_ref[:, sl] * sc * (gate_h * _sigmoid(gate_h))).astype(BF16)
            dpool_h = jnp.where(live, _dot(dmixed_h, mw_ref[g], NT), 0.0)
            q = dpool / jnp.minimum(t1, w).astype(F32)
            q_h = dpool_h / jnp.minimum(th1, w).astype(F32)
            qe = jnp.concatenate([q, q_h], axis=0)
            dp_ref[:, sl] = (_window_sums(qe, w, False)[:T, :] - dpool).astype(BF16)

    nxt = lambda i: jnp.minimum((i + 1) * hb, L // POOL_HALO - 1)
    row = lambda c: pl.BlockSpec((T, D), lambda i: (i, c))
    return pl.pallas_call(
        body, grid=(nT,),
        in_specs=[row(1), pl.BlockSpec((POOL_HALO, D), lambda i: (nxt(i), 1)),
                  row(0), pl.BlockSpec((POOL_HALO, D), lambda i: (nxt(i), 0)),
                  row(0), pl.BlockSpec((GROUPS, PGW, PGW), lambda i: (0, 0, 0)),
                  pl.BlockSpec((1, D), lambda i: (0, 0)), _ANY],
        out_specs=[pl.BlockSpec((T, 2 * D), lambda i: (i, 0)), pl.BlockSpec((GROUPS, PGW, PGW), lambda i: (0, 0, 0)),
                   pl.BlockSpec((1, D), lambda i: (0, 0))],
        out_shape=[_sds(dproj.shape, dproj.dtype), _sds((GROUPS, PGW, PGW), F32), _sds((1, D), F32)],
        input_output_aliases={7: 0},
        name="pool_bwd", compiler_params=_params("arbitrary"))(proj, proj, dmix, dmix, pooled, mixw, scale, dproj)


def conv_fwd(proj, cw, cb, D):
    L = proj.shape[0]
    C = cw.shape[1]
    assert (3 * D) % C == 0
    cblk = (3 * D) // C
    T = _pick(L, ROW_TILE)
    hb = T // CONV_HALO

    def body(u_ref, halo_ref, w_ref, b_ref, o_ref):
        i = pl.program_id(0)
        u = u_ref[...]
        xe = jnp.concatenate([jnp.where(i > 0, halo_ref[...], 0.0), u], axis=0)
        acc = b_ref[...] + w_ref[CONV_K - 1:CONV_K, :] * u
        for k in range(CONV_K - 1):
            acc = acc + w_ref[k:k + 1, :] * pltpu.roll(xe, CONV_K - 1 - k, 0)[CONV_HALO:, :]
        o_ref[...] = acc

    return pl.pallas_call(
        body, grid=(L // T,),
        in_specs=[pl.BlockSpec((T, C), lambda i: (i, cblk)),
                  pl.BlockSpec((CONV_HALO, C), lambda i: (jnp.maximum(i * hb - 1, 0), cblk)),
                  pl.BlockSpec((CONV_K, C), lambda i: (0, 0)),
                  pl.BlockSpec((1, C), lambda i: (0, 0))],
        out_specs=pl.BlockSpec((T, C), lambda i: (i, 0)),
        out_shape=_sds((L, C), F32), name="conv_fwd", compiler_params=_params("parallel"))(proj, proj, cw, cb)


def conv_bwd(dparts, proj, cw, dproj, D, jobs=()):
    L = proj.shape[0]
    C = cw.shape[1]
    cblk = (3 * D) // C
    T = _pick(L, ROW_TILE)
    hb = T // CONV_HALO
    nT = L // T
    widths = [p.shape[1] for p in dparts]
    assert sum(widths) == C
    n = len(dparts)

    def body(*refs):
        d_refs, dn_refs = refs[:n], refs[n:2 * n]
        u_ref, up_ref, w_ref, old_ref, dr_ref, dw_ref, db_ref = refs[2 * n:]
        i = pl.program_id(0)

        @pl.when(i == 0)
        def _():
            dw_ref[...] = jnp.zeros_like(dw_ref)
            db_ref[...] = jnp.zeros_like(db_ref)

        at = 0
        for d_ref, dn_ref, wd in zip(d_refs, dn_refs, widths):
            sl = slice(at, at + wd)
            at += wd
            d = d_ref[...]
            u = u_ref[:, sl]
            de = jnp.concatenate([d, jnp.where(i < nT - 1, dn_ref[...], 0.0)], axis=0)
            ue = jnp.concatenate([jnp.where(i > 0, up_ref[:, sl], 0.0), u], axis=0)
            acc = w_ref[CONV_K - 1:CONV_K, sl] * d
            dw_ref[CONV_K - 1:CONV_K, sl] += jnp.sum(d * u, axis=0, keepdims=True)
            for k in range(CONV_K - 1):
                sh = CONV_K - 1 - k
                acc = acc + w_ref[k:k + 1, sl] * pltpu.roll(de, T + CONV_HALO - sh, 0)[:T, :]
                dw_ref[k:k + 1, sl] += jnp.sum(d * pltpu.roll(ue, sh, 0)[CONV_HALO:, :], axis=0, keepdims=True)
            dr_ref[:, sl] = acc.astype(dr_ref.dtype)
            db_ref[:, sl] += jnp.sum(d, axis=0, keepdims=True)

    nxt = lambda i: jnp.minimum((i + 1) * hb, L // CONV_HALO - 1)
    return _call(
        body, grid=(nT,),
        in_specs=[pl.BlockSpec((T, wd), lambda i: (i, 0)) for wd in widths]
        + [pl.BlockSpec((CONV_HALO, wd), lambda i: (nxt(i), 0)) for wd in widths]
        + [pl.BlockSpec((T, C), lambda i: (i, cblk)),
           pl.BlockSpec((CONV_HALO, C), lambda i: (jnp.maximum(i * hb - 1, 0), cblk)),
           pl.BlockSpec((CONV_K, C), lambda i: (0, 0)), _ANY],
        out_specs=[pl.BlockSpec((T, C), lambda i: (i, cblk)),
                   pl.BlockSpec((8, C), lambda i: (0, 0)),
                   pl.BlockSpec((1, C), lambda i: (0, 0))],
        out_shape=[_sds(dproj.shape, dproj.dtype), _sds((8, C), F32), _sds((1, C), F32)],
        aliases={2 * n + 3: 0}, name="conv_bwd", sem=("arbitrary",),
        args=(*dparts, *dparts, proj, proj, cw, dproj), jobs=jobs)


def _softplus(v):
    y = jnp.exp(-jnp.abs(v))
    u = 1.0 + y
    log1p = jnp.where(u == 1.0, y, jnp.log(u) * y / jnp.where(u == 1.0, 1.0, u - 1.0))
    return jnp.maximum(v, 0.0) + log1p


def dt_prep(dtraw, bias, alog, expand, D, jobs=()):
    L = dtraw.shape[0]
    GC = D // GROUPS
    HPG = GC // HEAD_DIM
    Q = _pick(L, SCAN_CHUNK)
    nc = L // Q

    def body(r_ref, b_ref, a_ref, e_ref, dt_ref, acs_ref, acst_ref, dtx_ref, eax_ref, dsx_ref, cdx_ref):
        valid = lax.broadcasted_iota(jnp.int32, (1, LANES), 1) < HPG
        dt = jnp.where(valid, _softplus(r_ref[...] + b_ref[...]), 0.0)
        adt = dt * -jnp.exp(a_ref[...])
        tril = (_row(0, (Q, Q)) >= lax.broadcasted_iota(jnp.int32, (Q, Q), 1)).astype(BF16)
        acs = _sel_dot(tril, adt)
        last = acs[Q - 1:Q, :]
        dt_ref[...] = dt
        acs_ref[...] = acs
        acst_ref[...] = acs.T
        e = e_ref[...]
        dtx_ref[...] = _dot_sel(dt, e, 2)
        eax_ref[...] = jnp.exp(_dot_sel(acs, e, 2))
        dsx_ref[...] = jnp.exp(_dot_sel(last - acs, e, 2))
        cdx_ref[0] = jnp.exp(_dot_sel(jnp.broadcast_to(last, (8, LANES)), e, 2))

    head = pl.BlockSpec((Q, LANES), lambda g, c: (c, g))
    hvec = pl.BlockSpec((1, LANES), lambda g, c: (0, g))
    chan = pl.BlockSpec((Q, GC), lambda g, c: (c, g))
    return _call(
        body, grid=(GROUPS, nc),
        in_specs=[head, hvec, hvec, pl.BlockSpec((LANES, GC), lambda g, c: (0, 0))],
        out_specs=[head, head, pl.BlockSpec((LANES, Q), lambda g, c: (g, c)), chan, chan, chan,
                   pl.BlockSpec((1, 8, GC), lambda g, c: (c, 0, g))],
        out_shape=[_sds((L, GROUPS * LANES), F32), _sds((L, GROUPS * LANES), F32), _sds((GROUPS * LANES, L), F32),
                   _sds((L, D), F32), _sds((L, D), F32), _sds((L, D), F32), _sds((nc, 8, D), F32)],
        name="dt_prep", sem=("parallel", "parallel"), args=(dtraw, bias, alog, expand), jobs=jobs)


def _scan_specs(L, D, Q, rev):
    GC = D // GROUPS
    nc = L // Q
    ci = (lambda c: nc - 1 - c) if rev else (lambda c: c)
    return dict(
        xs=pl.BlockSpec((Q, GC), lambda g, c: (ci(c), g)),
        b=pl.BlockSpec((Q, STATE), lambda g, c: (ci(c), D // STATE + g)),
        c=pl.BlockSpec((Q, STATE), lambda g, c: (ci(c), D // STATE + GROUPS + g)),
        chan=pl.BlockSpec((Q, GC), lambda g, c: (ci(c), g)),
        cdx=pl.BlockSpec((1, 8, GC), lambda g, c: (ci(c), 0, g)),
        head=pl.BlockSpec((Q, LANES), lambda g, c: (ci(c), g)),
        headt=pl.BlockSpec((LANES, Q), lambda g, c: (g, ci(c))),
        state=pl.BlockSpec((1, 1, STATE, GC), lambda g, c: (ci(c), g, 0, 0)),
        hvec=pl.BlockSpec((1, LANES), lambda g, c: (0, g)),
        cvec=pl.BlockSpec((1, GC), lambda g, c: (0, g)))


def scan_fwd(pre, dtx, eax, dsx, cdx, acs, acst, proj, dexp, nw, mixed, D, jobs=()):
    L = pre.shape[0]
    GC = D // GROUPS
    Q = _pick(L, SCAN_CHUNK)
    nc = L // Q
    sp = _scan_specs(L, D, Q, False)

    def body(xs_ref, b_ref, c_ref, dtx_ref, eax_ref, dsx_ref, cdx_ref, acs_ref, acst_ref, z_ref, de_ref, nw_ref,
             old_ref, y_ref, st_ref, o_ref, s_scr):
        @pl.when(pl.program_id(1) == 0)
        def _():
            s_scr[...] = jnp.zeros_like(s_scr)

        tri = _row(0, (Q, Q)) >= lax.broadcasted_iota(jnp.int32, (Q, Q), 1)
        half = lax.broadcasted_iota(jnp.int32, (1, LANES), 1) // HEAD_DIM
        xs, _ = _silu_and_grad(xs_ref[...])
        bg = _silu_and_grad(b_ref[...])[0].astype(BF16)
        cg = _silu_and_grad(c_ref[...])[0].astype(BF16)
        xdt = xs * dtx_ref[...]
        sprev = s_scr[...]
        st_ref[0, 0] = sprev
        sc = _dot(cg, bg, NT)
        yoff = _dot(cg, sprev.astype(BF16)) * eax_ref[...]
        for j in range(GC // LANES):
            ps = slice(j * LANES, (j + 1) * LANES)
            xp = xdt[:, ps]
            acc = yoff[:, ps]
            for hh in range(2):
                h = 2 * j + hh
                lm = jnp.exp(jnp.where(tri, acs_ref[:, h:h + 1] - acst_ref[h:h + 1, :], -1e30))
                xm = jnp.where(half == hh, xp, 0.0).astype(BF16)
                acc = acc + _dot((sc * lm).astype(BF16), xm)
            y_ref[:, ps] = acc
        xw = (xdt * dsx_ref[...]).astype(BF16)
        s_scr[...] = cdx_ref[0, 0:1, :] * sprev + _dot(bg, xw, TN)
        z = z_ref[...]
        y3 = (y_ref[...] + de_ref[...] * xs) * (z * _sigmoid(z))
        r = lax.rsqrt(jnp.mean(y3 * y3, axis=-1, keepdims=True) + NORM_EPS)
        o_ref[...] = (y3 * r * nw_ref[...]).astype(o_ref.dtype)

    return _call(
        body, grid=(GROUPS, nc),
        in_specs=[sp["xs"], sp["b"], sp["c"], sp["chan"], sp["chan"], sp["chan"], sp["cdx"], sp["head"], sp["headt"],
                  pl.BlockSpec((Q, GC), lambda g, c: (c, 2 * GROUPS + g)), sp["cvec"], sp["cvec"], _ANY],
        out_specs=[sp["chan"], sp["state"], pl.BlockSpec((Q, GC), lambda g, c: (c, GROUPS + g))],
        out_shape=[_sds((L, D), F32), _sds((nc, GROUPS, STATE, GC), F32), _sds(mixed.shape, mixed.dtype)],
        aliases={12: 2}, scratch_shapes=[pltpu.VMEM((STATE, GC), F32)], name="scan_fwd",
        sem=("parallel", "arbitrary"), args=(pre, pre, pre, dtx, eax, dsx, cdx, acs, acst, proj, dexp, nw, mixed),
        jobs=jobs)


def scan_bwd(pre, dtx, eax, dsx, cdx, acs, acst, dt, dtraw, bias, alog, states, y, proj, dmix, nw, dexp, collapse, D,
             jobs=()):
    L = pre.shape[0]
    GC = D // GROUPS
    Q = _pick(L, SCAN_CHUNK)
    nc = L // Q
    sp = _scan_specs(L, D, Q, True)
    rc = lambda c: nc - 1 - c

    def body(xs_ref, b_ref, c_ref, dtx_ref, eax_ref, dsx_ref, cdx_ref, acs_ref, acst_ref, dt_ref, raw_ref,
             bias_ref, alog_ref, st_ref, y_ref, z_ref, dm_ref, nw_ref, dexp_ref, col_ref,
             dxs_ref, db_ref, dc_ref, ddt_ref, dal_ref, dbi_ref, dz_ref, dnw_ref, dde_ref, ds_scr, dx_scr):
        first = pl.program_id(1) == 0

        @pl.when(first)
        def _():
            ds_scr[...] = jnp.zeros_like(ds_scr)
            dal_ref[...] = jnp.zeros_like(dal_ref)
            dbi_ref[...] = jnp.zeros_like(dbi_ref)
            dnw_ref[...] = jnp.zeros_like(dnw_ref)
            dde_ref[...] = jnp.zeros_like(dde_ref)

        li = _row(0, (Q, Q))
        si = lax.broadcasted_iota(jnp.int32, (Q, Q), 1)
        lane = lax.broadcasted_iota(jnp.int32, (1, LANES), 1)
        half = lane // HEAD_DIM
        xs_pre, b_pre, c_pre = xs_ref[...], b_ref[...], c_ref[...]
        xs, xs_g = _silu_and_grad(xs_pre)
        bf, b_g = _silu_and_grad(b_pre)
        cf, c_g = _silu_and_grad(c_pre)
        bg, cg = bf.astype(BF16), cf.astype(BF16)
        dtx, eax, dsx = dtx_ref[...], eax_ref[...], dsx_ref[...]
        cd = cdx_ref[0, 0:1, :]
        xdt = xs * dtx
        sz, dsz = _silu_and_grad(z_ref[...])
        y2 = y_ref[...] + dexp_ref[...] * xs
        y3 = y2 * sz
        r = lax.rsqrt(jnp.mean(y3 * y3, axis=-1, keepdims=True) + NORM_EPS)
        n = y3 * r
        dm = dm_ref[...]
        gg = dm * nw_ref[...]
        dy3 = r * (gg - n * jnp.mean(gg * n, axis=-1, keepdims=True))
        dnw_ref[0:1, :] += jnp.sum(dm * n, axis=0, keepdims=True)
        G = dy3 * sz
        dz_ref[...] = (dy3 * y2 * dsz).astype(dz_ref.dtype)
        dde_ref[0:1, :] += jnp.sum(G * xs, axis=0, keepdims=True)
        prev = st_ref[0, 0]
        dsn = ds_scr[...]
        prev_b, dsn_b = prev.astype(BF16), dsn.astype(BF16)
        cp = _dot(cg, prev_b)
        ge_b = (G * eax).astype(BF16)
        d_c = _dot(ge_b, prev_b, NT)
        dprev = _dot(cg, ge_b, TN) + cd * dsn
        chan_a = G * cp * eax
        xw_b = (xdt * dsx).astype(BF16)
        dcd = jnp.sum(prev * dsn, axis=0, keepdims=True)
        d_b = _dot(xw_b, dsn_b, NT)
        dxw = _dot(bg, dsn_b)
        dd = dxw * xdt * dsx
        chan_a = chan_a - dd
        last_c = jnp.sum(dd, axis=0, keepdims=True) + dcd * cd
        sc = _dot(cg, bg, NT)
        head_row = _row(0, (LANES, 1))
        dsc = jnp.zeros((Q, Q), F32)
        dacs = jnp.zeros((Q, LANES), F32)
        colsums = jnp.zeros((LANES, Q), F32)
        for j in range(GC // LANES):
            ps = slice(j * LANES, (j + 1) * LANES)
            xp, gp = xdt[:, ps], G[:, ps]
            dxp = dxw[:, ps] * dsx[:, ps]
            for hh in range(2):
                h = 2 * j + hh
                lm = jnp.exp(jnp.where(li >= si, acs_ref[:, h:h + 1] - acst_ref[h:h + 1, :], -1e30))
                m = sc * lm
                xm = jnp.where(half == hh, xp, 0.0).astype(BF16)
                gm = jnp.where(half == hh, gp, 0.0).astype(BF16)
                dm = _dot(gm, xm, NT)
                dxp = dxp + _dot(m.astype(BF16), gm, TN)
                dsc = dsc + dm * lm
                w = dm * m
                dacs = dacs + jnp.where(lane == h, jnp.sum(w, axis=1, keepdims=True), 0.0)
                colsums = jnp.where(head_row == h, jnp.sum(w, axis=0, keepdims=True), colsums)
            dx_scr[:, ps] = dxp
        dacs = dacs - colsums.T
        dsc_b = dsc.astype(BF16)
        d_c = d_c + _dot(dsc_b, bg)
        d_b = d_b + _dot(dsc_b, cg, TN)
        ds_scr[...] = dprev
        dxdt = dx_scr[...]
        dxs_ref[...] = (dxdt * dtx + dexp_ref[...] * G) * xs_g
        db_ref[...] = d_b * b_g
        dc_ref[...] = d_c * c_g
        colm = col_ref[...]
        dacs = dacs + _dot_sel(chan_a, colm)
        dlast = _dot_sel(jnp.broadcast_to(last_c, (8, GC)), colm)[0:1, :]
        dacs = dacs + jnp.where(_row(0, (Q, 1)) == Q - 1, dlast, 0.0)
        dadt = _sel_dot((si >= li).astype(BF16), dacs)
        a = -jnp.exp(alog_ref[...])
        dt = dt_ref[...]
        ddt = dadt * a + _dot_sel(dxdt * xs, colm)
        dal_ref[0:1, :] += jnp.sum(dadt * dt * a, axis=0, keepdims=True)
        draw = ddt * _sigmoid(raw_ref[...] + bias_ref[...])
        dbi_ref[0:1, :] += jnp.sum(draw, axis=0, keepdims=True)
        ddt_ref[...] = draw.astype(ddt_ref.dtype)

    acc = pl.BlockSpec((8, LANES), lambda g, c: (0, g))
    cacc = pl.BlockSpec((8, GC), lambda g, c: (0, g))
    return _call(
        body, grid=(GROUPS, nc),
        in_specs=[sp["xs"], sp["b"], sp["c"], sp["chan"], sp["chan"], sp["chan"], sp["cdx"], sp["head"], sp["headt"],
                  sp["head"], sp["head"], sp["hvec"], sp["hvec"], sp["state"], sp["chan"],
                  pl.BlockSpec((Q, GC), lambda g, c: (rc(c), 2 * GROUPS + g)),
                  pl.BlockSpec((Q, GC), lambda g, c: (rc(c), GROUPS + g)), sp["cvec"], sp["cvec"],
                  pl.BlockSpec((GC, LANES), lambda g, c: (0, 0))],
        out_specs=[sp["chan"],
                   pl.BlockSpec((Q, STATE), lambda g, c: (rc(c), g)),
                   pl.BlockSpec((Q, STATE), lambda g, c: (rc(c), g)),
                   sp["head"], acc, acc,
                   pl.BlockSpec((Q, GC), lambda g, c: (rc(c), 2 * GROUPS + g)), cacc, cacc],
        out_shape=[_sds((L, D), F32), _sds((L, GROUPS * STATE), F32), _sds((L, GROUPS * STATE), F32),
                   _sds((L, GROUPS * LANES), BF16), _sds((8, GROUPS * LANES), F32), _sds((8, GROUPS * LANES), F32),
                   _sds((L, proj.shape[1]), BF16), _sds((8, D), F32), _sds((8, D), F32)],
        scratch_shapes=[pltpu.VMEM((STATE, GC), F32), pltpu.VMEM((Q, GC), F32)], name="scan_bwd",
        sem=("parallel", "arbitrary"),
        args=(pre, pre, pre, dtx, eax, dsx, cdx, acs, acst, dt, dtraw, bias, alog, states, y, proj, dmix, nw, dexp,
              collapse),
        jobs=jobs)


def _adam_math(gv, w, m, v):
    c1 = 1.0 - ADAM_B1 ** ADAM_STEP
    c2 = 1.0 - ADAM_B2 ** ADAM_STEP
    nm = ADAM_B1 * m + (1.0 - ADAM_B1) * gv
    nv = ADAM_B2 * v + (1.0 - ADAM_B2) * (gv * gv)
    return -ADAM_LR * ((nm / c1) / (jnp.sqrt(nv / c2) + ADAM_EPS) + ADAM_WD * w), nm, nv


def adamw(g, w, m, v, name):
    R, C = g.shape
    T = R if R <= 128 else 128
    assert R % T == 0

    def body(g_ref, w_ref, m_ref, v_ref, d_ref, nm_ref, nv_ref):
        d_ref[...], nm_ref[...], nv_ref[...] = _adam_math(g_ref[...], w_ref[...], m_ref[...], v_ref[...])

    blk = pl.BlockSpec((T, C), lambda i: (i, 0))
    return pl.pallas_call(
        body, grid=(R // T,), in_specs=[blk] * 4, out_specs=[blk] * 3,
        out_shape=[_sds((R, C), F32)] * 3, name=name, compiler_params=_params("parallel"))(g, w, m, v)


def adamw_layer(g, w, m, v, layer, prev, name):
    R, C = g.shape
    T = _rows_tile(R, ROW_TILE)
    assert R % T == 0

    def body(g_ref, w_ref, m_ref, v_ref, *rest):
        go_ref, d_ref, nm_ref, nv_ref = rest[-4:]
        gv = g_ref[...]
        go_ref[0] = gv
        d_ref[0], nm_ref[0], nv_ref[0] = _adam_math(gv, w_ref[0], m_ref[0], v_ref[0])

    mine = pl.BlockSpec((1, T, C), lambda i: (layer, i, 0))
    prev = list(prev or [])
    return pl.pallas_call(
        body, grid=(R // T,), in_specs=[pl.BlockSpec((T, C), lambda i: (i, 0)), mine, mine, mine] + [_ANY] * len(prev),
        out_specs=[mine] * 4, out_shape=[_sds(w.shape, F32)] * 4,
        input_output_aliases={4 + i: i for i in range(len(prev))}, name=name,
        compiler_params=_params("parallel"))(g, w, m, v, *prev)


def cast_bf16(a):
    R, C = a.shape
    T = _rows_tile(R, ROW_TILE)

    def body(a_ref, o_ref):
        o_ref[...] = a_ref[...].astype(BF16)

    blk = pl.BlockSpec((T, C), lambda i: (i, 0))
    return pl.pallas_call(body, grid=(R // T,), in_specs=[blk], out_specs=blk, out_shape=_sds((R, C), BF16),
                          name="cast_bf16", compiler_params=_params("parallel"))(a)


def _to_groups(v, hpg):
    lead = v.shape[:-1]
    t = v.reshape(lead + (GROUPS, hpg))
    t = jnp.pad(t, [(0, 0)] * (len(lead) + 1) + [(0, LANES - hpg)])
    return t.reshape(lead + (GROUPS * LANES,))


def _from_groups(a, hpg):
    lead = a.shape[:-1]
    return a.reshape(lead + (GROUPS, LANES))[..., :hpg].reshape(lead + (GROUPS * hpg,))


def _expand_matrix(D):
    gc = D // GROUPS
    return (jnp.arange(LANES)[:, None] == (jnp.arange(gc)[None, :] // HEAD_DIM)).astype(BF16)


def layer_params(pre_w, w_in_t, mixw, scale, cw, cb, bias, alog, dskip, nw, w_out_full, post_w, D):
    hpg = D // GROUPS // HEAD_DIM
    main = w_in_t.shape[0] - GROUPS * hpg
    wdt_t = _to_groups(w_in_t[main:].T, hpg).T
    return dict(
        pre_w=pre_w[None], win_t=w_in_t, main=main, wdt_t=wdt_t, mixw=mixw,
        scale=scale[None], cw=cw, cb=cb[None], bias=_to_groups(bias, hpg)[None], alog=_to_groups(alog, hpg)[None],
        dexp=jnp.repeat(dskip, HEAD_DIM)[None], nw=nw[None], wout=w_out_full, post_w=post_w[None])


def layer_fwd(x, p, D, next_shards=None, h=None, next_pre_w=None):
    nxt = bool(next_shards)
    parts = lambda a: ((a.shape[0] // 32) * 16, a.shape[0] - (a.shape[0] // 32) * 16)
    if nxt:
        s_in, s_out, s_mix, s_cw = next_shards
        (in_a, in_b), (out_a, out_b) = parts(s_in), parts(s_out)
    if h is None:
        h = rms_fwd(x, p["pre_w"])
    proj, got = matmul(h, p["win_t"], "nt", F32, "proj", tm=1024, n_out=p["main"],
                       jobs=[gather_rows_job(s_in, 0, in_a)] if nxt else [])
    g_in = got[0][0] if nxt else None
    dtraw, _ = matmul(h, p["wdt_t"], "nt", F32, "dtproj")
    mixed, pooled = pool_fwd(proj, p["mixw"], p["scale"], D)
    pre = conv_fwd(proj, p["cw"], p["cb"], D)
    dtp, got = dt_prep(dtraw, p["bias"], p["alog"], _expand_matrix(D), D,
                       jobs=[gather_rows_job(s_out, 0, out_a)] if nxt else [])
    g_out = got[0][0] if nxt else None
    dt, acs, acst, dtx, eax, dsx, cdx = dtp
    (y, states, mixed), got = scan_fwd(pre, dtx, eax, dsx, cdx, acs, acst, proj, p["dexp"], p["nw"], mixed, D,
                                       jobs=[gather_rows_job(s_in, in_a, in_b, g_in)] if nxt else [])
    g_in = got[0][0] if nxt else None
    out, got = matmul(mixed, p["wout"], "nn", F32, "outproj", tm=1024,
                      jobs=[gather_rows_job(s_out, out_a, out_b, g_out), gather_send_job([s_mix, s_cw]),
                            gather_pass_job([g_in], [True])] if nxt else [])
    (xn, h_next), got2 = post_fwd(x, out, p["post_w"], next_pre_w,
                                  jobs=[gather_pass_job(got[0] + got[1], [True, False, False])] if nxt else [])
    gathered = got[2] + got2[0] if nxt else None
    return xn, dict(x=x, h=h, proj=proj, dtraw=dtraw, pooled=pooled, pre=pre, dtp=dtp, y=y, states=states,
                    mixed=mixed, out=out), gathered, h_next


def layer_bwd(dxn, p, s, D, where=None, post=None, below=None):
    reduce = where is not None
    chip, core = where if reduce else (None, None)
    hpg = D // GROUPS // HEAD_DIM
    PGW = D // GROUPS
    main = p["main"]
    SH = (main + GROUPS * hpg) // 4
    dt, acs, acst, dtx, eax, dsx, cdx = s["dtp"]
    dout, d_post = post if post else post_bwd(s["out"], p["post_w"], dxn)
    part = BF16 if reduce else F32
    d_wout, _ = matmul(s["mixed"], dout, "tn", part, "dwout", tm=1024)
    g_out = d_wout.reshape(4, 2, D // 4, D)
    dmix, got = matmul(dout, p["wout"], "nt", F32, "dmixed", tm=1024,
                       jobs=[pair_exchange_job([g_out])] if reduce else [])
    pair_out = pair_add(g_out, got[0][0], core, BF16) if reduce else None
    (dxs, db, dc, ddtraw, d_alog, d_bias, dproj, d_nw, d_dexp), got = scan_bwd(
        s["pre"], dtx, eax, dsx, cdx, acs, acst, dt, s["dtraw"], p["bias"], p["alog"], s["states"], s["y"],
        s["proj"], dmix, p["nw"], p["dexp"], _expand_matrix(D).T, D,
        jobs=[chip_exchange_job([pair_out])] if reduce else [])
    mine_out = chip_add(pair_out, got[0][0], chip, core) if reduce else None
    (dproj, d_cw, d_cb), got = conv_bwd([dxs, db, dc], s["proj"], p["cw"], dproj, D,
                                        jobs=[pair_gather_job([mine_out])] if reduce else [])
    r_out = got[0][0] if reduce else None
    dproj, d_mixw, d_scale = pool_bwd(s["proj"], dmix, s["pooled"], p["mixw"], p["scale"], dproj, D)
    d_wmain_t, _ = matmul(dproj, s["h"], "tn", part, "dwmain", tm=1024)
    d_wdt_t, _ = matmul(ddtraw, s["h"], "tn", part, "dwdt")
    late = [d_wmain_t[None], d_wdt_t[None],
            d_mixw.reshape(GROUPS, 4, PGW // 4, PGW).transpose(1, 0, 2, 3).reshape(4, 2, GROUPS * PGW // 8, PGW)]
    cols = [True, True, False]
    dh_b, got = matmul(ddtraw, p["wdt_t"], "nn", F32, "dh_dt",
                       jobs=[pair_exchange_job(late, cols)] if reduce else [])
    if reduce:
        p_main, p_dt, p_mix = [pair_add(g, r, core, BF16, k, n) for g, r, k, n in
                               zip(late, got[0], cols, [4 * SH, None, None])]
        p_in = lax.dynamic_update_slice(p_main[0], _from_groups(p_dt[0].T, hpg).T, (main, 0))
        pairs = [p_in.reshape(4, SH, D // 2), p_mix]
    dh_a, got = matmul(dproj, p["win_t"], "nn", F32, "dh_main", jobs=[chip_exchange_job(pairs)] if reduce else [])
    mines = [chip_add(q, r, chip, core, k) for q, r, k in zip(pairs, got[0], [True, False])] if reduce else None
    (dx, d_pre, *post_below), got = rms_bwd(s["x"], p["pre_w"], dh_a, dh_b, dxn, below,
                                            jobs=[pair_gather_job(mines, [True, False])] if reduce else [])
    reduced = dict(w_in=got[0][0], w_out=r_out, pool_mix_w=got[0][1]) if reduce else None
    grads = dict(
        pre_norm_w=d_pre[0], pool_scale=d_scale[0], conv_w=d_cw[:CONV_K], conv_b=d_cb[0],
        dt_bias=_from_groups(d_bias[0], hpg), a_log=_from_groups(d_alog[0], hpg),
        d_skip=d_dexp[0].reshape(-1, HEAD_DIM).sum(axis=-1), ssd_norm_w=d_nw[0], post_norm_w=d_post[0])
    if not reduce:
        grads.update(w_in=jnp.concatenate([d_wmain_t.T, _from_groups(d_wdt_t.T, hpg)], axis=1), pool_mix_w=d_mixw,
                     w_out=d_wout)
    return dx, grads, reduced, tuple(post_below) or None


def local_step(x, target, params, D):
    saved, h, n = [], None, len(params)
    for l, p in enumerate(params):
        x, s, _, h = layer_fwd(x, p, D, h=h, next_pre_w=params[l + 1]["pre_w"] if l + 1 < n else None)
        saved.append(s)
    dx, sumsq = loss_head(x, target)
    grads, post = [None] * n, None
    for l in reversed(range(n)):
        below = (saved[l - 1]["out"], params[l - 1]["post_w"]) if l else None
        dx, grads[l], _, post = layer_bwd(dx, params[l], saved[l], D, post=post, below=below)
    return sumsq, dx, grads


SMALL = ("pre_norm_w", "pool_scale", "conv_w", "conv_b", "dt_bias", "a_log", "d_skip", "ssd_norm_w", "post_norm_w")
BIG = ("w_in", "w_out", "pool_mix_w")


def _pack(parts):
    flat = jnp.concatenate([p.reshape(-1) for p in parts])
    n = flat.shape[0]
    rows = -(-n // (LANES * LANES)) * LANES
    return jnp.pad(flat, (0, rows * LANES - n)).reshape(rows, LANES)


def _unpack(packed, shapes):
    flat, out, at = packed.reshape(-1), [], 0
    for s in shapes:
        n = math.prod(s)
        out.append(flat[at:at + n].reshape(s))
        at += n
    return out


def kernel(x, pre_norm_w, w_in, pool_mix_w, pool_scale, conv_w, conv_b, dt_bias, a_log, d_skip, ssd_norm_w, w_out, post_norm_w, loss_target, m_pre_norm_w, m_w_in, m_pool_mix_w, m_pool_scale, m_conv_w, m_conv_b, m_dt_bias, m_a_log, m_d_skip, m_ssd_norm_w, m_w_out, m_post_norm_w, v_pre_norm_w, v_w_in, v_pool_mix_w, v_pool_scale, v_conv_w, v_conv_b, v_dt_bias, v_a_log, v_d_skip, v_ssd_norm_w, v_w_out, v_post_norm_w):
    NL, D, SH = w_in.shape
    PGW = D // GROUPS
    CS = conv_w.shape[2]
    chip = (2 * lax.axis_index("x") + lax.axis_index("y")).astype(jnp.int32)
    chip1, core = chip.reshape(1), lax.axis_index("c").astype(jnp.int32).reshape(1)

    tr = lambda t: jnp.transpose(t, (0, 2, 1))
    w_in_t, m_w_in_t, v_w_in_t = tr(w_in), tr(m_w_in), tr(v_w_in)
    halved_by_cols = [True, True, False, False]

    def shards(l):
        return [cast_bf16(w_in_t[l]), cast_bf16(w_out[l]),
                cast_bf16(pool_mix_w[l].reshape(GROUPS * PGW // 4, PGW)).reshape(2, GROUPS * PGW // 8, PGW),
                conv_w[l].reshape(2, CONV_K * CS // (2 * LANES), LANES)]

    def params(l, g):
        g_in, g_out, g_mix, g_cw = g
        mix_l = g_mix.reshape(4, GROUPS, PGW // 4, PGW).transpose(1, 0, 2, 3).reshape(GROUPS, PGW, PGW)
        cw_l = g_cw.reshape(4, CONV_K, CS).transpose(1, 0, 2).reshape(CONV_K, 4 * CS)
        return layer_params(pre_norm_w[l], g_in.reshape(4 * SH, D), mix_l, pool_scale[l], cw_l, conv_b[l], dt_bias[l],
                            a_log[l], d_skip[l], ssd_norm_w[l], g_out.reshape(2 * D, D), post_norm_w[l], D)

    gathered = run_jobs([gather_send_job(shards(0), halved_by_cols)], "gather_send")[0]
    gathered = run_jobs([gather_pass_job(gathered, halved_by_cols)], "gather_pass")[0]
    xl, h, ps, saved = x[0], None, [], []
    for l in range(NL):
        ps.append(params(l, gathered))
        last = l + 1 == NL
        xl, s, gathered, h = layer_fwd(xl, ps[l], D, None if last else shards(l + 1), h,
                                       None if last else pre_norm_w[l + 1][None])
        saved.append(s)
    dx, sumsq = loss_head(xl, loss_target[0])

    given = dict(w_in=(w_in_t, m_w_in_t, v_w_in_t), w_out=(w_out, m_w_out, v_w_out),
                 pool_mix_w=(pool_mix_w, m_pool_mix_w, v_pool_mix_w))
    flat = {n: [t.reshape(NL, -1, t.shape[-1]) for t in given[n]] for n in BIG}
    done = {n: None for n in BIG}
    grads, post = [None] * NL, None
    for l in reversed(range(NL)):
        below = (saved[l - 1]["out"], ps[l - 1]["post_w"]) if l else None
        dx, grads[l], reduced, post = layer_bwd(dx, ps[l], saved[l], D, (chip1, core), post, below)
        for n in BIG:
            r = reduced[n]
            done[n] = adamw_layer(r.reshape(-1, r.shape[-1]), *flat[n], l, done[n], "adamw_" + n)

    small_shapes = [(NL,) + grads[0][n].shape for n in SMALL]
    packed = _pack([0.5 / D * sumsq[0, :1]] + [jnp.stack([g[n] for g in grads]) for n in SMALL])
    total = allreduce_small(packed)
    loss, *small = _unpack(total, [(1,)] + small_shapes)
    small = dict(zip(SMALL, small))
    small["conv_w"] = lax.dynamic_slice_in_dim(small["conv_w"], chip * CS, CS, axis=2)

    given_small = dict(
        pre_norm_w=(pre_norm_w, m_pre_norm_w, v_pre_norm_w), pool_scale=(pool_scale, m_pool_scale, v_pool_scale),
        conv_w=(conv_w, m_conv_w, v_conv_w), conv_b=(conv_b, m_conv_b, v_conv_b),
        dt_bias=(dt_bias, m_dt_bias, v_dt_bias), a_log=(a_log, m_a_log, v_a_log),
        d_skip=(d_skip, m_d_skip, v_d_skip), ssd_norm_w=(ssd_norm_w, m_ssd_norm_w, v_ssd_norm_w),
        post_norm_w=(post_norm_w, m_post_norm_w, v_post_norm_w))
    shapes = [given_small[n][0].shape for n in SMALL]
    upd = adamw(_pack([small[n] for n in SMALL]), *[_pack([given_small[n][i] for n in SMALL]) for i in range(3)],
                "adamw_small")
    upd = [dict(zip(SMALL, _unpack(u, shapes))) for u in upd]

    out = {n: (small[n], upd[0][n], upd[1][n], upd[2][n]) for n in SMALL}
    for n in BIG:
        out[n] = tuple(t.reshape(given[n][0].shape) for t in done[n])
    out["w_in"] = tuple(tr(t) for t in out["w_in"])

    order = ("pre_norm_w", "w_in", "pool_mix_w", "pool_scale", "conv_w", "conv_b", "dt_bias", "a_log", "d_skip",
             "ssd_norm_w", "w_out", "post_norm_w")
    return (loss.reshape(()), dx[None], *[out[n][0] for n in order], *[out[n][1] for n in order],
            *[out[n][2] for n in order], *[out[n][3] for n in order])
```

```python
import functools
import math

import jax
import jax.numpy as jnp
from jax import lax
from jax.experimental import pallas as pl
from jax.experimental.pallas import tpu as pltpu

F32 = jnp.float32
BF16 = jnp.bfloat16

NORM_EPS = 1e-6
HEAD_DIM = 64
STATE = 128
GROUPS = 4
POOL_WINDOWS = (2, 4, 8, 16)
POOL_HALO = 16
CONV_K = 4
CONV_HALO = 8
SCAN_CHUNK = 256
LANES = 128
VMEM_LIMIT = 52 * 1024 * 1024
ROW_TILE = 256

ADAM_LR = 0.001
ADAM_B1 = 0.9
ADAM_B2 = 0.999
ADAM_EPS = 1e-08
ADAM_WD = 0.01
ADAM_STEP = 10

MESH = pl.DeviceIdType.MESH

NN = (((1,), (0,)), ((), ()))
NT = (((1,), (1,)), ((), ()))
TN = (((0,), (0,)), ((), ()))

_ANY = pl.BlockSpec(memory_space=pl.ANY)


def _params(*sem):
    return pltpu.CompilerParams(dimension_semantics=sem, vmem_limit_bytes=VMEM_LIMIT)


def _pick(dim, pref):
    if dim <= pref:
        return dim
    t = (pref // LANES) * LANES
    while t > LANES and dim % t:
        t -= LANES
    assert dim % t == 0, (dim, pref)
    return t


def _rows_tile(rows, pref):
    t = (min(pref, rows) // 8) * 8
    while t >= 8 and rows % t:
        t -= 8
    return t if t >= 8 else rows


def _dot(a, b, dn=NN):
    return lax.dot_general(a, b, dn, preferred_element_type=F32)


def _split3(a):
    hi = a.astype(BF16)
    r = a - hi.astype(F32)
    mid = r.astype(BF16)
    return hi, mid, (r - mid.astype(F32)).astype(BF16)


def _dot_sel(a, e, parts=3):
    hi, mid, lo = _split3(a)
    return (_dot(lo, e) + _dot(mid, e)) + _dot(hi, e) if parts == 3 else _dot(mid, e) + _dot(hi, e)


def _sel_dot(e, b):
    hi, mid, lo = _split3(b)
    return (_dot(e, lo) + _dot(e, mid)) + _dot(e, hi)


def _sigmoid(v):
    return 0.5 * jnp.tanh(0.5 * v) + 0.5


def _silu_and_grad(v):
    s = _sigmoid(v)
    return v * s, s * (1.0 + v * (1.0 - s))


def _row(i, shape):
    return lax.broadcasted_iota(jnp.int32, shape, 0) + i


def _sds(shape, dtype):
    return jax.ShapeDtypeStruct(tuple(shape), dtype)


class Job:
    def __init__(self, ins, outs, aliased, nsem, start, finish):
        self.ins, self.outs, self.aliased, self.nsem, self.start, self.finish = ins, outs, aliased, nsem, start, finish


def _place():
    x, y, c = lax.axis_index("x"), lax.axis_index("y"), lax.axis_index("c")
    return x, y, c, [(1 - x, y), (x, 1 - y), (1 - x, 1 - y)]


def _remote(src, dst, send_sem, recv_sem, device):
    return pltpu.make_async_remote_copy(src_ref=src, dst_ref=dst, send_sem=send_sem, recv_sem=recv_sem,
                                        device_id=device, device_id_type=MESH)


def _half(ref, c, cols, lead=0):
    idx = [slice(None)] * lead
    if cols:
        w = ref.shape[-1] // 2
        idx += [slice(None)] * (len(ref.shape) - lead - 1) + [pl.ds(pl.multiple_of(c * w, LANES), w)]
    else:
        idx += [c]
    return ref.at[tuple(idx)]


def _flags(cols, n):
    return list(cols) if cols else [False] * n


def gather_send_job(arrs, cols=None):
    n = len(arrs)
    cols = _flags(cols, n)

    def copies(ins, outs, send, recv):
        x, y, c, chips = _place()
        mine = 2 * x + y
        out = []
        for a in range(n):
            out.append(_remote(ins[a], outs[a].at[mine], send.at[4 * a + 3], recv.at[4 * a + 3], (x, y, 1 - c)))
            for j, chip in enumerate(chips):
                out.append(_remote(_half(ins[a], c, cols[a]), _half(outs[a].at[mine], c, cols[a]),
                                   send.at[4 * a + j], recv.at[4 * a + j], (*chip, c)))
        return out

    def start(ins, outs, send, recv):
        for cp in copies(ins, outs, send, recv):
            cp.start()

    def finish(ins, outs, send, recv):
        x, y, c, chips = _place()
        for a in range(n):
            for j, chip in enumerate(chips):
                landed = _half(outs[a].at[2 * chip[0] + chip[1]], c, cols[a])
                _remote(landed, landed, send.at[4 * a + j], recv.at[4 * a + j], (x, y, 1 - c)).wait_recv()
            twin = outs[a].at[2 * x + y]
            _remote(twin, twin, send.at[4 * a + 3], recv.at[4 * a + 3], (x, y, 1 - c)).wait_recv()
        for cp in copies(ins, outs, send, recv):
            cp.wait_send()

    return Job(list(arrs), [_sds((4,) + a.shape, a.dtype) for a in arrs], False, 4 * n, start, finish)


def gather_rows_job(shard, start, size, into=None):
    rows = pl.ds(start, size)

    def copies(ins, outs, send, recv):
        x, y, c, chips = _place()
        mine = outs[0].at[2 * x + y, rows]
        out = [_remote(ins[0].at[rows], mine, send.at[3], recv.at[3], (x, y, 1 - c))]
        for j, chip in enumerate(chips):
            out.append(_remote(_half(ins[0].at[rows], c, True), _half(mine, c, True), send.at[j], recv.at[j],
                               (*chip, c)))
        return out

    def start_(ins, outs, send, recv):
        for cp in copies(ins, outs, send, recv):
            cp.start()

    def finish(ins, outs, send, recv):
        x, y, c, chips = _place()
        for j, chip in enumerate(chips):
            landed = _half(outs[0].at[2 * chip[0] + chip[1], rows], c, True)
            _remote(landed, landed, send.at[j], recv.at[j], (x, y, 1 - c)).wait_recv()
        twin = outs[0].at[2 * x + y, rows]
        _remote(twin, twin, send.at[3], recv.at[3], (x, y, 1 - c)).wait_recv()
        for cp in copies(ins, outs, send, recv):
            cp.wait_send()

    ins = [shard] if into is None else [shard, into]
    return Job(ins, [_sds((4,) + shard.shape, shard.dtype)], {} if into is None else {1: 0}, 4, start_, finish)


def gather_pass_job(bufs, cols=None):
    n = len(bufs)
    cols = _flags(cols, n)

    def copies(outs, send, recv):
        x, y, c, chips = _place()
        out = []
        for a in range(n):
            for j, chip in enumerate(chips):
                landed = _half(outs[a].at[2 * chip[0] + chip[1]], c, cols[a])
                out.append(_remote(landed, landed, send.at[3 * a + j], recv.at[3 * a + j], (x, y, 1 - c)))
        return out

    def start(ins, outs, send, recv):
        for cp in copies(outs, send, recv):
            cp.start()

    def finish(ins, outs, send, recv):
        x, y, c, chips = _place()
        for a in range(n):
            for j, chip in enumerate(chips):
                passed = _half(outs[a].at[2 * chip[0] + chip[1]], 1 - c, cols[a])
                _remote(passed, passed, send.at[3 * a + j], recv.at[3 * a + j], (x, y, 1 - c)).wait_recv()
        for cp in copies(outs, send, recv):
            cp.wait_send()

    return Job(list(bufs), [_sds(b.shape, b.dtype) for b in bufs], True, 3 * n, start, finish)


def pair_exchange_job(arrs, cols=None):
    n = len(arrs)
    cols = _flags(cols, n)

    def copies(ins, outs, send, recv):
        x, y, c, _ = _place()
        return [_remote(_half(ins[a], 1 - c, cols[a], 1), outs[a], send.at[a], recv.at[a], (x, y, 1 - c))
                for a in range(n)]

    def start(ins, outs, send, recv):
        for cp in copies(ins, outs, send, recv):
            cp.start()

    def finish(ins, outs, send, recv):
        for cp in copies(ins, outs, send, recv):
            cp.wait()

    shape = lambda a, k: a.shape[:-1] + (a.shape[-1] // 2,) if k else a.shape[:1] + a.shape[2:]
    return Job(list(arrs), [_sds(shape(a, k), a.dtype) for a, k in zip(arrs, cols)], False, n, start, finish)


def chip_exchange_job(arrs):
    n = len(arrs)

    def copies(ins, outs, send, recv):
        x, y, c, chips = _place()
        return [_remote(ins[a].at[2 * chip[0] + chip[1]], outs[a].at[j], send.at[3 * a + j], recv.at[3 * a + j],
                        (*chip, c)) for a in range(n) for j, chip in enumerate(chips)]

    def start(ins, outs, send, recv):
        for cp in copies(ins, outs, send, recv):
            cp.start()

    def finish(ins, outs, send, recv):
        for cp in copies(ins, outs, send, recv):
            cp.wait()

    return Job(list(arrs), [_sds((3,) + a.shape[1:], a.dtype) for a in arrs], False, 3 * n, start, finish)


def pair_gather_job(bufs, cols=None):
    n = len(bufs)
    cols = _flags(cols, n)

    def copies(outs, send, recv):
        x, y, c, _ = _place()
        return [_remote(_half(outs[a], c, cols[a]), _half(outs[a], c, cols[a]), send.at[a], recv.at[a],
                        (x, y, 1 - c)) for a in range(n)]

    def start(ins, outs, send, recv):
        for cp in copies(outs, send, recv):
            cp.start()

    def finish(ins, outs, send, recv):
        for cp in copies(outs, send, recv):
            cp.wait()

    return Job(list(bufs), [_sds(b.shape, b.dtype) for b in bufs], True, n, start, finish)


def _call(body, *, grid, in_specs, out_specs, out_shape, name, sem, args, scratch_shapes=(), jobs=(), aliases=None):
    in_specs, out_specs, out_shape, scratch_shapes = list(in_specs), list(out_specs), list(out_shape), list(scratch_shapes)
    aliases = dict(aliases or {})
    n_in, n_out, n_scr = len(in_specs), len(out_specs), len(scratch_shapes)
    if jobs:
        sem = ("arbitrary",) * len(grid)
    at_in, at_out = n_in, n_out
    for j in jobs:
        pairs = j.aliased if isinstance(j.aliased, dict) else {i: i for i in range(len(j.ins))} if j.aliased else {}
        aliases.update({at_in + i: at_out + o for i, o in pairs.items()})
        at_in, at_out = at_in + len(j.ins), at_out + len(j.outs)

    def wrapped(*refs):
        ins, p = refs[:n_in], n_in
        jins = []
        for j in jobs:
            jins.append(refs[p:p + len(j.ins)])
            p += len(j.ins)
        outs, p = refs[p:p + n_out], p + n_out
        jouts = []
        for j in jobs:
            jouts.append(refs[p:p + len(j.outs)])
            p += len(j.outs)
        scr, sems = refs[p:p + n_scr], refs[p + n_scr:]

        def start():
            for k, j in enumerate(jobs):
                j.start(jins[k], jouts[k], sems[2 * k], sems[2 * k + 1])

        def finish():
            for k, j in enumerate(jobs):
                j.finish(jins[k], jouts[k], sems[2 * k], sems[2 * k + 1])

        if jobs and grid:
            ids = [pl.program_id(d) for d in range(len(grid))]
            pl.when(functools.reduce(jnp.logical_and, [i == 0 for i in ids]))(start)
            body(*ins, *outs, *scr)
            pl.when(functools.reduce(jnp.logical_and, [i == g - 1 for i, g in zip(ids, grid)]))(finish)
        else:
            start()
            body(*ins, *outs, *scr)
            finish()

    kwargs = dict(grid=grid) if grid else {}
    res = pl.pallas_call(
        wrapped, in_specs=in_specs + [_ANY] * (at_in - n_in), out_specs=out_specs + [_ANY] * (at_out - n_out),
        out_shape=out_shape + [o for j in jobs for o in j.outs],
        scratch_shapes=scratch_shapes + [pltpu.SemaphoreType.DMA((j.nsem,)) for j in jobs for _ in range(2)],
        input_output_aliases=aliases, name=name,
        compiler_params=pltpu.CompilerParams(dimension_semantics=sem, vmem_limit_bytes=VMEM_LIMIT) if grid
        else pltpu.CompilerParams(vmem_limit_bytes=VMEM_LIMIT), **kwargs)(*args, *[a for j in jobs for a in j.ins])
    res = list(res)
    outs, rest, per_job = res[:n_out], res[n_out:], []
    for j in jobs:
        per_job.append(rest[:len(j.outs)])
        rest = rest[len(j.outs):]
    return outs, per_job


def run_jobs(jobs, name):
    return _call(lambda: None, grid=(), in_specs=[], out_specs=[], out_shape=[], name=name, sem=(), args=(), jobs=jobs)[1]


def pair_add(g, r, core, out_dtype, cols=False, out_rows=None):
    S, R, C = r.shape
    T = _rows_tile(R, 2 * ROW_TILE if C <= 1024 else ROW_TILE if C <= 4096 else ROW_TILE // 2)

    def body(c_ref, g_ref, r_ref, o_ref):
        o_ref[0] = ((g_ref[0] if cols else g_ref[0, 0]).astype(F32) + r_ref[0].astype(F32)).astype(o_ref.dtype)

    g_spec = (pl.BlockSpec((1, T, C), lambda k, i, c_ref: (k, i, c_ref[0])) if cols
              else pl.BlockSpec((1, 1, T, C), lambda k, i, c_ref: (k, c_ref[0], i, 0)))
    return pl.pallas_call(
        body,
        grid_spec=pltpu.PrefetchScalarGridSpec(
            num_scalar_prefetch=1, grid=(S, R // T),
            in_specs=[g_spec, pl.BlockSpec((1, T, C), lambda k, i, c_ref: (k, i, 0))],
            out_specs=pl.BlockSpec((1, T, C), lambda k, i, c_ref: (k, i, 0))),
        out_shape=_sds((S, out_rows or R, C), out_dtype), name="pair_add",
        compiler_params=_params("parallel", "parallel"))(core, g, r)


def chip_add(p, r, chip, core, cols=False):
    _, R, C = p.shape
    T = _rows_tile(R, ROW_TILE)

    def body(k_ref, c_ref, p_ref, r0_ref, r1_ref, r2_ref, o_ref):
        s = ((p_ref[0].astype(F32) + r0_ref[0].astype(F32)) + r1_ref[0].astype(F32)) + r2_ref[0].astype(F32)
        if cols:
            o_ref[...] = s
        else:
            o_ref[0] = s

    slot = lambda j: pl.BlockSpec((1, T, C), lambda i, k_ref, c_ref: (j, i, 0))
    out_spec = (pl.BlockSpec((T, C), lambda i, k_ref, c_ref: (i, c_ref[0])) if cols
                else pl.BlockSpec((1, T, C), lambda i, k_ref, c_ref: (c_ref[0], i, 0)))
    return pl.pallas_call(
        body,
        grid_spec=pltpu.PrefetchScalarGridSpec(
            num_scalar_prefetch=2, grid=(R // T,),
            in_specs=[pl.BlockSpec((1, T, C), lambda i, k_ref, c_ref: (k_ref[0], i, 0)), slot(0), slot(1), slot(2)],
            out_specs=out_spec),
        out_shape=_sds((R, 2 * C) if cols else (2, R, C), F32), name="chip_add",
        compiler_params=_params("parallel"))(chip, core, p, r, r, r)


def allreduce_small(v):
    R = v.shape[0]

    def body(v_ref, o_ref, buf, send_sems, recv_sems, local_sem):
        x, y, c, chips = _place()
        me, sibling = (x, y, c), (x, y, 1 - c)

        def rows(px, py, pc):
            return buf.at[pl.ds((4 * px + 2 * py + pc) * R, R), :]

        def copy(k, block, to, src=None):
            return _remote(rows(*block) if src is None else src, rows(*block), send_sems.at[k], recv_sems.at[k], to)

        mine = pltpu.make_async_copy(v_ref, rows(*me), local_sem)
        mine.start()
        first = [copy(0, me, sibling, src=v_ref)]
        first += [copy(1 + j, me, (*chip, c), src=v_ref) for j, chip in enumerate(chips)]
        for cp in first:
            cp.start()
        passed = [copy(4 + j, (*chip, c), sibling) for j, chip in enumerate(chips)]
        for j, chip in enumerate(chips):
            copy(1 + j, (*chip, c), me).wait_recv()
            passed[j].start()
        copy(0, sibling, me).wait_recv()
        for j, chip in enumerate(chips):
            copy(4 + j, (*chip, 1 - c), me).wait_recv()
        for cp in first + passed:
            cp.wait_send()
        mine.wait()
        acc = buf[0:R, :]
        for d in range(1, 8):
            acc = acc + buf[d * R:(d + 1) * R, :]
        o_ref[...] = acc

    return pl.pallas_call(
        body, in_specs=[pl.BlockSpec(memory_space=pltpu.VMEM)], out_specs=pl.BlockSpec(memory_space=pltpu.VMEM),
        out_shape=_sds((R, LANES), F32),
        scratch_shapes=[pltpu.VMEM((8 * R, LANES), F32), pltpu.SemaphoreType.DMA((7,)),
                        pltpu.SemaphoreType.DMA((7,)), pltpu.SemaphoreType.DMA],
        name="allreduce_small", compiler_params=pltpu.CompilerParams(vmem_limit_bytes=VMEM_LIMIT))(v)


def matmul(a, b, mode, out_dtype, name, tm=512, tn=1024, tk=4608, jobs=(), n_out=None):
    if mode == "nn":
        (M, K), (K2, N) = a.shape, b.shape
    elif mode == "nt":
        (M, K), (N, K2) = a.shape, b.shape
        N = n_out or N
    else:
        (K, M), (K2, N) = a.shape, b.shape
    assert K == K2 or (mode == "nn" and K2 > K)
    tm, tn, tk = _pick(M, tm), _pick(N, tn), _pick(K, tk)
    nk = K // tk
    dn = {"nn": NN, "nt": NT, "tn": TN}[mode]

    def body(a_ref, b_ref, o_ref, *acc):
        part = _dot(a_ref[...].astype(BF16), b_ref[...].astype(BF16), dn)
        if nk == 1:
            o_ref[...] = part.astype(o_ref.dtype)
            return
        acc_ref, = acc
        k = pl.program_id(2)

        @pl.when(k == 0)
        def _():
            acc_ref[...] = part

        @pl.when(jnp.logical_and(k > 0, k < nk - 1))
        def _():
            acc_ref[...] += part

        @pl.when(k == nk - 1)
        def _():
            o_ref[...] = (acc_ref[...] + part).astype(o_ref.dtype)

    a_spec = (pl.BlockSpec((tk, tm), lambda i, j, k: (k, i)) if mode == "tn"
              else pl.BlockSpec((tm, tk), lambda i, j, k: (i, k)))
    b_spec = (pl.BlockSpec((tn, tk), lambda i, j, k: (j, k)) if mode == "nt"
              else pl.BlockSpec((tk, tn), lambda i, j, k: (k, j)))
    outs, per_job = _call(
        body, grid=(M // tm, N // tn, nk), in_specs=[a_spec, b_spec],
        out_specs=[pl.BlockSpec((tm, tn), lambda i, j, k: (i, j))], out_shape=[_sds((M, N), out_dtype)],
        scratch_shapes=[pltpu.VMEM((tm, tn), F32)] if nk > 1 else [], name=name,
        sem=("parallel", "parallel", "arbitrary"), args=(a, b), jobs=jobs)
    return outs[0], per_job


def rms_fwd(x, w):
    L, D = x.shape
    T = _pick(L, ROW_TILE)

    def body(x_ref, w_ref, h_ref):
        xv = x_ref[...]
        r = lax.rsqrt(jnp.mean(xv * xv, axis=-1, keepdims=True) + NORM_EPS)
        h_ref[...] = (xv * r * w_ref[...]).astype(h_ref.dtype)

    return pl.pallas_call(
        body, grid=(L // T,),
        in_specs=[pl.BlockSpec((T, D), lambda i: (i, 0)), pl.BlockSpec((1, D), lambda i: (0, 0))],
        out_specs=pl.BlockSpec((T, D), lambda i: (i, 0)),
        out_shape=_sds((L, D), BF16), name="rms_fwd", compiler_params=_params("parallel"))(x, w)


def post_fwd(x, o, w, next_w=None, jobs=()):
    L, D = x.shape
    T = _pick(L, ROW_TILE)

    def body(x_ref, o_ref, w_ref, *rest):
        ov = o_ref[...]
        r = lax.rsqrt(jnp.mean(ov * ov, axis=-1, keepdims=True) + NORM_EPS)
        y = x_ref[...] + ov * r * w_ref[...]
        rest[-1 if next_w is None else -2][...] = y
        if next_w is not None:
            r2 = lax.rsqrt(jnp.mean(y * y, axis=-1, keepdims=True) + NORM_EPS)
            rest[-1][...] = (y * r2 * rest[0][...]).astype(BF16)

    row = pl.BlockSpec((T, D), lambda i: (i, 0))
    vec = pl.BlockSpec((1, D), lambda i: (0, 0))
    more = [] if next_w is None else [next_w]
    outs, per_job = _call(
        body, grid=(L // T,), in_specs=[row, row, vec] + [vec] * len(more), out_specs=[row] * (1 + len(more)),
        out_shape=[_sds((L, D), F32)] + [_sds((L, D), BF16)] * len(more), name="post_fwd", sem=("parallel",),
        args=(x, o, w, *more), jobs=jobs)
    return (outs[0], outs[1] if more else None), per_job


def _rms_bwd_math(xv, w, dy):
    r = lax.rsqrt(jnp.mean(xv * xv, axis=-1, keepdims=True) + NORM_EPS)
    xhat = xv * r
    g = dy * w
    dx = r * (g - xhat * jnp.mean(g * xhat, axis=-1, keepdims=True))
    return dx, jnp.sum(dy * xhat, axis=0, keepdims=True)


def post_bwd(o, w, dxn):
    L, D = o.shape
    T = _pick(L, ROW_TILE)

    def body(o_ref, w_ref, d_ref, do_ref, dw_ref):
        dx, dw = _rms_bwd_math(o_ref[...], w_ref[...], d_ref[...])
        do_ref[...] = dx.astype(do_ref.dtype)

        @pl.when(pl.program_id(0) == 0)
        def _():
            dw_ref[...] = jnp.zeros_like(dw_ref)

        dw_ref[...] += dw

    row = pl.BlockSpec((T, D), lambda i: (i, 0))
    vec = pl.BlockSpec((1, D), lambda i: (0, 0))
    return pl.pallas_call(
        body, grid=(L // T,), in_specs=[row, vec, row], out_specs=[row, vec],
        out_shape=[_sds((L, D), BF16), _sds((1, D), F32)],
        name="post_bwd", compiler_params=_params("arbitrary"))(o, w, dxn)


def rms_bwd(x, w, dh_a, dh_b, dxn, below=None, jobs=()):
    L, D = x.shape
    T = _pick(L, ROW_TILE)

    def body(x_ref, w_ref, a_ref, b_ref, d_ref, *rest):
        dx, dw = _rms_bwd_math(x_ref[...], w_ref[...], a_ref[...] + b_ref[...])
        dx = d_ref[...] + dx
        outs = rest[2:] if below else rest
        outs[0][...] = dx

        @pl.when(pl.program_id(0) == 0)
        def _():
            for acc in outs[1::2]:
                acc[...] = jnp.zeros_like(acc)

        outs[1][...] += dw
        if below:
            do, dwp = _rms_bwd_math(rest[0][...], rest[1][...], dx)
            outs[2][...] = do.astype(BF16)
            outs[3][...] += dwp

    row = pl.BlockSpec((T, D), lambda i: (i, 0))
    vec = pl.BlockSpec((1, D), lambda i: (0, 0))
    more = list(below) if below else []
    return _call(
        body, grid=(L // T,), in_specs=[row, vec, row, row, row] + ([row, vec] if below else []),
        out_specs=[row, vec] + ([row, vec] if below else []),
        out_shape=[_sds((L, D), F32), _sds((1, D), F32)] + ([_sds((L, D), BF16), _sds((1, D), F32)] if below else []),
        name="rms_bwd", sem=("arbitrary",), args=(x, w, dh_a, dh_b, dxn, *more), jobs=jobs)


def loss_head(y, target):
    L, D = y.shape
    T = _pick(L, ROW_TILE)

    def body(y_ref, t_ref, d_ref, s_ref):
        e = y_ref[...] - t_ref[...]
        d_ref[...] = e * (1.0 / D)

        @pl.when(pl.program_id(0) == 0)
        def _():
            s_ref[...] = jnp.zeros_like(s_ref)

        s_ref[...] += jnp.sum(e * e)

    row = pl.BlockSpec((T, D), lambda i: (i, 0))
    return pl.pallas_call(
        body, grid=(L // T,), in_specs=[row, row],
        out_specs=[row, pl.BlockSpec((8, LANES), lambda i: (0, 0))],
        out_shape=[_sds((L, D), F32), _sds((8, LANES), F32)],
        name="loss_head", compiler_params=_params("arbitrary"))(y, target)


def _window_sums(xe, w, back):
    n = xe.shape[0]
    s, k = xe, 1
    while k < w:
        s = s + pltpu.roll(s, k if back else n - k, 0)
        k *= 2
    return s


def pool_fwd(proj, mixw, scale, D):
    L = proj.shape[0]
    PGW = D // GROUPS
    T = _pick(L, ROW_TILE)
    hb = T // POOL_HALO

    def body(u_ref, halo_ref, g_ref, mw_ref, sc_ref, y_ref, p_ref):
        i = pl.program_id(0)
        u = u_ref[...]
        halo = jnp.where(i > 0, halo_ref[...], 0.0)
        xe = jnp.concatenate([halo, u], axis=0)
        t1 = _row(i * T + 1, (T, 1))
        for g, w in enumerate(POOL_WINDOWS):
            sl = slice(g * PGW, (g + 1) * PGW)
            win = _window_sums(xe[:, sl], w, True)[POOL_HALO:, :]
            cnt = jnp.minimum(t1, w).astype(F32)
            pooled = (win / cnt - u[:, sl]).astype(BF16)
            p_ref[:, sl] = pooled
            mixed = _dot(pooled, mw_ref[g])
            gate = g_ref[:, sl]
            y_ref[:, sl] = (mixed * sc_ref[:, sl] * (gate * _sigmoid(gate))).astype(BF16)

    return pl.pallas_call(
        body, grid=(L // T,),
        in_specs=[pl.BlockSpec((T, D), lambda i: (i, 0)),
                  pl.BlockSpec((POOL_HALO, D), lambda i: (jnp.maximum(i * hb - 1, 0), 0)),
                  pl.BlockSpec((T, D), lambda i: (i, 1)),
                  pl.BlockSpec((GROUPS, PGW, PGW), lambda i: (0, 0, 0)),
                  pl.BlockSpec((1, D), lambda i: (0, 0))],
        out_specs=[pl.BlockSpec((T, D), lambda i: (i, 0)), pl.BlockSpec((T, D), lambda i: (i, 0))],
        out_shape=[_sds((L, 2 * D), BF16), _sds((L, D), BF16)],
        name="pool_fwd", compiler_params=_params("parallel"))(proj, proj, proj, mixw, scale)


def pool_bwd(proj, dmix, pooled, mixw, scale, dproj, D):
    L = proj.shape[0]
    PGW = D // GROUPS
    T = _pick(L, ROW_TILE)
    hb = T // POOL_HALO
    nT = L // T

    def body(g_ref, gh_ref, dy_ref, dyh_ref, p_ref, mw_ref, sc_ref, old_ref, dp_ref, dm_ref, ds_ref):
        i = pl.program_id(0)
        t1 = _row(i * T + 1, (T, 1))
        th1 = _row((i + 1) * T + 1, (POOL_HALO, 1))
        live = i < nT - 1

        @pl.when(i == 0)
        def _():
            ds_ref[...] = jnp.zeros_like(ds_ref)
            dm_ref[...] = jnp.zeros_like(dm_ref)

        for g, w in enumerate(POOL_WINDOWS):
            sl = slice(g * PGW, (g + 1) * PGW)
            sc = sc_ref[:, sl]
            gate, dy = g_ref[:, sl], dy_ref[:, sl]
            sg, dsg = _silu_and_grad(gate)
            pooled = p_ref[:, sl]
            mixed = _dot(pooled, mw_ref[g])
            dmixed = (dy * sc * sg).astype(BF16)
            dm_ref[g] += _dot(pooled, dmixed, TN)
            dp_ref[:, D + g * PGW:D + (g + 1) * PGW] = (dy * mixed * sc * dsg).astype(BF16)
            ds_ref[:, sl] += jnp.sum(dy * mixed * sg, axis=0, keepdims=True)
            dpool = _dot(dmixed, mw_ref[g], NT)
            gate_h = gh_ref[:, sl]
            dmixed_h = (dyh_ref[:, sl] * sc * (gate_h * _sigmoid(gate_h))).astype(BF16)
            dpool_h = jnp.where(live, _dot(dmixed_h, mw_ref[g], NT), 0.0)
            q = dpool / jnp.minimum(t1, w).astype(F32)
            q_h = dpool_h / jnp.minimum(th1, w).astype(F32)
            qe = jnp.concatenate([q, q_h], axis=0)
            dp_ref[:, sl] = (_window_sums(qe, w, False)[:T, :] - dpool).astype(BF16)

    nxt = lambda i: jnp.minimum((i + 1) * hb, L // POOL_HALO - 1)
    row = lambda c: pl.BlockSpec((T, D), lambda i: (i, c))
    return pl.pallas_call(
        body, grid=(nT,),
        in_specs=[row(1), pl.BlockSpec((POOL_HALO, D), lambda i: (nxt(i), 1)),
                  row(0), pl.BlockSpec((POOL_HALO, D), lambda i: (nxt(i), 0)),
                  row(0), pl.BlockSpec((GROUPS, PGW, PGW), lambda i: (0, 0, 0)),
                  pl.BlockSpec((1, D), lambda i: (0, 0)), _ANY],
        out_specs=[pl.BlockSpec((T, 2 * D), lambda i: (i, 0)), pl.BlockSpec((GROUPS, PGW, PGW), lambda i: (0, 0, 0)),
                   pl.BlockSpec((1, D), lambda i: (0, 0))],
        out_shape=[_sds(dproj.shape, dproj.dtype), _sds((GROUPS, PGW, PGW), F32), _sds((1, D), F32)],
        input_output_aliases={7: 0},
        name="pool_bwd", compiler_params=_params("arbitrary"))(proj, proj, dmix, dmix, pooled, mixw, scale, dproj)


def conv_fwd(proj, cw, cb, D):
    L = proj.shape[0]
    C = cw.shape[1]
    assert (3 * D) % C == 0
    cblk = (3 * D) // C
    T = _pick(L, ROW_TILE)
    hb = T // CONV_HALO

    def body(u_ref, halo_ref, w_ref, b_ref, o_ref):
        i = pl.program_id(0)
        u = u_ref[...]
        xe = jnp.concatenate([jnp.where(i > 0, halo_ref[...], 0.0), u], axis=0)
        acc = b_ref[...] + w_ref[CONV_K - 1:CONV_K, :] * u
        for k in range(CONV_K - 1):
            acc = acc + w_ref[k:k + 1, :] * pltpu.roll(xe, CONV_K - 1 - k, 0)[CONV_HALO:, :]
        o_ref[...] = acc

    return pl.pallas_call(
        body, grid=(L // T,),
        in_specs=[pl.BlockSpec((T, C), lambda i: (i, cblk)),
                  pl.BlockSpec((CONV_HALO, C), lambda i: (jnp.maximum(i * hb - 1, 0), cblk)),
                  pl.BlockSpec((CONV_K, C), lambda i: (0, 0)),
                  pl.BlockSpec((1, C), lambda i: (0, 0))],
        out_specs=pl.BlockSpec((T, C), lambda i: (i, 0)),
        out_shape=_sds((L, C), F32), name="conv_fwd", compiler_params=_params("parallel"))(proj, proj, cw, cb)


def conv_bwd(dparts, proj, cw, dproj, D, jobs=()):
    L = proj.shape[0]
    C = cw.shape[1]
    cblk = (3 * D) // C
    T = _pick(L, ROW_TILE)
    hb = T // CONV_HALO
    nT = L // T
    widths = [p.shape[1] for p in dparts]
    assert sum(widths) == C
    n = len(dparts)

    def body(*refs):
        d_refs, dn_refs = refs[:n], refs[n:2 * n]
        u_ref, up_ref, w_ref, old_ref, dr_ref, dw_ref, db_ref = refs[2 * n:]
        i = pl.program_id(0)

        @pl.when(i == 0)
        def _():
            dw_ref[...] = jnp.zeros_like(dw_ref)
            db_ref[...] = jnp.zeros_like(db_ref)

        at = 0
        for d_ref, dn_ref, wd in zip(d_refs, dn_refs, widths):
            sl = slice(at, at + wd)
            at += wd
            d = d_ref[...]
            u = u_ref[:, sl]
            de = jnp.concatenate([d, jnp.where(i < nT - 1, dn_ref[...], 0.0)], axis=0)
            ue = jnp.concatenate([jnp.where(i > 0, up_ref[:, sl], 0.0), u], axis=0)
            acc = w_ref[CONV_K - 1:CONV_K, sl] * d
            dw_ref[CONV_K - 1:CONV_K, sl] += jnp.sum(d * u, axis=0, keepdims=True)
            for k in range(CONV_K - 1):
                sh = CONV_K - 1 - k
                acc = acc + w_ref[k:k + 1, sl] * pltpu.roll(de, T + CONV_HALO - sh, 0)[:T, :]
                dw_ref[k:k + 1, sl] += jnp.sum(d * pltpu.roll(ue, sh, 0)[CONV_HALO:, :], axis=0, keepdims=True)
            dr_ref[:, sl] = acc.astype(dr_ref.dtype)
            db_ref[:, sl] += jnp.sum(d, axis=0, keepdims=True)

    nxt = lambda i: jnp.minimum((i + 1) * hb, L // CONV_HALO - 1)
    return _call(
        body, grid=(nT,),
        in_specs=[pl.BlockSpec((T, wd), lambda i: (i, 0)) for wd in widths]
        + [pl.BlockSpec((CONV_HALO, wd), lambda i: (nxt(i), 0)) for wd in widths]
        + [pl.BlockSpec((T, C), lambda i: (i, cblk)),
           pl.BlockSpec((CONV_HALO, C), lambda i: (jnp.maximum(i * hb - 1, 0), cblk)),
           pl.BlockSpec((CONV_K, C), lambda i: (0, 0)), _ANY],
        out_specs=[pl.BlockSpec((T, C), lambda i: (i, cblk)),
                   pl.BlockSpec((8, C), lambda i: (0, 0)),
                   pl.BlockSpec((1, C), lambda i: (0, 0))],
        out_shape=[_sds(dproj.shape, dproj.dtype), _sds((8, C), F32), _sds((1, C), F32)],
        aliases={2 * n + 3: 0}, name="conv_bwd", sem=("arbitrary",),
        args=(*dparts, *dparts, proj, proj, cw, dproj), jobs=jobs)


def _softplus(v):
    y = jnp.exp(-jnp.abs(v))
    u = 1.0 + y
    log1p = jnp.where(u == 1.0, y, jnp.log(u) * y / jnp.where(u == 1.0, 1.0, u - 1.0))
    return jnp.maximum(v, 0.0) + log1p


def dt_prep(dtraw, bias, alog, expand, D, jobs=()):
    L = dtraw.shape[0]
    GC = D // GROUPS
    HPG = GC // HEAD_DIM
    Q = _pick(L, SCAN_CHUNK)
    nc = L // Q

    def body(r_ref, b_ref, a_ref, e_ref, dt_ref, acs_ref, acst_ref, dtx_ref, eax_ref, dsx_ref, cdx_ref):
        valid = lax.broadcasted_iota(jnp.int32, (1, LANES), 1) < HPG
        dt = jnp.where(valid, _softplus(r_ref[...] + b_ref[...]), 0.0)
        adt = dt * -jnp.exp(a_ref[...])
        tril = (_row(0, (Q, Q)) >= lax.broadcasted_iota(jnp.int32, (Q, Q), 1)).astype(BF16)
        acs = _sel_dot(tril, adt)
        last = acs[Q - 1:Q, :]
        dt_ref[...] = dt
        acs_ref[...] = acs
        acst_ref[...] = acs.T
        e = e_ref[...]
        dtx_ref[...] = _dot_sel(dt, e, 2)
        eax_ref[...] = jnp.exp(_dot_sel(acs, e, 2))
        dsx_ref[...] = jnp.exp(_dot_sel(last - acs, e, 2))
        cdx_ref[0] = jnp.exp(_dot_sel(jnp.broadcast_to(last, (8, LANES)), e, 2))

    head = pl.BlockSpec((Q, LANES), lambda g, c: (c, g))
    hvec = pl.BlockSpec((1, LANES), lambda g, c: (0, g))
    chan = pl.BlockSpec((Q, GC), lambda g, c: (c, g))
    return _call(
        body, grid=(GROUPS, nc),
        in_specs=[head, hvec, hvec, pl.BlockSpec((LANES, GC), lambda g, c: (0, 0))],
        out_specs=[head, head, pl.BlockSpec((LANES, Q), lambda g, c: (g, c)), chan, chan, chan,
                   pl.BlockSpec((1, 8, GC), lambda g, c: (c, 0, g))],
        out_shape=[_sds((L, GROUPS * LANES), F32), _sds((L, GROUPS * LANES), F32), _sds((GROUPS * LANES, L), F32),
                   _sds((L, D), F32), _sds((L, D), F32), _sds((L, D), F32), _sds((nc, 8, D), F32)],
        name="dt_prep", sem=("parallel", "parallel"), args=(dtraw, bias, alog, expand), jobs=jobs)


def _scan_specs(L, D, Q, rev):
    GC = D // GROUPS
    nc = L // Q
    ci = (lambda c: nc - 1 - c) if rev else (lambda c: c)
    return dict(
        xs=pl.BlockSpec((Q, GC), lambda g, c: (ci(c), g)),
        b=pl.BlockSpec((Q, STATE), lambda g, c: (ci(c), D // STATE + g)),
        c=pl.BlockSpec((Q, STATE), lambda g, c: (ci(c), D // STATE + GROUPS + g)),
        chan=pl.BlockSpec((Q, GC), lambda g, c: (ci(c), g)),
        cdx=pl.BlockSpec((1, 8, GC), lambda g, c: (ci(c), 0, g)),
        head=pl.BlockSpec((Q, LANES), lambda g, c: (ci(c), g)),
        headt=pl.BlockSpec((LANES, Q), lambda g, c: (g, ci(c))),
        state=pl.BlockSpec((1, 1, STATE, GC), lambda g, c: (ci(c), g, 0, 0)),
        hvec=pl.BlockSpec((1, LANES), lambda g, c: (0, g)),
        cvec=pl.BlockSpec((1, GC), lambda g, c: (0, g)))


def scan_fwd(pre, dtx, eax, dsx, cdx, acs, acst, proj, dexp, nw, mixed, D, jobs=()):
    L = pre.shape[0]
    GC = D // GROUPS
    Q = _pick(L, SCAN_CHUNK)
    nc = L // Q
    sp = _scan_specs(L, D, Q, False)

    def body(xs_ref, b_ref, c_ref, dtx_ref, eax_ref, dsx_ref, cdx_ref, acs_ref, acst_ref, z_ref, de_ref, nw_ref,
             old_ref, y_ref, st_ref, o_ref, s_scr):
        @pl.when(pl.program_id(1) == 0)
        def _():
            s_scr[...] = jnp.zeros_like(s_scr)

        tri = _row(0, (Q, Q)) >= lax.broadcasted_iota(jnp.int32, (Q, Q), 1)
        half = lax.broadcasted_iota(jnp.int32, (1, LANES), 1) // HEAD_DIM
        xs, _ = _silu_and_grad(xs_ref[...])
        bg = _silu_and_grad(b_ref[...])[0].astype(BF16)
        cg = _silu_and_grad(c_ref[...])[0].astype(BF16)
        xdt = xs * dtx_ref[...]
        sprev = s_scr[...]
        st_ref[0, 0] = sprev
        sc = _dot(cg, bg, NT)
        yoff = _dot(cg, sprev.astype(BF16)) * eax_ref[...]
        for j in range(GC // LANES):
            ps = slice(j * LANES, (j + 1) * LANES)
            xp = xdt[:, ps]
            acc = yoff[:, ps]
            for hh in range(2):
                h = 2 * j + hh
                lm = jnp.exp(jnp.where(tri, acs_ref[:, h:h + 1] - acst_ref[h:h + 1, :], -1e30))
                xm = jnp.where(half == hh, xp, 0.0).astype(BF16)
                acc = acc + _dot((sc * lm).astype(BF16), xm)
            y_ref[:, ps] = acc
        xw = (xdt * dsx_ref[...]).astype(BF16)
        s_scr[...] = cdx_ref[0, 0:1, :] * sprev + _dot(bg, xw, TN)
        z = z_ref[...]
        y3 = (y_ref[...] + de_ref[...] * xs) * (z * _sigmoid(z))
        r = lax.rsqrt(jnp.mean(y3 * y3, axis=-1, keepdims=True) + NORM_EPS)
        o_ref[...] = (y3 * r * nw_ref[...]).astype(o_ref.dtype)

    return _call(
        body, grid=(GROUPS, nc),
        in_specs=[sp["xs"], sp["b"], sp["c"], sp["chan"], sp["chan"], sp["chan"], sp["cdx"], sp["head"], sp["headt"],
                  pl.BlockSpec((Q, GC), lambda g, c: (c, 2 * GROUPS + g)), sp["cvec"], sp["cvec"], _ANY],
        out_specs=[sp["chan"], sp["state"], pl.BlockSpec((Q, GC), lambda g, c: (c, GROUPS + g))],
        out_shape=[_sds((L, D), F32), _sds((nc, GROUPS, STATE, GC), F32), _sds(mixed.shape, mixed.dtype)],
        aliases={12: 2}, scratch_shapes=[pltpu.VMEM((STATE, GC), F32)], name="scan_fwd",
        sem=("parallel", "arbitrary"), args=(pre, pre, pre, dtx, eax, dsx, cdx, acs, acst, proj, dexp, nw, mixed),
        jobs=jobs)


def scan_bwd(pre, dtx, eax, dsx, cdx, acs, acst, dt, dtraw, bias, alog, states, y, proj, dmix, nw, dexp, collapse, D,
             jobs=()):
    L = pre.shape[0]
    GC = D // GROUPS
    Q = _pick(L, SCAN_CHUNK)
    nc = L // Q
    sp = _scan_specs(L, D, Q, True)
    rc = lambda c: nc - 1 - c

    def body(xs_ref, b_ref, c_ref, dtx_ref, eax_ref, dsx_ref, cdx_ref, acs_ref, acst_ref, dt_ref, raw_ref,
             bias_ref, alog_ref, st_ref, y_ref, z_ref, dm_ref, nw_ref, dexp_ref, col_ref,
             dxs_ref, db_ref, dc_ref, ddt_ref, dal_ref, dbi_ref, dz_ref, dnw_ref, dde_ref, ds_scr, dx_scr):
        first = pl.program_id(1) == 0

        @pl.when(first)
        def _():
            ds_scr[...] = jnp.zeros_like(ds_scr)
            dal_ref[...] = jnp.zeros_like(dal_ref)
            dbi_ref[...] = jnp.zeros_like(dbi_ref)
            dnw_ref[...] = jnp.zeros_like(dnw_ref)
            dde_ref[...] = jnp.zeros_like(dde_ref)

        li = _row(0, (Q, Q))
        si = lax.broadcasted_iota(jnp.int32, (Q, Q), 1)
        lane = lax.broadcasted_iota(jnp.int32, (1, LANES), 1)
        half = lane // HEAD_DIM
        xs_pre, b_pre, c_pre = xs_ref[...], b_ref[...], c_ref[...]
        xs, xs_g = _silu_and_grad(xs_pre)
        bf, b_g = _silu_and_grad(b_pre)
        cf, c_g = _silu_and_grad(c_pre)
        bg, cg = bf.astype(BF16), cf.astype(BF16)
        dtx, eax, dsx = dtx_ref[...], eax_ref[...], dsx_ref[...]
        cd = cdx_ref[0, 0:1, :]
        xdt = xs * dtx
        sz, dsz = _silu_and_grad(z_ref[...])
        y2 = y_ref[...] + dexp_ref[...] * xs
        y3 = y2 * sz
        r = lax.rsqrt(jnp.mean(y3 * y3, axis=-1, keepdims=True) + NORM_EPS)
        n = y3 * r
        dm = dm_ref[...]
        gg = dm * nw_ref[...]
        dy3 = r * (gg - n * jnp.mean(gg * n, axis=-1, keepdims=True))
        dnw_ref[0:1, :] += jnp.sum(dm * n, axis=0, keepdims=True)
        G = dy3 * sz
        dz_ref[...] = (dy3 * y2 * dsz).astype(dz_ref.dtype)
        dde_ref[0:1, :] += jnp.sum(G * xs, axis=0, keepdims=True)
        prev = st_ref[0, 0]
        dsn = ds_scr[...]
        prev_b, dsn_b = prev.astype(BF16), dsn.astype(BF16)
        cp = _dot(cg, prev_b)
        ge_b = (G * eax).astype(BF16)
        d_c = _dot(ge_b, prev_b, NT)
        dprev = _dot(cg, ge_b, TN) + cd * dsn
        chan_a = G * cp * eax
        xw_b = (xdt * dsx).astype(BF16)
        dcd = jnp.sum(prev * dsn, axis=0, keepdims=True)
        d_b = _dot(xw_b, dsn_b, NT)
        dxw = _dot(bg, dsn_b)
        dd = dxw * xdt * dsx
        chan_a = chan_a - dd
        last_c = jnp.sum(dd, axis=0, keepdims=True) + dcd * cd
        sc = _dot(cg, bg, NT)
        head_row = _row(0, (LANES, 1))
        dsc = jnp.zeros((Q, Q), F32)
        dacs = jnp.zeros((Q, LANES), F32)
        colsums = jnp.zeros((LANES, Q), F32)
        for j in range(GC // LANES):
            ps = slice(j * LANES, (j + 1) * LANES)
            xp, gp = xdt[:, ps], G[:, ps]
            dxp = dxw[:, ps] * dsx[:, ps]
            for hh in range(2):
                h = 2 * j + hh
                lm = jnp.exp(jnp.where(li >= si, acs_ref[:, h:h + 1] - acst_ref[h:h + 1, :], -1e30))
                m = sc * lm
                xm = jnp.where(half == hh, xp, 0.0).astype(BF16)
                gm = jnp.where(half == hh, gp, 0.0).astype(BF16)
                dm = _dot(gm, xm, NT)
                dxp = dxp + _dot(m.astype(BF16), gm, TN)
                dsc = dsc + dm * lm
                w = dm * m
                dacs = dacs + jnp.where(lane == h, jnp.sum(w, axis=1, keepdims=True), 0.0)
                colsums = jnp.where(head_row == h, jnp.sum(w, axis=0, keepdims=True), colsums)
            dx_scr[:, ps] = dxp
        dacs = dacs - colsums.T
        dsc_b = dsc.astype(BF16)
        d_c = d_c + _dot(dsc_b, bg)
        d_b = d_b + _dot(dsc_b, cg, TN)
        ds_scr[...] = dprev
        dxdt = dx_scr[...]
        dxs_ref[...] = (dxdt * dtx + dexp_ref[...] * G) * xs_g
        db_ref[...] = d_b * b_g
        dc_ref[...] = d_c * c_g
        colm = col_ref[...]
        dacs = dacs + _dot_sel(chan_a, colm)
        dlast = _dot_sel(jnp.broadcast_to(last_c, (8, GC)), colm)[0:1, :]
        dacs = dacs + jnp.where(_row(0, (Q, 1)) == Q - 1, dlast, 0.0)
        dadt = _sel_dot((si >= li).astype(BF16), dacs)
        a = -jnp.exp(alog_ref[...])
        dt = dt_ref[...]
        ddt = dadt * a + _dot_sel(dxdt * xs, colm, 2)
        dal_ref[0:1, :] += jnp.sum(dadt * dt * a, axis=0, keepdims=True)
        draw = ddt * _sigmoid(raw_ref[...] + bias_ref[...])
        dbi_ref[0:1, :] += jnp.sum(draw, axis=0, keepdims=True)
        ddt_ref[...] = draw.astype(ddt_ref.dtype)

    acc = pl.BlockSpec((8, LANES), lambda g, c: (0, g))
    cacc = pl.BlockSpec((8, GC), lambda g, c: (0, g))
    return _call(
        body, grid=(GROUPS, nc),
        in_specs=[sp["xs"], sp["b"], sp["c"], sp["chan"], sp["chan"], sp["chan"], sp["cdx"], sp["head"], sp["headt"],
                  sp["head"], sp["head"], sp["hvec"], sp["hvec"], sp["state"], sp["chan"],
                  pl.BlockSpec((Q, GC), lambda g, c: (rc(c), 2 * GROUPS + g)),
                  pl.BlockSpec((Q, GC), lambda g, c: (rc(c), GROUPS + g)), sp["cvec"], sp["cvec"],
                  pl.BlockSpec((GC, LANES), lambda g, c: (0, 0))],
        out_specs=[sp["chan"],
                   pl.BlockSpec((Q, STATE), lambda g, c: (rc(c), g)),
                   pl.BlockSpec((Q, STATE), lambda g, c: (rc(c), g)),
                   sp["head"], acc, acc,
                   pl.BlockSpec((Q, GC), lambda g, c: (rc(c), 2 * GROUPS + g)), cacc, cacc],
        out_shape=[_sds((L, D), F32), _sds((L, GROUPS * STATE), F32), _sds((L, GROUPS * STATE), F32),
                   _sds((L, GROUPS * LANES), BF16), _sds((8, GROUPS * LANES), F32), _sds((8, GROUPS * LANES), F32),
                   _sds((L, proj.shape[1]), BF16), _sds((8, D), F32), _sds((8, D), F32)],
        scratch_shapes=[pltpu.VMEM((STATE, GC), F32), pltpu.VMEM((Q, GC), F32)], name="scan_bwd",
        sem=("parallel", "arbitrary"),
        args=(pre, pre, pre, dtx, eax, dsx, cdx, acs, acst, dt, dtraw, bias, alog, states, y, proj, dmix, nw, dexp,
              collapse),
        jobs=jobs)


def _adam_math(gv, w, m, v):
    c1 = 1.0 - ADAM_B1 ** ADAM_STEP
    c2 = 1.0 - ADAM_B2 ** ADAM_STEP
    nm = ADAM_B1 * m + (1.0 - ADAM_B1) * gv
    nv = ADAM_B2 * v + (1.0 - ADAM_B2) * (gv * gv)
    return -ADAM_LR * ((nm / c1) / (jnp.sqrt(nv / c2) + ADAM_EPS) + ADAM_WD * w), nm, nv


def adamw(g, w, m, v, name):
    R, C = g.shape
    T = R if R <= 128 else 128
    assert R % T == 0

    def body(g_ref, w_ref, m_ref, v_ref, d_ref, nm_ref, nv_ref):
        d_ref[...], nm_ref[...], nv_ref[...] = _adam_math(g_ref[...], w_ref[...], m_ref[...], v_ref[...])

    blk = pl.BlockSpec((T, C), lambda i: (i, 0))
    return pl.pallas_call(
        body, grid=(R // T,), in_specs=[blk] * 4, out_specs=[blk] * 3,
        out_shape=[_sds((R, C), F32)] * 3, name=name, compiler_params=_params("parallel"))(g, w, m, v)


def adamw_layer(g, w, m, v, layer, prev, name):
    R, C = g.shape
    T = _rows_tile(R, ROW_TILE)
    assert R % T == 0

    def body(g_ref, w_ref, m_ref, v_ref, *rest):
        go_ref, d_ref, nm_ref, nv_ref = rest[-4:]
        gv = g_ref[...]
        go_ref[0] = gv
        d_ref[0], nm_ref[0], nv_ref[0] = _adam_math(gv, w_ref[0], m_ref[0], v_ref[0])

    mine = pl.BlockSpec((1, T, C), lambda i: (layer, i, 0))
    prev = list(prev or [])
    return pl.pallas_call(
        body, grid=(R // T,), in_specs=[pl.BlockSpec((T, C), lambda i: (i, 0)), mine, mine, mine] + [_ANY] * len(prev),
        out_specs=[mine] * 4, out_shape=[_sds(w.shape, F32)] * 4,
        input_output_aliases={4 + i: i for i in range(len(prev))}, name=name,
        compiler_params=_params("parallel"))(g, w, m, v, *prev)


def cast_bf16(a, layer):
    _, R, C = a.shape
    T = _rows_tile(R, ROW_TILE)

    def body(a_ref, o_ref):
        o_ref[...] = a_ref[0].astype(BF16)

    return pl.pallas_call(
        body, grid=(R // T,), in_specs=[pl.BlockSpec((1, T, C), lambda i: (layer, i, 0))],
        out_specs=pl.BlockSpec((T, C), lambda i: (i, 0)), out_shape=_sds((R, C), BF16),
        name="cast_bf16", compiler_params=_params("parallel"))(a)


def _to_groups(v, hpg):
    lead = v.shape[:-1]
    t = v.reshape(lead + (GROUPS, hpg))
    t = jnp.pad(t, [(0, 0)] * (len(lead) + 1) + [(0, LANES - hpg)])
    return t.reshape(lead + (GROUPS * LANES,))


def _from_groups(a, hpg):
    lead = a.shape[:-1]
    return a.reshape(lead + (GROUPS, LANES))[..., :hpg].reshape(lead + (GROUPS * hpg,))


def _expand_matrix(D):
    gc = D // GROUPS
    return (jnp.arange(LANES)[:, None] == (jnp.arange(gc)[None, :] // HEAD_DIM)).astype(BF16)


def layer_params(pre_w, w_in_t, mixw, scale, cw, cb, bias, alog, dskip, nw, w_out_full, post_w, D):
    hpg = D // GROUPS // HEAD_DIM
    main = w_in_t.shape[0] - GROUPS * hpg
    wdt_t = _to_groups(w_in_t[main:].T, hpg).T
    return dict(
        pre_w=pre_w[None], win_t=w_in_t, main=main, wdt_t=wdt_t, mixw=mixw,
        scale=scale[None], cw=cw, cb=cb[None], bias=_to_groups(bias, hpg)[None], alog=_to_groups(alog, hpg)[None],
        dexp=jnp.repeat(dskip, HEAD_DIM)[None], nw=nw[None], wout=w_out_full, post_w=post_w[None])


def layer_fwd(x, p, D, next_shards=None, h=None, next_pre_w=None):
    nxt = bool(next_shards)
    parts = lambda a: ((a.shape[0] // 32) * 16, a.shape[0] - (a.shape[0] // 32) * 16)
    if nxt:
        s_in, s_out, s_mix, s_cw = next_shards
        (in_a, in_b), (out_a, out_b) = parts(s_in), parts(s_out)
    if h is None:
        h = rms_fwd(x, p["pre_w"])
    proj, got = matmul(h, p["win_t"], "nt", F32, "proj", tm=1024, n_out=p["main"],
                       jobs=[gather_rows_job(s_in, 0, in_a)] if nxt else [])
    g_in = got[0][0] if nxt else None
    dtraw, _ = matmul(h, p["wdt_t"], "nt", F32, "dtproj")
    mixed, pooled = pool_fwd(proj, p["mixw"], p["scale"], D)
    pre = conv_fwd(proj, p["cw"], p["cb"], D)
    dtp, got = dt_prep(dtraw, p["bias"], p["alog"], _expand_matrix(D), D,
                       jobs=[gather_rows_job(s_out, 0, out_a)] if nxt else [])
    g_out = got[0][0] if nxt else None
    dt, acs, acst, dtx, eax, dsx, cdx = dtp
    (y, states, mixed), got = scan_fwd(pre, dtx, eax, dsx, cdx, acs, acst, proj, p["dexp"], p["nw"], mixed, D,
                                       jobs=[gather_rows_job(s_in, in_a, in_b, g_in)] if nxt else [])
    g_in = got[0][0] if nxt else None
    out, got = matmul(mixed, p["wout"], "nn", F32, "outproj", tm=1024,
                      jobs=[gather_rows_job(s_out, out_a, out_b, g_out), gather_send_job([s_mix, s_cw]),
                            gather_pass_job([g_in], [True])] if nxt else [])
    (xn, h_next), got2 = post_fwd(x, out, p["post_w"], next_pre_w,
                                  jobs=[gather_pass_job(got[0] + got[1], [True, False, False])] if nxt else [])
    gathered = got[2] + got2[0] if nxt else None
    return xn, dict(x=x, h=h, proj=proj, dtraw=dtraw, pooled=pooled, pre=pre, dtp=dtp, y=y, states=states,
                    mixed=mixed, out=out), gathered, h_next


def layer_bwd(dxn, p, s, D, where=None, post=None, below=None):
    reduce = where is not None
    chip, core = where if reduce else (None, None)
    hpg = D // GROUPS // HEAD_DIM
    PGW = D // GROUPS
    main = p["main"]
    SH = (main + GROUPS * hpg) // 4
    dt, acs, acst, dtx, eax, dsx, cdx = s["dtp"]
    dout, d_post = post if post else post_bwd(s["out"], p["post_w"], dxn)
    part = BF16 if reduce else F32
    d_wout, _ = matmul(s["mixed"], dout, "tn", part, "dwout", tm=1024)
    g_out = d_wout.reshape(4, 2, D // 4, D)
    dmix, got = matmul(dout, p["wout"], "nt", F32, "dmixed", tm=1024,
                       jobs=[pair_exchange_job([g_out])] if reduce else [])
    pair_out = pair_add(g_out, got[0][0], core, BF16) if reduce else None
    (dxs, db, dc, ddtraw, d_alog, d_bias, dproj, d_nw, d_dexp), got = scan_bwd(
        s["pre"], dtx, eax, dsx, cdx, acs, acst, dt, s["dtraw"], p["bias"], p["alog"], s["states"], s["y"],
        s["proj"], dmix, p["nw"], p["dexp"], _expand_matrix(D).T, D,
        jobs=[chip_exchange_job([pair_out])] if reduce else [])
    mine_out = chip_add(pair_out, got[0][0], chip, core) if reduce else None
    (dproj, d_cw, d_cb), got = conv_bwd([dxs, db, dc], s["proj"], p["cw"], dproj, D,
                                        jobs=[pair_gather_job([mine_out])] if reduce else [])
    r_out = got[0][0] if reduce else None
    dproj, d_mixw, d_scale = pool_bwd(s["proj"], dmix, s["pooled"], p["mixw"], p["scale"], dproj, D)
    d_wmain_t, _ = matmul(dproj, s["h"], "tn", part, "dwmain", tm=1024)
    d_wdt_t, _ = matmul(ddtraw, s["h"], "tn", part, "dwdt")
    late = [d_wmain_t[None], d_wdt_t[None],
            d_mixw.reshape(GROUPS, 4, PGW // 4, PGW).transpose(1, 0, 2, 3).reshape(4, 2, GROUPS * PGW // 8, PGW)]
    cols = [True, True, False]
    dh_b, got = matmul(ddtraw, p["wdt_t"], "nn", F32, "dh_dt",
                       jobs=[pair_exchange_job(late, cols)] if reduce else [])
    if reduce:
        p_main, p_dt, p_mix = [pair_add(g, r, core, BF16, k, n) for g, r, k, n in
                               zip(late, got[0], cols, [4 * SH, None, None])]
        p_in = lax.dynamic_update_slice(p_main[0], _from_groups(p_dt[0].T, hpg).T, (main, 0))
        pairs = [p_in.reshape(4, SH, D // 2), p_mix]
    dh_a, got = matmul(dproj, p["win_t"], "nn", F32, "dh_main", jobs=[chip_exchange_job(pairs)] if reduce else [])
    mines = [chip_add(q, r, chip, core, k) for q, r, k in zip(pairs, got[0], [True, False])] if reduce else None
    (dx, d_pre, *post_below), got = rms_bwd(s["x"], p["pre_w"], dh_a, dh_b, dxn, below,
                                            jobs=[pair_gather_job(mines, [True, False])] if reduce else [])
    reduced = dict(w_in=got[0][0], w_out=r_out, pool_mix_w=got[0][1]) if reduce else None
    grads = dict(
        pre_norm_w=d_pre[0], pool_scale=d_scale[0], conv_w=d_cw[:CONV_K], conv_b=d_cb[0],
        dt_bias=_from_groups(d_bias[0], hpg), a_log=_from_groups(d_alog[0], hpg),
        d_skip=d_dexp[0].reshape(-1, HEAD_DIM).sum(axis=-1), ssd_norm_w=d_nw[0], post_norm_w=d_post[0])
    if not reduce:
        grads.update(w_in=jnp.concatenate([d_wmain_t.T, _from_groups(d_wdt_t.T, hpg)], axis=1), pool_mix_w=d_mixw,
                     w_out=d_wout)
    return dx, grads, reduced, tuple(post_below) or None


def local_step(x, target, params, D):
    saved, h, n = [], None, len(params)
    for l, p in enumerate(params):
        x, s, _, h = layer_fwd(x, p, D, h=h, next_pre_w=params[l + 1]["pre_w"] if l + 1 < n else None)
        saved.append(s)
    dx, sumsq = loss_head(x, target)
    grads, post = [None] * n, None
    for l in reversed(range(n)):
        below = (saved[l - 1]["out"], params[l - 1]["post_w"]) if l else None
        dx, grads[l], _, post = layer_bwd(dx, params[l], saved[l], D, post=post, below=below)
    return sumsq, dx, grads


SMALL = ("pre_norm_w", "pool_scale", "conv_w", "conv_b", "dt_bias", "a_log", "d_skip", "ssd_norm_w", "post_norm_w")
BIG = ("w_in", "w_out", "pool_mix_w")


def _pack(parts):
    flat = jnp.concatenate([p.reshape(-1) for p in parts])
    n = flat.shape[0]
    rows = -(-n // (LANES * LANES)) * LANES
    return jnp.pad(flat, (0, rows * LANES - n)).reshape(rows, LANES)


def _unpack(packed, shapes):
    flat, out, at = packed.reshape(-1), [], 0
    for s in shapes:
        n = math.prod(s)
        out.append(flat[at:at + n].reshape(s))
        at += n
    return out


def kernel(x, pre_norm_w, w_in, pool_mix_w, pool_scale, conv_w, conv_b, dt_bias, a_log, d_skip, ssd_norm_w, w_out, post_norm_w, loss_target, m_pre_norm_w, m_w_in, m_pool_mix_w, m_pool_scale, m_conv_w, m_conv_b, m_dt_bias, m_a_log, m_d_skip, m_ssd_norm_w, m_w_out, m_post_norm_w, v_pre_norm_w, v_w_in, v_pool_mix_w, v_pool_scale, v_conv_w, v_conv_b, v_dt_bias, v_a_log, v_d_skip, v_ssd_norm_w, v_w_out, v_post_norm_w):
    NL, D, SH = w_in.shape
    PGW = D // GROUPS
    CS = conv_w.shape[2]
    chip = (2 * lax.axis_index("x") + lax.axis_index("y")).astype(jnp.int32)
    chip1, core = chip.reshape(1), lax.axis_index("c").astype(jnp.int32).reshape(1)

    tr = lambda t: jnp.transpose(t, (0, 2, 1))
    w_in_t, m_w_in_t, v_w_in_t = tr(w_in), tr(m_w_in), tr(v_w_in)
    halved_by_cols = [True, True, False, False]

    def shards(l):
        return [cast_bf16(w_in_t, l), cast_bf16(w_out, l),
                cast_bf16(pool_mix_w.reshape(NL, GROUPS * PGW // 4, PGW), l).reshape(2, GROUPS * PGW // 8, PGW),
                conv_w[l].reshape(2, CONV_K * CS // (2 * LANES), LANES)]

    def params(l, g):
        g_in, g_out, g_mix, g_cw = g
        mix_l = g_mix.reshape(4, GROUPS, PGW // 4, PGW).transpose(1, 0, 2, 3).reshape(GROUPS, PGW, PGW)
        cw_l = g_cw.reshape(4, CONV_K, CS).transpose(1, 0, 2).reshape(CONV_K, 4 * CS)
        return layer_params(pre_norm_w[l], g_in.reshape(4 * SH, D), mix_l, pool_scale[l], cw_l, conv_b[l], dt_bias[l],
                            a_log[l], d_skip[l], ssd_norm_w[l], g_out.reshape(2 * D, D), post_norm_w[l], D)

    gathered = run_jobs([gather_send_job(shards(0), halved_by_cols)], "gather_send")[0]
    gathered = run_jobs([gather_pass_job(gathered, halved_by_cols)], "gather_pass")[0]
    xl, h, ps, saved = x[0], None, [], []
    for l in range(NL):
        ps.append(params(l, gathered))
        last = l + 1 == NL
        xl, s, gathered, h = layer_fwd(xl, ps[l], D, None if last else shards(l + 1), h,
                                       None if last else pre_norm_w[l + 1][None])
        saved.append(s)
    dx, sumsq = loss_head(xl, loss_target[0])

    given = dict(w_in=(w_in_t, m_w_in_t, v_w_in_t), w_out=(w_out, m_w_out, v_w_out),
                 pool_mix_w=(pool_mix_w, m_pool_mix_w, v_pool_mix_w))
    flat = {n: [t.reshape(NL, -1, t.shape[-1]) for t in given[n]] for n in BIG}
    done = {n: None for n in BIG}
    grads, post = [None] * NL, None
    for l in reversed(range(NL)):
        below = (saved[l - 1]["out"], ps[l - 1]["post_w"]) if l else None
        dx, grads[l], reduced, post = layer_bwd(dx, ps[l], saved[l], D, (chip1, core), post, below)
        for n in BIG:
            r = reduced[n]
            done[n] = adamw_layer(r.reshape(-1, r.shape[-1]), *flat[n], l, done[n], "adamw_" + n)

    small_shapes = [(NL,) + grads[0][n].shape for n in SMALL]
    packed = _pack([0.5 / D * sumsq[0, :1]] + [jnp.stack([g[n] for g in grads]) for n in SMALL])
    total = allreduce_small(packed)
    loss, *small = _unpack(total, [(1,)] + small_shapes)
    small = dict(zip(SMALL, small))
    small["conv_w"] = lax.dynamic_slice_in_dim(small["conv_w"], chip * CS, CS, axis=2)

    given_small = dict(
        pre_norm_w=(pre_norm_w, m_pre_norm_w, v_pre_norm_w), pool_scale=(pool_scale, m_pool_scale, v_pool_scale),
        conv_w=(conv_w, m_conv_w, v_conv_w), conv_b=(conv_b, m_conv_b, v_conv_b),
        dt_bias=(dt_bias, m_dt_bias, v_dt_bias), a_log=(a_log, m_a_log, v_a_log),
        d_skip=(d_skip, m_d_skip, v_d_skip), ssd_norm_w=(ssd_norm_w, m_ssd_norm_w, v_ssd_norm_w),
        post_norm_w=(post_norm_w, m_post_norm_w, v_post_norm_w))
    shapes = [given_small[n][0].shape for n in SMALL]
    upd = adamw(_pack([small[n] for n in SMALL]), *[_pack([given_small[n][i] for n in SMALL]) for i in range(3)],
                "adamw_small")
    upd = [dict(zip(SMALL, _unpack(u, shapes))) for u in upd]

    out = {n: (small[n], upd[0][n], upd[1][n], upd[2][n]) for n in SMALL}
    for n in BIG:
        out[n] = tuple(t.reshape(given[n][0].shape) for t in done[n])
    out["w_in"] = tuple(tr(t) for t in out["w_in"])

    order = ("pre_norm_w", "w_in", "pool_mix_w", "pool_scale", "conv_w", "conv_b", "dt_bias", "a_log", "d_skip",
             "ssd_norm_w", "w_out", "post_norm_w")
    return (loss.reshape(()), dx[None], *[out[n][0] for n in order], *[out[n][1] for n in order],
            *[out[n][2] for n in order], *[out[n][3] for n in order])
```

```python
import functools
import math

import jax
import jax.numpy as jnp
from jax import lax
from jax.experimental import pallas as pl
from jax.experimental.pallas import tpu as pltpu

F32 = jnp.float32
BF16 = jnp.bfloat16

NORM_EPS = 1e-6
HEAD_DIM = 64
STATE = 128
GROUPS = 4
POOL_WINDOWS = (2, 4, 8, 16)
POOL_HALO = 16
CONV_K = 4
CONV_HALO = 8
SCAN_CHUNK = 256
LANES = 128
VMEM_LIMIT = 52 * 1024 * 1024
ROW_TILE = 256

ADAM_LR = 0.001
ADAM_B1 = 0.9
ADAM_B2 = 0.999
ADAM_EPS = 1e-08
ADAM_WD = 0.01
ADAM_STEP = 10

MESH = pl.DeviceIdType.MESH

NN = (((1,), (0,)), ((), ()))
NT = (((1,), (1,)), ((), ()))
TN = (((0,), (0,)), ((), ()))

_ANY = pl.BlockSpec(memory_space=pl.ANY)


def _params(*sem):
    return pltpu.CompilerParams(dimension_semantics=sem, vmem_limit_bytes=VMEM_LIMIT)


def _pick(dim, pref):
    if dim <= pref:
        return dim
    t = (pref // LANES) * LANES
    while t > LANES and dim % t:
        t -= LANES
    assert dim % t == 0, (dim, pref)
    return t


def _rows_tile(rows, pref):
    t = (min(pref, rows) // 8) * 8
    while t >= 8 and rows % t:
        t -= 8
    return t if t >= 8 else rows


def _dot(a, b, dn=NN):
    return lax.dot_general(a, b, dn, preferred_element_type=F32)


def _split3(a):
    hi = a.astype(BF16)
    r = a - hi.astype(F32)
    mid = r.astype(BF16)
    return hi, mid, (r - mid.astype(F32)).astype(BF16)


def _dot_sel(a, e, parts=3):
    hi, mid, lo = _split3(a)
    return (_dot(lo, e) + _dot(mid, e)) + _dot(hi, e) if parts == 3 else _dot(mid, e) + _dot(hi, e)


def _sel_dot(e, b):
    hi, mid, lo = _split3(b)
    return (_dot(e, lo) + _dot(e, mid)) + _dot(e, hi)


def _sigmoid(v):
    return 0.5 * jnp.tanh(0.5 * v) + 0.5


def _silu_and_grad(v):
    s = _sigmoid(v)
    return v * s, s * (1.0 + v * (1.0 - s))


def _row(i, shape):
    return lax.broadcasted_iota(jnp.int32, shape, 0) + i


def _sds(shape, dtype):
    return jax.ShapeDtypeStruct(tuple(shape), dtype)


class Job:
    def __init__(self, ins, outs, aliased, nsem, start, finish):
        self.ins, self.outs, self.aliased, self.nsem, self.start, self.finish = ins, outs, aliased, nsem, start, finish


def _place():
    x, y, c = lax.axis_index("x"), lax.axis_index("y"), lax.axis_index("c")
    return x, y, c, [(1 - x, y), (x, 1 - y), (1 - x, 1 - y)]


def _remote(src, dst, send_sem, recv_sem, device):
    return pltpu.make_async_remote_copy(src_ref=src, dst_ref=dst, send_sem=send_sem, recv_sem=recv_sem,
                                        device_id=device, device_id_type=MESH)


def _half(ref, c, cols, lead=0):
    idx = [slice(None)] * lead
    if cols:
        w = ref.shape[-1] // 2
        idx += [slice(None)] * (len(ref.shape) - lead - 1) + [pl.ds(pl.multiple_of(c * w, LANES), w)]
    else:
        idx += [c]
    return ref.at[tuple(idx)]


def _flags(cols, n):
    return list(cols) if cols else [False] * n


def gather_send_job(arrs, cols=None):
    n = len(arrs)
    cols = _flags(cols, n)

    def copies(ins, outs, send, recv):
        x, y, c, chips = _place()
        mine = 2 * x + y
        out = []
        for a in range(n):
            out.append(_remote(ins[a], outs[a].at[mine], send.at[4 * a + 3], recv.at[4 * a + 3], (x, y, 1 - c)))
            for j, chip in enumerate(chips):
                out.append(_remote(_half(ins[a], c, cols[a]), _half(outs[a].at[mine], c, cols[a]),
                                   send.at[4 * a + j], recv.at[4 * a + j], (*chip, c)))
        return out

    def start(ins, outs, send, recv):
        for cp in copies(ins, outs, send, recv):
            cp.start()

    def finish(ins, outs, send, recv):
        x, y, c, chips = _place()
        for a in range(n):
            for j, chip in enumerate(chips):
                landed = _half(outs[a].at[2 * chip[0] + chip[1]], c, cols[a])
                _remote(landed, landed, send.at[4 * a + j], recv.at[4 * a + j], (x, y, 1 - c)).wait_recv()
            twin = outs[a].at[2 * x + y]
            _remote(twin, twin, send.at[4 * a + 3], recv.at[4 * a + 3], (x, y, 1 - c)).wait_recv()
        for cp in copies(ins, outs, send, recv):
            cp.wait_send()

    return Job(list(arrs), [_sds((4,) + a.shape, a.dtype) for a in arrs], False, 4 * n, start, finish)


def gather_rows_job(shard, start, size, into=None):
    rows = pl.ds(start, size)

    def copies(ins, outs, send, recv):
        x, y, c, chips = _place()
        mine = outs[0].at[2 * x + y, rows]
        out = [_remote(ins[0].at[rows], mine, send.at[3], recv.at[3], (x, y, 1 - c))]
        for j, chip in enumerate(chips):
            out.append(_remote(_half(ins[0].at[rows], c, True), _half(mine, c, True), send.at[j], recv.at[j],
                               (*chip, c)))
        return out

    def start_(ins, outs, send, recv):
        for cp in copies(ins, outs, send, recv):
            cp.start()

    def finish(ins, outs, send, recv):
        x, y, c, chips = _place()
        for j, chip in enumerate(chips):
            landed = _half(outs[0].at[2 * chip[0] + chip[1], rows], c, True)
            _remote(landed, landed, send.at[j], recv.at[j], (x, y, 1 - c)).wait_recv()
        twin = outs[0].at[2 * x + y, rows]
        _remote(twin, twin, send.at[3], recv.at[3], (x, y, 1 - c)).wait_recv()
        for cp in copies(ins, outs, send, recv):
            cp.wait_send()

    ins = [shard] if into is None else [shard, into]
    return Job(ins, [_sds((4,) + shard.shape, shard.dtype)], {} if into is None else {1: 0}, 4, start_, finish)


def gather_pass_job(bufs, cols=None):
    n = len(bufs)
    cols = _flags(cols, n)

    def copies(outs, send, recv):
        x, y, c, chips = _place()
        out = []
        for a in range(n):
            for j, chip in enumerate(chips):
                landed = _half(outs[a].at[2 * chip[0] + chip[1]], c, cols[a])
                out.append(_remote(landed, landed, send.at[3 * a + j], recv.at[3 * a + j], (x, y, 1 - c)))
        return out

    def start(ins, outs, send, recv):
        for cp in copies(outs, send, recv):
            cp.start()

    def finish(ins, outs, send, recv):
        x, y, c, chips = _place()
        for a in range(n):
            for j, chip in enumerate(chips):
                passed = _half(outs[a].at[2 * chip[0] + chip[1]], 1 - c, cols[a])
                _remote(passed, passed, send.at[3 * a + j], recv.at[3 * a + j], (x, y, 1 - c)).wait_recv()
        for cp in copies(outs, send, recv):
            cp.wait_send()

    return Job(list(bufs), [_sds(b.shape, b.dtype) for b in bufs], True, 3 * n, start, finish)


def pair_exchange_job(arrs, cols=None):
    n = len(arrs)
    cols = _flags(cols, n)

    def copies(ins, outs, send, recv):
        x, y, c, _ = _place()
        return [_remote(_half(ins[a], 1 - c, cols[a], 1), outs[a], send.at[a], recv.at[a], (x, y, 1 - c))
                for a in range(n)]

    def start(ins, outs, send, recv):
        for cp in copies(ins, outs, send, recv):
            cp.start()

    def finish(ins, outs, send, recv):
        for cp in copies(ins, outs, send, recv):
            cp.wait()

    shape = lambda a, k: a.shape[:-1] + (a.shape[-1] // 2,) if k else a.shape[:1] + a.shape[2:]
    return Job(list(arrs), [_sds(shape(a, k), a.dtype) for a, k in zip(arrs, cols)], False, n, start, finish)


def chip_exchange_job(arrs):
    n = len(arrs)

    def copies(ins, outs, send, recv):
        x, y, c, chips = _place()
        return [_remote(ins[a].at[2 * chip[0] + chip[1]], outs[a].at[j], send.at[3 * a + j], recv.at[3 * a + j],
                        (*chip, c)) for a in range(n) for j, chip in enumerate(chips)]

    def start(ins, outs, send, recv):
        for cp in copies(ins, outs, send, recv):
            cp.start()

    def finish(ins, outs, send, recv):
        for cp in copies(ins, outs, send, recv):
            cp.wait()

    return Job(list(arrs), [_sds((3,) + a.shape[1:], a.dtype) for a in arrs], False, 3 * n, start, finish)


def pair_gather_job(bufs, cols=None):
    n = len(bufs)
    cols = _flags(cols, n)

    def copies(outs, send, recv):
        x, y, c, _ = _place()
        return [_remote(_half(outs[a], c, cols[a]), _half(outs[a], c, cols[a]), send.at[a], recv.at[a],
                        (x, y, 1 - c)) for a in range(n)]

    def start(ins, outs, send, recv):
        for cp in copies(outs, send, recv):
            cp.start()

    def finish(ins, outs, send, recv):
        for cp in copies(outs, send, recv):
            cp.wait()

    return Job(list(bufs), [_sds(b.shape, b.dtype) for b in bufs], True, n, start, finish)


def _call(body, *, grid, in_specs, out_specs, out_shape, name, sem, args, scratch_shapes=(), jobs=(), aliases=None):
    in_specs, out_specs, out_shape, scratch_shapes = list(in_specs), list(out_specs), list(out_shape), list(scratch_shapes)
    aliases = dict(aliases or {})
    n_in, n_out, n_scr = len(in_specs), len(out_specs), len(scratch_shapes)
    if jobs:
        sem = ("arbitrary",) * len(grid)
    at_in, at_out = n_in, n_out
    for j in jobs:
        pairs = j.aliased if isinstance(j.aliased, dict) else {i: i for i in range(len(j.ins))} if j.aliased else {}
        aliases.update({at_in + i: at_out + o for i, o in pairs.items()})
        at_in, at_out = at_in + len(j.ins), at_out + len(j.outs)

    def wrapped(*refs):
        ins, p = refs[:n_in], n_in
        jins = []
        for j in jobs:
            jins.append(refs[p:p + len(j.ins)])
            p += len(j.ins)
        outs, p = refs[p:p + n_out], p + n_out
        jouts = []
        for j in jobs:
            jouts.append(refs[p:p + len(j.outs)])
            p += len(j.outs)
        scr, sems = refs[p:p + n_scr], refs[p + n_scr:]

        def start():
            for k, j in enumerate(jobs):
                j.start(jins[k], jouts[k], sems[2 * k], sems[2 * k + 1])

        def finish():
            for k, j in enumerate(jobs):
                j.finish(jins[k], jouts[k], sems[2 * k], sems[2 * k + 1])

        if jobs and grid:
            ids = [pl.program_id(d) for d in range(len(grid))]
            pl.when(functools.reduce(jnp.logical_and, [i == 0 for i in ids]))(start)
            body(*ins, *outs, *scr)
            pl.when(functools.reduce(jnp.logical_and, [i == g - 1 for i, g in zip(ids, grid)]))(finish)
        else:
            start()
            body(*ins, *outs, *scr)
            finish()

    kwargs = dict(grid=grid) if grid else {}
    res = pl.pallas_call(
        wrapped, in_specs=in_specs + [_ANY] * (at_in - n_in), out_specs=out_specs + [_ANY] * (at_out - n_out),
        out_shape=out_shape + [o for j in jobs for o in j.outs],
        scratch_shapes=scratch_shapes + [pltpu.SemaphoreType.DMA((j.nsem,)) for j in jobs for _ in range(2)],
        input_output_aliases=aliases, name=name,
        compiler_params=pltpu.CompilerParams(dimension_semantics=sem, vmem_limit_bytes=VMEM_LIMIT) if grid
        else pltpu.CompilerParams(vmem_limit_bytes=VMEM_LIMIT), **kwargs)(*args, *[a for j in jobs for a in j.ins])
    res = list(res)
    outs, rest, per_job = res[:n_out], res[n_out:], []
    for j in jobs:
        per_job.append(rest[:len(j.outs)])
        rest = rest[len(j.outs):]
    return outs, per_job


def run_jobs(jobs, name):
    return _call(lambda: None, grid=(), in_specs=[], out_specs=[], out_shape=[], name=name, sem=(), args=(), jobs=jobs)[1]


def pair_add(g, r, core, out_dtype, cols=False, out_rows=None):
    S, R, C = r.shape
    T = _rows_tile(R, 2 * ROW_TILE if C <= 1024 else ROW_TILE if C <= 4096 else ROW_TILE // 2)

    def body(c_ref, g_ref, r_ref, o_ref):
        o_ref[0] = ((g_ref[0] if cols else g_ref[0, 0]).astype(F32) + r_ref[0].astype(F32)).astype(o_ref.dtype)

    g_spec = (pl.BlockSpec((1, T, C), lambda k, i, c_ref: (k, i, c_ref[0])) if cols
              else pl.BlockSpec((1, 1, T, C), lambda k, i, c_ref: (k, c_ref[0], i, 0)))
    return pl.pallas_call(
        body,
        grid_spec=pltpu.PrefetchScalarGridSpec(
            num_scalar_prefetch=1, grid=(S, R // T),
            in_specs=[g_spec, pl.BlockSpec((1, T, C), lambda k, i, c_ref: (k, i, 0))],
            out_specs=pl.BlockSpec((1, T, C), lambda k, i, c_ref: (k, i, 0))),
        out_shape=_sds((S, out_rows or R, C), out_dtype), name="pair_add",
        compiler_params=_params("parallel", "parallel"))(core, g, r)


def chip_add(p, r, chip, core, cols=False):
    _, R, C = p.shape
    T = _rows_tile(R, ROW_TILE)

    def body(k_ref, c_ref, p_ref, r0_ref, r1_ref, r2_ref, o_ref):
        s = ((p_ref[0].astype(F32) + r0_ref[0].astype(F32)) + r1_ref[0].astype(F32)) + r2_ref[0].astype(F32)
        if cols:
            o_ref[...] = s
        else:
            o_ref[0] = s

    slot = lambda j: pl.BlockSpec((1, T, C), lambda i, k_ref, c_ref: (j, i, 0))
    out_spec = (pl.BlockSpec((T, C), lambda i, k_ref, c_ref: (i, c_ref[0])) if cols
                else pl.BlockSpec((1, T, C), lambda i, k_ref, c_ref: (c_ref[0], i, 0)))
    return pl.pallas_call(
        body,
        grid_spec=pltpu.PrefetchScalarGridSpec(
            num_scalar_prefetch=2, grid=(R // T,),
            in_specs=[pl.BlockSpec((1, T, C), lambda i, k_ref, c_ref: (k_ref[0], i, 0)), slot(0), slot(1), slot(2)],
            out_specs=out_spec),
        out_shape=_sds((R, 2 * C) if cols else (2, R, C), F32), name="chip_add",
        compiler_params=_params("parallel"))(chip, core, p, r, r, r)


def allreduce_small(v):
    R = v.shape[0]

    def body(v_ref, o_ref, buf, send_sems, recv_sems, local_sem):
        x, y, c, chips = _place()
        me, sibling = (x, y, c), (x, y, 1 - c)

        def rows(px, py, pc):
            return buf.at[pl.ds((4 * px + 2 * py + pc) * R, R), :]

        def copy(k, block, to, src=None):
            return _remote(rows(*block) if src is None else src, rows(*block), send_sems.at[k], recv_sems.at[k], to)

        mine = pltpu.make_async_copy(v_ref, rows(*me), local_sem)
        mine.start()
        first = [copy(0, me, sibling, src=v_ref)]
        first += [copy(1 + j, me, (*chip, c), src=v_ref) for j, chip in enumerate(chips)]
        for cp in first:
            cp.start()
        passed = [copy(4 + j, (*chip, c), sibling) for j, chip in enumerate(chips)]
        for j, chip in enumerate(chips):
            copy(1 + j, (*chip, c), me).wait_recv()
            passed[j].start()
        copy(0, sibling, me).wait_recv()
        for j, chip in enumerate(chips):
            copy(4 + j, (*chip, 1 - c), me).wait_recv()
        for cp in first + passed:
            cp.wait_send()
        mine.wait()
        acc = buf[0:R, :]
        for d in range(1, 8):
            acc = acc + buf[d * R:(d + 1) * R, :]
        o_ref[...] = acc

    return pl.pallas_call(
        body, in_specs=[pl.BlockSpec(memory_space=pltpu.VMEM)], out_specs=pl.BlockSpec(memory_space=pltpu.VMEM),
        out_shape=_sds((R, LANES), F32),
        scratch_shapes=[pltpu.VMEM((8 * R, LANES), F32), pltpu.SemaphoreType.DMA((7,)),
                        pltpu.SemaphoreType.DMA((7,)), pltpu.SemaphoreType.DMA],
        name="allreduce_small", compiler_params=pltpu.CompilerParams(vmem_limit_bytes=VMEM_LIMIT))(v)


def matmul(a, b, mode, out_dtype, name, tm=512, tn=1024, tk=4608, jobs=(), n_out=None):
    if mode == "nn":
        (M, K), (K2, N) = a.shape, b.shape
    elif mode == "nt":
        (M, K), (N, K2) = a.shape, b.shape
        N = n_out or N
    else:
        (K, M), (K2, N) = a.shape, b.shape
    assert K == K2 or (mode == "nn" and K2 > K)
    tm, tn, tk = _pick(M, tm), _pick(N, tn), _pick(K, tk)
    nk = K // tk
    dn = {"nn": NN, "nt": NT, "tn": TN}[mode]

    def body(a_ref, b_ref, o_ref, *acc):
        part = _dot(a_ref[...].astype(BF16), b_ref[...].astype(BF16), dn)
        if nk == 1:
            o_ref[...] = part.astype(o_ref.dtype)
            return
        acc_ref, = acc
        k = pl.program_id(2)

        @pl.when(k == 0)
        def _():
            acc_ref[...] = part

        @pl.when(jnp.logical_and(k > 0, k < nk - 1))
        def _():
            acc_ref[...] += part

        @pl.when(k == nk - 1)
        def _():
            o_ref[...] = (acc_ref[...] + part).astype(o_ref.dtype)

    a_spec = (pl.BlockSpec((tk, tm), lambda i, j, k: (k, i)) if mode == "tn"
              else pl.BlockSpec((tm, tk), lambda i, j, k: (i, k)))
    b_spec = (pl.BlockSpec((tn, tk), lambda i, j, k: (j, k)) if mode == "nt"
              else pl.BlockSpec((tk, tn), lambda i, j, k: (k, j)))
    outs, per_job = _call(
        body, grid=(M // tm, N // tn, nk), in_specs=[a_spec, b_spec],
        out_specs=[pl.BlockSpec((tm, tn), lambda i, j, k: (i, j))], out_shape=[_sds((M, N), out_dtype)],
        scratch_shapes=[pltpu.VMEM((tm, tn), F32)] if nk > 1 else [], name=name,
        sem=("parallel", "parallel", "arbitrary"), args=(a, b), jobs=jobs)
    return outs[0], per_job


def rms_fwd(x, w):
    L, D = x.shape
    T = _pick(L, ROW_TILE)

    def body(x_ref, w_ref, h_ref):
        xv = x_ref[...]
        r = lax.rsqrt(jnp.mean(xv * xv, axis=-1, keepdims=True) + NORM_EPS)
        h_ref[...] = (xv * r * w_ref[...]).astype(h_ref.dtype)

    return pl.pallas_call(
        body, grid=(L // T,),
        in_specs=[pl.BlockSpec((T, D), lambda i: (i, 0)), pl.BlockSpec((1, D), lambda i: (0, 0))],
        out_specs=pl.BlockSpec((T, D), lambda i: (i, 0)),
        out_shape=_sds((L, D), BF16), name="rms_fwd", compiler_params=_params("parallel"))(x, w)


def post_fwd(x, o, w, next_w=None, jobs=()):
    L, D = x.shape
    T = _pick(L, ROW_TILE)

    def body(x_ref, o_ref, w_ref, *rest):
        ov = o_ref[...]
        r = lax.rsqrt(jnp.mean(ov * ov, axis=-1, keepdims=True) + NORM_EPS)
        y = x_ref[...] + ov * r * w_ref[...]
        rest[-1 if next_w is None else -2][...] = y
        if next_w is not None:
            r2 = lax.rsqrt(jnp.mean(y * y, axis=-1, keepdims=True) + NORM_EPS)
            rest[-1][...] = (y * r2 * rest[0][...]).astype(BF16)

    row = pl.BlockSpec((T, D), lambda i: (i, 0))
    vec = pl.BlockSpec((1, D), lambda i: (0, 0))
    more = [] if next_w is None else [next_w]
    outs, per_job = _call(
        body, grid=(L // T,), in_specs=[row, row, vec] + [vec] * len(more), out_specs=[row] * (1 + len(more)),
        out_shape=[_sds((L, D), F32)] + [_sds((L, D), BF16)] * len(more), name="post_fwd", sem=("parallel",),
        args=(x, o, w, *more), jobs=jobs)
    return (outs[0], outs[1] if more else None), per_job


def _rms_bwd_math(xv, w, dy):
    r = lax.rsqrt(jnp.mean(xv * xv, axis=-1, keepdims=True) + NORM_EPS)
    xhat = xv * r
    g = dy * w
    dx = r * (g - xhat * jnp.mean(g * xhat, axis=-1, keepdims=True))
    return dx, jnp.sum(dy * xhat, axis=0, keepdims=True)


def post_bwd(o, w, dxn):
    L, D = o.shape
    T = _pick(L, ROW_TILE)

    def body(o_ref, w_ref, d_ref, do_ref, dw_ref):
        dx, dw = _rms_bwd_math(o_ref[...], w_ref[...], d_ref[...])
        do_ref[...] = dx.astype(do_ref.dtype)

        @pl.when(pl.program_id(0) == 0)
        def _():
            dw_ref[...] = jnp.zeros_like(dw_ref)

        dw_ref[...] += dw

    row = pl.BlockSpec((T, D), lambda i: (i, 0))
    vec = pl.BlockSpec((1, D), lambda i: (0, 0))
    return pl.pallas_call(
        body, grid=(L // T,), in_specs=[row, vec, row], out_specs=[row, vec],
        out_shape=[_sds((L, D), BF16), _sds((1, D), F32)],
        name="post_bwd", compiler_params=_params("arbitrary"))(o, w, dxn)


def rms_bwd(x, w, dh_a, dh_b, dxn, below=None, jobs=()):
    L, D = x.shape
    T = _pick(L, ROW_TILE)

    def body(x_ref, w_ref, a_ref, b_ref, d_ref, *rest):
        dx, dw = _rms_bwd_math(x_ref[...], w_ref[...], a_ref[...] + b_ref[...])
        dx = d_ref[...] + dx
        outs = rest[2:] if below else rest
        outs[0][...] = dx

        @pl.when(pl.program_id(0) == 0)
        def _():
            for acc in outs[1::2]:
                acc[...] = jnp.zeros_like(acc)

        outs[1][...] += dw
        if below:
            do, dwp = _rms_bwd_math(rest[0][...], rest[1][...], dx)
            outs[2][...] = do.astype(BF16)
            outs[3][...] += dwp

    row = pl.BlockSpec((T, D), lambda i: (i, 0))
    vec = pl.BlockSpec((1, D), lambda i: (0, 0))
    more = list(below) if below else []
    return _call(
        body, grid=(L // T,), in_specs=[row, vec, row, row, row] + ([row, vec] if below else []),
        out_specs=[row, vec] + ([row, vec] if below else []),
        out_shape=[_sds((L, D), F32), _sds((1, D), F32)] + ([_sds((L, D), BF16), _sds((1, D), F32)] if below else []),
        name="rms_bwd", sem=("arbitrary",), args=(x, w, dh_a, dh_b, dxn, *more), jobs=jobs)


def loss_head(y, target):
    L, D = y.shape
    T = _pick(L, ROW_TILE)

    def body(y_ref, t_ref, d_ref, s_ref):
        e = y_ref[...] - t_ref[...]
        d_ref[...] = e * (1.0 / D)

        @pl.when(pl.program_id(0) == 0)
        def _():
            s_ref[...] = jnp.zeros_like(s_ref)

        s_ref[...] += jnp.sum(e * e)

    row = pl.BlockSpec((T, D), lambda i: (i, 0))
    return pl.pallas_call(
        body, grid=(L // T,), in_specs=[row, row],
        out_specs=[row, pl.BlockSpec((8, LANES), lambda i: (0, 0))],
        out_shape=[_sds((L, D), F32), _sds((8, LANES), F32)],
        name="loss_head", compiler_params=_params("arbitrary"))(y, target)


def _window_sums(xe, w, back):
    n = xe.shape[0]
    s, k = xe, 1
    while k < w:
        s = s + pltpu.roll(s, k if back else n - k, 0)
        k *= 2
    return s


def pool_fwd(proj, mixw, scale, D):
    L = proj.shape[0]
    PGW = D // GROUPS
    T = _pick(L, ROW_TILE)
    hb = T // POOL_HALO

    def body(u_ref, halo_ref, g_ref, mw_ref, sc_ref, y_ref, p_ref):
        i = pl.program_id(0)
        u = u_ref[...]
        halo = jnp.where(i > 0, halo_ref[...], 0.0)
        xe = jnp.concatenate([halo, u], axis=0)
        t1 = _row(i * T + 1, (T, 1))
        for g, w in enumerate(POOL_WINDOWS):
            sl = slice(g * PGW, (g + 1) * PGW)
            win = _window_sums(xe[:, sl], w, True)[POOL_HALO:, :]
            cnt = jnp.minimum(t1, w).astype(F32)
            pooled = (win / cnt - u[:, sl]).astype(BF16)
            p_ref[:, sl] = pooled
            mixed = _dot(pooled, mw_ref[g])
            gate = g_ref[:, sl]
            y_ref[:, sl] = (mixed * sc_ref[:, sl] * (gate * _sigmoid(gate))).astype(BF16)

    return pl.pallas_call(
        body, grid=(L // T,),
        in_specs=[pl.BlockSpec((T, D), lambda i: (i, 0)),
                  pl.BlockSpec((POOL_HALO, D), lambda i: (jnp.maximum(i * hb - 1, 0), 0)),
                  pl.BlockSpec((T, D), lambda i: (i, 1)),
                  pl.BlockSpec((GROUPS, PGW, PGW), lambda i: (0, 0, 0)),
                  pl.BlockSpec((1, D), lambda i: (0, 0))],
        out_specs=[pl.BlockSpec((T, D), lambda i: (i, 0)), pl.BlockSpec((T, D), lambda i: (i, 0))],
        out_shape=[_sds((L, 2 * D), BF16), _sds((L, D), BF16)],
        name="pool_fwd", compiler_params=_params("parallel"))(proj, proj, proj, mixw, scale)


def pool_bwd(proj, dmix, pooled, mixw, scale, dproj, D):
    L = proj.shape[0]
    PGW = D // GROUPS
    T = _pick(L, ROW_TILE)
    hb = T // POOL_HALO
    nT = L // T

    def body(g_ref, gh_ref, dy_ref, dyh_ref, p_ref, mw_ref, sc_ref, old_ref, dp_ref, dm_ref, ds_ref):
        i = pl.program_id(0)
        t1 = _row(i * T + 1, (T, 1))
        th1 = _row((i + 1) * T + 1, (POOL_HALO, 1))
        live = i < nT - 1

        @pl.when(i == 0)
        def _():
            ds_ref[...] = jnp.zeros_like(ds_ref)
            dm_ref[...] = jnp.zeros_like(dm_ref)

        for g, w in enumerate(POOL_WINDOWS):
            sl = slice(g * PGW, (g + 1) * PGW)
            sc = sc_ref[:, sl]
            gate, dy = g_ref[:, sl], dy_ref[:, sl]
            sg, dsg = _silu_and_grad(gate)
            pooled = p_ref[:, sl]
            mixed = _dot(pooled, mw_ref[g])
            dmixed = (dy * sc * sg).astype(BF16)
            dm_ref[g] += _dot(pooled, dmixed, TN)
            dp_ref[:, D + g * PGW:D + (g + 1) * PGW] = (dy * mixed * sc * dsg).astype(BF16)
            ds_ref[:, sl] += jnp.sum(dy * mixed * sg, axis=0, keepdims=True)
            dpool = _dot(dmixed, mw_ref[g], NT)
            gate_h = gh_ref[:, sl]
            dmixed_h = (dyh_ref[:, sl] * sc * (gate_h * _sigmoid(gate_h))).astype(BF16)
            dpool_h = jnp.where(live, _dot(dmixed_h, mw_ref[g], NT), 0.0)
            q = dpool / jnp.minimum(t1, w).astype(F32)
            q_h = dpool_h / jnp.minimum(th1, w).astype(F32)
            qe = jnp.concatenate([q, q_h], axis=0)
            dp_ref[:, sl] = (_window_sums(qe, w, False)[:T, :] - dpool).astype(BF16)

    nxt = lambda i: jnp.minimum((i + 1) * hb, L // POOL_HALO - 1)
    row = lambda c: pl.BlockSpec((T, D), lambda i: (i, c))
    return pl.pallas_call(
        body, grid=(nT,),
        in_specs=[row(1), pl.BlockSpec((POOL_HALO, D), lambda i: (nxt(i), 1)),
                  row(0), pl.BlockSpec((POOL_HALO, D), lambda i: (nxt(i), 0)),
                  row(0), pl.BlockSpec((GROUPS, PGW, PGW), lambda i: (0, 0, 0)),
                  pl.BlockSpec((1, D), lambda i: (0, 0)), _ANY],
        out_specs=[pl.BlockSpec((T, 2 * D), lambda i: (i, 0)), pl.BlockSpec((GROUPS, PGW, PGW), lambda i: (0, 0, 0)),
                   pl.BlockSpec((1, D), lambda i: (0, 0))],
        out_shape=[_sds(dproj.shape, dproj.dtype), _sds((GROUPS, PGW, PGW), F32), _sds((1, D), F32)],
        input_output_aliases={7: 0},
        name="pool_bwd", compiler_params=_params("arbitrary"))(proj, proj, dmix, dmix, pooled, mixw, scale, dproj)


def conv_fwd(proj, cw, cb, D):
    L = proj.shape[0]
    C = cw.shape[1]
    assert (3 * D) % C == 0
    cblk = (3 * D) // C
    T = _pick(L, ROW_TILE)
    hb = T // CONV_HALO

    def body(u_ref, halo_ref, w_ref, b_ref, o_ref):
        i = pl.program_id(0)
        u = u_ref[...]
        xe = jnp.concatenate([jnp.where(i > 0, halo_ref[...], 0.0), u], axis=0)
        acc = b_ref[...] + w_ref[CONV_K - 1:CONV_K, :] * u
        for k in range(CONV_K - 1):
            acc = acc + w_ref[k:k + 1, :] * pltpu.roll(xe, CONV_K - 1 - k, 0)[CONV_HALO:, :]
        o_ref[...] = acc

    return pl.pallas_call(
        body, grid=(L // T,),
        in_specs=[pl.BlockSpec((T, C), lambda i: (i, cblk)),
                  pl.BlockSpec((CONV_HALO, C), lambda i: (jnp.maximum(i * hb - 1, 0), cblk)),
                  pl.BlockSpec((CONV_K, C), lambda i: (0, 0)),
                  pl.BlockSpec((1, C), lambda i: (0, 0))],
        out_specs=pl.BlockSpec((T, C), lambda i: (i, 0)),
        out_shape=_sds((L, C), F32), name="conv_fwd", compiler_params=_params("parallel"))(proj, proj, cw, cb)


def conv_bwd(dparts, proj, cw, dproj, D, jobs=()):
    L = proj.shape[0]
    C = cw.shape[1]
    cblk = (3 * D) // C
    T = _pick(L, ROW_TILE)
    hb = T // CONV_HALO
    nT = L // T
    widths = [p.shape[1] for p in dparts]
    assert sum(widths) == C
    n = len(dparts)

    def body(*refs):
        d_refs, dn_refs = refs[:n], refs[n:2 * n]
        u_ref, w_ref, old_ref, dr_ref, dw_ref, db_ref = refs[2 * n:]
        i = pl.program_id(0)

        @pl.when(i == 0)
        def _():
            dw_ref[...] = jnp.zeros_like(dw_ref)
            db_ref[...] = jnp.zeros_like(db_ref)

        at = 0
        for d_ref, dn_ref, wd in zip(d_refs, dn_refs, widths):
            sl = slice(at, at + wd)
            at += wd
            d = d_ref[...]
            u = u_ref[:, sl]
            de = jnp.concatenate([d, jnp.where(i < nT - 1, dn_ref[...], 0.0)], axis=0)
            acc = w_ref[CONV_K - 1:CONV_K, sl] * d
            dw_ref[CONV_K - 1:CONV_K, sl] += jnp.sum(d * u, axis=0, keepdims=True)
            for k in range(CONV_K - 1):
                sh = CONV_K - 1 - k
                ds = pltpu.roll(de, T + CONV_HALO - sh, 0)[:T, :]
                acc = acc + w_ref[k:k + 1, sl] * ds
                dw_ref[k:k + 1, sl] += jnp.sum(ds * u, axis=0, keepdims=True)
            dr_ref[:, sl] = acc.astype(dr_ref.dtype)
            db_ref[:, sl] += jnp.sum(d, axis=0, keepdims=True)

    nxt = lambda i: jnp.minimum((i + 1) * hb, L // CONV_HALO - 1)
    return _call(
        body, grid=(nT,),
        in_specs=[pl.BlockSpec((T, wd), lambda i: (i, 0)) for wd in widths]
        + [pl.BlockSpec((CONV_HALO, wd), lambda i: (nxt(i), 0)) for wd in widths]
        + [pl.BlockSpec((T, C), lambda i: (i, cblk)), pl.BlockSpec((CONV_K, C), lambda i: (0, 0)), _ANY],
        out_specs=[pl.BlockSpec((T, C), lambda i: (i, cblk)),
                   pl.BlockSpec((8, C), lambda i: (0, 0)),
                   pl.BlockSpec((1, C), lambda i: (0, 0))],
        out_shape=[_sds(dproj.shape, dproj.dtype), _sds((8, C), F32), _sds((1, C), F32)],
        aliases={2 * n + 2: 0}, name="conv_bwd", sem=("arbitrary",),
        args=(*dparts, *dparts, proj, cw, dproj), jobs=jobs)


def _softplus(v):
    y = jnp.exp(-jnp.abs(v))
    u = 1.0 + y
    log1p = jnp.where(u == 1.0, y, jnp.log(u) * y / jnp.where(u == 1.0, 1.0, u - 1.0))
    return jnp.maximum(v, 0.0) + log1p


def dt_prep(dtraw, bias, alog, expand, D, jobs=()):
    L = dtraw.shape[0]
    GC = D // GROUPS
    HPG = GC // HEAD_DIM
    Q = _pick(L, SCAN_CHUNK)
    nc = L // Q

    def body(r_ref, b_ref, a_ref, e_ref, dt_ref, acs_ref, acst_ref, dtx_ref, eax_ref, dsx_ref, cdx_ref):
        valid = lax.broadcasted_iota(jnp.int32, (1, LANES), 1) < HPG
        dt = jnp.where(valid, _softplus(r_ref[...] + b_ref[...]), 0.0)
        adt = dt * -jnp.exp(a_ref[...])
        tril = (_row(0, (Q, Q)) >= lax.broadcasted_iota(jnp.int32, (Q, Q), 1)).astype(BF16)
        acs = _sel_dot(tril, adt)
        last = acs[Q - 1:Q, :]
        dt_ref[...] = dt
        acs_ref[...] = acs
        acst_ref[...] = acs.T
        e = e_ref[...]
        dtx_ref[...] = _dot_sel(dt, e, 2)
        eax_ref[...] = jnp.exp(_dot_sel(acs, e, 2))
        dsx_ref[...] = jnp.exp(_dot_sel(last - acs, e, 2))
        cdx_ref[0] = jnp.exp(_dot_sel(jnp.broadcast_to(last, (8, LANES)), e, 2))

    head = pl.BlockSpec((Q, LANES), lambda g, c: (c, g))
    hvec = pl.BlockSpec((1, LANES), lambda g, c: (0, g))
    chan = pl.BlockSpec((Q, GC), lambda g, c: (c, g))
    return _call(
        body, grid=(GROUPS, nc),
        in_specs=[head, hvec, hvec, pl.BlockSpec((LANES, GC), lambda g, c: (0, 0))],
        out_specs=[head, head, pl.BlockSpec((LANES, Q), lambda g, c: (g, c)), chan, chan, chan,
                   pl.BlockSpec((1, 8, GC), lambda g, c: (c, 0, g))],
        out_shape=[_sds((L, GROUPS * LANES), F32), _sds((L, GROUPS * LANES), F32), _sds((GROUPS * LANES, L), F32),
                   _sds((L, D), F32), _sds((L, D), F32), _sds((L, D), F32), _sds((nc, 8, D), F32)],
        name="dt_prep", sem=("parallel", "parallel"), args=(dtraw, bias, alog, expand), jobs=jobs)


def _scan_specs(L, D, Q, rev):
    GC = D // GROUPS
    nc = L // Q
    ci = (lambda c: nc - 1 - c) if rev else (lambda c: c)
    return dict(
        xs=pl.BlockSpec((Q, GC), lambda g, c: (ci(c), g)),
        b=pl.BlockSpec((Q, STATE), lambda g, c: (ci(c), D // STATE + g)),
        c=pl.BlockSpec((Q, STATE), lambda g, c: (ci(c), D // STATE + GROUPS + g)),
        chan=pl.BlockSpec((Q, GC), lambda g, c: (ci(c), g)),
        cdx=pl.BlockSpec((1, 8, GC), lambda g, c: (ci(c), 0, g)),
        head=pl.BlockSpec((Q, LANES), lambda g, c: (ci(c), g)),
        headt=pl.BlockSpec((LANES, Q), lambda g, c: (g, ci(c))),
        state=pl.BlockSpec((1, 1, STATE, GC), lambda g, c: (ci(c), g, 0, 0)),
        hvec=pl.BlockSpec((1, LANES), lambda g, c: (0, g)),
        cvec=pl.BlockSpec((1, GC), lambda g, c: (0, g)))


def scan_fwd(pre, dtx, eax, dsx, cdx, acs, acst, proj, dexp, nw, mixed, D, jobs=()):
    L = pre.shape[0]
    GC = D // GROUPS
    Q = _pick(L, SCAN_CHUNK)
    nc = L // Q
    sp = _scan_specs(L, D, Q, False)

    def body(xs_ref, b_ref, c_ref, dtx_ref, eax_ref, dsx_ref, cdx_ref, acs_ref, acst_ref, z_ref, de_ref, nw_ref,
             old_ref, y_ref, st_ref, o_ref, s_scr):
        @pl.when(pl.program_id(1) == 0)
        def _():
            s_scr[...] = jnp.zeros_like(s_scr)

        tri = _row(0, (Q, Q)) >= lax.broadcasted_iota(jnp.int32, (Q, Q), 1)
        half = lax.broadcasted_iota(jnp.int32, (1, LANES), 1) // HEAD_DIM
        xs, _ = _silu_and_grad(xs_ref[...])
        bg = _silu_and_grad(b_ref[...])[0].astype(BF16)
        cg = _silu_and_grad(c_ref[...])[0].astype(BF16)
        xdt = xs * dtx_ref[...]
        sprev = s_scr[...]
        st_ref[0, 0] = sprev
        sc = _dot(cg, bg, NT)
        yoff = _dot(cg, sprev.astype(BF16)) * eax_ref[...]
        for j in range(GC // LANES):
            ps = slice(j * LANES, (j + 1) * LANES)
            xp = xdt[:, ps]
            acc = yoff[:, ps]
            for hh in range(2):
                h = 2 * j + hh
                lm = jnp.exp(jnp.where(tri, acs_ref[:, h:h + 1] - acst_ref[h:h + 1, :], -1e30))
                xm = jnp.where(half == hh, xp, 0.0).astype(BF16)
                acc = acc + _dot((sc * lm).astype(BF16), xm)
            y_ref[:, ps] = acc
        xw = (xdt * dsx_ref[...]).astype(BF16)
        s_scr[...] = cdx_ref[0, 0:1, :] * sprev + _dot(bg, xw, TN)
        z = z_ref[...]
        y3 = (y_ref[...] + de_ref[...] * xs) * (z * _sigmoid(z))
        r = lax.rsqrt(jnp.mean(y3 * y3, axis=-1, keepdims=True) + NORM_EPS)
        o_ref[...] = (y3 * r * nw_ref[...]).astype(o_ref.dtype)

    return _call(
        body, grid=(GROUPS, nc),
        in_specs=[sp["xs"], sp["b"], sp["c"], sp["chan"], sp["chan"], sp["chan"], sp["cdx"], sp["head"], sp["headt"],
                  pl.BlockSpec((Q, GC), lambda g, c: (c, 2 * GROUPS + g)), sp["cvec"], sp["cvec"], _ANY],
        out_specs=[sp["chan"], sp["state"], pl.BlockSpec((Q, GC), lambda g, c: (c, GROUPS + g))],
        out_shape=[_sds((L, D), F32), _sds((nc, GROUPS, STATE, GC), F32), _sds(mixed.shape, mixed.dtype)],
        aliases={12: 2}, scratch_shapes=[pltpu.VMEM((STATE, GC), F32)], name="scan_fwd",
        sem=("parallel", "arbitrary"), args=(pre, pre, pre, dtx, eax, dsx, cdx, acs, acst, proj, dexp, nw, mixed),
        jobs=jobs)


def scan_bwd(pre, dtx, eax, dsx, cdx, acs, acst, dt, dtraw, bias, alog, states, y, proj, dmix, nw, dexp, collapse, D,
             jobs=()):
    L = pre.shape[0]
    GC = D // GROUPS
    Q = _pick(L, SCAN_CHUNK)
    nc = L // Q
    sp = _scan_specs(L, D, Q, True)
    rc = lambda c: nc - 1 - c

    def body(xs_ref, b_ref, c_ref, dtx_ref, eax_ref, dsx_ref, cdx_ref, acs_ref, acst_ref, dt_ref, raw_ref,
             bias_ref, alog_ref, st_ref, y_ref, z_ref, dm_ref, nw_ref, dexp_ref, col_ref,
             dxs_ref, db_ref, dc_ref, ddt_ref, dal_ref, dbi_ref, dz_ref, dnw_ref, dde_ref, ds_scr, dx_scr):
        first = pl.program_id(1) == 0

        @pl.when(first)
        def _():
            ds_scr[...] = jnp.zeros_like(ds_scr)
            dal_ref[...] = jnp.zeros_like(dal_ref)
            dbi_ref[...] = jnp.zeros_like(dbi_ref)
            dnw_ref[...] = jnp.zeros_like(dnw_ref)
            dde_ref[...] = jnp.zeros_like(dde_ref)

        li = _row(0, (Q, Q))
        si = lax.broadcasted_iota(jnp.int32, (Q, Q), 1)
        lane = lax.broadcasted_iota(jnp.int32, (1, LANES), 1)
        half = lane // HEAD_DIM
        xs_pre, b_pre, c_pre = xs_ref[...], b_ref[...], c_ref[...]
        xs, xs_g = _silu_and_grad(xs_pre)
        bf, b_g = _silu_and_grad(b_pre)
        cf, c_g = _silu_and_grad(c_pre)
        bg, cg = bf.astype(BF16), cf.astype(BF16)
        dtx, eax, dsx = dtx_ref[...], eax_ref[...], dsx_ref[...]
        cd = cdx_ref[0, 0:1, :]
        xdt = xs * dtx
        sz, dsz = _silu_and_grad(z_ref[...])
        y2 = y_ref[...] + dexp_ref[...] * xs
        y3 = y2 * sz
        r = lax.rsqrt(jnp.mean(y3 * y3, axis=-1, keepdims=True) + NORM_EPS)
        n = y3 * r
        dm = dm_ref[...]
        gg = dm * nw_ref[...]
        dy3 = r * (gg - n * jnp.mean(gg * n, axis=-1, keepdims=True))
        dnw_ref[0:1, :] += jnp.sum(dm * n, axis=0, keepdims=True)
        G = dy3 * sz
        dz_ref[...] = (dy3 * y2 * dsz).astype(dz_ref.dtype)
        dde_ref[0:1, :] += jnp.sum(G * xs, axis=0, keepdims=True)
        prev = st_ref[0, 0]
        dsn = ds_scr[...]
        prev_b, dsn_b = prev.astype(BF16), dsn.astype(BF16)
        cp = _dot(cg, prev_b)
        ge_b = (G * eax).astype(BF16)
        d_c = _dot(ge_b, prev_b, NT)
        dprev = _dot(cg, ge_b, TN) + cd * dsn
        chan_a = G * cp * eax
        xw_b = (xdt * dsx).astype(BF16)
        dcd = jnp.sum(prev * dsn, axis=0, keepdims=True)
        d_b = _dot(xw_b, dsn_b, NT)
        dxw = _dot(bg, dsn_b)
        dd = dxw * xdt * dsx
        chan_a = chan_a - dd
        last_c = jnp.sum(dd, axis=0, keepdims=True) + dcd * cd
        sc = _dot(cg, bg, NT)
        head_row = _row(0, (LANES, 1))
        dsc = jnp.zeros((Q, Q), F32)
        dacs = jnp.zeros((Q, LANES), F32)
        colsums = jnp.zeros((LANES, Q), F32)
        for j in range(GC // LANES):
            ps = slice(j * LANES, (j + 1) * LANES)
            xp, gp = xdt[:, ps], G[:, ps]
            dxp = dxw[:, ps] * dsx[:, ps]
            for hh in range(2):
                h = 2 * j + hh
                lm = jnp.exp(jnp.where(li >= si, acs_ref[:, h:h + 1] - acst_ref[h:h + 1, :], -1e30))
                m = sc * lm
                xm = jnp.where(half == hh, xp, 0.0).astype(BF16)
                gm = jnp.where(half == hh, gp, 0.0).astype(BF16)
                dm = _dot(gm, xm, NT)
                dxp = dxp + _dot(m.astype(BF16), gm, TN)
                dsc = dsc + dm * lm
                w = dm * m
                dacs = dacs + jnp.where(lane == h, jnp.sum(w, axis=1, keepdims=True), 0.0)
                colsums = jnp.where(head_row == h, jnp.sum(w, axis=0, keepdims=True), colsums)
            dx_scr[:, ps] = dxp
        dacs = dacs - colsums.T
        dsc_b = dsc.astype(BF16)
        d_c = d_c + _dot(dsc_b, bg)
        d_b = d_b + _dot(dsc_b, cg, TN)
        ds_scr[...] = dprev
        dxdt = dx_scr[...]
        dxs_ref[...] = (dxdt * dtx + dexp_ref[...] * G) * xs_g
        db_ref[...] = d_b * b_g
        dc_ref[...] = d_c * c_g
        colm = col_ref[...]
        dacs = dacs + _dot_sel(chan_a, colm)
        dlast = _dot_sel(jnp.broadcast_to(last_c, (8, GC)), colm)[0:1, :]
        dacs = dacs + jnp.where(_row(0, (Q, 1)) == Q - 1, dlast, 0.0)
        dadt = _sel_dot((si >= li).astype(BF16), dacs)
        a = -jnp.exp(alog_ref[...])
        dt = dt_ref[...]
        ddt = dadt * a + _dot_sel(dxdt * xs, colm, 2)
        dal_ref[0:1, :] += jnp.sum(dadt * dt * a, axis=0, keepdims=True)
        draw = ddt * _sigmoid(raw_ref[...] + bias_ref[...])
        dbi_ref[0:1, :] += jnp.sum(draw, axis=0, keepdims=True)
        ddt_ref[...] = draw.astype(ddt_ref.dtype)

    acc = pl.BlockSpec((8, LANES), lambda g, c: (0, g))
    cacc = pl.BlockSpec((8, GC), lambda g, c: (0, g))
    return _call(
        body, grid=(GROUPS, nc),
        in_specs=[sp["xs"], sp["b"], sp["c"], sp["chan"], sp["chan"], sp["chan"], sp["cdx"], sp["head"], sp["headt"],
                  sp["head"], sp["head"], sp["hvec"], sp["hvec"], sp["state"], sp["chan"],
                  pl.BlockSpec((Q, GC), lambda g, c: (rc(c), 2 * GROUPS + g)),
                  pl.BlockSpec((Q, GC), lambda g, c: (rc(c), GROUPS + g)), sp["cvec"], sp["cvec"],
                  pl.BlockSpec((GC, LANES), lambda g, c: (0, 0))],
        out_specs=[sp["chan"],
                   pl.BlockSpec((Q, STATE), lambda g, c: (rc(c), g)),
                   pl.BlockSpec((Q, STATE), lambda g, c: (rc(c), g)),
                   sp["head"], acc, acc,
                   pl.BlockSpec((Q, GC), lambda g, c: (rc(c), 2 * GROUPS + g)), cacc, cacc],
        out_shape=[_sds((L, D), F32), _sds((L, GROUPS * STATE), F32), _sds((L, GROUPS * STATE), F32),
                   _sds((L, GROUPS * LANES), BF16), _sds((8, GROUPS * LANES), F32), _sds((8, GROUPS * LANES), F32),
                   _sds((L, proj.shape[1]), BF16), _sds((8, D), F32), _sds((8, D), F32)],
        scratch_shapes=[pltpu.VMEM((STATE, GC), F32), pltpu.VMEM((Q, GC), F32)], name="scan_bwd",
        sem=("parallel", "arbitrary"),
        args=(pre, pre, pre, dtx, eax, dsx, cdx, acs, acst, dt, dtraw, bias, alog, states, y, proj, dmix, nw, dexp,
              collapse),
        jobs=jobs)


def _adam_math(gv, w, m, v):
    c1 = 1.0 - ADAM_B1 ** ADAM_STEP
    c2 = 1.0 - ADAM_B2 ** ADAM_STEP
    nm = ADAM_B1 * m + (1.0 - ADAM_B1) * gv
    nv = ADAM_B2 * v + (1.0 - ADAM_B2) * (gv * gv)
    return -ADAM_LR * ((nm / c1) / (jnp.sqrt(nv / c2) + ADAM_EPS) + ADAM_WD * w), nm, nv


def adamw(g, w, m, v, name):
    R, C = g.shape
    T = R if R <= 128 else 128
    assert R % T == 0

    def body(g_ref, w_ref, m_ref, v_ref, d_ref, nm_ref, nv_ref):
        d_ref[...], nm_ref[...], nv_ref[...] = _adam_math(g_ref[...], w_ref[...], m_ref[...], v_ref[...])

    blk = pl.BlockSpec((T, C), lambda i: (i, 0))
    return pl.pallas_call(
        body, grid=(R // T,), in_specs=[blk] * 4, out_specs=[blk] * 3,
        out_shape=[_sds((R, C), F32)] * 3, name=name, compiler_params=_params("parallel"))(g, w, m, v)


def adamw_layer(g, w, m, v, layer, prev, name):
    R, C = g.shape
    T = _rows_tile(R, ROW_TILE)
    assert R % T == 0

    def body(g_ref, w_ref, m_ref, v_ref, *rest):
        go_ref, d_ref, nm_ref, nv_ref = rest[-4:]
        gv = g_ref[...]
        go_ref[0] = gv
        d_ref[0], nm_ref[0], nv_ref[0] = _adam_math(gv, w_ref[0], m_ref[0], v_ref[0])

    mine = pl.BlockSpec((1, T, C), lambda i: (layer, i, 0))
    prev = list(prev or [])
    return pl.pallas_call(
        body, grid=(R // T,), in_specs=[pl.BlockSpec((T, C), lambda i: (i, 0)), mine, mine, mine] + [_ANY] * len(prev),
        out_specs=[mine] * 4, out_shape=[_sds(w.shape, F32)] * 4,
        input_output_aliases={4 + i: i for i in range(len(prev))}, name=name,
        compiler_params=_params("parallel"))(g, w, m, v, *prev)


def cast_bf16(a, layer):
    _, R, C = a.shape
    T = _rows_tile(R, ROW_TILE)

    def body(a_ref, o_ref):
        o_ref[...] = a_ref[0].astype(BF16)

    return pl.pallas_call(
        body, grid=(R // T,), in_specs=[pl.BlockSpec((1, T, C), lambda i: (layer, i, 0))],
        out_specs=pl.BlockSpec((T, C), lambda i: (i, 0)), out_shape=_sds((R, C), BF16),
        name="cast_bf16", compiler_params=_params("parallel"))(a)


def _to_groups(v, hpg):
    lead = v.shape[:-1]
    t = v.reshape(lead + (GROUPS, hpg))
    t = jnp.pad(t, [(0, 0)] * (len(lead) + 1) + [(0, LANES - hpg)])
    return t.reshape(lead + (GROUPS * LANES,))


def _from_groups(a, hpg):
    lead = a.shape[:-1]
    return a.reshape(lead + (GROUPS, LANES))[..., :hpg].reshape(lead + (GROUPS * hpg,))


def _expand_matrix(D):
    gc = D // GROUPS
    return (jnp.arange(LANES)[:, None] == (jnp.arange(gc)[None, :] // HEAD_DIM)).astype(BF16)


def layer_params(pre_w, w_in_t, mixw, scale, cw, cb, bias, alog, dskip, nw, w_out_full, post_w, D):
    hpg = D // GROUPS // HEAD_DIM
    main = w_in_t.shape[0] - GROUPS * hpg
    wdt_t = _to_groups(w_in_t[main:].T, hpg).T
    return dict(
        pre_w=pre_w[None], win_t=w_in_t, main=main, wdt_t=wdt_t, mixw=mixw,
        scale=scale[None], cw=cw, cb=cb[None], bias=_to_groups(bias, hpg)[None], alog=_to_groups(alog, hpg)[None],
        dexp=jnp.repeat(dskip, HEAD_DIM)[None], nw=nw[None], wout=w_out_full, post_w=post_w[None])


def layer_fwd(x, p, D, next_shards=None, h=None, next_pre_w=None, own_rest=None):
    nxt = bool(next_shards)
    parts = lambda a: ((a.shape[0] // 32) * 16, a.shape[0] - (a.shape[0] // 32) * 16)
    if nxt:
        s_in, s_out, s_mix, s_cw = next_shards
        (in_a, in_b), (out_a, out_b) = parts(s_in), parts(s_out)
    if h is None:
        h = rms_fwd(x, p["pre_w"])
    own = own_rest is not None
    proj, got = matmul(h, p["win_t"], "nt", F32, "proj", tm=1024, n_out=p["main"],
                       jobs=([gather_rows_job(s_in, 0, in_a)] if nxt else [])
                       + ([gather_send_job(own_rest[0], own_rest[1])] if own else []))
    g_in = got[0][0] if nxt else None
    dtraw, got = matmul(h, p["wdt_t"], "nt", F32, "dtproj",
                        jobs=[gather_pass_job(got[-1], own_rest[1])] if own else [])
    if own:
        p.update(own_rest[2](got[0]))
    mixed, pooled = pool_fwd(proj, p["mixw"], p["scale"], D)
    pre = conv_fwd(proj, p["cw"], p["cb"], D)
    dtp, got = dt_prep(dtraw, p["bias"], p["alog"], _expand_matrix(D), D,
                       jobs=[gather_rows_job(s_out, 0, out_a)] if nxt else [])
    g_out = got[0][0] if nxt else None
    dt, acs, acst, dtx, eax, dsx, cdx = dtp
    (y, states, mixed), got = scan_fwd(pre, dtx, eax, dsx, cdx, acs, acst, proj, p["dexp"], p["nw"], mixed, D,
                                       jobs=[gather_rows_job(s_in, in_a, in_b, g_in)] if nxt else [])
    g_in = got[0][0] if nxt else None
    out, got = matmul(mixed, p["wout"], "nn", F32, "outproj", tm=1024,
                      jobs=[gather_rows_job(s_out, out_a, out_b, g_out), gather_send_job([s_mix, s_cw]),
                            gather_pass_job([g_in], [True])] if nxt else [])
    (xn, h_next), got2 = post_fwd(x, out, p["post_w"], next_pre_w,
                                  jobs=[gather_pass_job(got[0] + got[1], [True, False, False])] if nxt else [])
    gathered = got[2] + got2[0] if nxt else None
    return xn, dict(x=x, h=h, proj=proj, dtraw=dtraw, pooled=pooled, pre=pre, dtp=dtp, y=y, states=states,
                    mixed=mixed, out=out), gathered, h_next


def layer_bwd(dxn, p, s, D, where=None, post=None, below=None):
    reduce = where is not None
    chip, core = where if reduce else (None, None)
    hpg = D // GROUPS // HEAD_DIM
    PGW = D // GROUPS
    main = p["main"]
    SH = (main + GROUPS * hpg) // 4
    dt, acs, acst, dtx, eax, dsx, cdx = s["dtp"]
    dout, d_post = post if post else post_bwd(s["out"], p["post_w"], dxn)
    part = BF16 if reduce else F32
    d_wout, _ = matmul(s["mixed"], dout, "tn", part, "dwout", tm=1024)
    g_out = d_wout.reshape(4, 2, D // 4, D)
    dmix, got = matmul(dout, p["wout"], "nt", F32, "dmixed", tm=1024,
                       jobs=[pair_exchange_job([g_out])] if reduce else [])
    pair_out = pair_add(g_out, got[0][0], core, BF16) if reduce else None
    (dxs, db, dc, ddtraw, d_alog, d_bias, dproj, d_nw, d_dexp), got = scan_bwd(
        s["pre"], dtx, eax, dsx, cdx, acs, acst, dt, s["dtraw"], p["bias"], p["alog"], s["states"], s["y"],
        s["proj"], dmix, p["nw"], p["dexp"], _expand_matrix(D).T, D,
        jobs=[chip_exchange_job([pair_out])] if reduce else [])
    mine_out = chip_add(pair_out, got[0][0], chip, core) if reduce else None
    (dproj, d_cw, d_cb), got = conv_bwd([dxs, db, dc], s["proj"], p["cw"], dproj, D,
                                        jobs=[pair_gather_job([mine_out])] if reduce else [])
    r_out = got[0][0] if reduce else None
    dproj, d_mixw, d_scale = pool_bwd(s["proj"], dmix, s["pooled"], p["mixw"], p["scale"], dproj, D)
    d_wmain_t, _ = matmul(dproj, s["h"], "tn", part, "dwmain", tm=1024)
    d_wdt_t, _ = matmul(ddtraw, s["h"], "tn", part, "dwdt")
    late = [d_wmain_t[None], d_wdt_t[None],
            d_mixw.reshape(GROUPS, 4, PGW // 4, PGW).transpose(1, 0, 2, 3).reshape(4, 2, GROUPS * PGW // 8, PGW)]
    cols = [True, True, False]
    dh_b, got = matmul(ddtraw, p["wdt_t"], "nn", F32, "dh_dt",
                       jobs=[pair_exchange_job(late, cols)] if reduce else [])
    if reduce:
        p_main, p_dt, p_mix = [pair_add(g, r, core, BF16, k, n) for g, r, k, n in
                               zip(late, got[0], cols, [4 * SH, None, None])]
        p_in = lax.dynamic_update_slice(p_main[0], _from_groups(p_dt[0].T, hpg).T, (main, 0))
        pairs = [p_in.reshape(4, SH, D // 2), p_mix]
    dh_a, got = matmul(dproj, p["win_t"], "nn", F32, "dh_main", jobs=[chip_exchange_job(pairs)] if reduce else [])
    mines = [chip_add(q, r, chip, core, k) for q, r, k in zip(pairs, got[0], [True, False])] if reduce else None
    (dx, d_pre, *post_below), got = rms_bwd(s["x"], p["pre_w"], dh_a, dh_b, dxn, below,
                                            jobs=[pair_gather_job(mines, [True, False])] if reduce else [])
    reduced = dict(w_in=got[0][0], w_out=r_out, pool_mix_w=got[0][1]) if reduce else None
    grads = dict(
        pre_norm_w=d_pre[0], pool_scale=d_scale[0], conv_w=d_cw[:CONV_K], conv_b=d_cb[0],
        dt_bias=_from_groups(d_bias[0], hpg), a_log=_from_groups(d_alog[0], hpg),
        d_skip=d_dexp[0].reshape(-1, HEAD_DIM).sum(axis=-1), ssd_norm_w=d_nw[0], post_norm_w=d_post[0])
    if not reduce:
        grads.update(w_in=jnp.concatenate([d_wmain_t.T, _from_groups(d_wdt_t.T, hpg)], axis=1), pool_mix_w=d_mixw,
                     w_out=d_wout)
    return dx, grads, reduced, tuple(post_below) or None


def local_step(x, target, params, D):
    saved, h, n = [], None, len(params)
    for l, p in enumerate(params):
        x, s, _, h = layer_fwd(x, p, D, h=h, next_pre_w=params[l + 1]["pre_w"] if l + 1 < n else None)
        saved.append(s)
    dx, sumsq = loss_head(x, target)
    grads, post = [None] * n, None
    for l in reversed(range(n)):
        below = (saved[l - 1]["out"], params[l - 1]["post_w"]) if l else None
        dx, grads[l], _, post = layer_bwd(dx, params[l], saved[l], D, post=post, below=below)
    return sumsq, dx, grads


SMALL = ("pre_norm_w", "pool_scale", "conv_w", "conv_b", "dt_bias", "a_log", "d_skip", "ssd_norm_w", "post_norm_w")
BIG = ("w_in", "w_out", "pool_mix_w")


def _pack(parts):
    flat = jnp.concatenate([p.reshape(-1) for p in parts])
    n = flat.shape[0]
    rows = -(-n // (LANES * LANES)) * LANES
    return jnp.pad(flat, (0, rows * LANES - n)).reshape(rows, LANES)


def _unpack(packed, shapes):
    flat, out, at = packed.reshape(-1), [], 0
    for s in shapes:
        n = math.prod(s)
        out.append(flat[at:at + n].reshape(s))
        at += n
    return out


def kernel(x, pre_norm_w, w_in, pool_mix_w, pool_scale, conv_w, conv_b, dt_bias, a_log, d_skip, ssd_norm_w, w_out, post_norm_w, loss_target, m_pre_norm_w, m_w_in, m_pool_mix_w, m_pool_scale, m_conv_w, m_conv_b, m_dt_bias, m_a_log, m_d_skip, m_ssd_norm_w, m_w_out, m_post_norm_w, v_pre_norm_w, v_w_in, v_pool_mix_w, v_pool_scale, v_conv_w, v_conv_b, v_dt_bias, v_a_log, v_d_skip, v_ssd_norm_w, v_w_out, v_post_norm_w):
    NL, D, SH = w_in.shape
    PGW = D // GROUPS
    CS = conv_w.shape[2]
    chip = (2 * lax.axis_index("x") + lax.axis_index("y")).astype(jnp.int32)
    chip1, core = chip.reshape(1), lax.axis_index("c").astype(jnp.int32).reshape(1)

    tr = lambda t: jnp.transpose(t, (0, 2, 1))
    w_in_t, m_w_in_t, v_w_in_t = tr(w_in), tr(m_w_in), tr(v_w_in)
    halved_by_cols = [True, True, False, False]

    def shards(l):
        return [cast_bf16(w_in_t, l), cast_bf16(w_out, l),
                cast_bf16(pool_mix_w.reshape(NL, GROUPS * PGW // 4, PGW), l).reshape(2, GROUPS * PGW // 8, PGW),
                conv_w[l].reshape(2, CONV_K * CS // (2 * LANES), LANES)]

    def rest_params(g):
        g_out, g_mix, g_cw = g
        return dict(wout=g_out.reshape(2 * D, D),
                    mixw=g_mix.reshape(4, GROUPS, PGW // 4, PGW).transpose(1, 0, 2, 3).reshape(GROUPS, PGW, PGW),
                    cw=g_cw.reshape(4, CONV_K, CS).transpose(1, 0, 2).reshape(CONV_K, 4 * CS))

    def params(l, g):
        late = rest_params(g[1:]) if len(g) > 1 else dict(wout=None, mixw=None, cw=None)
        return layer_params(pre_norm_w[l], g[0].reshape(4 * SH, D), late["mixw"], pool_scale[l], late["cw"], conv_b[l],
                            dt_bias[l], a_log[l], d_skip[l], ssd_norm_w[l], late["wout"], post_norm_w[l], D)

    first = shards(0)
    gathered = run_jobs([gather_send_job(first[:1], [True])], "gather_send")[0]
    gathered = run_jobs([gather_pass_job(gathered, [True])], "gather_pass")[0]
    xl, h, ps, saved = x[0], None, [], []
    for l in range(NL):
        ps.append(params(l, gathered))
        last = l + 1 == NL
        xl, s, gathered, h = layer_fwd(xl, ps[l], D, None if last else shards(l + 1), h,
                                       None if last else pre_norm_w[l + 1][None],
                                       (first[1:], halved_by_cols[1:], rest_params) if l == 0 else None)
        saved.append(s)
    dx, sumsq = loss_head(xl, loss_target[0])

    given = dict(w_in=(w_in_t, m_w_in_t, v_w_in_t), w_out=(w_out, m_w_out, v_w_out),
                 pool_mix_w=(pool_mix_w, m_pool_mix_w, v_pool_mix_w))
    flat = {n: [t.reshape(NL, -1, t.shape[-1]) for t in given[n]] for n in BIG}
    done = {n: None for n in BIG}
    grads, post = [None] * NL, None
    for l in reversed(range(NL)):
        below = (saved[l - 1]["out"], ps[l - 1]["post_w"]) if l else None
        dx, grads[l], reduced, post = layer_bwd(dx, ps[l], saved[l], D, (chip1, core), post, below)
        for n in BIG:
            r = reduced[n]
            done[n] = adamw_layer(r.reshape(-1, r.shape[-1]), *flat[n], l, done[n], "adamw_" + n)

    small_shapes = [(NL,) + grads[0][n].shape for n in SMALL]
    packed = _pack([0.5 / D * sumsq[0, :1]] + [jnp.stack([g[n] for g in grads]) for n in SMALL])
    total = allreduce_small(packed)
    loss, *small = _unpack(total, [(1,)] + small_shapes)
    small = dict(zip(SMALL, small))
    small["conv_w"] = lax.dynamic_slice_in_dim(small["conv_w"], chip * CS, CS, axis=2)

    given_small = dict(
        pre_norm_w=(pre_norm_w, m_pre_norm_w, v_pre_norm_w), pool_scale=(pool_scale, m_pool_scale, v_pool_scale),
        conv_w=(conv_w, m_conv_w, v_conv_w), conv_b=(conv_b, m_conv_b, v_conv_b),
        dt_bias=(dt_bias, m_dt_bias, v_dt_bias), a_log=(a_log, m_a_log, v_a_log),
        d_skip=(d_skip, m_d_skip, v_d_skip), ssd_norm_w=(ssd_norm_w, m_ssd_norm_w, v_ssd_norm_w),
        post_norm_w=(post_norm_w, m_post_norm_w, v_post_norm_w))
    shapes = [given_small[n][0].shape for n in SMALL]
    upd = adamw(_pack([small[n] for n in SMALL]), *[_pack([given_small[n][i] for n in SMALL]) for i in range(3)],
                "adamw_small")
    upd = [dict(zip(SMALL, _unpack(u, shapes))) for u in upd]

    out = {n: (small[n], upd[0][n], upd[1][n], upd[2][n]) for n in SMALL}
    for n in BIG:
        out[n] = tuple(t.reshape(given[n][0].shape) for t in done[n])
    out["w_in"] = tuple(tr(t) for t in out["w_in"])

    order = ("pre_norm_w", "w_in", "pool_mix_w", "pool_scale", "conv_w", "conv_b", "dt_bias", "a_log", "d_skip",
             "ssd_norm_w", "w_out", "post_norm_w")
    return (loss.reshape(()), dx[None], *[out[n][0] for n in order], *[out[n][1] for n in order],
            *[out[n][2] for n in order], *[out[n][3] for n in order])
```

```python
import functools
import math

import jax
import jax.numpy as jnp
from jax import lax
from jax.experimental import pallas as pl
from jax.experimental.pallas import tpu as pltpu

F32 = jnp.float32
BF16 = jnp.bfloat16

NORM_EPS = 1e-6
HEAD_DIM = 64
STATE = 128
GROUPS = 4
POOL_WINDOWS = (2, 4, 8, 16)
POOL_HALO = 16
CONV_K = 4
CONV_HALO = 8
SCAN_CHUNK = 256
LANES = 128
VMEM_LIMIT = 52 * 1024 * 1024
ROW_TILE = 256

ADAM_LR = 0.001
ADAM_B1 = 0.9
ADAM_B2 = 0.999
ADAM_EPS = 1e-08
ADAM_WD = 0.01
ADAM_STEP = 10

MESH = pl.DeviceIdType.MESH

NN = (((1,), (0,)), ((), ()))
NT = (((1,), (1,)), ((), ()))
TN = (((0,), (0,)), ((), ()))

_ANY = pl.BlockSpec(memory_space=pl.ANY)


def _params(*sem):
    return pltpu.CompilerParams(dimension_semantics=sem, vmem_limit_bytes=VMEM_LIMIT)


def _pick(dim, pref):
    if dim <= pref:
        return dim
    t = (pref // LANES) * LANES
    while t > LANES and dim % t:
        t -= LANES
    assert dim % t == 0, (dim, pref)
    return t


def _rows_tile(rows, pref):
    t = (min(pref, rows) // 8) * 8
    while t >= 8 and rows % t:
        t -= 8
    return t if t >= 8 else rows


def _dot(a, b, dn=NN):
    return lax.dot_general(a, b, dn, preferred_element_type=F32)


def _split3(a):
    hi = a.astype(BF16)
    r = a - hi.astype(F32)
    mid = r.astype(BF16)
    return hi, mid, (r - mid.astype(F32)).astype(BF16)


def _dot_sel(a, e, parts=3):
    hi, mid, lo = _split3(a)
    return (_dot(lo, e) + _dot(mid, e)) + _dot(hi, e) if parts == 3 else _dot(mid, e) + _dot(hi, e)


def _sel_dot(e, b):
    hi, mid, lo = _split3(b)
    return (_dot(e, lo) + _dot(e, mid)) + _dot(e, hi)


def _sigmoid(v):
    return 0.5 * jnp.tanh(0.5 * v) + 0.5


def _silu_and_grad(v):
    s = _sigmoid(v)
    return v * s, s * (1.0 + v * (1.0 - s))


def _row(i, shape):
    return lax.broadcasted_iota(jnp.int32, shape, 0) + i


def _sds(shape, dtype):
    return jax.ShapeDtypeStruct(tuple(shape), dtype)


class Job:
    def __init__(self, ins, outs, aliased, nsem, start, finish):
        self.ins, self.outs, self.aliased, self.nsem, self.start, self.finish = ins, outs, aliased, nsem, start, finish


def _place():
    x, y, c = lax.axis_index("x"), lax.axis_index("y"), lax.axis_index("c")
    return x, y, c, [(1 - x, y), (x, 1 - y), (1 - x, 1 - y)]


def _remote(src, dst, send_sem, recv_sem, device):
    return pltpu.make_async_remote_copy(src_ref=src, dst_ref=dst, send_sem=send_sem, recv_sem=recv_sem,
                                        device_id=device, device_id_type=MESH)


def _half(ref, c, cols, lead=0):
    idx = [slice(None)] * lead
    if cols:
        w = ref.shape[-1] // 2
        idx += [slice(None)] * (len(ref.shape) - lead - 1) + [pl.ds(pl.multiple_of(c * w, LANES), w)]
    else:
        idx += [c]
    return ref.at[tuple(idx)]


def _flags(cols, n):
    return list(cols) if cols else [False] * n


def gather_send_job(arrs, cols=None):
    n = len(arrs)
    cols = _flags(cols, n)

    def copies(ins, outs, send, recv):
        x, y, c, chips = _place()
        mine = 2 * x + y
        out = []
        for a in range(n):
            out.append(_remote(ins[a], outs[a].at[mine], send.at[4 * a + 3], recv.at[4 * a + 3], (x, y, 1 - c)))
            for j, chip in enumerate(chips):
                out.append(_remote(_half(ins[a], c, cols[a]), _half(outs[a].at[mine], c, cols[a]),
                                   send.at[4 * a + j], recv.at[4 * a + j], (*chip, c)))
        return out

    def start(ins, outs, send, recv):
        for cp in copies(ins, outs, send, recv):
            cp.start()

    def finish(ins, outs, send, recv):
        x, y, c, chips = _place()
        for a in range(n):
            for j, chip in enumerate(chips):
                landed = _half(outs[a].at[2 * chip[0] + chip[1]], c, cols[a])
                _remote(landed, landed, send.at[4 * a + j], recv.at[4 * a + j], (x, y, 1 - c)).wait_recv()
            twin = outs[a].at[2 * x + y]
            _remote(twin, twin, send.at[4 * a + 3], recv.at[4 * a + 3], (x, y, 1 - c)).wait_recv()
        for cp in copies(ins, outs, send, recv):
            cp.wait_send()

    return Job(list(arrs), [_sds((4,) + a.shape, a.dtype) for a in arrs], False, 4 * n, start, finish)


def gather_rows_job(shard, start, size, into=None):
    rows = pl.ds(start, size)

    def copies(ins, outs, send, recv):
        x, y, c, chips = _place()
        mine = outs[0].at[2 * x + y, rows]
        out = [_remote(ins[0].at[rows], mine, send.at[3], recv.at[3], (x, y, 1 - c))]
        for j, chip in enumerate(chips):
            out.append(_remote(_half(ins[0].at[rows], c, True), _half(mine, c, True), send.at[j], recv.at[j],
                               (*chip, c)))
        return out

    def start_(ins, outs, send, recv):
        for cp in copies(ins, outs, send, recv):
            cp.start()

    def finish(ins, outs, send, recv):
        x, y, c, chips = _place()
        for j, chip in enumerate(chips):
            landed = _half(outs[0].at[2 * chip[0] + chip[1], rows], c, True)
            _remote(landed, landed, send.at[j], recv.at[j], (x, y, 1 - c)).wait_recv()
        twin = outs[0].at[2 * x + y, rows]
        _remote(twin, twin, send.at[3], recv.at[3], (x, y, 1 - c)).wait_recv()
        for cp in copies(ins, outs, send, recv):
            cp.wait_send()

    ins = [shard] if into is None else [shard, into]
    return Job(ins, [_sds((4,) + shard.shape, shard.dtype)], {} if into is None else {1: 0}, 4, start_, finish)


def gather_pass_job(bufs, cols=None):
    n = len(bufs)
    cols = _flags(cols, n)

    def copies(outs, send, recv):
        x, y, c, chips = _place()
        out = []
        for a in range(n):
            for j, chip in enumerate(chips):
                landed = _half(outs[a].at[2 * chip[0] + chip[1]], c, cols[a])
                out.append(_remote(landed, landed, send.at[3 * a + j], recv.at[3 * a + j], (x, y, 1 - c)))
        return out

    def start(ins, outs, send, recv):
        for cp in copies(outs, send, recv):
            cp.start()

    def finish(ins, outs, send, recv):
        x, y, c, chips = _place()
        for a in range(n):
            for j, chip in enumerate(chips):
                passed = _half(outs[a].at[2 * chip[0] + chip[1]], 1 - c, cols[a])
                _remote(passed, passed, send.at[3 * a + j], recv.at[3 * a + j], (x, y, 1 - c)).wait_recv()
        for cp in copies(outs, send, recv):
            cp.wait_send()

    return Job(list(bufs), [_sds(b.shape, b.dtype) for b in bufs], True, 3 * n, start, finish)


def pair_exchange_job(arrs, cols=None):
    n = len(arrs)
    cols = _flags(cols, n)

    def copies(ins, outs, send, recv):
        x, y, c, _ = _place()
        return [_remote(_half(ins[a], 1 - c, cols[a], 1), outs[a], send.at[a], recv.at[a], (x, y, 1 - c))
                for a in range(n)]

    def start(ins, outs, send, recv):
        for cp in copies(ins, outs, send, recv):
            cp.start()

    def finish(ins, outs, send, recv):
        for cp in copies(ins, outs, send, recv):
            cp.wait()

    shape = lambda a, k: a.shape[:-1] + (a.shape[-1] // 2,) if k else a.shape[:1] + a.shape[2:]
    return Job(list(arrs), [_sds(shape(a, k), a.dtype) for a, k in zip(arrs, cols)], False, n, start, finish)


def chip_exchange_job(arrs):
    n = len(arrs)

    def copies(ins, outs, send, recv):
        x, y, c, chips = _place()
        return [_remote(ins[a].at[2 * chip[0] + chip[1]], outs[a].at[j], send.at[3 * a + j], recv.at[3 * a + j],
                        (*chip, c)) for a in range(n) for j, chip in enumerate(chips)]

    def start(ins, outs, send, recv):
        for cp in copies(ins, outs, send, recv):
            cp.start()

    def finish(ins, outs, send, recv):
        for cp in copies(ins, outs, send, recv):
            cp.wait()

    return Job(list(arrs), [_sds((3,) + a.shape[1:], a.dtype) for a in arrs], False, 3 * n, start, finish)


def chip_exchange_rows_job(p, start, size, into=None):
    rows = pl.ds(start, size)

    def copies(ins, outs, send, recv):
        x, y, c, chips = _place()
        return [_remote(ins[0].at[2 * chip[0] + chip[1], rows], outs[0].at[j, rows], send.at[j], recv.at[j], (*chip, c))
                for j, chip in enumerate(chips)]

    def start_(ins, outs, send, recv):
        for cp in copies(ins, outs, send, recv):
            cp.start()

    def finish(ins, outs, send, recv):
        for cp in copies(ins, outs, send, recv):
            cp.wait()

    ins = [p] if into is None else [p, into]
    return Job(ins, [_sds((3,) + p.shape[1:], p.dtype)], {} if into is None else {1: 0}, 3, start_, finish)


def pair_gather_job(bufs, cols=None):
    n = len(bufs)
    cols = _flags(cols, n)

    def copies(outs, send, recv):
        x, y, c, _ = _place()
        return [_remote(_half(outs[a], c, cols[a]), _half(outs[a], c, cols[a]), send.at[a], recv.at[a],
                        (x, y, 1 - c)) for a in range(n)]

    def start(ins, outs, send, recv):
        for cp in copies(outs, send, recv):
            cp.start()

    def finish(ins, outs, send, recv):
        for cp in copies(outs, send, recv):
            cp.wait()

    return Job(list(bufs), [_sds(b.shape, b.dtype) for b in bufs], True, n, start, finish)


def _call(body, *, grid, in_specs, out_specs, out_shape, name, sem, args, scratch_shapes=(), jobs=(), aliases=None):
    in_specs, out_specs, out_shape, scratch_shapes = list(in_specs), list(out_specs), list(out_shape), list(scratch_shapes)
    aliases = dict(aliases or {})
    n_in, n_out, n_scr = len(in_specs), len(out_specs), len(scratch_shapes)
    if jobs:
        sem = ("arbitrary",) * len(grid)
    at_in, at_out = n_in, n_out
    for j in jobs:
        pairs = j.aliased if isinstance(j.aliased, dict) else {i: i for i in range(len(j.ins))} if j.aliased else {}
        aliases.update({at_in + i: at_out + o for i, o in pairs.items()})
        at_in, at_out = at_in + len(j.ins), at_out + len(j.outs)

    def wrapped(*refs):
        ins, p = refs[:n_in], n_in
        jins = []
        for j in jobs:
            jins.append(refs[p:p + len(j.ins)])
            p += len(j.ins)
        outs, p = refs[p:p + n_out], p + n_out
        jouts = []
        for j in jobs:
            jouts.append(refs[p:p + len(j.outs)])
            p += len(j.outs)
        scr, sems = refs[p:p + n_scr], refs[p + n_scr:]

        def start():
            for k, j in enumerate(jobs):
                j.start(jins[k], jouts[k], sems[2 * k], sems[2 * k + 1])

        def finish():
            for k, j in enumerate(jobs):
                j.finish(jins[k], jouts[k], sems[2 * k], sems[2 * k + 1])

        if jobs and grid:
            ids = [pl.program_id(d) for d in range(len(grid))]
            pl.when(functools.reduce(jnp.logical_and, [i == 0 for i in ids]))(start)
            body(*ins, *outs, *scr)
            pl.when(functools.reduce(jnp.logical_and, [i == g - 1 for i, g in zip(ids, grid)]))(finish)
        else:
            start()
            body(*ins, *outs, *scr)
            finish()

    kwargs = dict(grid=grid) if grid else {}
    res = pl.pallas_call(
        wrapped, in_specs=in_specs + [_ANY] * (at_in - n_in), out_specs=out_specs + [_ANY] * (at_out - n_out),
        out_shape=out_shape + [o for j in jobs for o in j.outs],
        scratch_shapes=scratch_shapes + [pltpu.SemaphoreType.DMA((j.nsem,)) for j in jobs for _ in range(2)],
        input_output_aliases=aliases, name=name,
        compiler_params=pltpu.CompilerParams(dimension_semantics=sem, vmem_limit_bytes=VMEM_LIMIT) if grid
        else pltpu.CompilerParams(vmem_limit_bytes=VMEM_LIMIT), **kwargs)(*args, *[a for j in jobs for a in j.ins])
    res = list(res)
    outs, rest, per_job = res[:n_out], res[n_out:], []
    for j in jobs:
        per_job.append(rest[:len(j.outs)])
        rest = rest[len(j.outs):]
    return outs, per_job


def run_jobs(jobs, name):
    return _call(lambda: None, grid=(), in_specs=[], out_specs=[], out_shape=[], name=name, sem=(), args=(), jobs=jobs)[1]


def pair_add(g, r, core, out_dtype, cols=False, out_rows=None):
    S, R, C = r.shape
    T = _rows_tile(R, 2 * ROW_TILE if C <= 1024 else ROW_TILE if C <= 4096 else ROW_TILE // 2)

    def body(c_ref, g_ref, r_ref, o_ref):
        o_ref[0] = ((g_ref[0] if cols else g_ref[0, 0]).astype(F32) + r_ref[0].astype(F32)).astype(o_ref.dtype)

    g_spec = (pl.BlockSpec((1, T, C), lambda k, i, c_ref: (k, i, c_ref[0])) if cols
              else pl.BlockSpec((1, 1, T, C), lambda k, i, c_ref: (k, c_ref[0], i, 0)))
    return pl.pallas_call(
        body,
        grid_spec=pltpu.PrefetchScalarGridSpec(
            num_scalar_prefetch=1, grid=(S, R // T),
            in_specs=[g_spec, pl.BlockSpec((1, T, C), lambda k, i, c_ref: (k, i, 0))],
            out_specs=pl.BlockSpec((1, T, C), lambda k, i, c_ref: (k, i, 0))),
        out_shape=_sds((S, out_rows or R, C), out_dtype), name="pair_add",
        compiler_params=_params("parallel", "parallel"))(core, g, r)


def chip_add(p, r, chip, core, cols=False):
    _, R, C = p.shape
    T = _rows_tile(R, ROW_TILE)

    def body(k_ref, c_ref, p_ref, r0_ref, r1_ref, r2_ref, o_ref):
        s = ((p_ref[0].astype(F32) + r0_ref[0].astype(F32)) + r1_ref[0].astype(F32)) + r2_ref[0].astype(F32)
        if cols:
            o_ref[...] = s
        else:
            o_ref[0] = s

    slot = lambda j: pl.BlockSpec((1, T, C), lambda i, k_ref, c_ref: (j, i, 0))
    out_spec = (pl.BlockSpec((T, C), lambda i, k_ref, c_ref: (i, c_ref[0])) if cols
                else pl.BlockSpec((1, T, C), lambda i, k_ref, c_ref: (c_ref[0], i, 0)))
    return pl.pallas_call(
        body,
        grid_spec=pltpu.PrefetchScalarGridSpec(
            num_scalar_prefetch=2, grid=(R // T,),
            in_specs=[pl.BlockSpec((1, T, C), lambda i, k_ref, c_ref: (k_ref[0], i, 0)), slot(0), slot(1), slot(2)],
            out_specs=out_spec),
        out_shape=_sds((R, 2 * C) if cols else (2, R, C), F32), name="chip_add",
        compiler_params=_params("parallel"))(chip, core, p, r, r, r)


def allreduce_small(v):
    R = v.shape[0]

    def body(v_ref, o_ref, buf, send_sems, recv_sems, local_sem):
        x, y, c, chips = _place()
        me, sibling = (x, y, c), (x, y, 1 - c)

        def rows(px, py, pc):
            return buf.at[pl.ds((4 * px + 2 * py + pc) * R, R), :]

        def copy(k, block, to, src=None):
            return _remote(rows(*block) if src is None else src, rows(*block), send_sems.at[k], recv_sems.at[k], to)

        mine = pltpu.make_async_copy(v_ref, rows(*me), local_sem)
        mine.start()
        first = [copy(0, me, sibling, src=v_ref)]
        first += [copy(1 + j, me, (*chip, c), src=v_ref) for j, chip in enumerate(chips)]
        for cp in first:
            cp.start()
        passed = [copy(4 + j, (*chip, c), sibling) for j, chip in enumerate(chips)]
        for j, chip in enumerate(chips):
            copy(1 + j, (*chip, c), me).wait_recv()
            passed[j].start()
        copy(0, sibling, me).wait_recv()
        for j, chip in enumerate(chips):
            copy(4 + j, (*chip, 1 - c), me).wait_recv()
        for cp in first + passed:
            cp.wait_send()
        mine.wait()
        acc = buf[0:R, :]
        for d in range(1, 8):
            acc = acc + buf[d * R:(d + 1) * R, :]
        o_ref[...] = acc

    return pl.pallas_call(
        body, in_specs=[pl.BlockSpec(memory_space=pltpu.VMEM)], out_specs=pl.BlockSpec(memory_space=pltpu.VMEM),
        out_shape=_sds((R, LANES), F32),
        scratch_shapes=[pltpu.VMEM((8 * R, LANES), F32), pltpu.SemaphoreType.DMA((7,)),
                        pltpu.SemaphoreType.DMA((7,)), pltpu.SemaphoreType.DMA],
        name="allreduce_small", compiler_params=pltpu.CompilerParams(vmem_limit_bytes=VMEM_LIMIT))(v)


def matmul(a, b, mode, out_dtype, name, tm=512, tn=1024, tk=4608, jobs=(), n_out=None):
    if mode == "nn":
        (M, K), (K2, N) = a.shape, b.shape
    elif mode == "nt":
        (M, K), (N, K2) = a.shape, b.shape
        N = n_out or N
    else:
        (K, M), (K2, N) = a.shape, b.shape
    assert K == K2 or (mode == "nn" and K2 > K)
    tm, tn, tk = _pick(M, tm), _pick(N, tn), _pick(K, tk)
    nk = K // tk
    dn = {"nn": NN, "nt": NT, "tn": TN}[mode]

    def body(a_ref, b_ref, o_ref, *acc):
        part = _dot(a_ref[...].astype(BF16), b_ref[...].astype(BF16), dn)
        if nk == 1:
            o_ref[...] = part.astype(o_ref.dtype)
            return
        acc_ref, = acc
        k = pl.program_id(2)

        @pl.when(k == 0)
        def _():
            acc_ref[...] = part

        @pl.when(jnp.logical_and(k > 0, k < nk - 1))
        def _():
            acc_ref[...] += part

        @pl.when(k == nk - 1)
        def _():
            o_ref[...] = (acc_ref[...] + part).astype(o_ref.dtype)

    a_spec = (pl.BlockSpec((tk, tm), lambda i, j, k: (k, i)) if mode == "tn"
              else pl.BlockSpec((tm, tk), lambda i, j, k: (i, k)))
    b_spec = (pl.BlockSpec((tn, tk), lambda i, j, k: (j, k)) if mode == "nt"
              else pl.BlockSpec((tk, tn), lambda i, j, k: (k, j)))
    outs, per_job = _call(
        body, grid=(M // tm, N // tn, nk), in_specs=[a_spec, b_spec],
        out_specs=[pl.BlockSpec((tm, tn), lambda i, j, k: (i, j))], out_shape=[_sds((M, N), out_dtype)],
        scratch_shapes=[pltpu.VMEM((tm, tn), F32)] if nk > 1 else [], name=name,
        sem=("parallel", "parallel", "arbitrary"), args=(a, b), jobs=jobs)
    return outs[0], per_job


def rms_fwd(x, w):
    L, D = x.shape
    T = _pick(L, ROW_TILE)

    def body(x_ref, w_ref, h_ref):
        xv = x_ref[...]
        r = lax.rsqrt(jnp.mean(xv * xv, axis=-1, keepdims=True) + NORM_EPS)
        h_ref[...] = (xv * r * w_ref[...]).astype(h_ref.dtype)

    return pl.pallas_call(
        body, grid=(L // T,),
        in_specs=[pl.BlockSpec((T, D), lambda i: (i, 0)), pl.BlockSpec((1, D), lambda i: (0, 0))],
        out_specs=pl.BlockSpec((T, D), lambda i: (i, 0)),
        out_shape=_sds((L, D), BF16), name="rms_fwd", compiler_params=_params("parallel"))(x, w)


def post_fwd(x, o, w, next_w=None, jobs=()):
    L, D = x.shape
    T = _pick(L, ROW_TILE)

    def body(x_ref, o_ref, w_ref, *rest):
        ov = o_ref[...]
        r = lax.rsqrt(jnp.mean(ov * ov, axis=-1, keepdims=True) + NORM_EPS)
        y = x_ref[...] + ov * r * w_ref[...]
        rest[-1 if next_w is None else -2][...] = y
        if next_w is not None:
            r2 = lax.rsqrt(jnp.mean(y * y, axis=-1, keepdims=True) + NORM_EPS)
            rest[-1][...] = (y * r2 * rest[0][...]).astype(BF16)

    row = pl.BlockSpec((T, D), lambda i: (i, 0))
    vec = pl.BlockSpec((1, D), lambda i: (0, 0))
    more = [] if next_w is None else [next_w]
    outs, per_job = _call(
        body, grid=(L // T,), in_specs=[row, row, vec] + [vec] * len(more), out_specs=[row] * (1 + len(more)),
        out_shape=[_sds((L, D), F32)] + [_sds((L, D), BF16)] * len(more), name="post_fwd", sem=("parallel",),
        args=(x, o, w, *more), jobs=jobs)
    return (outs[0], outs[1] if more else None), per_job


def _rms_bwd_math(xv, w, dy):
    r = lax.rsqrt(jnp.mean(xv * xv, axis=-1, keepdims=True) + NORM_EPS)
    xhat = xv * r
    g = dy * w
    dx = r * (g - xhat * jnp.mean(g * xhat, axis=-1, keepdims=True))
    return dx, jnp.sum(dy * xhat, axis=0, keepdims=True)


def post_bwd(o, w, dxn):
    L, D = o.shape
    T = _pick(L, ROW_TILE)

    def body(o_ref, w_ref, d_ref, do_ref, dw_ref):
        dx, dw = _rms_bwd_math(o_ref[...], w_ref[...], d_ref[...])
        do_ref[...] = dx.astype(do_ref.dtype)

        @pl.when(pl.program_id(0) == 0)
        def _():
            dw_ref[...] = jnp.zeros_like(dw_ref)

        dw_ref[...] += dw

    row = pl.BlockSpec((T, D), lambda i: (i, 0))
    vec = pl.BlockSpec((1, D), lambda i: (0, 0))
    return pl.pallas_call(
        body, grid=(L // T,), in_specs=[row, vec, row], out_specs=[row, vec],
        out_shape=[_sds((L, D), BF16), _sds((1, D), F32)],
        name="post_bwd", compiler_params=_params("arbitrary"))(o, w, dxn)


def rms_bwd(x, w, dh_a, dh_b, dxn, below=None, jobs=()):
    L, D = x.shape
    T = _pick(L, ROW_TILE)

    def body(x_ref, w_ref, a_ref, b_ref, d_ref, *rest):
        dx, dw = _rms_bwd_math(x_ref[...], w_ref[...], a_ref[...] + b_ref[...])
        dx = d_ref[...] + dx
        outs = rest[2:] if below else rest
        outs[0][...] = dx

        @pl.when(pl.program_id(0) == 0)
        def _():
            for acc in outs[1::2]:
                acc[...] = jnp.zeros_like(acc)

        outs[1][...] += dw
        if below:
            do, dwp = _rms_bwd_math(rest[0][...], rest[1][...], dx)
            outs[2][...] = do.astype(BF16)
            outs[3][...] += dwp

    row = pl.BlockSpec((T, D), lambda i: (i, 0))
    vec = pl.BlockSpec((1, D), lambda i: (0, 0))
    more = list(below) if below else []
    return _call(
        body, grid=(L // T,), in_specs=[row, vec, row, row, row] + ([row, vec] if below else []),
        out_specs=[row, vec] + ([row, vec] if below else []),
        out_shape=[_sds((L, D), F32), _sds((1, D), F32)] + ([_sds((L, D), BF16), _sds((1, D), F32)] if below else []),
        name="rms_bwd", sem=("arbitrary",), args=(x, w, dh_a, dh_b, dxn, *more), jobs=jobs)


def loss_head(y, target):
    L, D = y.shape
    T = _pick(L, ROW_TILE)

    def body(y_ref, t_ref, d_ref, s_ref):
        e = y_ref[...] - t_ref[...]
        d_ref[...] = e * (1.0 / D)

        @pl.when(pl.program_id(0) == 0)
        def _():
            s_ref[...] = jnp.zeros_like(s_ref)

        s_ref[...] += jnp.sum(e * e)

    row = pl.BlockSpec((T, D), lambda i: (i, 0))
    return pl.pallas_call(
        body, grid=(L // T,), in_specs=[row, row],
        out_specs=[row, pl.BlockSpec((8, LANES), lambda i: (0, 0))],
        out_shape=[_sds((L, D), F32), _sds((8, LANES), F32)],
        name="loss_head", compiler_params=_params("arbitrary"))(y, target)


def _window_sums(xe, w, back):
    n = xe.shape[0]
    s, k = xe, 1
    while k < w:
        s = s + pltpu.roll(s, k if back else n - k, 0)
        k *= 2
    return s


def pool_fwd(proj, mixw, scale, D):
    L = proj.shape[0]
    PGW = D // GROUPS
    T = _pick(L, ROW_TILE)
    hb = T // POOL_HALO

    def body(u_ref, halo_ref, g_ref, mw_ref, sc_ref, y_ref, p_ref):
        i = pl.program_id(0)
        u = u_ref[...]
        halo = jnp.where(i > 0, halo_ref[...], 0.0)
        xe = jnp.concatenate([halo, u], axis=0)
        t1 = _row(i * T + 1, (T, 1))
        for g, w in enumerate(POOL_WINDOWS):
            sl = slice(g * PGW, (g + 1) * PGW)
            win = _window_sums(xe[:, sl], w, True)[POOL_HALO:, :]
            cnt = jnp.minimum(t1, w).astype(F32)
            pooled = (win / cnt - u[:, sl]).astype(BF16)
            p_ref[:, sl] = pooled
            mixed = _dot(pooled, mw_ref[g])
            gate = g_ref[:, sl]
            y_ref[:, sl] = (mixed * sc_ref[:, sl] * (gate * _sigmoid(gate))).astype(BF16)

    return pl.pallas_call(
        body, grid=(L // T,),
        in_specs=[pl.BlockSpec((T, D), lambda i: (i, 0)),
                  pl.BlockSpec((POOL_HALO, D), lambda i: (jnp.maximum(i * hb - 1, 0), 0)),
                  pl.BlockSpec((T, D), lambda i: (i, 1)),
                  pl.BlockSpec((GROUPS, PGW, PGW), lambda i: (0, 0, 0)),
                  pl.BlockSpec((1, D), lambda i: (0, 0))],
        out_specs=[pl.BlockSpec((T, D), lambda i: (i, 0)), pl.BlockSpec((T, D), lambda i: (i, 0))],
        out_shape=[_sds((L, 2 * D), BF16), _sds((L, D), BF16)],
        name="pool_fwd", compiler_params=_params("parallel"))(proj, proj, proj, mixw, scale)


def pool_bwd(proj, dmix, pooled, mixw, scale, dproj, D):
    L = proj.shape[0]
    PGW = D // GROUPS
    T = _pick(L, ROW_TILE)
    hb = T // POOL_HALO
    nT = L // T

    def body(g_ref, gh_ref, dy_ref, dyh_ref, p_ref, mw_ref, sc_ref, old_ref, dp_ref, dm_ref, ds_ref):
        i = pl.program_id(0)
        t1 = _row(i * T + 1, (T, 1))
        th1 = _row((i + 1) * T + 1, (POOL_HALO, 1))
        live = i < nT - 1

        @pl.when(i == 0)
        def _():
            ds_ref[...] = jnp.zeros_like(ds_ref)
            dm_ref[...] = jnp.zeros_like(dm_ref)

        for g, w in enumerate(POOL_WINDOWS):
            sl = slice(g * PGW, (g + 1) * PGW)
            sc = sc_ref[:, sl]
            gate, dy = g_ref[:, sl], dy_ref[:, sl]
            sg, dsg = _silu_and_grad(gate)
            pooled = p_ref[:, sl]
            mixed = _dot(pooled, mw_ref[g])
            dmixed = (dy * sc * sg).astype(BF16)
            dm_ref[g] += _dot(pooled, dmixed, TN)
            dp_ref[:, D + g * PGW:D + (g + 1) * PGW] = (dy * mixed * sc * dsg).astype(BF16)
            ds_ref[:, sl] += jnp.sum(dy * mixed * sg, axis=0, keepdims=True)
            dpool = _dot(dmixed, mw_ref[g], NT)
            gate_h = gh_ref[:, sl]
            dmixed_h = (dyh_ref[:, sl] * sc * (gate_h * _sigmoid(gate_h))).astype(BF16)
            dpool_h = jnp.where(live, _dot(dmixed_h, mw_ref[g], NT), 0.0)
            q = dpool / jnp.minimum(t1, w).astype(F32)
            q_h = dpool_h / jnp.minimum(th1, w).astype(F32)
            qe = jnp.concatenate([q, q_h], axis=0)
            dp_ref[:, sl] = (_window_sums(qe, w, False)[:T, :] - dpool).astype(BF16)

    nxt = lambda i: jnp.minimum((i + 1) * hb, L // POOL_HALO - 1)
    row = lambda c: pl.BlockSpec((T, D), lambda i: (i, c))
    return pl.pallas_call(
        body, grid=(nT,),
        in_specs=[row(1), pl.BlockSpec((POOL_HALO, D), lambda i: (nxt(i), 1)),
                  row(0), pl.BlockSpec((POOL_HALO, D), lambda i: (nxt(i), 0)),
                  row(0), pl.BlockSpec((GROUPS, PGW, PGW), lambda i: (0, 0, 0)),
                  pl.BlockSpec((1, D), lambda i: (0, 0)), _ANY],
        out_specs=[pl.BlockSpec((T, 2 * D), lambda i: (i, 0)), pl.BlockSpec((GROUPS, PGW, PGW), lambda i: (0, 0, 0)),
                   pl.BlockSpec((1, D), lambda i: (0, 0))],
        out_shape=[_sds(dproj.shape, dproj.dtype), _sds((GROUPS, PGW, PGW), F32), _sds((1, D), F32)],
        input_output_aliases={7: 0},
        name="pool_bwd", compiler_params=_params("arbitrary"))(proj, proj, dmix, dmix, pooled, mixw, scale, dproj)


def conv_fwd(proj, cw, cb, D):
    L = proj.shape[0]
    C = cw.shape[1]
    assert (3 * D) % C == 0
    cblk = (3 * D) // C
    T = _pick(L, ROW_TILE)
    hb = T // CONV_HALO

    def body(u_ref, halo_ref, w_ref, b_ref, o_ref):
        i = pl.program_id(0)
        u = u_ref[...]
        xe = jnp.concatenate([jnp.where(i > 0, halo_ref[...], 0.0), u], axis=0)
        acc = b_ref[...] + w_ref[CONV_K - 1:CONV_K, :] * u
        for k in range(CONV_K - 1):
            acc = acc + w_ref[k:k + 1, :] * pltpu.roll(xe, CONV_K - 1 - k, 0)[CONV_HALO:, :]
        o_ref[...] = acc

    return pl.pallas_call(
        body, grid=(L // T,),
        in_specs=[pl.BlockSpec((T, C), lambda i: (i, cblk)),
                  pl.BlockSpec((CONV_HALO, C), lambda i: (jnp.maximum(i * hb - 1, 0), cblk)),
                  pl.BlockSpec((CONV_K, C), lambda i: (0, 0)),
                  pl.BlockSpec((1, C), lambda i: (0, 0))],
        out_specs=pl.BlockSpec((T, C), lambda i: (i, 0)),
        out_shape=_sds((L, C), F32), name="conv_fwd", compiler_params=_params("parallel"))(proj, proj, cw, cb)


def conv_bwd(dparts, proj, cw, dproj, D, jobs=()):
    L = proj.shape[0]
    C = cw.shape[1]
    cblk = (3 * D) // C
    T = _pick(L, ROW_TILE)
    hb = T // CONV_HALO
    nT = L // T
    widths = [p.shape[1] for p in dparts]
    assert sum(widths) == C
    n = len(dparts)

    def body(*refs):
        d_refs, dn_refs = refs[:n], refs[n:2 * n]
        u_ref, w_ref, old_ref, dr_ref, dw_ref, db_ref = refs[2 * n:]
        i = pl.program_id(0)

        @pl.when(i == 0)
        def _():
            dw_ref[...] = jnp.zeros_like(dw_ref)
            db_ref[...] = jnp.zeros_like(db_ref)

        at = 0
        for d_ref, dn_ref, wd in zip(d_refs, dn_refs, widths):
            sl = slice(at, at + wd)
            at += wd
            d = d_ref[...]
            u = u_ref[:, sl]
            de = jnp.concatenate([d, jnp.where(i < nT - 1, dn_ref[...], 0.0)], axis=0)
            acc = w_ref[CONV_K - 1:CONV_K, sl] * d
            dw_ref[CONV_K - 1:CONV_K, sl] += jnp.sum(d * u, axis=0, keepdims=True)
            for k in range(CONV_K - 1):
                sh = CONV_K - 1 - k
                ds = pltpu.roll(de, T + CONV_HALO - sh, 0)[:T, :]
                acc = acc + w_ref[k:k + 1, sl] * ds
                dw_ref[k:k + 1, sl] += jnp.sum(ds * u, axis=0, keepdims=True)
            dr_ref[:, sl] = acc.astype(dr_ref.dtype)
            db_ref[:, sl] += jnp.sum(d, axis=0, keepdims=True)

    nxt = lambda i: jnp.minimum((i + 1) * hb, L // CONV_HALO - 1)
    return _call(
        body, grid=(nT,),
        in_specs=[pl.BlockSpec((T, wd), lambda i: (i, 0)) for wd in widths]
        + [pl.BlockSpec((CONV_HALO, wd), lambda i: (nxt(i), 0)) for wd in widths]
        + [pl.BlockSpec((T, C), lambda i: (i, cblk)), pl.BlockSpec((CONV_K, C), lambda i: (0, 0)), _ANY],
        out_specs=[pl.BlockSpec((T, C), lambda i: (i, cblk)),
                   pl.BlockSpec((8, C), lambda i: (0, 0)),
                   pl.BlockSpec((1, C), lambda i: (0, 0))],
        out_shape=[_sds(dproj.shape, dproj.dtype), _sds((8, C), F32), _sds((1, C), F32)],
        aliases={2 * n + 2: 0}, name="conv_bwd", sem=("arbitrary",),
        args=(*dparts, *dparts, proj, cw, dproj), jobs=jobs)


def _softplus(v):
    y = jnp.exp(-jnp.abs(v))
    u = 1.0 + y
    log1p = jnp.where(u == 1.0, y, jnp.log(u) * y / jnp.where(u == 1.0, 1.0, u - 1.0))
    return jnp.maximum(v, 0.0) + log1p


def dt_prep(dtraw, bias, alog, expand, D, jobs=()):
    L = dtraw.shape[0]
    GC = D // GROUPS
    HPG = GC // HEAD_DIM
    Q = _pick(L, SCAN_CHUNK)
    nc = L // Q

    def body(r_ref, b_ref, a_ref, e_ref, dt_ref, acs_ref, acst_ref, dtx_ref, eax_ref, dsx_ref, cdx_ref):
        valid = lax.broadcasted_iota(jnp.int32, (1, LANES), 1) < HPG
        dt = jnp.where(valid, _softplus(r_ref[...] + b_ref[...]), 0.0)
        adt = dt * -jnp.exp(a_ref[...])
        tril = (_row(0, (Q, Q)) >= lax.broadcasted_iota(jnp.int32, (Q, Q), 1)).astype(BF16)
        acs = _sel_dot(tril, adt)
        last = acs[Q - 1:Q, :]
        dt_ref[...] = dt
        acs_ref[...] = acs
        acst_ref[...] = acs.T
        e = e_ref[...]
        dtx_ref[...] = _dot_sel(dt, e, 2)
        eax_ref[...] = jnp.exp(_dot_sel(acs, e, 2))
        dsx_ref[...] = jnp.exp(_dot_sel(last - acs, e, 2))
        cdx_ref[0] = jnp.exp(_dot_sel(jnp.broadcast_to(last, (8, LANES)), e, 2))

    head = pl.BlockSpec((Q, LANES), lambda g, c: (c, g))
    hvec = pl.BlockSpec((1, LANES), lambda g, c: (0, g))
    chan = pl.BlockSpec((Q, GC), lambda g, c: (c, g))
    return _call(
        body, grid=(GROUPS, nc),
        in_specs=[head, hvec, hvec, pl.BlockSpec((LANES, GC), lambda g, c: (0, 0))],
        out_specs=[head, head, pl.BlockSpec((LANES, Q), lambda g, c: (g, c)), chan, chan, chan,
                   pl.BlockSpec((1, 8, GC), lambda g, c: (c, 0, g))],
        out_shape=[_sds((L, GROUPS * LANES), F32), _sds((L, GROUPS * LANES), F32), _sds((GROUPS * LANES, L), F32),
                   _sds((L, D), F32), _sds((L, D), F32), _sds((L, D), F32), _sds((nc, 8, D), F32)],
        name="dt_prep", sem=("parallel", "parallel"), args=(dtraw, bias, alog, expand), jobs=jobs)


def _scan_specs(L, D, Q, rev):
    GC = D // GROUPS
    nc = L // Q
    ci = (lambda c: nc - 1 - c) if rev else (lambda c: c)
    return dict(
        xs=pl.BlockSpec((Q, GC), lambda g, c: (ci(c), g)),
        b=pl.BlockSpec((Q, STATE), lambda g, c: (ci(c), D // STATE + g)),
        c=pl.BlockSpec((Q, STATE), lambda g, c: (ci(c), D // STATE + GROUPS + g)),
        chan=pl.BlockSpec((Q, GC), lambda g, c: (ci(c), g)),
        cdx=pl.BlockSpec((1, 8, GC), lambda g, c: (ci(c), 0, g)),
        head=pl.BlockSpec((Q, LANES), lambda g, c: (ci(c), g)),
        headt=pl.BlockSpec((LANES, Q), lambda g, c: (g, ci(c))),
        state=pl.BlockSpec((1, 1, STATE, GC), lambda g, c: (ci(c), g, 0, 0)),
        hvec=pl.BlockSpec((1, LANES), lambda g, c: (0, g)),
        cvec=pl.BlockSpec((1, GC), lambda g, c: (0, g)))


def scan_fwd(pre, dtx, eax, dsx, cdx, acs, acst, proj, dexp, nw, mixed, D, jobs=()):
    L = pre.shape[0]
    GC = D // GROUPS
    Q = _pick(L, SCAN_CHUNK)
    nc = L // Q
    sp = _scan_specs(L, D, Q, False)

    def body(xs_ref, b_ref, c_ref, dtx_ref, eax_ref, dsx_ref, cdx_ref, acs_ref, acst_ref, z_ref, de_ref, nw_ref,
             old_ref, y_ref, st_ref, o_ref, s_scr):
        @pl.when(pl.program_id(1) == 0)
        def _():
            s_scr[...] = jnp.zeros_like(s_scr)

        tri = _row(0, (Q, Q)) >= lax.broadcasted_iota(jnp.int32, (Q, Q), 1)
        half = lax.broadcasted_iota(jnp.int32, (1, LANES), 1) // HEAD_DIM
        xs, _ = _silu_and_grad(xs_ref[...])
        bg = _silu_and_grad(b_ref[...])[0].astype(BF16)
        cg = _silu_and_grad(c_ref[...])[0].astype(BF16)
        xdt = xs * dtx_ref[...]
        sprev = s_scr[...]
        st_ref[0, 0] = sprev
        sc = _dot(cg, bg, NT)
        yoff = _dot(cg, sprev.astype(BF16)) * eax_ref[...]
        for j in range(GC // LANES):
            ps = slice(j * LANES, (j + 1) * LANES)
            xp = xdt[:, ps]
            acc = yoff[:, ps]
            for hh in range(2):
                h = 2 * j + hh
                lm = jnp.exp(jnp.where(tri, acs_ref[:, h:h + 1] - acst_ref[h:h + 1, :], -1e30))
                xm = jnp.where(half == hh, xp, 0.0).astype(BF16)
                acc = acc + _dot((sc * lm).astype(BF16), xm)
            y_ref[:, ps] = acc
        xw = (xdt * dsx_ref[...]).astype(BF16)
        s_scr[...] = cdx_ref[0, 0:1, :] * sprev + _dot(bg, xw, TN)
        z = z_ref[...]
        y3 = (y_ref[...] + de_ref[...] * xs) * (z * _sigmoid(z))
        r = lax.rsqrt(jnp.mean(y3 * y3, axis=-1, keepdims=True) + NORM_EPS)
        o_ref[...] = (y3 * r * nw_ref[...]).astype(o_ref.dtype)

    return _call(
        body, grid=(GROUPS, nc),
        in_specs=[sp["xs"], sp["b"], sp["c"], sp["chan"], sp["chan"], sp["chan"], sp["cdx"], sp["head"], sp["headt"],
                  pl.BlockSpec((Q, GC), lambda g, c: (c, 2 * GROUPS + g)), sp["cvec"], sp["cvec"], _ANY],
        out_specs=[sp["chan"], sp["state"], pl.BlockSpec((Q, GC), lambda g, c: (c, GROUPS + g))],
        out_shape=[_sds((L, D), F32), _sds((nc, GROUPS, STATE, GC), F32), _sds(mixed.shape, mixed.dtype)],
        aliases={12: 2}, scratch_shapes=[pltpu.VMEM((STATE, GC), F32)], name="scan_fwd",
        sem=("parallel", "arbitrary"), args=(pre, pre, pre, dtx, eax, dsx, cdx, acs, acst, proj, dexp, nw, mixed),
        jobs=jobs)


def scan_bwd(pre, dtx, eax, dsx, cdx, acs, acst, dt, dtraw, bias, alog, states, y, proj, dmix, nw, dexp, collapse, D,
             jobs=()):
    L = pre.shape[0]
    GC = D // GROUPS
    Q = _pick(L, SCAN_CHUNK)
    nc = L // Q
    sp = _scan_specs(L, D, Q, True)
    rc = lambda c: nc - 1 - c

    def body(xs_ref, b_ref, c_ref, dtx_ref, eax_ref, dsx_ref, cdx_ref, acs_ref, acst_ref, dt_ref, raw_ref,
             bias_ref, alog_ref, st_ref, y_ref, z_ref, dm_ref, nw_ref, dexp_ref, col_ref,
             dxs_ref, db_ref, dc_ref, ddt_ref, dal_ref, dbi_ref, dz_ref, dnw_ref, dde_ref, ds_scr, dx_scr):
        first = pl.program_id(1) == 0

        @pl.when(first)
        def _():
            ds_scr[...] = jnp.zeros_like(ds_scr)
            dal_ref[...] = jnp.zeros_like(dal_ref)
            dbi_ref[...] = jnp.zeros_like(dbi_ref)
            dnw_ref[...] = jnp.zeros_like(dnw_ref)
            dde_ref[...] = jnp.zeros_like(dde_ref)

        li = _row(0, (Q, Q))
        si = lax.broadcasted_iota(jnp.int32, (Q, Q), 1)
        lane = lax.broadcasted_iota(jnp.int32, (1, LANES), 1)
        half = lane // HEAD_DIM
        xs_pre, b_pre, c_pre = xs_ref[...], b_ref[...], c_ref[...]
        xs, xs_g = _silu_and_grad(xs_pre)
        bf, b_g = _silu_and_grad(b_pre)
        cf, c_g = _silu_and_grad(c_pre)
        bg, cg = bf.astype(BF16), cf.astype(BF16)
        dtx, eax, dsx = dtx_ref[...], eax_ref[...], dsx_ref[...]
        cd = cdx_ref[0, 0:1, :]
        xdt = xs * dtx
        sz, dsz = _silu_and_grad(z_ref[...])
        y2 = y_ref[...] + dexp_ref[...] * xs
        y3 = y2 * sz
        r = lax.rsqrt(jnp.mean(y3 * y3, axis=-1, keepdims=True) + NORM_EPS)
        n = y3 * r
        dm = dm_ref[...]
        gg = dm * nw_ref[...]
        dy3 = r * (gg - n * jnp.mean(gg * n, axis=-1, keepdims=True))
        dnw_ref[0:1, :] += jnp.sum(dm * n, axis=0, keepdims=True)
        G = dy3 * sz
        dz_ref[...] = (dy3 * y2 * dsz).astype(dz_ref.dtype)
        dde_ref[0:1, :] += jnp.sum(G * xs, axis=0, keepdims=True)
        prev = st_ref[0, 0]
        dsn = ds_scr[...]
        prev_b, dsn_b = prev.astype(BF16), dsn.astype(BF16)
        cp = _dot(cg, prev_b)
        ge_b = (G * eax).astype(BF16)
        d_c = _dot(ge_b, prev_b, NT)
        dprev = _dot(cg, ge_b, TN) + cd * dsn
        chan_a = G * cp * eax
        xw_b = (xdt * dsx).astype(BF16)
        dcd = jnp.sum(prev * dsn, axis=0, keepdims=True)
        d_b = _dot(xw_b, dsn_b, NT)
        dxw = _dot(bg, dsn_b)
        dd = dxw * xdt * dsx
        chan_a = chan_a - dd
        last_c = jnp.sum(dd, axis=0, keepdims=True) + dcd * cd
        sc = _dot(cg, bg, NT)
        head_row = _row(0, (LANES, 1))
        dsc = jnp.zeros((Q, Q), F32)
        dacs = jnp.zeros((Q, LANES), F32)
        colsums = jnp.zeros((LANES, Q), F32)
        for j in range(GC // LANES):
            ps = slice(j * LANES, (j + 1) * LANES)
            xp, gp = xdt[:, ps], G[:, ps]
            dxp = dxw[:, ps] * dsx[:, ps]
            for hh in range(2):
                h = 2 * j + hh
                lm = jnp.exp(jnp.where(li >= si, acs_ref[:, h:h + 1] - acst_ref[h:h + 1, :], -1e30))
                m = sc * lm
                xm = jnp.where(half == hh, xp, 0.0).astype(BF16)
                gm = jnp.where(half == hh, gp, 0.0).astype(BF16)
                dm = _dot(gm, xm, NT)
                dxp = dxp + _dot(m.astype(BF16), gm, TN)
                dsc = dsc + dm * lm
                w = dm * m
                dacs = dacs + jnp.where(lane == h, jnp.sum(w, axis=1, keepdims=True), 0.0)
                colsums = jnp.where(head_row == h, jnp.sum(w, axis=0, keepdims=True), colsums)
            dx_scr[:, ps] = dxp
        dacs = dacs - colsums.T
        dsc_b = dsc.astype(BF16)
        d_c = d_c + _dot(dsc_b, bg)
        d_b = d_b + _dot(dsc_b, cg, TN)
        ds_scr[...] = dprev
        dxdt = dx_scr[...]
        dxs_ref[...] = (dxdt * dtx + dexp_ref[...] * G) * xs_g
        db_ref[...] = d_b * b_g
        dc_ref[...] = d_c * c_g
        colm = col_ref[...]
        dacs = dacs + _dot_sel(chan_a, colm)
        dlast = _dot_sel(jnp.broadcast_to(last_c, (8, GC)), colm)[0:1, :]
        dacs = dacs + jnp.where(_row(0, (Q, 1)) == Q - 1, dlast, 0.0)
        dadt = _sel_dot((si >= li).astype(BF16), dacs)
        a = -jnp.exp(alog_ref[...])
        dt = dt_ref[...]
        ddt = dadt * a + _dot_sel(dxdt * xs, colm, 2)
        dal_ref[0:1, :] += jnp.sum(dadt * dt * a, axis=0, keepdims=True)
        draw = ddt * _sigmoid(raw_ref[...] + bias_ref[...])
        dbi_ref[0:1, :] += jnp.sum(draw, axis=0, keepdims=True)
        ddt_ref[...] = draw.astype(ddt_ref.dtype)

    acc = pl.BlockSpec((8, LANES), lambda g, c: (0, g))
    cacc = pl.BlockSpec((8, GC), lambda g, c: (0, g))
    return _call(
        body, grid=(GROUPS, nc),
        in_specs=[sp["xs"], sp["b"], sp["c"], sp["chan"], sp["chan"], sp["chan"], sp["cdx"], sp["head"], sp["headt"],
                  sp["head"], sp["head"], sp["hvec"], sp["hvec"], sp["state"], sp["chan"],
                  pl.BlockSpec((Q, GC), lambda g, c: (rc(c), 2 * GROUPS + g)),
                  pl.BlockSpec((Q, GC), lambda g, c: (rc(c), GROUPS + g)), sp["cvec"], sp["cvec"],
                  pl.BlockSpec((GC, LANES), lambda g, c: (0, 0))],
        out_specs=[sp["chan"],
                   pl.BlockSpec((Q, STATE), lambda g, c: (rc(c), g)),
                   pl.BlockSpec((Q, STATE), lambda g, c: (rc(c), g)),
                   sp["head"], acc, acc,
                   pl.BlockSpec((Q, GC), lambda g, c: (rc(c), 2 * GROUPS + g)), cacc, cacc],
        out_shape=[_sds((L, D), F32), _sds((L, GROUPS * STATE), F32), _sds((L, GROUPS * STATE), F32),
                   _sds((L, GROUPS * LANES), BF16), _sds((8, GROUPS * LANES), F32), _sds((8, GROUPS * LANES), F32),
                   _sds((L, proj.shape[1]), BF16), _sds((8, D), F32), _sds((8, D), F32)],
        scratch_shapes=[pltpu.VMEM((STATE, GC), F32), pltpu.VMEM((Q, GC), F32)], name="scan_bwd",
        sem=("parallel", "arbitrary"),
        args=(pre, pre, pre, dtx, eax, dsx, cdx, acs, acst, dt, dtraw, bias, alog, states, y, proj, dmix, nw, dexp,
              collapse),
        jobs=jobs)


def _adam_math(gv, w, m, v):
    c1 = 1.0 - ADAM_B1 ** ADAM_STEP
    c2 = 1.0 - ADAM_B2 ** ADAM_STEP
    nm = ADAM_B1 * m + (1.0 - ADAM_B1) * gv
    nv = ADAM_B2 * v + (1.0 - ADAM_B2) * (gv * gv)
    return -ADAM_LR * ((nm / c1) / (jnp.sqrt(nv / c2) + ADAM_EPS) + ADAM_WD * w), nm, nv


def adamw(g, w, m, v, name):
    R, C = g.shape
    T = R if R <= 128 else 128
    assert R % T == 0

    def body(g_ref, w_ref, m_ref, v_ref, d_ref, nm_ref, nv_ref):
        d_ref[...], nm_ref[...], nv_ref[...] = _adam_math(g_ref[...], w_ref[...], m_ref[...], v_ref[...])

    blk = pl.BlockSpec((T, C), lambda i: (i, 0))
    return pl.pallas_call(
        body, grid=(R // T,), in_specs=[blk] * 4, out_specs=[blk] * 3,
        out_shape=[_sds((R, C), F32)] * 3, name=name, compiler_params=_params("parallel"))(g, w, m, v)


def adamw_layer(g, w, m, v, layer, prev, name):
    R, C = g.shape
    T = _rows_tile(R, ROW_TILE)
    assert R % T == 0

    def body(g_ref, w_ref, m_ref, v_ref, *rest):
        go_ref, d_ref, nm_ref, nv_ref = rest[-4:]
        gv = g_ref[...]
        go_ref[0] = gv
        d_ref[0], nm_ref[0], nv_ref[0] = _adam_math(gv, w_ref[0], m_ref[0], v_ref[0])

    mine = pl.BlockSpec((1, T, C), lambda i: (layer, i, 0))
    prev = list(prev or [])
    return pl.pallas_call(
        body, grid=(R // T,), in_specs=[pl.BlockSpec((T, C), lambda i: (i, 0)), mine, mine, mine] + [_ANY] * len(prev),
        out_specs=[mine] * 4, out_shape=[_sds(w.shape, F32)] * 4,
        input_output_aliases={4 + i: i for i in range(len(prev))}, name=name,
        compiler_params=_params("parallel"))(g, w, m, v, *prev)


def cast_bf16(a, layer):
    _, R, C = a.shape
    T = _rows_tile(R, ROW_TILE)

    def body(a_ref, o_ref):
        o_ref[...] = a_ref[0].astype(BF16)

    return pl.pallas_call(
        body, grid=(R // T,), in_specs=[pl.BlockSpec((1, T, C), lambda i: (layer, i, 0))],
        out_specs=pl.BlockSpec((T, C), lambda i: (i, 0)), out_shape=_sds((R, C), BF16),
        name="cast_bf16", compiler_params=_params("parallel"))(a)


def _to_groups(v, hpg):
    lead = v.shape[:-1]
    t = v.reshape(lead + (GROUPS, hpg))
    t = jnp.pad(t, [(0, 0)] * (len(lead) + 1) + [(0, LANES - hpg)])
    return t.reshape(lead + (GROUPS * LANES,))


def _from_groups(a, hpg):
    lead = a.shape[:-1]
    return a.reshape(lead + (GROUPS, LANES))[..., :hpg].reshape(lead + (GROUPS * hpg,))


def _expand_matrix(D):
    gc = D // GROUPS
    return (jnp.arange(LANES)[:, None] == (jnp.arange(gc)[None, :] // HEAD_DIM)).astype(BF16)


def layer_params(pre_w, w_in_t, mixw, scale, cw, cb, bias, alog, dskip, nw, w_out_full, post_w, D):
    hpg = D // GROUPS // HEAD_DIM
    main = w_in_t.shape[0] - GROUPS * hpg
    wdt_t = _to_groups(w_in_t[main:].T, hpg).T
    return dict(
        pre_w=pre_w[None], win_t=w_in_t, main=main, wdt_t=wdt_t, mixw=mixw,
        scale=scale[None], cw=cw, cb=cb[None], bias=_to_groups(bias, hpg)[None], alog=_to_groups(alog, hpg)[None],
        dexp=jnp.repeat(dskip, HEAD_DIM)[None], nw=nw[None], wout=w_out_full, post_w=post_w[None])


def layer_fwd(x, p, D, next_shards=None, h=None, next_pre_w=None, own_rest=None):
    nxt = bool(next_shards)
    parts = lambda a: ((a.shape[0] // 32) * 16, a.shape[0] - (a.shape[0] // 32) * 16)
    if nxt:
        s_in, s_out, s_mix, s_cw = next_shards
        (in_a, in_b), (out_a, out_b) = parts(s_in), parts(s_out)
    if h is None:
        h = rms_fwd(x, p["pre_w"])
    own = own_rest is not None
    proj, got = matmul(h, p["win_t"], "nt", F32, "proj", tm=1024, n_out=p["main"],
                       jobs=([gather_rows_job(s_in, 0, in_a)] if nxt else [])
                       + ([gather_send_job(own_rest[0], own_rest[1])] if own else []))
    g_in = got[0][0] if nxt else None
    dtraw, got = matmul(h, p["wdt_t"], "nt", F32, "dtproj",
                        jobs=[gather_pass_job(got[-1], own_rest[1])] if own else [])
    if own:
        p.update(own_rest[2](got[0]))
    mixed, pooled = pool_fwd(proj, p["mixw"], p["scale"], D)
    pre = conv_fwd(proj, p["cw"], p["cb"], D)
    dtp, got = dt_prep(dtraw, p["bias"], p["alog"], _expand_matrix(D), D,
                       jobs=[gather_rows_job(s_out, 0, out_a)] if nxt else [])
    g_out = got[0][0] if nxt else None
    dt, acs, acst, dtx, eax, dsx, cdx = dtp
    (y, states, mixed), got = scan_fwd(pre, dtx, eax, dsx, cdx, acs, acst, proj, p["dexp"], p["nw"], mixed, D,
                                       jobs=[gather_rows_job(s_in, in_a, in_b, g_in)] if nxt else [])
    g_in = got[0][0] if nxt else None
    out, got = matmul(mixed, p["wout"], "nn", F32, "outproj", tm=1024,
                      jobs=[gather_rows_job(s_out, out_a, out_b, g_out), gather_send_job([s_mix, s_cw]),
                            gather_pass_job([g_in], [True])] if nxt else [])
    (xn, h_next), got2 = post_fwd(x, out, p["post_w"], next_pre_w,
                                  jobs=[gather_pass_job(got[0] + got[1], [True, False, False])] if nxt else [])
    gathered = got[2] + got2[0] if nxt else None
    return xn, dict(x=x, h=h, proj=proj, dtraw=dtraw, pooled=pooled, pre=pre, dtp=dtp, y=y, states=states,
                    mixed=mixed, out=out), gathered, h_next


def layer_bwd(dxn, p, s, D, where=None, post=None, below=None, pending=None):
    reduce = where is not None
    chip, core = where if reduce else (None, None)
    hpg = D // GROUPS // HEAD_DIM
    PGW = D // GROUPS
    main = p["main"]
    SH = (main + GROUPS * hpg) // 4
    dt, acs, acst, dtx, eax, dsx, cdx = s["dtp"]
    dout, d_post = post if post else post_bwd(s["out"], p["post_w"], dxn)
    part = BF16 if reduce else F32
    d_wout, got = matmul(s["mixed"], dout, "tn", part, "dwout", tm=1024,
                         jobs=[pair_gather_job(pending, [True, False])] if pending else [])
    above = dict(w_in=got[0][0], pool_mix_w=got[0][1]) if pending else None
    g_out = d_wout.reshape(4, 2, D // 4, D)
    dmix, got = matmul(dout, p["wout"], "nt", F32, "dmixed", tm=1024,
                       jobs=[pair_exchange_job([g_out])] if reduce else [])
    pair_out = pair_add(g_out, got[0][0], core, BF16) if reduce else None
    (dxs, db, dc, ddtraw, d_alog, d_bias, dproj, d_nw, d_dexp), got = scan_bwd(
        s["pre"], dtx, eax, dsx, cdx, acs, acst, dt, s["dtraw"], p["bias"], p["alog"], s["states"], s["y"],
        s["proj"], dmix, p["nw"], p["dexp"], _expand_matrix(D).T, D,
        jobs=[chip_exchange_job([pair_out])] if reduce else [])
    mine_out = chip_add(pair_out, got[0][0], chip, core) if reduce else None
    (dproj, d_cw, d_cb), got = conv_bwd([dxs, db, dc], s["proj"], p["cw"], dproj, D,
                                        jobs=[pair_gather_job([mine_out])] if reduce else [])
    r_out = got[0][0] if reduce else None
    dproj, d_mixw, d_scale = pool_bwd(s["proj"], dmix, s["pooled"], p["mixw"], p["scale"], dproj, D)
    d_wmain_t, _ = matmul(dproj, s["h"], "tn", part, "dwmain", tm=1024)
    d_wdt_t, _ = matmul(ddtraw, s["h"], "tn", part, "dwdt")
    late = [d_wmain_t[None], d_wdt_t[None],
            d_mixw.reshape(GROUPS, 4, PGW // 4, PGW).transpose(1, 0, 2, 3).reshape(4, 2, GROUPS * PGW // 8, PGW)]
    cols = [True, True, False]
    dh_b, got = matmul(ddtraw, p["wdt_t"], "nn", F32, "dh_dt",
                       jobs=[pair_exchange_job(late, cols)] if reduce else [])
    if reduce:
        p_main, p_dt, p_mix = [pair_add(g, r, core, BF16, k, n) for g, r, k, n in
                               zip(late, got[0], cols, [4 * SH, None, None])]
        p_in = lax.dynamic_update_slice(p_main[0], _from_groups(p_dt[0].T, hpg).T, (main, 0))
        pairs = [p_in.reshape(4, SH, D // 2), p_mix]
    cut = (SH * 5 // 128) * 16
    dh_a, got = matmul(dproj, p["win_t"], "nn", F32, "dh_main",
                       jobs=[chip_exchange_rows_job(pairs[0], 0, cut), chip_exchange_job(pairs[1:])] if reduce else [])
    (dx, d_pre, *post_below), got2 = rms_bwd(
        s["x"], p["pre_w"], dh_a, dh_b, dxn, below,
        jobs=[chip_exchange_rows_job(pairs[0], cut, SH - cut, got[0][0])] if reduce else [])
    mines = [chip_add(pairs[0], got2[0][0], chip, core, True),
             chip_add(pairs[1], got[1][0], chip, core, False)] if reduce else None
    reduced = dict(w_out=r_out) if reduce else None
    grads = dict(
        pre_norm_w=d_pre[0], pool_scale=d_scale[0], conv_w=d_cw[:CONV_K], conv_b=d_cb[0],
        dt_bias=_from_groups(d_bias[0], hpg), a_log=_from_groups(d_alog[0], hpg),
        d_skip=d_dexp[0].reshape(-1, HEAD_DIM).sum(axis=-1), ssd_norm_w=d_nw[0], post_norm_w=d_post[0])
    if not reduce:
        grads.update(w_in=jnp.concatenate([d_wmain_t.T, _from_groups(d_wdt_t.T, hpg)], axis=1), pool_mix_w=d_mixw,
                     w_out=d_wout)
    return dx, grads, reduced, tuple(post_below) or None, mines, above


def local_step(x, target, params, D):
    saved, h, n = [], None, len(params)
    for l, p in enumerate(params):
        x, s, _, h = layer_fwd(x, p, D, h=h, next_pre_w=params[l + 1]["pre_w"] if l + 1 < n else None)
        saved.append(s)
    dx, sumsq = loss_head(x, target)
    grads, post = [None] * n, None
    for l in reversed(range(n)):
        below = (saved[l - 1]["out"], params[l - 1]["post_w"]) if l else None
        dx, grads[l], _, post, _, _ = layer_bwd(dx, params[l], saved[l], D, post=post, below=below)
    return sumsq, dx, grads


SMALL = ("pre_norm_w", "pool_scale", "conv_w", "conv_b", "dt_bias", "a_log", "d_skip", "ssd_norm_w", "post_norm_w")
BIG = ("w_in", "w_out", "pool_mix_w")


def _pack(parts):
    flat = jnp.concatenate([p.reshape(-1) for p in parts])
    n = flat.shape[0]
    rows = -(-n // (LANES * LANES)) * LANES
    return jnp.pad(flat, (0, rows * LANES - n)).reshape(rows, LANES)


def _unpack(packed, shapes):
    flat, out, at = packed.reshape(-1), [], 0
    for s in shapes:
        n = math.prod(s)
        out.append(flat[at:at + n].reshape(s))
        at += n
    return out


def kernel(x, pre_norm_w, w_in, pool_mix_w, pool_scale, conv_w, conv_b, dt_bias, a_log, d_skip, ssd_norm_w, w_out, post_norm_w, loss_target, m_pre_norm_w, m_w_in, m_pool_mix_w, m_pool_scale, m_conv_w, m_conv_b, m_dt_bias, m_a_log, m_d_skip, m_ssd_norm_w, m_w_out, m_post_norm_w, v_pre_norm_w, v_w_in, v_pool_mix_w, v_pool_scale, v_conv_w, v_conv_b, v_dt_bias, v_a_log, v_d_skip, v_ssd_norm_w, v_w_out, v_post_norm_w):
    NL, D, SH = w_in.shape
    PGW = D // GROUPS
    CS = conv_w.shape[2]
    chip = (2 * lax.axis_index("x") + lax.axis_index("y")).astype(jnp.int32)
    chip1, core = chip.reshape(1), lax.axis_index("c").astype(jnp.int32).reshape(1)

    tr = lambda t: jnp.transpose(t, (0, 2, 1))
    w_in_t, m_w_in_t, v_w_in_t = tr(w_in), tr(m_w_in), tr(v_w_in)
    halved_by_cols = [True, True, False, False]

    def shards(l):
        return [cast_bf16(w_in_t, l), cast_bf16(w_out, l),
                cast_bf16(pool_mix_w.reshape(NL, GROUPS * PGW // 4, PGW), l).reshape(2, GROUPS * PGW // 8, PGW),
                conv_w[l].reshape(2, CONV_K * CS // (2 * LANES), LANES)]

    def rest_params(g):
        g_out, g_mix, g_cw = g
        return dict(wout=g_out.reshape(2 * D, D),
                    mixw=g_mix.reshape(4, GROUPS, PGW // 4, PGW).transpose(1, 0, 2, 3).reshape(GROUPS, PGW, PGW),
                    cw=g_cw.reshape(4, CONV_K, CS).transpose(1, 0, 2).reshape(CONV_K, 4 * CS))

    def params(l, g):
        late = rest_params(g[1:]) if len(g) > 1 else dict(wout=None, mixw=None, cw=None)
        return layer_params(pre_norm_w[l], g[0].reshape(4 * SH, D), late["mixw"], pool_scale[l], late["cw"], conv_b[l],
                            dt_bias[l], a_log[l], d_skip[l], ssd_norm_w[l], late["wout"], post_norm_w[l], D)

    first = shards(0)
    gathered = run_jobs([gather_send_job(first[:1], [True])], "gather_send")[0]
    gathered = run_jobs([gather_pass_job(gathered, [True])], "gather_pass")[0]
    xl, h, ps, saved = x[0], None, [], []
    for l in range(NL):
        ps.append(params(l, gathered))
        last = l + 1 == NL
        xl, s, gathered, h = layer_fwd(xl, ps[l], D, None if last else shards(l + 1), h,
                                       None if last else pre_norm_w[l + 1][None],
                                       (first[1:], halved_by_cols[1:], rest_params) if l == 0 else None)
        saved.append(s)
    dx, sumsq = loss_head(xl, loss_target[0])

    given = dict(w_in=(w_in_t, m_w_in_t, v_w_in_t), w_out=(w_out, m_w_out, v_w_out),
                 pool_mix_w=(pool_mix_w, m_pool_mix_w, v_pool_mix_w))
    flat = {n: [t.reshape(NL, -1, t.shape[-1]) for t in given[n]] for n in BIG}
    done = {n: None for n in BIG}
    def update(l, reduced):
        for n, r in reduced.items():
            done[n] = adamw_layer(r.reshape(-1, r.shape[-1]), *flat[n], l, done[n], "adamw_" + n)

    grads, post, pending = [None] * NL, None, None
    for l in reversed(range(NL)):
        below = (saved[l - 1]["out"], ps[l - 1]["post_w"]) if l else None
        dx, grads[l], reduced, post, pending, above = layer_bwd(dx, ps[l], saved[l], D, (chip1, core), post, below,
                                                                pending)
        update(l, reduced)
        if above:
            update(l + 1, above)
    last = run_jobs([pair_gather_job(pending, [True, False])], "pair_gather")[0]
    update(0, dict(w_in=last[0], pool_mix_w=last[1]))

    small_shapes = [(NL,) + grads[0][n].shape for n in SMALL]
    packed = _pack([0.5 / D * sumsq[0, :1]] + [jnp.stack([g[n] for g in grads]) for n in SMALL])
    total = allreduce_small(packed)
    loss, *small = _unpack(total, [(1,)] + small_shapes)
    small = dict(zip(SMALL, small))
    small["conv_w"] = lax.dynamic_slice_in_dim(small["conv_w"], chip * CS, CS, axis=2)

    given_small = dict(
        pre_norm_w=(pre_norm_w, m_pre_norm_w, v_pre_norm_w), pool_scale=(pool_scale, m_pool_scale, v_pool_scale),
        conv_w=(conv_w, m_conv_w, v_conv_w), conv_b=(conv_b, m_conv_b, v_conv_b),
        dt_bias=(dt_bias, m_dt_bias, v_dt_bias), a_log=(a_log, m_a_log, v_a_log),
        d_skip=(d_skip, m_d_skip, v_d_skip), ssd_norm_w=(ssd_norm_w, m_ssd_norm_w, v_ssd_norm_w),
        post_norm_w=(post_norm_w, m_post_norm_w, v_post_norm_w))
    shapes = [given_small[n][0].shape for n in SMALL]
    upd = adamw(_pack([small[n] for n in SMALL]), *[_pack([given_small[n][i] for n in SMALL]) for i in range(3)],
                "adamw_small")
    upd = [dict(zip(SMALL, _unpack(u, shapes))) for u in upd]

    out = {n: (small[n], upd[0][n], upd[1][n], upd[2][n]) for n in SMALL}
    for n in BIG:
        out[n] = tuple(t.reshape(given[n][0].shape) for t in done[n])
    out["w_in"] = tuple(tr(t) for t in out["w_in"])

    order = ("pre_norm_w", "w_in", "pool_mix_w", "pool_scale", "conv_w", "conv_b", "dt_bias", "a_log", "d_skip",
             "ssd_norm_w", "w_out", "post_norm_w")
    return (loss.reshape(()), dx[None], *[out[n][0] for n in order], *[out[n][1] for n in order],
            *[out[n][2] for n in order], *[out[n][3] for n in order])
```

```python
import functools
import math

import jax
import jax.numpy as jnp
from jax import lax
from jax.experimental import pallas as pl
from jax.experimental.pallas import tpu as pltpu

F32 = jnp.float32
BF16 = jnp.bfloat16

NORM_EPS = 1e-6
HEAD_DIM = 64
STATE = 128
GROUPS = 4
POOL_WINDOWS = (2, 4, 8, 16)
POOL_HALO = 16
CONV_K = 4
CONV_HALO = 8
SCAN_CHUNK = 256
LANES = 128
VMEM_LIMIT = 52 * 1024 * 1024
ROW_TILE = 256

ADAM_LR = 0.001
ADAM_B1 = 0.9
ADAM_B2 = 0.999
ADAM_EPS = 1e-08
ADAM_WD = 0.01
ADAM_STEP = 10

MESH = pl.DeviceIdType.MESH

NN = (((1,), (0,)), ((), ()))
NT = (((1,), (1,)), ((), ()))
TN = (((0,), (0,)), ((), ()))

_ANY = pl.BlockSpec(memory_space=pl.ANY)


def _params(*sem):
    return pltpu.CompilerParams(dimension_semantics=sem, vmem_limit_bytes=VMEM_LIMIT)


def _pick(dim, pref):
    if dim <= pref:
        return dim
    t = (pref // LANES) * LANES
    while t > LANES and dim % t:
        t -= LANES
    assert dim % t == 0, (dim, pref)
    return t


def _rows_tile(rows, pref):
    t = (min(pref, rows) // 8) * 8
    while t >= 8 and rows % t:
        t -= 8
    return t if t >= 8 else rows


def _dot(a, b, dn=NN):
    return lax.dot_general(a, b, dn, preferred_element_type=F32)


def _split3(a):
    hi = a.astype(BF16)
    r = a - hi.astype(F32)
    mid = r.astype(BF16)
    return hi, mid, (r - mid.astype(F32)).astype(BF16)


def _dot_sel(a, e, parts=3):
    hi, mid, lo = _split3(a)
    return (_dot(lo, e) + _dot(mid, e)) + _dot(hi, e) if parts == 3 else _dot(mid, e) + _dot(hi, e)


def _sel_dot(e, b):
    hi, mid, lo = _split3(b)
    return (_dot(e, lo) + _dot(e, mid)) + _dot(e, hi)


def _sigmoid(v):
    return 0.5 * jnp.tanh(0.5 * v) + 0.5


def _silu_and_grad(v):
    s = _sigmoid(v)
    return v * s, s * (1.0 + v * (1.0 - s))


def _row(i, shape):
    return lax.broadcasted_iota(jnp.int32, shape, 0) + i


def _sds(shape, dtype):
    return jax.ShapeDtypeStruct(tuple(shape), dtype)


class Job:
    def __init__(self, ins, outs, aliased, nsem, start, finish):
        self.ins, self.outs, self.aliased, self.nsem, self.start, self.finish = ins, outs, aliased, nsem, start, finish


def _place():
    x, y, c = lax.axis_index("x"), lax.axis_index("y"), lax.axis_index("c")
    return x, y, c, [(1 - x, y), (x, 1 - y), (1 - x, 1 - y)]


def _remote(src, dst, send_sem, recv_sem, device):
    return pltpu.make_async_remote_copy(src_ref=src, dst_ref=dst, send_sem=send_sem, recv_sem=recv_sem,
                                        device_id=device, device_id_type=MESH)


def _half(ref, c, cols, lead=0):
    idx = [slice(None)] * lead
    if cols:
        w = ref.shape[-1] // 2
        idx += [slice(None)] * (len(ref.shape) - lead - 1) + [pl.ds(pl.multiple_of(c * w, LANES), w)]
    else:
        idx += [c]
    return ref.at[tuple(idx)]


def _flags(cols, n):
    return list(cols) if cols else [False] * n


def gather_send_job(arrs, cols=None):
    n = len(arrs)
    cols = _flags(cols, n)

    def copies(ins, outs, send, recv):
        x, y, c, chips = _place()
        mine = 2 * x + y
        out = []
        for a in range(n):
            out.append(_remote(ins[a], outs[a].at[mine], send.at[4 * a + 3], recv.at[4 * a + 3], (x, y, 1 - c)))
            for j, chip in enumerate(chips):
                out.append(_remote(_half(ins[a], c, cols[a]), _half(outs[a].at[mine], c, cols[a]),
                                   send.at[4 * a + j], recv.at[4 * a + j], (*chip, c)))
        return out

    def start(ins, outs, send, recv):
        for cp in copies(ins, outs, send, recv):
            cp.start()

    def finish(ins, outs, send, recv):
        x, y, c, chips = _place()
        for a in range(n):
            for j, chip in enumerate(chips):
                landed = _half(outs[a].at[2 * chip[0] + chip[1]], c, cols[a])
                _remote(landed, landed, send.at[4 * a + j], recv.at[4 * a + j], (x, y, 1 - c)).wait_recv()
            twin = outs[a].at[2 * x + y]
            _remote(twin, twin, send.at[4 * a + 3], recv.at[4 * a + 3], (x, y, 1 - c)).wait_recv()
        for cp in copies(ins, outs, send, recv):
            cp.wait_send()

    return Job(list(arrs), [_sds((4,) + a.shape, a.dtype) for a in arrs], False, 4 * n, start, finish)


def gather_rows_job(shard, start, size, into=None):
    rows = pl.ds(start, size)

    def copies(ins, outs, send, recv):
        x, y, c, chips = _place()
        mine = outs[0].at[2 * x + y, rows]
        out = [_remote(ins[0].at[rows], mine, send.at[3], recv.at[3], (x, y, 1 - c))]
        for j, chip in enumerate(chips):
            out.append(_remote(_half(ins[0].at[rows], c, True), _half(mine, c, True), send.at[j], recv.at[j],
                               (*chip, c)))
        return out

    def start_(ins, outs, send, recv):
        for cp in copies(ins, outs, send, recv):
            cp.start()

    def finish(ins, outs, send, recv):
        x, y, c, chips = _place()
        for j, chip in enumerate(chips):
            landed = _half(outs[0].at[2 * chip[0] + chip[1], rows], c, True)
            _remote(landed, landed, send.at[j], recv.at[j], (x, y, 1 - c)).wait_recv()
        twin = outs[0].at[2 * x + y, rows]
        _remote(twin, twin, send.at[3], recv.at[3], (x, y, 1 - c)).wait_recv()
        for cp in copies(ins, outs, send, recv):
            cp.wait_send()

    ins = [shard] if into is None else [shard, into]
    return Job(ins, [_sds((4,) + shard.shape, shard.dtype)], {} if into is None else {1: 0}, 4, start_, finish)


def gather_pass_job(bufs, cols=None):
    n = len(bufs)
    cols = _flags(cols, n)

    def copies(outs, send, recv):
        x, y, c, chips = _place()
        out = []
        for a in range(n):
            for j, chip in enumerate(chips):
                landed = _half(outs[a].at[2 * chip[0] + chip[1]], c, cols[a])
                out.append(_remote(landed, landed, send.at[3 * a + j], recv.at[3 * a + j], (x, y, 1 - c)))
        return out

    def start(ins, outs, send, recv):
        for cp in copies(outs, send, recv):
            cp.start()

    def finish(ins, outs, send, recv):
        x, y, c, chips = _place()
        for a in range(n):
            for j, chip in enumerate(chips):
                passed = _half(outs[a].at[2 * chip[0] + chip[1]], 1 - c, cols[a])
                _remote(passed, passed, send.at[3 * a + j], recv.at[3 * a + j], (x, y, 1 - c)).wait_recv()
        for cp in copies(outs, send, recv):
            cp.wait_send()

    return Job(list(bufs), [_sds(b.shape, b.dtype) for b in bufs], True, 3 * n, start, finish)


def pair_exchange_job(arrs, cols=None):
    n = len(arrs)
    cols = _flags(cols, n)

    def copies(ins, outs, send, recv):
        x, y, c, _ = _place()
        return [_remote(_half(ins[a], 1 - c, cols[a], 1), outs[a], send.at[a], recv.at[a], (x, y, 1 - c))
                for a in range(n)]

    def start(ins, outs, send, recv):
        for cp in copies(ins, outs, send, recv):
            cp.start()

    def finish(ins, outs, send, recv):
        for cp in copies(ins, outs, send, recv):
            cp.wait()

    shape = lambda a, k: a.shape[:-1] + (a.shape[-1] // 2,) if k else a.shape[:1] + a.shape[2:]
    return Job(list(arrs), [_sds(shape(a, k), a.dtype) for a, k in zip(arrs, cols)], False, n, start, finish)


def chip_exchange_job(arrs):
    n = len(arrs)

    def copies(ins, outs, send, recv):
        x, y, c, chips = _place()
        return [_remote(ins[a].at[2 * chip[0] + chip[1]], outs[a].at[j], send.at[3 * a + j], recv.at[3 * a + j],
                        (*chip, c)) for a in range(n) for j, chip in enumerate(chips)]

    def start(ins, outs, send, recv):
        for cp in copies(ins, outs, send, recv):
            cp.start()

    def finish(ins, outs, send, recv):
        for cp in copies(ins, outs, send, recv):
            cp.wait()

    return Job(list(arrs), [_sds((3,) + a.shape[1:], a.dtype) for a in arrs], False, 3 * n, start, finish)


def chip_exchange_rows_job(p, start, size, into=None):
    rows = pl.ds(start, size)

    def copies(ins, outs, send, recv):
        x, y, c, chips = _place()
        return [_remote(ins[0].at[2 * chip[0] + chip[1], rows], outs[0].at[j, rows], send.at[j], recv.at[j], (*chip, c))
                for j, chip in enumerate(chips)]

    def start_(ins, outs, send, recv):
        for cp in copies(ins, outs, send, recv):
            cp.start()

    def finish(ins, outs, send, recv):
        for cp in copies(ins, outs, send, recv):
            cp.wait()

    ins = [p] if into is None else [p, into]
    return Job(ins, [_sds((3,) + p.shape[1:], p.dtype)], {} if into is None else {1: 0}, 3, start_, finish)


def pair_gather_job(bufs, cols=None):
    n = len(bufs)
    cols = _flags(cols, n)

    def copies(outs, send, recv):
        x, y, c, _ = _place()
        return [_remote(_half(outs[a], c, cols[a]), _half(outs[a], c, cols[a]), send.at[a], recv.at[a],
                        (x, y, 1 - c)) for a in range(n)]

    def start(ins, outs, send, recv):
        for cp in copies(outs, send, recv):
            cp.start()

    def finish(ins, outs, send, recv):
        for cp in copies(outs, send, recv):
            cp.wait()

    return Job(list(bufs), [_sds(b.shape, b.dtype) for b in bufs], True, n, start, finish)


def _call(body, *, grid, in_specs, out_specs, out_shape, name, sem, args, scratch_shapes=(), jobs=(), aliases=None):
    in_specs, out_specs, out_shape, scratch_shapes = list(in_specs), list(out_specs), list(out_shape), list(scratch_shapes)
    aliases = dict(aliases or {})
    n_in, n_out, n_scr = len(in_specs), len(out_specs), len(scratch_shapes)
    if jobs:
        sem = ("arbitrary",) * len(grid)
    at_in, at_out = n_in, n_out
    for j in jobs:
        pairs = j.aliased if isinstance(j.aliased, dict) else {i: i for i in range(len(j.ins))} if j.aliased else {}
        aliases.update({at_in + i: at_out + o for i, o in pairs.items()})
        at_in, at_out = at_in + len(j.ins), at_out + len(j.outs)

    def wrapped(*refs):
        ins, p = refs[:n_in], n_in
        jins = []
        for j in jobs:
            jins.append(refs[p:p + len(j.ins)])
            p += len(j.ins)
        outs, p = refs[p:p + n_out], p + n_out
        jouts = []
        for j in jobs:
            jouts.append(refs[p:p + len(j.outs)])
            p += len(j.outs)
        scr, sems = refs[p:p + n_scr], refs[p + n_scr:]

        def start():
            for k, j in enumerate(jobs):
                j.start(jins[k], jouts[k], sems[2 * k], sems[2 * k + 1])

        def finish():
            for k, j in enumerate(jobs):
                j.finish(jins[k], jouts[k], sems[2 * k], sems[2 * k + 1])

        if jobs and grid:
            ids = [pl.program_id(d) for d in range(len(grid))]
            pl.when(functools.reduce(jnp.logical_and, [i == 0 for i in ids]))(start)
            body(*ins, *outs, *scr)
            pl.when(functools.reduce(jnp.logical_and, [i == g - 1 for i, g in zip(ids, grid)]))(finish)
        else:
            start()
            body(*ins, *outs, *scr)
            finish()

    kwargs = dict(grid=grid) if grid else {}
    res = pl.pallas_call(
        wrapped, in_specs=in_specs + [_ANY] * (at_in - n_in), out_specs=out_specs + [_ANY] * (at_out - n_out),
        out_shape=out_shape + [o for j in jobs for o in j.outs],
        scratch_shapes=scratch_shapes + [pltpu.SemaphoreType.DMA((j.nsem,)) for j in jobs for _ in range(2)],
        input_output_aliases=aliases, name=name,
        compiler_params=pltpu.CompilerParams(dimension_semantics=sem, vmem_limit_bytes=VMEM_LIMIT) if grid
        else pltpu.CompilerParams(vmem_limit_bytes=VMEM_LIMIT), **kwargs)(*args, *[a for j in jobs for a in j.ins])
    res = list(res)
    outs, rest, per_job = res[:n_out], res[n_out:], []
    for j in jobs:
        per_job.append(rest[:len(j.outs)])
        rest = rest[len(j.outs):]
    return outs, per_job


def run_jobs(jobs, name):
    return _call(lambda: None, grid=(), in_specs=[], out_specs=[], out_shape=[], name=name, sem=(), args=(), jobs=jobs)[1]


def pair_add(g, r, core, out_dtype, cols=False, out_rows=None):
    S, R, C = r.shape
    T = _rows_tile(R, 2 * ROW_TILE if C <= 1024 else ROW_TILE if C <= 4096 else ROW_TILE // 2)

    def body(c_ref, g_ref, r_ref, o_ref):
        o_ref[0] = ((g_ref[0] if cols else g_ref[0, 0]).astype(F32) + r_ref[0].astype(F32)).astype(o_ref.dtype)

    g_spec = (pl.BlockSpec((1, T, C), lambda k, i, c_ref: (k, i, c_ref[0])) if cols
              else pl.BlockSpec((1, 1, T, C), lambda k, i, c_ref: (k, c_ref[0], i, 0)))
    return pl.pallas_call(
        body,
        grid_spec=pltpu.PrefetchScalarGridSpec(
            num_scalar_prefetch=1, grid=(S, R // T),
            in_specs=[g_spec, pl.BlockSpec((1, T, C), lambda k, i, c_ref: (k, i, 0))],
            out_specs=pl.BlockSpec((1, T, C), lambda k, i, c_ref: (k, i, 0))),
        out_shape=_sds((S, out_rows or R, C), out_dtype), name="pair_add",
        compiler_params=_params("parallel", "parallel"))(core, g, r)


def chip_add(p, r, chip, core, cols=False):
    _, R, C = p.shape
    T = _rows_tile(R, ROW_TILE)

    def body(k_ref, c_ref, p_ref, r0_ref, r1_ref, r2_ref, o_ref):
        s = ((p_ref[0].astype(F32) + r0_ref[0].astype(F32)) + r1_ref[0].astype(F32)) + r2_ref[0].astype(F32)
        if cols:
            o_ref[...] = s
        else:
            o_ref[0] = s

    slot = lambda j: pl.BlockSpec((1, T, C), lambda i, k_ref, c_ref: (j, i, 0))
    out_spec = (pl.BlockSpec((T, C), lambda i, k_ref, c_ref: (i, c_ref[0])) if cols
                else pl.BlockSpec((1, T, C), lambda i, k_ref, c_ref: (c_ref[0], i, 0)))
    return pl.pallas_call(
        body,
        grid_spec=pltpu.PrefetchScalarGridSpec(
            num_scalar_prefetch=2, grid=(R // T,),
            in_specs=[pl.BlockSpec((1, T, C), lambda i, k_ref, c_ref: (k_ref[0], i, 0)), slot(0), slot(1), slot(2)],
            out_specs=out_spec),
        out_shape=_sds((R, 2 * C) if cols else (2, R, C), F32), name="chip_add",
        compiler_params=_params("parallel"))(chip, core, p, r, r, r)


def allreduce_small(v):
    R = v.shape[0]

    def body(v_ref, o_ref, buf, send_sems, recv_sems, local_sem):
        x, y, c, chips = _place()
        me, sibling = (x, y, c), (x, y, 1 - c)

        def rows(px, py, pc):
            return buf.at[pl.ds((4 * px + 2 * py + pc) * R, R), :]

        def copy(k, block, to, src=None):
            return _remote(rows(*block) if src is None else src, rows(*block), send_sems.at[k], recv_sems.at[k], to)

        mine = pltpu.make_async_copy(v_ref, rows(*me), local_sem)
        mine.start()
        first = [copy(0, me, sibling, src=v_ref)]
        first += [copy(1 + j, me, (*chip, c), src=v_ref) for j, chip in enumerate(chips)]
        for cp in first:
            cp.start()
        passed = [copy(4 + j, (*chip, c), sibling) for j, chip in enumerate(chips)]
        for j, chip in enumerate(chips):
            copy(1 + j, (*chip, c), me).wait_recv()
            passed[j].start()
        copy(0, sibling, me).wait_recv()
        for j, chip in enumerate(chips):
            copy(4 + j, (*chip, 1 - c), me).wait_recv()
        for cp in first + passed:
            cp.wait_send()
        mine.wait()
        acc = buf[0:R, :]
        for d in range(1, 8):
            acc = acc + buf[d * R:(d + 1) * R, :]
        o_ref[...] = acc

    return pl.pallas_call(
        body, in_specs=[pl.BlockSpec(memory_space=pltpu.VMEM)], out_specs=pl.BlockSpec(memory_space=pltpu.VMEM),
        out_shape=_sds((R, LANES), F32),
        scratch_shapes=[pltpu.VMEM((8 * R, LANES), F32), pltpu.SemaphoreType.DMA((7,)),
                        pltpu.SemaphoreType.DMA((7,)), pltpu.SemaphoreType.DMA],
        name="allreduce_small", compiler_params=pltpu.CompilerParams(vmem_limit_bytes=VMEM_LIMIT))(v)


def matmul(a, b, mode, out_dtype, name, tm=512, tn=1024, tk=4608, jobs=(), n_out=None):
    if mode == "nn":
        (M, K), (K2, N) = a.shape, b.shape
    elif mode == "nt":
        (M, K), (N, K2) = a.shape, b.shape
        N = n_out or N
    else:
        (K, M), (K2, N) = a.shape, b.shape
    assert K == K2 or (mode == "nn" and K2 > K)
    tm, tn, tk = _pick(M, tm), _pick(N, tn), _pick(K, tk)
    nk = K // tk
    dn = {"nn": NN, "nt": NT, "tn": TN}[mode]

    def body(a_ref, b_ref, o_ref, *acc):
        part = _dot(a_ref[...].astype(BF16), b_ref[...].astype(BF16), dn)
        if nk == 1:
            o_ref[...] = part.astype(o_ref.dtype)
            return
        acc_ref, = acc
        k = pl.program_id(2)

        @pl.when(k == 0)
        def _():
            acc_ref[...] = part

        @pl.when(jnp.logical_and(k > 0, k < nk - 1))
        def _():
            acc_ref[...] += part

        @pl.when(k == nk - 1)
        def _():
            o_ref[...] = (acc_ref[...] + part).astype(o_ref.dtype)

    a_spec = (pl.BlockSpec((tk, tm), lambda i, j, k: (k, i)) if mode == "tn"
              else pl.BlockSpec((tm, tk), lambda i, j, k: (i, k)))
    b_spec = (pl.BlockSpec((tn, tk), lambda i, j, k: (j, k)) if mode == "nt"
              else pl.BlockSpec((tk, tn), lambda i, j, k: (k, j)))
    outs, per_job = _call(
        body, grid=(M // tm, N // tn, nk), in_specs=[a_spec, b_spec],
        out_specs=[pl.BlockSpec((tm, tn), lambda i, j, k: (i, j))], out_shape=[_sds((M, N), out_dtype)],
        scratch_shapes=[pltpu.VMEM((tm, tn), F32)] if nk > 1 else [], name=name,
        sem=("parallel", "parallel", "arbitrary"), args=(a, b), jobs=jobs)
    return outs[0], per_job


def rms_fwd(x, w):
    L, D = x.shape
    T = _pick(L, ROW_TILE)

    def body(x_ref, w_ref, h_ref):
        xv = x_ref[...]
        r = lax.rsqrt(jnp.mean(xv * xv, axis=-1, keepdims=True) + NORM_EPS)
        h_ref[...] = (xv * r * w_ref[...]).astype(h_ref.dtype)

    return pl.pallas_call(
        body, grid=(L // T,),
        in_specs=[pl.BlockSpec((T, D), lambda i: (i, 0)), pl.BlockSpec((1, D), lambda i: (0, 0))],
        out_specs=pl.BlockSpec((T, D), lambda i: (i, 0)),
        out_shape=_sds((L, D), BF16), name="rms_fwd", compiler_params=_params("parallel"))(x, w)


def post_fwd(x, o, w, next_w=None, jobs=()):
    L, D = x.shape
    T = _pick(L, ROW_TILE)

    def body(x_ref, o_ref, w_ref, *rest):
        ov = o_ref[...]
        r = lax.rsqrt(jnp.mean(ov * ov, axis=-1, keepdims=True) + NORM_EPS)
        y = x_ref[...] + ov * r * w_ref[...]
        rest[-1 if next_w is None else -2][...] = y
        if next_w is not None:
            r2 = lax.rsqrt(jnp.mean(y * y, axis=-1, keepdims=True) + NORM_EPS)
            rest[-1][...] = (y * r2 * rest[0][...]).astype(BF16)

    row = pl.BlockSpec((T, D), lambda i: (i, 0))
    vec = pl.BlockSpec((1, D), lambda i: (0, 0))
    more = [] if next_w is None else [next_w]
    outs, per_job = _call(
        body, grid=(L // T,), in_specs=[row, row, vec] + [vec] * len(more), out_specs=[row] * (1 + len(more)),
        out_shape=[_sds((L, D), F32)] + [_sds((L, D), BF16)] * len(more), name="post_fwd", sem=("parallel",),
        args=(x, o, w, *more), jobs=jobs)
    return (outs[0], outs[1] if more else None), per_job


def _rms_bwd_math(xv, w, dy):
    r = lax.rsqrt(jnp.mean(xv * xv, axis=-1, keepdims=True) + NORM_EPS)
    xhat = xv * r
    g = dy * w
    dx = r * (g - xhat * jnp.mean(g * xhat, axis=-1, keepdims=True))
    return dx, jnp.sum(dy * xhat, axis=0, keepdims=True)


def post_bwd(o, w, dxn):
    L, D = o.shape
    T = _pick(L, ROW_TILE)

    def body(o_ref, w_ref, d_ref, do_ref, dw_ref):
        dx, dw = _rms_bwd_math(o_ref[...], w_ref[...], d_ref[...])
        do_ref[...] = dx.astype(do_ref.dtype)

        @pl.when(pl.program_id(0) == 0)
        def _():
            dw_ref[...] = jnp.zeros_like(dw_ref)

        dw_ref[...] += dw

    row = pl.BlockSpec((T, D), lambda i: (i, 0))
    vec = pl.BlockSpec((1, D), lambda i: (0, 0))
    return pl.pallas_call(
        body, grid=(L // T,), in_specs=[row, vec, row], out_specs=[row, vec],
        out_shape=[_sds((L, D), BF16), _sds((1, D), F32)],
        name="post_bwd", compiler_params=_params("arbitrary"))(o, w, dxn)


def rms_bwd(x, w, dh_a, dh_b, dxn, below=None, jobs=()):
    L, D = x.shape
    T = _pick(L, ROW_TILE)

    def body(x_ref, w_ref, a_ref, b_ref, d_ref, *rest):
        dx, dw = _rms_bwd_math(x_ref[...], w_ref[...], a_ref[...] + b_ref[...])
        dx = d_ref[...] + dx
        outs = rest[2:] if below else rest
        outs[0][...] = dx

        @pl.when(pl.program_id(0) == 0)
        def _():
            for acc in outs[1::2]:
                acc[...] = jnp.zeros_like(acc)

        outs[1][...] += dw
        if below:
            do, dwp = _rms_bwd_math(rest[0][...], rest[1][...], dx)
            outs[2][...] = do.astype(BF16)
            outs[3][...] += dwp

    row = pl.BlockSpec((T, D), lambda i: (i, 0))
    vec = pl.BlockSpec((1, D), lambda i: (0, 0))
    more = list(below) if below else []
    return _call(
        body, grid=(L // T,), in_specs=[row, vec, row, row, row] + ([row, vec] if below else []),
        out_specs=[row, vec] + ([row, vec] if below else []),
        out_shape=[_sds((L, D), F32), _sds((1, D), F32)] + ([_sds((L, D), BF16), _sds((1, D), F32)] if below else []),
        name="rms_bwd", sem=("arbitrary",), args=(x, w, dh_a, dh_b, dxn, *more), jobs=jobs)


def loss_head(y, target):
    L, D = y.shape
    T = _pick(L, ROW_TILE)

    def body(y_ref, t_ref, d_ref, s_ref):
        e = y_ref[...] - t_ref[...]
        d_ref[...] = e * (1.0 / D)

        @pl.when(pl.program_id(0) == 0)
        def _():
            s_ref[...] = jnp.zeros_like(s_ref)

        s_ref[...] += jnp.sum(e * e)

    row = pl.BlockSpec((T, D), lambda i: (i, 0))
    return pl.pallas_call(
        body, grid=(L // T,), in_specs=[row, row],
        out_specs=[row, pl.BlockSpec((8, LANES), lambda i: (0, 0))],
        out_shape=[_sds((L, D), F32), _sds((8, LANES), F32)],
        name="loss_head", compiler_params=_params("arbitrary"))(y, target)


def _window_sums(xe, w, back):
    n = xe.shape[0]
    s, k = xe, 1
    while k < w:
        s = s + pltpu.roll(s, k if back else n - k, 0)
        k *= 2
    return s


def pool_fwd(proj, mixw, scale, D):
    L = proj.shape[0]
    PGW = D // GROUPS
    T = _pick(L, ROW_TILE)
    hb = T // POOL_HALO

    def body(u_ref, halo_ref, g_ref, mw_ref, sc_ref, y_ref, p_ref):
        i = pl.program_id(0)
        u = u_ref[...]
        halo = jnp.where(i > 0, halo_ref[...], 0.0)
        xe = jnp.concatenate([halo, u], axis=0)
        t1 = _row(i * T + 1, (T, 1))
        for g, w in enumerate(POOL_WINDOWS):
            sl = slice(g * PGW, (g + 1) * PGW)
            win = _window_sums(xe[:, sl], w, True)[POOL_HALO:, :]
            cnt = jnp.minimum(t1, w).astype(F32)
            pooled = (win / cnt - u[:, sl]).astype(BF16)
            p_ref[:, sl] = pooled
            mixed = _dot(pooled, mw_ref[g])
            gate = g_ref[:, sl]
            y_ref[:, sl] = (mixed * sc_ref[:, sl] * (gate * _sigmoid(gate))).astype(BF16)

    return pl.pallas_call(
        body, grid=(L // T,),
        in_specs=[pl.BlockSpec((T, D), lambda i: (i, 0)),
                  pl.BlockSpec((POOL_HALO, D), lambda i: (jnp.maximum(i * hb - 1, 0), 0)),
                  pl.BlockSpec((T, D), lambda i: (i, 1)),
                  pl.BlockSpec((GROUPS, PGW, PGW), lambda i: (0, 0, 0)),
                  pl.BlockSpec((1, D), lambda i: (0, 0))],
        out_specs=[pl.BlockSpec((T, D), lambda i: (i, 0)), pl.BlockSpec((T, D), lambda i: (i, 0))],
        out_shape=[_sds((L, 2 * D), BF16), _sds((L, D), BF16)],
        name="pool_fwd", compiler_params=_params("parallel"))(proj, proj, proj, mixw, scale)


def pool_bwd(proj, dmix, pooled, mixw, scale, dproj, D):
    L = proj.shape[0]
    PGW = D // GROUPS
    T = _pick(L, ROW_TILE)
    hb = T // POOL_HALO
    nT = L // T

    def body(g_ref, gh_ref, dy_ref, dyh_ref, p_ref, mw_ref, sc_ref, old_ref, dp_ref, dm_ref, ds_ref):
        i = pl.program_id(0)
        t1 = _row(i * T + 1, (T, 1))
        th1 = _row((i + 1) * T + 1, (POOL_HALO, 1))
        live = i < nT - 1

        @pl.when(i == 0)
        def _():
            ds_ref[...] = jnp.zeros_like(ds_ref)
            dm_ref[...] = jnp.zeros_like(dm_ref)

        for g, w in enumerate(POOL_WINDOWS):
            sl = slice(g * PGW, (g + 1) * PGW)
            sc = sc_ref[:, sl]
            gate, dy = g_ref[:, sl], dy_ref[:, sl]
            sg, dsg = _silu_and_grad(gate)
            pooled = p_ref[:, sl]
            mixed = _dot(pooled, mw_ref[g])
            dmixed = (dy * sc * sg).astype(BF16)
            dm_ref[g] += _dot(pooled, dmixed, TN)
            dp_ref[:, D + g * PGW:D + (g + 1) * PGW] = (dy * mixed * sc * dsg).astype(BF16)
            ds_ref[:, sl] += jnp.sum(dy * mixed * sg, axis=0, keepdims=True)
            dpool = _dot(dmixed, mw_ref[g], NT)
            gate_h = gh_ref[:, sl]
            dmixed_h = (dyh_ref[:, sl] * sc * (gate_h * _sigmoid(gate_h))).astype(BF16)
            dpool_h = jnp.where(live, _dot(dmixed_h, mw_ref[g], NT), 0.0)
            q = dpool / jnp.minimum(t1, w).astype(F32)
            q_h = dpool_h / jnp.minimum(th1, w).astype(F32)
            qe = jnp.concatenate([q, q_h], axis=0)
            dp_ref[:, sl] = (_window_sums(qe, w, False)[:T, :] - dpool).astype(BF16)

    nxt = lambda i: jnp.minimum((i + 1) * hb, L // POOL_HALO - 1)
    row = lambda c: pl.BlockSpec((T, D), lambda i: (i, c))
    return pl.pallas_call(
        body, grid=(nT,),
        in_specs=[row(1), pl.BlockSpec((POOL_HALO, D), lambda i: (nxt(i), 1)),
                  row(0), pl.BlockSpec((POOL_HALO, D), lambda i: (nxt(i), 0)),
                  row(0), pl.BlockSpec((GROUPS, PGW, PGW), lambda i: (0, 0, 0)),
                  pl.BlockSpec((1, D), lambda i: (0, 0)), _ANY],
        out_specs=[pl.BlockSpec((T, 2 * D), lambda i: (i, 0)), pl.BlockSpec((GROUPS, PGW, PGW), lambda i: (0, 0, 0)),
                   pl.BlockSpec((1, D), lambda i: (0, 0))],
        out_shape=[_sds(dproj.shape, dproj.dtype), _sds((GROUPS, PGW, PGW), F32), _sds((1, D), F32)],
        input_output_aliases={7: 0},
        name="pool_bwd", compiler_params=_params("arbitrary"))(proj, proj, dmix, dmix, pooled, mixw, scale, dproj)


def conv_fwd(proj, cw, cb, D):
    L = proj.shape[0]
    C = cw.shape[1]
    assert (3 * D) % C == 0
    cblk = (3 * D) // C
    T = _pick(L, ROW_TILE)
    hb = T // CONV_HALO

    def body(u_ref, halo_ref, w_ref, b_ref, o_ref):
        i = pl.program_id(0)
        u = u_ref[...]
        xe = jnp.concatenate([jnp.where(i > 0, halo_ref[...], 0.0), u], axis=0)
        acc = b_ref[...] + w_ref[CONV_K - 1:CONV_K, :] * u
        for k in range(CONV_K - 1):
            acc = acc + w_ref[k:k + 1, :] * pltpu.roll(xe, CONV_K - 1 - k, 0)[CONV_HALO:, :]
        o_ref[...] = acc

    return pl.pallas_call(
        body, grid=(L // T,),
        in_specs=[pl.BlockSpec((T, C), lambda i: (i, cblk)),
                  pl.BlockSpec((CONV_HALO, C), lambda i: (jnp.maximum(i * hb - 1, 0), cblk)),
                  pl.BlockSpec((CONV_K, C), lambda i: (0, 0)),
                  pl.BlockSpec((1, C), lambda i: (0, 0))],
        out_specs=pl.BlockSpec((T, C), lambda i: (i, 0)),
        out_shape=_sds((L, C), F32), name="conv_fwd", compiler_params=_params("parallel"))(proj, proj, cw, cb)


def conv_bwd(dparts, proj, cw, dproj, D, jobs=()):
    L = proj.shape[0]
    C = cw.shape[1]
    cblk = (3 * D) // C
    T = _pick(L, ROW_TILE)
    hb = T // CONV_HALO
    nT = L // T
    widths = [p.shape[1] for p in dparts]
    assert sum(widths) == C
    n = len(dparts)

    def body(*refs):
        d_refs, dn_refs = refs[:n], refs[n:2 * n]
        u_ref, w_ref, old_ref, dr_ref, dw_ref, db_ref = refs[2 * n:]
        i = pl.program_id(0)

        @pl.when(i == 0)
        def _():
            dw_ref[...] = jnp.zeros_like(dw_ref)
            db_ref[...] = jnp.zeros_like(db_ref)

        at = 0
        for d_ref, dn_ref, wd in zip(d_refs, dn_refs, widths):
            sl = slice(at, at + wd)
            at += wd
            d = d_ref[...]
            u = u_ref[:, sl]
            de = jnp.concatenate([d, jnp.where(i < nT - 1, dn_ref[...], 0.0)], axis=0)
            acc = w_ref[CONV_K - 1:CONV_K, sl] * d
            dw_ref[CONV_K - 1:CONV_K, sl] += jnp.sum(d * u, axis=0, keepdims=True)
            for k in range(CONV_K - 1):
                sh = CONV_K - 1 - k
                ds = pltpu.roll(de, T + CONV_HALO - sh, 0)[:T, :]
                acc = acc + w_ref[k:k + 1, sl] * ds
                dw_ref[k:k + 1, sl] += jnp.sum(ds * u, axis=0, keepdims=True)
            dr_ref[:, sl] = acc.astype(dr_ref.dtype)
            db_ref[:, sl] += jnp.sum(d, axis=0, keepdims=True)

    nxt = lambda i: jnp.minimum((i + 1) * hb, L // CONV_HALO - 1)
    return _call(
        body, grid=(nT,),
        in_specs=[pl.BlockSpec((T, wd), lambda i: (i, 0)) for wd in widths]
        + [pl.BlockSpec((CONV_HALO, wd), lambda i: (nxt(i), 0)) for wd in widths]
        + [pl.BlockSpec((T, C), lambda i: (i, cblk)), pl.BlockSpec((CONV_K, C), lambda i: (0, 0)), _ANY],
        out_specs=[pl.BlockSpec((T, C), lambda i: (i, cblk)),
                   pl.BlockSpec((8, C), lambda i: (0, 0)),
                   pl.BlockSpec((1, C), lambda i: (0, 0))],
        out_shape=[_sds(dproj.shape, dproj.dtype), _sds((8, C), F32), _sds((1, C), F32)],
        aliases={2 * n + 2: 0}, name="conv_bwd", sem=("arbitrary",),
        args=(*dparts, *dparts, proj, cw, dproj), jobs=jobs)


def _softplus(v):
    y = jnp.exp(-jnp.abs(v))
    u = 1.0 + y
    log1p = jnp.where(u == 1.0, y, jnp.log(u) * y / jnp.where(u == 1.0, 1.0, u - 1.0))
    return jnp.maximum(v, 0.0) + log1p


def dt_prep(dtraw, bias, alog, expand, D, jobs=()):
    L = dtraw.shape[0]
    GC = D // GROUPS
    HPG = GC // HEAD_DIM
    Q = _pick(L, SCAN_CHUNK)
    nc = L // Q

    def body(r_ref, b_ref, a_ref, e_ref, dt_ref, acs_ref, acst_ref, dtx_ref, eax_ref, dsx_ref, cdx_ref):
        valid = lax.broadcasted_iota(jnp.int32, (1, LANES), 1) < HPG
        tril = (_row(0, (Q, Q)) >= lax.broadcasted_iota(jnp.int32, (Q, Q), 1)).astype(BF16)
        e = e_ref[...]
        for g in range(GROUPS):
            hs, cs = slice(g * LANES, (g + 1) * LANES), slice(g * GC, (g + 1) * GC)
            dt = jnp.where(valid, _softplus(r_ref[:, hs] + b_ref[:, hs]), 0.0)
            adt = dt * -jnp.exp(a_ref[:, hs])
            acs = _sel_dot(tril, adt)
            last = acs[Q - 1:Q, :]
            dt_ref[:, hs] = dt
            acs_ref[:, hs] = acs
            acst_ref[hs, :] = acs.T
            dtx_ref[:, cs] = _dot_sel(dt, e, 2)
            eax_ref[:, cs] = jnp.exp(_dot_sel(acs, e, 2))
            dsx_ref[:, cs] = jnp.exp(_dot_sel(last - acs, e, 2))
            cdx_ref[0, :, cs] = jnp.exp(_dot_sel(jnp.broadcast_to(last, (8, LANES)), e, 2))

    head = pl.BlockSpec((Q, GROUPS * LANES), lambda c: (c, 0))
    hvec = pl.BlockSpec((1, GROUPS * LANES), lambda c: (0, 0))
    chan = pl.BlockSpec((Q, D), lambda c: (c, 0))
    return _call(
        body, grid=(nc,),
        in_specs=[head, hvec, hvec, pl.BlockSpec((LANES, GC), lambda c: (0, 0))],
        out_specs=[head, head, pl.BlockSpec((GROUPS * LANES, Q), lambda c: (0, c)), chan, chan, chan,
                   pl.BlockSpec((1, 8, D), lambda c: (c, 0, 0))],
        out_shape=[_sds((L, GROUPS * LANES), F32), _sds((L, GROUPS * LANES), F32), _sds((GROUPS * LANES, L), F32),
                   _sds((L, D), F32), _sds((L, D), F32), _sds((L, D), F32), _sds((nc, 8, D), F32)],
        name="dt_prep", sem=("parallel",), args=(dtraw, bias, alog, expand), jobs=jobs)


def _scan_specs(L, D, Q, rev):
    GC = D // GROUPS
    nc = L // Q
    ci = (lambda c: nc - 1 - c) if rev else (lambda c: c)
    return dict(
        xs=pl.BlockSpec((Q, GC), lambda g, c: (ci(c), g)),
        b=pl.BlockSpec((Q, STATE), lambda g, c: (ci(c), D // STATE + g)),
        c=pl.BlockSpec((Q, STATE), lambda g, c: (ci(c), D // STATE + GROUPS + g)),
        chan=pl.BlockSpec((Q, GC), lambda g, c: (ci(c), g)),
        cdx=pl.BlockSpec((1, 8, GC), lambda g, c: (ci(c), 0, g)),
        head=pl.BlockSpec((Q, LANES), lambda g, c: (ci(c), g)),
        headt=pl.BlockSpec((LANES, Q), lambda g, c: (g, ci(c))),
        state=pl.BlockSpec((1, 1, STATE, GC), lambda g, c: (ci(c), g, 0, 0)),
        hvec=pl.BlockSpec((1, LANES), lambda g, c: (0, g)),
        cvec=pl.BlockSpec((1, GC), lambda g, c: (0, g)))


def scan_fwd(pre, dtx, eax, dsx, cdx, acs, acst, proj, dexp, nw, mixed, D, jobs=()):
    L = pre.shape[0]
    GC = D // GROUPS
    Q = _pick(L, SCAN_CHUNK)
    nc = L // Q
    sp = _scan_specs(L, D, Q, False)

    def body(xs_ref, b_ref, c_ref, dtx_ref, eax_ref, dsx_ref, cdx_ref, acs_ref, acst_ref, z_ref, de_ref, nw_ref,
             old_ref, y_ref, st_ref, o_ref, s_scr):
        @pl.when(pl.program_id(1) == 0)
        def _():
            s_scr[...] = jnp.zeros_like(s_scr)

        tri = _row(0, (Q, Q)) >= lax.broadcasted_iota(jnp.int32, (Q, Q), 1)
        half = lax.broadcasted_iota(jnp.int32, (1, LANES), 1) // HEAD_DIM
        xs, _ = _silu_and_grad(xs_ref[...])
        bg = _silu_and_grad(b_ref[...])[0].astype(BF16)
        cg = _silu_and_grad(c_ref[...])[0].astype(BF16)
        xdt = xs * dtx_ref[...]
        sprev = s_scr[...]
        st_ref[0, 0] = sprev
        sc = _dot(cg, bg, NT)
        yoff = _dot(cg, sprev.astype(BF16)) * eax_ref[...]
        for j in range(GC // LANES):
            ps = slice(j * LANES, (j + 1) * LANES)
            xp = xdt[:, ps]
            acc = yoff[:, ps]
            for hh in range(2):
                h = 2 * j + hh
                lm = jnp.exp(jnp.where(tri, acs_ref[:, h:h + 1] - acst_ref[h:h + 1, :], -1e30))
                xm = jnp.where(half == hh, xp, 0.0).astype(BF16)
                acc = acc + _dot((sc * lm).astype(BF16), xm)
            y_ref[:, ps] = acc
        xw = (xdt * dsx_ref[...]).astype(BF16)
        s_scr[...] = cdx_ref[0, 0:1, :] * sprev + _dot(bg, xw, TN)
        z = z_ref[...]
        y3 = (y_ref[...] + de_ref[...] * xs) * (z * _sigmoid(z))
        r = lax.rsqrt(jnp.mean(y3 * y3, axis=-1, keepdims=True) + NORM_EPS)
        o_ref[...] = (y3 * r * nw_ref[...]).astype(o_ref.dtype)

    return _call(
        body, grid=(GROUPS, nc),
        in_specs=[sp["xs"], sp["b"], sp["c"], sp["chan"], sp["chan"], sp["chan"], sp["cdx"], sp["head"], sp["headt"],
                  pl.BlockSpec((Q, GC), lambda g, c: (c, 2 * GROUPS + g)), sp["cvec"], sp["cvec"], _ANY],
        out_specs=[sp["chan"], sp["state"], pl.BlockSpec((Q, GC), lambda g, c: (c, GROUPS + g))],
        out_shape=[_sds((L, D), F32), _sds((nc, GROUPS, STATE, GC), F32), _sds(mixed.shape, mixed.dtype)],
        aliases={12: 2}, scratch_shapes=[pltpu.VMEM((STATE, GC), F32)], name="scan_fwd",
        sem=("parallel", "arbitrary"), args=(pre, pre, pre, dtx, eax, dsx, cdx, acs, acst, proj, dexp, nw, mixed),
        jobs=jobs)


def scan_bwd(pre, dtx, eax, dsx, cdx, acs, acst, dt, dtraw, bias, alog, states, y, proj, dmix, nw, dexp, collapse, D,
             jobs=()):
    L = pre.shape[0]
    GC = D // GROUPS
    Q = _pick(L, SCAN_CHUNK)
    nc = L // Q
    sp = _scan_specs(L, D, Q, True)
    rc = lambda c: nc - 1 - c

    def body(xs_ref, b_ref, c_ref, dtx_ref, eax_ref, dsx_ref, cdx_ref, acs_ref, acst_ref, dt_ref, raw_ref,
             bias_ref, alog_ref, st_ref, y_ref, z_ref, dm_ref, nw_ref, dexp_ref, col_ref,
             dxs_ref, db_ref, dc_ref, ddt_ref, dal_ref, dbi_ref, dz_ref, dnw_ref, dde_ref, ds_scr, dx_scr):
        first = pl.program_id(1) == 0

        @pl.when(first)
        def _():
            ds_scr[...] = jnp.zeros_like(ds_scr)
            dal_ref[...] = jnp.zeros_like(dal_ref)
            dbi_ref[...] = jnp.zeros_like(dbi_ref)
            dnw_ref[...] = jnp.zeros_like(dnw_ref)
            dde_ref[...] = jnp.zeros_like(dde_ref)

        li = _row(0, (Q, Q))
        si = lax.broadcasted_iota(jnp.int32, (Q, Q), 1)
        lane = lax.broadcasted_iota(jnp.int32, (1, LANES), 1)
        half = lane // HEAD_DIM
        xs_pre, b_pre, c_pre = xs_ref[...], b_ref[...], c_ref[...]
        xs, xs_g = _silu_and_grad(xs_pre)
        bf, b_g = _silu_and_grad(b_pre)
        cf, c_g = _silu_and_grad(c_pre)
        bg, cg = bf.astype(BF16), cf.astype(BF16)
        dtx, eax, dsx = dtx_ref[...], eax_ref[...], dsx_ref[...]
        cd = cdx_ref[0, 0:1, :]
        xdt = xs * dtx
        sz, dsz = _silu_and_grad(z_ref[...])
        y2 = y_ref[...] + dexp_ref[...] * xs
        y3 = y2 * sz
        r = lax.rsqrt(jnp.mean(y3 * y3, axis=-1, keepdims=True) + NORM_EPS)
        n = y3 * r
        dm = dm_ref[...]
        gg = dm * nw_ref[...]
        dy3 = r * (gg - n * jnp.mean(gg * n, axis=-1, keepdims=True))
        dnw_ref[0:1, :] += jnp.sum(dm * n, axis=0, keepdims=True)
        G = dy3 * sz
        dz_ref[...] = (dy3 * y2 * dsz).astype(dz_ref.dtype)
        dde_ref[0:1, :] += jnp.sum(G * xs, axis=0, keepdims=True)
        prev = st_ref[0, 0]
        dsn = ds_scr[...]
        prev_b, dsn_b = prev.astype(BF16), dsn.astype(BF16)
        cp = _dot(cg, prev_b)
        ge_b = (G * eax).astype(BF16)
        d_c = _dot(ge_b, prev_b, NT)
        dprev = _dot(cg, ge_b, TN) + cd * dsn
        chan_a = G * cp * eax
        xw_b = (xdt * dsx).astype(BF16)
        dcd = jnp.sum(prev * dsn, axis=0, keepdims=True)
        d_b = _dot(xw_b, dsn_b, NT)
        dxw = _dot(bg, dsn_b)
        dd = dxw * xdt * dsx
        chan_a = chan_a - dd
        last_c = jnp.sum(dd, axis=0, keepdims=True) + dcd * cd
        sc = _dot(cg, bg, NT)
        head_row = _row(0, (LANES, 1))
        dsc = jnp.zeros((Q, Q), F32)
        dacs = jnp.zeros((Q, LANES), F32)
        colsums = jnp.zeros((LANES, Q), F32)
        for j in range(GC // LANES):
            ps = slice(j * LANES, (j + 1) * LANES)
            xp, gp = xdt[:, ps], G[:, ps]
            dxp = dxw[:, ps] * dsx[:, ps]
            for hh in range(2):
                h = 2 * j + hh
                lm = jnp.exp(jnp.where(li >= si, acs_ref[:, h:h + 1] - acst_ref[h:h + 1, :], -1e30))
                m = sc * lm
                xm = jnp.where(half == hh, xp, 0.0).astype(BF16)
                gm = jnp.where(half == hh, gp, 0.0).astype(BF16)
                dm = _dot(gm, xm, NT)
                dxp = dxp + _dot(m.astype(BF16), gm, TN)
                dsc = dsc + dm * lm
                w = dm * m
                dacs = dacs + jnp.where(lane == h, jnp.sum(w, axis=1, keepdims=True), 0.0)
                colsums = jnp.where(head_row == h, jnp.sum(w, axis=0, keepdims=True), colsums)
            dx_scr[:, ps] = dxp
        dacs = dacs - colsums.T
        dsc_b = dsc.astype(BF16)
        d_c = d_c + _dot(dsc_b, bg)
        d_b = d_b + _dot(dsc_b, cg, TN)
        ds_scr[...] = dprev
        dxdt = dx_scr[...]
        dxs_ref[...] = (dxdt * dtx + dexp_ref[...] * G) * xs_g
        db_ref[...] = d_b * b_g
        dc_ref[...] = d_c * c_g
        colm = col_ref[...]
        dacs = dacs + _dot_sel(chan_a, colm)
        dlast = _dot_sel(jnp.broadcast_to(last_c, (8, GC)), colm)[0:1, :]
        dacs = dacs + jnp.where(_row(0, (Q, 1)) == Q - 1, dlast, 0.0)
        dadt = _sel_dot((si >= li).astype(BF16), dacs)
        a = -jnp.exp(alog_ref[...])
        dt = dt_ref[...]
        ddt = dadt * a + _dot_sel(dxdt * xs, colm, 2)
        dal_ref[0:1, :] += jnp.sum(dadt * dt * a, axis=0, keepdims=True)
        draw = ddt * _sigmoid(raw_ref[...] + bias_ref[...])
        dbi_ref[0:1, :] += jnp.sum(draw, axis=0, keepdims=True)
        ddt_ref[...] = draw.astype(ddt_ref.dtype)

    acc = pl.BlockSpec((8, LANES), lambda g, c: (0, g))
    cacc = pl.BlockSpec((8, GC), lambda g, c: (0, g))
    return _call(
        body, grid=(GROUPS, nc),
        in_specs=[sp["xs"], sp["b"], sp["c"], sp["chan"], sp["chan"], sp["chan"], sp["cdx"], sp["head"], sp["headt"],
                  sp["head"], sp["head"], sp["hvec"], sp["hvec"], sp["state"], sp["chan"],
                  pl.BlockSpec((Q, GC), lambda g, c: (rc(c), 2 * GROUPS + g)),
                  pl.BlockSpec((Q, GC), lambda g, c: (rc(c), GROUPS + g)), sp["cvec"], sp["cvec"],
                  pl.BlockSpec((GC, LANES), lambda g, c: (0, 0))],
        out_specs=[sp["chan"],
                   pl.BlockSpec((Q, STATE), lambda g, c: (rc(c), g)),
                   pl.BlockSpec((Q, STATE), lambda g, c: (rc(c), g)),
                   sp["head"], acc, acc,
                   pl.BlockSpec((Q, GC), lambda g, c: (rc(c), 2 * GROUPS + g)), cacc, cacc],
        out_shape=[_sds((L, D), F32), _sds((L, GROUPS * STATE), F32), _sds((L, GROUPS * STATE), F32),
                   _sds((L, GROUPS * LANES), BF16), _sds((8, GROUPS * LANES), F32), _sds((8, GROUPS * LANES), F32),
                   _sds((L, proj.shape[1]), BF16), _sds((8, D), F32), _sds((8, D), F32)],
        scratch_shapes=[pltpu.VMEM((STATE, GC), F32), pltpu.VMEM((Q, GC), F32)], name="scan_bwd",
        sem=("parallel", "arbitrary"),
        args=(pre, pre, pre, dtx, eax, dsx, cdx, acs, acst, dt, dtraw, bias, alog, states, y, proj, dmix, nw, dexp,
              collapse),
        jobs=jobs)


def _adam_math(gv, w, m, v):
    c1 = 1.0 - ADAM_B1 ** ADAM_STEP
    c2 = 1.0 - ADAM_B2 ** ADAM_STEP
    nm = ADAM_B1 * m + (1.0 - ADAM_B1) * gv
    nv = ADAM_B2 * v + (1.0 - ADAM_B2) * (gv * gv)
    return -ADAM_LR * ((nm / c1) / (jnp.sqrt(nv / c2) + ADAM_EPS) + ADAM_WD * w), nm, nv


def adamw(g, w, m, v, name):
    R, C = g.shape
    T = R if R <= 128 else 128
    assert R % T == 0

    def body(g_ref, w_ref, m_ref, v_ref, d_ref, nm_ref, nv_ref):
        d_ref[...], nm_ref[...], nv_ref[...] = _adam_math(g_ref[...], w_ref[...], m_ref[...], v_ref[...])

    blk = pl.BlockSpec((T, C), lambda i: (i, 0))
    return pl.pallas_call(
        body, grid=(R // T,), in_specs=[blk] * 4, out_specs=[blk] * 3,
        out_shape=[_sds((R, C), F32)] * 3, name=name, compiler_params=_params("parallel"))(g, w, m, v)


def adamw_layer(g, w, m, v, layer, prev, name):
    R, C = g.shape
    T = _rows_tile(R, ROW_TILE)
    assert R % T == 0

    def body(g_ref, w_ref, m_ref, v_ref, *rest):
        go_ref, d_ref, nm_ref, nv_ref = rest[-4:]
        gv = g_ref[...]
        go_ref[0] = gv
        d_ref[0], nm_ref[0], nv_ref[0] = _adam_math(gv, w_ref[0], m_ref[0], v_ref[0])

    mine = pl.BlockSpec((1, T, C), lambda i: (layer, i, 0))
    prev = list(prev or [])
    return pl.pallas_call(
        body, grid=(R // T,), in_specs=[pl.BlockSpec((T, C), lambda i: (i, 0)), mine, mine, mine] + [_ANY] * len(prev),
        out_specs=[mine] * 4, out_shape=[_sds(w.shape, F32)] * 4,
        input_output_aliases={4 + i: i for i in range(len(prev))}, name=name,
        compiler_params=_params("parallel"))(g, w, m, v, *prev)


def cast_bf16(a, layer):
    _, R, C = a.shape
    T = _rows_tile(R, ROW_TILE)

    def body(a_ref, o_ref):
        o_ref[...] = a_ref[0].astype(BF16)

    return pl.pallas_call(
        body, grid=(R // T,), in_specs=[pl.BlockSpec((1, T, C), lambda i: (layer, i, 0))],
        out_specs=pl.BlockSpec((T, C), lambda i: (i, 0)), out_shape=_sds((R, C), BF16),
        name="cast_bf16", compiler_params=_params("parallel"))(a)


def _to_groups(v, hpg):
    lead = v.shape[:-1]
    t = v.reshape(lead + (GROUPS, hpg))
    t = jnp.pad(t, [(0, 0)] * (len(lead) + 1) + [(0, LANES - hpg)])
    return t.reshape(lead + (GROUPS * LANES,))


def _from_groups(a, hpg):
    lead = a.shape[:-1]
    return a.reshape(lead + (GROUPS, LANES))[..., :hpg].reshape(lead + (GROUPS * hpg,))


def _expand_matrix(D):
    gc = D // GROUPS
    return (jnp.arange(LANES)[:, None] == (jnp.arange(gc)[None, :] // HEAD_DIM)).astype(BF16)


def layer_params(pre_w, w_in_t, mixw, scale, cw, cb, bias, alog, dskip, nw, w_out_full, post_w, D):
    hpg = D // GROUPS // HEAD_DIM
    main = w_in_t.shape[0] - GROUPS * hpg
    wdt_t = _to_groups(w_in_t[main:].T, hpg).T
    return dict(
        pre_w=pre_w[None], win_t=w_in_t, main=main, wdt_t=wdt_t, mixw=mixw,
        scale=scale[None], cw=cw, cb=cb[None], bias=_to_groups(bias, hpg)[None], alog=_to_groups(alog, hpg)[None],
        dexp=jnp.repeat(dskip, HEAD_DIM)[None], nw=nw[None], wout=w_out_full, post_w=post_w[None])


def layer_fwd(x, p, D, next_shards=None, h=None, next_pre_w=None, own_rest=None):
    nxt = bool(next_shards)
    parts = lambda a: ((a.shape[0] // 32) * 16, a.shape[0] - (a.shape[0] // 32) * 16)
    if nxt:
        s_in, s_out, s_mix, s_cw = next_shards
        (in_a, in_b), (out_a, out_b) = parts(s_in), parts(s_out)
    if h is None:
        h = rms_fwd(x, p["pre_w"])
    own = own_rest is not None
    proj, got = matmul(h, p["win_t"], "nt", F32, "proj", tm=1024, n_out=p["main"],
                       jobs=([gather_rows_job(s_in, 0, in_a)] if nxt else [])
                       + ([gather_send_job(own_rest[0], own_rest[1])] if own else []))
    g_in = got[0][0] if nxt else None
    dtraw, got = matmul(h, p["wdt_t"], "nt", F32, "dtproj",
                        jobs=[gather_pass_job(got[-1], own_rest[1])] if own else [])
    if own:
        p.update(own_rest[2](got[0]))
    mixed, pooled = pool_fwd(proj, p["mixw"], p["scale"], D)
    pre = conv_fwd(proj, p["cw"], p["cb"], D)
    dtp, got = dt_prep(dtraw, p["bias"], p["alog"], _expand_matrix(D), D,
                       jobs=[gather_rows_job(s_out, 0, out_a)] if nxt else [])
    g_out = got[0][0] if nxt else None
    dt, acs, acst, dtx, eax, dsx, cdx = dtp
    (y, states, mixed), got = scan_fwd(pre, dtx, eax, dsx, cdx, acs, acst, proj, p["dexp"], p["nw"], mixed, D,
                                       jobs=[gather_rows_job(s_in, in_a, in_b, g_in)] if nxt else [])
    g_in = got[0][0] if nxt else None
    out, got = matmul(mixed, p["wout"], "nn", F32, "outproj", tm=1024,
                      jobs=[gather_rows_job(s_out, out_a, out_b, g_out), gather_send_job([s_mix, s_cw]),
                            gather_pass_job([g_in], [True])] if nxt else [])
    (xn, h_next), got2 = post_fwd(x, out, p["post_w"], next_pre_w,
                                  jobs=[gather_pass_job(got[0] + got[1], [True, False, False])] if nxt else [])
    gathered = got[2] + got2[0] if nxt else None
    return xn, dict(x=x, h=h, proj=proj, dtraw=dtraw, pooled=pooled, pre=pre, dtp=dtp, y=y, states=states,
                    mixed=mixed, out=out), gathered, h_next


def layer_bwd(dxn, p, s, D, where=None, post=None, below=None, pending=None):
    reduce = where is not None
    chip, core = where if reduce else (None, None)
    hpg = D // GROUPS // HEAD_DIM
    PGW = D // GROUPS
    main = p["main"]
    SH = (main + GROUPS * hpg) // 4
    dt, acs, acst, dtx, eax, dsx, cdx = s["dtp"]
    dout, d_post = post if post else post_bwd(s["out"], p["post_w"], dxn)
    part = BF16 if reduce else F32
    d_wout, got = matmul(s["mixed"], dout, "tn", part, "dwout", tm=1024,
                         jobs=[pair_gather_job(pending, [True, False])] if pending else [])
    above = dict(w_in=got[0][0], pool_mix_w=got[0][1]) if pending else None
    g_out = d_wout.reshape(4, 2, D // 4, D)
    dmix, got = matmul(dout, p["wout"], "nt", F32, "dmixed", tm=1024,
                       jobs=[pair_exchange_job([g_out])] if reduce else [])
    pair_out = pair_add(g_out, got[0][0], core, BF16) if reduce else None
    (dxs, db, dc, ddtraw, d_alog, d_bias, dproj, d_nw, d_dexp), got = scan_bwd(
        s["pre"], dtx, eax, dsx, cdx, acs, acst, dt, s["dtraw"], p["bias"], p["alog"], s["states"], s["y"],
        s["proj"], dmix, p["nw"], p["dexp"], _expand_matrix(D).T, D,
        jobs=[chip_exchange_job([pair_out])] if reduce else [])
    mine_out = chip_add(pair_out, got[0][0], chip, core) if reduce else None
    (dproj, d_cw, d_cb), got = conv_bwd([dxs, db, dc], s["proj"], p["cw"], dproj, D,
                                        jobs=[pair_gather_job([mine_out])] if reduce else [])
    r_out = got[0][0] if reduce else None
    dproj, d_mixw, d_scale = pool_bwd(s["proj"], dmix, s["pooled"], p["mixw"], p["scale"], dproj, D)
    d_wmain_t, _ = matmul(dproj, s["h"], "tn", part, "dwmain", tm=1024)
    d_wdt_t, _ = matmul(ddtraw, s["h"], "tn", part, "dwdt")
    late = [d_wmain_t[None], d_wdt_t[None],
            d_mixw.reshape(GROUPS, 4, PGW // 4, PGW).transpose(1, 0, 2, 3).reshape(4, 2, GROUPS * PGW // 8, PGW)]
    cols = [True, True, False]
    dh_b, got = matmul(ddtraw, p["wdt_t"], "nn", F32, "dh_dt",
                       jobs=[pair_exchange_job(late, cols)] if reduce else [])
    if reduce:
        p_main, p_dt, p_mix = [pair_add(g, r, core, BF16, k, n) for g, r, k, n in
                               zip(late, got[0], cols, [4 * SH, None, None])]
        p_in = lax.dynamic_update_slice(p_main[0], _from_groups(p_dt[0].T, hpg).T, (main, 0))
        pairs = [p_in.reshape(4, SH, D // 2), p_mix]
    cut = (SH * 5 // 128) * 16
    dh_a, got = matmul(dproj, p["win_t"], "nn", F32, "dh_main",
                       jobs=[chip_exchange_rows_job(pairs[0], 0, cut), chip_exchange_job(pairs[1:])] if reduce else [])
    (dx, d_pre, *post_below), got2 = rms_bwd(
        s["x"], p["pre_w"], dh_a, dh_b, dxn, below,
        jobs=[chip_exchange_rows_job(pairs[0], cut, SH - cut, got[0][0])] if reduce else [])
    mines = [chip_add(pairs[0], got2[0][0], chip, core, True),
             chip_add(pairs[1], got[1][0], chip, core, False)] if reduce else None
    reduced = dict(w_out=r_out) if reduce else None
    grads = dict(
        pre_norm_w=d_pre[0], pool_scale=d_scale[0], conv_w=d_cw[:CONV_K], conv_b=d_cb[0],
        dt_bias=_from_groups(d_bias[0], hpg), a_log=_from_groups(d_alog[0], hpg),
        d_skip=d_dexp[0].reshape(-1, HEAD_DIM).sum(axis=-1), ssd_norm_w=d_nw[0], post_norm_w=d_post[0])
    if not reduce:
        grads.update(w_in=jnp.concatenate([d_wmain_t.T, _from_groups(d_wdt_t.T, hpg)], axis=1), pool_mix_w=d_mixw,
                     w_out=d_wout)
    return dx, grads, reduced, tuple(post_below) or None, mines, above


def local_step(x, target, params, D):
    saved, h, n = [], None, len(params)
    for l, p in enumerate(params):
        x, s, _, h = layer_fwd(x, p, D, h=h, next_pre_w=params[l + 1]["pre_w"] if l + 1 < n else None)
        saved.append(s)
    dx, sumsq = loss_head(x, target)
    grads, post = [None] * n, None
    for l in reversed(range(n)):
        below = (saved[l - 1]["out"], params[l - 1]["post_w"]) if l else None
        dx, grads[l], _, post, _, _ = layer_bwd(dx, params[l], saved[l], D, post=post, below=below)
    return sumsq, dx, grads


SMALL = ("pre_norm_w", "pool_scale", "conv_w", "conv_b", "dt_bias", "a_log", "d_skip", "ssd_norm_w", "post_norm_w")
BIG = ("w_in", "w_out", "pool_mix_w")


def _pack(parts):
    flat = jnp.concatenate([p.reshape(-1) for p in parts])
    n = flat.shape[0]
    rows = -(-n // (LANES * LANES)) * LANES
    return jnp.pad(flat, (0, rows * LANES - n)).reshape(rows, LANES)


def _unpack(packed, shapes):
    flat, out, at = packed.reshape(-1), [], 0
    for s in shapes:
        n = math.prod(s)
        out.append(flat[at:at + n].reshape(s))
        at += n
    return out


def kernel(x, pre_norm_w, w_in, pool_mix_w, pool_scale, conv_w, conv_b, dt_bias, a_log, d_skip, ssd_norm_w, w_out, post_norm_w, loss_target, m_pre_norm_w, m_w_in, m_pool_mix_w, m_pool_scale, m_conv_w, m_conv_b, m_dt_bias, m_a_log, m_d_skip, m_ssd_norm_w, m_w_out, m_post_norm_w, v_pre_norm_w, v_w_in, v_pool_mix_w, v_pool_scale, v_conv_w, v_conv_b, v_dt_bias, v_a_log, v_d_skip, v_ssd_norm_w, v_w_out, v_post_norm_w):
    NL, D, SH = w_in.shape
    PGW = D // GROUPS
    CS = conv_w.shape[2]
    chip = (2 * lax.axis_index("x") + lax.axis_index("y")).astype(jnp.int32)
    chip1, core = chip.reshape(1), lax.axis_index("c").astype(jnp.int32).reshape(1)

    tr = lambda t: jnp.transpose(t, (0, 2, 1))
    w_in_t, m_w_in_t, v_w_in_t = tr(w_in), tr(m_w_in), tr(v_w_in)
    halved_by_cols = [True, True, False, False]

    def shards(l):
        return [cast_bf16(w_in_t, l), cast_bf16(w_out, l),
                cast_bf16(pool_mix_w.reshape(NL, GROUPS * PGW // 4, PGW), l).reshape(2, GROUPS * PGW // 8, PGW),
                conv_w[l].reshape(2, CONV_K * CS // (2 * LANES), LANES)]

    def rest_params(g):
        g_out, g_mix, g_cw = g
        return dict(wout=g_out.reshape(2 * D, D),
                    mixw=g_mix.reshape(4, GROUPS, PGW // 4, PGW).transpose(1, 0, 2, 3).reshape(GROUPS, PGW, PGW),
                    cw=g_cw.reshape(4, CONV_K, CS).transpose(1, 0, 2).reshape(CONV_K, 4 * CS))

    def params(l, g):
        late = rest_params(g[1:]) if len(g) > 1 else dict(wout=None, mixw=None, cw=None)
        return layer_params(pre_norm_w[l], g[0].reshape(4 * SH, D), late["mixw"], pool_scale[l], late["cw"], conv_b[l],
                            dt_bias[l], a_log[l], d_skip[l], ssd_norm_w[l], late["wout"], post_norm_w[l], D)

    first = shards(0)
    gathered = run_jobs([gather_send_job(first[:1], [True])], "gather_send")[0]
    gathered = run_jobs([gather_pass_job(gathered, [True])], "gather_pass")[0]
    xl, h, ps, saved = x[0], None, [], []
    for l in range(NL):
        ps.append(params(l, gathered))
        last = l + 1 == NL
        xl, s, gathered, h = layer_fwd(xl, ps[l], D, None if last else shards(l + 1), h,
                                       None if last else pre_norm_w[l + 1][None],
                                       (first[1:], halved_by_cols[1:], rest_params) if l == 0 else None)
        saved.append(s)
    dx, sumsq = loss_head(xl, loss_target[0])

    given = dict(w_in=(w_in_t, m_w_in_t, v_w_in_t), w_out=(w_out, m_w_out, v_w_out),
                 pool_mix_w=(pool_mix_w, m_pool_mix_w, v_pool_mix_w))
    flat = {n: [t.reshape(NL, -1, t.shape[-1]) for t in given[n]] for n in BIG}
    done = {n: None for n in BIG}
    def update(l, reduced):
        for n, r in reduced.items():
            done[n] = adamw_layer(r.reshape(-1, r.shape[-1]), *flat[n], l, done[n], "adamw_" + n)

    grads, post, pending = [None] * NL, None, None
    for l in reversed(range(NL)):
        below = (saved[l - 1]["out"], ps[l - 1]["post_w"]) if l else None
        dx, grads[l], reduced, post, pending, above = layer_bwd(dx, ps[l], saved[l], D, (chip1, core), post, below,
                                                                pending)
        update(l, reduced)
        if above:
            update(l + 1, above)
    last = run_jobs([pair_gather_job(pending, [True, False])], "pair_gather")[0]
    update(0, dict(w_in=last[0], pool_mix_w=last[1]))

    small_shapes = [(NL,) + grads[0][n].shape for n in SMALL]
    packed = _pack([0.5 / D * sumsq[0, :1]] + [jnp.stack([g[n] for g in grads]) for n in SMALL])
    total = allreduce_small(packed)
    loss, *small = _unpack(total, [(1,)] + small_shapes)
    small = dict(zip(SMALL, small))
    small["conv_w"] = lax.dynamic_slice_in_dim(small["conv_w"], chip * CS, CS, axis=2)

    given_small = dict(
        pre_norm_w=(pre_norm_w, m_pre_norm_w, v_pre_norm_w), pool_scale=(pool_scale, m_pool_scale, v_pool_scale),
        conv_w=(conv_w, m_conv_w, v_conv_w), conv_b=(conv_b, m_conv_b, v_conv_b),
        dt_bias=(dt_bias, m_dt_bias, v_dt_bias), a_log=(a_log, m_a_log, v_a_log),
        d_skip=(d_skip, m_d_skip, v_d_skip), ssd_norm_w=(ssd_norm_w, m_ssd_norm_w, v_ssd_norm_w),
        post_norm_w=(post_norm_w, m_post_norm_w, v_post_norm_w))
    shapes = [given_small[n][0].shape for n in SMALL]
    upd = adamw(_pack([small[n] for n in SMALL]), *[_pack([given_small[n][i] for n in SMALL]) for i in range(3)],
                "adamw_small")
    upd = [dict(zip(SMALL, _unpack(u, shapes))) for u in upd]

    out = {n: (small[n], upd[0][n], upd[1][n], upd[2][n]) for n in SMALL}
    for n in BIG:
        out[n] = tuple(t.reshape(given[n][0].shape) for t in done[n])
    out["w_in"] = tuple(tr(t) for t in out["w_in"])

    order = ("pre_norm_w", "w_in", "pool_mix_w", "pool_scale", "conv_w", "conv_b", "dt_bias", "a_log", "d_skip",
             "ssd_norm_w", "w_out", "post_norm_w")
    return (loss.reshape(()), dx[None], *[out[n][0] for n in order], *[out[n][1] for n in order],
            *[out[n][2] for n in order], *[out[n][3] for n in order])
```

```python
import functools
import math

import jax
import jax.numpy as jnp
from jax import lax
from jax.experimental import pallas as pl
from jax.experimental.pallas import tpu as pltpu

F32 = jnp.float32
BF16 = jnp.bfloat16

NORM_EPS = 1e-6
HEAD_DIM = 64
STATE = 128
GROUPS = 4
POOL_WINDOWS = (2, 4, 8, 16)
POOL_HALO = 16
CONV_K = 4
CONV_HALO = 8
SCAN_CHUNK = 256
LANES = 128
VMEM_LIMIT = 52 * 1024 * 1024
ROW_TILE = 256

ADAM_LR = 0.001
ADAM_B1 = 0.9
ADAM_B2 = 0.999
ADAM_EPS = 1e-08
ADAM_WD = 0.01
ADAM_STEP = 10

MESH = pl.DeviceIdType.MESH

NN = (((1,), (0,)), ((), ()))
NT = (((1,), (1,)), ((), ()))
TN = (((0,), (0,)), ((), ()))

_ANY = pl.BlockSpec(memory_space=pl.ANY)


def _params(*sem):
    return pltpu.CompilerParams(dimension_semantics=sem, vmem_limit_bytes=VMEM_LIMIT)


def _pick(dim, pref):
    if dim <= pref:
        return dim
    t = (pref // LANES) * LANES
    while t > LANES and dim % t:
        t -= LANES
    assert dim % t == 0, (dim, pref)
    return t


def _rows_tile(rows, pref):
    t = (min(pref, rows) // 8) * 8
    while t >= 8 and rows % t:
        t -= 8
    return t if t >= 8 else rows


def _dot(a, b, dn=NN):
    return lax.dot_general(a, b, dn, preferred_element_type=F32)


def _split3(a):
    hi = a.astype(BF16)
    r = a - hi.astype(F32)
    mid = r.astype(BF16)
    return hi, mid, (r - mid.astype(F32)).astype(BF16)


def _dot_sel(a, e, parts=3):
    hi, mid, lo = _split3(a)
    return (_dot(lo, e) + _dot(mid, e)) + _dot(hi, e) if parts == 3 else _dot(mid, e) + _dot(hi, e)


def _sel_dot(e, b):
    hi, mid, lo = _split3(b)
    return (_dot(e, lo) + _dot(e, mid)) + _dot(e, hi)


def _sigmoid(v):
    return 0.5 * jnp.tanh(0.5 * v) + 0.5


def _silu_and_grad(v):
    s = _sigmoid(v)
    return v * s, s * (1.0 + v * (1.0 - s))


def _row(i, shape):
    return lax.broadcasted_iota(jnp.int32, shape, 0) + i


def _sds(shape, dtype):
    return jax.ShapeDtypeStruct(tuple(shape), dtype)


class Job:
    def __init__(self, ins, outs, aliased, nsem, start, finish):
        self.ins, self.outs, self.aliased, self.nsem, self.start, self.finish = ins, outs, aliased, nsem, start, finish


def _place():
    x, y, c = lax.axis_index("x"), lax.axis_index("y"), lax.axis_index("c")
    return x, y, c, [(1 - x, y), (x, 1 - y), (1 - x, 1 - y)]


def _remote(src, dst, send_sem, recv_sem, device):
    return pltpu.make_async_remote_copy(src_ref=src, dst_ref=dst, send_sem=send_sem, recv_sem=recv_sem,
                                        device_id=device, device_id_type=MESH)


def _half(ref, c, cols, lead=0):
    idx = [slice(None)] * lead
    if cols:
        w = ref.shape[-1] // 2
        idx += [slice(None)] * (len(ref.shape) - lead - 1) + [pl.ds(pl.multiple_of(c * w, LANES), w)]
    else:
        idx += [c]
    return ref.at[tuple(idx)]


def _flags(cols, n):
    return list(cols) if cols else [False] * n


def gather_send_job(arrs, cols=None):
    n = len(arrs)
    cols = _flags(cols, n)

    def copies(ins, outs, send, recv):
        x, y, c, chips = _place()
        mine = 2 * x + y
        out = []
        for a in range(n):
            out.append(_remote(ins[a], outs[a].at[mine], send.at[4 * a + 3], recv.at[4 * a + 3], (x, y, 1 - c)))
            for j, chip in enumerate(chips):
                out.append(_remote(_half(ins[a], c, cols[a]), _half(outs[a].at[mine], c, cols[a]),
                                   send.at[4 * a + j], recv.at[4 * a + j], (*chip, c)))
        return out

    def start(ins, outs, send, recv):
        for cp in copies(ins, outs, send, recv):
            cp.start()

    def finish(ins, outs, send, recv):
        x, y, c, chips = _place()
        for a in range(n):
            for j, chip in enumerate(chips):
                landed = _half(outs[a].at[2 * chip[0] + chip[1]], c, cols[a])
                _remote(landed, landed, send.at[4 * a + j], recv.at[4 * a + j], (x, y, 1 - c)).wait_recv()
            twin = outs[a].at[2 * x + y]
            _remote(twin, twin, send.at[4 * a + 3], recv.at[4 * a + 3], (x, y, 1 - c)).wait_recv()
        for cp in copies(ins, outs, send, recv):
            cp.wait_send()

    return Job(list(arrs), [_sds((4,) + a.shape, a.dtype) for a in arrs], False, 4 * n, start, finish)


def gather_rows_job(shard, start, size, into=None):
    rows = pl.ds(start, size)

    def copies(ins, outs, send, recv):
        x, y, c, chips = _place()
        mine = outs[0].at[2 * x + y, rows]
        out = [_remote(ins[0].at[rows], mine, send.at[3], recv.at[3], (x, y, 1 - c))]
        for j, chip in enumerate(chips):
            out.append(_remote(_half(ins[0].at[rows], c, True), _half(mine, c, True), send.at[j], recv.at[j],
                               (*chip, c)))
        return out

    def start_(ins, outs, send, recv):
        for cp in copies(ins, outs, send, recv):
            cp.start()

    def finish(ins, outs, send, recv):
        x, y, c, chips = _place()
        for j, chip in enumerate(chips):
            landed = _half(outs[0].at[2 * chip[0] + chip[1], rows], c, True)
            _remote(landed, landed, send.at[j], recv.at[j], (x, y, 1 - c)).wait_recv()
        twin = outs[0].at[2 * x + y, rows]
        _remote(twin, twin, send.at[3], recv.at[3], (x, y, 1 - c)).wait_recv()
        for cp in copies(ins, outs, send, recv):
            cp.wait_send()

    ins = [shard] if into is None else [shard, into]
    return Job(ins, [_sds((4,) + shard.shape, shard.dtype)], {} if into is None else {1: 0}, 4, start_, finish)


def gather_pass_job(bufs, cols=None):
    n = len(bufs)
    cols = _flags(cols, n)

    def copies(outs, send, recv):
        x, y, c, chips = _place()
        out = []
        for a in range(n):
            for j, chip in enumerate(chips):
                landed = _half(outs[a].at[2 * chip[0] + chip[1]], c, cols[a])
                out.append(_remote(landed, landed, send.at[3 * a + j], recv.at[3 * a + j], (x, y, 1 - c)))
        return out

    def start(ins, outs, send, recv):
        for cp in copies(outs, send, recv):
            cp.start()

    def finish(ins, outs, send, recv):
        x, y, c, chips = _place()
        for a in range(n):
            for j, chip in enumerate(chips):
                passed = _half(outs[a].at[2 * chip[0] + chip[1]], 1 - c, cols[a])
                _remote(passed, passed, send.at[3 * a + j], recv.at[3 * a + j], (x, y, 1 - c)).wait_recv()
        for cp in copies(outs, send, recv):
            cp.wait_send()

    return Job(list(bufs), [_sds(b.shape, b.dtype) for b in bufs], True, 3 * n, start, finish)


def pair_exchange_job(arrs, cols=None):
    n = len(arrs)
    cols = _flags(cols, n)

    def copies(ins, outs, send, recv):
        x, y, c, _ = _place()
        return [_remote(_half(ins[a], 1 - c, cols[a], 1), outs[a], send.at[a], recv.at[a], (x, y, 1 - c))
                for a in range(n)]

    def start(ins, outs, send, recv):
        for cp in copies(ins, outs, send, recv):
            cp.start()

    def finish(ins, outs, send, recv):
        for cp in copies(ins, outs, send, recv):
            cp.wait()

    shape = lambda a, k: a.shape[:-1] + (a.shape[-1] // 2,) if k else a.shape[:1] + a.shape[2:]
    return Job(list(arrs), [_sds(shape(a, k), a.dtype) for a, k in zip(arrs, cols)], False, n, start, finish)


def chip_exchange_job(arrs):
    n = len(arrs)

    def copies(ins, outs, send, recv):
        x, y, c, chips = _place()
        return [_remote(ins[a].at[2 * chip[0] + chip[1]], outs[a].at[j], send.at[3 * a + j], recv.at[3 * a + j],
                        (*chip, c)) for a in range(n) for j, chip in enumerate(chips)]

    def start(ins, outs, send, recv):
        for cp in copies(ins, outs, send, recv):
            cp.start()

    def finish(ins, outs, send, recv):
        for cp in copies(ins, outs, send, recv):
            cp.wait()

    return Job(list(arrs), [_sds((3,) + a.shape[1:], a.dtype) for a in arrs], False, 3 * n, start, finish)


def chip_exchange_rows_job(p, start, size, into=None):
    rows = pl.ds(start, size)

    def copies(ins, outs, send, recv):
        x, y, c, chips = _place()
        return [_remote(ins[0].at[2 * chip[0] + chip[1], rows], outs[0].at[j, rows], send.at[j], recv.at[j], (*chip, c))
                for j, chip in enumerate(chips)]

    def start_(ins, outs, send, recv):
        for cp in copies(ins, outs, send, recv):
            cp.start()

    def finish(ins, outs, send, recv):
        for cp in copies(ins, outs, send, recv):
            cp.wait()

    ins = [p] if into is None else [p, into]
    return Job(ins, [_sds((3,) + p.shape[1:], p.dtype)], {} if into is None else {1: 0}, 3, start_, finish)


def pair_gather_job(bufs, cols=None):
    n = len(bufs)
    cols = _flags(cols, n)

    def copies(outs, send, recv):
        x, y, c, _ = _place()
        return [_remote(_half(outs[a], c, cols[a]), _half(outs[a], c, cols[a]), send.at[a], recv.at[a],
                        (x, y, 1 - c)) for a in range(n)]

    def start(ins, outs, send, recv):
        for cp in copies(outs, send, recv):
            cp.start()

    def finish(ins, outs, send, recv):
        for cp in copies(outs, send, recv):
            cp.wait()

    return Job(list(bufs), [_sds(b.shape, b.dtype) for b in bufs], True, n, start, finish)


def _call(body, *, grid, in_specs, out_specs, out_shape, name, sem, args, scratch_shapes=(), jobs=(), aliases=None):
    in_specs, out_specs, out_shape, scratch_shapes = list(in_specs), list(out_specs), list(out_shape), list(scratch_shapes)
    aliases = dict(aliases or {})
    n_in, n_out, n_scr = len(in_specs), len(out_specs), len(scratch_shapes)
    if jobs:
        sem = ("arbitrary",) * len(grid)
    at_in, at_out = n_in, n_out
    for j in jobs:
        pairs = j.aliased if isinstance(j.aliased, dict) else {i: i for i in range(len(j.ins))} if j.aliased else {}
        aliases.update({at_in + i: at_out + o for i, o in pairs.items()})
        at_in, at_out = at_in + len(j.ins), at_out + len(j.outs)

    def wrapped(*refs):
        ins, p = refs[:n_in], n_in
        jins = []
        for j in jobs:
            jins.append(refs[p:p + len(j.ins)])
            p += len(j.ins)
        outs, p = refs[p:p + n_out], p + n_out
        jouts = []
        for j in jobs:
            jouts.append(refs[p:p + len(j.outs)])
            p += len(j.outs)
        scr, sems = refs[p:p + n_scr], refs[p + n_scr:]

        def start():
            for k, j in enumerate(jobs):
                j.start(jins[k], jouts[k], sems[2 * k], sems[2 * k + 1])

        def finish():
            for k, j in enumerate(jobs):
                j.finish(jins[k], jouts[k], sems[2 * k], sems[2 * k + 1])

        if jobs and grid:
            ids = [pl.program_id(d) for d in range(len(grid))]
            pl.when(functools.reduce(jnp.logical_and, [i == 0 for i in ids]))(start)
            body(*ins, *outs, *scr)
            pl.when(functools.reduce(jnp.logical_and, [i == g - 1 for i, g in zip(ids, grid)]))(finish)
        else:
            start()
            body(*ins, *outs, *scr)
            finish()

    kwargs = dict(grid=grid) if grid else {}
    res = pl.pallas_call(
        wrapped, in_specs=in_specs + [_ANY] * (at_in - n_in), out_specs=out_specs + [_ANY] * (at_out - n_out),
        out_shape=out_shape + [o for j in jobs for o in j.outs],
        scratch_shapes=scratch_shapes + [pltpu.SemaphoreType.DMA((j.nsem,)) for j in jobs for _ in range(2)],
        input_output_aliases=aliases, name=name,
        compiler_params=pltpu.CompilerParams(dimension_semantics=sem, vmem_limit_bytes=VMEM_LIMIT) if grid
        else pltpu.CompilerParams(vmem_limit_bytes=VMEM_LIMIT), **kwargs)(*args, *[a for j in jobs for a in j.ins])
    res = list(res)
    outs, rest, per_job = res[:n_out], res[n_out:], []
    for j in jobs:
        per_job.append(rest[:len(j.outs)])
        rest = rest[len(j.outs):]
    return outs, per_job


def run_jobs(jobs, name):
    return _call(lambda: None, grid=(), in_specs=[], out_specs=[], out_shape=[], name=name, sem=(), args=(), jobs=jobs)[1]


def pair_add(g, r, core, out_dtype, cols=False, out_rows=None):
    S, R, C = r.shape
    T = _rows_tile(R, 2 * ROW_TILE if C <= 1024 else ROW_TILE if C <= 4096 else ROW_TILE // 2)

    def body(c_ref, g_ref, r_ref, o_ref):
        o_ref[0] = ((g_ref[0] if cols else g_ref[0, 0]).astype(F32) + r_ref[0].astype(F32)).astype(o_ref.dtype)

    g_spec = (pl.BlockSpec((1, T, C), lambda k, i, c_ref: (k, i, c_ref[0])) if cols
              else pl.BlockSpec((1, 1, T, C), lambda k, i, c_ref: (k, c_ref[0], i, 0)))
    return pl.pallas_call(
        body,
        grid_spec=pltpu.PrefetchScalarGridSpec(
            num_scalar_prefetch=1, grid=(S, R // T),
            in_specs=[g_spec, pl.BlockSpec((1, T, C), lambda k, i, c_ref: (k, i, 0))],
            out_specs=pl.BlockSpec((1, T, C), lambda k, i, c_ref: (k, i, 0))),
        out_shape=_sds((S, out_rows or R, C), out_dtype), name="pair_add",
        compiler_params=_params("parallel", "parallel"))(core, g, r)


def chip_add(p, r, chip, core, cols=False):
    _, R, C = p.shape
    T = _rows_tile(R, ROW_TILE)

    def body(k_ref, c_ref, p_ref, r0_ref, r1_ref, r2_ref, o_ref):
        s = ((p_ref[0].astype(F32) + r0_ref[0].astype(F32)) + r1_ref[0].astype(F32)) + r2_ref[0].astype(F32)
        if cols:
            o_ref[...] = s
        else:
            o_ref[0] = s

    slot = lambda j: pl.BlockSpec((1, T, C), lambda i, k_ref, c_ref: (j, i, 0))
    out_spec = (pl.BlockSpec((T, C), lambda i, k_ref, c_ref: (i, c_ref[0])) if cols
                else pl.BlockSpec((1, T, C), lambda i, k_ref, c_ref: (c_ref[0], i, 0)))
    return pl.pallas_call(
        body,
        grid_spec=pltpu.PrefetchScalarGridSpec(
            num_scalar_prefetch=2, grid=(R // T,),
            in_specs=[pl.BlockSpec((1, T, C), lambda i, k_ref, c_ref: (k_ref[0], i, 0)), slot(0), slot(1), slot(2)],
            out_specs=out_spec),
        out_shape=_sds((R, 2 * C) if cols else (2, R, C), F32), name="chip_add",
        compiler_params=_params("parallel"))(chip, core, p, r, r, r)


def allreduce_small(v):
    R = v.shape[0]

    def body(v_ref, o_ref, buf, send_sems, recv_sems, local_sem):
        x, y, c, chips = _place()
        me, sibling = (x, y, c), (x, y, 1 - c)

        def rows(px, py, pc):
            return buf.at[pl.ds((4 * px + 2 * py + pc) * R, R), :]

        def copy(k, block, to, src=None):
            return _remote(rows(*block) if src is None else src, rows(*block), send_sems.at[k], recv_sems.at[k], to)

        mine = pltpu.make_async_copy(v_ref, rows(*me), local_sem)
        mine.start()
        first = [copy(0, me, sibling, src=v_ref)]
        first += [copy(1 + j, me, (*chip, c), src=v_ref) for j, chip in enumerate(chips)]
        for cp in first:
            cp.start()
        passed = [copy(4 + j, (*chip, c), sibling) for j, chip in enumerate(chips)]
        for j, chip in enumerate(chips):
            copy(1 + j, (*chip, c), me).wait_recv()
            passed[j].start()
        copy(0, sibling, me).wait_recv()
        for j, chip in enumerate(chips):
            copy(4 + j, (*chip, 1 - c), me).wait_recv()
        for cp in first + passed:
            cp.wait_send()
        mine.wait()
        acc = buf[0:R, :]
        for d in range(1, 8):
            acc = acc + buf[d * R:(d + 1) * R, :]
        o_ref[...] = acc

    return pl.pallas_call(
        body, in_specs=[pl.BlockSpec(memory_space=pltpu.VMEM)], out_specs=pl.BlockSpec(memory_space=pltpu.VMEM),
        out_shape=_sds((R, LANES), F32),
        scratch_shapes=[pltpu.VMEM((8 * R, LANES), F32), pltpu.SemaphoreType.DMA((7,)),
                        pltpu.SemaphoreType.DMA((7,)), pltpu.SemaphoreType.DMA],
        name="allreduce_small", compiler_params=pltpu.CompilerParams(vmem_limit_bytes=VMEM_LIMIT))(v)


def matmul(a, b, mode, out_dtype, name, tm=512, tn=1024, tk=4608, jobs=(), n_out=None):
    if mode == "nn":
        (M, K), (K2, N) = a.shape, b.shape
    elif mode == "nt":
        (M, K), (N, K2) = a.shape, b.shape
        N = n_out or N
    else:
        (K, M), (K2, N) = a.shape, b.shape
    assert K == K2 or (mode == "nn" and K2 > K)
    tm, tn, tk = _pick(M, tm), _pick(N, tn), _pick(K, tk)
    nk = K // tk
    dn = {"nn": NN, "nt": NT, "tn": TN}[mode]

    def body(a_ref, b_ref, o_ref, *acc):
        part = _dot(a_ref[...].astype(BF16), b_ref[...].astype(BF16), dn)
        if nk == 1:
            o_ref[...] = part.astype(o_ref.dtype)
            return
        acc_ref, = acc
        k = pl.program_id(2)

        @pl.when(k == 0)
        def _():
            acc_ref[...] = part

        @pl.when(jnp.logical_and(k > 0, k < nk - 1))
        def _():
            acc_ref[...] += part

        @pl.when(k == nk - 1)
        def _():
            o_ref[...] = (acc_ref[...] + part).astype(o_ref.dtype)

    a_spec = (pl.BlockSpec((tk, tm), lambda i, j, k: (k, i)) if mode == "tn"
              else pl.BlockSpec((tm, tk), lambda i, j, k: (i, k)))
    b_spec = (pl.BlockSpec((tn, tk), lambda i, j, k: (j, k)) if mode == "nt"
              else pl.BlockSpec((tk, tn), lambda i, j, k: (k, j)))
    outs, per_job = _call(
        body, grid=(M // tm, N // tn, nk), in_specs=[a_spec, b_spec],
        out_specs=[pl.BlockSpec((tm, tn), lambda i, j, k: (i, j))], out_shape=[_sds((M, N), out_dtype)],
        scratch_shapes=[pltpu.VMEM((tm, tn), F32)] if nk > 1 else [], name=name,
        sem=("parallel", "parallel", "arbitrary"), args=(a, b), jobs=jobs)
    return outs[0], per_job


def rms_fwd(x, w):
    L, D = x.shape
    T = _pick(L, ROW_TILE)

    def body(x_ref, w_ref, h_ref):
        xv = x_ref[...]
        r = lax.rsqrt(jnp.mean(xv * xv, axis=-1, keepdims=True) + NORM_EPS)
        h_ref[...] = (xv * r * w_ref[...]).astype(h_ref.dtype)

    return pl.pallas_call(
        body, grid=(L // T,),
        in_specs=[pl.BlockSpec((T, D), lambda i: (i, 0)), pl.BlockSpec((1, D), lambda i: (0, 0))],
        out_specs=pl.BlockSpec((T, D), lambda i: (i, 0)),
        out_shape=_sds((L, D), BF16), name="rms_fwd", compiler_params=_params("parallel"))(x, w)


def post_fwd(x, o, w, next_w=None, jobs=()):
    L, D = x.shape
    T = _pick(L, ROW_TILE)

    def body(x_ref, o_ref, w_ref, *rest):
        ov = o_ref[...]
        r = lax.rsqrt(jnp.mean(ov * ov, axis=-1, keepdims=True) + NORM_EPS)
        y = x_ref[...] + ov * r * w_ref[...]
        rest[-1 if next_w is None else -2][...] = y
        if next_w is not None:
            r2 = lax.rsqrt(jnp.mean(y * y, axis=-1, keepdims=True) + NORM_EPS)
            rest[-1][...] = (y * r2 * rest[0][...]).astype(BF16)

    row = pl.BlockSpec((T, D), lambda i: (i, 0))
    vec = pl.BlockSpec((1, D), lambda i: (0, 0))
    more = [] if next_w is None else [next_w]
    outs, per_job = _call(
        body, grid=(L // T,), in_specs=[row, row, vec] + [vec] * len(more), out_specs=[row] * (1 + len(more)),
        out_shape=[_sds((L, D), F32)] + [_sds((L, D), BF16)] * len(more), name="post_fwd", sem=("parallel",),
        args=(x, o, w, *more), jobs=jobs)
    return (outs[0], outs[1] if more else None), per_job


def _rms_bwd_math(xv, w, dy):
    r = lax.rsqrt(jnp.mean(xv * xv, axis=-1, keepdims=True) + NORM_EPS)
    xhat = xv * r
    g = dy * w
    dx = r * (g - xhat * jnp.mean(g * xhat, axis=-1, keepdims=True))
    return dx, jnp.sum(dy * xhat, axis=0, keepdims=True)


def post_bwd(o, w, dxn):
    L, D = o.shape
    T = _pick(L, ROW_TILE)

    def body(o_ref, w_ref, d_ref, do_ref, dw_ref):
        dx, dw = _rms_bwd_math(o_ref[...], w_ref[...], d_ref[...])
        do_ref[...] = dx.astype(do_ref.dtype)

        @pl.when(pl.program_id(0) == 0)
        def _():
            dw_ref[...] = jnp.zeros_like(dw_ref)

        dw_ref[...] += dw

    row = pl.BlockSpec((T, D), lambda i: (i, 0))
    vec = pl.BlockSpec((1, D), lambda i: (0, 0))
    return pl.pallas_call(
        body, grid=(L // T,), in_specs=[row, vec, row], out_specs=[row, vec],
        out_shape=[_sds((L, D), BF16), _sds((1, D), F32)],
        name="post_bwd", compiler_params=_params("arbitrary"))(o, w, dxn)


def rms_bwd(x, w, dh_a, dh_b, dxn, below=None, jobs=()):
    L, D = x.shape
    T = _pick(L, ROW_TILE)

    def body(x_ref, w_ref, a_ref, b_ref, d_ref, *rest):
        dx, dw = _rms_bwd_math(x_ref[...], w_ref[...], a_ref[...] + b_ref[...])
        dx = d_ref[...] + dx
        outs = rest[2:] if below else rest
        outs[0][...] = dx

        @pl.when(pl.program_id(0) == 0)
        def _():
            for acc in outs[1::2]:
                acc[...] = jnp.zeros_like(acc)

        outs[1][...] += dw
        if below:
            do, dwp = _rms_bwd_math(rest[0][...], rest[1][...], dx)
            outs[2][...] = do.astype(BF16)
            outs[3][...] += dwp

    row = pl.BlockSpec((T, D), lambda i: (i, 0))
    vec = pl.BlockSpec((1, D), lambda i: (0, 0))
    more = list(below) if below else []
    return _call(
        body, grid=(L // T,), in_specs=[row, vec, row, row, row] + ([row, vec] if below else []),
        out_specs=[row, vec] + ([row, vec] if below else []),
        out_shape=[_sds((L, D), F32), _sds((1, D), F32)] + ([_sds((L, D), BF16), _sds((1, D), F32)] if below else []),
        name="rms_bwd", sem=("arbitrary",), args=(x, w, dh_a, dh_b, dxn, *more), jobs=jobs)


def loss_head(y, target):
    L, D = y.shape
    T = _pick(L, ROW_TILE)

    def body(y_ref, t_ref, d_ref, s_ref):
        e = y_ref[...] - t_ref[...]
        d_ref[...] = e * (1.0 / D)

        @pl.when(pl.program_id(0) == 0)
        def _():
            s_ref[...] = jnp.zeros_like(s_ref)

        s_ref[...] += jnp.sum(e * e)

    row = pl.BlockSpec((T, D), lambda i: (i, 0))
    return pl.pallas_call(
        body, grid=(L // T,), in_specs=[row, row],
        out_specs=[row, pl.BlockSpec((8, LANES), lambda i: (0, 0))],
        out_shape=[_sds((L, D), F32), _sds((8, LANES), F32)],
        name="loss_head", compiler_params=_params("arbitrary"))(y, target)


def _window_sums(xe, w, back):
    n = xe.shape[0]
    s, k = xe, 1
    while k < w:
        s = s + pltpu.roll(s, k if back else n - k, 0)
        k *= 2
    return s


def pool_fwd(proj, mixw, scale, D):
    L = proj.shape[0]
    PGW = D // GROUPS
    T = _pick(L, ROW_TILE)
    hb = T // POOL_HALO

    def body(u_ref, halo_ref, g_ref, mw_ref, sc_ref, y_ref, p_ref):
        i = pl.program_id(0)
        u = u_ref[...]
        halo = jnp.where(i > 0, halo_ref[...], 0.0)
        xe = jnp.concatenate([halo, u], axis=0)
        t1 = _row(i * T + 1, (T, 1))
        for g, w in enumerate(POOL_WINDOWS):
            sl = slice(g * PGW, (g + 1) * PGW)
            win = _window_sums(xe[:, sl], w, True)[POOL_HALO:, :]
            cnt = jnp.minimum(t1, w).astype(F32)
            pooled = (win / cnt - u[:, sl]).astype(BF16)
            p_ref[:, sl] = pooled
            mixed = _dot(pooled, mw_ref[g])
            gate = g_ref[:, sl]
            y_ref[:, sl] = (mixed * sc_ref[:, sl] * (gate * _sigmoid(gate))).astype(BF16)

    return pl.pallas_call(
        body, grid=(L // T,),
        in_specs=[pl.BlockSpec((T, D), lambda i: (i, 0)),
                  pl.BlockSpec((POOL_HALO, D), lambda i: (jnp.maximum(i * hb - 1, 0), 0)),
                  pl.BlockSpec((T, D), lambda i: (i, 1)),
                  pl.BlockSpec((GROUPS, PGW, PGW), lambda i: (0, 0, 0)),
                  pl.BlockSpec((1, D), lambda i: (0, 0))],
        out_specs=[pl.BlockSpec((T, D), lambda i: (i, 0)), pl.BlockSpec((T, D), lambda i: (i, 0))],
        out_shape=[_sds((L, 2 * D), BF16), _sds((L, D), BF16)],
        name="pool_fwd", compiler_params=_params("parallel"))(proj, proj, proj, mixw, scale)


def pool_bwd(proj, dmix, pooled, mixw, scale, dproj, D):
    L = proj.shape[0]
    PGW = D // GROUPS
    T = _pick(L, ROW_TILE)
    hb = T // POOL_HALO
    nT = L // T

    def body(g_ref, gh_ref, dy_ref, dyh_ref, p_ref, mw_ref, sc_ref, old_ref, dp_ref, dm_ref, ds_ref):
        i = pl.program_id(0)
        t1 = _row(i * T + 1, (T, 1))
        th1 = _row((i + 1) * T + 1, (POOL_HALO, 1))
        live = i < nT - 1

        @pl.when(i == 0)
        def _():
            ds_ref[...] = jnp.zeros_like(ds_ref)
            dm_ref[...] = jnp.zeros_like(dm_ref)

        for g, w in enumerate(POOL_WINDOWS):
            sl = slice(g * PGW, (g + 1) * PGW)
            sc = sc_ref[:, sl]
            gate, dy = g_ref[:, sl], dy_ref[:, sl]
            sg, dsg = _silu_and_grad(gate)
            pooled = p_ref[:, sl]
            mixed = _dot(pooled, mw_ref[g])
            dmixed = (dy * sc * sg).astype(BF16)
            dm_ref[g] += _dot(pooled, dmixed, TN)
            dp_ref[:, D + g * PGW:D + (g + 1) * PGW] = (dy * mixed * sc * dsg).astype(BF16)
            ds_ref[:, sl] += jnp.sum(dy * mixed * sg, axis=0, keepdims=True)
            dpool = _dot(dmixed, mw_ref[g], NT)
            gate_h = gh_ref[:, sl]
            dmixed_h = (dyh_ref[:, sl] * sc * (gate_h * _sigmoid(gate_h))).astype(BF16)
            dpool_h = jnp.where(live, _dot(dmixed_h, mw_ref[g], NT), 0.0)
            q = dpool / jnp.minimum(t1, w).astype(F32)
            q_h = dpool_h / jnp.minimum(th1, w).astype(F32)
            qe = jnp.concatenate([q, q_h], axis=0)
            dp_ref[:, sl] = (_window_sums(qe, w, False)[:T, :] - dpool).astype(BF16)

    nxt = lambda i: jnp.minimum((i + 1) * hb, L // POOL_HALO - 1)
    row = lambda c: pl.BlockSpec((T, D), lambda i: (i, c))
    return pl.pallas_call(
        body, grid=(nT,),
        in_specs=[row(1), pl.BlockSpec((POOL_HALO, D), lambda i: (nxt(i), 1)),
                  row(0), pl.BlockSpec((POOL_HALO, D), lambda i: (nxt(i), 0)),
                  row(0), pl.BlockSpec((GROUPS, PGW, PGW), lambda i: (0, 0, 0)),
                  pl.BlockSpec((1, D), lambda i: (0, 0)), _ANY],
        out_specs=[pl.BlockSpec((T, 2 * D), lambda i: (i, 0)), pl.BlockSpec((GROUPS, PGW, PGW), lambda i: (0, 0, 0)),
                   pl.BlockSpec((1, D), lambda i: (0, 0))],
        out_shape=[_sds(dproj.shape, dproj.dtype), _sds((GROUPS, PGW, PGW), F32), _sds((1, D), F32)],
        input_output_aliases={7: 0},
        name="pool_bwd", compiler_params=_params("arbitrary"))(proj, proj, dmix, dmix, pooled, mixw, scale, dproj)


def conv_fwd(proj, cw, cb, D):
    L = proj.shape[0]
    C = cw.shape[1]
    assert (3 * D) % C == 0
    cblk = (3 * D) // C
    T = _pick(L, ROW_TILE)
    hb = T // CONV_HALO

    def body(u_ref, halo_ref, w_ref, b_ref, o_ref):
        i = pl.program_id(0)
        u = u_ref[...]
        xe = jnp.concatenate([jnp.where(i > 0, halo_ref[...], 0.0), u], axis=0)
        acc = b_ref[...] + w_ref[CONV_K - 1:CONV_K, :] * u
        for k in range(CONV_K - 1):
            acc = acc + w_ref[k:k + 1, :] * pltpu.roll(xe, CONV_K - 1 - k, 0)[CONV_HALO:, :]
        o_ref[...] = acc

    return pl.pallas_call(
        body, grid=(L // T,),
        in_specs=[pl.BlockSpec((T, C), lambda i: (i, cblk)),
                  pl.BlockSpec((CONV_HALO, C), lambda i: (jnp.maximum(i * hb - 1, 0), cblk)),
                  pl.BlockSpec((CONV_K, C), lambda i: (0, 0)),
                  pl.BlockSpec((1, C), lambda i: (0, 0))],
        out_specs=pl.BlockSpec((T, C), lambda i: (i, 0)),
        out_shape=_sds((L, C), F32), name="conv_fwd", compiler_params=_params("parallel"))(proj, proj, cw, cb)


def conv_bwd(dparts, proj, cw, dproj, D, jobs=()):
    L = proj.shape[0]
    C = cw.shape[1]
    cblk = (3 * D) // C
    T = _pick(L, ROW_TILE)
    hb = T // CONV_HALO
    nT = L // T
    widths = [p.shape[1] for p in dparts]
    assert sum(widths) == C
    n = len(dparts)

    def body(*refs):
        d_refs, dn_refs = refs[:n], refs[n:2 * n]
        u_ref, w_ref, old_ref, dr_ref, dw_ref, db_ref = refs[2 * n:]
        i = pl.program_id(0)

        @pl.when(i == 0)
        def _():
            dw_ref[...] = jnp.zeros_like(dw_ref)
            db_ref[...] = jnp.zeros_like(db_ref)

        at = 0
        for d_ref, dn_ref, wd in zip(d_refs, dn_refs, widths):
            sl = slice(at, at + wd)
            at += wd
            d = d_ref[...]
            u = u_ref[:, sl]
            de = jnp.concatenate([d, jnp.where(i < nT - 1, dn_ref[...], 0.0)], axis=0)
            acc = w_ref[CONV_K - 1:CONV_K, sl] * d
            dw_ref[CONV_K - 1:CONV_K, sl] += jnp.sum(d * u, axis=0, keepdims=True)
            for k in range(CONV_K - 1):
                sh = CONV_K - 1 - k
                ds = pltpu.roll(de, T + CONV_HALO - sh, 0)[:T, :]
                acc = acc + w_ref[k:k + 1, sl] * ds
                dw_ref[k:k + 1, sl] += jnp.sum(ds * u, axis=0, keepdims=True)
            dr_ref[:, sl] = acc.astype(dr_ref.dtype)
            db_ref[:, sl] += jnp.sum(d, axis=0, keepdims=True)

    nxt = lambda i: jnp.minimum((i + 1) * hb, L // CONV_HALO - 1)
    return _call(
        body, grid=(nT,),
        in_specs=[pl.BlockSpec((T, wd), lambda i: (i, 0)) for wd in widths]
        + [pl.BlockSpec((CONV_HALO, wd), lambda i: (nxt(i), 0)) for wd in widths]
        + [pl.BlockSpec((T, C), lambda i: (i, cblk)), pl.BlockSpec((CONV_K, C), lambda i: (0, 0)), _ANY],
        out_specs=[pl.BlockSpec((T, C), lambda i: (i, cblk)),
                   pl.BlockSpec((8, C), lambda i: (0, 0)),
                   pl.BlockSpec((1, C), lambda i: (0, 0))],
        out_shape=[_sds(dproj.shape, dproj.dtype), _sds((8, C), F32), _sds((1, C), F32)],
        aliases={2 * n + 2: 0}, name="conv_bwd", sem=("arbitrary",),
        args=(*dparts, *dparts, proj, cw, dproj), jobs=jobs)


def _softplus(v):
    y = jnp.exp(-jnp.abs(v))
    u = 1.0 + y
    log1p = jnp.where(u == 1.0, y, jnp.log(u) * y / jnp.where(u == 1.0, 1.0, u - 1.0))
    return jnp.maximum(v, 0.0) + log1p


def dt_prep(dtraw, bias, alog, expand, D, jobs=()):
    L = dtraw.shape[0]
    GC = D // GROUPS
    HPG = GC // HEAD_DIM
    Q = _pick(L, SCAN_CHUNK)
    nc = L // Q

    def body(r_ref, b_ref, a_ref, e_ref, dt_ref, acs_ref, acst_ref, ex_ref, cdx_ref):
        valid = lax.broadcasted_iota(jnp.int32, (1, LANES), 1) < HPG
        tril = (_row(0, (Q, Q)) >= lax.broadcasted_iota(jnp.int32, (Q, Q), 1)).astype(BF16)
        e = e_ref[...]
        for g in range(GROUPS):
            hs, cs = slice(g * LANES, (g + 1) * LANES), slice(g * GC, (g + 1) * GC)
            dt = jnp.where(valid, _softplus(r_ref[:, hs] + b_ref[:, hs]), 0.0)
            adt = dt * -jnp.exp(a_ref[:, hs])
            acs = _sel_dot(tril, adt)
            last = acs[Q - 1:Q, :]
            dt_ref[:, hs] = dt
            acs_ref[:, hs] = acs
            acst_ref[hs, :] = acs.T
            ex_ref[0, :, cs] = _dot_sel(dt, e, 2)
            ex_ref[1, :, cs] = jnp.exp(_dot_sel(acs, e, 2))
            ex_ref[2, :, cs] = jnp.exp(_dot_sel(last - acs, e, 2))
            cdx_ref[0, :, cs] = jnp.exp(_dot_sel(jnp.broadcast_to(last, (8, LANES)), e, 2))

    head = pl.BlockSpec((Q, GROUPS * LANES), lambda c: (c, 0))
    hvec = pl.BlockSpec((1, GROUPS * LANES), lambda c: (0, 0))
    return _call(
        body, grid=(nc,),
        in_specs=[head, hvec, hvec, pl.BlockSpec((LANES, GC), lambda c: (0, 0))],
        out_specs=[head, head, pl.BlockSpec((GROUPS * LANES, Q), lambda c: (0, c)),
                   pl.BlockSpec((3, Q, D), lambda c: (0, c, 0)), pl.BlockSpec((1, 8, D), lambda c: (c, 0, 0))],
        out_shape=[_sds((L, GROUPS * LANES), F32), _sds((L, GROUPS * LANES), F32), _sds((GROUPS * LANES, L), F32),
                   _sds((3, L, D), F32), _sds((nc, 8, D), F32)],
        name="dt_prep", sem=("parallel",), args=(dtraw, bias, alog, expand), jobs=jobs)


def _scan_specs(L, D, Q, rev):
    GC = D // GROUPS
    nc = L // Q
    ci = (lambda c: nc - 1 - c) if rev else (lambda c: c)
    return dict(
        xs=pl.BlockSpec((Q, GC), lambda g, c: (ci(c), g)),
        b=pl.BlockSpec((Q, STATE), lambda g, c: (ci(c), D // STATE + g)),
        c=pl.BlockSpec((Q, STATE), lambda g, c: (ci(c), D // STATE + GROUPS + g)),
        chan=pl.BlockSpec((Q, GC), lambda g, c: (ci(c), g)),
        ex=pl.BlockSpec((3, Q, GC), lambda g, c: (0, ci(c), g)),
        cdx=pl.BlockSpec((1, 8, GC), lambda g, c: (ci(c), 0, g)),
        head=pl.BlockSpec((Q, LANES), lambda g, c: (ci(c), g)),
        headt=pl.BlockSpec((LANES, Q), lambda g, c: (g, ci(c))),
        state=pl.BlockSpec((1, 1, STATE, GC), lambda g, c: (ci(c), g, 0, 0)),
        hvec=pl.BlockSpec((1, LANES), lambda g, c: (0, g)),
        cvec=pl.BlockSpec((1, GC), lambda g, c: (0, g)))


def scan_fwd(pre, ex, cdx, acs, acst, proj, dexp, nw, mixed, D, jobs=()):
    L = pre.shape[0]
    GC = D // GROUPS
    Q = _pick(L, SCAN_CHUNK)
    nc = L // Q
    sp = _scan_specs(L, D, Q, False)

    def body(xs_ref, b_ref, c_ref, ex_ref, cdx_ref, acs_ref, acst_ref, z_ref, de_ref, nw_ref,
             old_ref, y_ref, st_ref, o_ref, s_scr):
        @pl.when(pl.program_id(1) == 0)
        def _():
            s_scr[...] = jnp.zeros_like(s_scr)

        tri = _row(0, (Q, Q)) >= lax.broadcasted_iota(jnp.int32, (Q, Q), 1)
        half = lax.broadcasted_iota(jnp.int32, (1, LANES), 1) // HEAD_DIM
        xs, _ = _silu_and_grad(xs_ref[...])
        bg = _silu_and_grad(b_ref[...])[0].astype(BF16)
        cg = _silu_and_grad(c_ref[...])[0].astype(BF16)
        xdt = xs * ex_ref[0]
        sprev = s_scr[...]
        st_ref[0, 0] = sprev
        sc = _dot(cg, bg, NT)
        yoff = _dot(cg, sprev.astype(BF16)) * ex_ref[1]
        for j in range(GC // LANES):
            ps = slice(j * LANES, (j + 1) * LANES)
            xp = xdt[:, ps]
            acc = yoff[:, ps]
            for hh in range(2):
                h = 2 * j + hh
                lm = jnp.exp(jnp.where(tri, acs_ref[:, h:h + 1] - acst_ref[h:h + 1, :], -1e30))
                xm = jnp.where(half == hh, xp, 0.0).astype(BF16)
                acc = acc + _dot((sc * lm).astype(BF16), xm)
            y_ref[:, ps] = acc
        xw = (xdt * ex_ref[2]).astype(BF16)
        s_scr[...] = cdx_ref[0, 0:1, :] * sprev + _dot(bg, xw, TN)
        z = z_ref[...]
        y3 = (y_ref[...] + de_ref[...] * xs) * (z * _sigmoid(z))
        r = lax.rsqrt(jnp.mean(y3 * y3, axis=-1, keepdims=True) + NORM_EPS)
        o_ref[...] = (y3 * r * nw_ref[...]).astype(o_ref.dtype)

    return _call(
        body, grid=(GROUPS, nc),
        in_specs=[sp["xs"], sp["b"], sp["c"], sp["ex"], sp["cdx"], sp["head"], sp["headt"],
                  pl.BlockSpec((Q, GC), lambda g, c: (c, 2 * GROUPS + g)), sp["cvec"], sp["cvec"], _ANY],
        out_specs=[sp["chan"], sp["state"], pl.BlockSpec((Q, GC), lambda g, c: (c, GROUPS + g))],
        out_shape=[_sds((L, D), F32), _sds((nc, GROUPS, STATE, GC), F32), _sds(mixed.shape, mixed.dtype)],
        aliases={10: 2}, scratch_shapes=[pltpu.VMEM((STATE, GC), F32)], name="scan_fwd",
        sem=("parallel", "arbitrary"), args=(pre, pre, pre, ex, cdx, acs, acst, proj, dexp, nw, mixed),
        jobs=jobs)


def scan_bwd(pre, ex, cdx, acs, acst, dt, dtraw, bias, alog, states, y, proj, dmix, nw, dexp, collapse, D,
             jobs=()):
    L = pre.shape[0]
    GC = D // GROUPS
    Q = _pick(L, SCAN_CHUNK)
    nc = L // Q
    sp = _scan_specs(L, D, Q, True)
    rc = lambda c: nc - 1 - c

    def body(xs_ref, b_ref, c_ref, ex_ref, cdx_ref, acs_ref, acst_ref, dt_ref, raw_ref,
             bias_ref, alog_ref, st_ref, y_ref, z_ref, dm_ref, nw_ref, dexp_ref, col_ref,
             dxs_ref, db_ref, dc_ref, ddt_ref, dal_ref, dbi_ref, dz_ref, dnw_ref, dde_ref, ds_scr, dx_scr):
        first = pl.program_id(1) == 0

        @pl.when(first)
        def _():
            ds_scr[...] = jnp.zeros_like(ds_scr)
            dal_ref[...] = jnp.zeros_like(dal_ref)
            dbi_ref[...] = jnp.zeros_like(dbi_ref)
            dnw_ref[...] = jnp.zeros_like(dnw_ref)
            dde_ref[...] = jnp.zeros_like(dde_ref)

        li = _row(0, (Q, Q))
        si = lax.broadcasted_iota(jnp.int32, (Q, Q), 1)
        lane = lax.broadcasted_iota(jnp.int32, (1, LANES), 1)
        half = lane // HEAD_DIM
        xs_pre, b_pre, c_pre = xs_ref[...], b_ref[...], c_ref[...]
        xs, xs_g = _silu_and_grad(xs_pre)
        bf, b_g = _silu_and_grad(b_pre)
        cf, c_g = _silu_and_grad(c_pre)
        bg, cg = bf.astype(BF16), cf.astype(BF16)
        dtx, eax, dsx = ex_ref[0], ex_ref[1], ex_ref[2]
        cd = cdx_ref[0, 0:1, :]
        xdt = xs * dtx
        sz, dsz = _silu_and_grad(z_ref[...])
        y2 = y_ref[...] + dexp_ref[...] * xs
        y3 = y2 * sz
        r = lax.rsqrt(jnp.mean(y3 * y3, axis=-1, keepdims=True) + NORM_EPS)
        n = y3 * r
        dm = dm_ref[...]
        gg = dm * nw_ref[...]
        dy3 = r * (gg - n * jnp.mean(gg * n, axis=-1, keepdims=True))
        dnw_ref[0:1, :] += jnp.sum(dm * n, axis=0, keepdims=True)
        G = dy3 * sz
        dz_ref[...] = (dy3 * y2 * dsz).astype(dz_ref.dtype)
        dde_ref[0:1, :] += jnp.sum(G * xs, axis=0, keepdims=True)
        prev = st_ref[0, 0]
        dsn = ds_scr[...]
        prev_b, dsn_b = prev.astype(BF16), dsn.astype(BF16)
        cp = _dot(cg, prev_b)
        ge_b = (G * eax).astype(BF16)
        d_c = _dot(ge_b, prev_b, NT)
        dprev = _dot(cg, ge_b, TN) + cd * dsn
        chan_a = G * cp * eax
        xw_b = (xdt * dsx).astype(BF16)
        dcd = jnp.sum(prev * dsn, axis=0, keepdims=True)
        d_b = _dot(xw_b, dsn_b, NT)
        dxw = _dot(bg, dsn_b)
        dd = dxw * xdt * dsx
        chan_a = chan_a - dd
        last_c = jnp.sum(dd, axis=0, keepdims=True) + dcd * cd
        sc = _dot(cg, bg, NT)
        head_row = _row(0, (LANES, 1))
        dsc = jnp.zeros((Q, Q), F32)
        dacs = jnp.zeros((Q, LANES), F32)
        colsums = jnp.zeros((LANES, Q), F32)
        for j in range(GC // LANES):
            ps = slice(j * LANES, (j + 1) * LANES)
            xp, gp = xdt[:, ps], G[:, ps]
            dxp = dxw[:, ps] * dsx[:, ps]
            for hh in range(2):
                h = 2 * j + hh
                lm = jnp.exp(jnp.where(li >= si, acs_ref[:, h:h + 1] - acst_ref[h:h + 1, :], -1e30))
                m = sc * lm
                xm = jnp.where(half == hh, xp, 0.0).astype(BF16)
                gm = jnp.where(half == hh, gp, 0.0).astype(BF16)
                dm = _dot(gm, xm, NT)
                dxp = dxp + _dot(m.astype(BF16), gm, TN)
                dsc = dsc + dm * lm
                w = dm * m
                dacs = dacs + jnp.where(lane == h, jnp.sum(w, axis=1, keepdims=True), 0.0)
                colsums = jnp.where(head_row == h, jnp.sum(w, axis=0, keepdims=True), colsums)
            dx_scr[:, ps] = dxp
        dacs = dacs - colsums.T
        dsc_b = dsc.astype(BF16)
        d_c = d_c + _dot(dsc_b, bg)
        d_b = d_b + _dot(dsc_b, cg, TN)
        ds_scr[...] = dprev
        dxdt = dx_scr[...]
        dxs_ref[...] = (dxdt * dtx + dexp_ref[...] * G) * xs_g
        db_ref[...] = d_b * b_g
        dc_ref[...] = d_c * c_g
        colm = col_ref[...]
        dacs = dacs + _dot_sel(chan_a, colm)
        dlast = _dot_sel(jnp.broadcast_to(last_c, (8, GC)), colm)[0:1, :]
        dacs = dacs + jnp.where(_row(0, (Q, 1)) == Q - 1, dlast, 0.0)
        dadt = _sel_dot((si >= li).astype(BF16), dacs)
        a = -jnp.exp(alog_ref[...])
        dt = dt_ref[...]
        ddt = dadt * a + _dot_sel(dxdt * xs, colm, 2)
        dal_ref[0:1, :] += jnp.sum(dadt * dt * a, axis=0, keepdims=True)
        draw = ddt * _sigmoid(raw_ref[...] + bias_ref[...])
        dbi_ref[0:1, :] += jnp.sum(draw, axis=0, keepdims=True)
        ddt_ref[...] = draw.astype(ddt_ref.dtype)

    acc = pl.BlockSpec((8, LANES), lambda g, c: (0, g))
    cacc = pl.BlockSpec((8, GC), lambda g, c: (0, g))
    return _call(
        body, grid=(GROUPS, nc),
        in_specs=[sp["xs"], sp["b"], sp["c"], sp["ex"], sp["cdx"], sp["head"], sp["headt"],
                  sp["head"], sp["head"], sp["hvec"], sp["hvec"], sp["state"], sp["chan"],
                  pl.BlockSpec((Q, GC), lambda g, c: (rc(c), 2 * GROUPS + g)),
                  pl.BlockSpec((Q, GC), lambda g, c: (rc(c), GROUPS + g)), sp["cvec"], sp["cvec"],
                  pl.BlockSpec((GC, LANES), lambda g, c: (0, 0))],
        out_specs=[sp["chan"],
                   pl.BlockSpec((Q, STATE), lambda g, c: (rc(c), g)),
                   pl.BlockSpec((Q, STATE), lambda g, c: (rc(c), g)),
                   sp["head"], acc, acc,
                   pl.BlockSpec((Q, GC), lambda g, c: (rc(c), 2 * GROUPS + g)), cacc, cacc],
        out_shape=[_sds((L, D), F32), _sds((L, GROUPS * STATE), F32), _sds((L, GROUPS * STATE), F32),
                   _sds((L, GROUPS * LANES), BF16), _sds((8, GROUPS * LANES), F32), _sds((8, GROUPS * LANES), F32),
                   _sds((L, proj.shape[1]), BF16), _sds((8, D), F32), _sds((8, D), F32)],
        scratch_shapes=[pltpu.VMEM((STATE, GC), F32), pltpu.VMEM((Q, GC), F32)], name="scan_bwd",
        sem=("parallel", "arbitrary"),
        args=(pre, pre, pre, ex, cdx, acs, acst, dt, dtraw, bias, alog, states, y, proj, dmix, nw, dexp,
              collapse),
        jobs=jobs)


def _adam_math(gv, w, m, v):
    c1 = 1.0 - ADAM_B1 ** ADAM_STEP
    c2 = 1.0 - ADAM_B2 ** ADAM_STEP
    nm = ADAM_B1 * m + (1.0 - ADAM_B1) * gv
    nv = ADAM_B2 * v + (1.0 - ADAM_B2) * (gv * gv)
    return -ADAM_LR * ((nm / c1) / (jnp.sqrt(nv / c2) + ADAM_EPS) + ADAM_WD * w), nm, nv


def adamw(g, w, m, v, name):
    R, C = g.shape
    T = R if R <= 128 else 128
    assert R % T == 0

    def body(g_ref, w_ref, m_ref, v_ref, d_ref, nm_ref, nv_ref):
        d_ref[...], nm_ref[...], nv_ref[...] = _adam_math(g_ref[...], w_ref[...], m_ref[...], v_ref[...])

    blk = pl.BlockSpec((T, C), lambda i: (i, 0))
    return pl.pallas_call(
        body, grid=(R // T,), in_specs=[blk] * 4, out_specs=[blk] * 3,
        out_shape=[_sds((R, C), F32)] * 3, name=name, compiler_params=_params("parallel"))(g, w, m, v)


def adamw_layer(g, w, m, v, layer, prev, name):
    R, C = g.shape
    T = _rows_tile(R, ROW_TILE)
    assert R % T == 0

    def body(g_ref, w_ref, m_ref, v_ref, *rest):
        go_ref, d_ref, nm_ref, nv_ref = rest[-4:]
        gv = g_ref[...]
        go_ref[0] = gv
        d_ref[0], nm_ref[0], nv_ref[0] = _adam_math(gv, w_ref[0], m_ref[0], v_ref[0])

    mine = pl.BlockSpec((1, T, C), lambda i: (layer, i, 0))
    prev = list(prev or [])
    return pl.pallas_call(
        body, grid=(R // T,), in_specs=[pl.BlockSpec((T, C), lambda i: (i, 0)), mine, mine, mine] + [_ANY] * len(prev),
        out_specs=[mine] * 4, out_shape=[_sds(w.shape, F32)] * 4,
        input_output_aliases={4 + i: i for i in range(len(prev))}, name=name,
        compiler_params=_params("parallel"))(g, w, m, v, *prev)


def cast_bf16(a, layer):
    _, R, C = a.shape
    T = _rows_tile(R, ROW_TILE)

    def body(a_ref, o_ref):
        o_ref[...] = a_ref[0].astype(BF16)

    return pl.pallas_call(
        body, grid=(R // T,), in_specs=[pl.BlockSpec((1, T, C), lambda i: (layer, i, 0))],
        out_specs=pl.BlockSpec((T, C), lambda i: (i, 0)), out_shape=_sds((R, C), BF16),
        name="cast_bf16", compiler_params=_params("parallel"))(a)


def _to_groups(v, hpg):
    lead = v.shape[:-1]
    t = v.reshape(lead + (GROUPS, hpg))
    t = jnp.pad(t, [(0, 0)] * (len(lead) + 1) + [(0, LANES - hpg)])
    return t.reshape(lead + (GROUPS * LANES,))


def _from_groups(a, hpg):
    lead = a.shape[:-1]
    return a.reshape(lead + (GROUPS, LANES))[..., :hpg].reshape(lead + (GROUPS * hpg,))


def _expand_matrix(D):
    gc = D // GROUPS
    return (jnp.arange(LANES)[:, None] == (jnp.arange(gc)[None, :] // HEAD_DIM)).astype(BF16)


def layer_params(pre_w, w_in_t, mixw, scale, cw, cb, bias, alog, dskip, nw, w_out_full, post_w, D):
    hpg = D // GROUPS // HEAD_DIM
    main = w_in_t.shape[0] - GROUPS * hpg
    wdt_t = _to_groups(w_in_t[main:].T, hpg).T
    return dict(
        pre_w=pre_w[None], win_t=w_in_t, main=main, wdt_t=wdt_t, mixw=mixw,
        scale=scale[None], cw=cw, cb=cb[None], bias=_to_groups(bias, hpg)[None], alog=_to_groups(alog, hpg)[None],
        dexp=jnp.repeat(dskip, HEAD_DIM)[None], nw=nw[None], wout=w_out_full, post_w=post_w[None])


def layer_fwd(x, p, D, next_shards=None, h=None, next_pre_w=None, own_rest=None):
    nxt = bool(next_shards)
    parts = lambda a: ((a.shape[0] // 32) * 16, a.shape[0] - (a.shape[0] // 32) * 16)
    if nxt:
        s_in, s_out, s_mix, s_cw = next_shards
        (in_a, in_b), (out_a, out_b) = parts(s_in), parts(s_out)
    if h is None:
        h = rms_fwd(x, p["pre_w"])
    own = own_rest is not None
    proj, got = matmul(h, p["win_t"], "nt", F32, "proj", tm=1024, n_out=p["main"],
                       jobs=([gather_rows_job(s_in, 0, in_a)] if nxt else [])
                       + ([gather_send_job(own_rest[0], own_rest[1])] if own else []))
    g_in = got[0][0] if nxt else None
    dtraw, got = matmul(h, p["wdt_t"], "nt", F32, "dtproj",
                        jobs=[gather_pass_job(got[-1], own_rest[1])] if own else [])
    if own:
        p.update(own_rest[2](got[0]))
    mixed, pooled = pool_fwd(proj, p["mixw"], p["scale"], D)
    pre = conv_fwd(proj, p["cw"], p["cb"], D)
    dtp, got = dt_prep(dtraw, p["bias"], p["alog"], _expand_matrix(D), D,
                       jobs=[gather_rows_job(s_out, 0, out_a)] if nxt else [])
    g_out = got[0][0] if nxt else None
    dt, acs, acst, ex, cdx = dtp
    (y, states, mixed), got = scan_fwd(pre, ex, cdx, acs, acst, proj, p["dexp"], p["nw"], mixed, D,
                                       jobs=[gather_rows_job(s_in, in_a, in_b, g_in)] if nxt else [])
    g_in = got[0][0] if nxt else None
    out, got = matmul(mixed, p["wout"], "nn", F32, "outproj", tm=1024,
                      jobs=[gather_rows_job(s_out, out_a, out_b, g_out), gather_send_job([s_mix, s_cw]),
                            gather_pass_job([g_in], [True])] if nxt else [])
    (xn, h_next), got2 = post_fwd(x, out, p["post_w"], next_pre_w,
                                  jobs=[gather_pass_job(got[0] + got[1], [True, False, False])] if nxt else [])
    gathered = got[2] + got2[0] if nxt else None
    return xn, dict(x=x, h=h, proj=proj, dtraw=dtraw, pooled=pooled, pre=pre, dtp=dtp, y=y, states=states,
                    mixed=mixed, out=out), gathered, h_next


def layer_bwd(dxn, p, s, D, where=None, post=None, below=None, pending=None):
    reduce = where is not None
    chip, core = where if reduce else (None, None)
    hpg = D // GROUPS // HEAD_DIM
    PGW = D // GROUPS
    main = p["main"]
    SH = (main + GROUPS * hpg) // 4
    dt, acs, acst, ex, cdx = s["dtp"]
    dout, d_post = post if post else post_bwd(s["out"], p["post_w"], dxn)
    part = BF16 if reduce else F32
    d_wout, got = matmul(s["mixed"], dout, "tn", part, "dwout", tm=1024,
                         jobs=[pair_gather_job(pending, [True, False])] if pending else [])
    above = dict(w_in=got[0][0], pool_mix_w=got[0][1]) if pending else None
    g_out = d_wout.reshape(4, 2, D // 4, D)
    dmix, got = matmul(dout, p["wout"], "nt", F32, "dmixed", tm=1024,
                       jobs=[pair_exchange_job([g_out])] if reduce else [])
    pair_out = pair_add(g_out, got[0][0], core, BF16) if reduce else None
    (dxs, db, dc, ddtraw, d_alog, d_bias, dproj, d_nw, d_dexp), got = scan_bwd(
        s["pre"], ex, cdx, acs, acst, dt, s["dtraw"], p["bias"], p["alog"], s["states"], s["y"],
        s["proj"], dmix, p["nw"], p["dexp"], _expand_matrix(D).T, D,
        jobs=[chip_exchange_job([pair_out])] if reduce else [])
    mine_out = chip_add(pair_out, got[0][0], chip, core) if reduce else None
    (dproj, d_cw, d_cb), got = conv_bwd([dxs, db, dc], s["proj"], p["cw"], dproj, D,
                                        jobs=[pair_gather_job([mine_out])] if reduce else [])
    r_out = got[0][0] if reduce else None
    dproj, d_mixw, d_scale = pool_bwd(s["proj"], dmix, s["pooled"], p["mixw"], p["scale"], dproj, D)
    d_wmain_t, _ = matmul(dproj, s["h"], "tn", part, "dwmain", tm=1024)
    d_wdt_t, _ = matmul(ddtraw, s["h"], "tn", part, "dwdt")
    late = [d_wmain_t[None], d_wdt_t[None],
            d_mixw.reshape(GROUPS, 4, PGW // 4, PGW).transpose(1, 0, 2, 3).reshape(4, 2, GROUPS * PGW // 8, PGW)]
    cols = [True, True, False]
    dh_b, got = matmul(ddtraw, p["wdt_t"], "nn", F32, "dh_dt",
                       jobs=[pair_exchange_job(late, cols)] if reduce else [])
    if reduce:
        p_main, p_dt, p_mix = [pair_add(g, r, core, BF16, k, n) for g, r, k, n in
                               zip(late, got[0], cols, [4 * SH, None, None])]
        p_in = lax.dynamic_update_slice(p_main[0], _from_groups(p_dt[0].T, hpg).T, (main, 0))
        pairs = [p_in.reshape(4, SH, D // 2), p_mix]
    cut = (SH * 5 // 128) * 16
    dh_a, got = matmul(dproj, p["win_t"], "nn", F32, "dh_main",
                       jobs=[chip_exchange_rows_job(pairs[0], 0, cut), chip_exchange_job(pairs[1:])] if reduce else [])
    (dx, d_pre, *post_below), got2 = rms_bwd(
        s["x"], p["pre_w"], dh_a, dh_b, dxn, below,
        jobs=[chip_exchange_rows_job(pairs[0], cut, SH - cut, got[0][0])] if reduce else [])
    mines = [chip_add(pairs[0], got2[0][0], chip, core, True),
             chip_add(pairs[1], got[1][0], chip, core, False)] if reduce else None
    reduced = dict(w_out=r_out) if reduce else None
    grads = dict(
        pre_norm_w=d_pre[0], pool_scale=d_scale[0], conv_w=d_cw[:CONV_K], conv_b=d_cb[0],
        dt_bias=_from_groups(d_bias[0], hpg), a_log=_from_groups(d_alog[0], hpg),
        d_skip=d_dexp[0].reshape(-1, HEAD_DIM).sum(axis=-1), ssd_norm_w=d_nw[0], post_norm_w=d_post[0])
    if not reduce:
        grads.update(w_in=jnp.concatenate([d_wmain_t.T, _from_groups(d_wdt_t.T, hpg)], axis=1), pool_mix_w=d_mixw,
                     w_out=d_wout)
    return dx, grads, reduced, tuple(post_below) or None, mines, above


def local_step(x, target, params, D):
    saved, h, n = [], None, len(params)
    for l, p in enumerate(params):
        x, s, _, h = layer_fwd(x, p, D, h=h, next_pre_w=params[l + 1]["pre_w"] if l + 1 < n else None)
        saved.append(s)
    dx, sumsq = loss_head(x, target)
    grads, post = [None] * n, None
    for l in reversed(range(n)):
        below = (saved[l - 1]["out"], params[l - 1]["post_w"]) if l else None
        dx, grads[l], _, post, _, _ = layer_bwd(dx, params[l], saved[l], D, post=post, below=below)
    return sumsq, dx, grads


SMALL = ("pre_norm_w", "pool_scale", "conv_w", "conv_b", "dt_bias", "a_log", "d_skip", "ssd_norm_w", "post_norm_w")
BIG = ("w_in", "w_out", "pool_mix_w")


def _pack(parts):
    flat = jnp.concatenate([p.reshape(-1) for p in parts])
    n = flat.shape[0]
    rows = -(-n // (LANES * LANES)) * LANES
    return jnp.pad(flat, (0, rows * LANES - n)).reshape(rows, LANES)


def _unpack(packed, shapes):
    flat, out, at = packed.reshape(-1), [], 0
    for s in shapes:
        n = math.prod(s)
        out.append(flat[at:at + n].reshape(s))
        at += n
    return out


def kernel(x, pre_norm_w, w_in, pool_mix_w, pool_scale, conv_w, conv_b, dt_bias, a_log, d_skip, ssd_norm_w, w_out, post_norm_w, loss_target, m_pre_norm_w, m_w_in, m_pool_mix_w, m_pool_scale, m_conv_w, m_conv_b, m_dt_bias, m_a_log, m_d_skip, m_ssd_norm_w, m_w_out, m_post_norm_w, v_pre_norm_w, v_w_in, v_pool_mix_w, v_pool_scale, v_conv_w, v_conv_b, v_dt_bias, v_a_log, v_d_skip, v_ssd_norm_w, v_w_out, v_post_norm_w):
    NL, D, SH = w_in.shape
    PGW = D // GROUPS
    CS = conv_w.shape[2]
    chip = (2 * lax.axis_index("x") + lax.axis_index("y")).astype(jnp.int32)
    chip1, core = chip.reshape(1), lax.axis_index("c").astype(jnp.int32).reshape(1)

    tr = lambda t: jnp.transpose(t, (0, 2, 1))
    w_in_t, m_w_in_t, v_w_in_t = tr(w_in), tr(m_w_in), tr(v_w_in)
    halved_by_cols = [True, True, False, False]

    def shards(l):
        return [cast_bf16(w_in_t, l), cast_bf16(w_out, l),
                cast_bf16(pool_mix_w.reshape(NL, GROUPS * PGW // 4, PGW), l).reshape(2, GROUPS * PGW // 8, PGW),
                conv_w[l].reshape(2, CONV_K * CS // (2 * LANES), LANES)]

    def rest_params(g):
        g_out, g_mix, g_cw = g
        return dict(wout=g_out.reshape(2 * D, D),
                    mixw=g_mix.reshape(4, GROUPS, PGW // 4, PGW).transpose(1, 0, 2, 3).reshape(GROUPS, PGW, PGW),
                    cw=g_cw.reshape(4, CONV_K, CS).transpose(1, 0, 2).reshape(CONV_K, 4 * CS))

    def params(l, g):
        late = rest_params(g[1:]) if len(g) > 1 else dict(wout=None, mixw=None, cw=None)
        return layer_params(pre_norm_w[l], g[0].reshape(4 * SH, D), late["mixw"], pool_scale[l], late["cw"], conv_b[l],
                            dt_bias[l], a_log[l], d_skip[l], ssd_norm_w[l], late["wout"], post_norm_w[l], D)

    first = shards(0)
    gathered = run_jobs([gather_send_job(first[:1], [True])], "gather_send")[0]
    gathered = run_jobs([gather_pass_job(gathered, [True])], "gather_pass")[0]
    xl, h, ps, saved = x[0], None, [], []
    for l in range(NL):
        ps.append(params(l, gathered))
        last = l + 1 == NL
        xl, s, gathered, h = layer_fwd(xl, ps[l], D, None if last else shards(l + 1), h,
                                       None if last else pre_norm_w[l + 1][None],
                                       (first[1:], halved_by_cols[1:], rest_params) if l == 0 else None)
        saved.append(s)
    dx, sumsq = loss_head(xl, loss_target[0])

    given = dict(w_in=(w_in_t, m_w_in_t, v_w_in_t), w_out=(w_out, m_w_out, v_w_out),
                 pool_mix_w=(pool_mix_w, m_pool_mix_w, v_pool_mix_w))
    flat = {n: [t.reshape(NL, -1, t.shape[-1]) for t in given[n]] for n in BIG}
    done = {n: None for n in BIG}
    def update(l, reduced):
        for n, r in reduced.items():
            done[n] = adamw_layer(r.reshape(-1, r.shape[-1]), *flat[n], l, done[n], "adamw_" + n)

    grads, post, pending = [None] * NL, None, None
    for l in reversed(range(NL)):
        below = (saved[l - 1]["out"], ps[l - 1]["post_w"]) if l else None
        dx, grads[l], reduced, post, pending, above = layer_bwd(dx, ps[l], saved[l], D, (chip1, core), post, below,
                                                                pending)
        update(l, reduced)
        if above:
            update(l + 1, above)
    last = run_jobs([pair_gather_job(pending, [True, False])], "pair_gather")[0]
    update(0, dict(w_in=last[0], pool_mix_w=last[1]))

    small_shapes = [(NL,) + grads[0][n].shape for n in SMALL]
    packed = _pack([0.5 / D * sumsq[0, :1]] + [jnp.stack([g[n] for g in grads]) for n in SMALL])
    total = allreduce_small(packed)
    loss, *small = _unpack(total, [(1,)] + small_shapes)
    small = dict(zip(SMALL, small))
    small["conv_w"] = lax.dynamic_slice_in_dim(small["conv_w"], chip * CS, CS, axis=2)

    given_small = dict(
        pre_norm_w=(pre_norm_w, m_pre_norm_w, v_pre_norm_w), pool_scale=(pool_scale, m_pool_scale, v_pool_scale),
        conv_w=(conv_w, m_conv_w, v_conv_w), conv_b=(conv_b, m_conv_b, v_conv_b),
        dt_bias=(dt_bias, m_dt_bias, v_dt_bias), a_log=(a_log, m_a_log, v_a_log),
        d_skip=(d_skip, m_d_skip, v_d_skip), ssd_norm_w=(ssd_norm_w, m_ssd_norm_w, v_ssd_norm_w),
        post_norm_w=(post_norm_w, m_post_norm_w, v_post_norm_w))
    shapes = [given_small[n][0].shape for n in SMALL]
    upd = adamw(_pack([small[n] for n in SMALL]), *[_pack([given_small[n][i] for n in SMALL]) for i in range(3)],
                "adamw_small")
    upd = [dict(zip(SMALL, _unpack(u, shapes))) for u in upd]

    out = {n: (small[n], upd[0][n], upd[1][n], upd[2][n]) for n in SMALL}
    for n in BIG:
        out[n] = tuple(t.reshape(given[n][0].shape) for t in done[n])
    out["w_in"] = tuple(tr(t) for t in out["w_in"])

    order = ("pre_norm_w", "w_in", "pool_mix_w", "pool_scale", "conv_w", "conv_b", "dt_bias", "a_log", "d_skip",
             "ssd_norm_w", "w_out", "post_norm_w")
    return (loss.reshape(()), dx[None], *[out[n][0] for n in order], *[out[n][1] for n in order],
            *[out[n][2] for n in order], *[out[n][3] for n in order])
```

```python
import functools
import math

import jax
import jax.numpy as jnp
from jax import lax
from jax.experimental import pallas as pl
from jax.experimental.pallas import tpu as pltpu

F32 = jnp.float32
BF16 = jnp.bfloat16

NORM_EPS = 1e-6
HEAD_DIM = 64
STATE = 128
GROUPS = 4
POOL_WINDOWS = (2, 4, 8, 16)
POOL_HALO = 16
CONV_K = 4
CONV_HALO = 8
SCAN_CHUNK = 256
LANES = 128
VMEM_LIMIT = 52 * 1024 * 1024
ROW_TILE = 256

ADAM_LR = 0.001
ADAM_B1 = 0.9
ADAM_B2 = 0.999
ADAM_EPS = 1e-08
ADAM_WD = 0.01
ADAM_STEP = 10

MESH = pl.DeviceIdType.MESH

NN = (((1,), (0,)), ((), ()))
NT = (((1,), (1,)), ((), ()))
TN = (((0,), (0,)), ((), ()))

_ANY = pl.BlockSpec(memory_space=pl.ANY)


def _params(*sem):
    return pltpu.CompilerParams(dimension_semantics=sem, vmem_limit_bytes=VMEM_LIMIT)


def _pick(dim, pref):
    if dim <= pref:
        return dim
    t = (pref // LANES) * LANES
    while t > LANES and dim % t:
        t -= LANES
    assert dim % t == 0, (dim, pref)
    return t


def _rows_tile(rows, pref):
    t = (min(pref, rows) // 8) * 8
    while t >= 8 and rows % t:
        t -= 8
    return t if t >= 8 else rows


def _dot(a, b, dn=NN):
    return lax.dot_general(a, b, dn, preferred_element_type=F32)


def _split3(a):
    hi = a.astype(BF16)
    r = a - hi.astype(F32)
    mid = r.astype(BF16)
    return hi, mid, (r - mid.astype(F32)).astype(BF16)


def _dot_sel(a, e, parts=3):
    hi, mid, lo = _split3(a)
    return (_dot(lo, e) + _dot(mid, e)) + _dot(hi, e) if parts == 3 else _dot(mid, e) + _dot(hi, e)


def _sel_dot(e, b):
    hi, mid, lo = _split3(b)
    return (_dot(e, lo) + _dot(e, mid)) + _dot(e, hi)


def _sigmoid(v):
    return 0.5 * jnp.tanh(0.5 * v) + 0.5


def _silu_and_grad(v):
    s = _sigmoid(v)
    return v * s, s * (1.0 + v * (1.0 - s))


def _row(i, shape):
    return lax.broadcasted_iota(jnp.int32, shape, 0) + i


def _sds(shape, dtype):
    return jax.ShapeDtypeStruct(tuple(shape), dtype)


class Job:
    def __init__(self, ins, outs, aliased, nsem, start, finish):
        self.ins, self.outs, self.aliased, self.nsem, self.start, self.finish = ins, outs, aliased, nsem, start, finish


def _place():
    x, y, c = lax.axis_index("x"), lax.axis_index("y"), lax.axis_index("c")
    return x, y, c, [(1 - x, y), (x, 1 - y), (1 - x, 1 - y)]


def _remote(src, dst, send_sem, recv_sem, device):
    return pltpu.make_async_remote_copy(src_ref=src, dst_ref=dst, send_sem=send_sem, recv_sem=recv_sem,
                                        device_id=device, device_id_type=MESH)


def _half(ref, c, cols, lead=0):
    idx = [slice(None)] * lead
    if cols:
        w = ref.shape[-1] // 2
        idx += [slice(None)] * (len(ref.shape) - lead - 1) + [pl.ds(pl.multiple_of(c * w, LANES), w)]
    else:
        idx += [c]
    return ref.at[tuple(idx)]


def _flags(cols, n):
    return list(cols) if cols else [False] * n


def gather_send_job(arrs, cols=None):
    n = len(arrs)
    cols = _flags(cols, n)

    def copies(ins, outs, send, recv):
        x, y, c, chips = _place()
        mine = 2 * x + y
        out = []
        for a in range(n):
            out.append(_remote(ins[a], outs[a].at[mine], send.at[4 * a + 3], recv.at[4 * a + 3], (x, y, 1 - c)))
            for j, chip in enumerate(chips):
                out.append(_remote(_half(ins[a], c, cols[a]), _half(outs[a].at[mine], c, cols[a]),
                                   send.at[4 * a + j], recv.at[4 * a + j], (*chip, c)))
        return out

    def start(ins, outs, send, recv):
        for cp in copies(ins, outs, send, recv):
            cp.start()

    def finish(ins, outs, send, recv):
        x, y, c, chips = _place()
        for a in range(n):
            for j, chip in enumerate(chips):
                landed = _half(outs[a].at[2 * chip[0] + chip[1]], c, cols[a])
                _remote(landed, landed, send.at[4 * a + j], recv.at[4 * a + j], (x, y, 1 - c)).wait_recv()
            twin = outs[a].at[2 * x + y]
            _remote(twin, twin, send.at[4 * a + 3], recv.at[4 * a + 3], (x, y, 1 - c)).wait_recv()
        for cp in copies(ins, outs, send, recv):
            cp.wait_send()

    return Job(list(arrs), [_sds((4,) + a.shape, a.dtype) for a in arrs], False, 4 * n, start, finish)


def gather_rows_job(shard, start, size, into=None):
    rows = pl.ds(start, size)

    def copies(ins, outs, send, recv):
        x, y, c, chips = _place()
        mine = outs[0].at[2 * x + y, rows]
        out = [_remote(ins[0].at[rows], mine, send.at[3], recv.at[3], (x, y, 1 - c))]
        for j, chip in enumerate(chips):
            out.append(_remote(_half(ins[0].at[rows], c, True), _half(mine, c, True), send.at[j], recv.at[j],
                               (*chip, c)))
        return out

    def start_(ins, outs, send, recv):
        for cp in copies(ins, outs, send, recv):
            cp.start()

    def finish(ins, outs, send, recv):
        x, y, c, chips = _place()
        for j, chip in enumerate(chips):
            landed = _half(outs[0].at[2 * chip[0] + chip[1], rows], c, True)
            _remote(landed, landed, send.at[j], recv.at[j], (x, y, 1 - c)).wait_recv()
        twin = outs[0].at[2 * x + y, rows]
        _remote(twin, twin, send.at[3], recv.at[3], (x, y, 1 - c)).wait_recv()
        for cp in copies(ins, outs, send, recv):
            cp.wait_send()

    ins = [shard] if into is None else [shard, into]
    return Job(ins, [_sds((4,) + shard.shape, shard.dtype)], {} if into is None else {1: 0}, 4, start_, finish)


def gather_pass_job(bufs, cols=None):
    n = len(bufs)
    cols = _flags(cols, n)

    def copies(outs, send, recv):
        x, y, c, chips = _place()
        out = []
        for a in range(n):
            for j, chip in enumerate(chips):
                landed = _half(outs[a].at[2 * chip[0] + chip[1]], c, cols[a])
                out.append(_remote(landed, landed, send.at[3 * a + j], recv.at[3 * a + j], (x, y, 1 - c)))
        return out

    def start(ins, outs, send, recv):
        for cp in copies(outs, send, recv):
            cp.start()

    def finish(ins, outs, send, recv):
        x, y, c, chips = _place()
        for a in range(n):
            for j, chip in enumerate(chips):
                passed = _half(outs[a].at[2 * chip[0] + chip[1]], 1 - c, cols[a])
                _remote(passed, passed, send.at[3 * a + j], recv.at[3 * a + j], (x, y, 1 - c)).wait_recv()
        for cp in copies(outs, send, recv):
            cp.wait_send()

    return Job(list(bufs), [_sds(b.shape, b.dtype) for b in bufs], True, 3 * n, start, finish)


def pair_exchange_job(arrs, cols=None):
    n = len(arrs)
    cols = _flags(cols, n)

    def copies(ins, outs, send, recv):
        x, y, c, _ = _place()
        return [_remote(_half(ins[a], 1 - c, cols[a], 1), outs[a], send.at[a], recv.at[a], (x, y, 1 - c))
                for a in range(n)]

    def start(ins, outs, send, recv):
        for cp in copies(ins, outs, send, recv):
            cp.start()

    def finish(ins, outs, send, recv):
        for cp in copies(ins, outs, send, recv):
            cp.wait()

    shape = lambda a, k: a.shape[:-1] + (a.shape[-1] // 2,) if k else a.shape[:1] + a.shape[2:]
    return Job(list(arrs), [_sds(shape(a, k), a.dtype) for a, k in zip(arrs, cols)], False, n, start, finish)


def chip_exchange_job(arrs):
    n = len(arrs)

    def copies(ins, outs, send, recv):
        x, y, c, chips = _place()
        return [_remote(ins[a].at[2 * chip[0] + chip[1]], outs[a].at[j], send.at[3 * a + j], recv.at[3 * a + j],
                        (*chip, c)) for a in range(n) for j, chip in enumerate(chips)]

    def start(ins, outs, send, recv):
        for cp in copies(ins, outs, send, recv):
            cp.start()

    def finish(ins, outs, send, recv):
        for cp in copies(ins, outs, send, recv):
            cp.wait()

    return Job(list(arrs), [_sds((3,) + a.shape[1:], a.dtype) for a in arrs], False, 3 * n, start, finish)


def chip_exchange_rows_job(p, start, size, into=None):
    rows = pl.ds(start, size)

    def copies(ins, outs, send, recv):
        x, y, c, chips = _place()
        return [_remote(ins[0].at[2 * chip[0] + chip[1], rows], outs[0].at[j, rows], send.at[j], recv.at[j], (*chip, c))
                for j, chip in enumerate(chips)]

    def start_(ins, outs, send, recv):
        for cp in copies(ins, outs, send, recv):
            cp.start()

    def finish(ins, outs, send, recv):
        for cp in copies(ins, outs, send, recv):
            cp.wait()

    ins = [p] if into is None else [p, into]
    return Job(ins, [_sds((3,) + p.shape[1:], p.dtype)], {} if into is None else {1: 0}, 3, start_, finish)


def pair_gather_job(bufs, cols=None):
    n = len(bufs)
    cols = _flags(cols, n)

    def copies(outs, send, recv):
        x, y, c, _ = _place()
        return [_remote(_half(outs[a], c, cols[a]), _half(outs[a], c, cols[a]), send.at[a], recv.at[a],
                        (x, y, 1 - c)) for a in range(n)]

    def start(ins, outs, send, recv):
        for cp in copies(outs, send, recv):
            cp.start()

    def finish(ins, outs, send, recv):
        for cp in copies(outs, send, recv):
            cp.wait()

    return Job(list(bufs), [_sds(b.shape, b.dtype) for b in bufs], True, n, start, finish)


def _call(body, *, grid, in_specs, out_specs, out_shape, name, sem, args, scratch_shapes=(), jobs=(), aliases=None):
    in_specs, out_specs, out_shape, scratch_shapes = list(in_specs), list(out_specs), list(out_shape), list(scratch_shapes)
    aliases = dict(aliases or {})
    n_in, n_out, n_scr = len(in_specs), len(out_specs), len(scratch_shapes)
    if jobs:
        sem = ("arbitrary",) * len(grid)
    at_in, at_out = n_in, n_out
    for j in jobs:
        pairs = j.aliased if isinstance(j.aliased, dict) else {i: i for i in range(len(j.ins))} if j.aliased else {}
        aliases.update({at_in + i: at_out + o for i, o in pairs.items()})
        at_in, at_out = at_in + len(j.ins), at_out + len(j.outs)

    def wrapped(*refs):
        ins, p = refs[:n_in], n_in
        jins = []
        for j in jobs:
            jins.append(refs[p:p + len(j.ins)])
            p += len(j.ins)
        outs, p = refs[p:p + n_out], p + n_out
        jouts = []
        for j in jobs:
            jouts.append(refs[p:p + len(j.outs)])
            p += len(j.outs)
        scr, sems = refs[p:p + n_scr], refs[p + n_scr:]

        def start():
            for k, j in enumerate(jobs):
                j.start(jins[k], jouts[k], sems[2 * k], sems[2 * k + 1])

        def finish():
            for k, j in enumerate(jobs):
                j.finish(jins[k], jouts[k], sems[2 * k], sems[2 * k + 1])

        if jobs and grid:
            ids = [pl.program_id(d) for d in range(len(grid))]
            pl.when(functools.reduce(jnp.logical_and, [i == 0 for i in ids]))(start)
            body(*ins, *outs, *scr)
            pl.when(functools.reduce(jnp.logical_and, [i == g - 1 for i, g in zip(ids, grid)]))(finish)
        else:
            start()
            body(*ins, *outs, *scr)
            finish()

    kwargs = dict(grid=grid) if grid else {}
    res = pl.pallas_call(
        wrapped, in_specs=in_specs + [_ANY] * (at_in - n_in), out_specs=out_specs + [_ANY] * (at_out - n_out),
        out_shape=out_shape + [o for j in jobs for o in j.outs],
        scratch_shapes=scratch_shapes + [pltpu.SemaphoreType.DMA((j.nsem,)) for j in jobs for _ in range(2)],
        input_output_aliases=aliases, name=name,
        compiler_params=pltpu.CompilerParams(dimension_semantics=sem, vmem_limit_bytes=VMEM_LIMIT) if grid
        else pltpu.CompilerParams(vmem_limit_bytes=VMEM_LIMIT), **kwargs)(*args, *[a for j in jobs for a in j.ins])
    res = list(res)
    outs, rest, per_job = res[:n_out], res[n_out:], []
    for j in jobs:
        per_job.append(rest[:len(j.outs)])
        rest = rest[len(j.outs):]
    return outs, per_job


def run_jobs(jobs, name):
    return _call(lambda: None, grid=(), in_specs=[], out_specs=[], out_shape=[], name=name, sem=(), args=(), jobs=jobs)[1]


def pair_add(g, r, core, out_dtype, cols=False, out_rows=None):
    S, R, C = r.shape
    T = _rows_tile(R, 2 * ROW_TILE if C <= 1024 else ROW_TILE if C <= 4096 else ROW_TILE // 2)

    def body(c_ref, g_ref, r_ref, o_ref):
        o_ref[0] = ((g_ref[0] if cols else g_ref[0, 0]).astype(F32) + r_ref[0].astype(F32)).astype(o_ref.dtype)

    g_spec = (pl.BlockSpec((1, T, C), lambda k, i, c_ref: (k, i, c_ref[0])) if cols
              else pl.BlockSpec((1, 1, T, C), lambda k, i, c_ref: (k, c_ref[0], i, 0)))
    return pl.pallas_call(
        body,
        grid_spec=pltpu.PrefetchScalarGridSpec(
            num_scalar_prefetch=1, grid=(S, R // T),
            in_specs=[g_spec, pl.BlockSpec((1, T, C), lambda k, i, c_ref: (k, i, 0))],
            out_specs=pl.BlockSpec((1, T, C), lambda k, i, c_ref: (k, i, 0))),
        out_shape=_sds((S, out_rows or R, C), out_dtype), name="pair_add",
        compiler_params=_params("parallel", "parallel"))(core, g, r)


def chip_add(p, r, chip, core, cols=False):
    _, R, C = p.shape
    T = _rows_tile(R, ROW_TILE)

    def body(k_ref, c_ref, p_ref, r0_ref, r1_ref, r2_ref, o_ref):
        s = ((p_ref[0].astype(F32) + r0_ref[0].astype(F32)) + r1_ref[0].astype(F32)) + r2_ref[0].astype(F32)
        if cols:
            o_ref[...] = s
        else:
            o_ref[0] = s

    slot = lambda j: pl.BlockSpec((1, T, C), lambda i, k_ref, c_ref: (j, i, 0))
    out_spec = (pl.BlockSpec((T, C), lambda i, k_ref, c_ref: (i, c_ref[0])) if cols
                else pl.BlockSpec((1, T, C), lambda i, k_ref, c_ref: (c_ref[0], i, 0)))
    return pl.pallas_call(
        body,
        grid_spec=pltpu.PrefetchScalarGridSpec(
            num_scalar_prefetch=2, grid=(R // T,),
            in_specs=[pl.BlockSpec((1, T, C), lambda i, k_ref, c_ref: (k_ref[0], i, 0)), slot(0), slot(1), slot(2)],
            out_specs=out_spec),
        out_shape=_sds((R, 2 * C) if cols else (2, R, C), F32), name="chip_add",
        compiler_params=_params("parallel"))(chip, core, p, r, r, r)


def allreduce_small(v):
    R = v.shape[0]

    def body(v_ref, o_ref, buf, send_sems, recv_sems, local_sem):
        x, y, c, chips = _place()
        me, sibling = (x, y, c), (x, y, 1 - c)

        def rows(px, py, pc):
            return buf.at[pl.ds((4 * px + 2 * py + pc) * R, R), :]

        def copy(k, block, to, src=None):
            return _remote(rows(*block) if src is None else src, rows(*block), send_sems.at[k], recv_sems.at[k], to)

        mine = pltpu.make_async_copy(v_ref, rows(*me), local_sem)
        mine.start()
        first = [copy(0, me, sibling, src=v_ref)]
        first += [copy(1 + j, me, (*chip, c), src=v_ref) for j, chip in enumerate(chips)]
        for cp in first:
            cp.start()
        passed = [copy(4 + j, (*chip, c), sibling) for j, chip in enumerate(chips)]
        for j, chip in enumerate(chips):
            copy(1 + j, (*chip, c), me).wait_recv()
            passed[j].start()
        copy(0, sibling, me).wait_recv()
        for j, chip in enumerate(chips):
            copy(4 + j, (*chip, 1 - c), me).wait_recv()
        for cp in first + passed:
            cp.wait_send()
        mine.wait()
        acc = buf[0:R, :]
        for d in range(1, 8):
            acc = acc + buf[d * R:(d + 1) * R, :]
        o_ref[...] = acc

    return pl.pallas_call(
        body, in_specs=[pl.BlockSpec(memory_space=pltpu.VMEM)], out_specs=pl.BlockSpec(memory_space=pltpu.VMEM),
        out_shape=_sds((R, LANES), F32),
        scratch_shapes=[pltpu.VMEM((8 * R, LANES), F32), pltpu.SemaphoreType.DMA((7,)),
                        pltpu.SemaphoreType.DMA((7,)), pltpu.SemaphoreType.DMA],
        name="allreduce_small", compiler_params=pltpu.CompilerParams(vmem_limit_bytes=VMEM_LIMIT))(v)


def matmul(a, b, mode, out_dtype, name, tm=512, tn=1024, tk=4608, jobs=(), n_out=None):
    if mode == "nn":
        (M, K), (K2, N) = a.shape, b.shape
    elif mode == "nt":
        (M, K), (N, K2) = a.shape, b.shape
        N = n_out or N
    else:
        (K, M), (K2, N) = a.shape, b.shape
    assert K == K2 or (mode == "nn" and K2 > K)
    tm, tn, tk = _pick(M, tm), _pick(N, tn), _pick(K, tk)
    nk = K // tk
    dn = {"nn": NN, "nt": NT, "tn": TN}[mode]

    def body(a_ref, b_ref, o_ref, *acc):
        part = _dot(a_ref[...].astype(BF16), b_ref[...].astype(BF16), dn)
        if nk == 1:
            o_ref[...] = part.astype(o_ref.dtype)
            return
        acc_ref, = acc
        k = pl.program_id(2)

        @pl.when(k == 0)
        def _():
            acc_ref[...] = part

        @pl.when(jnp.logical_and(k > 0, k < nk - 1))
        def _():
            acc_ref[...] += part

        @pl.when(k == nk - 1)
        def _():
            o_ref[...] = (acc_ref[...] + part).astype(o_ref.dtype)

    a_spec = (pl.BlockSpec((tk, tm), lambda i, j, k: (k, i)) if mode == "tn"
              else pl.BlockSpec((tm, tk), lambda i, j, k: (i, k)))
    b_spec = (pl.BlockSpec((tn, tk), lambda i, j, k: (j, k)) if mode == "nt"
              else pl.BlockSpec((tk, tn), lambda i, j, k: (k, j)))
    outs, per_job = _call(
        body, grid=(M // tm, N // tn, nk), in_specs=[a_spec, b_spec],
        out_specs=[pl.BlockSpec((tm, tn), lambda i, j, k: (i, j))], out_shape=[_sds((M, N), out_dtype)],
        scratch_shapes=[pltpu.VMEM((tm, tn), F32)] if nk > 1 else [], name=name,
        sem=("parallel", "parallel", "arbitrary"), args=(a, b), jobs=jobs)
    return outs[0], per_job


def rms_fwd(x, w):
    L, D = x.shape
    T = _pick(L, ROW_TILE)

    def body(x_ref, w_ref, h_ref):
        xv = x_ref[...]
        r = lax.rsqrt(jnp.mean(xv * xv, axis=-1, keepdims=True) + NORM_EPS)
        h_ref[...] = (xv * r * w_ref[...]).astype(h_ref.dtype)

    return pl.pallas_call(
        body, grid=(L // T,),
        in_specs=[pl.BlockSpec((T, D), lambda i: (i, 0)), pl.BlockSpec((1, D), lambda i: (0, 0))],
        out_specs=pl.BlockSpec((T, D), lambda i: (i, 0)),
        out_shape=_sds((L, D), BF16), name="rms_fwd", compiler_params=_params("parallel"))(x, w)


def post_fwd(x, o, w, next_w=None, jobs=()):
    L, D = x.shape
    T = _pick(L, ROW_TILE)

    def body(x_ref, o_ref, w_ref, *rest):
        ov = o_ref[...]
        r = lax.rsqrt(jnp.mean(ov * ov, axis=-1, keepdims=True) + NORM_EPS)
        y = x_ref[...] + ov * r * w_ref[...]
        rest[-1 if next_w is None else -2][...] = y
        if next_w is not None:
            r2 = lax.rsqrt(jnp.mean(y * y, axis=-1, keepdims=True) + NORM_EPS)
            rest[-1][...] = (y * r2 * rest[0][...]).astype(BF16)

    row = pl.BlockSpec((T, D), lambda i: (i, 0))
    vec = pl.BlockSpec((1, D), lambda i: (0, 0))
    more = [] if next_w is None else [next_w]
    outs, per_job = _call(
        body, grid=(L // T,), in_specs=[row, row, vec] + [vec] * len(more), out_specs=[row] * (1 + len(more)),
        out_shape=[_sds((L, D), F32)] + [_sds((L, D), BF16)] * len(more), name="post_fwd", sem=("parallel",),
        args=(x, o, w, *more), jobs=jobs)
    return (outs[0], outs[1] if more else None), per_job


def _rms_bwd_math(xv, w, dy):
    r = lax.rsqrt(jnp.mean(xv * xv, axis=-1, keepdims=True) + NORM_EPS)
    xhat = xv * r
    g = dy * w
    dx = r * (g - xhat * jnp.mean(g * xhat, axis=-1, keepdims=True))
    return dx, jnp.sum(dy * xhat, axis=0, keepdims=True)


def post_bwd(o, w, dxn):
    L, D = o.shape
    T = _pick(L, ROW_TILE)

    def body(o_ref, w_ref, d_ref, do_ref, dw_ref):
        dx, dw = _rms_bwd_math(o_ref[...], w_ref[...], d_ref[...])
        do_ref[...] = dx.astype(do_ref.dtype)

        @pl.when(pl.program_id(0) == 0)
        def _():
            dw_ref[...] = jnp.zeros_like(dw_ref)

        dw_ref[...] += dw

    row = pl.BlockSpec((T, D), lambda i: (i, 0))
    vec = pl.BlockSpec((1, D), lambda i: (0, 0))
    return pl.pallas_call(
        body, grid=(L // T,), in_specs=[row, vec, row], out_specs=[row, vec],
        out_shape=[_sds((L, D), BF16), _sds((1, D), F32)],
        name="post_bwd", compiler_params=_params("arbitrary"))(o, w, dxn)


def rms_bwd(x, w, dh_a, dh_b, dxn, below=None, jobs=()):
    L, D = x.shape
    T = _pick(L, ROW_TILE)

    def body(x_ref, w_ref, a_ref, b_ref, d_ref, *rest):
        dx, dw = _rms_bwd_math(x_ref[...], w_ref[...], a_ref[...] + b_ref[...])
        dx = d_ref[...] + dx
        outs = rest[2:] if below else rest
        outs[0][...] = dx

        @pl.when(pl.program_id(0) == 0)
        def _():
            for acc in outs[1::2]:
                acc[...] = jnp.zeros_like(acc)

        outs[1][...] += dw
        if below:
            do, dwp = _rms_bwd_math(rest[0][...], rest[1][...], dx)
            outs[2][...] = do.astype(BF16)
            outs[3][...] += dwp

    row = pl.BlockSpec((T, D), lambda i: (i, 0))
    vec = pl.BlockSpec((1, D), lambda i: (0, 0))
    more = list(below) if below else []
    return _call(
        body, grid=(L // T,), in_specs=[row, vec, row, row, row] + ([row, vec] if below else []),
        out_specs=[row, vec] + ([row, vec] if below else []),
        out_shape=[_sds((L, D), F32), _sds((1, D), F32)] + ([_sds((L, D), BF16), _sds((1, D), F32)] if below else []),
        name="rms_bwd", sem=("arbitrary",), args=(x, w, dh_a, dh_b, dxn, *more), jobs=jobs)


def loss_head(y, target):
    L, D = y.shape
    T = _pick(L, ROW_TILE)

    def body(y_ref, t_ref, d_ref, s_ref):
        e = y_ref[...] - t_ref[...]
        d_ref[...] = e * (1.0 / D)

        @pl.when(pl.program_id(0) == 0)
        def _():
            s_ref[...] = jnp.zeros_like(s_ref)

        s_ref[...] += jnp.sum(e * e)

    row = pl.BlockSpec((T, D), lambda i: (i, 0))
    return pl.pallas_call(
        body, grid=(L // T,), in_specs=[row, row],
        out_specs=[row, pl.BlockSpec((8, LANES), lambda i: (0, 0))],
        out_shape=[_sds((L, D), F32), _sds((8, LANES), F32)],
        name="loss_head", compiler_params=_params("arbitrary"))(y, target)


def _window_sums(xe, w, back):
    n = xe.shape[0]
    s, k = xe, 1
    while k < w:
        s = s + pltpu.roll(s, k if back else n - k, 0)
        k *= 2
    return s


def pool_fwd(proj, mixw, scale, D):
    L = proj.shape[0]
    PGW = D // GROUPS
    T = _pick(L, ROW_TILE)
    hb = T // POOL_HALO

    def body(u_ref, halo_ref, g_ref, mw_ref, sc_ref, y_ref, p_ref):
        i = pl.program_id(0)
        u = u_ref[...]
        halo = jnp.where(i > 0, halo_ref[...], 0.0)
        xe = jnp.concatenate([halo, u], axis=0)
        t1 = _row(i * T + 1, (T, 1))
        for g, w in enumerate(POOL_WINDOWS):
            sl = slice(g * PGW, (g + 1) * PGW)
            win = _window_sums(xe[:, sl], w, True)[POOL_HALO:, :]
            inv = 1.0 / jnp.minimum(t1, w).astype(F32)
            pooled = (win * inv - u[:, sl]).astype(BF16)
            p_ref[:, sl] = pooled
            mixed = _dot(pooled, mw_ref[g])
            gate = g_ref[:, sl]
            y_ref[:, sl] = (mixed * sc_ref[:, sl] * (gate * _sigmoid(gate))).astype(BF16)

    return pl.pallas_call(
        body, grid=(L // T,),
        in_specs=[pl.BlockSpec((T, D), lambda i: (i, 0)),
                  pl.BlockSpec((POOL_HALO, D), lambda i: (jnp.maximum(i * hb - 1, 0), 0)),
                  pl.BlockSpec((T, D), lambda i: (i, 1)),
                  pl.BlockSpec((GROUPS, PGW, PGW), lambda i: (0, 0, 0)),
                  pl.BlockSpec((1, D), lambda i: (0, 0))],
        out_specs=[pl.BlockSpec((T, D), lambda i: (i, 0)), pl.BlockSpec((T, D), lambda i: (i, 0))],
        out_shape=[_sds((L, 2 * D), BF16), _sds((L, D), BF16)],
        name="pool_fwd", compiler_params=_params("parallel"))(proj, proj, proj, mixw, scale)


def pool_bwd(proj, dmix, pooled, mixw, scale, dproj, D):
    L = proj.shape[0]
    PGW = D // GROUPS
    T = _pick(L, ROW_TILE)
    hb = T // POOL_HALO
    nT = L // T

    def body(g_ref, gh_ref, dy_ref, dyh_ref, p_ref, mw_ref, sc_ref, old_ref, dp_ref, dm_ref, ds_ref):
        i = pl.program_id(0)
        t1 = _row(i * T + 1, (T, 1))
        th1 = _row((i + 1) * T + 1, (POOL_HALO, 1))
        live = i < nT - 1

        @pl.when(i == 0)
        def _():
            ds_ref[...] = jnp.zeros_like(ds_ref)
            dm_ref[...] = jnp.zeros_like(dm_ref)

        for g, w in enumerate(POOL_WINDOWS):
            sl = slice(g * PGW, (g + 1) * PGW)
            sc = sc_ref[:, sl]
            gate, dy = g_ref[:, sl], dy_ref[:, sl]
            sg, dsg = _silu_and_grad(gate)
            pooled = p_ref[:, sl]
            mixed = _dot(pooled, mw_ref[g])
            dmixed = (dy * sc * sg).astype(BF16)
            dm_ref[g] += _dot(pooled, dmixed, TN)
            dp_ref[:, D + g * PGW:D + (g + 1) * PGW] = (dy * mixed * sc * dsg).astype(BF16)
            ds_ref[:, sl] += jnp.sum(dy * mixed * sg, axis=0, keepdims=True)
            dpool = _dot(dmixed, mw_ref[g], NT)
            gate_h = gh_ref[:, sl]
            dmixed_h = (dyh_ref[:, sl] * sc * (gate_h * _sigmoid(gate_h))).astype(BF16)
            dpool_h = jnp.where(live, _dot(dmixed_h, mw_ref[g], NT), 0.0)
            q = dpool * (1.0 / jnp.minimum(t1, w).astype(F32))
            q_h = dpool_h * (1.0 / jnp.minimum(th1, w).astype(F32))
            qe = jnp.concatenate([q, q_h], axis=0)
            dp_ref[:, sl] = (_window_sums(qe, w, False)[:T, :] - dpool).astype(BF16)

    nxt = lambda i: jnp.minimum((i + 1) * hb, L // POOL_HALO - 1)
    row = lambda c: pl.BlockSpec((T, D), lambda i: (i, c))
    return pl.pallas_call(
        body, grid=(nT,),
        in_specs=[row(1), pl.BlockSpec((POOL_HALO, D), lambda i: (nxt(i), 1)),
                  row(0), pl.BlockSpec((POOL_HALO, D), lambda i: (nxt(i), 0)),
                  row(0), pl.BlockSpec((GROUPS, PGW, PGW), lambda i: (0, 0, 0)),
                  pl.BlockSpec((1, D), lambda i: (0, 0)), _ANY],
        out_specs=[pl.BlockSpec((T, 2 * D), lambda i: (i, 0)), pl.BlockSpec((GROUPS, PGW, PGW), lambda i: (0, 0, 0)),
                   pl.BlockSpec((1, D), lambda i: (0, 0))],
        out_shape=[_sds(dproj.shape, dproj.dtype), _sds((GROUPS, PGW, PGW), F32), _sds((1, D), F32)],
        input_output_aliases={7: 0},
        name="pool_bwd", compiler_params=_params("arbitrary"))(proj, proj, dmix, dmix, pooled, mixw, scale, dproj)


def conv_fwd(proj, cw, cb, D):
    L = proj.shape[0]
    C = cw.shape[1]
    assert (3 * D) % C == 0
    cblk = (3 * D) // C
    T = _pick(L, ROW_TILE)
    hb = T // CONV_HALO

    def body(u_ref, halo_ref, w_ref, b_ref, o_ref):
        i = pl.program_id(0)
        u = u_ref[...]
        xe = jnp.concatenate([jnp.where(i > 0, halo_ref[...], 0.0), u], axis=0)
        acc = b_ref[...] + w_ref[CONV_K - 1:CONV_K, :] * u
        for k in range(CONV_K - 1):
            acc = acc + w_ref[k:k + 1, :] * pltpu.roll(xe, CONV_K - 1 - k, 0)[CONV_HALO:, :]
        o_ref[...] = acc

    return pl.pallas_call(
        body, grid=(L // T,),
        in_specs=[pl.BlockSpec((T, C), lambda i: (i, cblk)),
                  pl.BlockSpec((CONV_HALO, C), lambda i: (jnp.maximum(i * hb - 1, 0), cblk)),
                  pl.BlockSpec((CONV_K, C), lambda i: (0, 0)),
                  pl.BlockSpec((1, C), lambda i: (0, 0))],
        out_specs=pl.BlockSpec((T, C), lambda i: (i, 0)),
        out_shape=_sds((L, C), F32), name="conv_fwd", compiler_params=_params("parallel"))(proj, proj, cw, cb)


def conv_bwd(dparts, proj, cw, dproj, D, jobs=()):
    L = proj.shape[0]
    C = cw.shape[1]
    cblk = (3 * D) // C
    T = _pick(L, ROW_TILE)
    hb = T // CONV_HALO
    nT = L // T
    widths = [p.shape[1] for p in dparts]
    assert sum(widths) == C
    n = len(dparts)

    def body(*refs):
        d_refs, dn_refs = refs[:n], refs[n:2 * n]
        u_ref, w_ref, old_ref, dr_ref, dw_ref, db_ref = refs[2 * n:]
        i = pl.program_id(0)

        @pl.when(i == 0)
        def _():
            dw_ref[...] = jnp.zeros_like(dw_ref)
            db_ref[...] = jnp.zeros_like(db_ref)

        at = 0
        for d_ref, dn_ref, wd in zip(d_refs, dn_refs, widths):
            sl = slice(at, at + wd)
            at += wd
            d = d_ref[...]
            u = u_ref[:, sl]
            de = jnp.concatenate([d, jnp.where(i < nT - 1, dn_ref[...], 0.0)], axis=0)
            acc = w_ref[CONV_K - 1:CONV_K, sl] * d
            dw_ref[CONV_K - 1:CONV_K, sl] += jnp.sum(d * u, axis=0, keepdims=True)
            for k in range(CONV_K - 1):
                sh = CONV_K - 1 - k
                ds = pltpu.roll(de, T + CONV_HALO - sh, 0)[:T, :]
                acc = acc + w_ref[k:k + 1, sl] * ds
                dw_ref[k:k + 1, sl] += jnp.sum(ds * u, axis=0, keepdims=True)
            dr_ref[:, sl] = acc.astype(dr_ref.dtype)
            db_ref[:, sl] += jnp.sum(d, axis=0, keepdims=True)

    nxt = lambda i: jnp.minimum((i + 1) * hb, L // CONV_HALO - 1)
    return _call(
        body, grid=(nT,),
        in_specs=[pl.BlockSpec((T, wd), lambda i: (i, 0)) for wd in widths]
        + [pl.BlockSpec((CONV_HALO, wd), lambda i: (nxt(i), 0)) for wd in widths]
        + [pl.BlockSpec((T, C), lambda i: (i, cblk)), pl.BlockSpec((CONV_K, C), lambda i: (0, 0)), _ANY],
        out_specs=[pl.BlockSpec((T, C), lambda i: (i, cblk)),
                   pl.BlockSpec((8, C), lambda i: (0, 0)),
                   pl.BlockSpec((1, C), lambda i: (0, 0))],
        out_shape=[_sds(dproj.shape, dproj.dtype), _sds((8, C), F32), _sds((1, C), F32)],
        aliases={2 * n + 2: 0}, name="conv_bwd", sem=("arbitrary",),
        args=(*dparts, *dparts, proj, cw, dproj), jobs=jobs)


def _softplus(v):
    y = jnp.exp(-jnp.abs(v))
    u = 1.0 + y
    log1p = jnp.where(u == 1.0, y, jnp.log(u) * y / jnp.where(u == 1.0, 1.0, u - 1.0))
    return jnp.maximum(v, 0.0) + log1p


def dt_prep(dtraw, bias, alog, expand, D, jobs=()):
    L = dtraw.shape[0]
    GC = D // GROUPS
    HPG = GC // HEAD_DIM
    Q = _pick(L, SCAN_CHUNK)
    nc = L // Q

    def body(r_ref, b_ref, a_ref, e_ref, dt_ref, acs_ref, acst_ref, ex_ref, cdx_ref):
        valid = lax.broadcasted_iota(jnp.int32, (1, LANES), 1) < HPG
        tril = (_row(0, (Q, Q)) >= lax.broadcasted_iota(jnp.int32, (Q, Q), 1)).astype(BF16)
        e = e_ref[...]
        for g in range(GROUPS):
            hs, cs = slice(g * LANES, (g + 1) * LANES), slice(g * GC, (g + 1) * GC)
            dt = jnp.where(valid, _softplus(r_ref[:, hs] + b_ref[:, hs]), 0.0)
            adt = dt * -jnp.exp(a_ref[:, hs])
            acs = _sel_dot(tril, adt)
            last = acs[Q - 1:Q, :]
            dt_ref[:, hs] = dt
            acs_ref[:, hs] = acs
            acst_ref[hs, :] = acs.T
            ex_ref[0, :, cs] = _dot_sel(dt, e, 2)
            ex_ref[1, :, cs] = jnp.exp(_dot_sel(acs, e, 2))
            ex_ref[2, :, cs] = jnp.exp(_dot_sel(last - acs, e, 2))
            cdx_ref[0, :, cs] = jnp.exp(_dot_sel(jnp.broadcast_to(last, (8, LANES)), e, 2))

    head = pl.BlockSpec((Q, GROUPS * LANES), lambda c: (c, 0))
    hvec = pl.BlockSpec((1, GROUPS * LANES), lambda c: (0, 0))
    return _call(
        body, grid=(nc,),
        in_specs=[head, hvec, hvec, pl.BlockSpec((LANES, GC), lambda c: (0, 0))],
        out_specs=[head, head, pl.BlockSpec((GROUPS * LANES, Q), lambda c: (0, c)),
                   pl.BlockSpec((3, Q, D), lambda c: (0, c, 0)), pl.BlockSpec((1, 8, D), lambda c: (c, 0, 0))],
        out_shape=[_sds((L, GROUPS * LANES), F32), _sds((L, GROUPS * LANES), F32), _sds((GROUPS * LANES, L), F32),
                   _sds((3, L, D), F32), _sds((nc, 8, D), F32)],
        name="dt_prep", sem=("parallel",), args=(dtraw, bias, alog, expand), jobs=jobs)


def _scan_specs(L, D, Q, rev):
    GC = D // GROUPS
    nc = L // Q
    ci = (lambda c: nc - 1 - c) if rev else (lambda c: c)
    return dict(
        xs=pl.BlockSpec((Q, GC), lambda g, c: (ci(c), g)),
        b=pl.BlockSpec((Q, STATE), lambda g, c: (ci(c), D // STATE + g)),
        c=pl.BlockSpec((Q, STATE), lambda g, c: (ci(c), D // STATE + GROUPS + g)),
        chan=pl.BlockSpec((Q, GC), lambda g, c: (ci(c), g)),
        ex=pl.BlockSpec((3, Q, GC), lambda g, c: (0, ci(c), g)),
        cdx=pl.BlockSpec((1, 8, GC), lambda g, c: (ci(c), 0, g)),
        head=pl.BlockSpec((Q, LANES), lambda g, c: (ci(c), g)),
        headt=pl.BlockSpec((LANES, Q), lambda g, c: (g, ci(c))),
        state=pl.BlockSpec((1, 1, STATE, GC), lambda g, c: (ci(c), g, 0, 0)),
        hvec=pl.BlockSpec((1, LANES), lambda g, c: (0, g)),
        cvec=pl.BlockSpec((1, GC), lambda g, c: (0, g)))


def scan_fwd(pre, ex, cdx, acs, acst, proj, dexp, nw, mixed, D, jobs=()):
    L = pre.shape[0]
    GC = D // GROUPS
    Q = _pick(L, SCAN_CHUNK)
    nc = L // Q
    sp = _scan_specs(L, D, Q, False)

    def body(xs_ref, b_ref, c_ref, ex_ref, cdx_ref, acs_ref, acst_ref, z_ref, de_ref, nw_ref,
             old_ref, y_ref, st_ref, o_ref, s_scr):
        @pl.when(pl.program_id(1) == 0)
        def _():
            s_scr[...] = jnp.zeros_like(s_scr)

        tri = _row(0, (Q, Q)) >= lax.broadcasted_iota(jnp.int32, (Q, Q), 1)
        half = lax.broadcasted_iota(jnp.int32, (1, LANES), 1) // HEAD_DIM
        xs, _ = _silu_and_grad(xs_ref[...])
        bg = _silu_and_grad(b_ref[...])[0].astype(BF16)
        cg = _silu_and_grad(c_ref[...])[0].astype(BF16)
        xdt = xs * ex_ref[0]
        sprev = s_scr[...]
        st_ref[0, 0] = sprev
        sc = _dot(cg, bg, NT)
        yoff = _dot(cg, sprev.astype(BF16)) * ex_ref[1]
        for j in range(GC // LANES):
            ps = slice(j * LANES, (j + 1) * LANES)
            xp = xdt[:, ps]
            acc = yoff[:, ps]
            for hh in range(2):
                h = 2 * j + hh
                lm = jnp.exp(jnp.where(tri, acs_ref[:, h:h + 1] - acst_ref[h:h + 1, :], -1e30))
                xm = jnp.where(half == hh, xp, 0.0).astype(BF16)
                acc = acc + _dot((sc * lm).astype(BF16), xm)
            y_ref[:, ps] = acc
        xw = (xdt * ex_ref[2]).astype(BF16)
        s_scr[...] = cdx_ref[0, 0:1, :] * sprev + _dot(bg, xw, TN)
        z = z_ref[...]
        y3 = (y_ref[...] + de_ref[...] * xs) * (z * _sigmoid(z))
        r = lax.rsqrt(jnp.mean(y3 * y3, axis=-1, keepdims=True) + NORM_EPS)
        o_ref[...] = (y3 * r * nw_ref[...]).astype(o_ref.dtype)

    return _call(
        body, grid=(GROUPS, nc),
        in_specs=[sp["xs"], sp["b"], sp["c"], sp["ex"], sp["cdx"], sp["head"], sp["headt"],
                  pl.BlockSpec((Q, GC), lambda g, c: (c, 2 * GROUPS + g)), sp["cvec"], sp["cvec"], _ANY],
        out_specs=[sp["chan"], sp["state"], pl.BlockSpec((Q, GC), lambda g, c: (c, GROUPS + g))],
        out_shape=[_sds((L, D), F32), _sds((nc, GROUPS, STATE, GC), F32), _sds(mixed.shape, mixed.dtype)],
        aliases={10: 2}, scratch_shapes=[pltpu.VMEM((STATE, GC), F32)], name="scan_fwd",
        sem=("parallel", "arbitrary"), args=(pre, pre, pre, ex, cdx, acs, acst, proj, dexp, nw, mixed),
        jobs=jobs)


def scan_bwd(pre, ex, cdx, acs, acst, dt, dtraw, bias, alog, states, y, proj, dmix, nw, dexp, collapse, D,
             jobs=()):
    L = pre.shape[0]
    GC = D // GROUPS
    Q = _pick(L, SCAN_CHUNK)
    nc = L // Q
    sp = _scan_specs(L, D, Q, True)
    rc = lambda c: nc - 1 - c

    def body(xs_ref, b_ref, c_ref, ex_ref, cdx_ref, acs_ref, acst_ref, dt_ref, raw_ref,
             bias_ref, alog_ref, st_ref, y_ref, z_ref, dm_ref, nw_ref, dexp_ref, col_ref,
             dxs_ref, db_ref, dc_ref, ddt_ref, dal_ref, dbi_ref, dz_ref, dnw_ref, dde_ref, ds_scr, dx_scr):
        first = pl.program_id(1) == 0

        @pl.when(first)
        def _():
            ds_scr[...] = jnp.zeros_like(ds_scr)
            dal_ref[...] = jnp.zeros_like(dal_ref)
            dbi_ref[...] = jnp.zeros_like(dbi_ref)
            dnw_ref[...] = jnp.zeros_like(dnw_ref)
            dde_ref[...] = jnp.zeros_like(dde_ref)

        li = _row(0, (Q, Q))
        si = lax.broadcasted_iota(jnp.int32, (Q, Q), 1)
        lane = lax.broadcasted_iota(jnp.int32, (1, LANES), 1)
        half = lane // HEAD_DIM
        xs_pre, b_pre, c_pre = xs_ref[...], b_ref[...], c_ref[...]
        xs, xs_g = _silu_and_grad(xs_pre)
        bf, b_g = _silu_and_grad(b_pre)
        cf, c_g = _silu_and_grad(c_pre)
        bg, cg = bf.astype(BF16), cf.astype(BF16)
        dtx, eax, dsx = ex_ref[0], ex_ref[1], ex_ref[2]
        cd = cdx_ref[0, 0:1, :]
        xdt = xs * dtx
        sz, dsz = _silu_and_grad(z_ref[...])
        y2 = y_ref[...] + dexp_ref[...] * xs
        y3 = y2 * sz
        r = lax.rsqrt(jnp.mean(y3 * y3, axis=-1, keepdims=True) + NORM_EPS)
        n = y3 * r
        dm = dm_ref[...]
        gg = dm * nw_ref[...]
        dy3 = r * (gg - n * jnp.mean(gg * n, axis=-1, keepdims=True))
        dnw_ref[0:1, :] += jnp.sum(dm * n, axis=0, keepdims=True)
        G = dy3 * sz
        dz_ref[...] = (dy3 * y2 * dsz).astype(dz_ref.dtype)
        dde_ref[0:1, :] += jnp.sum(G * xs, axis=0, keepdims=True)
        prev = st_ref[0, 0]
        dsn = ds_scr[...]
        prev_b, dsn_b = prev.astype(BF16), dsn.astype(BF16)
        cp = _dot(cg, prev_b)
        ge_b = (G * eax).astype(BF16)
        d_c = _dot(ge_b, prev_b, NT)
        dprev = _dot(cg, ge_b, TN) + cd * dsn
        chan_a = G * cp * eax
        xw_b = (xdt * dsx).astype(BF16)
        dcd = jnp.sum(prev * dsn, axis=0, keepdims=True)
        d_b = _dot(xw_b, dsn_b, NT)
        dxw = _dot(bg, dsn_b)
        dd = dxw * xdt * dsx
        chan_a = chan_a - dd
        last_c = jnp.sum(dd, axis=0, keepdims=True) + dcd * cd
        sc = _dot(cg, bg, NT)
        head_row = _row(0, (LANES, 1))
        dsc = jnp.zeros((Q, Q), F32)
        dacs = jnp.zeros((Q, LANES), F32)
        colsums = jnp.zeros((LANES, Q), F32)
        for j in range(GC // LANES):
            ps = slice(j * LANES, (j + 1) * LANES)
            xp, gp = xdt[:, ps], G[:, ps]
            dxp = dxw[:, ps] * dsx[:, ps]
            for hh in range(2):
                h = 2 * j + hh
                lm = jnp.exp(jnp.where(li >= si, acs_ref[:, h:h + 1] - acst_ref[h:h + 1, :], -1e30))
                m = sc * lm
                xm = jnp.where(half == hh, xp, 0.0).astype(BF16)
                gm = jnp.where(half == hh, gp, 0.0).astype(BF16)
                dm = _dot(gm, xm, NT)
                dxp = dxp + _dot(m.astype(BF16), gm, TN)
                dsc = dsc + dm * lm
                w = dm * m
                dacs = dacs + jnp.where(lane == h, jnp.sum(w, axis=1, keepdims=True), 0.0)
                colsums = jnp.where(head_row == h, jnp.sum(w, axis=0, keepdims=True), colsums)
            dx_scr[:, ps] = dxp
        dacs = dacs - colsums.T
        dsc_b = dsc.astype(BF16)
        d_c = d_c + _dot(dsc_b, bg)
        d_b = d_b + _dot(dsc_b, cg, TN)
        ds_scr[...] = dprev
        dxdt = dx_scr[...]
        dxs_ref[...] = (dxdt * dtx + dexp_ref[...] * G) * xs_g
        db_ref[...] = d_b * b_g
        dc_ref[...] = d_c * c_g
        colm = col_ref[...]
        dacs = dacs + _dot_sel(chan_a, colm)
        dlast = _dot_sel(jnp.broadcast_to(last_c, (8, GC)), colm)[0:1, :]
        dacs = dacs + jnp.where(_row(0, (Q, 1)) == Q - 1, dlast, 0.0)
        dadt = _sel_dot((si >= li).astype(BF16), dacs)
        a = -jnp.exp(alog_ref[...])
        dt = dt_ref[...]
        ddt = dadt * a + _dot_sel(dxdt * xs, colm, 2)
        dal_ref[0:1, :] += jnp.sum(dadt * dt * a, axis=0, keepdims=True)
        draw = ddt * _sigmoid(raw_ref[...] + bias_ref[...])
        dbi_ref[0:1, :] += jnp.sum(draw, axis=0, keepdims=True)
        ddt_ref[...] = draw.astype(ddt_ref.dtype)

    acc = pl.BlockSpec((8, LANES), lambda g, c: (0, g))
    cacc = pl.BlockSpec((8, GC), lambda g, c: (0, g))
    return _call(
        body, grid=(GROUPS, nc),
        in_specs=[sp["xs"], sp["b"], sp["c"], sp["ex"], sp["cdx"], sp["head"], sp["headt"],
                  sp["head"], sp["head"], sp["hvec"], sp["hvec"], sp["state"], sp["chan"],
                  pl.BlockSpec((Q, GC), lambda g, c: (rc(c), 2 * GROUPS + g)),
                  pl.BlockSpec((Q, GC), lambda g, c: (rc(c), GROUPS + g)), sp["cvec"], sp["cvec"],
                  pl.BlockSpec((GC, LANES), lambda g, c: (0, 0))],
        out_specs=[sp["chan"],
                   pl.BlockSpec((Q, STATE), lambda g, c: (rc(c), g)),
                   pl.BlockSpec((Q, STATE), lambda g, c: (rc(c), g)),
                   sp["head"], acc, acc,
                   pl.BlockSpec((Q, GC), lambda g, c: (rc(c), 2 * GROUPS + g)), cacc, cacc],
        out_shape=[_sds((L, D), F32), _sds((L, GROUPS * STATE), F32), _sds((L, GROUPS * STATE), F32),
                   _sds((L, GROUPS * LANES), BF16), _sds((8, GROUPS * LANES), F32), _sds((8, GROUPS * LANES), F32),
                   _sds((L, proj.shape[1]), BF16), _sds((8, D), F32), _sds((8, D), F32)],
        scratch_shapes=[pltpu.VMEM((STATE, GC), F32), pltpu.VMEM((Q, GC), F32)], name="scan_bwd",
        sem=("parallel", "arbitrary"),
        args=(pre, pre, pre, ex, cdx, acs, acst, dt, dtraw, bias, alog, states, y, proj, dmix, nw, dexp,
              collapse),
        jobs=jobs)


def _adam_math(gv, w, m, v):
    c1 = 1.0 - ADAM_B1 ** ADAM_STEP
    c2 = 1.0 - ADAM_B2 ** ADAM_STEP
    nm = ADAM_B1 * m + (1.0 - ADAM_B1) * gv
    nv = ADAM_B2 * v + (1.0 - ADAM_B2) * (gv * gv)
    return -ADAM_LR * ((nm / c1) / (jnp.sqrt(nv / c2) + ADAM_EPS) + ADAM_WD * w), nm, nv


def adamw(g, w, m, v, name):
    R, C = g.shape
    T = R if R <= 128 else 128
    assert R % T == 0

    def body(g_ref, w_ref, m_ref, v_ref, d_ref, nm_ref, nv_ref):
        d_ref[...], nm_ref[...], nv_ref[...] = _adam_math(g_ref[...], w_ref[...], m_ref[...], v_ref[...])

    blk = pl.BlockSpec((T, C), lambda i: (i, 0))
    return pl.pallas_call(
        body, grid=(R // T,), in_specs=[blk] * 4, out_specs=[blk] * 3,
        out_shape=[_sds((R, C), F32)] * 3, name=name, compiler_params=_params("parallel"))(g, w, m, v)


def adamw_layer(g, w, m, v, layer, prev, name):
    R, C = g.shape
    T = _rows_tile(R, ROW_TILE)
    assert R % T == 0

    def body(g_ref, w_ref, m_ref, v_ref, *rest):
        go_ref, d_ref, nm_ref, nv_ref = rest[-4:]
        gv = g_ref[...]
        go_ref[0] = gv
        d_ref[0], nm_ref[0], nv_ref[0] = _adam_math(gv, w_ref[0], m_ref[0], v_ref[0])

    mine = pl.BlockSpec((1, T, C), lambda i: (layer, i, 0))
    prev = list(prev or [])
    return pl.pallas_call(
        body, grid=(R // T,), in_specs=[pl.BlockSpec((T, C), lambda i: (i, 0)), mine, mine, mine] + [_ANY] * len(prev),
        out_specs=[mine] * 4, out_shape=[_sds(w.shape, F32)] * 4,
        input_output_aliases={4 + i: i for i in range(len(prev))}, name=name,
        compiler_params=_params("parallel"))(g, w, m, v, *prev)


def cast_bf16(a, layer):
    _, R, C = a.shape
    T = _rows_tile(R, ROW_TILE)

    def body(a_ref, o_ref):
        o_ref[...] = a_ref[0].astype(BF16)

    return pl.pallas_call(
        body, grid=(R // T,), in_specs=[pl.BlockSpec((1, T, C), lambda i: (layer, i, 0))],
        out_specs=pl.BlockSpec((T, C), lambda i: (i, 0)), out_shape=_sds((R, C), BF16),
        name="cast_bf16", compiler_params=_params("parallel"))(a)


def _to_groups(v, hpg):
    lead = v.shape[:-1]
    t = v.reshape(lead + (GROUPS, hpg))
    t = jnp.pad(t, [(0, 0)] * (len(lead) + 1) + [(0, LANES - hpg)])
    return t.reshape(lead + (GROUPS * LANES,))


def _from_groups(a, hpg):
    lead = a.shape[:-1]
    return a.reshape(lead + (GROUPS, LANES))[..., :hpg].reshape(lead + (GROUPS * hpg,))


def _expand_matrix(D):
    gc = D // GROUPS
    return (jnp.arange(LANES)[:, None] == (jnp.arange(gc)[None, :] // HEAD_DIM)).astype(BF16)


def layer_params(pre_w, w_in_t, mixw, scale, cw, cb, bias, alog, dskip, nw, w_out_full, post_w, D):
    hpg = D // GROUPS // HEAD_DIM
    main = w_in_t.shape[0] - GROUPS * hpg
    wdt_t = _to_groups(w_in_t[main:].T, hpg).T
    return dict(
        pre_w=pre_w[None], win_t=w_in_t, main=main, wdt_t=wdt_t, mixw=mixw,
        scale=scale[None], cw=cw, cb=cb[None], bias=_to_groups(bias, hpg)[None], alog=_to_groups(alog, hpg)[None],
        dexp=jnp.repeat(dskip, HEAD_DIM)[None], nw=nw[None], wout=w_out_full, post_w=post_w[None])


def layer_fwd(x, p, D, next_shards=None, h=None, next_pre_w=None, own_rest=None):
    nxt = bool(next_shards)
    parts = lambda a: ((a.shape[0] // 32) * 16, a.shape[0] - (a.shape[0] // 32) * 16)
    if nxt:
        s_in, s_out, s_mix, s_cw = next_shards
        (in_a, in_b), (out_a, out_b) = parts(s_in), parts(s_out)
    if h is None:
        h = rms_fwd(x, p["pre_w"])
    own = own_rest is not None
    proj, got = matmul(h, p["win_t"], "nt", F32, "proj", tm=1024, n_out=p["main"],
                       jobs=([gather_rows_job(s_in, 0, in_a)] if nxt else [])
                       + ([gather_send_job(own_rest[0], own_rest[1])] if own else []))
    g_in = got[0][0] if nxt else None
    dtraw, got = matmul(h, p["wdt_t"], "nt", F32, "dtproj",
                        jobs=[gather_pass_job(got[-1], own_rest[1])] if own else [])
    if own:
        p.update(own_rest[2](got[0]))
    mixed, pooled = pool_fwd(proj, p["mixw"], p["scale"], D)
    pre = conv_fwd(proj, p["cw"], p["cb"], D)
    dtp, got = dt_prep(dtraw, p["bias"], p["alog"], _expand_matrix(D), D,
                       jobs=[gather_rows_job(s_out, 0, out_a)] if nxt else [])
    g_out = got[0][0] if nxt else None
    dt, acs, acst, ex, cdx = dtp
    (y, states, mixed), got = scan_fwd(pre, ex, cdx, acs, acst, proj, p["dexp"], p["nw"], mixed, D,
                                       jobs=[gather_rows_job(s_in, in_a, in_b, g_in)] if nxt else [])
    g_in = got[0][0] if nxt else None
    out, got = matmul(mixed, p["wout"], "nn", F32, "outproj", tm=1024,
                      jobs=[gather_rows_job(s_out, out_a, out_b, g_out), gather_send_job([s_mix, s_cw]),
                            gather_pass_job([g_in], [True])] if nxt else [])
    (xn, h_next), got2 = post_fwd(x, out, p["post_w"], next_pre_w,
                                  jobs=[gather_pass_job(got[0] + got[1], [True, False, False])] if nxt else [])
    gathered = got[2] + got2[0] if nxt else None
    return xn, dict(x=x, h=h, proj=proj, dtraw=dtraw, pooled=pooled, pre=pre, dtp=dtp, y=y, states=states,
                    mixed=mixed, out=out), gathered, h_next


def layer_bwd(dxn, p, s, D, where=None, post=None, below=None, pending=None):
    reduce = where is not None
    chip, core = where if reduce else (None, None)
    hpg = D // GROUPS // HEAD_DIM
    PGW = D // GROUPS
    main = p["main"]
    SH = (main + GROUPS * hpg) // 4
    dt, acs, acst, ex, cdx = s["dtp"]
    dout, d_post = post if post else post_bwd(s["out"], p["post_w"], dxn)
    part = BF16 if reduce else F32
    d_wout, got = matmul(s["mixed"], dout, "tn", part, "dwout", tm=1024,
                         jobs=[pair_gather_job(pending, [True, False])] if pending else [])
    above = dict(w_in=got[0][0], pool_mix_w=got[0][1]) if pending else None
    g_out = d_wout.reshape(4, 2, D // 4, D)
    dmix, got = matmul(dout, p["wout"], "nt", F32, "dmixed", tm=1024,
                       jobs=[pair_exchange_job([g_out])] if reduce else [])
    pair_out = pair_add(g_out, got[0][0], core, BF16) if reduce else None
    (dxs, db, dc, ddtraw, d_alog, d_bias, dproj, d_nw, d_dexp), got = scan_bwd(
        s["pre"], ex, cdx, acs, acst, dt, s["dtraw"], p["bias"], p["alog"], s["states"], s["y"],
        s["proj"], dmix, p["nw"], p["dexp"], _expand_matrix(D).T, D,
        jobs=[chip_exchange_job([pair_out])] if reduce else [])
    mine_out = chip_add(pair_out, got[0][0], chip, core) if reduce else None
    (dproj, d_cw, d_cb), got = conv_bwd([dxs, db, dc], s["proj"], p["cw"], dproj, D,
                                        jobs=[pair_gather_job([mine_out])] if reduce else [])
    r_out = got[0][0] if reduce else None
    dproj, d_mixw, d_scale = pool_bwd(s["proj"], dmix, s["pooled"], p["mixw"], p["scale"], dproj, D)
    d_wmain_t, _ = matmul(dproj, s["h"], "tn", part, "dwmain", tm=1024)
    d_wdt_t, _ = matmul(ddtraw, s["h"], "tn", part, "dwdt")
    late = [d_wmain_t[None], d_wdt_t[None],
            d_mixw.reshape(GROUPS, 4, PGW // 4, PGW).transpose(1, 0, 2, 3).reshape(4, 2, GROUPS * PGW // 8, PGW)]
    cols = [True, True, False]
    dh_b, got = matmul(ddtraw, p["wdt_t"], "nn", F32, "dh_dt",
                       jobs=[pair_exchange_job(late, cols)] if reduce else [])
    if reduce:
        p_main, p_dt, p_mix = [pair_add(g, r, core, BF16, k, n) for g, r, k, n in
                               zip(late, got[0], cols, [4 * SH, None, None])]
        p_in = lax.dynamic_update_slice(p_main[0], _from_groups(p_dt[0].T, hpg).T, (main, 0))
        pairs = [p_in.reshape(4, SH, D // 2), p_mix]
    cut = (SH * 5 // 128) * 16
    dh_a, got = matmul(dproj, p["win_t"], "nn", F32, "dh_main",
                       jobs=[chip_exchange_rows_job(pairs[0], 0, cut), chip_exchange_job(pairs[1:])] if reduce else [])
    (dx, d_pre, *post_below), got2 = rms_bwd(
        s["x"], p["pre_w"], dh_a, dh_b, dxn, below,
        jobs=[chip_exchange_rows_job(pairs[0], cut, SH - cut, got[0][0])] if reduce else [])
    mines = [chip_add(pairs[0], got2[0][0], chip, core, True),
             chip_add(pairs[1], got[1][0], chip, core, False)] if reduce else None
    reduced = dict(w_out=r_out) if reduce else None
    grads = dict(
        pre_norm_w=d_pre[0], pool_scale=d_scale[0], conv_w=d_cw[:CONV_K], conv_b=d_cb[0],
        dt_bias=_from_groups(d_bias[0], hpg), a_log=_from_groups(d_alog[0], hpg),
        d_skip=d_dexp[0].reshape(-1, HEAD_DIM).sum(axis=-1), ssd_norm_w=d_nw[0], post_norm_w=d_post[0])
    if not reduce:
        grads.update(w_in=jnp.concatenate([d_wmain_t.T, _from_groups(d_wdt_t.T, hpg)], axis=1), pool_mix_w=d_mixw,
                     w_out=d_wout)
    return dx, grads, reduced, tuple(post_below) or None, mines, above


def local_step(x, target, params, D):
    saved, h, n = [], None, len(params)
    for l, p in enumerate(params):
        x, s, _, h = layer_fwd(x, p, D, h=h, next_pre_w=params[l + 1]["pre_w"] if l + 1 < n else None)
        saved.append(s)
    dx, sumsq = loss_head(x, target)
    grads, post = [None] * n, None
    for l in reversed(range(n)):
        below = (saved[l - 1]["out"], params[l - 1]["post_w"]) if l else None
        dx, grads[l], _, post, _, _ = layer_bwd(dx, params[l], saved[l], D, post=post, below=below)
    return sumsq, dx, grads


SMALL = ("pre_norm_w", "pool_scale", "conv_w", "conv_b", "dt_bias", "a_log", "d_skip", "ssd_norm_w", "post_norm_w")
BIG = ("w_in", "w_out", "pool_mix_w")


def _pack(parts):
    flat = jnp.concatenate([p.reshape(-1) for p in parts])
    n = flat.shape[0]
    rows = -(-n // (LANES * LANES)) * LANES
    return jnp.pad(flat, (0, rows * LANES - n)).reshape(rows, LANES)


def _unpack(packed, shapes):
    flat, out, at = packed.reshape(-1), [], 0
    for s in shapes:
        n = math.prod(s)
        out.append(flat[at:at + n].reshape(s))
        at += n
    return out


def kernel(x, pre_norm_w, w_in, pool_mix_w, pool_scale, conv_w, conv_b, dt_bias, a_log, d_skip, ssd_norm_w, w_out, post_norm_w, loss_target, m_pre_norm_w, m_w_in, m_pool_mix_w, m_pool_scale, m_conv_w, m_conv_b, m_dt_bias, m_a_log, m_d_skip, m_ssd_norm_w, m_w_out, m_post_norm_w, v_pre_norm_w, v_w_in, v_pool_mix_w, v_pool_scale, v_conv_w, v_conv_b, v_dt_bias, v_a_log, v_d_skip, v_ssd_norm_w, v_w_out, v_post_norm_w):
    NL, D, SH = w_in.shape
    PGW = D // GROUPS
    CS = conv_w.shape[2]
    chip = (2 * lax.axis_index("x") + lax.axis_index("y")).astype(jnp.int32)
    chip1, core = chip.reshape(1), lax.axis_index("c").astype(jnp.int32).reshape(1)

    tr = lambda t: jnp.transpose(t, (0, 2, 1))
    w_in_t, m_w_in_t, v_w_in_t = tr(w_in), tr(m_w_in), tr(v_w_in)
    halved_by_cols = [True, True, False, False]

    def shards(l):
        return [cast_bf16(w_in_t, l), cast_bf16(w_out, l),
                cast_bf16(pool_mix_w.reshape(NL, GROUPS * PGW // 4, PGW), l).reshape(2, GROUPS * PGW // 8, PGW),
                conv_w[l].reshape(2, CONV_K * CS // (2 * LANES), LANES)]

    def rest_params(g):
        g_out, g_mix, g_cw = g
        return dict(wout=g_out.reshape(2 * D, D),
                    mixw=g_mix.reshape(4, GROUPS, PGW // 4, PGW).transpose(1, 0, 2, 3).reshape(GROUPS, PGW, PGW),
                    cw=g_cw.reshape(4, CONV_K, CS).transpose(1, 0, 2).reshape(CONV_K, 4 * CS))

    def params(l, g):
        late = rest_params(g[1:]) if len(g) > 1 else dict(wout=None, mixw=None, cw=None)
        return layer_params(pre_norm_w[l], g[0].reshape(4 * SH, D), late["mixw"], pool_scale[l], late["cw"], conv_b[l],
                            dt_bias[l], a_log[l], d_skip[l], ssd_norm_w[l], late["wout"], post_norm_w[l], D)

    first = shards(0)
    gathered = run_jobs([gather_send_job(first[:1], [True])], "gather_send")[0]
    gathered = run_jobs([gather_pass_job(gathered, [True])], "gather_pass")[0]
    xl, h, ps, saved = x[0], None, [], []
    for l in range(NL):
        ps.append(params(l, gathered))
        last = l + 1 == NL
        xl, s, gathered, h = layer_fwd(xl, ps[l], D, None if last else shards(l + 1), h,
                                       None if last else pre_norm_w[l + 1][None],
                                       (first[1:], halved_by_cols[1:], rest_params) if l == 0 else None)
        saved.append(s)
    dx, sumsq = loss_head(xl, loss_target[0])

    given = dict(w_in=(w_in_t, m_w_in_t, v_w_in_t), w_out=(w_out, m_w_out, v_w_out),
                 pool_mix_w=(pool_mix_w, m_pool_mix_w, v_pool_mix_w))
    flat = {n: [t.reshape(NL, -1, t.shape[-1]) for t in given[n]] for n in BIG}
    done = {n: None for n in BIG}
    def update(l, reduced):
        for n, r in reduced.items():
            done[n] = adamw_layer(r.reshape(-1, r.shape[-1]), *flat[n], l, done[n], "adamw_" + n)

    grads, post, pending = [None] * NL, None, None
    for l in reversed(range(NL)):
        below = (saved[l - 1]["out"], ps[l - 1]["post_w"]) if l else None
        dx, grads[l], reduced, post, pending, above = layer_bwd(dx, ps[l], saved[l], D, (chip1, core), post, below,
                                                                pending)
        update(l, reduced)
        if above:
            update(l + 1, above)
    last = run_jobs([pair_gather_job(pending, [True, False])], "pair_gather")[0]
    update(0, dict(w_in=last[0], pool_mix_w=last[1]))

    small_shapes = [(NL,) + grads[0][n].shape for n in SMALL]
    packed = _pack([0.5 / D * sumsq[0, :1]] + [jnp.stack([g[n] for g in grads]) for n in SMALL])
    total = allreduce_small(packed)
    loss, *small = _unpack(total, [(1,)] + small_shapes)
    small = dict(zip(SMALL, small))
    small["conv_w"] = lax.dynamic_slice_in_dim(small["conv_w"], chip * CS, CS, axis=2)

    given_small = dict(
        pre_norm_w=(pre_norm_w, m_pre_norm_w, v_pre_norm_w), pool_scale=(pool_scale, m_pool_scale, v_pool_scale),
        conv_w=(conv_w, m_conv_w, v_conv_w), conv_b=(conv_b, m_conv_b, v_conv_b),
        dt_bias=(dt_bias, m_dt_bias, v_dt_bias), a_log=(a_log, m_a_log, v_a_log),
        d_skip=(d_skip, m_d_skip, v_d_skip), ssd_norm_w=(ssd_norm_w, m_ssd_norm_w, v_ssd_norm_w),
        post_norm_w=(post_norm_w, m_post_norm_w, v_post_norm_w))
    shapes = [given_small[n][0].shape for n in SMALL]
    upd = adamw(_pack([small[n] for n in SMALL]), *[_pack([given_small[n][i] for n in SMALL]) for i in range(3)],
                "adamw_small")
    upd = [dict(zip(SMALL, _unpack(u, shapes))) for u in upd]

    out = {n: (small[n], upd[0][n], upd[1][n], upd[2][n]) for n in SMALL}
    for n in BIG:
        out[n] = tuple(t.reshape(given[n][0].shape) for t in done[n])
    out["w_in"] = tuple(tr(t) for t in out["w_in"])

    order = ("pre_norm_w", "w_in", "pool_mix_w", "pool_scale", "conv_w", "conv_b", "dt_bias", "a_log", "d_skip",
             "ssd_norm_w", "w_out", "post_norm_w")
    return (loss.reshape(()), dx[None], *[out[n][0] for n in order], *[out[n][1] for n in order],
            *[out[n][2] for n in order], *[out[n][3] for n in order])
```

```python
import functools
import math

import jax
import jax.numpy as jnp
from jax import lax
from jax.experimental import pallas as pl
from jax.experimental.pallas import tpu as pltpu

F32 = jnp.float32
BF16 = jnp.bfloat16

NORM_EPS = 1e-6
HEAD_DIM = 64
STATE = 128
GROUPS = 4
POOL_WINDOWS = (2, 4, 8, 16)
POOL_HALO = 16
CONV_K = 4
CONV_HALO = 8
SCAN_CHUNK = 256
LANES = 128
VMEM_LIMIT = 52 * 1024 * 1024
ROW_TILE = 256

ADAM_LR = 0.001
ADAM_B1 = 0.9
ADAM_B2 = 0.999
ADAM_EPS = 1e-08
ADAM_WD = 0.01
ADAM_STEP = 10

MESH = pl.DeviceIdType.MESH

NN = (((1,), (0,)), ((), ()))
NT = (((1,), (1,)), ((), ()))
TN = (((0,), (0,)), ((), ()))

_ANY = pl.BlockSpec(memory_space=pl.ANY)


def _params(*sem):
    return pltpu.CompilerParams(dimension_semantics=sem, vmem_limit_bytes=VMEM_LIMIT)


def _pick(dim, pref):
    if dim <= pref:
        return dim
    t = (pref // LANES) * LANES
    while t > LANES and dim % t:
        t -= LANES
    assert dim % t == 0, (dim, pref)
    return t


def _rows_tile(rows, pref):
    t = (min(pref, rows) // 8) * 8
    while t >= 8 and rows % t:
        t -= 8
    return t if t >= 8 else rows


def _dot(a, b, dn=NN):
    return lax.dot_general(a, b, dn, preferred_element_type=F32)


def _split3(a):
    hi = a.astype(BF16)
    r = a - hi.astype(F32)
    mid = r.astype(BF16)
    return hi, mid, (r - mid.astype(F32)).astype(BF16)


def _dot_sel(a, e, parts=3):
    hi, mid, lo = _split3(a)
    return (_dot(lo, e) + _dot(mid, e)) + _dot(hi, e) if parts == 3 else _dot(mid, e) + _dot(hi, e)


def _sel_dot(e, b):
    hi, mid, lo = _split3(b)
    return (_dot(e, lo) + _dot(e, mid)) + _dot(e, hi)


def _sigmoid(v):
    return 0.5 * jnp.tanh(0.5 * v) + 0.5


def _silu_and_grad(v):
    s = _sigmoid(v)
    return v * s, s * (1.0 + v * (1.0 - s))


def _row(i, shape):
    return lax.broadcasted_iota(jnp.int32, shape, 0) + i


def _sds(shape, dtype):
    return jax.ShapeDtypeStruct(tuple(shape), dtype)


class Job:
    def __init__(self, ins, outs, aliased, nsem, start, finish):
        self.ins, self.outs, self.aliased, self.nsem, self.start, self.finish = ins, outs, aliased, nsem, start, finish


def _place():
    x, y, c = lax.axis_index("x"), lax.axis_index("y"), lax.axis_index("c")
    return x, y, c, [(1 - x, y), (x, 1 - y), (1 - x, 1 - y)]


def _remote(src, dst, send_sem, recv_sem, device):
    return pltpu.make_async_remote_copy(src_ref=src, dst_ref=dst, send_sem=send_sem, recv_sem=recv_sem,
                                        device_id=device, device_id_type=MESH)


def _half(ref, c, cols, lead=0):
    idx = [slice(None)] * lead
    if cols:
        w = ref.shape[-1] // 2
        idx += [slice(None)] * (len(ref.shape) - lead - 1) + [pl.ds(pl.multiple_of(c * w, LANES), w)]
    else:
        idx += [c]
    return ref.at[tuple(idx)]


def _flags(cols, n):
    return list(cols) if cols else [False] * n


def gather_send_job(arrs, cols=None):
    n = len(arrs)
    cols = _flags(cols, n)

    def copies(ins, outs, send, recv):
        x, y, c, chips = _place()
        mine = 2 * x + y
        out = []
        for a in range(n):
            out.append(_remote(ins[a], outs[a].at[mine], send.at[4 * a + 3], recv.at[4 * a + 3], (x, y, 1 - c)))
            for j, chip in enumerate(chips):
                out.append(_remote(_half(ins[a], c, cols[a]), _half(outs[a].at[mine], c, cols[a]),
                                   send.at[4 * a + j], recv.at[4 * a + j], (*chip, c)))
        return out

    def start(ins, outs, send, recv):
        for cp in copies(ins, outs, send, recv):
            cp.start()

    def finish(ins, outs, send, recv):
        x, y, c, chips = _place()
        for a in range(n):
            for j, chip in enumerate(chips):
                landed = _half(outs[a].at[2 * chip[0] + chip[1]], c, cols[a])
                _remote(landed, landed, send.at[4 * a + j], recv.at[4 * a + j], (x, y, 1 - c)).wait_recv()
            twin = outs[a].at[2 * x + y]
            _remote(twin, twin, send.at[4 * a + 3], recv.at[4 * a + 3], (x, y, 1 - c)).wait_recv()
        for cp in copies(ins, outs, send, recv):
            cp.wait_send()

    return Job(list(arrs), [_sds((4,) + a.shape, a.dtype) for a in arrs], False, 4 * n, start, finish)


def gather_rows_job(shard, start, size, into=None):
    rows = pl.ds(start, size)

    def copies(ins, outs, send, recv):
        x, y, c, chips = _place()
        mine = outs[0].at[2 * x + y, rows]
        out = [_remote(ins[0].at[rows], mine, send.at[3], recv.at[3], (x, y, 1 - c))]
        for j, chip in enumerate(chips):
            out.append(_remote(_half(ins[0].at[rows], c, True), _half(mine, c, True), send.at[j], recv.at[j],
                               (*chip, c)))
        return out

    def start_(ins, outs, send, recv):
        for cp in copies(ins, outs, send, recv):
            cp.start()

    def finish(ins, outs, send, recv):
        x, y, c, chips = _place()
        for j, chip in enumerate(chips):
            landed = _half(outs[0].at[2 * chip[0] + chip[1], rows], c, True)
            _remote(landed, landed, send.at[j], recv.at[j], (x, y, 1 - c)).wait_recv()
        twin = outs[0].at[2 * x + y, rows]
        _remote(twin, twin, send.at[3], recv.at[3], (x, y, 1 - c)).wait_recv()
        for cp in copies(ins, outs, send, recv):
            cp.wait_send()

    ins = [shard] if into is None else [shard, into]
    return Job(ins, [_sds((4,) + shard.shape, shard.dtype)], {} if into is None else {1: 0}, 4, start_, finish)


def gather_pass_job(bufs, cols=None):
    n = len(bufs)
    cols = _flags(cols, n)

    def copies(outs, send, recv):
        x, y, c, chips = _place()
        out = []
        for a in range(n):
            for j, chip in enumerate(chips):
                landed = _half(outs[a].at[2 * chip[0] + chip[1]], c, cols[a])
                out.append(_remote(landed, landed, send.at[3 * a + j], recv.at[3 * a + j], (x, y, 1 - c)))
        return out

    def start(ins, outs, send, recv):
        for cp in copies(outs, send, recv):
            cp.start()

    def finish(ins, outs, send, recv):
        x, y, c, chips = _place()
        for a in range(n):
            for j, chip in enumerate(chips):
                passed = _half(outs[a].at[2 * chip[0] + chip[1]], 1 - c, cols[a])
                _remote(passed, passed, send.at[3 * a + j], recv.at[3 * a + j], (x, y, 1 - c)).wait_recv()
        for cp in copies(outs, send, recv):
            cp.wait_send()

    return Job(list(bufs), [_sds(b.shape, b.dtype) for b in bufs], True, 3 * n, start, finish)


def pair_exchange_job(arrs, cols=None):
    n = len(arrs)
    cols = _flags(cols, n)

    def copies(ins, outs, send, recv):
        x, y, c, _ = _place()
        return [_remote(_half(ins[a], 1 - c, cols[a], 1), outs[a], send.at[a], recv.at[a], (x, y, 1 - c))
                for a in range(n)]

    def start(ins, outs, send, recv):
        for cp in copies(ins, outs, send, recv):
            cp.start()

    def finish(ins, outs, send, recv):
        for cp in copies(ins, outs, send, recv):
            cp.wait()

    shape = lambda a, k: a.shape[:-1] + (a.shape[-1] // 2,) if k else a.shape[:1] + a.shape[2:]
    return Job(list(arrs), [_sds(shape(a, k), a.dtype) for a, k in zip(arrs, cols)], False, n, start, finish)


def chip_exchange_job(arrs):
    n = len(arrs)

    def copies(ins, outs, send, recv):
        x, y, c, chips = _place()
        return [_remote(ins[a].at[2 * chip[0] + chip[1]], outs[a].at[j], send.at[3 * a + j], recv.at[3 * a + j],
                        (*chip, c)) for a in range(n) for j, chip in enumerate(chips)]

    def start(ins, outs, send, recv):
        for cp in copies(ins, outs, send, recv):
            cp.start()

    def finish(ins, outs, send, recv):
        for cp in copies(ins, outs, send, recv):
            cp.wait()

    return Job(list(arrs), [_sds((3,) + a.shape[1:], a.dtype) for a in arrs], False, 3 * n, start, finish)


def chip_exchange_rows_job(p, start, size, into=None):
    rows = pl.ds(start, size)

    def copies(ins, outs, send, recv):
        x, y, c, chips = _place()
        return [_remote(ins[0].at[2 * chip[0] + chip[1], rows], outs[0].at[j, rows], send.at[j], recv.at[j], (*chip, c))
                for j, chip in enumerate(chips)]

    def start_(ins, outs, send, recv):
        for cp in copies(ins, outs, send, recv):
            cp.start()

    def finish(ins, outs, send, recv):
        for cp in copies(ins, outs, send, recv):
            cp.wait()

    ins = [p] if into is None else [p, into]
    return Job(ins, [_sds((3,) + p.shape[1:], p.dtype)], {} if into is None else {1: 0}, 3, start_, finish)


def pair_gather_job(bufs, cols=None):
    n = len(bufs)
    cols = _flags(cols, n)

    def copies(outs, send, recv):
        x, y, c, _ = _place()
        return [_remote(_half(outs[a], c, cols[a]), _half(outs[a], c, cols[a]), send.at[a], recv.at[a],
                        (x, y, 1 - c)) for a in range(n)]

    def start(ins, outs, send, recv):
        for cp in copies(outs, send, recv):
            cp.start()

    def finish(ins, outs, send, recv):
        for cp in copies(outs, send, recv):
            cp.wait()

    return Job(list(bufs), [_sds(b.shape, b.dtype) for b in bufs], True, n, start, finish)


def _call(body, *, grid, in_specs, out_specs, out_shape, name, sem, args, scratch_shapes=(), jobs=(), aliases=None):
    in_specs, out_specs, out_shape, scratch_shapes = list(in_specs), list(out_specs), list(out_shape), list(scratch_shapes)
    aliases = dict(aliases or {})
    n_in, n_out, n_scr = len(in_specs), len(out_specs), len(scratch_shapes)
    if jobs:
        sem = ("arbitrary",) * len(grid)
    at_in, at_out = n_in, n_out
    for j in jobs:
        pairs = j.aliased if isinstance(j.aliased, dict) else {i: i for i in range(len(j.ins))} if j.aliased else {}
        aliases.update({at_in + i: at_out + o for i, o in pairs.items()})
        at_in, at_out = at_in + len(j.ins), at_out + len(j.outs)

    def wrapped(*refs):
        ins, p = refs[:n_in], n_in
        jins = []
        for j in jobs:
            jins.append(refs[p:p + len(j.ins)])
            p += len(j.ins)
        outs, p = refs[p:p + n_out], p + n_out
        jouts = []
        for j in jobs:
            jouts.append(refs[p:p + len(j.outs)])
            p += len(j.outs)
        scr, sems = refs[p:p + n_scr], refs[p + n_scr:]

        def start():
            for k, j in enumerate(jobs):
                j.start(jins[k], jouts[k], sems[2 * k], sems[2 * k + 1])

        def finish():
            for k, j in enumerate(jobs):
                j.finish(jins[k], jouts[k], sems[2 * k], sems[2 * k + 1])

        if jobs and grid:
            ids = [pl.program_id(d) for d in range(len(grid))]
            pl.when(functools.reduce(jnp.logical_and, [i == 0 for i in ids]))(start)
            body(*ins, *outs, *scr)
            pl.when(functools.reduce(jnp.logical_and, [i == g - 1 for i, g in zip(ids, grid)]))(finish)
        else:
            start()
            body(*ins, *outs, *scr)
            finish()

    kwargs = dict(grid=grid) if grid else {}
    res = pl.pallas_call(
        wrapped, in_specs=in_specs + [_ANY] * (at_in - n_in), out_specs=out_specs + [_ANY] * (at_out - n_out),
        out_shape=out_shape + [o for j in jobs for o in j.outs],
        scratch_shapes=scratch_shapes + [pltpu.SemaphoreType.DMA((j.nsem,)) for j in jobs for _ in range(2)],
        input_output_aliases=aliases, name=name,
        compiler_params=pltpu.CompilerParams(dimension_semantics=sem, vmem_limit_bytes=VMEM_LIMIT) if grid
        else pltpu.CompilerParams(vmem_limit_bytes=VMEM_LIMIT), **kwargs)(*args, *[a for j in jobs for a in j.ins])
    res = list(res)
    outs, rest, per_job = res[:n_out], res[n_out:], []
    for j in jobs:
        per_job.append(rest[:len(j.outs)])
        rest = rest[len(j.outs):]
    return outs, per_job


def run_jobs(jobs, name):
    return _call(lambda: None, grid=(), in_specs=[], out_specs=[], out_shape=[], name=name, sem=(), args=(), jobs=jobs)[1]


def pair_add(g, r, core, out_dtype, cols=False, out_rows=None):
    S, R, C = r.shape
    T = _rows_tile(R, 2 * ROW_TILE if C <= 1024 else ROW_TILE if C <= 4096 else ROW_TILE // 2)

    def body(c_ref, g_ref, r_ref, o_ref):
        o_ref[0] = ((g_ref[0] if cols else g_ref[0, 0]).astype(F32) + r_ref[0].astype(F32)).astype(o_ref.dtype)

    g_spec = (pl.BlockSpec((1, T, C), lambda k, i, c_ref: (k, i, c_ref[0])) if cols
              else pl.BlockSpec((1, 1, T, C), lambda k, i, c_ref: (k, c_ref[0], i, 0)))
    return pl.pallas_call(
        body,
        grid_spec=pltpu.PrefetchScalarGridSpec(
            num_scalar_prefetch=1, grid=(S, R // T),
            in_specs=[g_spec, pl.BlockSpec((1, T, C), lambda k, i, c_ref: (k, i, 0))],
            out_specs=pl.BlockSpec((1, T, C), lambda k, i, c_ref: (k, i, 0))),
        out_shape=_sds((S, out_rows or R, C), out_dtype), name="pair_add",
        compiler_params=_params("parallel", "parallel"))(core, g, r)


def chip_add(p, r, chip, core, cols=False):
    _, R, C = p.shape
    T = _rows_tile(R, ROW_TILE)

    def body(k_ref, c_ref, p_ref, r0_ref, r1_ref, r2_ref, o_ref):
        s = ((p_ref[0].astype(F32) + r0_ref[0].astype(F32)) + r1_ref[0].astype(F32)) + r2_ref[0].astype(F32)
        if cols:
            o_ref[...] = s
        else:
            o_ref[0] = s

    slot = lambda j: pl.BlockSpec((1, T, C), lambda i, k_ref, c_ref: (j, i, 0))
    out_spec = (pl.BlockSpec((T, C), lambda i, k_ref, c_ref: (i, c_ref[0])) if cols
                else pl.BlockSpec((1, T, C), lambda i, k_ref, c_ref: (c_ref[0], i, 0)))
    return pl.pallas_call(
        body,
        grid_spec=pltpu.PrefetchScalarGridSpec(
            num_scalar_prefetch=2, grid=(R // T,),
            in_specs=[pl.BlockSpec((1, T, C), lambda i, k_ref, c_ref: (k_ref[0], i, 0)), slot(0), slot(1), slot(2)],
            out_specs=out_spec),
        out_shape=_sds((R, 2 * C) if cols else (2, R, C), F32), name="chip_add",
        compiler_params=_params("parallel"))(chip, core, p, r, r, r)


def allreduce_small(v):
    R = v.shape[0]

    def body(v_ref, o_ref, buf, send_sems, recv_sems, local_sem):
        x, y, c, chips = _place()
        me, sibling = (x, y, c), (x, y, 1 - c)

        def rows(px, py, pc):
            return buf.at[pl.ds((4 * px + 2 * py + pc) * R, R), :]

        def copy(k, block, to, src=None):
            return _remote(rows(*block) if src is None else src, rows(*block), send_sems.at[k], recv_sems.at[k], to)

        mine = pltpu.make_async_copy(v_ref, rows(*me), local_sem)
        mine.start()
        first = [copy(0, me, sibling, src=v_ref)]
        first += [copy(1 + j, me, (*chip, c), src=v_ref) for j, chip in enumerate(chips)]
        for cp in first:
            cp.start()
        passed = [copy(4 + j, (*chip, c), sibling) for j, chip in enumerate(chips)]
        for j, chip in enumerate(chips):
            copy(1 + j, (*chip, c), me).wait_recv()
            passed[j].start()
        copy(0, sibling, me).wait_recv()
        for j, chip in enumerate(chips):
            copy(4 + j, (*chip, 1 - c), me).wait_recv()
        for cp in first + passed:
            cp.wait_send()
        mine.wait()
        acc = buf[0:R, :]
        for d in range(1, 8):
            acc = acc + buf[d * R:(d + 1) * R, :]
        o_ref[...] = acc

    return pl.pallas_call(
        body, in_specs=[pl.BlockSpec(memory_space=pltpu.VMEM)], out_specs=pl.BlockSpec(memory_space=pltpu.VMEM),
        out_shape=_sds((R, LANES), F32),
        scratch_shapes=[pltpu.VMEM((8 * R, LANES), F32), pltpu.SemaphoreType.DMA((7,)),
                        pltpu.SemaphoreType.DMA((7,)), pltpu.SemaphoreType.DMA],
        name="allreduce_small", compiler_params=pltpu.CompilerParams(vmem_limit_bytes=VMEM_LIMIT))(v)


def matmul(a, b, mode, out_dtype, name, tm=512, tn=1024, tk=4608, jobs=(), n_out=None):
    if mode == "nn":
        (M, K), (K2, N) = a.shape, b.shape
    elif mode == "nt":
        (M, K), (N, K2) = a.shape, b.shape
        N = n_out or N
    else:
        (K, M), (K2, N) = a.shape, b.shape
    assert K == K2 or (mode == "nn" and K2 > K)
    tm, tn, tk = _pick(M, tm), _pick(N, tn), _pick(K, tk)
    nk = K // tk
    dn = {"nn": NN, "nt": NT, "tn": TN}[mode]

    def body(a_ref, b_ref, o_ref, *acc):
        part = _dot(a_ref[...].astype(BF16), b_ref[...].astype(BF16), dn)
        if nk == 1:
            o_ref[...] = part.astype(o_ref.dtype)
            return
        acc_ref, = acc
        k = pl.program_id(2)

        @pl.when(k == 0)
        def _():
            acc_ref[...] = part

        @pl.when(jnp.logical_and(k > 0, k < nk - 1))
        def _():
            acc_ref[...] += part

        @pl.when(k == nk - 1)
        def _():
            o_ref[...] = (acc_ref[...] + part).astype(o_ref.dtype)

    a_spec = (pl.BlockSpec((tk, tm), lambda i, j, k: (k, i)) if mode == "tn"
              else pl.BlockSpec((tm, tk), lambda i, j, k: (i, k)))
    b_spec = (pl.BlockSpec((tn, tk), lambda i, j, k: (j, k)) if mode == "nt"
              else pl.BlockSpec((tk, tn), lambda i, j, k: (k, j)))
    outs, per_job = _call(
        body, grid=(M // tm, N // tn, nk), in_specs=[a_spec, b_spec],
        out_specs=[pl.BlockSpec((tm, tn), lambda i, j, k: (i, j))], out_shape=[_sds((M, N), out_dtype)],
        scratch_shapes=[pltpu.VMEM((tm, tn), F32)] if nk > 1 else [], name=name,
        sem=("parallel", "parallel", "arbitrary"), args=(a, b), jobs=jobs)
    return outs[0], per_job


def rms_fwd(x, w):
    L, D = x.shape
    T = _pick(L, ROW_TILE)

    def body(x_ref, w_ref, h_ref):
        xv = x_ref[...]
        r = lax.rsqrt(jnp.mean(xv * xv, axis=-1, keepdims=True) + NORM_EPS)
        h_ref[...] = (xv * r * w_ref[...]).astype(h_ref.dtype)

    return pl.pallas_call(
        body, grid=(L // T,),
        in_specs=[pl.BlockSpec((T, D), lambda i: (i, 0)), pl.BlockSpec((1, D), lambda i: (0, 0))],
        out_specs=pl.BlockSpec((T, D), lambda i: (i, 0)),
        out_shape=_sds((L, D), BF16), name="rms_fwd", compiler_params=_params("parallel"))(x, w)


def post_fwd(x, o, w, next_w=None, jobs=()):
    L, D = x.shape
    T = _pick(L, ROW_TILE)

    def body(x_ref, o_ref, w_ref, *rest):
        ov = o_ref[...]
        r = lax.rsqrt(jnp.mean(ov * ov, axis=-1, keepdims=True) + NORM_EPS)
        y = x_ref[...] + ov * r * w_ref[...]
        rest[-1 if next_w is None else -2][...] = y
        if next_w is not None:
            r2 = lax.rsqrt(jnp.mean(y * y, axis=-1, keepdims=True) + NORM_EPS)
            rest[-1][...] = (y * r2 * rest[0][...]).astype(BF16)

    row = pl.BlockSpec((T, D), lambda i: (i, 0))
    vec = pl.BlockSpec((1, D), lambda i: (0, 0))
    more = [] if next_w is None else [next_w]
    outs, per_job = _call(
        body, grid=(L // T,), in_specs=[row, row, vec] + [vec] * len(more), out_specs=[row] * (1 + len(more)),
        out_shape=[_sds((L, D), F32)] + [_sds((L, D), BF16)] * len(more), name="post_fwd", sem=("parallel",),
        args=(x, o, w, *more), jobs=jobs)
    return (outs[0], outs[1] if more else None), per_job


def _rms_bwd_math(xv, w, dy):
    r = lax.rsqrt(jnp.mean(xv * xv, axis=-1, keepdims=True) + NORM_EPS)
    xhat = xv * r
    g = dy * w
    dx = r * (g - xhat * jnp.mean(g * xhat, axis=-1, keepdims=True))
    return dx, jnp.sum(dy * xhat, axis=0, keepdims=True)


def post_bwd(o, w, dxn):
    L, D = o.shape
    T = _pick(L, ROW_TILE)

    def body(o_ref, w_ref, d_ref, do_ref, dw_ref):
        dx, dw = _rms_bwd_math(o_ref[...], w_ref[...], d_ref[...])
        do_ref[...] = dx.astype(do_ref.dtype)

        @pl.when(pl.program_id(0) == 0)
        def _():
            dw_ref[...] = jnp.zeros_like(dw_ref)

        dw_ref[...] += dw

    row = pl.BlockSpec((T, D), lambda i: (i, 0))
    vec = pl.BlockSpec((1, D), lambda i: (0, 0))
    return pl.pallas_call(
        body, grid=(L // T,), in_specs=[row, vec, row], out_specs=[row, vec],
        out_shape=[_sds((L, D), BF16), _sds((1, D), F32)],
        name="post_bwd", compiler_params=_params("arbitrary"))(o, w, dxn)


def rms_bwd(x, w, dh_a, dh_b, dxn, below=None, jobs=()):
    L, D = x.shape
    T = _pick(L, ROW_TILE)

    def body(x_ref, w_ref, a_ref, b_ref, d_ref, *rest):
        dx, dw = _rms_bwd_math(x_ref[...], w_ref[...], a_ref[...] + b_ref[...])
        dx = d_ref[...] + dx
        outs = rest[2:] if below else rest
        outs[0][...] = dx

        @pl.when(pl.program_id(0) == 0)
        def _():
            for acc in outs[1::2]:
                acc[...] = jnp.zeros_like(acc)

        outs[1][...] += dw
        if below:
            do, dwp = _rms_bwd_math(rest[0][...], rest[1][...], dx)
            outs[2][...] = do.astype(BF16)
            outs[3][...] += dwp

    row = pl.BlockSpec((T, D), lambda i: (i, 0))
    vec = pl.BlockSpec((1, D), lambda i: (0, 0))
    more = list(below) if below else []
    return _call(
        body, grid=(L // T,), in_specs=[row, vec, row, row, row] + ([row, vec] if below else []),
        out_specs=[row, vec] + ([row, vec] if below else []),
        out_shape=[_sds((L, D), F32), _sds((1, D), F32)] + ([_sds((L, D), BF16), _sds((1, D), F32)] if below else []),
        name="rms_bwd", sem=("arbitrary",), args=(x, w, dh_a, dh_b, dxn, *more), jobs=jobs)


def loss_head(y, target):
    L, D = y.shape
    T = _pick(L, ROW_TILE)

    def body(y_ref, t_ref, d_ref, s_ref):
        e = y_ref[...] - t_ref[...]
        d_ref[...] = e * (1.0 / D)

        @pl.when(pl.program_id(0) == 0)
        def _():
            s_ref[...] = jnp.zeros_like(s_ref)

        s_ref[...] += jnp.sum(e * e)

    row = pl.BlockSpec((T, D), lambda i: (i, 0))
    return pl.pallas_call(
        body, grid=(L // T,), in_specs=[row, row],
        out_specs=[row, pl.BlockSpec((8, LANES), lambda i: (0, 0))],
        out_shape=[_sds((L, D), F32), _sds((8, LANES), F32)],
        name="loss_head", compiler_params=_params("arbitrary"))(y, target)


def _window_sums(xe, w, back):
    n = xe.shape[0]
    s, k = xe, 1
    while k < w:
        s = s + pltpu.roll(s, k if back else n - k, 0)
        k *= 2
    return s


def pool_fwd(proj, mixw, scale, D):
    L = proj.shape[0]
    PGW = D // GROUPS
    T = _pick(L, ROW_TILE)
    hb = T // POOL_HALO

    def body(u_ref, halo_ref, g_ref, mw_ref, sc_ref, y_ref, p_ref):
        i = pl.program_id(0)
        u = u_ref[...]
        halo = jnp.where(i > 0, halo_ref[...], 0.0)
        xe = jnp.concatenate([halo, u], axis=0)
        t1 = _row(i * T + 1, (T, 1))
        for g, w in enumerate(POOL_WINDOWS):
            sl = slice(g * PGW, (g + 1) * PGW)
            win = _window_sums(xe[:, sl], w, True)[POOL_HALO:, :]
            cnt = jnp.minimum(t1, w).astype(F32)
            pooled = (win / cnt - u[:, sl]).astype(BF16)
            p_ref[:, sl] = pooled
            mixed = _dot(pooled, mw_ref[g])
            gate = g_ref[:, sl]
            y_ref[:, sl] = (mixed * sc_ref[:, sl] * (gate * _sigmoid(gate))).astype(BF16)

    return pl.pallas_call(
        body, grid=(L // T,),
        in_specs=[pl.BlockSpec((T, D), lambda i: (i, 0)),
                  pl.BlockSpec((POOL_HALO, D), lambda i: (jnp.maximum(i * hb - 1, 0), 0)),
                  pl.BlockSpec((T, D), lambda i: (i, 1)),
                  pl.BlockSpec((GROUPS, PGW, PGW), lambda i: (0, 0, 0)),
                  pl.BlockSpec((1, D), lambda i: (0, 0))],
        out_specs=[pl.BlockSpec((T, D), lambda i: (i, 0)), pl.BlockSpec((T, D), lambda i: (i, 0))],
        out_shape=[_sds((L, 2 * D), BF16), _sds((L, D), BF16)],
        name="pool_fwd", compiler_params=_params("parallel"))(proj, proj, proj, mixw, scale)


def pool_bwd(proj, dmix, pooled, mixw, scale, dproj, D):
    L = proj.shape[0]
    PGW = D // GROUPS
    T = _pick(L, ROW_TILE)
    hb = T // POOL_HALO
    nT = L // T

    def body(g_ref, gh_ref, dy_ref, dyh_ref, p_ref, mw_ref, sc_ref, old_ref, dp_ref, dm_ref, ds_ref):
        i = pl.program_id(0)
        t1 = _row(i * T + 1, (T, 1))
        th1 = _row((i + 1) * T + 1, (POOL_HALO, 1))
        live = i < nT - 1

        @pl.when(i == 0)
        def _():
            ds_ref[...] = jnp.zeros_like(ds_ref)
            dm_ref[...] = jnp.zeros_like(dm_ref)

        for g, w in enumerate(POOL_WINDOWS):
            sl = slice(g * PGW, (g + 1) * PGW)
            sc = sc_ref[:, sl]
            gate, dy = g_ref[:, sl], dy_ref[:, sl]
            sg, dsg = _silu_and_grad(gate)
            pooled = p_ref[:, sl]
            mixed = _dot(pooled, mw_ref[g])
            dmixed = (dy * sc * sg).astype(BF16)
            dm_ref[g] += _dot(pooled, dmixed, TN)
            dp_ref[:, D + g * PGW:D + (g + 1) * PGW] = (dy * mixed * sc * dsg).astype(BF16)
            ds_ref[:, sl] += jnp.sum(dy * mixed * sg, axis=0, keepdims=True)
            dpool = _dot(dmixed, mw_ref[g], NT)
            gate_h = gh_ref[:, sl]
            dmixed_h = (dyh_ref[:, sl] * sc * (gate_h * _sigmoid(gate_h))).astype(BF16)
            dpool_h = jnp.where(live, _dot(dmixed_h, mw_ref[g], NT), 0.0)
            q = dpool / jnp.minimum(t1, w).astype(F32)
            q_h = dpool_h / jnp.minimum(th1, w).astype(F32)
            qe = jnp.concatenate([q, q_h], axis=0)
            dp_ref[:, sl] = (_window_sums(qe, w, False)[:T, :] - dpool).astype(BF16)

    nxt = lambda i: jnp.minimum((i + 1) * hb, L // POOL_HALO - 1)
    row = lambda c: pl.BlockSpec((T, D), lambda i: (i, c))
    return pl.pallas_call(
        body, grid=(nT,),
        in_specs=[row(1), pl.BlockSpec((POOL_HALO, D), lambda i: (nxt(i), 1)),
                  row(0), pl.BlockSpec((POOL_HALO, D), lambda i: (nxt(i), 0)),
                  row(0), pl.BlockSpec((GROUPS, PGW, PGW), lambda i: (0, 0, 0)),
                  pl.BlockSpec((1, D), lambda i: (0, 0)), _ANY],
        out_specs=[pl.BlockSpec((T, 2 * D), lambda i: (i, 0)), pl.BlockSpec((GROUPS, PGW, PGW), lambda i: (0, 0, 0)),
                   pl.BlockSpec((1, D), lambda i: (0, 0))],
        out_shape=[_sds(dproj.shape, dproj.dtype), _sds((GROUPS, PGW, PGW), F32), _sds((1, D), F32)],
        input_output_aliases={7: 0},
        name="pool_bwd", compiler_params=_params("arbitrary"))(proj, proj, dmix, dmix, pooled, mixw, scale, dproj)


def conv_fwd(proj, cw, cb, D):
    L = proj.shape[0]
    C = cw.shape[1]
    assert (3 * D) % C == 0
    cblk = (3 * D) // C
    T = _pick(L, ROW_TILE)
    hb = T // CONV_HALO

    def body(u_ref, halo_ref, w_ref, b_ref, o_ref):
        i = pl.program_id(0)
        for c0 in range(0, C, LANES):
            sl = slice(c0, c0 + LANES)
            u = u_ref[:, sl]
            xe = jnp.concatenate([jnp.where(i > 0, halo_ref[:, sl], 0.0), u], axis=0)
            acc = b_ref[:, sl] + w_ref[CONV_K - 1:CONV_K, sl] * u
            for k in range(CONV_K - 1):
                acc = acc + w_ref[k:k + 1, sl] * pltpu.roll(xe, CONV_K - 1 - k, 0)[CONV_HALO:, :]
            o_ref[:, sl] = acc

    return pl.pallas_call(
        body, grid=(L // T,),
        in_specs=[pl.BlockSpec((T, C), lambda i: (i, cblk)),
                  pl.BlockSpec((CONV_HALO, C), lambda i: (jnp.maximum(i * hb - 1, 0), cblk)),
                  pl.BlockSpec((CONV_K, C), lambda i: (0, 0)),
                  pl.BlockSpec((1, C), lambda i: (0, 0))],
        out_specs=pl.BlockSpec((T, C), lambda i: (i, 0)),
        out_shape=_sds((L, C), F32), name="conv_fwd", compiler_params=_params("parallel"))(proj, proj, cw, cb)


def conv_bwd(dparts, proj, cw, dproj, D, jobs=()):
    L = proj.shape[0]
    C = cw.shape[1]
    cblk = (3 * D) // C
    T = _pick(L, ROW_TILE)
    hb = T // CONV_HALO
    nT = L // T
    widths = [p.shape[1] for p in dparts]
    assert sum(widths) == C
    n = len(dparts)

    def body(*refs):
        d_refs, dn_refs = refs[:n], refs[n:2 * n]
        u_ref, w_ref, old_ref, dr_ref, dw_ref, db_ref = refs[2 * n:]
        i = pl.program_id(0)

        @pl.when(i == 0)
        def _():
            dw_ref[...] = jnp.zeros_like(dw_ref)
            db_ref[...] = jnp.zeros_like(db_ref)

        at = 0
        for d_ref, dn_ref, wd in zip(d_refs, dn_refs, widths):
            for c0 in range(0, wd, LANES):
                ls, sl = slice(c0, c0 + LANES), slice(at + c0, at + c0 + LANES)
                d = d_ref[:, ls]
                u = u_ref[:, sl]
                de = jnp.concatenate([d, jnp.where(i < nT - 1, dn_ref[:, ls], 0.0)], axis=0)
                acc = w_ref[CONV_K - 1:CONV_K, sl] * d
                dw_ref[CONV_K - 1:CONV_K, sl] += jnp.sum(d * u, axis=0, keepdims=True)
                for k in range(CONV_K - 1):
                    sh = CONV_K - 1 - k
                    ds = pltpu.roll(de, T + CONV_HALO - sh, 0)[:T, :]
                    acc = acc + w_ref[k:k + 1, sl] * ds
                    dw_ref[k:k + 1, sl] += jnp.sum(ds * u, axis=0, keepdims=True)
                dr_ref[:, sl] = acc.astype(dr_ref.dtype)
                db_ref[:, sl] += jnp.sum(d, axis=0, keepdims=True)
            at += wd

    nxt = lambda i: jnp.minimum((i + 1) * hb, L // CONV_HALO - 1)
    return _call(
        body, grid=(nT,),
        in_specs=[pl.BlockSpec((T, wd), lambda i: (i, 0)) for wd in widths]
        + [pl.BlockSpec((CONV_HALO, wd), lambda i: (nxt(i), 0)) for wd in widths]
        + [pl.BlockSpec((T, C), lambda i: (i, cblk)), pl.BlockSpec((CONV_K, C), lambda i: (0, 0)), _ANY],
        out_specs=[pl.BlockSpec((T, C), lambda i: (i, cblk)),
                   pl.BlockSpec((8, C), lambda i: (0, 0)),
                   pl.BlockSpec((1, C), lambda i: (0, 0))],
        out_shape=[_sds(dproj.shape, dproj.dtype), _sds((8, C), F32), _sds((1, C), F32)],
        aliases={2 * n + 2: 0}, name="conv_bwd", sem=("arbitrary",),
        args=(*dparts, *dparts, proj, cw, dproj), jobs=jobs)


def _softplus(v):
    y = jnp.exp(-jnp.abs(v))
    u = 1.0 + y
    log1p = jnp.where(u == 1.0, y, jnp.log(u) * y / jnp.where(u == 1.0, 1.0, u - 1.0))
    return jnp.maximum(v, 0.0) + log1p


def dt_prep(dtraw, bias, alog, expand, D, jobs=()):
    L = dtraw.shape[0]
    GC = D // GROUPS
    HPG = GC // HEAD_DIM
    Q = _pick(L, SCAN_CHUNK)
    nc = L // Q

    def body(r_ref, b_ref, a_ref, e_ref, dt_ref, acs_ref, acst_ref, ex_ref, cdx_ref):
        valid = lax.broadcasted_iota(jnp.int32, (1, LANES), 1) < HPG
        tril = (_row(0, (Q, Q)) >= lax.broadcasted_iota(jnp.int32, (Q, Q), 1)).astype(BF16)
        e = e_ref[...]
        for g in range(GROUPS):
            hs, cs = slice(g * LANES, (g + 1) * LANES), slice(g * GC, (g + 1) * GC)
            dt = jnp.where(valid, _softplus(r_ref[:, hs] + b_ref[:, hs]), 0.0)
            adt = dt * -jnp.exp(a_ref[:, hs])
            acs = _sel_dot(tril, adt)
            last = acs[Q - 1:Q, :]
            dt_ref[:, hs] = dt
            acs_ref[:, hs] = acs
            acst_ref[hs, :] = acs.T
            ex_ref[0, :, cs] = _dot_sel(dt, e, 2)
            ex_ref[1, :, cs] = jnp.exp(_dot_sel(acs, e, 2))
            ex_ref[2, :, cs] = jnp.exp(_dot_sel(last - acs, e, 2))
            cdx_ref[0, :, cs] = jnp.exp(_dot_sel(jnp.broadcast_to(last, (8, LANES)), e, 2))

    head = pl.BlockSpec((Q, GROUPS * LANES), lambda c: (c, 0))
    hvec = pl.BlockSpec((1, GROUPS * LANES), lambda c: (0, 0))
    return _call(
        body, grid=(nc,),
        in_specs=[head, hvec, hvec, pl.BlockSpec((LANES, GC), lambda c: (0, 0))],
        out_specs=[head, head, pl.BlockSpec((GROUPS * LANES, Q), lambda c: (0, c)),
                   pl.BlockSpec((3, Q, D), lambda c: (0, c, 0)), pl.BlockSpec((1, 8, D), lambda c: (c, 0, 0))],
        out_shape=[_sds((L, GROUPS * LANES), F32), _sds((L, GROUPS * LANES), F32), _sds((GROUPS * LANES, L), F32),
                   _sds((3, L, D), F32), _sds((nc, 8, D), F32)],
        name="dt_prep", sem=("parallel",), args=(dtraw, bias, alog, expand), jobs=jobs)


def _scan_specs(L, D, Q, rev):
    GC = D // GROUPS
    nc = L // Q
    ci = (lambda c: nc - 1 - c) if rev else (lambda c: c)
    return dict(
        xs=pl.BlockSpec((Q, GC), lambda g, c: (ci(c), g)),
        b=pl.BlockSpec((Q, STATE), lambda g, c: (ci(c), D // STATE + g)),
        c=pl.BlockSpec((Q, STATE), lambda g, c: (ci(c), D // STATE + GROUPS + g)),
        chan=pl.BlockSpec((Q, GC), lambda g, c: (ci(c), g)),
        ex=pl.BlockSpec((3, Q, GC), lambda g, c: (0, ci(c), g)),
        cdx=pl.BlockSpec((1, 8, GC), lambda g, c: (ci(c), 0, g)),
        head=pl.BlockSpec((Q, LANES), lambda g, c: (ci(c), g)),
        headt=pl.BlockSpec((LANES, Q), lambda g, c: (g, ci(c))),
        state=pl.BlockSpec((1, 1, STATE, GC), lambda g, c: (ci(c), g, 0, 0)),
        hvec=pl.BlockSpec((1, LANES), lambda g, c: (0, g)),
        cvec=pl.BlockSpec((1, GC), lambda g, c: (0, g)))


def scan_fwd(pre, ex, cdx, acs, acst, proj, dexp, nw, mixed, D, jobs=()):
    L = pre.shape[0]
    GC = D // GROUPS
    Q = _pick(L, SCAN_CHUNK)
    nc = L // Q
    sp = _scan_specs(L, D, Q, False)

    def body(xs_ref, b_ref, c_ref, ex_ref, cdx_ref, acs_ref, acst_ref, z_ref, de_ref, nw_ref,
             old_ref, y_ref, st_ref, o_ref, s_scr):
        @pl.when(pl.program_id(1) == 0)
        def _():
            s_scr[...] = jnp.zeros_like(s_scr)

        tri = _row(0, (Q, Q)) >= lax.broadcasted_iota(jnp.int32, (Q, Q), 1)
        half = lax.broadcasted_iota(jnp.int32, (1, LANES), 1) // HEAD_DIM
        xs, _ = _silu_and_grad(xs_ref[...])
        bg = _silu_and_grad(b_ref[...])[0].astype(BF16)
        cg = _silu_and_grad(c_ref[...])[0].astype(BF16)
        xdt = xs * ex_ref[0]
        sprev = s_scr[...]
        st_ref[0, 0] = sprev
        sc = _dot(cg, bg, NT)
        yoff = _dot(cg, sprev.astype(BF16)) * ex_ref[1]
        for j in range(GC // LANES):
            ps = slice(j * LANES, (j + 1) * LANES)
            xp = xdt[:, ps]
            acc = yoff[:, ps]
            for hh in range(2):
                h = 2 * j + hh
                lm = jnp.exp(jnp.where(tri, acs_ref[:, h:h + 1] - acst_ref[h:h + 1, :], -1e30))
                xm = jnp.where(half == hh, xp, 0.0).astype(BF16)
                acc = acc + _dot((sc * lm).astype(BF16), xm)
            y_ref[:, ps] = acc
        xw = (xdt * ex_ref[2]).astype(BF16)
        s_scr[...] = cdx_ref[0, 0:1, :] * sprev + _dot(bg, xw, TN)
        z = z_ref[...]
        y3 = (y_ref[...] + de_ref[...] * xs) * (z * _sigmoid(z))
        r = lax.rsqrt(jnp.mean(y3 * y3, axis=-1, keepdims=True) + NORM_EPS)
        o_ref[...] = (y3 * r * nw_ref[...]).astype(o_ref.dtype)

    return _call(
        body, grid=(GROUPS, nc),
        in_specs=[sp["xs"], sp["b"], sp["c"], sp["ex"], sp["cdx"], sp["head"], sp["headt"],
                  pl.BlockSpec((Q, GC), lambda g, c: (c, 2 * GROUPS + g)), sp["cvec"], sp["cvec"], _ANY],
        out_specs=[sp["chan"], sp["state"], pl.BlockSpec((Q, GC), lambda g, c: (c, GROUPS + g))],
        out_shape=[_sds((L, D), F32), _sds((nc, GROUPS, STATE, GC), F32), _sds(mixed.shape, mixed.dtype)],
        aliases={10: 2}, scratch_shapes=[pltpu.VMEM((STATE, GC), F32)], name="scan_fwd",
        sem=("parallel", "arbitrary"), args=(pre, pre, pre, ex, cdx, acs, acst, proj, dexp, nw, mixed),
        jobs=jobs)


def scan_bwd(pre, ex, cdx, acs, acst, dt, dtraw, bias, alog, states, y, proj, dmix, nw, dexp, collapse, D,
             jobs=()):
    L = pre.shape[0]
    GC = D // GROUPS
    Q = _pick(L, SCAN_CHUNK)
    nc = L // Q
    sp = _scan_specs(L, D, Q, True)
    rc = lambda c: nc - 1 - c

    def body(xs_ref, b_ref, c_ref, ex_ref, cdx_ref, acs_ref, acst_ref, dt_ref, raw_ref,
             bias_ref, alog_ref, st_ref, y_ref, z_ref, dm_ref, nw_ref, dexp_ref, col_ref,
             dxs_ref, db_ref, dc_ref, ddt_ref, dal_ref, dbi_ref, dz_ref, dnw_ref, dde_ref, ds_scr, dx_scr):
        first = pl.program_id(1) == 0

        @pl.when(first)
        def _():
            ds_scr[...] = jnp.zeros_like(ds_scr)
            dal_ref[...] = jnp.zeros_like(dal_ref)
            dbi_ref[...] = jnp.zeros_like(dbi_ref)
            dnw_ref[...] = jnp.zeros_like(dnw_ref)
            dde_ref[...] = jnp.zeros_like(dde_ref)

        li = _row(0, (Q, Q))
        si = lax.broadcasted_iota(jnp.int32, (Q, Q), 1)
        lane = lax.broadcasted_iota(jnp.int32, (1, LANES), 1)
        half = lane // HEAD_DIM
        xs_pre, b_pre, c_pre = xs_ref[...], b_ref[...], c_ref[...]
        xs, xs_g = _silu_and_grad(xs_pre)
        bf, b_g = _silu_and_grad(b_pre)
        cf, c_g = _silu_and_grad(c_pre)
        bg, cg = bf.astype(BF16), cf.astype(BF16)
        dtx, eax, dsx = ex_ref[0], ex_ref[1], ex_ref[2]
        cd = cdx_ref[0, 0:1, :]
        xdt = xs * dtx
        sz, dsz = _silu_and_grad(z_ref[...])
        y2 = y_ref[...] + dexp_ref[...] * xs
        y3 = y2 * sz
        r = lax.rsqrt(jnp.mean(y3 * y3, axis=-1, keepdims=True) + NORM_EPS)
        n = y3 * r
        dm = dm_ref[...]
        gg = dm * nw_ref[...]
        dy3 = r * (gg - n * jnp.mean(gg * n, axis=-1, keepdims=True))
        dnw_ref[0:1, :] += jnp.sum(dm * n, axis=0, keepdims=True)
        G = dy3 * sz
        dz_ref[...] = (dy3 * y2 * dsz).astype(dz_ref.dtype)
        dde_ref[0:1, :] += jnp.sum(G * xs, axis=0, keepdims=True)
        prev = st_ref[0, 0]
        dsn = ds_scr[...]
        prev_b, dsn_b = prev.astype(BF16), dsn.astype(BF16)
        cp = _dot(cg, prev_b)
        ge_b = (G * eax).astype(BF16)
        d_c = _dot(ge_b, prev_b, NT)
        dprev = _dot(cg, ge_b, TN) + cd * dsn
        chan_a = G * cp * eax
        xw_b = (xdt * dsx).astype(BF16)
        dcd = jnp.sum(prev * dsn, axis=0, keepdims=True)
        d_b = _dot(xw_b, dsn_b, NT)
        dxw = _dot(bg, dsn_b)
        dd = dxw * xdt * dsx
        chan_a = chan_a - dd
        last_c = jnp.sum(dd, axis=0, keepdims=True) + dcd * cd
        sc = _dot(cg, bg, NT)
        head_row = _row(0, (LANES, 1))
        dsc = jnp.zeros((Q, Q), F32)
        dacs = jnp.zeros((Q, LANES), F32)
        colsums = jnp.zeros((LANES, Q), F32)
        for j in range(GC // LANES):
            ps = slice(j * LANES, (j + 1) * LANES)
            xp, gp = xdt[:, ps], G[:, ps]
            dxp = dxw[:, ps] * dsx[:, ps]
            for hh in range(2):
                h = 2 * j + hh
                lm = jnp.exp(jnp.where(li >= si, acs_ref[:, h:h + 1] - acst_ref[h:h + 1, :], -1e30))
                m = sc * lm
                xm = jnp.where(half == hh, xp, 0.0).astype(BF16)
                gm = jnp.where(half == hh, gp, 0.0).astype(BF16)
                dm = _dot(gm, xm, NT)
                dxp = dxp + _dot(m.astype(BF16), gm, TN)
                dsc = dsc + dm * lm
                w = dm * m
                dacs = dacs + jnp.where(lane == h, jnp.sum(w, axis=1, keepdims=True), 0.0)
                colsums = jnp.where(head_row == h, jnp.sum(w, axis=0, keepdims=True), colsums)
            dx_scr[:, ps] = dxp
        dacs = dacs - colsums.T
        dsc_b = dsc.astype(BF16)
        d_c = d_c + _dot(dsc_b, bg)
        d_b = d_b + _dot(dsc_b, cg, TN)
        ds_scr[...] = dprev
        dxdt = dx_scr[...]
        dxs_ref[...] = (dxdt * dtx + dexp_ref[...] * G) * xs_g
        db_ref[...] = d_b * b_g
        dc_ref[...] = d_c * c_g
        colm = col_ref[...]
        dacs = dacs + _dot_sel(chan_a, colm)
        dlast = _dot_sel(jnp.broadcast_to(last_c, (8, GC)), colm)[0:1, :]
        dacs = dacs + jnp.where(_row(0, (Q, 1)) == Q - 1, dlast, 0.0)
        dadt = _sel_dot((si >= li).astype(BF16), dacs)
        a = -jnp.exp(alog_ref[...])
        dt = dt_ref[...]
        ddt = dadt * a + _dot_sel(dxdt * xs, colm, 2)
        dal_ref[0:1, :] += jnp.sum(dadt * dt * a, axis=0, keepdims=True)
        draw = ddt * _sigmoid(raw_ref[...] + bias_ref[...])
        dbi_ref[0:1, :] += jnp.sum(draw, axis=0, keepdims=True)
        ddt_ref[...] = draw.astype(ddt_ref.dtype)

    acc = pl.BlockSpec((8, LANES), lambda g, c: (0, g))
    cacc = pl.BlockSpec((8, GC), lambda g, c: (0, g))
    return _call(
        body, grid=(GROUPS, nc),
        in_specs=[sp["xs"], sp["b"], sp["c"], sp["ex"], sp["cdx"], sp["head"], sp["headt"],
                  sp["head"], sp["head"], sp["hvec"], sp["hvec"], sp["state"], sp["chan"],
                  pl.BlockSpec((Q, GC), lambda g, c: (rc(c), 2 * GROUPS + g)),
                  pl.BlockSpec((Q, GC), lambda g, c: (rc(c), GROUPS + g)), sp["cvec"], sp["cvec"],
                  pl.BlockSpec((GC, LANES), lambda g, c: (0, 0))],
        out_specs=[sp["chan"],
                   pl.BlockSpec((Q, STATE), lambda g, c: (rc(c), g)),
                   pl.BlockSpec((Q, STATE), lambda g, c: (rc(c), g)),
                   sp["head"], acc, acc,
                   pl.BlockSpec((Q, GC), lambda g, c: (rc(c), 2 * GROUPS + g)), cacc, cacc],
        out_shape=[_sds((L, D), F32), _sds((L, GROUPS * STATE), F32), _sds((L, GROUPS * STATE), F32),
                   _sds((L, GROUPS * LANES), BF16), _sds((8, GROUPS * LANES), F32), _sds((8, GROUPS * LANES), F32),
                   _sds((L, proj.shape[1]), BF16), _sds((8, D), F32), _sds((8, D), F32)],
        scratch_shapes=[pltpu.VMEM((STATE, GC), F32), pltpu.VMEM((Q, GC), F32)], name="scan_bwd",
        sem=("parallel", "arbitrary"),
        args=(pre, pre, pre, ex, cdx, acs, acst, dt, dtraw, bias, alog, states, y, proj, dmix, nw, dexp,
              collapse),
        jobs=jobs)


def _adam_math(gv, w, m, v):
    c1 = 1.0 - ADAM_B1 ** ADAM_STEP
    c2 = 1.0 - ADAM_B2 ** ADAM_STEP
    nm = ADAM_B1 * m + (1.0 - ADAM_B1) * gv
    nv = ADAM_B2 * v + (1.0 - ADAM_B2) * (gv * gv)
    return -ADAM_LR * ((nm / c1) / (jnp.sqrt(nv / c2) + ADAM_EPS) + ADAM_WD * w), nm, nv


def adamw(g, w, m, v, name):
    R, C = g.shape
    T = R if R <= 128 else 128
    assert R % T == 0

    def body(g_ref, w_ref, m_ref, v_ref, d_ref, nm_ref, nv_ref):
        d_ref[...], nm_ref[...], nv_ref[...] = _adam_math(g_ref[...], w_ref[...], m_ref[...], v_ref[...])

    blk = pl.BlockSpec((T, C), lambda i: (i, 0))
    return pl.pallas_call(
        body, grid=(R // T,), in_specs=[blk] * 4, out_specs=[blk] * 3,
        out_shape=[_sds((R, C), F32)] * 3, name=name, compiler_params=_params("parallel"))(g, w, m, v)


def adamw_layer(g, w, m, v, layer, prev, name):
    R, C = g.shape
    T = _rows_tile(R, ROW_TILE)
    assert R % T == 0

    def body(g_ref, w_ref, m_ref, v_ref, *rest):
        go_ref, d_ref, nm_ref, nv_ref = rest[-4:]
        gv = g_ref[...]
        go_ref[0] = gv
        d_ref[0], nm_ref[0], nv_ref[0] = _adam_math(gv, w_ref[0], m_ref[0], v_ref[0])

    mine = pl.BlockSpec((1, T, C), lambda i: (layer, i, 0))
    prev = list(prev or [])
    return pl.pallas_call(
        body, grid=(R // T,), in_specs=[pl.BlockSpec((T, C), lambda i: (i, 0)), mine, mine, mine] + [_ANY] * len(prev),
        out_specs=[mine] * 4, out_shape=[_sds(w.shape, F32)] * 4,
        input_output_aliases={4 + i: i for i in range(len(prev))}, name=name,
        compiler_params=_params("parallel"))(g, w, m, v, *prev)


def cast_bf16(a, layer):
    _, R, C = a.shape
    T = _rows_tile(R, ROW_TILE)

    def body(a_ref, o_ref):
        o_ref[...] = a_ref[0].astype(BF16)

    return pl.pallas_call(
        body, grid=(R // T,), in_specs=[pl.BlockSpec((1, T, C), lambda i: (layer, i, 0))],
        out_specs=pl.BlockSpec((T, C), lambda i: (i, 0)), out_shape=_sds((R, C), BF16),
        name="cast_bf16", compiler_params=_params("parallel"))(a)


def _to_groups(v, hpg):
    lead = v.shape[:-1]
    t = v.reshape(lead + (GROUPS, hpg))
    t = jnp.pad(t, [(0, 0)] * (len(lead) + 1) + [(0, LANES - hpg)])
    return t.reshape(lead + (GROUPS * LANES,))


def _from_groups(a, hpg):
    lead = a.shape[:-1]
    return a.reshape(lead + (GROUPS, LANES))[..., :hpg].reshape(lead + (GROUPS * hpg,))


def _expand_matrix(D):
    gc = D // GROUPS
    return (jnp.arange(LANES)[:, None] == (jnp.arange(gc)[None, :] // HEAD_DIM)).astype(BF16)


def layer_params(pre_w, w_in_t, mixw, scale, cw, cb, bias, alog, dskip, nw, w_out_full, post_w, D):
    hpg = D // GROUPS // HEAD_DIM
    main = w_in_t.shape[0] - GROUPS * hpg
    wdt_t = _to_groups(w_in_t[main:].T, hpg).T
    return dict(
        pre_w=pre_w[None], win_t=w_in_t, main=main, wdt_t=wdt_t, mixw=mixw,
        scale=scale[None], cw=cw, cb=cb[None], bias=_to_groups(bias, hpg)[None], alog=_to_groups(alog, hpg)[None],
        dexp=jnp.repeat(dskip, HEAD_DIM)[None], nw=nw[None], wout=w_out_full, post_w=post_w[None])


def layer_fwd(x, p, D, next_shards=None, h=None, next_pre_w=None, own_rest=None):
    nxt = bool(next_shards)
    parts = lambda a: ((a.shape[0] // 32) * 16, a.shape[0] - (a.shape[0] // 32) * 16)
    if nxt:
        s_in, s_out, s_mix, s_cw = next_shards
        (in_a, in_b), (out_a, out_b) = parts(s_in), parts(s_out)
    if h is None:
        h = rms_fwd(x, p["pre_w"])
    own = own_rest is not None
    proj, got = matmul(h, p["win_t"], "nt", F32, "proj", tm=1024, n_out=p["main"],
                       jobs=([gather_rows_job(s_in, 0, in_a)] if nxt else [])
                       + ([gather_send_job(own_rest[0], own_rest[1])] if own else []))
    g_in = got[0][0] if nxt else None
    dtraw, got = matmul(h, p["wdt_t"], "nt", F32, "dtproj",
                        jobs=[gather_pass_job(got[-1], own_rest[1])] if own else [])
    if own:
        p.update(own_rest[2](got[0]))
    mixed, pooled = pool_fwd(proj, p["mixw"], p["scale"], D)
    pre = conv_fwd(proj, p["cw"], p["cb"], D)
    dtp, got = dt_prep(dtraw, p["bias"], p["alog"], _expand_matrix(D), D,
                       jobs=[gather_rows_job(s_out, 0, out_a)] if nxt else [])
    g_out = got[0][0] if nxt else None
    dt, acs, acst, ex, cdx = dtp
    (y, states, mixed), got = scan_fwd(pre, ex, cdx, acs, acst, proj, p["dexp"], p["nw"], mixed, D,
                                       jobs=[gather_rows_job(s_in, in_a, in_b, g_in)] if nxt else [])
    g_in = got[0][0] if nxt else None
    out, got = matmul(mixed, p["wout"], "nn", F32, "outproj", tm=1024,
                      jobs=[gather_rows_job(s_out, out_a, out_b, g_out), gather_send_job([s_mix, s_cw]),
                            gather_pass_job([g_in], [True])] if nxt else [])
    (xn, h_next), got2 = post_fwd(x, out, p["post_w"], next_pre_w,
                                  jobs=[gather_pass_job(got[0] + got[1], [True, False, False])] if nxt else [])
    gathered = got[2] + got2[0] if nxt else None
    return xn, dict(x=x, h=h, proj=proj, dtraw=dtraw, pooled=pooled, pre=pre, dtp=dtp, y=y, states=states,
                    mixed=mixed, out=out), gathered, h_next


def layer_bwd(dxn, p, s, D, where=None, post=None, below=None, pending=None):
    reduce = where is not None
    chip, core = where if reduce else (None, None)
    hpg = D // GROUPS // HEAD_DIM
    PGW = D // GROUPS
    main = p["main"]
    SH = (main + GROUPS * hpg) // 4
    dt, acs, acst, ex, cdx = s["dtp"]
    dout, d_post = post if post else post_bwd(s["out"], p["post_w"], dxn)
    part = BF16 if reduce else F32
    d_wout, got = matmul(s["mixed"], dout, "tn", part, "dwout", tm=1024,
                         jobs=[pair_gather_job(pending, [True, False])] if pending else [])
    above = dict(w_in=got[0][0], pool_mix_w=got[0][1]) if pending else None
    g_out = d_wout.reshape(4, 2, D // 4, D)
    dmix, got = matmul(dout, p["wout"], "nt", F32, "dmixed", tm=1024,
                       jobs=[pair_exchange_job([g_out])] if reduce else [])
    pair_out = pair_add(g_out, got[0][0], core, BF16) if reduce else None
    (dxs, db, dc, ddtraw, d_alog, d_bias, dproj, d_nw, d_dexp), got = scan_bwd(
        s["pre"], ex, cdx, acs, acst, dt, s["dtraw"], p["bias"], p["alog"], s["states"], s["y"],
        s["proj"], dmix, p["nw"], p["dexp"], _expand_matrix(D).T, D,
        jobs=[chip_exchange_job([pair_out])] if reduce else [])
    mine_out = chip_add(pair_out, got[0][0], chip, core) if reduce else None
    (dproj, d_cw, d_cb), got = conv_bwd([dxs, db, dc], s["proj"], p["cw"], dproj, D,
                                        jobs=[pair_gather_job([mine_out])] if reduce else [])
    r_out = got[0][0] if reduce else None
    dproj, d_mixw, d_scale = pool_bwd(s["proj"], dmix, s["pooled"], p["mixw"], p["scale"], dproj, D)
    d_wmain_t, _ = matmul(dproj, s["h"], "tn", part, "dwmain", tm=1024)
    d_wdt_t, _ = matmul(ddtraw, s["h"], "tn", part, "dwdt")
    late = [d_wmain_t[None], d_wdt_t[None],
            d_mixw.reshape(GROUPS, 4, PGW // 4, PGW).transpose(1, 0, 2, 3).reshape(4, 2, GROUPS * PGW // 8, PGW)]
    cols = [True, True, False]
    dh_b, got = matmul(ddtraw, p["wdt_t"], "nn", F32, "dh_dt",
                       jobs=[pair_exchange_job(late, cols)] if reduce else [])
    if reduce:
        p_main, p_dt, p_mix = [pair_add(g, r, core, BF16, k, n) for g, r, k, n in
                               zip(late, got[0], cols, [4 * SH, None, None])]
        p_in = lax.dynamic_update_slice(p_main[0], _from_groups(p_dt[0].T, hpg).T, (main, 0))
        pairs = [p_in.reshape(4, SH, D // 2), p_mix]
    cut = (SH * 5 // 128) * 16
    dh_a, got = matmul(dproj, p["win_t"], "nn", F32, "dh_main",
                       jobs=[chip_exchange_rows_job(pairs[0], 0, cut), chip_exchange_job(pairs[1:])] if reduce else [])
    (dx, d_pre, *post_below), got2 = rms_bwd(
        s["x"], p["pre_w"], dh_a, dh_b, dxn, below,
        jobs=[chip_exchange_rows_job(pairs[0], cut, SH - cut, got[0][0])] if reduce else [])
    mines = [chip_add(pairs[0], got2[0][0], chip, core, True),
             chip_add(pairs[1], got[1][0], chip, core, False)] if reduce else None
    reduced = dict(w_out=r_out) if reduce else None
    grads = dict(
        pre_norm_w=d_pre[0], pool_scale=d_scale[0], conv_w=d_cw[:CONV_K], conv_b=d_cb[0],
        dt_bias=_from_groups(d_bias[0], hpg), a_log=_from_groups(d_alog[0], hpg),
        d_skip=d_dexp[0].reshape(-1, HEAD_DIM).sum(axis=-1), ssd_norm_w=d_nw[0], post_norm_w=d_post[0])
    if not reduce:
        grads.update(w_in=jnp.concatenate([d_wmain_t.T, _from_groups(d_wdt_t.T, hpg)], axis=1), pool_mix_w=d_mixw,
                     w_out=d_wout)
    return dx, grads, reduced, tuple(post_below) or None, mines, above


def local_step(x, target, params, D):
    saved, h, n = [], None, len(params)
    for l, p in enumerate(params):
        x, s, _, h = layer_fwd(x, p, D, h=h, next_pre_w=params[l + 1]["pre_w"] if l + 1 < n else None)
        saved.append(s)
    dx, sumsq = loss_head(x, target)
    grads, post = [None] * n, None
    for l in reversed(range(n)):
        below = (saved[l - 1]["out"], params[l - 1]["post_w"]) if l else None
        dx, grads[l], _, post, _, _ = layer_bwd(dx, params[l], saved[l], D, post=post, below=below)
    return sumsq, dx, grads


SMALL = ("pre_norm_w", "pool_scale", "conv_w", "conv_b", "dt_bias", "a_log", "d_skip", "ssd_norm_w", "post_norm_w")
BIG = ("w_in", "w_out", "pool_mix_w")


def _pack(parts):
    flat = jnp.concatenate([p.reshape(-1) for p in parts])
    n = flat.shape[0]
    rows = -(-n // (LANES * LANES)) * LANES
    return jnp.pad(flat, (0, rows * LANES - n)).reshape(rows, LANES)


def _unpack(packed, shapes):
    flat, out, at = packed.reshape(-1), [], 0
    for s in shapes:
        n = math.prod(s)
        out.append(flat[at:at + n].reshape(s))
        at += n
    return out


def kernel(x, pre_norm_w, w_in, pool_mix_w, pool_scale, conv_w, conv_b, dt_bias, a_log, d_skip, ssd_norm_w, w_out, post_norm_w, loss_target, m_pre_norm_w, m_w_in, m_pool_mix_w, m_pool_scale, m_conv_w, m_conv_b, m_dt_bias, m_a_log, m_d_skip, m_ssd_norm_w, m_w_out, m_post_norm_w, v_pre_norm_w, v_w_in, v_pool_mix_w, v_pool_scale, v_conv_w, v_conv_b, v_dt_bias, v_a_log, v_d_skip, v_ssd_norm_w, v_w_out, v_post_norm_w):
    NL, D, SH = w_in.shape
    PGW = D // GROUPS
    CS = conv_w.shape[2]
    chip = (2 * lax.axis_index("x") + lax.axis_index("y")).astype(jnp.int32)
    chip1, core = chip.reshape(1), lax.axis_index("c").astype(jnp.int32).reshape(1)

    tr = lambda t: jnp.transpose(t, (0, 2, 1))
    w_in_t, m_w_in_t, v_w_in_t = tr(w_in), tr(m_w_in), tr(v_w_in)
    halved_by_cols = [True, True, False, False]

    def shards(l):
        return [cast_bf16(w_in_t, l), cast_bf16(w_out, l),
                cast_bf16(pool_mix_w.reshape(NL, GROUPS * PGW // 4, PGW), l).reshape(2, GROUPS * PGW // 8, PGW),
                conv_w[l].reshape(2, CONV_K * CS // (2 * LANES), LANES)]

    def rest_params(g):
        g_out, g_mix, g_cw = g
        return dict(wout=g_out.reshape(2 * D, D),
                    mixw=g_mix.reshape(4, GROUPS, PGW // 4, PGW).transpose(1, 0, 2, 3).reshape(GROUPS, PGW, PGW),
                    cw=g_cw.reshape(4, CONV_K, CS).transpose(1, 0, 2).reshape(CONV_K, 4 * CS))

    def params(l, g):
        late = rest_params(g[1:]) if len(g) > 1 else dict(wout=None, mixw=None, cw=None)
        return layer_params(pre_norm_w[l], g[0].reshape(4 * SH, D), late["mixw"], pool_scale[l], late["cw"], conv_b[l],
                            dt_bias[l], a_log[l], d_skip[l], ssd_norm_w[l], late["wout"], post_norm_w[l], D)

    first = shards(0)
    gathered = run_jobs([gather_send_job(first[:1], [True])], "gather_send")[0]
    gathered = run_jobs([gather_pass_job(gathered, [True])], "gather_pass")[0]
    xl, h, ps, saved = x[0], None, [], []
    for l in range(NL):
        ps.append(params(l, gathered))
        last = l + 1 == NL
        xl, s, gathered, h = layer_fwd(xl, ps[l], D, None if last else shards(l + 1), h,
                                       None if last else pre_norm_w[l + 1][None],
                                       (first[1:], halved_by_cols[1:], rest_params) if l == 0 else None)
        saved.append(s)
    dx, sumsq = loss_head(xl, loss_target[0])

    given = dict(w_in=(w_in_t, m_w_in_t, v_w_in_t), w_out=(w_out, m_w_out, v_w_out),
                 pool_mix_w=(pool_mix_w, m_pool_mix_w, v_pool_mix_w))
    flat = {n: [t.reshape(NL, -1, t.shape[-1]) for t in given[n]] for n in BIG}
    done = {n: None for n in BIG}
    def update(l, reduced):
        for n, r in reduced.items():
            done[n] = adamw_layer(r.reshape(-1, r.shape[-1]), *flat[n], l, done[n], "adamw_" + n)

    grads, post, pending = [None] * NL, None, None
    for l in reversed(range(NL)):
        below = (saved[l - 1]["out"], ps[l - 1]["post_w"]) if l else None
        dx, grads[l], reduced, post, pending, above = layer_bwd(dx, ps[l], saved[l], D, (chip1, core), post, below,
                                                                pending)
        update(l, reduced)
        if above:
            update(l + 1, above)
    last = run_jobs([pair_gather_job(pending, [True, False])], "pair_gather")[0]
    update(0, dict(w_in=last[0], pool_mix_w=last[1]))

    small_shapes = [(NL,) + grads[0][n].shape for n in SMALL]
    packed = _pack([0.5 / D * sumsq[0, :1]] + [jnp.stack([g[n] for g in grads]) for n in SMALL])
    total = allreduce_small(packed)
    loss, *small = _unpack(total, [(1,)] + small_shapes)
    small = dict(zip(SMALL, small))
    small["conv_w"] = lax.dynamic_slice_in_dim(small["conv_w"], chip * CS, CS, axis=2)

    given_small = dict(
        pre_norm_w=(pre_norm_w, m_pre_norm_w, v_pre_norm_w), pool_scale=(pool_scale, m_pool_scale, v_pool_scale),
        conv_w=(conv_w, m_conv_w, v_conv_w), conv_b=(conv_b, m_conv_b, v_conv_b),
        dt_bias=(dt_bias, m_dt_bias, v_dt_bias), a_log=(a_log, m_a_log, v_a_log),
        d_skip=(d_skip, m_d_skip, v_d_skip), ssd_norm_w=(ssd_norm_w, m_ssd_norm_w, v_ssd_norm_w),
        post_norm_w=(post_norm_w, m_post_norm_w, v_post_norm_w))
    shapes = [given_small[n][0].shape for n in SMALL]
    upd = adamw(_pack([small[n] for n in SMALL]), *[_pack([given_small[n][i] for n in SMALL]) for i in range(3)],
                "adamw_small")
    upd = [dict(zip(SMALL, _unpack(u, shapes))) for u in upd]

    out = {n: (small[n], upd[0][n], upd[1][n], upd[2][n]) for n in SMALL}
    for n in BIG:
        out[n] = tuple(t.reshape(given[n][0].shape) for t in done[n])
    out["w_in"] = tuple(tr(t) for t in out["w_in"])

    order = ("pre_norm_w", "w_in", "pool_mix_w", "pool_scale", "conv_w", "conv_b", "dt_bias", "a_log", "d_skip",
             "ssd_norm_w", "w_out", "post_norm_w")
    return (loss.reshape(()), dx[None], *[out[n][0] for n in order], *[out[n][1] for n in order],
            *[out[n][2] for n in order], *[out[n][3] for n in order])
```

```python
import functools
import math

import jax
import jax.numpy as jnp
from jax import lax
from jax.experimental import pallas as pl
from jax.experimental.pallas import tpu as pltpu

F32 = jnp.float32
BF16 = jnp.bfloat16

NORM_EPS = 1e-6
HEAD_DIM = 64
STATE = 128
GROUPS = 4
POOL_WINDOWS = (2, 4, 8, 16)
POOL_HALO = 16
CONV_K = 4
CONV_HALO = 8
SCAN_CHUNK = 256
LANES = 128
VMEM_LIMIT = 52 * 1024 * 1024
ROW_TILE = 256

ADAM_LR = 0.001
ADAM_B1 = 0.9
ADAM_B2 = 0.999
ADAM_EPS = 1e-08
ADAM_WD = 0.01
ADAM_STEP = 10

MESH = pl.DeviceIdType.MESH

NN = (((1,), (0,)), ((), ()))
NT = (((1,), (1,)), ((), ()))
TN = (((0,), (0,)), ((), ()))

_ANY = pl.BlockSpec(memory_space=pl.ANY)


def _params(*sem):
    return pltpu.CompilerParams(dimension_semantics=sem, vmem_limit_bytes=VMEM_LIMIT)


def _pick(dim, pref):
    if dim <= pref:
        return dim
    t = (pref // LANES) * LANES
    while t > LANES and dim % t:
        t -= LANES
    assert dim % t == 0, (dim, pref)
    return t


def _rows_tile(rows, pref):
    t = (min(pref, rows) // 8) * 8
    while t >= 8 and rows % t:
        t -= 8
    return t if t >= 8 else rows


def _dot(a, b, dn=NN):
    return lax.dot_general(a, b, dn, preferred_element_type=F32)


def _split3(a):
    hi = a.astype(BF16)
    r = a - hi.astype(F32)
    mid = r.astype(BF16)
    return hi, mid, (r - mid.astype(F32)).astype(BF16)


def _dot_sel(a, e, parts=3):
    hi, mid, lo = _split3(a)
    return (_dot(lo, e) + _dot(mid, e)) + _dot(hi, e) if parts == 3 else _dot(mid, e) + _dot(hi, e)


def _sel_dot(e, b):
    hi, mid, lo = _split3(b)
    return (_dot(e, lo) + _dot(e, mid)) + _dot(e, hi)


def _sigmoid(v):
    return 0.5 * jnp.tanh(0.5 * v) + 0.5


def _silu_and_grad(v):
    s = _sigmoid(v)
    return v * s, s * (1.0 + v * (1.0 - s))


def _row(i, shape):
    return lax.broadcasted_iota(jnp.int32, shape, 0) + i


def _sds(shape, dtype):
    return jax.ShapeDtypeStruct(tuple(shape), dtype)


class Job:
    def __init__(self, ins, outs, aliased, nsem, start, finish):
        self.ins, self.outs, self.aliased, self.nsem, self.start, self.finish = ins, outs, aliased, nsem, start, finish


def _place():
    x, y, c = lax.axis_index("x"), lax.axis_index("y"), lax.axis_index("c")
    return x, y, c, [(1 - x, y), (x, 1 - y), (1 - x, 1 - y)]


def _remote(src, dst, send_sem, recv_sem, device):
    return pltpu.make_async_remote_copy(src_ref=src, dst_ref=dst, send_sem=send_sem, recv_sem=recv_sem,
                                        device_id=device, device_id_type=MESH)


def _half(ref, c, cols, lead=0):
    idx = [slice(None)] * lead
    if cols:
        w = ref.shape[-1] // 2
        idx += [slice(None)] * (len(ref.shape) - lead - 1) + [pl.ds(pl.multiple_of(c * w, LANES), w)]
    else:
        idx += [c]
    return ref.at[tuple(idx)]


def _flags(cols, n):
    return list(cols) if cols else [False] * n


def gather_send_job(arrs, cols=None):
    n = len(arrs)
    cols = _flags(cols, n)

    def copies(ins, outs, send, recv):
        x, y, c, chips = _place()
        mine = 2 * x + y
        out = []
        for a in range(n):
            out.append(_remote(ins[a], outs[a].at[mine], send.at[4 * a + 3], recv.at[4 * a + 3], (x, y, 1 - c)))
            for j, chip in enumerate(chips):
                out.append(_remote(_half(ins[a], c, cols[a]), _half(outs[a].at[mine], c, cols[a]),
                                   send.at[4 * a + j], recv.at[4 * a + j], (*chip, c)))
        return out

    def start(ins, outs, send, recv):
        for cp in copies(ins, outs, send, recv):
            cp.start()

    def finish(ins, outs, send, recv):
        x, y, c, chips = _place()
        for a in range(n):
            for j, chip in enumerate(chips):
                landed = _half(outs[a].at[2 * chip[0] + chip[1]], c, cols[a])
                _remote(landed, landed, send.at[4 * a + j], recv.at[4 * a + j], (x, y, 1 - c)).wait_recv()
            twin = outs[a].at[2 * x + y]
            _remote(twin, twin, send.at[4 * a + 3], recv.at[4 * a + 3], (x, y, 1 - c)).wait_recv()
        for cp in copies(ins, outs, send, recv):
            cp.wait_send()

    return Job(list(arrs), [_sds((4,) + a.shape, a.dtype) for a in arrs], False, 4 * n, start, finish)


def gather_rows_job(shard, start, size, into=None):
    rows = pl.ds(start, size)

    def copies(ins, outs, send, recv):
        x, y, c, chips = _place()
        mine = outs[0].at[2 * x + y, rows]
        out = [_remote(ins[0].at[rows], mine, send.at[3], recv.at[3], (x, y, 1 - c))]
        for j, chip in enumerate(chips):
            out.append(_remote(_half(ins[0].at[rows], c, True), _half(mine, c, True), send.at[j], recv.at[j],
                               (*chip, c)))
        return out

    def start_(ins, outs, send, recv):
        for cp in copies(ins, outs, send, recv):
            cp.start()

    def finish(ins, outs, send, recv):
        x, y, c, chips = _place()
        for j, chip in enumerate(chips):
            landed = _half(outs[0].at[2 * chip[0] + chip[1], rows], c, True)
            _remote(landed, landed, send.at[j], recv.at[j], (x, y, 1 - c)).wait_recv()
        twin = outs[0].at[2 * x + y, rows]
        _remote(twin, twin, send.at[3], recv.at[3], (x, y, 1 - c)).wait_recv()
        for cp in copies(ins, outs, send, recv):
            cp.wait_send()

    ins = [shard] if into is None else [shard, into]
    return Job(ins, [_sds((4,) + shard.shape, shard.dtype)], {} if into is None else {1: 0}, 4, start_, finish)


def gather_pass_job(bufs, cols=None):
    n = len(bufs)
    cols = _flags(cols, n)

    def copies(outs, send, recv):
        x, y, c, chips = _place()
        out = []
        for a in range(n):
            for j, chip in enumerate(chips):
                landed = _half(outs[a].at[2 * chip[0] + chip[1]], c, cols[a])
                out.append(_remote(landed, landed, send.at[3 * a + j], recv.at[3 * a + j], (x, y, 1 - c)))
        return out

    def start(ins, outs, send, recv):
        for cp in copies(outs, send, recv):
            cp.start()

    def finish(ins, outs, send, recv):
        x, y, c, chips = _place()
        for a in range(n):
            for j, chip in enumerate(chips):
                passed = _half(outs[a].at[2 * chip[0] + chip[1]], 1 - c, cols[a])
                _remote(passed, passed, send.at[3 * a + j], recv.at[3 * a + j], (x, y, 1 - c)).wait_recv()
        for cp in copies(outs, send, recv):
            cp.wait_send()

    return Job(list(bufs), [_sds(b.shape, b.dtype) for b in bufs], True, 3 * n, start, finish)


def pair_exchange_job(arrs, cols=None):
    n = len(arrs)
    cols = _flags(cols, n)

    def copies(ins, outs, send, recv):
        x, y, c, _ = _place()
        return [_remote(_half(ins[a], 1 - c, cols[a], 1), outs[a], send.at[a], recv.at[a], (x, y, 1 - c))
                for a in range(n)]

    def start(ins, outs, send, recv):
        for cp in copies(ins, outs, send, recv):
            cp.start()

    def finish(ins, outs, send, recv):
        for cp in copies(ins, outs, send, recv):
            cp.wait()

    shape = lambda a, k: a.shape[:-1] + (a.shape[-1] // 2,) if k else a.shape[:1] + a.shape[2:]
    return Job(list(arrs), [_sds(shape(a, k), a.dtype) for a, k in zip(arrs, cols)], False, n, start, finish)


def chip_exchange_job(arrs):
    n = len(arrs)

    def copies(ins, outs, send, recv):
        x, y, c, chips = _place()
        return [_remote(ins[a].at[2 * chip[0] + chip[1]], outs[a].at[j], send.at[3 * a + j], recv.at[3 * a + j],
                        (*chip, c)) for a in range(n) for j, chip in enumerate(chips)]

    def start(ins, outs, send, recv):
        for cp in copies(ins, outs, send, recv):
            cp.start()

    def finish(ins, outs, send, recv):
        for cp in copies(ins, outs, send, recv):
            cp.wait()

    return Job(list(arrs), [_sds((3,) + a.shape[1:], a.dtype) for a in arrs], False, 3 * n, start, finish)


def chip_exchange_rows_job(p, start, size, into=None):
    rows = pl.ds(start, size)

    def copies(ins, outs, send, recv):
        x, y, c, chips = _place()
        return [_remote(ins[0].at[2 * chip[0] + chip[1], rows], outs[0].at[j, rows], send.at[j], recv.at[j], (*chip, c))
                for j, chip in enumerate(chips)]

    def start_(ins, outs, send, recv):
        for cp in copies(ins, outs, send, recv):
            cp.start()

    def finish(ins, outs, send, recv):
        for cp in copies(ins, outs, send, recv):
            cp.wait()

    ins = [p] if into is None else [p, into]
    return Job(ins, [_sds((3,) + p.shape[1:], p.dtype)], {} if into is None else {1: 0}, 3, start_, finish)


def pair_gather_job(bufs, cols=None):
    n = len(bufs)
    cols = _flags(cols, n)

    def copies(outs, send, recv):
        x, y, c, _ = _place()
        return [_remote(_half(outs[a], c, cols[a]), _half(outs[a], c, cols[a]), send.at[a], recv.at[a],
                        (x, y, 1 - c)) for a in range(n)]

    def start(ins, outs, send, recv):
        for cp in copies(outs, send, recv):
            cp.start()

    def finish(ins, outs, send, recv):
        for cp in copies(outs, send, recv):
            cp.wait()

    return Job(list(bufs), [_sds(b.shape, b.dtype) for b in bufs], True, n, start, finish)


def _call(body, *, grid, in_specs, out_specs, out_shape, name, sem, args, scratch_shapes=(), jobs=(), aliases=None):
    in_specs, out_specs, out_shape, scratch_shapes = list(in_specs), list(out_specs), list(out_shape), list(scratch_shapes)
    aliases = dict(aliases or {})
    n_in, n_out, n_scr = len(in_specs), len(out_specs), len(scratch_shapes)
    if jobs:
        sem = ("arbitrary",) * len(grid)
    at_in, at_out = n_in, n_out
    for j in jobs:
        pairs = j.aliased if isinstance(j.aliased, dict) else {i: i for i in range(len(j.ins))} if j.aliased else {}
        aliases.update({at_in + i: at_out + o for i, o in pairs.items()})
        at_in, at_out = at_in + len(j.ins), at_out + len(j.outs)

    def wrapped(*refs):
        ins, p = refs[:n_in], n_in
        jins = []
        for j in jobs:
            jins.append(refs[p:p + len(j.ins)])
            p += len(j.ins)
        outs, p = refs[p:p + n_out], p + n_out
        jouts = []
        for j in jobs:
            jouts.append(refs[p:p + len(j.outs)])
            p += len(j.outs)
        scr, sems = refs[p:p + n_scr], refs[p + n_scr:]

        def start():
            for k, j in enumerate(jobs):
                j.start(jins[k], jouts[k], sems[2 * k], sems[2 * k + 1])

        def finish():
            for k, j in enumerate(jobs):
                j.finish(jins[k], jouts[k], sems[2 * k], sems[2 * k + 1])

        if jobs and grid:
            ids = [pl.program_id(d) for d in range(len(grid))]
            pl.when(functools.reduce(jnp.logical_and, [i == 0 for i in ids]))(start)
            body(*ins, *outs, *scr)
            pl.when(functools.reduce(jnp.logical_and, [i == g - 1 for i, g in zip(ids, grid)]))(finish)
        else:
            start()
            body(*ins, *outs, *scr)
            finish()

    kwargs = dict(grid=grid) if grid else {}
    res = pl.pallas_call(
        wrapped, in_specs=in_specs + [_ANY] * (at_in - n_in), out_specs=out_specs + [_ANY] * (at_out - n_out),
        out_shape=out_shape + [o for j in jobs for o in j.outs],
        scratch_shapes=scratch_shapes + [pltpu.SemaphoreType.DMA((j.nsem,)) for j in jobs for _ in range(2)],
        input_output_aliases=aliases, name=name,
        compiler_params=pltpu.CompilerParams(dimension_semantics=sem, vmem_limit_bytes=VMEM_LIMIT) if grid
        else pltpu.CompilerParams(vmem_limit_bytes=VMEM_LIMIT), **kwargs)(*args, *[a for j in jobs for a in j.ins])
    res = list(res)
    outs, rest, per_job = res[:n_out], res[n_out:], []
    for j in jobs:
        per_job.append(rest[:len(j.outs)])
        rest = rest[len(j.outs):]
    return outs, per_job


def run_jobs(jobs, name):
    return _call(lambda: None, grid=(), in_specs=[], out_specs=[], out_shape=[], name=name, sem=(), args=(), jobs=jobs)[1]


def pair_add(g, r, core, out_dtype, cols=False, out_rows=None):
    S, R, C = r.shape
    T = _rows_tile(R, 2 * ROW_TILE if C <= 1024 else ROW_TILE if C <= 4096 else ROW_TILE // 2)

    def body(c_ref, g_ref, r_ref, o_ref):
        o_ref[0] = ((g_ref[0] if cols else g_ref[0, 0]).astype(F32) + r_ref[0].astype(F32)).astype(o_ref.dtype)

    g_spec = (pl.BlockSpec((1, T, C), lambda k, i, c_ref: (k, i, c_ref[0])) if cols
              else pl.BlockSpec((1, 1, T, C), lambda k, i, c_ref: (k, c_ref[0], i, 0)))
    return pl.pallas_call(
        body,
        grid_spec=pltpu.PrefetchScalarGridSpec(
            num_scalar_prefetch=1, grid=(S, R // T),
            in_specs=[g_spec, pl.BlockSpec((1, T, C), lambda k, i, c_ref: (k, i, 0))],
            out_specs=pl.BlockSpec((1, T, C), lambda k, i, c_ref: (k, i, 0))),
        out_shape=_sds((S, out_rows or R, C), out_dtype), name="pair_add",
        compiler_params=_params("parallel", "parallel"))(core, g, r)


def chip_add(p, r, chip, core, cols=False):
    _, R, C = p.shape
    T = _rows_tile(R, ROW_TILE)

    def body(k_ref, c_ref, p_ref, r0_ref, r1_ref, r2_ref, o_ref):
        s = ((p_ref[0].astype(F32) + r0_ref[0].astype(F32)) + r1_ref[0].astype(F32)) + r2_ref[0].astype(F32)
        if cols:
            o_ref[...] = s
        else:
            o_ref[0] = s

    slot = lambda j: pl.BlockSpec((1, T, C), lambda i, k_ref, c_ref: (j, i, 0))
    out_spec = (pl.BlockSpec((T, C), lambda i, k_ref, c_ref: (i, c_ref[0])) if cols
                else pl.BlockSpec((1, T, C), lambda i, k_ref, c_ref: (c_ref[0], i, 0)))
    return pl.pallas_call(
        body,
        grid_spec=pltpu.PrefetchScalarGridSpec(
            num_scalar_prefetch=2, grid=(R // T,),
            in_specs=[pl.BlockSpec((1, T, C), lambda i, k_ref, c_ref: (k_ref[0], i, 0)), slot(0), slot(1), slot(2)],
            out_specs=out_spec),
        out_shape=_sds((R, 2 * C) if cols else (2, R, C), F32), name="chip_add",
        compiler_params=_params("parallel"))(chip, core, p, r, r, r)


def allreduce_small(v):
    R = v.shape[0]

    def body(v_ref, o_ref, buf, send_sems, recv_sems, local_sem):
        x, y, c, chips = _place()
        me, sibling = (x, y, c), (x, y, 1 - c)

        def rows(px, py, pc):
            return buf.at[pl.ds((4 * px + 2 * py + pc) * R, R), :]

        def copy(k, block, to, src=None):
            return _remote(rows(*block) if src is None else src, rows(*block), send_sems.at[k], recv_sems.at[k], to)

        mine = pltpu.make_async_copy(v_ref, rows(*me), local_sem)
        mine.start()
        first = [copy(0, me, sibling, src=v_ref)]
        first += [copy(1 + j, me, (*chip, c), src=v_ref) for j, chip in enumerate(chips)]
        for cp in first:
            cp.start()
        passed = [copy(4 + j, (*chip, c), sibling) for j, chip in enumerate(chips)]
        for j, chip in enumerate(chips):
            copy(1 + j, (*chip, c), me).wait_recv()
            passed[j].start()
        copy(0, sibling, me).wait_recv()
        for j, chip in enumerate(chips):
            copy(4 + j, (*chip, 1 - c), me).wait_recv()
        for cp in first + passed:
            cp.wait_send()
        mine.wait()
        acc = buf[0:R, :]
        for d in range(1, 8):
            acc = acc + buf[d * R:(d + 1) * R, :]
        o_ref[...] = acc

    return pl.pallas_call(
        body, in_specs=[pl.BlockSpec(memory_space=pltpu.VMEM)], out_specs=pl.BlockSpec(memory_space=pltpu.VMEM),
        out_shape=_sds((R, LANES), F32),
        scratch_shapes=[pltpu.VMEM((8 * R, LANES), F32), pltpu.SemaphoreType.DMA((7,)),
                        pltpu.SemaphoreType.DMA((7,)), pltpu.SemaphoreType.DMA],
        name="allreduce_small", compiler_params=pltpu.CompilerParams(vmem_limit_bytes=VMEM_LIMIT))(v)


def matmul(a, b, mode, out_dtype, name, tm=512, tn=1024, tk=4608, jobs=(), n_out=None):
    if mode == "nn":
        (M, K), (K2, N) = a.shape, b.shape
    elif mode == "nt":
        (M, K), (N, K2) = a.shape, b.shape
        N = n_out or N
    else:
        (K, M), (K2, N) = a.shape, b.shape
    assert K == K2 or (mode == "nn" and K2 > K)
    tm, tn, tk = _pick(M, tm), _pick(N, tn), _pick(K, tk)
    nk = K // tk
    dn = {"nn": NN, "nt": NT, "tn": TN}[mode]

    def body(a_ref, b_ref, o_ref, *acc):
        part = _dot(a_ref[...].astype(BF16), b_ref[...].astype(BF16), dn)
        if nk == 1:
            o_ref[...] = part.astype(o_ref.dtype)
            return
        acc_ref, = acc
        k = pl.program_id(2)

        @pl.when(k == 0)
        def _():
            acc_ref[...] = part

        @pl.when(jnp.logical_and(k > 0, k < nk - 1))
        def _():
            acc_ref[...] += part

        @pl.when(k == nk - 1)
        def _():
            o_ref[...] = (acc_ref[...] + part).astype(o_ref.dtype)

    a_spec = (pl.BlockSpec((tk, tm), lambda i, j, k: (k, i)) if mode == "tn"
              else pl.BlockSpec((tm, tk), lambda i, j, k: (i, k)))
    b_spec = (pl.BlockSpec((tn, tk), lambda i, j, k: (j, k)) if mode == "nt"
              else pl.BlockSpec((tk, tn), lambda i, j, k: (k, j)))
    outs, per_job = _call(
        body, grid=(M // tm, N // tn, nk), in_specs=[a_spec, b_spec],
        out_specs=[pl.BlockSpec((tm, tn), lambda i, j, k: (i, j))], out_shape=[_sds((M, N), out_dtype)],
        scratch_shapes=[pltpu.VMEM((tm, tn), F32)] if nk > 1 else [], name=name,
        sem=("parallel", "parallel", "arbitrary"), args=(a, b), jobs=jobs)
    return outs[0], per_job


def rms_fwd(x, w):
    L, D = x.shape
    T = _pick(L, ROW_TILE)

    def body(x_ref, w_ref, h_ref):
        xv = x_ref[...]
        r = lax.rsqrt(jnp.mean(xv * xv, axis=-1, keepdims=True) + NORM_EPS)
        h_ref[...] = (xv * r * w_ref[...]).astype(h_ref.dtype)

    return pl.pallas_call(
        body, grid=(L // T,),
        in_specs=[pl.BlockSpec((T, D), lambda i: (i, 0)), pl.BlockSpec((1, D), lambda i: (0, 0))],
        out_specs=pl.BlockSpec((T, D), lambda i: (i, 0)),
        out_shape=_sds((L, D), BF16), name="rms_fwd", compiler_params=_params("parallel"))(x, w)


def post_fwd(x, o, w, next_w=None, jobs=()):
    L, D = x.shape
    T = _pick(L, ROW_TILE)

    def body(x_ref, o_ref, w_ref, *rest):
        ov = o_ref[...]
        r = lax.rsqrt(jnp.mean(ov * ov, axis=-1, keepdims=True) + NORM_EPS)
        y = x_ref[...] + ov * r * w_ref[...]
        rest[-1 if next_w is None else -2][...] = y
        if next_w is not None:
            r2 = lax.rsqrt(jnp.mean(y * y, axis=-1, keepdims=True) + NORM_EPS)
            rest[-1][...] = (y * r2 * rest[0][...]).astype(BF16)

    row = pl.BlockSpec((T, D), lambda i: (i, 0))
    vec = pl.BlockSpec((1, D), lambda i: (0, 0))
    more = [] if next_w is None else [next_w]
    outs, per_job = _call(
        body, grid=(L // T,), in_specs=[row, row, vec] + [vec] * len(more), out_specs=[row] * (1 + len(more)),
        out_shape=[_sds((L, D), F32)] + [_sds((L, D), BF16)] * len(more), name="post_fwd", sem=("parallel",),
        args=(x, o, w, *more), jobs=jobs)
    return (outs[0], outs[1] if more else None), per_job


def _rms_bwd_math(xv, w, dy):
    r = lax.rsqrt(jnp.mean(xv * xv, axis=-1, keepdims=True) + NORM_EPS)
    xhat = xv * r
    g = dy * w
    dx = r * (g - xhat * jnp.mean(g * xhat, axis=-1, keepdims=True))
    return dx, jnp.sum(dy * xhat, axis=0, keepdims=True)


def post_bwd(o, w, dxn):
    L, D = o.shape
    T = _pick(L, ROW_TILE)

    def body(o_ref, w_ref, d_ref, do_ref, dw_ref):
        dx, dw = _rms_bwd_math(o_ref[...], w_ref[...], d_ref[...])
        do_ref[...] = dx.astype(do_ref.dtype)

        @pl.when(pl.program_id(0) == 0)
        def _():
            dw_ref[...] = jnp.zeros_like(dw_ref)

        dw_ref[...] += dw

    row = pl.BlockSpec((T, D), lambda i: (i, 0))
    vec = pl.BlockSpec((1, D), lambda i: (0, 0))
    return pl.pallas_call(
        body, grid=(L // T,), in_specs=[row, vec, row], out_specs=[row, vec],
        out_shape=[_sds((L, D), BF16), _sds((1, D), F32)],
        name="post_bwd", compiler_params=_params("arbitrary"))(o, w, dxn)


def rms_bwd(x, w, dh_a, dh_b, dxn, below=None, jobs=()):
    L, D = x.shape
    T = _pick(L, ROW_TILE)

    def body(x_ref, w_ref, a_ref, b_ref, d_ref, *rest):
        dx, dw = _rms_bwd_math(x_ref[...], w_ref[...], a_ref[...] + b_ref[...])
        dx = d_ref[...] + dx
        outs = rest[2:] if below else rest
        outs[0][...] = dx

        @pl.when(pl.program_id(0) == 0)
        def _():
            for acc in outs[1::2]:
                acc[...] = jnp.zeros_like(acc)

        outs[1][...] += dw
        if below:
            do, dwp = _rms_bwd_math(rest[0][...], rest[1][...], dx)
            outs[2][...] = do.astype(BF16)
            outs[3][...] += dwp

    row = pl.BlockSpec((T, D), lambda i: (i, 0))
    vec = pl.BlockSpec((1, D), lambda i: (0, 0))
    more = list(below) if below else []
    return _call(
        body, grid=(L // T,), in_specs=[row, vec, row, row, row] + ([row, vec] if below else []),
        out_specs=[row, vec] + ([row, vec] if below else []),
        out_shape=[_sds((L, D), F32), _sds((1, D), F32)] + ([_sds((L, D), BF16), _sds((1, D), F32)] if below else []),
        name="rms_bwd", sem=("arbitrary",), args=(x, w, dh_a, dh_b, dxn, *more), jobs=jobs)


def loss_head(y, target):
    L, D = y.shape
    T = _pick(L, ROW_TILE)

    def body(y_ref, t_ref, d_ref, s_ref):
        e = y_ref[...] - t_ref[...]
        d_ref[...] = e * (1.0 / D)

        @pl.when(pl.program_id(0) == 0)
        def _():
            s_ref[...] = jnp.zeros_like(s_ref)

        s_ref[...] += jnp.sum(e * e)

    row = pl.BlockSpec((T, D), lambda i: (i, 0))
    return pl.pallas_call(
        body, grid=(L // T,), in_specs=[row, row],
        out_specs=[row, pl.BlockSpec((8, LANES), lambda i: (0, 0))],
        out_shape=[_sds((L, D), F32), _sds((8, LANES), F32)],
        name="loss_head", compiler_params=_params("arbitrary"))(y, target)


def _window_sums(xe, w, back):
    n = xe.shape[0]
    s, k = xe, 1
    while k < w:
        s = s + pltpu.roll(s, k if back else n - k, 0)
        k *= 2
    return s


def pool_fwd(proj, mixw, scale, D):
    L = proj.shape[0]
    PGW = D // GROUPS
    T = _pick(L, ROW_TILE)
    hb = T // POOL_HALO

    def body(u_ref, halo_ref, g_ref, mw_ref, sc_ref, y_ref, p_ref):
        i = pl.program_id(0)
        t1 = _row(i * T + 1, (T, 1))
        for g, w in enumerate(POOL_WINDOWS):
            sl = slice(g * PGW, (g + 1) * PGW)
            cnt = jnp.minimum(t1, w).astype(F32)
            for c0 in range(g * PGW, (g + 1) * PGW, LANES):
                cl = slice(c0, c0 + LANES)
                u = u_ref[:, cl]
                xe = jnp.concatenate([jnp.where(i > 0, halo_ref[:, cl], 0.0), u], axis=0)
                p_ref[:, cl] = (_window_sums(xe, w, True)[POOL_HALO:, :] / cnt - u).astype(BF16)
            mixed = _dot(p_ref[:, sl], mw_ref[g])
            for c0 in range(0, PGW, LANES):
                cl = slice(g * PGW + c0, g * PGW + c0 + LANES)
                gate = g_ref[:, cl]
                y_ref[:, cl] = (mixed[:, c0:c0 + LANES] * sc_ref[:, cl] * (gate * _sigmoid(gate))).astype(BF16)

    return pl.pallas_call(
        body, grid=(L // T,),
        in_specs=[pl.BlockSpec((T, D), lambda i: (i, 0)),
                  pl.BlockSpec((POOL_HALO, D), lambda i: (jnp.maximum(i * hb - 1, 0), 0)),
                  pl.BlockSpec((T, D), lambda i: (i, 1)),
                  pl.BlockSpec((GROUPS, PGW, PGW), lambda i: (0, 0, 0)),
                  pl.BlockSpec((1, D), lambda i: (0, 0))],
        out_specs=[pl.BlockSpec((T, D), lambda i: (i, 0)), pl.BlockSpec((T, D), lambda i: (i, 0))],
        out_shape=[_sds((L, 2 * D), BF16), _sds((L, D), BF16)],
        name="pool_fwd", compiler_params=_params("parallel"))(proj, proj, proj, mixw, scale)


def pool_bwd(proj, dmix, pooled, mixw, scale, dproj, D):
    L = proj.shape[0]
    PGW = D // GROUPS
    T = _pick(L, ROW_TILE)
    hb = T // POOL_HALO
    nT = L // T

    def body(g_ref, gh_ref, dy_ref, dyh_ref, p_ref, mw_ref, sc_ref, old_ref, dp_ref, dm_ref, ds_ref):
        i = pl.program_id(0)
        t1 = _row(i * T + 1, (T, 1))
        th1 = _row((i + 1) * T + 1, (POOL_HALO, 1))
        live = i < nT - 1

        @pl.when(i == 0)
        def _():
            ds_ref[...] = jnp.zeros_like(ds_ref)
            dm_ref[...] = jnp.zeros_like(dm_ref)

        for g, w in enumerate(POOL_WINDOWS):
            sl = slice(g * PGW, (g + 1) * PGW)
            sc = sc_ref[:, sl]
            gate, dy = g_ref[:, sl], dy_ref[:, sl]
            sg, dsg = _silu_and_grad(gate)
            pooled = p_ref[:, sl]
            mixed = _dot(pooled, mw_ref[g])
            dmixed = (dy * sc * sg).astype(BF16)
            dm_ref[g] += _dot(pooled, dmixed, TN)
            dp_ref[:, D + g * PGW:D + (g + 1) * PGW] = (dy * mixed * sc * dsg).astype(BF16)
            ds_ref[:, sl] += jnp.sum(dy * mixed * sg, axis=0, keepdims=True)
            dpool = _dot(dmixed, mw_ref[g], NT)
            gate_h = gh_ref[:, sl]
            dmixed_h = (dyh_ref[:, sl] * sc * (gate_h * _sigmoid(gate_h))).astype(BF16)
            dpool_h = jnp.where(live, _dot(dmixed_h, mw_ref[g], NT), 0.0)
            cnt, cnt_h = jnp.minimum(t1, w).astype(F32), jnp.minimum(th1, w).astype(F32)
            for c0 in range(0, PGW, LANES):
                lc = slice(c0, c0 + LANES)
                dpc = dpool[:, lc]
                qe = jnp.concatenate([dpc / cnt, dpool_h[:, lc] / cnt_h], axis=0)
                dp_ref[:, g * PGW + c0:g * PGW + c0 + LANES] = (_window_sums(qe, w, False)[:T, :] - dpc).astype(BF16)

    nxt = lambda i: jnp.minimum((i + 1) * hb, L // POOL_HALO - 1)
    row = lambda c: pl.BlockSpec((T, D), lambda i: (i, c))
    return pl.pallas_call(
        body, grid=(nT,),
        in_specs=[row(1), pl.BlockSpec((POOL_HALO, D), lambda i: (nxt(i), 1)),
                  row(0), pl.BlockSpec((POOL_HALO, D), lambda i: (nxt(i), 0)),
                  row(0), pl.BlockSpec((GROUPS, PGW, PGW), lambda i: (0, 0, 0)),
                  pl.BlockSpec((1, D), lambda i: (0, 0)), _ANY],
        out_specs=[pl.BlockSpec((T, 2 * D), lambda i: (i, 0)), pl.BlockSpec((GROUPS, PGW, PGW), lambda i: (0, 0, 0)),
                   pl.BlockSpec((1, D), lambda i: (0, 0))],
        out_shape=[_sds(dproj.shape, dproj.dtype), _sds((GROUPS, PGW, PGW), F32), _sds((1, D), F32)],
        input_output_aliases={7: 0},
        name="pool_bwd", compiler_params=_params("arbitrary"))(proj, proj, dmix, dmix, pooled, mixw, scale, dproj)


def conv_fwd(proj, cw, cb, D):
    L = proj.shape[0]
    C = cw.shape[1]
    assert (3 * D) % C == 0
    cblk = (3 * D) // C
    T = _pick(L, ROW_TILE)
    hb = T // CONV_HALO

    def body(u_ref, halo_ref, w_ref, b_ref, o_ref):
        i = pl.program_id(0)
        for c0 in range(0, C, LANES):
            sl = slice(c0, c0 + LANES)
            u = u_ref[:, sl]
            xe = jnp.concatenate([jnp.where(i > 0, halo_ref[:, sl], 0.0), u], axis=0)
            acc = b_ref[:, sl] + w_ref[CONV_K - 1:CONV_K, sl] * u
            for k in range(CONV_K - 1):
                acc = acc + w_ref[k:k + 1, sl] * pltpu.roll(xe, CONV_K - 1 - k, 0)[CONV_HALO:, :]
            o_ref[:, sl] = acc

    return pl.pallas_call(
        body, grid=(L // T,),
        in_specs=[pl.BlockSpec((T, C), lambda i: (i, cblk)),
                  pl.BlockSpec((CONV_HALO, C), lambda i: (jnp.maximum(i * hb - 1, 0), cblk)),
                  pl.BlockSpec((CONV_K, C), lambda i: (0, 0)),
                  pl.BlockSpec((1, C), lambda i: (0, 0))],
        out_specs=pl.BlockSpec((T, C), lambda i: (i, 0)),
        out_shape=_sds((L, C), F32), name="conv_fwd", compiler_params=_params("parallel"))(proj, proj, cw, cb)


def conv_bwd(dparts, proj, cw, dproj, D, jobs=()):
    L = proj.shape[0]
    C = cw.shape[1]
    cblk = (3 * D) // C
    T = _pick(L, ROW_TILE)
    hb = T // CONV_HALO
    nT = L // T
    widths = [p.shape[1] for p in dparts]
    assert sum(widths) == C
    n = len(dparts)

    def body(*refs):
        d_refs, dn_refs = refs[:n], refs[n:2 * n]
        u_ref, w_ref, old_ref, dr_ref, dw_ref, db_ref = refs[2 * n:]
        i = pl.program_id(0)

        @pl.when(i == 0)
        def _():
            dw_ref[...] = jnp.zeros_like(dw_ref)
            db_ref[...] = jnp.zeros_like(db_ref)

        at = 0
        for d_ref, dn_ref, wd in zip(d_refs, dn_refs, widths):
            for c0 in range(0, wd, LANES):
                ls, sl = slice(c0, c0 + LANES), slice(at + c0, at + c0 + LANES)
                d = d_ref[:, ls]
                u = u_ref[:, sl]
                de = jnp.concatenate([d, jnp.where(i < nT - 1, dn_ref[:, ls], 0.0)], axis=0)
                acc = w_ref[CONV_K - 1:CONV_K, sl] * d
                dw_ref[CONV_K - 1:CONV_K, sl] += jnp.sum(d * u, axis=0, keepdims=True)
                for k in range(CONV_K - 1):
                    sh = CONV_K - 1 - k
                    ds = pltpu.roll(de, T + CONV_HALO - sh, 0)[:T, :]
                    acc = acc + w_ref[k:k + 1, sl] * ds
                    dw_ref[k:k + 1, sl] += jnp.sum(ds * u, axis=0, keepdims=True)
                dr_ref[:, sl] = acc.astype(dr_ref.dtype)
                db_ref[:, sl] += jnp.sum(d, axis=0, keepdims=True)
            at += wd

    nxt = lambda i: jnp.minimum((i + 1) * hb, L // CONV_HALO - 1)
    return _call(
        body, grid=(nT,),
        in_specs=[pl.BlockSpec((T, wd), lambda i: (i, 0)) for wd in widths]
        + [pl.BlockSpec((CONV_HALO, wd), lambda i: (nxt(i), 0)) for wd in widths]
        + [pl.BlockSpec((T, C), lambda i: (i, cblk)), pl.BlockSpec((CONV_K, C), lambda i: (0, 0)), _ANY],
        out_specs=[pl.BlockSpec((T, C), lambda i: (i, cblk)),
                   pl.BlockSpec((8, C), lambda i: (0, 0)),
                   pl.BlockSpec((1, C), lambda i: (0, 0))],
        out_shape=[_sds(dproj.shape, dproj.dtype), _sds((8, C), F32), _sds((1, C), F32)],
        aliases={2 * n + 2: 0}, name="conv_bwd", sem=("arbitrary",),
        args=(*dparts, *dparts, proj, cw, dproj), jobs=jobs)


def _softplus(v):
    y = jnp.exp(-jnp.abs(v))
    u = 1.0 + y
    log1p = jnp.where(u == 1.0, y, jnp.log(u) * y / jnp.where(u == 1.0, 1.0, u - 1.0))
    return jnp.maximum(v, 0.0) + log1p


def dt_prep(dtraw, bias, alog, expand, D, jobs=()):
    L = dtraw.shape[0]
    GC = D // GROUPS
    HPG = GC // HEAD_DIM
    Q = _pick(L, SCAN_CHUNK)
    nc = L // Q

    def body(r_ref, b_ref, a_ref, e_ref, dt_ref, acs_ref, acst_ref, ex_ref, cdx_ref):
        valid = lax.broadcasted_iota(jnp.int32, (1, LANES), 1) < HPG
        tril = (_row(0, (Q, Q)) >= lax.broadcasted_iota(jnp.int32, (Q, Q), 1)).astype(BF16)
        e = e_ref[...]
        for g in range(GROUPS):
            hs, cs = slice(g * LANES, (g + 1) * LANES), slice(g * GC, (g + 1) * GC)
            dt = jnp.where(valid, _softplus(r_ref[:, hs] + b_ref[:, hs]), 0.0)
            adt = dt * -jnp.exp(a_ref[:, hs])
            acs = _sel_dot(tril, adt)
            last = acs[Q - 1:Q, :]
            dt_ref[:, hs] = dt
            acs_ref[:, hs] = acs
            acst_ref[hs, :] = acs.T
            ex_ref[0, :, cs] = _dot_sel(dt, e, 2)
            ex_ref[1, :, cs] = jnp.exp(_dot_sel(acs, e, 2))
            ex_ref[2, :, cs] = jnp.exp(_dot_sel(last - acs, e, 2))
            cdx_ref[0, :, cs] = jnp.exp(_dot_sel(jnp.broadcast_to(last, (8, LANES)), e, 2))

    head = pl.BlockSpec((Q, GROUPS * LANES), lambda c: (c, 0))
    hvec = pl.BlockSpec((1, GROUPS * LANES), lambda c: (0, 0))
    return _call(
        body, grid=(nc,),
        in_specs=[head, hvec, hvec, pl.BlockSpec((LANES, GC), lambda c: (0, 0))],
        out_specs=[head, head, pl.BlockSpec((GROUPS * LANES, Q), lambda c: (0, c)),
                   pl.BlockSpec((3, Q, D), lambda c: (0, c, 0)), pl.BlockSpec((1, 8, D), lambda c: (c, 0, 0))],
        out_shape=[_sds((L, GROUPS * LANES), F32), _sds((L, GROUPS * LANES), F32), _sds((GROUPS * LANES, L), F32),
                   _sds((3, L, D), F32), _sds((nc, 8, D), F32)],
        name="dt_prep", sem=("parallel",), args=(dtraw, bias, alog, expand), jobs=jobs)


def _scan_specs(L, D, Q, rev):
    GC = D // GROUPS
    nc = L // Q
    ci = (lambda c: nc - 1 - c) if rev else (lambda c: c)
    return dict(
        xs=pl.BlockSpec((Q, GC), lambda g, c: (ci(c), g)),
        b=pl.BlockSpec((Q, STATE), lambda g, c: (ci(c), D // STATE + g)),
        c=pl.BlockSpec((Q, STATE), lambda g, c: (ci(c), D // STATE + GROUPS + g)),
        chan=pl.BlockSpec((Q, GC), lambda g, c: (ci(c), g)),
        ex=pl.BlockSpec((3, Q, GC), lambda g, c: (0, ci(c), g)),
        cdx=pl.BlockSpec((1, 8, GC), lambda g, c: (ci(c), 0, g)),
        head=pl.BlockSpec((Q, LANES), lambda g, c: (ci(c), g)),
        headt=pl.BlockSpec((LANES, Q), lambda g, c: (g, ci(c))),
        state=pl.BlockSpec((1, 1, STATE, GC), lambda g, c: (ci(c), g, 0, 0)),
        hvec=pl.BlockSpec((1, LANES), lambda g, c: (0, g)),
        cvec=pl.BlockSpec((1, GC), lambda g, c: (0, g)))


def scan_fwd(pre, ex, cdx, acs, acst, proj, dexp, nw, mixed, D, jobs=()):
    L = pre.shape[0]
    GC = D // GROUPS
    Q = _pick(L, SCAN_CHUNK)
    nc = L // Q
    sp = _scan_specs(L, D, Q, False)

    def body(xs_ref, b_ref, c_ref, ex_ref, cdx_ref, acs_ref, acst_ref, z_ref, de_ref, nw_ref,
             old_ref, y_ref, st_ref, o_ref, s_scr):
        @pl.when(pl.program_id(1) == 0)
        def _():
            s_scr[...] = jnp.zeros_like(s_scr)

        tri = _row(0, (Q, Q)) >= lax.broadcasted_iota(jnp.int32, (Q, Q), 1)
        half = lax.broadcasted_iota(jnp.int32, (1, LANES), 1) // HEAD_DIM
        xs, _ = _silu_and_grad(xs_ref[...])
        bg = _silu_and_grad(b_ref[...])[0].astype(BF16)
        cg = _silu_and_grad(c_ref[...])[0].astype(BF16)
        xdt = xs * ex_ref[0]
        sprev = s_scr[...]
        st_ref[0, 0] = sprev
        sc = _dot(cg, bg, NT)
        yoff = _dot(cg, sprev.astype(BF16)) * ex_ref[1]
        for j in range(GC // LANES):
            ps = slice(j * LANES, (j + 1) * LANES)
            xp = xdt[:, ps]
            acc = yoff[:, ps]
            for hh in range(2):
                h = 2 * j + hh
                lm = jnp.exp(jnp.where(tri, acs_ref[:, h:h + 1] - acst_ref[h:h + 1, :], -1e30))
                xm = jnp.where(half == hh, xp, 0.0).astype(BF16)
                acc = acc + _dot((sc * lm).astype(BF16), xm)
            y_ref[:, ps] = acc
        xw = (xdt * ex_ref[2]).astype(BF16)
        s_scr[...] = cdx_ref[0, 0:1, :] * sprev + _dot(bg, xw, TN)
        z = z_ref[...]
        y3 = (y_ref[...] + de_ref[...] * xs) * (z * _sigmoid(z))
        r = lax.rsqrt(jnp.mean(y3 * y3, axis=-1, keepdims=True) + NORM_EPS)
        o_ref[...] = (y3 * r * nw_ref[...]).astype(o_ref.dtype)

    return _call(
        body, grid=(GROUPS, nc),
        in_specs=[sp["xs"], sp["b"], sp["c"], sp["ex"], sp["cdx"], sp["head"], sp["headt"],
                  pl.BlockSpec((Q, GC), lambda g, c: (c, 2 * GROUPS + g)), sp["cvec"], sp["cvec"], _ANY],
        out_specs=[sp["chan"], sp["state"], pl.BlockSpec((Q, GC), lambda g, c: (c, GROUPS + g))],
        out_shape=[_sds((L, D), F32), _sds((nc, GROUPS, STATE, GC), F32), _sds(mixed.shape, mixed.dtype)],
        aliases={10: 2}, scratch_shapes=[pltpu.VMEM((STATE, GC), F32)], name="scan_fwd",
        sem=("parallel", "arbitrary"), args=(pre, pre, pre, ex, cdx, acs, acst, proj, dexp, nw, mixed),
        jobs=jobs)


def scan_bwd(pre, ex, cdx, acs, acst, dt, dtraw, bias, alog, states, y, proj, dmix, nw, dexp, collapse, D,
             jobs=()):
    L = pre.shape[0]
    GC = D // GROUPS
    Q = _pick(L, SCAN_CHUNK)
    nc = L // Q
    sp = _scan_specs(L, D, Q, True)
    rc = lambda c: nc - 1 - c

    def body(xs_ref, b_ref, c_ref, ex_ref, cdx_ref, acs_ref, acst_ref, dt_ref, raw_ref,
             bias_ref, alog_ref, st_ref, y_ref, z_ref, dm_ref, nw_ref, dexp_ref, col_ref,
             dxs_ref, db_ref, dc_ref, ddt_ref, dal_ref, dbi_ref, dz_ref, dnw_ref, dde_ref, ds_scr, dx_scr):
        first = pl.program_id(1) == 0

        @pl.when(first)
        def _():
            ds_scr[...] = jnp.zeros_like(ds_scr)
            dal_ref[...] = jnp.zeros_like(dal_ref)
            dbi_ref[...] = jnp.zeros_like(dbi_ref)
            dnw_ref[...] = jnp.zeros_like(dnw_ref)
            dde_ref[...] = jnp.zeros_like(dde_ref)

        li = _row(0, (Q, Q))
        si = lax.broadcasted_iota(jnp.int32, (Q, Q), 1)
        lane = lax.broadcasted_iota(jnp.int32, (1, LANES), 1)
        half = lane // HEAD_DIM
        xs_pre, b_pre, c_pre = xs_ref[...], b_ref[...], c_ref[...]
        xs, xs_g = _silu_and_grad(xs_pre)
        bf, b_g = _silu_and_grad(b_pre)
        cf, c_g = _silu_and_grad(c_pre)
        bg, cg = bf.astype(BF16), cf.astype(BF16)
        dtx, eax, dsx = ex_ref[0], ex_ref[1], ex_ref[2]
        cd = cdx_ref[0, 0:1, :]
        xdt = xs * dtx
        sz, dsz = _silu_and_grad(z_ref[...])
        y2 = y_ref[...] + dexp_ref[...] * xs
        y3 = y2 * sz
        r = lax.rsqrt(jnp.mean(y3 * y3, axis=-1, keepdims=True) + NORM_EPS)
        n = y3 * r
        dm = dm_ref[...]
        gg = dm * nw_ref[...]
        dy3 = r * (gg - n * jnp.mean(gg * n, axis=-1, keepdims=True))
        dnw_ref[0:1, :] += jnp.sum(dm * n, axis=0, keepdims=True)
        G = dy3 * sz
        dz_ref[...] = (dy3 * y2 * dsz).astype(dz_ref.dtype)
        dde_ref[0:1, :] += jnp.sum(G * xs, axis=0, keepdims=True)
        prev = st_ref[0, 0]
        dsn = ds_scr[...]
        prev_b, dsn_b = prev.astype(BF16), dsn.astype(BF16)
        cp = _dot(cg, prev_b)
        ge_b = (G * eax).astype(BF16)
        d_c = _dot(ge_b, prev_b, NT)
        dprev = _dot(cg, ge_b, TN) + cd * dsn
        chan_a = G * cp * eax
        xw_b = (xdt * dsx).astype(BF16)
        dcd = jnp.sum(prev * dsn, axis=0, keepdims=True)
        d_b = _dot(xw_b, dsn_b, NT)
        dxw = _dot(bg, dsn_b)
        dd = dxw * xdt * dsx
        chan_a = chan_a - dd
        last_c = jnp.sum(dd, axis=0, keepdims=True) + dcd * cd
        sc = _dot(cg, bg, NT)
        head_row = _row(0, (LANES, 1))
        dsc = jnp.zeros((Q, Q), F32)
        dacs = jnp.zeros((Q, LANES), F32)
        colsums = jnp.zeros((LANES, Q), F32)
        for j in range(GC // LANES):
            ps = slice(j * LANES, (j + 1) * LANES)
            xp, gp = xdt[:, ps], G[:, ps]
            dxp = dxw[:, ps] * dsx[:, ps]
            for hh in range(2):
                h = 2 * j + hh
                lm = jnp.exp(jnp.where(li >= si, acs_ref[:, h:h + 1] - acst_ref[h:h + 1, :], -1e30))
                m = sc * lm
                xm = jnp.where(half == hh, xp, 0.0).astype(BF16)
                gm = jnp.where(half == hh, gp, 0.0).astype(BF16)
                dm = _dot(gm, xm, NT)
                dxp = dxp + _dot(m.astype(BF16), gm, TN)
                dsc = dsc + dm * lm
                w = dm * m
                dacs = dacs + jnp.where(lane == h, jnp.sum(w, axis=1, keepdims=True), 0.0)
                colsums = jnp.where(head_row == h, jnp.sum(w, axis=0, keepdims=True), colsums)
            dx_scr[:, ps] = dxp
        dacs = dacs - colsums.T
        dsc_b = dsc.astype(BF16)
        d_c = d_c + _dot(dsc_b, bg)
        d_b = d_b + _dot(dsc_b, cg, TN)
        ds_scr[...] = dprev
        dxdt = dx_scr[...]
        dxs_ref[...] = (dxdt * dtx + dexp_ref[...] * G) * xs_g
        db_ref[...] = d_b * b_g
        dc_ref[...] = d_c * c_g
        colm = col_ref[...]
        dacs = dacs + _dot_sel(chan_a, colm)
        dlast = _dot_sel(jnp.broadcast_to(last_c, (8, GC)), colm)[0:1, :]
        dacs = dacs + jnp.where(_row(0, (Q, 1)) == Q - 1, dlast, 0.0)
        dadt = _sel_dot((si >= li).astype(BF16), dacs)
        a = -jnp.exp(alog_ref[...])
        dt = dt_ref[...]
        ddt = dadt * a + _dot_sel(dxdt * xs, colm, 2)
        dal_ref[0:1, :] += jnp.sum(dadt * dt * a, axis=0, keepdims=True)
        draw = ddt * _sigmoid(raw_ref[...] + bias_ref[...])
        dbi_ref[0:1, :] += jnp.sum(draw, axis=0, keepdims=True)
        ddt_ref[...] = draw.astype(ddt_ref.dtype)

    acc = pl.BlockSpec((8, LANES), lambda g, c: (0, g))
    cacc = pl.BlockSpec((8, GC), lambda g, c: (0, g))
    return _call(
        body, grid=(GROUPS, nc),
        in_specs=[sp["xs"], sp["b"], sp["c"], sp["ex"], sp["cdx"], sp["head"], sp["headt"],
                  sp["head"], sp["head"], sp["hvec"], sp["hvec"], sp["state"], sp["chan"],
                  pl.BlockSpec((Q, GC), lambda g, c: (rc(c), 2 * GROUPS + g)),
                  pl.BlockSpec((Q, GC), lambda g, c: (rc(c), GROUPS + g)), sp["cvec"], sp["cvec"],
                  pl.BlockSpec((GC, LANES), lambda g, c: (0, 0))],
        out_specs=[sp["chan"],
                   pl.BlockSpec((Q, STATE), lambda g, c: (rc(c), g)),
                   pl.BlockSpec((Q, STATE), lambda g, c: (rc(c), g)),
                   sp["head"], acc, acc,
                   pl.BlockSpec((Q, GC), lambda g, c: (rc(c), 2 * GROUPS + g)), cacc, cacc],
        out_shape=[_sds((L, D), F32), _sds((L, GROUPS * STATE), F32), _sds((L, GROUPS * STATE), F32),
                   _sds((L, GROUPS * LANES), BF16), _sds((8, GROUPS * LANES), F32), _sds((8, GROUPS * LANES), F32),
                   _sds((L, proj.shape[1]), BF16), _sds((8, D), F32), _sds((8, D), F32)],
        scratch_shapes=[pltpu.VMEM((STATE, GC), F32), pltpu.VMEM((Q, GC), F32)], name="scan_bwd",
        sem=("parallel", "arbitrary"),
        args=(pre, pre, pre, ex, cdx, acs, acst, dt, dtraw, bias, alog, states, y, proj, dmix, nw, dexp,
              collapse),
        jobs=jobs)


def _adam_math(gv, w, m, v):
    c1 = 1.0 - ADAM_B1 ** ADAM_STEP
    c2 = 1.0 - ADAM_B2 ** ADAM_STEP
    nm = ADAM_B1 * m + (1.0 - ADAM_B1) * gv
    nv = ADAM_B2 * v + (1.0 - ADAM_B2) * (gv * gv)
    return -ADAM_LR * ((nm / c1) / (jnp.sqrt(nv / c2) + ADAM_EPS) + ADAM_WD * w), nm, nv


def adamw(g, w, m, v, name):
    R, C = g.shape
    T = R if R <= 128 else 128
    assert R % T == 0

    def body(g_ref, w_ref, m_ref, v_ref, d_ref, nm_ref, nv_ref):
        d_ref[...], nm_ref[...], nv_ref[...] = _adam_math(g_ref[...], w_ref[...], m_ref[...], v_ref[...])

    blk = pl.BlockSpec((T, C), lambda i: (i, 0))
    return pl.pallas_call(
        body, grid=(R // T,), in_specs=[blk] * 4, out_specs=[blk] * 3,
        out_shape=[_sds((R, C), F32)] * 3, name=name, compiler_params=_params("parallel"))(g, w, m, v)


def adamw_layer(g, w, m, v, layer, prev, name):
    R, C = g.shape
    T = _rows_tile(R, ROW_TILE)
    assert R % T == 0

    def body(g_ref, w_ref, m_ref, v_ref, *rest):
        go_ref, d_ref, nm_ref, nv_ref = rest[-4:]
        gv = g_ref[...]
        go_ref[0] = gv
        d_ref[0], nm_ref[0], nv_ref[0] = _adam_math(gv, w_ref[0], m_ref[0], v_ref[0])

    mine = pl.BlockSpec((1, T, C), lambda i: (layer, i, 0))
    prev = list(prev or [])
    return pl.pallas_call(
        body, grid=(R // T,), in_specs=[pl.BlockSpec((T, C), lambda i: (i, 0)), mine, mine, mine] + [_ANY] * len(prev),
        out_specs=[mine] * 4, out_shape=[_sds(w.shape, F32)] * 4,
        input_output_aliases={4 + i: i for i in range(len(prev))}, name=name,
        compiler_params=_params("parallel"))(g, w, m, v, *prev)


def cast_bf16(a, layer):
    _, R, C = a.shape
    T = _rows_tile(R, ROW_TILE)

    def body(a_ref, o_ref):
        o_ref[...] = a_ref[0].astype(BF16)

    return pl.pallas_call(
        body, grid=(R // T,), in_specs=[pl.BlockSpec((1, T, C), lambda i: (layer, i, 0))],
        out_specs=pl.BlockSpec((T, C), lambda i: (i, 0)), out_shape=_sds((R, C), BF16),
        name="cast_bf16", compiler_params=_params("parallel"))(a)


def _to_groups(v, hpg):
    lead = v.shape[:-1]
    t = v.reshape(lead + (GROUPS, hpg))
    t = jnp.pad(t, [(0, 0)] * (len(lead) + 1) + [(0, LANES - hpg)])
    return t.reshape(lead + (GROUPS * LANES,))


def _from_groups(a, hpg):
    lead = a.shape[:-1]
    return a.reshape(lead + (GROUPS, LANES))[..., :hpg].reshape(lead + (GROUPS * hpg,))


def _expand_matrix(D):
    gc = D // GROUPS
    return (jnp.arange(LANES)[:, None] == (jnp.arange(gc)[None, :] // HEAD_DIM)).astype(BF16)


def layer_params(pre_w, w_in_t, mixw, scale, cw, cb, bias, alog, dskip, nw, w_out_full, post_w, D):
    hpg = D // GROUPS // HEAD_DIM
    main = w_in_t.shape[0] - GROUPS * hpg
    wdt_t = _to_groups(w_in_t[main:].T, hpg).T
    return dict(
        pre_w=pre_w[None], win_t=w_in_t, main=main, wdt_t=wdt_t, mixw=mixw,
        scale=scale[None], cw=cw, cb=cb[None], bias=_to_groups(bias, hpg)[None], alog=_to_groups(alog, hpg)[None],
        dexp=jnp.repeat(dskip, HEAD_DIM)[None], nw=nw[None], wout=w_out_full, post_w=post_w[None])


def layer_fwd(x, p, D, next_shards=None, h=None, next_pre_w=None, own_rest=None):
    nxt = bool(next_shards)
    parts = lambda a: ((a.shape[0] // 32) * 16, a.shape[0] - (a.shape[0] // 32) * 16)
    if nxt:
        s_in, s_out, s_mix, s_cw = next_shards
        (in_a, in_b), (out_a, out_b) = parts(s_in), parts(s_out)
    if h is None:
        h = rms_fwd(x, p["pre_w"])
    own = own_rest is not None
    proj, got = matmul(h, p["win_t"], "nt", F32, "proj", tm=1024, n_out=p["main"],
                       jobs=([gather_rows_job(s_in, 0, in_a)] if nxt else [])
                       + ([gather_send_job(own_rest[0], own_rest[1])] if own else []))
    g_in = got[0][0] if nxt else None
    dtraw, got = matmul(h, p["wdt_t"], "nt", F32, "dtproj",
                        jobs=[gather_pass_job(got[-1], own_rest[1])] if own else [])
    if own:
        p.update(own_rest[2](got[0]))
    mixed, pooled = pool_fwd(proj, p["mixw"], p["scale"], D)
    pre = conv_fwd(proj, p["cw"], p["cb"], D)
    dtp, got = dt_prep(dtraw, p["bias"], p["alog"], _expand_matrix(D), D,
                       jobs=[gather_rows_job(s_out, 0, out_a)] if nxt else [])
    g_out = got[0][0] if nxt else None
    dt, acs, acst, ex, cdx = dtp
    (y, states, mixed), got = scan_fwd(pre, ex, cdx, acs, acst, proj, p["dexp"], p["nw"], mixed, D,
                                       jobs=[gather_rows_job(s_in, in_a, in_b, g_in)] if nxt else [])
    g_in = got[0][0] if nxt else None
    out, got = matmul(mixed, p["wout"], "nn", F32, "outproj", tm=1024,
                      jobs=[gather_rows_job(s_out, out_a, out_b, g_out), gather_send_job([s_mix, s_cw]),
                            gather_pass_job([g_in], [True])] if nxt else [])
    (xn, h_next), got2 = post_fwd(x, out, p["post_w"], next_pre_w,
                                  jobs=[gather_pass_job(got[0] + got[1], [True, False, False])] if nxt else [])
    gathered = got[2] + got2[0] if nxt else None
    return xn, dict(x=x, h=h, proj=proj, dtraw=dtraw, pooled=pooled, pre=pre, dtp=dtp, y=y, states=states,
                    mixed=mixed, out=out), gathered, h_next


def layer_bwd(dxn, p, s, D, where=None, post=None, below=None, pending=None):
    reduce = where is not None
    chip, core = where if reduce else (None, None)
    hpg = D // GROUPS // HEAD_DIM
    PGW = D // GROUPS
    main = p["main"]
    SH = (main + GROUPS * hpg) // 4
    dt, acs, acst, ex, cdx = s["dtp"]
    dout, d_post = post if post else post_bwd(s["out"], p["post_w"], dxn)
    part = BF16 if reduce else F32
    d_wout, got = matmul(s["mixed"], dout, "tn", part, "dwout", tm=1024,
                         jobs=[pair_gather_job(pending, [True, False])] if pending else [])
    above = dict(w_in=got[0][0], pool_mix_w=got[0][1]) if pending else None
    g_out = d_wout.reshape(4, 2, D // 4, D)
    dmix, got = matmul(dout, p["wout"], "nt", F32, "dmixed", tm=1024,
                       jobs=[pair_exchange_job([g_out])] if reduce else [])
    pair_out = pair_add(g_out, got[0][0], core, BF16) if reduce else None
    (dxs, db, dc, ddtraw, d_alog, d_bias, dproj, d_nw, d_dexp), got = scan_bwd(
        s["pre"], ex, cdx, acs, acst, dt, s["dtraw"], p["bias"], p["alog"], s["states"], s["y"],
        s["proj"], dmix, p["nw"], p["dexp"], _expand_matrix(D).T, D,
        jobs=[chip_exchange_job([pair_out])] if reduce else [])
    mine_out = chip_add(pair_out, got[0][0], chip, core) if reduce else None
    (dproj, d_cw, d_cb), got = conv_bwd([dxs, db, dc], s["proj"], p["cw"], dproj, D,
                                        jobs=[pair_gather_job([mine_out])] if reduce else [])
    r_out = got[0][0] if reduce else None
    dproj, d_mixw, d_scale = pool_bwd(s["proj"], dmix, s["pooled"], p["mixw"], p["scale"], dproj, D)
    d_wmain_t, _ = matmul(dproj, s["h"], "tn", part, "dwmain", tm=1024)
    d_wdt_t, _ = matmul(ddtraw, s["h"], "tn", part, "dwdt")
    late = [d_wmain_t[None], d_wdt_t[None],
            d_mixw.reshape(GROUPS, 4, PGW // 4, PGW).transpose(1, 0, 2, 3).reshape(4, 2, GROUPS * PGW // 8, PGW)]
    cols = [True, True, False]
    dh_b, got = matmul(ddtraw, p["wdt_t"], "nn", F32, "dh_dt",
                       jobs=[pair_exchange_job(late, cols)] if reduce else [])
    if reduce:
        p_main, p_dt, p_mix = [pair_add(g, r, core, BF16, k, n) for g, r, k, n in
                               zip(late, got[0], cols, [4 * SH, None, None])]
        p_in = lax.dynamic_update_slice(p_main[0], _from_groups(p_dt[0].T, hpg).T, (main, 0))
        pairs = [p_in.reshape(4, SH, D // 2), p_mix]
    cut = (SH * 5 // 128) * 16
    dh_a, got = matmul(dproj, p["win_t"], "nn", F32, "dh_main",
                       jobs=[chip_exchange_rows_job(pairs[0], 0, cut), chip_exchange_job(pairs[1:])] if reduce else [])
    (dx, d_pre, *post_below), got2 = rms_bwd(
        s["x"], p["pre_w"], dh_a, dh_b, dxn, below,
        jobs=[chip_exchange_rows_job(pairs[0], cut, SH - cut, got[0][0])] if reduce else [])
    mines = [chip_add(pairs[0], got2[0][0], chip, core, True),
             chip_add(pairs[1], got[1][0], chip, core, False)] if reduce else None
    reduced = dict(w_out=r_out) if reduce else None
    grads = dict(
        pre_norm_w=d_pre[0], pool_scale=d_scale[0], conv_w=d_cw[:CONV_K], conv_b=d_cb[0],
        dt_bias=_from_groups(d_bias[0], hpg), a_log=_from_groups(d_alog[0], hpg),
        d_skip=d_dexp[0].reshape(-1, HEAD_DIM).sum(axis=-1), ssd_norm_w=d_nw[0], post_norm_w=d_post[0])
    if not reduce:
        grads.update(w_in=jnp.concatenate([d_wmain_t.T, _from_groups(d_wdt_t.T, hpg)], axis=1), pool_mix_w=d_mixw,
                     w_out=d_wout)
    return dx, grads, reduced, tuple(post_below) or None, mines, above


def local_step(x, target, params, D):
    saved, h, n = [], None, len(params)
    for l, p in enumerate(params):
        x, s, _, h = layer_fwd(x, p, D, h=h, next_pre_w=params[l + 1]["pre_w"] if l + 1 < n else None)
        saved.append(s)
    dx, sumsq = loss_head(x, target)
    grads, post = [None] * n, None
    for l in reversed(range(n)):
        below = (saved[l - 1]["out"], params[l - 1]["post_w"]) if l else None
        dx, grads[l], _, post, _, _ = layer_bwd(dx, params[l], saved[l], D, post=post, below=below)
    return sumsq, dx, grads


SMALL = ("pre_norm_w", "pool_scale", "conv_w", "conv_b", "dt_bias", "a_log", "d_skip", "ssd_norm_w", "post_norm_w")
BIG = ("w_in", "w_out", "pool_mix_w")


def _pack(parts):
    flat = jnp.concatenate([p.reshape(-1) for p in parts])
    n = flat.shape[0]
    rows = -(-n // (LANES * LANES)) * LANES
    return jnp.pad(flat, (0, rows * LANES - n)).reshape(rows, LANES)


def _unpack(packed, shapes):
    flat, out, at = packed.reshape(-1), [], 0
    for s in shapes:
        n = math.prod(s)
        out.append(flat[at:at + n].reshape(s))
        at += n
    return out


def kernel(x, pre_norm_w, w_in, pool_mix_w, pool_scale, conv_w, conv_b, dt_bias, a_log, d_skip, ssd_norm_w, w_out, post_norm_w, loss_target, m_pre_norm_w, m_w_in, m_pool_mix_w, m_pool_scale, m_conv_w, m_conv_b, m_dt_bias, m_a_log, m_d_skip, m_ssd_norm_w, m_w_out, m_post_norm_w, v_pre_norm_w, v_w_in, v_pool_mix_w, v_pool_scale, v_conv_w, v_conv_b, v_dt_bias, v_a_log, v_d_skip, v_ssd_norm_w, v_w_out, v_post_norm_w):
    NL, D, SH = w_in.shape
    PGW = D // GROUPS
    CS = conv_w.shape[2]
    chip = (2 * lax.axis_index("x") + lax.axis_index("y")).astype(jnp.int32)
    chip1, core = chip.reshape(1), lax.axis_index("c").astype(jnp.int32).reshape(1)

    tr = lambda t: jnp.transpose(t, (0, 2, 1))
    w_in_t, m_w_in_t, v_w_in_t = tr(w_in), tr(m_w_in), tr(v_w_in)
    halved_by_cols = [True, True, False, False]

    def shards(l):
        return [cast_bf16(w_in_t, l), cast_bf16(w_out, l),
                cast_bf16(pool_mix_w.reshape(NL, GROUPS * PGW // 4, PGW), l).reshape(2, GROUPS * PGW // 8, PGW),
                conv_w[l].reshape(2, CONV_K * CS // (2 * LANES), LANES)]

    def rest_params(g):
        g_out, g_mix, g_cw = g
        return dict(wout=g_out.reshape(2 * D, D),
                    mixw=g_mix.reshape(4, GROUPS, PGW // 4, PGW).transpose(1, 0, 2, 3).reshape(GROUPS, PGW, PGW),
                    cw=g_cw.reshape(4, CONV_K, CS).transpose(1, 0, 2).reshape(CONV_K, 4 * CS))

    def params(l, g):
        late = rest_params(g[1:]) if len(g) > 1 else dict(wout=None, mixw=None, cw=None)
        return layer_params(pre_norm_w[l], g[0].reshape(4 * SH, D), late["mixw"], pool_scale[l], late["cw"], conv_b[l],
                            dt_bias[l], a_log[l], d_skip[l], ssd_norm_w[l], late["wout"], post_norm_w[l], D)

    first = shards(0)
    gathered = run_jobs([gather_send_job(first[:1], [True])], "gather_send")[0]
    gathered = run_jobs([gather_pass_job(gathered, [True])], "gather_pass")[0]
    xl, h, ps, saved = x[0], None, [], []
    for l in range(NL):
        ps.append(params(l, gathered))
        last = l + 1 == NL
        xl, s, gathered, h = layer_fwd(xl, ps[l], D, None if last else shards(l + 1), h,
                                       None if last else pre_norm_w[l + 1][None],
                                       (first[1:], halved_by_cols[1:], rest_params) if l == 0 else None)
        saved.append(s)
    dx, sumsq = loss_head(xl, loss_target[0])

    given = dict(w_in=(w_in_t, m_w_in_t, v_w_in_t), w_out=(w_out, m_w_out, v_w_out),
                 pool_mix_w=(pool_mix_w, m_pool_mix_w, v_pool_mix_w))
    flat = {n: [t.reshape(NL, -1, t.shape[-1]) for t in given[n]] for n in BIG}
    done = {n: None for n in BIG}
    def update(l, reduced):
        for n, r in reduced.items():
            done[n] = adamw_layer(r.reshape(-1, r.shape[-1]), *flat[n], l, done[n], "adamw_" + n)

    grads, post, pending = [None] * NL, None, None
    for l in reversed(range(NL)):
        below = (saved[l - 1]["out"], ps[l - 1]["post_w"]) if l else None
        dx, grads[l], reduced, post, pending, above = layer_bwd(dx, ps[l], saved[l], D, (chip1, core), post, below,
                                                                pending)
        update(l, reduced)
        if above:
            update(l + 1, above)
    last = run_jobs([pair_gather_job(pending, [True, False])], "pair_gather")[0]
    update(0, dict(w_in=last[0], pool_mix_w=last[1]))

    small_shapes = [(NL,) + grads[0][n].shape for n in SMALL]
    packed = _pack([0.5 / D * sumsq[0, :1]] + [jnp.stack([g[n] for g in grads]) for n in SMALL])
    total = allreduce_small(packed)
    loss, *small = _unpack(total, [(1,)] + small_shapes)
    small = dict(zip(SMALL, small))
    small["conv_w"] = lax.dynamic_slice_in_dim(small["conv_w"], chip * CS, CS, axis=2)

    given_small = dict(
        pre_norm_w=(pre_norm_w, m_pre_norm_w, v_pre_norm_w), pool_scale=(pool_scale, m_pool_scale, v_pool_scale),
        conv_w=(conv_w, m_conv_w, v_conv_w), conv_b=(conv_b, m_conv_b, v_conv_b),
        dt_bias=(dt_bias, m_dt_bias, v_dt_bias), a_log=(a_log, m_a_log, v_a_log),
        d_skip=(d_skip, m_d_skip, v_d_skip), ssd_norm_w=(ssd_norm_w, m_ssd_norm_w, v_ssd_norm_w),
        post_norm_w=(post_norm_w, m_post_norm_w, v_post_norm_w))
    shapes = [given_small[n][0].shape for n in SMALL]
    upd = adamw(_pack([small[n] for n in SMALL]), *[_pack([given_small[n][i] for n in SMALL]) for i in range(3)],
                "adamw_small")
    upd = [dict(zip(SMALL, _unpack(u, shapes))) for u in upd]

    out = {n: (small[n], upd[0][n], upd[1][n], upd[2][n]) for n in SMALL}
    for n in BIG:
        out[n] = tuple(t.reshape(given[n][0].shape) for t in done[n])
    out["w_in"] = tuple(tr(t) for t in out["w_in"])

    order = ("pre_norm_w", "w_in", "pool_mix_w", "pool_scale", "conv_w", "conv_b", "dt_bias", "a_log", "d_skip",
             "ssd_norm_w", "w_out", "post_norm_w")
    return (loss.reshape(()), dx[None], *[out[n][0] for n in order], *[out[n][1] for n in order],
            *[out[n][2] for n in order], *[out[n][3] for n in order])
```
